```python
import jax, jax.numpy as jnp
from jax import lax
import numpy as np

D_MODEL = 1024
BATCH = 16
SEQ = 2048
DEPTH = 1

D_RNN = D_MODEL
RNN_BLOCKS = 16
RNN_BLOCK_W = D_RNN // RNN_BLOCKS
CONV_W = 4
LRU_C = 8.0
HEAD_DIM = 64
N_Q_HEADS = D_MODEL // HEAD_DIM
N_KV_HEADS = 4
GQA_GROUP = N_Q_HEADS // N_KV_HEADS
WINDOW = 128
ATTN_BLOCK = WINDOW
ROPE_THETA = 10000.0
Q_W = N_Q_HEADS * HEAD_DIM
KV_W = N_KV_HEADS * HEAD_DIM
D_FF = 4 * D_MODEL
PLE_DIM = 256
NORM_EPS = 1e-6
IN_WIDTHS = [D_RNN, D_RNN, Q_W, KV_W, KV_W, D_MODEL, D_MODEL]
IN_TOTAL = int(sum(IN_WIDTHS))
SPLIT_IDX = [int(v) for v in np.cumsum(IN_WIDTHS)[:-1]]

kernel_name = 'hybrid_rglru_swa_sink_gated_block'


def _rmsnorm(t, g):
    tf = t.astype(jnp.float32)
    y = tf * lax.rsqrt(jnp.mean(tf * tf, axis=-1, keepdims=True) + NORM_EPS)
    return (y * g.astype(jnp.float32)).astype(t.dtype)


def _rope_tables(S):
    inv = ROPE_THETA ** (-jnp.arange(0, HEAD_DIM, 2, dtype=jnp.float32) / HEAD_DIM)
    ang = jnp.arange(S, dtype=jnp.float32)[:, None] * inv[None, :]
    return jnp.cos(ang), jnp.sin(ang)


def _rope(t, cos, sin):
    tf = t.astype(jnp.float32)
    t1, t2 = jnp.split(tf, 2, axis=-1)
    c = cos[None, :, None, :]
    s = sin[None, :, None, :]
    return jnp.concatenate([t1 * c - t2 * s, t2 * c + t1 * s], axis=-1).astype(t.dtype)


def _causal_conv(t, w, b):
    S = t.shape[1]
    tp = jnp.pad(t, ((0, 0), (CONV_W - 1, 0), (0, 0)))
    out = b + tp[:, 0:S] * w[0]
    for j in range(1, CONV_W):
        out = out + tp[:, j:j + S] * w[j]
    return out


def _rg_lru(xc, w_rg, b_rg, w_ig, b_ig, lam):
    B, S, _ = xc.shape
    xb = xc.reshape(B, S, RNN_BLOCKS, RNN_BLOCK_W)
    r = jax.nn.sigmoid(jnp.einsum('bshi,hij->bshj', xb, w_rg).reshape(B, S, D_RNN) + b_rg)
    i = jax.nn.sigmoid(jnp.einsum('bshi,hij->bshj', xb, w_ig).reshape(B, S, D_RNN) + b_ig)
    log_a = -LRU_C * r.astype(jnp.float32) * jax.nn.softplus(-lam.astype(jnp.float32))
    a = jnp.exp(log_a)
    mult = jnp.sqrt(-jnp.expm1(2.0 * log_a))
    bterm = mult * (i * xc).astype(jnp.float32)

    def combine(left, right):
        a1, b1 = left
        a2, b2 = right
        return a1 * a2, a2 * b1 + b2

    _, h = lax.associative_scan(combine, (a, bterm), axis=1)
    return h.astype(xc.dtype)


def _sliding_window_attention(q, k, v, q_gain, k_gain, sinks, cos, sin):
    B, S, _ = q.shape
    NB = S // ATTN_BLOCK
    q = _rope(_rmsnorm(q.reshape(B, S, N_Q_HEADS, HEAD_DIM), q_gain), cos, sin)
    k = _rope(_rmsnorm(k.reshape(B, S, N_KV_HEADS, HEAD_DIM), k_gain), cos, sin)
    v = v.reshape(B, S, N_KV_HEADS, HEAD_DIM)
    qb = q.reshape(B, NB, ATTN_BLOCK, N_KV_HEADS, GQA_GROUP, HEAD_DIM)

    def band(t):
        tb = t.reshape(B, NB, ATTN_BLOCK, N_KV_HEADS, HEAD_DIM)
        prev = jnp.pad(tb[:, :-1], ((0, 0), (1, 0), (0, 0), (0, 0), (0, 0)))
        return jnp.concatenate([prev, tb], axis=2)

    kb = band(k)
    vb = band(v)
    s = jnp.einsum('bnqkgd,bnckd->bnkgqc', qb, kb).astype(jnp.float32) * (HEAD_DIM ** -0.5)
    qi = jnp.arange(ATTN_BLOCK)[:, None]
    ci = jnp.arange(2 * ATTN_BLOCK)[None, :]
    diff = ATTN_BLOCK + qi - ci
    blk = jnp.arange(NB)[:, None, None]
    valid = (diff >= 0) & (diff < WINDOW) & ((blk - 1) * ATTN_BLOCK + ci >= 0)
    s = jnp.where(valid[None, :, None, None, :, :], s, -jnp.inf)
    sink = sinks.astype(jnp.float32).reshape(N_KV_HEADS, GQA_GROUP)[None, None, :, :, None, None]
    m = jnp.maximum(jnp.max(s, axis=-1, keepdims=True), sink)
    e = jnp.exp(s - m)
    probs = e / (jnp.sum(e, axis=-1, keepdims=True) + jnp.exp(sink - m))
    o = jnp.einsum('bnkgqc,bnckd->bnqkgd', probs.astype(v.dtype), vb)
    return o.reshape(B, S, Q_W)


def _fwd_setup_inputs(seed: int = 0) -> dict:
    key = jax.random.key(seed)
    ks = jax.random.split(key, 24)
    f32 = jnp.float32
    L = DEPTH

    def nrm(k, shape, scale):
        return jax.random.normal(k, shape, f32) * scale

    u = jax.random.uniform(ks[10], (L, D_RNN), f32, minval=0.9, maxval=0.999)
    s_a = u ** (1.0 / LRU_C)
    lru_lambda = jnp.log(s_a) - jnp.log1p(-s_a)
    return {
        'x': nrm(ks[0], (BATCH, SEQ, D_MODEL), 1.0),
        'p': nrm(ks[1], (DEPTH, BATCH, SEQ, PLE_DIM), 1.0),
        'g_mix': 1.0 + nrm(ks[2], (L, D_MODEL), 0.02),
        'w_in': nrm(ks[3], (L, D_MODEL, IN_TOTAL), D_MODEL ** -0.5),
        'conv_w': nrm(ks[4], (L, CONV_W, D_RNN), CONV_W ** -0.5),
        'conv_b': nrm(ks[5], (L, D_RNN), 0.01),
        'w_rg': nrm(ks[6], (L, RNN_BLOCKS, RNN_BLOCK_W, RNN_BLOCK_W), RNN_BLOCK_W ** -0.5),
        'b_rg': nrm(ks[7], (L, D_RNN), 0.01),
        'w_ig': nrm(ks[8], (L, RNN_BLOCKS, RNN_BLOCK_W, RNN_BLOCK_W), RNN_BLOCK_W ** -0.5),
        'b_ig': nrm(ks[9], (L, D_RNN), 0.01),
        'lru_lambda': lru_lambda,
        'w_rnn_proj': nrm(ks[11], (L, D_RNN, D_MODEL), D_RNN ** -0.5),
        'q_gain': 1.0 + nrm(ks[12], (L, HEAD_DIM), 0.02),
        'k_gain': 1.0 + nrm(ks[13], (L, HEAD_DIM), 0.02),
        'sinks': nrm(ks[14], (L, N_Q_HEADS), 0.5),
        'w_attn_proj': nrm(ks[15], (L, Q_W, D_MODEL), Q_W ** -0.5),
        'w_out': nrm(ks[16], (L, D_MODEL, D_MODEL), D_MODEL ** -0.5),
        'g_mlp': 1.0 + nrm(ks[17], (L, D_MODEL), 0.02),
        'w_up': nrm(ks[18], (L, D_MODEL, D_FF), D_MODEL ** -0.5),
        'w_down': nrm(ks[19], (L, D_FF, D_MODEL), D_FF ** -0.5),
        'g_ple': 1.0 + nrm(ks[20], (L, D_MODEL), 0.02),
        'w_ple_gate': nrm(ks[21], (L, D_MODEL, D_MODEL), D_MODEL ** -0.5),
        'w_ple_proj': nrm(ks[22], (L, PLE_DIM, D_MODEL), PLE_DIM ** -0.5),
    }


def _fwd_reference(x, p, g_mix, w_in, conv_w, conv_b, w_rg, b_rg, w_ig, b_ig, lru_lambda,
              w_rnn_proj, q_gain, k_gain, sinks, w_attn_proj, w_out, g_mlp, w_up, w_down,
              g_ple, w_ple_gate, w_ple_proj):
    S = x.shape[1]
    cos, sin = _rope_tables(S)
    for l in range(DEPTH):
        h = _rmsnorm(x, g_mix[l])
        z = h @ w_in[l]
        x_rnn, g_rnn, q, k, v, gate_a, gate_b = jnp.split(z, SPLIT_IDX, axis=-1)
        xc = _causal_conv(x_rnn, conv_w[l], conv_b[l])
        hr = _rg_lru(xc, w_rg[l], b_rg[l], w_ig[l], b_ig[l], lru_lambda[l])
        y_a = (hr * jax.nn.gelu(g_rnn)) @ w_rnn_proj[l]
        y_b = _sliding_window_attention(q, k, v, q_gain[l], k_gain[l], sinks[l], cos, sin) @ w_attn_proj[l]
        merged = jax.nn.sigmoid(gate_a) * y_a + jax.nn.sigmoid(gate_b) * y_b
        x = x + merged @ w_out[l]
        hm = _rmsnorm(x, g_mlp[l])
        x = x + jnp.square(jax.nn.relu(hm @ w_up[l])) @ w_down[l]
        e = p[l] @ w_ple_proj[l]
        x = x + e * jax.nn.sigmoid(_rmsnorm(x, g_ple[l]) @ w_ple_gate[l])
    return x


import jax as _jax
import jax.numpy as _jnp

TWIN_FORMAT = 'train_step'
FWD_PARAMS = ['x', 'p', 'g_mix', 'w_in', 'conv_w', 'conv_b', 'w_rg', 'b_rg', 'w_ig', 'b_ig', 'lru_lambda', 'w_rnn_proj', 'q_gain', 'k_gain', 'sinks', 'w_attn_proj', 'w_out', 'g_mlp', 'w_up', 'w_down', 'g_ple', 'w_ple_gate', 'w_ple_proj']
TWIN_WEIGHTS = ['g_mix', 'w_in', 'conv_w', 'conv_b', 'w_rg', 'b_rg', 'w_ig', 'b_ig', 'lru_lambda', 'w_rnn_proj', 'q_gain', 'k_gain', 'sinks', 'w_attn_proj', 'w_out', 'g_mlp', 'w_up', 'w_down', 'g_ple', 'w_ple_gate', 'w_ple_proj']
TWIN_DIFF_INPUT = 'x'
TWIN_INPUTS = ['x', 'p', 'g_mix', 'w_in', 'conv_w', 'conv_b', 'w_rg', 'b_rg', 'w_ig', 'b_ig', 'lru_lambda', 'w_rnn_proj', 'q_gain', 'k_gain', 'sinks', 'w_attn_proj', 'w_out', 'g_mlp', 'w_up', 'w_down', 'g_ple', 'w_ple_gate', 'w_ple_proj', 'loss_target', 'm_g_mix', 'm_w_in', 'm_conv_w', 'm_conv_b', 'm_w_rg', 'm_b_rg', 'm_w_ig', 'm_b_ig', 'm_lru_lambda', 'm_w_rnn_proj', 'm_q_gain', 'm_k_gain', 'm_sinks', 'm_w_attn_proj', 'm_w_out', 'm_g_mlp', 'm_w_up', 'm_w_down', 'm_g_ple', 'm_w_ple_gate', 'm_w_ple_proj', 'v_g_mix', 'v_w_in', 'v_conv_w', 'v_conv_b', 'v_w_rg', 'v_b_rg', 'v_w_ig', 'v_b_ig', 'v_lru_lambda', 'v_w_rnn_proj', 'v_q_gain', 'v_k_gain', 'v_sinks', 'v_w_attn_proj', 'v_w_out', 'v_g_mlp', 'v_w_up', 'v_w_down', 'v_g_ple', 'v_w_ple_gate', 'v_w_ple_proj']
TWIN_OUTPUTS = ['loss', 'grad_x', 'grad_g_mix', 'grad_w_in', 'grad_conv_w', 'grad_conv_b', 'grad_w_rg', 'grad_b_rg', 'grad_w_ig', 'grad_b_ig', 'grad_lru_lambda', 'grad_w_rnn_proj', 'grad_q_gain', 'grad_k_gain', 'grad_sinks', 'grad_w_attn_proj', 'grad_w_out', 'grad_g_mlp', 'grad_w_up', 'grad_w_down', 'grad_g_ple', 'grad_w_ple_gate', 'grad_w_ple_proj', 'delta_g_mix', 'delta_w_in', 'delta_conv_w', 'delta_conv_b', 'delta_w_rg', 'delta_b_rg', 'delta_w_ig', 'delta_b_ig', 'delta_lru_lambda', 'delta_w_rnn_proj', 'delta_q_gain', 'delta_k_gain', 'delta_sinks', 'delta_w_attn_proj', 'delta_w_out', 'delta_g_mlp', 'delta_w_up', 'delta_w_down', 'delta_g_ple', 'delta_w_ple_gate', 'delta_w_ple_proj', 'new_m_g_mix', 'new_m_w_in', 'new_m_conv_w', 'new_m_conv_b', 'new_m_w_rg', 'new_m_b_rg', 'new_m_w_ig', 'new_m_b_ig', 'new_m_lru_lambda', 'new_m_w_rnn_proj', 'new_m_q_gain', 'new_m_k_gain', 'new_m_sinks', 'new_m_w_attn_proj', 'new_m_w_out', 'new_m_g_mlp', 'new_m_w_up', 'new_m_w_down', 'new_m_g_ple', 'new_m_w_ple_gate', 'new_m_w_ple_proj', 'new_v_g_mix', 'new_v_w_in', 'new_v_conv_w', 'new_v_conv_b', 'new_v_w_rg', 'new_v_b_rg', 'new_v_w_ig', 'new_v_b_ig', 'new_v_lru_lambda', 'new_v_w_rnn_proj', 'new_v_q_gain', 'new_v_k_gain', 'new_v_sinks', 'new_v_w_attn_proj', 'new_v_w_out', 'new_v_g_mlp', 'new_v_w_up', 'new_v_w_down', 'new_v_g_ple', 'new_v_w_ple_gate', 'new_v_w_ple_proj']
TWIN_LEAF_KINDS = {'loss': 'loss', 'grad_x': 'grad_x', 'grad_g_mix': 'grad_w', 'grad_w_in': 'grad_w', 'grad_conv_w': 'grad_w', 'grad_conv_b': 'grad_w', 'grad_w_rg': 'grad_w', 'grad_b_rg': 'grad_w', 'grad_w_ig': 'grad_w', 'grad_b_ig': 'grad_w', 'grad_lru_lambda': 'grad_w', 'grad_w_rnn_proj': 'grad_w', 'grad_q_gain': 'grad_w', 'grad_k_gain': 'grad_w', 'grad_sinks': 'grad_w', 'grad_w_attn_proj': 'grad_w', 'grad_w_out': 'grad_w', 'grad_g_mlp': 'grad_w', 'grad_w_up': 'grad_w', 'grad_w_down': 'grad_w', 'grad_g_ple': 'grad_w', 'grad_w_ple_gate': 'grad_w', 'grad_w_ple_proj': 'grad_w', 'delta_g_mix': 'delta_w', 'delta_w_in': 'delta_w', 'delta_conv_w': 'delta_w', 'delta_conv_b': 'delta_w', 'delta_w_rg': 'delta_w', 'delta_b_rg': 'delta_w', 'delta_w_ig': 'delta_w', 'delta_b_ig': 'delta_w', 'delta_lru_lambda': 'delta_w', 'delta_w_rnn_proj': 'delta_w', 'delta_q_gain': 'delta_w', 'delta_k_gain': 'delta_w', 'delta_sinks': 'delta_w', 'delta_w_attn_proj': 'delta_w', 'delta_w_out': 'delta_w', 'delta_g_mlp': 'delta_w', 'delta_w_up': 'delta_w', 'delta_w_down': 'delta_w', 'delta_g_ple': 'delta_w', 'delta_w_ple_gate': 'delta_w', 'delta_w_ple_proj': 'delta_w', 'new_m_g_mix': 'new_m', 'new_m_w_in': 'new_m', 'new_m_conv_w': 'new_m', 'new_m_conv_b': 'new_m', 'new_m_w_rg': 'new_m', 'new_m_b_rg': 'new_m', 'new_m_w_ig': 'new_m', 'new_m_b_ig': 'new_m', 'new_m_lru_lambda': 'new_m', 'new_m_w_rnn_proj': 'new_m', 'new_m_q_gain': 'new_m', 'new_m_k_gain': 'new_m', 'new_m_sinks': 'new_m', 'new_m_w_attn_proj': 'new_m', 'new_m_w_out': 'new_m', 'new_m_g_mlp': 'new_m', 'new_m_w_up': 'new_m', 'new_m_w_down': 'new_m', 'new_m_g_ple': 'new_m', 'new_m_w_ple_gate': 'new_m', 'new_m_w_ple_proj': 'new_m', 'new_v_g_mix': 'new_v', 'new_v_w_in': 'new_v', 'new_v_conv_w': 'new_v', 'new_v_conv_b': 'new_v', 'new_v_w_rg': 'new_v', 'new_v_b_rg': 'new_v', 'new_v_w_ig': 'new_v', 'new_v_b_ig': 'new_v', 'new_v_lru_lambda': 'new_v', 'new_v_w_rnn_proj': 'new_v', 'new_v_q_gain': 'new_v', 'new_v_k_gain': 'new_v', 'new_v_sinks': 'new_v', 'new_v_w_attn_proj': 'new_v', 'new_v_w_out': 'new_v', 'new_v_g_mlp': 'new_v', 'new_v_w_up': 'new_v', 'new_v_w_down': 'new_v', 'new_v_g_ple': 'new_v', 'new_v_w_ple_gate': 'new_v', 'new_v_w_ple_proj': 'new_v'}


def _forward(args):
    return _fwd_reference(*[args[k] for k in FWD_PARAMS])


def _output_shape():
    out = _jax.eval_shape(lambda: _forward(_fwd_setup_inputs(0)))
    return out.shape, out.dtype

N_MICROBATCH = 1
ADAM_LR = 0.001
ADAM_B1 = 0.9
ADAM_B2 = 0.999
ADAM_EPS = 1e-08
ADAM_WD = 0.01
ADAM_STEP = 10
PER_EXAMPLE_BATCH_AXIS = {'x': 0, 'p': 1, 'loss_target': 0}
SHARED_INPUTS = []
_WEIGHT_DTYPES = {'g_mix': _jnp.float32, 'w_in': _jnp.float32, 'conv_w': _jnp.float32, 'conv_b': _jnp.float32, 'w_rg': _jnp.float32, 'b_rg': _jnp.float32, 'w_ig': _jnp.float32, 'b_ig': _jnp.float32, 'lru_lambda': _jnp.float32, 'w_rnn_proj': _jnp.float32, 'q_gain': _jnp.float32, 'k_gain': _jnp.float32, 'sinks': _jnp.float32, 'w_attn_proj': _jnp.float32, 'w_out': _jnp.float32, 'g_mlp': _jnp.float32, 'w_up': _jnp.float32, 'w_down': _jnp.float32, 'g_ple': _jnp.float32, 'w_ple_gate': _jnp.float32, 'w_ple_proj': _jnp.float32}
MOMENT_SCALE = {'g_mix': 4.406309e+00, 'w_in': 2.858177e-01, 'conv_w': 2.003677e+00, 'conv_b': 1.880063e+01, 'w_rg': 7.705667e-01, 'b_rg': 4.732135e-01, 'w_ig': 1.415270e+00, 'b_ig': 8.643682e-01, 'lru_lambda': 7.585694e-01, 'w_rnn_proj': 1.634780e+00, 'q_gain': 1.653931e+00, 'k_gain': 1.605075e+00, 'sinks': 2.451534e-01, 'w_attn_proj': 1.467824e-01, 'w_out': 1.841962e+00, 'g_mlp': 9.611288e+01, 'w_up': 9.299996e-01, 'w_down': 8.160516e+00, 'g_ple': 1.424420e+00, 'w_ple_gate': 9.934459e-01, 'w_ple_proj': 4.217248e-01}


def _to_microbatches(a, axis):
    t = _jnp.moveaxis(a, axis, 0)
    t = t.reshape((N_MICROBATCH, t.shape[0] // N_MICROBATCH) + t.shape[1:])
    return _jnp.moveaxis(t, 1, axis + 1)


def setup_inputs(seed: int = 0) -> dict:
    inp = _fwd_setup_inputs(seed)
    key = _jax.random.fold_in(_jax.random.key(seed), 7919)
    shape, _ = _output_shape()
    out = dict(inp)
    out["loss_target"] = _jax.random.normal(_jax.random.fold_in(key, 0), shape, _jnp.float32)
    for i, name in enumerate(TWIN_WEIGHTS):
        w = inp[name].astype(_jnp.float32)
        if MOMENT_SCALE is None:
            s = _jnp.sqrt(_jnp.mean(_jnp.square(w)) + 1e-30)
        else:
            s = MOMENT_SCALE[name]
        km, kv = _jax.random.split(_jax.random.fold_in(key, i + 1))
        out[name] = w
        out["m_" + name] = s * _jax.random.normal(km, w.shape, _jnp.float32)
        out["v_" + name] = (s * s) * _jax.random.uniform(kv, w.shape, _jnp.float32, 0.5, 1.5)
    if N_MICROBATCH > 1:
        for name, axis in PER_EXAMPLE_BATCH_AXIS.items():
            out[name] = _to_microbatches(out[name], axis)
    return {'x': out['x'], 'p': out['p'], 'g_mix': out['g_mix'], 'w_in': out['w_in'], 'conv_w': out['conv_w'], 'conv_b': out['conv_b'], 'w_rg': out['w_rg'], 'b_rg': out['b_rg'], 'w_ig': out['w_ig'], 'b_ig': out['b_ig'], 'lru_lambda': out['lru_lambda'], 'w_rnn_proj': out['w_rnn_proj'], 'q_gain': out['q_gain'], 'k_gain': out['k_gain'], 'sinks': out['sinks'], 'w_attn_proj': out['w_attn_proj'], 'w_out': out['w_out'], 'g_mlp': out['g_mlp'], 'w_up': out['w_up'], 'w_down': out['w_down'], 'g_ple': out['g_ple'], 'w_ple_gate': out['w_ple_gate'], 'w_ple_proj': out['w_ple_proj'], 'loss_target': out['loss_target'], 'm_g_mix': out['m_g_mix'], 'm_w_in': out['m_w_in'], 'm_conv_w': out['m_conv_w'], 'm_conv_b': out['m_conv_b'], 'm_w_rg': out['m_w_rg'], 'm_b_rg': out['m_b_rg'], 'm_w_ig': out['m_w_ig'], 'm_b_ig': out['m_b_ig'], 'm_lru_lambda': out['m_lru_lambda'], 'm_w_rnn_proj': out['m_w_rnn_proj'], 'm_q_gain': out['m_q_gain'], 'm_k_gain': out['m_k_gain'], 'm_sinks': out['m_sinks'], 'm_w_attn_proj': out['m_w_attn_proj'], 'm_w_out': out['m_w_out'], 'm_g_mlp': out['m_g_mlp'], 'm_w_up': out['m_w_up'], 'm_w_down': out['m_w_down'], 'm_g_ple': out['m_g_ple'], 'm_w_ple_gate': out['m_w_ple_gate'], 'm_w_ple_proj': out['m_w_ple_proj'], 'v_g_mix': out['v_g_mix'], 'v_w_in': out['v_w_in'], 'v_conv_w': out['v_conv_w'], 'v_conv_b': out['v_conv_b'], 'v_w_rg': out['v_w_rg'], 'v_b_rg': out['v_b_rg'], 'v_w_ig': out['v_w_ig'], 'v_b_ig': out['v_b_ig'], 'v_lru_lambda': out['v_lru_lambda'], 'v_w_rnn_proj': out['v_w_rnn_proj'], 'v_q_gain': out['v_q_gain'], 'v_k_gain': out['v_k_gain'], 'v_sinks': out['v_sinks'], 'v_w_attn_proj': out['v_w_attn_proj'], 'v_w_out': out['v_w_out'], 'v_g_mlp': out['v_g_mlp'], 'v_w_up': out['v_w_up'], 'v_w_down': out['v_w_down'], 'v_g_ple': out['v_g_ple'], 'v_w_ple_gate': out['v_w_ple_gate'], 'v_w_ple_proj': out['v_w_ple_proj']}


def _loss(weights, diff, rest, loss_target):
    with _jax.named_scope("forward"):
        args = {**rest, TWIN_DIFF_INPUT: diff, **{k: w.astype(_WEIGHT_DTYPES[k]) for k, w in weights.items()}}
        y = _forward(args)
    with _jax.named_scope("loss_head"):
        err = _jnp.square(y.astype(_jnp.float32) - loss_target)
        return 0.5 * _jnp.sum(_jnp.mean(err, axis=-1)) if err.ndim else 0.5 * err


def _adamw(w, g, m, v):
    m = ADAM_B1 * m + (1.0 - ADAM_B1) * g
    v = ADAM_B2 * v + (1.0 - ADAM_B2) * _jnp.square(g)
    m_hat = m / (1.0 - ADAM_B1 ** ADAM_STEP)
    v_hat = v / (1.0 - ADAM_B2 ** ADAM_STEP)
    delta = -ADAM_LR * (m_hat / (_jnp.sqrt(v_hat) + ADAM_EPS) + ADAM_WD * w)
    return delta, m, v


def reference(x, p, g_mix, w_in, conv_w, conv_b, w_rg, b_rg, w_ig, b_ig, lru_lambda, w_rnn_proj, q_gain, k_gain, sinks, w_attn_proj, w_out, g_mlp, w_up, w_down, g_ple, w_ple_gate, w_ple_proj, loss_target, m_g_mix, m_w_in, m_conv_w, m_conv_b, m_w_rg, m_b_rg, m_w_ig, m_b_ig, m_lru_lambda, m_w_rnn_proj, m_q_gain, m_k_gain, m_sinks, m_w_attn_proj, m_w_out, m_g_mlp, m_w_up, m_w_down, m_g_ple, m_w_ple_gate, m_w_ple_proj, v_g_mix, v_w_in, v_conv_w, v_conv_b, v_w_rg, v_b_rg, v_w_ig, v_b_ig, v_lru_lambda, v_w_rnn_proj, v_q_gain, v_k_gain, v_sinks, v_w_attn_proj, v_w_out, v_g_mlp, v_w_up, v_w_down, v_g_ple, v_w_ple_gate, v_w_ple_proj):
    given = dict(x=x, p=p, g_mix=g_mix, w_in=w_in, conv_w=conv_w, conv_b=conv_b, w_rg=w_rg, b_rg=b_rg, w_ig=w_ig, b_ig=b_ig, lru_lambda=lru_lambda, w_rnn_proj=w_rnn_proj, q_gain=q_gain, k_gain=k_gain, sinks=sinks, w_attn_proj=w_attn_proj, w_out=w_out, g_mlp=g_mlp, w_up=w_up, w_down=w_down, g_ple=g_ple, w_ple_gate=w_ple_gate, w_ple_proj=w_ple_proj, loss_target=loss_target, m_g_mix=m_g_mix, m_w_in=m_w_in, m_conv_w=m_conv_w, m_conv_b=m_conv_b, m_w_rg=m_w_rg, m_b_rg=m_b_rg, m_w_ig=m_w_ig, m_b_ig=m_b_ig, m_lru_lambda=m_lru_lambda, m_w_rnn_proj=m_w_rnn_proj, m_q_gain=m_q_gain, m_k_gain=m_k_gain, m_sinks=m_sinks, m_w_attn_proj=m_w_attn_proj, m_w_out=m_w_out, m_g_mlp=m_g_mlp, m_w_up=m_w_up, m_w_down=m_w_down, m_g_ple=m_g_ple, m_w_ple_gate=m_w_ple_gate, m_w_ple_proj=m_w_ple_proj, v_g_mix=v_g_mix, v_w_in=v_w_in, v_conv_w=v_conv_w, v_conv_b=v_conv_b, v_w_rg=v_w_rg, v_b_rg=v_b_rg, v_w_ig=v_w_ig, v_b_ig=v_b_ig, v_lru_lambda=v_lru_lambda, v_w_rnn_proj=v_w_rnn_proj, v_q_gain=v_q_gain, v_k_gain=v_k_gain, v_sinks=v_sinks, v_w_attn_proj=v_w_attn_proj, v_w_out=v_w_out, v_g_mlp=v_g_mlp, v_w_up=v_w_up, v_w_down=v_w_down, v_g_ple=v_g_ple, v_w_ple_gate=v_w_ple_gate, v_w_ple_proj=v_w_ple_proj)
    weights = {n: given[n] for n in TWIN_WEIGHTS}
    shared = {n: given[n] for n in SHARED_INPUTS}
    per_example = {n: given[n] for n in ['x', 'p']}
    grad_fn = _jax.value_and_grad(_loss, argnums=(0, 1))

    def one_microbatch(ex, loss_target):
        ex = dict(ex)
        diff = ex.pop(TWIN_DIFF_INPUT)
        return grad_fn(weights, diff, {**shared, **ex}, loss_target)

    if N_MICROBATCH == 1:
        loss, (grad_w, grad_x) = one_microbatch(per_example, given["loss_target"])
    else:
        def body(carry, xs):
            loss_sum, grad_sum = carry
            l_k, (gw_k, gx_k) = one_microbatch(xs[0], xs[1])
            with _jax.named_scope("update"):
                return (loss_sum + l_k, _jax.tree.map(_jnp.add, grad_sum, gw_k)), gx_k

        init = (_jnp.zeros((), _jnp.float32), _jax.tree.map(_jnp.zeros_like, weights))
        (loss, grad_w), grad_x = _jax.lax.scan(body, init, (per_example, given["loss_target"]))
    with _jax.named_scope("update"):
        delta_w, new_m, new_v = {}, {}, {}
        for n in TWIN_WEIGHTS:
            delta_w[n], new_m[n], new_v[n] = _adamw(weights[n], grad_w[n], given["m_" + n], given["v_" + n])
    return (loss, grad_x, *[grad_w[n] for n in TWIN_WEIGHTS], *[delta_w[n] for n in TWIN_WEIGHTS],
            *[new_m[n] for n in TWIN_WEIGHTS], *[new_v[n] for n in TWIN_WEIGHTS])
```

```python
import functools
import math

import numpy as np
import jax
import jax.numpy as jnp
from jax import lax
from jax.experimental import pallas as pl
from jax.experimental.pallas import tpu as pltpu

F32 = jnp.float32
BF16 = jnp.bfloat16

D_MODEL = 1024
N_HEADS = 16
N_KV = 4
HEAD_DIM = 64
KV_W = N_KV * HEAD_DIM
D_FF = 4096
PLE_DIM = 256
WINDOW = 128
CONV_W = 4
LRU_C = 8.0
NORM_EPS = 1e-6
ROPE_THETA = 10000.0
N_CHIPS = 4
IN_TOTAL = 5632
IN_BLK = IN_TOTAL // N_CHIPS
IN_SEGS = (0, 1024, 2048, 3072, 3328, 3584, 4608, 5632)

ADAM_LR = 0.001
ADAM_B1 = 0.9
ADAM_B2 = 0.999
ADAM_EPS = 1e-08
ADAM_WD = 0.01
ADAM_STEP = 10

LANES = 128
VMEM_LIMIT = 56 * 1024 * 1024
MESH_ID = pl.DeviceIdType.MESH


def _dot(a, b):
    return jnp.dot(a, b, preferred_element_type=F32)


def _dot_nt(a, b):
    return lax.dot_general(a, b, (((1,), (1,)), ((), ())), preferred_element_type=F32)


def _dot_tn(a, b):
    return lax.dot_general(a, b, (((0,), (0,)), ((), ())), preferred_element_type=F32)


def _split_dot(x, ind):
    hi = x.astype(BF16)
    lo = (x - hi.astype(F32)).astype(BF16)
    return _dot(hi, ind) + _dot(lo, ind)


def _sigmoid(x):
    return 1.0 / (1.0 + jnp.exp(-x))


_GELU_C = math.sqrt(2.0 / math.pi)


def _gelu_and_grad(g):
    inner = _GELU_C * (g + 0.044715 * g * g * g)
    t = jnp.tanh(inner)
    gelu = 0.5 * g * (1.0 + t)
    dgelu = 0.5 * (1.0 + t) + 0.5 * g * (1.0 - t * t) * _GELU_C * (1.0 + 3.0 * 0.044715 * g * g)
    return gelu, dgelu


def _const(shape):
    nd = len(shape)
    return pl.BlockSpec(shape, lambda *_: (0,) * nd)


def _params(n_grid, vmem=VMEM_LIMIT):
    return pltpu.CompilerParams(dimension_semantics=("arbitrary",) * n_grid, vmem_limit_bytes=vmem)


def _rms_fwd(x, g):
    r = lax.rsqrt(jnp.mean(x * x, axis=-1, keepdims=True) + NORM_EPS)
    return (x * r) * g, r


def _rms_bwd(dy, x, r, g):
    dn = dy * g
    dx = r * dn - x * (r * r * r * jnp.mean(dn * x, axis=-1, keepdims=True))
    dg = jnp.sum(dy * (x * r), axis=0, keepdims=True)
    return dx, dg


def _seg_pieces(blk_lo, blk_hi):
    out = []
    for s in range(7):
        lo, hi = max(blk_lo, IN_SEGS[s]), min(blk_hi, IN_SEGS[s + 1])
        if lo < hi:
            out.append((s, lo - IN_SEGS[s], hi - IN_SEGS[s], lo - blk_lo))
    return out


def _inproj_fwd(x, g_mix, w_in, tm):
    T = x.shape[0]
    widths = [IN_SEGS[i + 1] - IN_SEGS[i] for i in range(7)]

    def body(x_ref, g_ref, w_ref, h_ref, *z_refs):
        h, _ = _rms_fwd(x_ref[...], g_ref[...])
        hb = h.astype(BF16)
        h_ref[...] = hb
        for j in range(N_CHIPS):
            zj = _dot(hb, w_ref[j])
            for s, lo, hi, off in _seg_pieces(j * IN_BLK, (j + 1) * IN_BLK):
                z_refs[s][:, lo:hi] = zj[:, off:off + hi - lo]

    return pl.pallas_call(
        body, name="inproj_fwd", grid=(T // tm,),
        in_specs=[pl.BlockSpec((tm, D_MODEL), lambda i: (i, 0)), _const((1, D_MODEL)),
                  _const((N_CHIPS, D_MODEL, IN_BLK))],
        out_specs=[pl.BlockSpec((tm, D_MODEL), lambda i: (i, 0))]
        + [pl.BlockSpec((tm, w), lambda i: (i, 0)) for w in widths],
        out_shape=[jax.ShapeDtypeStruct((T, D_MODEL), BF16)]
        + [jax.ShapeDtypeStruct((T, w), F32) for w in widths],
        compiler_params=_params(1),
    )(x, g_mix, w_in)


def _inproj_bwd(dz_parts, w_in, x, g_mix, dx1, tm):
    T = x.shape[0]
    widths = [IN_SEGS[i + 1] - IN_SEGS[i] for i in range(7)]

    def body(*refs):
        p_refs = refs[:7]
        w_ref, x_ref, g_ref, dx1_ref, gx_ref, dz_ref, dg_ref = refs[7:]

        @pl.when(pl.program_id(0) == 0)
        def _():
            dg_ref[...] = jnp.zeros_like(dg_ref)

        for s in range(7):
            dz_ref[:, IN_SEGS[s]:IN_SEGS[s + 1]] = p_refs[s][...]
        dh = jnp.zeros((tm, D_MODEL), F32)
        for j in range(N_CHIPS):
            dh = dh + _dot_nt(dz_ref[:, j * IN_BLK:(j + 1) * IN_BLK], w_ref[j])
        xv = x_ref[...]
        g = g_ref[...]
        _, r = _rms_fwd(xv, g)
        dx, dg = _rms_bwd(dh, xv, r, g)
        gx_ref[...] = dx1_ref[...] + dx
        dg_ref[...] += dg

    row = lambda w: pl.BlockSpec((tm, w), lambda i: (i, 0))
    return pl.pallas_call(
        body, name="inproj_bwd", grid=(T // tm,),
        in_specs=[row(w) for w in widths]
        + [_const((N_CHIPS, D_MODEL, IN_BLK)), row(D_MODEL), _const((1, D_MODEL)), row(D_MODEL)],
        out_specs=[row(D_MODEL), row(IN_TOTAL), _const((1, D_MODEL))],
        out_shape=[jax.ShapeDtypeStruct((T, D_MODEL), F32), jax.ShapeDtypeStruct((T, IN_TOTAL), BF16),
                   jax.ShapeDtypeStruct((1, D_MODEL), F32)],
        compiler_params=_params(1),
    )(*dz_parts, w_in, x, g_mix, dx1)


def _wgrad(a, g, name, blocked, cn, tm):
    T, K = a.shape
    N = g.shape[1]
    nb = N // cn

    def body(a_ref, g_ref, o_ref):
        @pl.when(pl.program_id(1) == 0)
        def _():
            o_ref[...] = jnp.zeros_like(o_ref)

        o_ref[...] += _dot_tn(a_ref[...].astype(BF16), g_ref[...].astype(BF16))

    if blocked:
        out_spec = pl.BlockSpec((None, K, cn), lambda j, t: (j, 0, 0))
        out_shape = jax.ShapeDtypeStruct((nb, K, cn), F32)
    else:
        out_spec = pl.BlockSpec((K, cn), lambda j, t: (0, j))
        out_shape = jax.ShapeDtypeStruct((K, N), F32)
    return pl.pallas_call(
        body, name=name, grid=(nb, T // tm),
        in_specs=[pl.BlockSpec((tm, K), lambda j, t: (t, 0)), pl.BlockSpec((tm, cn), lambda j, t: (t, j))],
        out_specs=out_spec, out_shape=out_shape, compiler_params=_params(2),
    )(a, g)


def _shift_down(x, prev8, sft, row, row8, tm):
    xs = pltpu.roll(x, sft, 0)
    top = jnp.where(row8 < sft, pltpu.roll(prev8, sft, 0), xs[0:8])
    return jnp.concatenate([top, xs[8:]], axis=0)


def _shift_up(x, next8, sft, row8, tm):
    xs = pltpu.roll(x, tm - sft, 0)
    bot = jnp.where(row8 >= 8 - sft, pltpu.roll(next8, 8 - sft, 0), xs[tm - 8:tm])
    return jnp.concatenate([xs[0:tm - 8], bot], axis=0)


def _conv_fwd(x, prev8, cw_ref, cb, row, row8, tm):
    xc = cb + cw_ref[CONV_W - 1:CONV_W, :] * x
    for sft in range(1, CONV_W):
        j = CONV_W - 1 - sft
        xc = xc + cw_ref[j:j + 1, :] * _shift_down(x, prev8, sft, row, row8, tm)
    return xc


def _blockdiag_dot(xb, w_ref, transpose):
    outs = []
    for b in range(D_MODEL // LANES):
        xs = xb[:, b * LANES:(b + 1) * LANES]
        outs.append(_dot_nt(xs, w_ref[b]) if transpose else _dot(xs, w_ref[b]))
    return jnp.concatenate(outs, axis=1)


def _softplus_neg(lam):
    e = jnp.exp(-jnp.abs(lam))
    u = 1.0 + e
    log1p_e = jnp.where(u == 1.0, e, jnp.log(u) * (e / (u - 1.0)))
    sp = jnp.maximum(-lam, 0.0) + log1p_e
    return sp, -_sigmoid(-lam)


def _lru_gates(xc, wrg_ref, brg, wig_ref, big, sp):
    xcb = xc.astype(BF16)
    r = _sigmoid(_blockdiag_dot(xcb, wrg_ref, False) + brg)
    i = _sigmoid(_blockdiag_dot(xcb, wig_ref, False) + big)
    log_a = (-LRU_C) * r * sp
    a = jnp.exp(log_a)
    t = jnp.tanh(log_a)
    one_m_a2 = (-2.0) * t / (1.0 - t)
    mult = jnp.sqrt(one_m_a2)
    return xcb, r, i, a, mult


def _scan_down(a, b, row, tm):
    d = 1
    while d < tm:
        keep = row >= d
        a_s = jnp.where(keep, pltpu.roll(a, d, 0), 1.0)
        b_s = jnp.where(keep, pltpu.roll(b, d, 0), 0.0)
        b = a * b_s + b
        a = a * a_s
        d *= 2
    return a, b


def _scan_up(c, b, row, tm):
    d = 1
    while d < tm:
        keep = row < tm - d
        c_s = jnp.where(keep, pltpu.roll(c, tm - d, 0), 1.0)
        b_s = jnp.where(keep, pltpu.roll(b, tm - d, 0), 0.0)
        b = c * b_s + b
        c = c * c_s
        d *= 2
    return c, b


def _rnn_fwd(xr, gr, conv_w, conv_b, wrg2, b_rg, wig2, b_ig, lam, n_seq, S, tm):
    T = xr.shape[0]
    nt = S // tm
    W = D_MODEL

    def body(xr_ref, gr_ref, cw_ref, cb_ref, wrg_ref, brg_ref, wig_ref, big_ref, lam_ref,
             xc_ref, h_ref, ya_ref, px_ref, ph_ref):
        @pl.when(pl.program_id(1) == 0)
        def _():
            px_ref[...] = jnp.zeros_like(px_ref)
            ph_ref[...] = jnp.zeros_like(ph_ref)

        row = lax.broadcasted_iota(jnp.int32, (tm, W), 0)
        row8 = lax.broadcasted_iota(jnp.int32, (8, W), 0)
        x = xr_ref[...]
        xc = _conv_fwd(x, px_ref[...], cw_ref, cb_ref[...], row, row8, tm)
        sp, _ = _softplus_neg(lam_ref[...])
        _, r, i, a, mult = _lru_gates(xc, wrg_ref, brg_ref[...], wig_ref, big_ref[...], sp)
        bterm = mult * (i * xc)
        acum, hloc = _scan_down(a, bterm, row, tm)
        h = hloc + acum * ph_ref[7:8, :]
        h_ref[...] = h
        xc_ref[...] = xc
        gelu, _ = _gelu_and_grad(gr_ref[...])
        ya_ref[...] = (h * gelu).astype(BF16)
        px_ref[...] = xr_ref[tm - 8:tm, :]
        ph_ref[...] = h_ref[tm - 8:tm, :]

    tile = pl.BlockSpec((tm, W), lambda s, t: (s * nt + t, 0))
    return pl.pallas_call(
        body, name="rnn_fwd", grid=(n_seq, nt),
        in_specs=[tile, tile, _const((CONV_W, W)), _const((1, W)), _const((8, LANES, LANES)), _const((1, W)),
                  _const((8, LANES, LANES)), _const((1, W)), _const((1, W))],
        out_specs=[tile, tile, tile],
        out_shape=[jax.ShapeDtypeStruct((T, W), F32), jax.ShapeDtypeStruct((T, W), F32),
                   jax.ShapeDtypeStruct((T, W), BF16)],
        scratch_shapes=[pltpu.VMEM((8, W), F32), pltpu.VMEM((8, W), F32)],
        compiler_params=_params(2),
    )(xr, gr, conv_w, conv_b, wrg2, b_rg, wig2, b_ig, lam)


def _rnn_bwd(dya, xr, gr, xc, h, conv_w, wrg2, b_rg, wig2, b_ig, lam, n_seq, S, tm):
    T = xr.shape[0]
    nt = S // tm
    W = D_MODEL
    nb8 = tm // 8

    def body(dya_ref, xr_ref, gr_ref, xc_ref, h_ref, xprev_ref, hprev_ref, cw_ref, wrg_ref, brg_ref, wig_ref,
             big_ref, lam_ref, dxr_ref, dgr_ref, vec_ref, dwrg_ref, dwig_ref, cg_ref, ndxc_ref, tmp_ref):
        s, ti = pl.program_id(0), pl.program_id(1)

        @pl.when((s == 0) & (ti == 0))
        def _():
            vec_ref[...] = jnp.zeros_like(vec_ref)
            dwrg_ref[...] = jnp.zeros_like(dwrg_ref)
            dwig_ref[...] = jnp.zeros_like(dwig_ref)

        @pl.when(ti == 0)
        def _():
            cg_ref[...] = jnp.zeros_like(cg_ref)
            ndxc_ref[...] = jnp.zeros_like(ndxc_ref)

        first = ti == nt - 1
        row = lax.broadcasted_iota(jnp.int32, (tm, W), 0)
        row8 = lax.broadcasted_iota(jnp.int32, (8, W), 0)
        x = xr_ref[...]
        xc = xc_ref[...]
        hv = h_ref[...]
        xprev = jnp.where(first, 0.0, xprev_ref[...])
        hprev = jnp.where(first, 0.0, hprev_ref[...])
        sp, dsp_dlam = _softplus_neg(lam_ref[...])
        xcb, r, i, a, mult = _lru_gates(xc, wrg_ref, brg_ref[...], wig_ref, big_ref[...], sp)

        gelu, dgelu = _gelu_and_grad(gr_ref[...])
        dya_v = dya_ref[...]
        dgr_ref[...] = (dya_v * hv * dgelu).astype(BF16)
        dh = dya_v * gelu
        c = jnp.where(row < tm - 1, pltpu.roll(a, tm - 1, 0), 1.0)
        ccum, gloc = _scan_up(c, dh, row, tm)
        G = gloc + ccum * cg_ref[0:1, :]
        tmp_ref[...] = a * G
        cg_ref[...] = tmp_ref[0:8, :]

        h_m1 = _shift_down(hv, hprev, 1, row, row8, tm)
        ixc = i * xc
        dixc = G * mult
        dlog_a = (G * h_m1) * a - (G * ixc) * (a * a / mult)
        dr = dlog_a * ((-LRU_C) * sp)
        di = dixc * xc
        drg = dr * r * (1.0 - r)
        dig = di * i * (1.0 - i)
        vec_ref[7:8, :] += jnp.sum(dlog_a * ((-LRU_C) * r), axis=0, keepdims=True) * dsp_dlam
        vec_ref[5:6, :] += jnp.sum(drg, axis=0, keepdims=True)
        vec_ref[6:7, :] += jnp.sum(dig, axis=0, keepdims=True)
        drgb = drg.astype(BF16)
        digb = dig.astype(BF16)
        dxc = dixc * i + _blockdiag_dot(drgb, wrg_ref, True) + _blockdiag_dot(digb, wig_ref, True)
        for b in range(W // LANES):
            sl = slice(b * LANES, (b + 1) * LANES)
            dwrg_ref[b] += _dot_tn(xcb[:, sl], drgb[:, sl])
            dwig_ref[b] += _dot_tn(xcb[:, sl], digb[:, sl])

        vec_ref[4:5, :] += jnp.sum(dxc, axis=0, keepdims=True)
        vec_ref[3:4, :] += jnp.sum(dxc * x, axis=0, keepdims=True)
        dxr = cw_ref[CONV_W - 1:CONV_W, :] * dxc
        nxt = ndxc_ref[...]
        for sft in range(1, CONV_W):
            j = CONV_W - 1 - sft
            vec_ref[j:j + 1, :] += jnp.sum(dxc * _shift_down(x, xprev, sft, row, row8, tm), axis=0, keepdims=True)
            dxr = dxr + cw_ref[j:j + 1, :] * _shift_up(dxc, nxt, sft, row8, tm)
        dxr_ref[...] = dxr.astype(BF16)
        tmp_ref[...] = dxc
        ndxc_ref[...] = tmp_ref[0:8, :]

    rev = lambda s, t: (s * nt + nt - 1 - t, 0)
    tile = pl.BlockSpec((tm, W), rev)
    prev8 = pl.BlockSpec((8, W), lambda s, t: (jnp.maximum((s * nt + nt - 1 - t) * nb8 - 1, 0), 0))
    return pl.pallas_call(
        body, name="rnn_bwd", grid=(n_seq, nt),
        in_specs=[tile, tile, tile, tile, tile, prev8, prev8, _const((CONV_W, W)), _const((8, LANES, LANES)),
                  _const((1, W)), _const((8, LANES, LANES)), _const((1, W)), _const((1, W))],
        out_specs=[tile, tile, _const((16, W)), _const((8, LANES, LANES)), _const((8, LANES, LANES))],
        out_shape=[jax.ShapeDtypeStruct((T, W), BF16), jax.ShapeDtypeStruct((T, W), BF16),
                   jax.ShapeDtypeStruct((16, W), F32), jax.ShapeDtypeStruct((8, LANES, LANES), F32),
                   jax.ShapeDtypeStruct((8, LANES, LANES), F32)],
        scratch_shapes=[pltpu.VMEM((8, W), F32), pltpu.VMEM((8, W), F32), pltpu.VMEM((tm, W), F32)],
        compiler_params=_params(2),
    )(dya, xr, gr, xc, h, xr, h, conv_w, wrg2, b_rg, wig2, b_ig, lam)


def _head_swap(t, lane):
    w = t.shape[1]
    return jnp.where(lane % HEAD_DIM < HEAD_DIM // 2, pltpu.roll(t, w - HEAD_DIM // 2, 1),
                     pltpu.roll(t, HEAD_DIM // 2, 1))


def _qk_prep(t, gain, cosf, sins, ind, indt, lane):
    ms = _split_dot(t * t, ind) * (1.0 / HEAD_DIM)
    rstd = _split_dot(lax.rsqrt(ms + NORM_EPS), indt)
    tn = (t * rstd) * gain
    return tn * cosf + _head_swap(tn, lane) * sins, rstd


def _qk_prep_bwd(dy, t, rstd, gain, cosf, sins, ind, indt, lane):
    dtn = dy * cosf + _head_swap(dy * sins, lane)
    dgain = jnp.sum(dtn * (t * rstd), axis=0, keepdims=True)
    dn = dtn * gain
    m = _split_dot(_split_dot(dn * t, ind), indt) * (1.0 / HEAD_DIM)
    return rstd * dn - t * (rstd * rstd * rstd * m), dgain


def _attn_mask(blk_idx):
    qi = lax.broadcasted_iota(jnp.int32, (WINDOW, 2 * WINDOW), 0)
    ci = lax.broadcasted_iota(jnp.int32, (WINDOW, 2 * WINDOW), 1)
    diff = WINDOW + qi - ci
    return (diff >= 0) & (diff < WINDOW) & ((ci >= WINDOW) | (blk_idx > 0))


def _pair_operands(kc, vc, lane128):
    out = []
    for m in range(KV_W // LANES):
        k2 = kc[:, m * LANES:(m + 1) * LANES]
        v2 = vc[:, m * LANES:(m + 1) * LANES]
        out.append((k2.astype(BF16), pltpu.roll(k2, HEAD_DIM, 1).astype(BF16),
                    v2.astype(BF16), pltpu.roll(v2, HEAD_DIM, 1).astype(BF16)))
    return out


def _softmax_sink(s, mask, sink):
    s = jnp.where(mask, s, -1e30)
    mx = jnp.maximum(jnp.max(s, axis=-1, keepdims=True), sink)
    e = jnp.where(mask, jnp.exp(s - mx), 0.0)
    es = jnp.exp(sink - mx)
    inv = 1.0 / (jnp.sum(e, axis=-1, keepdims=True) + es)
    return e * inv, es * inv


def _attn_fwd(q, k, v, qg, kg, sinks, cosf, sins, ind_q, ind_qt, ind_k, ind_kt, n_seq, S):
    T = q.shape[0]
    nblk = S // WINDOW
    W = D_MODEL

    def body(sink_ref, q_ref, k_ref, v_ref, qg_ref, kg_ref, cos_ref, sin_ref, iq_ref, iqt_ref, ik_ref, ikt_ref,
             o_ref, kc_ref, vc_ref):
        n = pl.program_id(1)

        @pl.when(n == 0)
        def _():
            kc_ref[...] = jnp.zeros_like(kc_ref)
            vc_ref[...] = jnp.zeros_like(vc_ref)

        lane = lax.broadcasted_iota(jnp.int32, (WINDOW, W), 1)
        lane_k = lane[:, :KV_W]
        lane128 = lane[:, :LANES]
        cosf, sinv = cos_ref[...], sin_ref[...]
        qr, _ = _qk_prep(q_ref[...], qg_ref[...], cosf, sinv, iq_ref[...], iqt_ref[...], lane)
        kr, _ = _qk_prep(k_ref[...], kg_ref[...], cosf[:, :KV_W], sinv[:, :KV_W], ik_ref[...], ikt_ref[...], lane_k)
        kc_ref[WINDOW:2 * WINDOW, :] = kr
        vc_ref[WINDOW:2 * WINDOW, :] = v_ref[...]
        ops = _pair_operands(kc_ref[...], vc_ref[...], lane128)
        mask = _attn_mask(n)
        lo = lane128 < HEAD_DIM
        scale = HEAD_DIM ** -0.5
        for i in range(N_HEADS // 2):
            kvh = i // 2
            k2, k2r, v2, v2r = ops[kvh // 2]
            if kvh % 2:
                k2, k2r, v2, v2r = k2r, k2, v2r, v2
            qp = qr[:, i * LANES:(i + 1) * LANES]
            q_lo = jnp.where(lo, qp, 0.0).astype(BF16)
            q_hi = jnp.where(lo, 0.0, qp).astype(BF16)
            p_lo, _ = _softmax_sink(_dot_nt(q_lo, k2) * scale, mask, sink_ref[2 * i])
            p_hi, _ = _softmax_sink(_dot_nt(q_hi, k2r) * scale, mask, sink_ref[2 * i + 1])
            o_pair = jnp.where(lo, _dot(p_lo.astype(BF16), v2), _dot(p_hi.astype(BF16), v2r))
            o_ref[:, i * LANES:(i + 1) * LANES] = o_pair.astype(BF16)
        kc_ref[0:WINDOW, :] = kc_ref[WINDOW:2 * WINDOW, :]
        vc_ref[0:WINDOW, :] = vc_ref[WINDOW:2 * WINDOW, :]

    blk = lambda w: pl.BlockSpec((WINDOW, w), lambda s, n: (s * nblk + n, 0))
    pos = pl.BlockSpec((WINDOW, W), lambda s, n: (n, 0))
    return pl.pallas_call(
        body, name="attn_fwd", grid=(n_seq, nblk),
        in_specs=[pl.BlockSpec(memory_space=pltpu.SMEM), blk(W), blk(KV_W), blk(KV_W), _const((1, W)),
                  _const((1, KV_W)), pos, pos, _const((W, LANES)), _const((LANES, W)), _const((KV_W, LANES)),
                  _const((LANES, KV_W))],
        out_specs=blk(W), out_shape=jax.ShapeDtypeStruct((T, W), BF16),
        scratch_shapes=[pltpu.VMEM((2 * WINDOW, KV_W), F32), pltpu.VMEM((2 * WINDOW, KV_W), F32)],
        compiler_params=_params(2),
    )(sinks, q, k, v, qg, kg, cosf, sins, ind_q, ind_qt, ind_k, ind_kt)


def _attn_bwd(do, q, k, v, qg, kg, sinks, cosf, sins, ind_q, ind_qt, ind_k, ind_kt, n_seq, S):
    T = q.shape[0]
    nblk = S // WINDOW
    W = D_MODEL

    def body(sink_ref, do_ref, q_ref, k_ref, v_ref, qg_ref, kg_ref, cos_ref, sin_ref, iq_ref, iqt_ref, ik_ref,
             ikt_ref, dq_ref, dkc_ref, dkp_ref, dvc_ref, dvp_ref, dqg_ref, dsk_ref, kc_ref, vc_ref, dqr_ref,
             dk_ref, dv_ref):
        s_id, n = pl.program_id(0), pl.program_id(1)

        @pl.when((s_id == 0) & (n == 0))
        def _():
            dqg_ref[...] = jnp.zeros_like(dqg_ref)
            dsk_ref[...] = jnp.zeros_like(dsk_ref)

        @pl.when(n == 0)
        def _():
            kc_ref[...] = jnp.zeros_like(kc_ref)
            vc_ref[...] = jnp.zeros_like(vc_ref)

        lane = lax.broadcasted_iota(jnp.int32, (WINDOW, W), 1)
        lane_k = lane[:, :KV_W]
        lane128 = lane[:, :LANES]
        cosf, sinv = cos_ref[...], sin_ref[...]
        qv = q_ref[...]
        qr, q_rstd = _qk_prep(qv, qg_ref[...], cosf, sinv, iq_ref[...], iqt_ref[...], lane)
        kr, _ = _qk_prep(k_ref[...], kg_ref[...], cosf[:, :KV_W], sinv[:, :KV_W], ik_ref[...], ikt_ref[...], lane_k)
        kc_ref[WINDOW:2 * WINDOW, :] = kr
        vc_ref[WINDOW:2 * WINDOW, :] = v_ref[...]
        ops = _pair_operands(kc_ref[...], vc_ref[...], lane128)
        mask = _attn_mask(n)
        lo = lane128 < HEAD_DIM
        lo2 = lax.broadcasted_iota(jnp.int32, (2 * WINDOW, LANES), 1) < HEAD_DIM
        scale = HEAD_DIM ** -0.5
        dk_ref[...] = jnp.zeros_like(dk_ref)
        dv_ref[...] = jnp.zeros_like(dv_ref)
        dsk = jnp.zeros((WINDOW, LANES), F32)
        for i in range(N_HEADS // 2):
            kvh = i // 2
            m = kvh // 2
            k2, k2r, v2, v2r = ops[m]
            if kvh % 2:
                k2, k2r, v2, v2r = k2r, k2, v2r, v2
            qp = qr[:, i * LANES:(i + 1) * LANES]
            dop = do_ref[:, i * LANES:(i + 1) * LANES]
            dq_pair = None
            for half in range(2):
                sel = lo if half == 0 else ~lo
                kk, vv = (k2, v2) if half == 0 else (k2r, v2r)
                qh = jnp.where(sel, qp, 0.0).astype(BF16)
                doh = jnp.where(sel, dop, 0.0).astype(BF16)
                p, ps = _softmax_sink(_dot_nt(qh, kk) * scale, mask, sink_ref[2 * i + half])
                dp = _dot_nt(doh, vv)
                dd = jnp.sum(p * dp, axis=-1, keepdims=True)
                ds = (p * (dp - dd) * scale).astype(BF16)
                dsk = dsk + jnp.where(lane128 == 2 * i + half, -(ps * dd), 0.0)
                dq_h = _dot(ds, kk)
                dq_pair = dq_h if half == 0 else jnp.where(lo, dq_pair, dq_h)
                dk_h = _dot_tn(ds, qh)
                dv_h = _dot_tn(p.astype(BF16), doh)
                own_lo = (kvh % 2 == 0)
                if (half == 0) != own_lo:
                    dk_h = pltpu.roll(dk_h, HEAD_DIM, 1)
                    dv_h = pltpu.roll(dv_h, HEAD_DIM, 1)
                dk_ref[:, m * LANES:(m + 1) * LANES] += dk_h
                dv_ref[:, m * LANES:(m + 1) * LANES] += dv_h
            dqr_ref[:, i * LANES:(i + 1) * LANES] = dq_pair
        dsk_ref[...] += dsk
        dq, dqg = _qk_prep_bwd(dqr_ref[...], qv, q_rstd, qg_ref[...], cosf, sinv, iq_ref[...], iqt_ref[...], lane)
        dq_ref[...] = dq.astype(BF16)
        dqg_ref[...] += dqg
        dkp_ref[...] = dk_ref[0:WINDOW, :]
        dkc_ref[...] = dk_ref[WINDOW:2 * WINDOW, :]
        dvp_ref[...] = dv_ref[0:WINDOW, :]
        dvc_ref[...] = dv_ref[WINDOW:2 * WINDOW, :]
        kc_ref[0:WINDOW, :] = kc_ref[WINDOW:2 * WINDOW, :]
        vc_ref[0:WINDOW, :] = vc_ref[WINDOW:2 * WINDOW, :]

    blk = lambda w: pl.BlockSpec((WINDOW, w), lambda s, n: (s * nblk + n, 0))
    pos = pl.BlockSpec((WINDOW, W), lambda s, n: (n, 0))
    kv_out = jax.ShapeDtypeStruct((T, KV_W), F32)
    return pl.pallas_call(
        body, name="attn_bwd", grid=(n_seq, nblk),
        in_specs=[pl.BlockSpec(memory_space=pltpu.SMEM), blk(W), blk(W), blk(KV_W), blk(KV_W), _const((1, W)),
                  _const((1, KV_W)), pos, pos, _const((W, LANES)), _const((LANES, W)), _const((KV_W, LANES)),
                  _const((LANES, KV_W))],
        out_specs=[blk(W), blk(KV_W), blk(KV_W), blk(KV_W), blk(KV_W), _const((1, W)), _const((WINDOW, LANES))],
        out_shape=[jax.ShapeDtypeStruct((T, W), BF16), kv_out, kv_out, kv_out, kv_out,
                   jax.ShapeDtypeStruct((1, W), F32), jax.ShapeDtypeStruct((WINDOW, LANES), F32)],
        scratch_shapes=[pltpu.VMEM((2 * WINDOW, KV_W), F32), pltpu.VMEM((2 * WINDOW, KV_W), F32),
                        pltpu.VMEM((WINDOW, W), F32), pltpu.VMEM((2 * WINDOW, KV_W), F32),
                        pltpu.VMEM((2 * WINDOW, KV_W), F32)],
        compiler_params=_params(2),
    )(sinks, do, q, k, v, qg, kg, cosf, sins, ind_q, ind_qt, ind_k, ind_kt)


def _kv_bwd(dkc, dkp, dvc, dvp, k, kg, cosf, sins, ind_k, ind_kt, n_seq, S):
    T = k.shape[0]
    nblk = S // WINDOW

    def body(dkc_ref, dkp_ref, dvc_ref, dvp_ref, k_ref, kg_ref, cos_ref, sin_ref, ik_ref, ikt_ref,
             dk_ref, dv_ref, dkg_ref):
        s_id, n = pl.program_id(0), pl.program_id(1)

        @pl.when((s_id == 0) & (n == 0))
        def _():
            dkg_ref[...] = jnp.zeros_like(dkg_ref)

        has_next = n < nblk - 1
        lane = lax.broadcasted_iota(jnp.int32, (WINDOW, KV_W), 1)
        dkr = dkc_ref[...] + jnp.where(has_next, dkp_ref[...], 0.0)
        dv_ref[...] = (dvc_ref[...] + jnp.where(has_next, dvp_ref[...], 0.0)).astype(BF16)
        cosf, sinv = cos_ref[:, :KV_W], sin_ref[:, :KV_W]
        kv = k_ref[...]
        _, rstd = _qk_prep(kv, kg_ref[...], cosf, sinv, ik_ref[...], ikt_ref[...], lane)
        dk, dkg = _qk_prep_bwd(dkr, kv, rstd, kg_ref[...], cosf, sinv, ik_ref[...], ikt_ref[...], lane)
        dk_ref[...] = dk.astype(BF16)
        dkg_ref[...] += dkg

    cur = pl.BlockSpec((WINDOW, KV_W), lambda s, n: (s * nblk + n, 0))
    nxt = pl.BlockSpec((WINDOW, KV_W), lambda s, n: (s * nblk + jnp.minimum(n + 1, nblk - 1), 0))
    pos = pl.BlockSpec((WINDOW, D_MODEL), lambda s, n: (n, 0))
    return pl.pallas_call(
        body, name="kv_bwd", grid=(n_seq, nblk),
        in_specs=[cur, nxt, cur, nxt, cur, _const((1, KV_W)), pos, pos, _const((KV_W, LANES)),
                  _const((LANES, KV_W))],
        out_specs=[cur, cur, _const((1, KV_W))],
        out_shape=[jax.ShapeDtypeStruct((T, KV_W), BF16), jax.ShapeDtypeStruct((T, KV_W), BF16),
                   jax.ShapeDtypeStruct((1, KV_W), F32)],
        compiler_params=_params(2),
    )(dkc, dkp, dvc, dvp, k, kg, cosf, sins, ind_k, ind_kt)


def _merge_fwd(x, ya, o, ga, gb, w_rnn, w_attn, w_out, tm):
    T = x.shape[0]
    W = D_MODEL

    def body(x_ref, ya_ref, o_ref, ga_ref, gb_ref, wr_ref, wa_ref, wo_ref, x1_ref, mg_ref, yao_ref, ybo_ref):
        y_a = _dot(ya_ref[...], wr_ref[...])
        y_b = _dot(o_ref[...], wa_ref[...])
        yao_ref[...] = y_a
        ybo_ref[...] = y_b
        mg = (_sigmoid(ga_ref[...]) * y_a + _sigmoid(gb_ref[...]) * y_b).astype(BF16)
        mg_ref[...] = mg
        x1_ref[...] = x_ref[...] + _dot(mg, wo_ref[...])

    row = pl.BlockSpec((tm, W), lambda i: (i, 0))
    sq = _const((W, W))
    return pl.pallas_call(
        body, name="merge_fwd", grid=(T // tm,),
        in_specs=[row, row, row, row, row, sq, sq, sq], out_specs=[row, row, row, row],
        out_shape=[jax.ShapeDtypeStruct((T, W), F32), jax.ShapeDtypeStruct((T, W), BF16),
                   jax.ShapeDtypeStruct((T, W), F32), jax.ShapeDtypeStruct((T, W), F32)],
        compiler_params=_params(1),
    )(x, ya, o, ga, gb, w_rnn, w_attn, w_out)


def _merge_bwd(dx1, ga, gb, y_a, y_b, w_rnn, w_attn, w_out, tm):
    T = dx1.shape[0]
    W = D_MODEL

    def body(dx1_ref, ga_ref, gb_ref, ya_ref, yb_ref, wr_ref, wa_ref, wo_ref,
             dga_ref, dgb_ref, dya_ref, dyb_ref, dyain_ref, do_ref):
        dm = _dot_nt(dx1_ref[...].astype(BF16), wo_ref[...])
        sa = _sigmoid(ga_ref[...])
        sb = _sigmoid(gb_ref[...])
        dga_ref[...] = (dm * ya_ref[...] * (sa * (1.0 - sa))).astype(BF16)
        dgb_ref[...] = (dm * yb_ref[...] * (sb * (1.0 - sb))).astype(BF16)
        dya = (dm * sa).astype(BF16)
        dyb = (dm * sb).astype(BF16)
        dya_ref[...] = dya
        dyb_ref[...] = dyb
        dyain_ref[...] = _dot_nt(dya, wr_ref[...])
        do_ref[...] = _dot_nt(dyb, wa_ref[...])

    row = pl.BlockSpec((tm, W), lambda i: (i, 0))
    sq = _const((W, W))
    b16 = jax.ShapeDtypeStruct((T, W), BF16)
    f32 = jax.ShapeDtypeStruct((T, W), F32)
    return pl.pallas_call(
        body, name="merge_bwd", grid=(T // tm,),
        in_specs=[row, row, row, row, row, sq, sq, sq], out_specs=[row] * 6,
        out_shape=[b16, b16, b16, b16, f32, f32], compiler_params=_params(1),
    )(dx1, ga, gb, y_a, y_b, w_rnn, w_attn, w_out)


def _mlp_fwd(x1, g_mlp, w_up, w_down, tm):
    T = x1.shape[0]
    W = D_MODEL

    def body(x_ref, g_ref, wu_ref, wd_ref, x2_ref, hm_ref, u_ref, act_ref):
        xv = x_ref[...]
        hm, _ = _rms_fwd(xv, g_ref[...])
        hmb = hm.astype(BF16)
        hm_ref[...] = hmb
        for j in range(N_CHIPS):
            u = _dot(hmb, wu_ref[j])
            u_ref[:, j * W:(j + 1) * W] = u
            ru = jnp.maximum(u, 0.0)
            act_ref[:, j * W:(j + 1) * W] = (ru * ru).astype(BF16)
        x2_ref[...] = xv + _dot(act_ref[...], wd_ref[...])

    row = lambda w: pl.BlockSpec((tm, w), lambda i: (i, 0))
    return pl.pallas_call(
        body, name="mlp_fwd", grid=(T // tm,),
        in_specs=[row(W), _const((1, W)), _const((N_CHIPS, W, W)), _const((D_FF, W))],
        out_specs=[row(W), row(W), row(D_FF), row(D_FF)],
        out_shape=[jax.ShapeDtypeStruct((T, W), F32), jax.ShapeDtypeStruct((T, W), BF16),
                   jax.ShapeDtypeStruct((T, D_FF), F32), jax.ShapeDtypeStruct((T, D_FF), BF16)],
        compiler_params=_params(1),
    )(x1, g_mlp, w_up, w_down)


def _mlp_bwd(dx2, u, x1, g_mlp, w_up, w_down, tm):
    T = x1.shape[0]
    W = D_MODEL

    def body(dx2_ref, u_ref, x_ref, g_ref, wu_ref, wd_ref, dx1_ref, du_ref, dg_ref):
        @pl.when(pl.program_id(0) == 0)
        def _():
            dg_ref[...] = jnp.zeros_like(dg_ref)

        dx2 = dx2_ref[...]
        dact = _dot_nt(dx2.astype(BF16), wd_ref[...])
        du_ref[...] = (dact * (2.0 * jnp.maximum(u_ref[...], 0.0))).astype(BF16)
        dhm = jnp.zeros((tm, W), F32)
        for j in range(N_CHIPS):
            dhm = dhm + _dot_nt(du_ref[:, j * W:(j + 1) * W], wu_ref[j])
        xv = x_ref[...]
        g = g_ref[...]
        _, r = _rms_fwd(xv, g)
        dx, dg = _rms_bwd(dhm, xv, r, g)
        dx1_ref[...] = dx2 + dx
        dg_ref[...] += dg

    row = lambda w: pl.BlockSpec((tm, w), lambda i: (i, 0))
    return pl.pallas_call(
        body, name="mlp_bwd", grid=(T // tm,),
        in_specs=[row(W), row(D_FF), row(W), _const((1, W)), _const((N_CHIPS, W, W)), _const((D_FF, W))],
        out_specs=[row(W), row(D_FF), _const((1, W))],
        out_shape=[jax.ShapeDtypeStruct((T, W), F32), jax.ShapeDtypeStruct((T, D_FF), BF16),
                   jax.ShapeDtypeStruct((1, W), F32)],
        compiler_params=_params(1),
    )(dx2, u, x1, g_mlp, w_up, w_down)


def _ple_loss(x2, p, target, g_ple, w_gate, w_proj, tm):
    T = x2.shape[0]
    W = D_MODEL
    cw = W // N_CHIPS

    def body(x_ref, p_ref, t_ref, g_ref, wg_ref, wp_ref, loss_ref, dx2_ref, pb_ref, de_ref, hp_ref, dtg_ref, dg_ref):
        @pl.when(pl.program_id(0) == 0)
        def _():
            dg_ref[...] = jnp.zeros_like(dg_ref)
            loss_ref[...] = jnp.zeros_like(loss_ref)

        xv = x_ref[...]
        g = g_ref[...]
        pb = p_ref[...].astype(BF16)
        pb_ref[...] = pb
        e = jnp.concatenate([_dot(pb, wp_ref[j]) for j in range(N_CHIPS)], axis=1)
        hp, r = _rms_fwd(xv, g)
        hpb = hp.astype(BF16)
        hp_ref[...] = hpb
        sg = _sigmoid(_dot(hpb, wg_ref[...]))
        diff = (xv + e * sg) - t_ref[...]
        loss_ref[...] += jnp.sum(diff * diff) * (0.5 / W)
        dx3 = diff * (1.0 / W)
        de_ref[...] = (dx3 * sg).astype(BF16)
        dtg = (dx3 * e * (sg * (1.0 - sg))).astype(BF16)
        dtg_ref[...] = dtg
        dx, dg = _rms_bwd(_dot_nt(dtg, wg_ref[...]), xv, r, g)
        dx2_ref[...] = dx3 + dx
        dg_ref[...] += dg

    row = lambda w: pl.BlockSpec((tm, w), lambda i: (i, 0))
    b16 = lambda w: jax.ShapeDtypeStruct((T, w), BF16)
    return pl.pallas_call(
        body, name="ple_loss", grid=(T // tm,),
        in_specs=[row(W), row(PLE_DIM), row(W), _const((1, W)), _const((W, W)), _const((N_CHIPS, PLE_DIM, cw))],
        out_specs=[_const((8, LANES)), row(W), row(PLE_DIM), row(W), row(W), row(W), _const((1, W))],
        out_shape=[jax.ShapeDtypeStruct((8, LANES), F32), jax.ShapeDtypeStruct((T, W), F32), b16(PLE_DIM),
                   b16(W), b16(W), b16(W), jax.ShapeDtypeStruct((1, W), F32)],
        compiler_params=_params(1),
    )(x2, p, target, g_ple, w_gate, w_proj)


def _adamw(w, g, m, v, name, tr):
    R, C = w.shape
    c1 = 1.0 / (1.0 - ADAM_B1 ** ADAM_STEP)
    c2 = 1.0 / (1.0 - ADAM_B2 ** ADAM_STEP)

    def body(w_ref, g_ref, m_ref, v_ref, d_ref, nm_ref, nv_ref):
        gv = g_ref[...]
        nm = ADAM_B1 * m_ref[...] + (1.0 - ADAM_B1) * gv
        nv = ADAM_B2 * v_ref[...] + (1.0 - ADAM_B2) * (gv * gv)
        nm_ref[...] = nm
        nv_ref[...] = nv
        d_ref[...] = (-ADAM_LR) * ((nm * c1) / (jnp.sqrt(nv * c2) + ADAM_EPS) + ADAM_WD * w_ref[...])

    row = pl.BlockSpec((tr, C), lambda i: (i, 0))
    sds = jax.ShapeDtypeStruct((R, C), F32)
    return pl.pallas_call(
        body, name=name, grid=(R // tr,), in_specs=[row] * 4, out_specs=[row] * 3, out_shape=[sds] * 3,
        compiler_params=_params(1),
    )(w, g, m, v)


def _indicator(width):
    ind = np.zeros((width, LANES), np.float32)
    ind[np.arange(width), np.arange(width) // HEAD_DIM] = 1.0
    return jnp.asarray(ind, BF16), jnp.asarray(ind.T, BF16)


def _rope_tables(S):
    inv = ROPE_THETA ** (-jnp.arange(0, HEAD_DIM, 2, dtype=F32) / HEAD_DIM)
    ang = jnp.arange(S, dtype=F32)[:, None] * inv[None, :]
    cos, sin = jnp.cos(ang), jnp.sin(ang)
    cosf = jnp.tile(jnp.concatenate([cos, cos], axis=1), (1, N_HEADS))
    sins = jnp.tile(jnp.concatenate([-sin, sin], axis=1), (1, N_HEADS))
    return cosf, sins


def _pair_blockdiag(w):
    w4 = w.reshape(8, 2, HEAD_DIM, HEAD_DIM)
    eye = jnp.eye(2, dtype=w.dtype)
    return jnp.einsum("bpij,pq->bpiqj", w4, eye).reshape(8, LANES, LANES)


def _pair_blockdiag_extract(g):
    g5 = g.reshape(8, 2, HEAD_DIM, 2, HEAD_DIM)
    return jnp.stack([g5[:, 0, :, 0, :], g5[:, 1, :, 1, :]], axis=1).reshape(16, HEAD_DIM, HEAD_DIM)


def _local_step(x, p, target, small, big, n_seq, S, tm=512, tm_rnn=256):
    cosf, sins = _rope_tables(S)
    ind_q, ind_qt = _indicator(D_MODEL)
    ind_k, ind_kt = _indicator(KV_W)
    row = lambda a: a.reshape(1, -1)
    g_mix, g_mlp, g_ple = row(small["g_mix"]), row(small["g_mlp"]), row(small["g_ple"])
    conv_w = small["conv_w"]
    conv_b, b_rg, b_ig, lam = row(small["conv_b"]), row(small["b_rg"]), row(small["b_ig"]), row(small["lru_lambda"])
    wrg2 = _pair_blockdiag(small["w_rg"]).astype(BF16)
    wig2 = _pair_blockdiag(small["w_ig"]).astype(BF16)
    qg = jnp.tile(row(small["q_gain"]), (1, N_HEADS))
    kg = jnp.tile(row(small["k_gain"]), (1, N_KV))
    sinks = small["sinks"].reshape(N_HEADS)

    h0, xr, gr, q, k, v, ga, gb = _inproj_fwd(x, g_mix, big["w_in"], tm)
    xc, h, ya = _rnn_fwd(xr, gr, conv_w, conv_b, wrg2, b_rg, wig2, b_ig, lam, n_seq, S, tm_rnn)
    o = _attn_fwd(q, k, v, qg, kg, sinks, cosf, sins, ind_q, ind_qt, ind_k, ind_kt, n_seq, S)
    x1, merged, y_a, y_b = _merge_fwd(x, ya, o, ga, gb, big["w_rnn_proj"], big["w_attn_proj"], big["w_out"], tm)
    x2, hm, u, act = _mlp_fwd(x1, g_mlp, big["w_up"], big["w_down"], tm // 2)
    loss, dx2, pb, de, hp, dtg, dg_ple = _ple_loss(x2, p, target, g_ple, big["w_ple_gate"], big["w_ple_proj"], tm)
    dx1, du, dg_mlp = _mlp_bwd(dx2, u, x1, g_mlp, big["w_up"], big["w_down"], tm // 2)
    dga, dgb, dya, dyb, dyain, do = _merge_bwd(dx1, ga, gb, y_a, y_b, big["w_rnn_proj"], big["w_attn_proj"],
                                               big["w_out"], tm)
    dxr, dgr, vec, dwrg2, dwig2 = _rnn_bwd(dyain, xr, gr, xc, h, conv_w, wrg2, b_rg, wig2, b_ig, lam, n_seq, S,
                                           tm_rnn)
    dq, dkc, dkp, dvc, dvp, dqg, dsk = _attn_bwd(do, q, k, v, qg, kg, sinks, cosf, sins, ind_q, ind_qt, ind_k,
                                                 ind_kt, n_seq, S)
    dk, dv, dkg = _kv_bwd(dkc, dkp, dvc, dvp, k, kg, cosf, sins, ind_k, ind_kt, n_seq, S)
    grad_x, dz, dg_mix = _inproj_bwd([dxr, dgr, dq, dk, dv, dga, dgb], big["w_in"], x, g_mix, dx1, tm)

    grads = {
        "w_in": _wgrad(h0, dz, "wgrad_in", True, IN_BLK, tm),
        "w_rnn_proj": _wgrad(ya, dya, "wgrad_rnn_proj", False, D_MODEL, tm),
        "w_attn_proj": _wgrad(o, dyb, "wgrad_attn_proj", False, D_MODEL, tm),
        "w_out": _wgrad(merged, dx1, "wgrad_out", False, D_MODEL, tm),
        "w_up": _wgrad(hm, du, "wgrad_up", True, D_MODEL, tm),
        "w_down": _wgrad(act, dx2, "wgrad_down", False, D_MODEL // 2, tm),
        "w_ple_gate": _wgrad(hp, dtg, "wgrad_ple_gate", False, D_MODEL, tm),
        "w_ple_proj": _wgrad(pb, de, "wgrad_ple_proj", True, D_MODEL // N_CHIPS, tm),
        "g_mix": dg_mix[0], "g_mlp": dg_mlp[0], "g_ple": dg_ple[0],
        "conv_w": vec[0:CONV_W], "conv_b": vec[4], "b_rg": vec[5], "b_ig": vec[6], "lru_lambda": vec[7],
        "w_rg": _pair_blockdiag_extract(dwrg2), "w_ig": _pair_blockdiag_extract(dwig2),
        "q_gain": dqg.reshape(N_HEADS, HEAD_DIM).sum(0), "k_gain": dkg.reshape(N_KV, HEAD_DIM).sum(0),
        "sinks": dsk.sum(0)[:N_HEADS],
    }
    return loss[0, 0], grad_x, grads


def _mesh_pos():
    x, y, c = lax.axis_index("x"), lax.axis_index("y"), lax.axis_index("c")
    other_chips = [(1 - x, y), (x, 1 - y), (1 - x, 1 - y)]
    return x, y, c, other_chips


def _peer_slot(k, x, y):
    dx = jnp.bitwise_xor(k // 2, x)
    dy = jnp.bitwise_xor(k % 2, y)
    return jnp.maximum(dx + 2 * dy - 1, 0)


def _gather_bf16(shard, name):
    R2, C = shard.shape
    R = R2 // 2

    def body(s_ref, o_ref, send_sems, recv_sems):
        x, y, c, chips = _mesh_pos()
        me = 2 * x + y
        mine = pl.ds(pl.multiple_of(c * R, R), R)
        theirs = pl.ds(pl.multiple_of((1 - c) * R, R), R)
        o_ref[me] = s_ref[...].astype(BF16)

        def copy(k, chip, rows, to):
            blk = o_ref.at[chip, rows]
            return pltpu.make_async_remote_copy(src_ref=blk, dst_ref=blk, send_sem=send_sems.at[k],
                                                recv_sem=recv_sems.at[k], device_id=to, device_id_type=MESH_ID)

        first = [copy(j, me, mine, (cx, cy, c)) for j, (cx, cy) in enumerate(chips)]
        for cp in first:
            cp.start()
        passed = []
        for j, (cx, cy) in enumerate(chips):
            copy(j, 2 * cx + cy, mine, (x, y, c)).wait_recv()
            cp = copy(3 + j, 2 * cx + cy, mine, (x, y, 1 - c))
            cp.start()
            passed.append(cp)
        for j, (cx, cy) in enumerate(chips):
            copy(3 + j, 2 * cx + cy, theirs, (x, y, c)).wait_recv()
        for cp in first + passed:
            cp.wait_send()

    return pl.pallas_call(
        body, name=name, out_shape=jax.ShapeDtypeStruct((N_CHIPS, R2, C), BF16),
        in_specs=[pl.BlockSpec(memory_space=pltpu.VMEM)], out_specs=pl.BlockSpec(memory_space=pltpu.VMEM),
        scratch_shapes=[pltpu.SemaphoreType.DMA((6,)), pltpu.SemaphoreType.DMA((6,))],
        compiler_params=pltpu.CompilerParams(vmem_limit_bytes=VMEM_LIMIT),
    )(shard)


def _reduce_scatter(partial, name):
    _, R2, C = partial.shape
    R = R2 // 2

    def body(p_ref, o_ref, mine_ref, sib_ref, out_ref, in_ref, loc_sem, pair_sems, send_sems, recv_sems, fin_sems):
        x, y, c, chips = _mesh_pos()
        me = 2 * x + y
        rows_c = pl.ds(pl.multiple_of(c * R, R), R)
        rows_o = pl.ds(pl.multiple_of((1 - c) * R, R), R)
        sibling = (x, y, 1 - c)
        loc = pltpu.make_async_copy(p_ref.at[:, rows_c, :], mine_ref, loc_sem)
        pair = pltpu.make_async_remote_copy(src_ref=p_ref.at[:, rows_o, :], dst_ref=sib_ref, send_sem=pair_sems.at[0],
                                            recv_sem=pair_sems.at[1], device_id=sibling, device_id_type=MESH_ID)
        loc.start()
        pair.start()
        loc.wait()
        pair.wait()
        sends = []
        for j, (cx, cy) in enumerate(chips):
            k = 2 * cx + cy
            out_ref[j] = (mine_ref[k] + sib_ref[k]).astype(BF16)
            cp = pltpu.make_async_remote_copy(src_ref=out_ref.at[j], dst_ref=in_ref.at[j], send_sem=send_sems.at[j],
                                              recv_sem=recv_sems.at[j], device_id=(cx, cy, c), device_id_type=MESH_ID)
            cp.start()
            sends.append(cp)
        own = mine_ref[me] + sib_ref[me]
        for cp in sends:
            cp.wait_recv()
        acc = None
        for k in range(N_CHIPS):
            term = jnp.where(me == k, own, in_ref[_peer_slot(k, x, y)].astype(F32))
            acc = term if acc is None else acc + term
        o_ref[rows_c, :] = acc
        fin = pltpu.make_async_remote_copy(src_ref=o_ref.at[rows_c, :], dst_ref=o_ref.at[rows_c, :],
                                           send_sem=fin_sems.at[0], recv_sem=fin_sems.at[1], device_id=sibling,
                                           device_id_type=MESH_ID)
        fin.start()
        fin.wait_send()
        pltpu.make_async_remote_copy(src_ref=o_ref.at[rows_o, :], dst_ref=o_ref.at[rows_o, :], send_sem=fin_sems.at[0],
                                     recv_sem=fin_sems.at[1], device_id=sibling, device_id_type=MESH_ID).wait_recv()
        for cp in sends:
            cp.wait_send()

    return pl.pallas_call(
        body, name=name, out_shape=jax.ShapeDtypeStruct((R2, C), F32),
        in_specs=[pl.BlockSpec(memory_space=pl.ANY)], out_specs=pl.BlockSpec(memory_space=pltpu.VMEM),
        scratch_shapes=[pltpu.VMEM((N_CHIPS, R, C), F32), pltpu.VMEM((N_CHIPS, R, C), F32),
                        pltpu.VMEM((3, R, C), BF16), pltpu.VMEM((3, R, C), BF16),
                        pltpu.SemaphoreType.DMA, pltpu.SemaphoreType.DMA((2,)), pltpu.SemaphoreType.DMA((3,)),
                        pltpu.SemaphoreType.DMA((3,)), pltpu.SemaphoreType.DMA((2,))],
        compiler_params=pltpu.CompilerParams(vmem_limit_bytes=VMEM_LIMIT),
    )(partial)


def _allreduce_small(buf, name):
    shape = buf.shape

    def body(b_ref, o_ref, sib_ref, pair_ref, in_ref, pair_sems, send_sems, recv_sems):
        x, y, c, chips = _mesh_pos()
        me = 2 * x + y
        pair = pltpu.make_async_remote_copy(src_ref=b_ref, dst_ref=sib_ref, send_sem=pair_sems.at[0],
                                            recv_sem=pair_sems.at[1], device_id=(x, y, 1 - c), device_id_type=MESH_ID)
        pair.start()
        pair.wait()
        pair_ref[...] = b_ref[...] + sib_ref[...]
        sends = []
        for j, (cx, cy) in enumerate(chips):
            cp = pltpu.make_async_remote_copy(src_ref=pair_ref, dst_ref=in_ref.at[j], send_sem=send_sems.at[j],
                                              recv_sem=recv_sems.at[j], device_id=(cx, cy, c), device_id_type=MESH_ID)
            cp.start()
            sends.append(cp)
        for cp in sends:
            cp.wait_recv()
        acc = None
        for k in range(N_CHIPS):
            term = jnp.where(me == k, pair_ref[...], in_ref[_peer_slot(k, x, y)])
            acc = term if acc is None else acc + term
        o_ref[...] = acc
        for cp in sends:
            cp.wait_send()

    return pl.pallas_call(
        body, name=name, out_shape=jax.ShapeDtypeStruct(shape, F32),
        in_specs=[pl.BlockSpec(memory_space=pltpu.VMEM)], out_specs=pl.BlockSpec(memory_space=pltpu.VMEM),
        scratch_shapes=[pltpu.VMEM(shape, F32), pltpu.VMEM(shape, F32), pltpu.VMEM((3,) + shape, F32),
                        pltpu.SemaphoreType.DMA((2,)), pltpu.SemaphoreType.DMA((3,)), pltpu.SemaphoreType.DMA((3,))],
        compiler_params=pltpu.CompilerParams(vmem_limit_bytes=VMEM_LIMIT),
    )(buf)


def _adamw_small(ws, gs, ms, vs):
    n = len(ws)
    c1 = 1.0 / (1.0 - ADAM_B1 ** ADAM_STEP)
    c2 = 1.0 / (1.0 - ADAM_B2 ** ADAM_STEP)

    def body(*refs):
        w_r, g_r, m_r, v_r = refs[0:n], refs[n:2 * n], refs[2 * n:3 * n], refs[3 * n:4 * n]
        d_r, nm_r, nv_r = refs[4 * n:5 * n], refs[5 * n:6 * n], refs[6 * n:7 * n]
        for i in range(n):
            gv = g_r[i][...]
            nm = ADAM_B1 * m_r[i][...] + (1.0 - ADAM_B1) * gv
            nv = ADAM_B2 * v_r[i][...] + (1.0 - ADAM_B2) * (gv * gv)
            nm_r[i][...] = nm
            nv_r[i][...] = nv
            d_r[i][...] = (-ADAM_LR) * ((nm * c1) / (jnp.sqrt(nv * c2) + ADAM_EPS) + ADAM_WD * w_r[i][...])

    vm = pl.BlockSpec(memory_space=pltpu.VMEM)
    sds = [jax.ShapeDtypeStruct(w.shape, F32) for w in ws]
    out = pl.pallas_call(body, name="adamw_small", in_specs=[vm] * (4 * n), out_specs=[vm] * (3 * n),
                         out_shape=sds * 3)(*ws, *gs, *ms, *vs)
    return out[0:n], out[n:2 * n], out[2 * n:3 * n]


_BIG = ("w_in", "w_rnn_proj", "w_attn_proj", "w_out", "w_up", "w_down", "w_ple_gate", "w_ple_proj")
_SMALL = ("g_mix", "conv_w", "conv_b", "w_rg", "b_rg", "w_ig", "b_ig", "lru_lambda", "q_gain", "k_gain", "sinks",
          "g_mlp", "g_ple")
_WEIGHTS = ("g_mix", "w_in", "conv_w", "conv_b", "w_rg", "b_rg", "w_ig", "b_ig", "lru_lambda", "w_rnn_proj",
            "q_gain", "k_gain", "sinks", "w_attn_proj", "w_out", "g_mlp", "w_up", "w_down", "g_ple", "w_ple_gate",
            "w_ple_proj")


def _pad_row(v):
    v = v.reshape(1, -1)
    return jnp.pad(v, ((0, 0), (0, D_MODEL - v.shape[1])))


def kernel(x, p, g_mix, w_in, conv_w, conv_b, w_rg, b_rg, w_ig, b_ig, lru_lambda, w_rnn_proj, q_gain, k_gain, sinks, w_attn_proj, w_out, g_mlp, w_up, w_down, g_ple, w_ple_gate, w_ple_proj, loss_target, m_g_mix, m_w_in, m_conv_w, m_conv_b, m_w_rg, m_b_rg, m_w_ig, m_b_ig, m_lru_lambda, m_w_rnn_proj, m_q_gain, m_k_gain, m_sinks, m_w_attn_proj, m_w_out, m_g_mlp, m_w_up, m_w_down, m_g_ple, m_w_ple_gate, m_w_ple_proj, v_g_mix, v_w_in, v_conv_w, v_conv_b, v_w_rg, v_b_rg, v_w_ig, v_b_ig, v_lru_lambda, v_w_rnn_proj, v_q_gain, v_k_gain, v_sinks, v_w_attn_proj, v_w_out, v_g_mlp, v_w_up, v_w_down, v_g_ple, v_w_ple_gate, v_w_ple_proj):
    w = dict(g_mix=g_mix, w_in=w_in, conv_w=conv_w, conv_b=conv_b, w_rg=w_rg, b_rg=b_rg, w_ig=w_ig, b_ig=b_ig,
             lru_lambda=lru_lambda, w_rnn_proj=w_rnn_proj, q_gain=q_gain, k_gain=k_gain, sinks=sinks,
             w_attn_proj=w_attn_proj, w_out=w_out, g_mlp=g_mlp, w_up=w_up, w_down=w_down, g_ple=g_ple,
             w_ple_gate=w_ple_gate, w_ple_proj=w_ple_proj)
    m = dict(g_mix=m_g_mix, w_in=m_w_in, conv_w=m_conv_w, conv_b=m_conv_b, w_rg=m_w_rg, b_rg=m_b_rg, w_ig=m_w_ig,
             b_ig=m_b_ig, lru_lambda=m_lru_lambda, w_rnn_proj=m_w_rnn_proj, q_gain=m_q_gain, k_gain=m_k_gain,
             sinks=m_sinks, w_attn_proj=m_w_attn_proj, w_out=m_w_out, g_mlp=m_g_mlp, w_up=m_w_up, w_down=m_w_down,
             g_ple=m_g_ple, w_ple_gate=m_w_ple_gate, w_ple_proj=m_w_ple_proj)
    v = dict(g_mix=v_g_mix, w_in=v_w_in, conv_w=v_conv_w, conv_b=v_conv_b, w_rg=v_w_rg, b_rg=v_b_rg, w_ig=v_w_ig,
             b_ig=v_b_ig, lru_lambda=v_lru_lambda, w_rnn_proj=v_w_rnn_proj, q_gain=v_q_gain, k_gain=v_k_gain,
             sinks=v_sinks, w_attn_proj=v_w_attn_proj, w_out=v_w_out, g_mlp=v_g_mlp, w_up=v_w_up, w_down=v_w_down,
             g_ple=v_g_ple, w_ple_gate=v_w_ple_gate, w_ple_proj=v_w_ple_proj)
    n_seq, S, _ = x.shape
    T = n_seq * S
    chip = 2 * lax.axis_index("x") + lax.axis_index("y")

    big = {}
    for name in _BIG:
        shard = w[name][0]
        g = _gather_bf16(shard, "gather_" + name)
        big[name] = g if name in ("w_in", "w_up", "w_ple_proj") else g.reshape(N_CHIPS * shard.shape[0], shard.shape[1])

    cw_full = jnp.zeros((8, D_MODEL), F32)
    cw_full = lax.dynamic_update_slice(cw_full, conv_w[0], (0, chip * (D_MODEL // N_CHIPS)))
    cw_full = _allreduce_small(0.5 * cw_full.reshape(64, LANES), "allgather_conv_w").reshape(8, D_MODEL)[0:CONV_W]

    small = {k: w[k][0] for k in _SMALL}
    small["conv_w"] = cw_full
    loss_sum, grad_x, grads = _local_step(x.reshape(T, D_MODEL), p.reshape(T, PLE_DIM),
                                          loss_target.reshape(T, D_MODEL), small, big, n_seq, S)
    loss = lax.psum(loss_sum, ("x", "y", "c"))

    rows = [grads["conv_w"], _pad_row(grads["conv_b"]), _pad_row(grads["b_rg"]), _pad_row(grads["b_ig"]),
            _pad_row(grads["lru_lambda"]), _pad_row(grads["g_mix"]), _pad_row(grads["g_mlp"]),
            _pad_row(grads["g_ple"]), _pad_row(grads["q_gain"]), _pad_row(grads["k_gain"]), _pad_row(grads["sinks"]),
            jnp.zeros((2, D_MODEL), F32)]
    vecs = jnp.concatenate(rows, axis=0)
    packed = jnp.concatenate([vecs.reshape(-1, LANES), grads["w_rg"].reshape(-1, LANES),
                              grads["w_ig"].reshape(-1, LANES)], axis=0)
    red = _allreduce_small(packed, "allreduce_small")
    nv = vecs.size // LANES
    rvec = red[0:nv].reshape(16, D_MODEL)
    nw = grads["w_rg"].size // LANES
    sg = {
        "conv_w": lax.dynamic_slice(rvec[0:CONV_W], (0, chip * (D_MODEL // N_CHIPS)), (CONV_W, D_MODEL // N_CHIPS)),
        "conv_b": rvec[4], "b_rg": rvec[5], "b_ig": rvec[6], "lru_lambda": rvec[7], "g_mix": rvec[8],
        "g_mlp": rvec[9], "g_ple": rvec[10], "q_gain": rvec[11, :HEAD_DIM], "k_gain": rvec[12, :HEAD_DIM],
        "sinks": rvec[13, :N_HEADS], "w_rg": red[nv:nv + nw], "w_ig": red[nv + nw:nv + 2 * nw],
    }
    sg = {k: sg[k].reshape(w[k].shape) for k in _SMALL}
    d_s, m_s, v_s = _adamw_small([w[k] for k in _SMALL], [sg[k] for k in _SMALL], [m[k] for k in _SMALL],
                                 [v[k] for k in _SMALL])
    grad, delta, new_m, new_v = dict(sg), dict(zip(_SMALL, d_s)), dict(zip(_SMALL, m_s)), dict(zip(_SMALL, v_s))

    for name in _BIG:
        shape = w[name].shape
        part = grads[name]
        if part.ndim == 2:
            part = part.reshape(N_CHIPS, shape[1], shape[2])
        g = _reduce_scatter(part, "reduce_" + name)
        d, nm, nvv = _adamw(w[name][0], g, m[name][0], v[name][0], "adamw_" + name, 128)
        grad[name], delta[name], new_m[name], new_v[name] = (a.reshape(shape) for a in (g, d, nm, nvv))

    return (loss, grad_x.reshape(x.shape), *[grad[k] for k in _WEIGHTS], *[delta[k] for k in _WEIGHTS],
            *[new_m[k] for k in _WEIGHTS], *[new_v[k] for k in _WEIGHTS])
```

```python
import functools
import math

import numpy as np
import jax
import jax.numpy as jnp
from jax import lax
from jax.experimental import pallas as pl
from jax.experimental.pallas import tpu as pltpu

F32 = jnp.float32
BF16 = jnp.bfloat16

D_MODEL = 1024
N_HEADS = 16
N_KV = 4
HEAD_DIM = 64
KV_W = N_KV * HEAD_DIM
D_FF = 4096
PLE_DIM = 256
WINDOW = 128
CONV_W = 4
LRU_C = 8.0
NORM_EPS = 1e-6
ROPE_THETA = 10000.0
N_CHIPS = 4
IN_TOTAL = 5632
IN_BLK = IN_TOTAL // N_CHIPS
IN_SEGS = (0, 1024, 2048, 3072, 3328, 3584, 4608, 5632)

ADAM_LR = 0.001
ADAM_B1 = 0.9
ADAM_B2 = 0.999
ADAM_EPS = 1e-08
ADAM_WD = 0.01
ADAM_STEP = 10

LANES = 128
VMEM_LIMIT = 56 * 1024 * 1024
MESH_ID = pl.DeviceIdType.MESH


def _dot(a, b):
    return jnp.dot(a, b, preferred_element_type=F32)


def _dot_nt(a, b):
    return lax.dot_general(a, b, (((1,), (1,)), ((), ())), preferred_element_type=F32)


def _dot_tn(a, b):
    return lax.dot_general(a, b, (((0,), (0,)), ((), ())), preferred_element_type=F32)


def _split_dot(x, ind):
    hi = x.astype(BF16)
    lo = (x - hi.astype(F32)).astype(BF16)
    return _dot(hi, ind) + _dot(lo, ind)


def _sigmoid(x):
    return 1.0 / (1.0 + jnp.exp(-x))


_GELU_C = math.sqrt(2.0 / math.pi)


def _gelu_and_grad(g):
    inner = _GELU_C * (g + 0.044715 * g * g * g)
    t = jnp.tanh(inner)
    gelu = 0.5 * g * (1.0 + t)
    dgelu = 0.5 * (1.0 + t) + 0.5 * g * (1.0 - t * t) * _GELU_C * (1.0 + 3.0 * 0.044715 * g * g)
    return gelu, dgelu


def _const(shape):
    nd = len(shape)
    return pl.BlockSpec(shape, lambda *_: (0,) * nd)


def _params(n_grid, vmem=VMEM_LIMIT):
    return pltpu.CompilerParams(dimension_semantics=("arbitrary",) * n_grid, vmem_limit_bytes=vmem)


def _rms_fwd(x, g):
    r = lax.rsqrt(jnp.mean(x * x, axis=-1, keepdims=True) + NORM_EPS)
    return (x * r) * g, r


def _rms_bwd(dy, x, r, g):
    dn = dy * g
    dx = r * dn - x * (r * r * r * jnp.mean(dn * x, axis=-1, keepdims=True))
    dg = jnp.sum(dy * (x * r), axis=0, keepdims=True)
    return dx, dg


def _seg_pieces(blk_lo, blk_hi):
    out = []
    for s in range(7):
        lo, hi = max(blk_lo, IN_SEGS[s]), min(blk_hi, IN_SEGS[s + 1])
        if lo < hi:
            out.append((s, lo - IN_SEGS[s], hi - IN_SEGS[s], lo - blk_lo))
    return out


def _mesh_pos():
    x, y, c = lax.axis_index("x"), lax.axis_index("y"), lax.axis_index("c")
    other_chips = [(1 - x, y), (x, 1 - y), (1 - x, 1 - y)]
    return x, y, c, other_chips


def _peer_slot(k, x, y):
    dx = jnp.bitwise_xor(k // 2, x)
    dy = jnp.bitwise_xor(k % 2, y)
    return jnp.maximum(dx + 2 * dy - 1, 0)


def _half_rows(c, R):
    return pl.ds(pl.multiple_of(c * R, R), R), pl.ds(pl.multiple_of((1 - c) * R, R), R)


def _remote(src, dst, sems, to):
    return pltpu.make_async_remote_copy(src_ref=src, dst_ref=dst, send_sem=sems[0], recv_sem=sems[1],
                                        device_id=to, device_id_type=MESH_ID)


class _Phase:
    def __init__(self, ins, inout, outs, n_remote, n_local, build):
        self.ins, self.inout, self.outs = list(ins), list(inout), list(outs)
        self.n_remote, self.n_local, self.build = n_remote, n_local, build


def _ph_gather_send(wb):
    R2, C = wb.shape
    R = R2 // 2

    def build(ins, outs, rsem, lsem):
        (w_ref,), (g_ref,) = ins, outs
        x, y, c, chips = _mesh_pos()
        me = 2 * x + y
        mine, _ = _half_rows(c, R)
        loc = [pltpu.make_async_copy(w_ref, g_ref.at[me], lsem(0))]
        outg = [_remote(w_ref.at[mine], g_ref.at[me, mine], rsem(j), (cx, cy, c)) for j, (cx, cy) in enumerate(chips)]
        inc = [functools.partial(_remote, w_ref.at[mine], g_ref.at[2 * cx + cy, mine], rsem(j), (x, y, c))
               for j, (cx, cy) in enumerate(chips)]
        return loc, outg, inc

    return _Phase([wb], [], [jax.ShapeDtypeStruct((N_CHIPS, R2, C), BF16)], 3, 1, build)


def _ph_gather_pass(gath):
    _, R2, C = gath.shape
    R = R2 // 2

    def build(ins, outs, rsem, lsem):
        (g_ref,) = outs
        x, y, c, chips = _mesh_pos()
        mine, theirs = _half_rows(c, R)
        outg, inc = [], []
        for j, (cx, cy) in enumerate(chips):
            blk = g_ref.at[2 * cx + cy, mine]
            outg.append(_remote(blk, blk, rsem(j), (x, y, 1 - c)))
            got = g_ref.at[2 * cx + cy, theirs]
            inc.append(functools.partial(_remote, got, got, rsem(j), (x, y, c)))
        return [], outg, inc

    return _Phase([], [gath], [], 3, 0, build)


def _ph_pair_send(partial):
    _, R2, C = partial.shape
    R = R2 // 2

    def build(ins, outs, rsem, lsem):
        (p_ref,), (s_ref,) = ins, outs
        x, y, c, _ = _mesh_pos()
        _, theirs = _half_rows(c, R)
        src = p_ref.at[:, theirs, :]
        return ([], [_remote(src, s_ref, rsem(0), (x, y, 1 - c))],
                [functools.partial(_remote, src, s_ref, rsem(0), (x, y, c))])

    return _Phase([partial], [], [jax.ShapeDtypeStruct((N_CHIPS, R, C), F32)], 1, 0, build)


def _ph_chip_send(sendb):
    def build(ins, outs, rsem, lsem):
        (s_ref,), (r_ref,) = ins, outs
        x, y, c, chips = _mesh_pos()
        outg = [_remote(s_ref.at[j], r_ref.at[j], rsem(j), (cx, cy, c)) for j, (cx, cy) in enumerate(chips)]
        inc = [functools.partial(_remote, s_ref.at[j], r_ref.at[j], rsem(j), (x, y, c)) for j in range(3)]
        return [], outg, inc

    return _Phase([sendb], [], [jax.ShapeDtypeStruct(sendb.shape, sendb.dtype)], 3, 0, build)


def _ph_half_swap(red):
    R2, C = red.shape
    R = R2 // 2

    def build(ins, outs, rsem, lsem):
        (r_ref,) = outs
        x, y, c, _ = _mesh_pos()
        mine, theirs = _half_rows(c, R)
        return ([], [_remote(r_ref.at[mine], r_ref.at[mine], rsem(0), (x, y, 1 - c))],
                [functools.partial(_remote, r_ref.at[theirs], r_ref.at[theirs], rsem(0), (x, y, c))])

    return _Phase([], [red], [], 1, 0, build)


def _call(body, *, name, grid, in_specs, out_specs, out_shape, scratch_shapes=(), phases=()):
    single = not isinstance(out_specs, (list, tuple))
    out_specs = [out_specs] if single else list(out_specs)
    out_shape = [out_shape] if single else list(out_shape)
    n_in, n_out, n_scr = len(in_specs), len(out_specs), len(scratch_shapes)
    if not phases:
        call = pl.pallas_call(body, name=name, grid=grid, in_specs=in_specs, out_specs=out_specs,
                              out_shape=out_shape, scratch_shapes=list(scratch_shapes),
                              compiler_params=_params(len(grid)))
        return lambda *operands: (list(call(*operands)), [])

    ex_in, ex_out, aliases, spans = [], [], {}, []
    for ph in phases:
        i0, o0 = len(ex_in), len(ex_out)
        ex_in += ph.ins
        for a in ph.inout:
            aliases[n_in + len(ex_in)] = n_out + len(ex_out)
            ex_in.append(a)
            ex_out.append(jax.ShapeDtypeStruct(a.shape, a.dtype))
        ex_out += ph.outs
        spans.append((i0, len(ph.ins), o0, len(ex_out) - o0))
    n_remote = sum(ph.n_remote for ph in phases)
    n_local = max(sum(ph.n_local for ph in phases), 1)

    def wrapped(*refs):
        base_in, xin = refs[:n_in], refs[n_in:n_in + len(ex_in)]
        o0 = n_in + len(ex_in)
        base_out, xout = refs[o0:o0 + n_out], refs[o0 + n_out:o0 + n_out + len(ex_out)]
        scr = refs[o0 + n_out + len(ex_out):]
        send_sems, recv_sems, loc_sems = scr[n_scr:]
        first = functools.reduce(jnp.logical_and, [pl.program_id(i) == 0 for i in range(len(grid))])
        last = functools.reduce(jnp.logical_and, [pl.program_id(i) == grid[i] - 1 for i in range(len(grid))])

        def copies():
            out, r0, l0 = [], 0, 0
            for ph, (i0, ni, p0, no) in zip(phases, spans):
                rsem = lambda k, r0=r0: (send_sems.at[r0 + k], recv_sems.at[r0 + k])
                lsem = lambda k, l0=l0: loc_sems.at[l0 + k]
                out.append(ph.build(xin[i0:i0 + ni], xout[p0:p0 + no], rsem, lsem))
                r0, l0 = r0 + ph.n_remote, l0 + ph.n_local
            return out

        @pl.when(first)
        def _():
            for loc, outg, _ in copies():
                for cp in loc + outg:
                    cp.start()

        body(*base_in, *base_out, *scr[:n_scr])

        @pl.when(last)
        def _():
            for loc, outg, inc in copies():
                for make in inc:
                    make().wait_recv()
                for cp in outg:
                    cp.wait_send()
                for cp in loc:
                    cp.wait()

    hbm = pl.BlockSpec(memory_space=pl.ANY)
    call = pl.pallas_call(
        wrapped, name=name, grid=grid, in_specs=list(in_specs) + [hbm] * len(ex_in),
        out_specs=out_specs + [hbm] * len(ex_out), out_shape=out_shape + ex_out,
        scratch_shapes=list(scratch_shapes) + [pltpu.SemaphoreType.DMA((n_remote,)), pltpu.SemaphoreType.DMA((n_remote,)),
                                              pltpu.SemaphoreType.DMA((n_local,))],
        input_output_aliases=aliases, compiler_params=_params(len(grid)))

    def run(*operands):
        res = call(*operands, *ex_in)
        extra = res[n_out:]
        return list(res[:n_out]), [list(extra[p0:p0 + no]) for (_, _, p0, no) in spans]

    return run


def _inproj_fwd(x, g_mix, w_in, tm, phases=()):
    T = x.shape[0]
    widths = [IN_SEGS[i + 1] - IN_SEGS[i] for i in range(7)]

    def body(x_ref, g_ref, w_ref, h_ref, *z_refs):
        h, _ = _rms_fwd(x_ref[...], g_ref[...])
        hb = h.astype(BF16)
        h_ref[...] = hb
        for j in range(N_CHIPS):
            zj = _dot(hb, w_ref[j])
            for s, lo, hi, off in _seg_pieces(j * IN_BLK, (j + 1) * IN_BLK):
                z_refs[s][:, lo:hi] = zj[:, off:off + hi - lo]

    return _call(
        body, phases=phases, name="inproj_fwd", grid=(T // tm,),
        in_specs=[pl.BlockSpec((tm, D_MODEL), lambda i: (i, 0)), _const((1, D_MODEL)),
                  _const((N_CHIPS, D_MODEL, IN_BLK))],
        out_specs=[pl.BlockSpec((tm, D_MODEL), lambda i: (i, 0))]
        + [pl.BlockSpec((tm, w), lambda i: (i, 0)) for w in widths],
        out_shape=[jax.ShapeDtypeStruct((T, D_MODEL), BF16)]
        + [jax.ShapeDtypeStruct((T, w), F32) for w in widths],
    )(x, g_mix, w_in)


def _inproj_bwd(dz_parts, w_in, x, g_mix, dx1, tm, phases=()):
    T = x.shape[0]
    widths = [IN_SEGS[i + 1] - IN_SEGS[i] for i in range(7)]

    def body(*refs):
        p_refs = refs[:7]
        w_ref, x_ref, g_ref, dx1_ref, gx_ref, dz_ref, dg_ref = refs[7:]

        @pl.when(pl.program_id(0) == 0)
        def _():
            dg_ref[...] = jnp.zeros_like(dg_ref)

        for s in range(7):
            dz_ref[:, IN_SEGS[s]:IN_SEGS[s + 1]] = p_refs[s][...]
        dh = jnp.zeros((tm, D_MODEL), F32)
        for j in range(N_CHIPS):
            dh = dh + _dot_nt(dz_ref[:, j * IN_BLK:(j + 1) * IN_BLK], w_ref[j])
        xv = x_ref[...]
        g = g_ref[...]
        _, r = _rms_fwd(xv, g)
        dx, dg = _rms_bwd(dh, xv, r, g)
        gx_ref[...] = dx1_ref[...] + dx
        dg_ref[...] += dg

    row = lambda w: pl.BlockSpec((tm, w), lambda i: (i, 0))
    return _call(
        body, phases=phases, name="inproj_bwd", grid=(T // tm,),
        in_specs=[row(w) for w in widths]
        + [_const((N_CHIPS, D_MODEL, IN_BLK)), row(D_MODEL), _const((1, D_MODEL)), row(D_MODEL)],
        out_specs=[row(D_MODEL), row(IN_TOTAL), _const((1, D_MODEL))],
        out_shape=[jax.ShapeDtypeStruct((T, D_MODEL), F32), jax.ShapeDtypeStruct((T, IN_TOTAL), BF16),
                   jax.ShapeDtypeStruct((1, D_MODEL), F32)],
    )(*dz_parts, w_in, x, g_mix, dx1)


def _wgrad(a, g, name, blocked, cn, tm, phases=()):
    T, K = a.shape
    N = g.shape[1]
    nb = N // cn

    def body(a_ref, g_ref, o_ref):
        @pl.when(pl.program_id(1) == 0)
        def _():
            o_ref[...] = jnp.zeros_like(o_ref)

        o_ref[...] += _dot_tn(a_ref[...].astype(BF16), g_ref[...].astype(BF16))

    if blocked:
        out_spec = pl.BlockSpec((None, K, cn), lambda j, t: (j, 0, 0))
        out_shape = jax.ShapeDtypeStruct((nb, K, cn), F32)
    else:
        out_spec = pl.BlockSpec((K, cn), lambda j, t: (0, j))
        out_shape = jax.ShapeDtypeStruct((K, N), F32)
    outs, extra = _call(
        body, phases=phases, name=name, grid=(nb, T // tm),
        in_specs=[pl.BlockSpec((tm, K), lambda j, t: (t, 0)), pl.BlockSpec((tm, cn), lambda j, t: (t, j))],
        out_specs=out_spec, out_shape=out_shape,
    )(a, g)
    return outs[0], extra


def _shift_down(x, prev8, sft, row, row8, tm):
    xs = pltpu.roll(x, sft, 0)
    top = jnp.where(row8 < sft, pltpu.roll(prev8, sft, 0), xs[0:8])
    return jnp.concatenate([top, xs[8:]], axis=0)


def _shift_up(x, next8, sft, row8, tm):
    xs = pltpu.roll(x, tm - sft, 0)
    bot = jnp.where(row8 >= 8 - sft, pltpu.roll(next8, 8 - sft, 0), xs[tm - 8:tm])
    return jnp.concatenate([xs[0:tm - 8], bot], axis=0)


def _conv_fwd(x, prev8, cw_ref, cb, row, row8, tm):
    xc = cb + cw_ref[CONV_W - 1:CONV_W, :] * x
    for sft in range(1, CONV_W):
        j = CONV_W - 1 - sft
        xc = xc + cw_ref[j:j + 1, :] * _shift_down(x, prev8, sft, row, row8, tm)
    return xc


def _blockdiag_dot(xb, w_ref, transpose):
    outs = []
    for b in range(D_MODEL // LANES):
        xs = xb[:, b * LANES:(b + 1) * LANES]
        outs.append(_dot_nt(xs, w_ref[b]) if transpose else _dot(xs, w_ref[b]))
    return jnp.concatenate(outs, axis=1)


def _softplus_neg(lam):
    e = jnp.exp(-jnp.abs(lam))
    u = 1.0 + e
    log1p_e = jnp.where(u == 1.0, e, jnp.log(u) * (e / (u - 1.0)))
    sp = jnp.maximum(-lam, 0.0) + log1p_e
    return sp, -_sigmoid(-lam)


def _lru_gates(xc, wrg_ref, brg, wig_ref, big, sp):
    xcb = xc.astype(BF16)
    r = _sigmoid(_blockdiag_dot(xcb, wrg_ref, False) + brg)
    i = _sigmoid(_blockdiag_dot(xcb, wig_ref, False) + big)
    log_a = (-LRU_C) * r * sp
    a = jnp.exp(log_a)
    t = jnp.tanh(log_a)
    one_m_a2 = (-2.0) * t / (1.0 - t)
    mult = jnp.sqrt(one_m_a2)
    return xcb, r, i, a, mult


def _scan_down(a, b, row, tm):
    d = 1
    while d < tm:
        keep = row >= d
        a_s = jnp.where(keep, pltpu.roll(a, d, 0), 1.0)
        b_s = jnp.where(keep, pltpu.roll(b, d, 0), 0.0)
        b = a * b_s + b
        a = a * a_s
        d *= 2
    return a, b


def _scan_up(c, b, row, tm):
    d = 1
    while d < tm:
        keep = row < tm - d
        c_s = jnp.where(keep, pltpu.roll(c, tm - d, 0), 1.0)
        b_s = jnp.where(keep, pltpu.roll(b, tm - d, 0), 0.0)
        b = c * b_s + b
        c = c * c_s
        d *= 2
    return c, b


def _rnn_fwd(xr, gr, conv_w, conv_b, wrg2, b_rg, wig2, b_ig, lam, n_seq, S, tm, phases=()):
    T = xr.shape[0]
    nt = S // tm
    W = D_MODEL

    def body(xr_ref, gr_ref, cw_ref, cb_ref, wrg_ref, brg_ref, wig_ref, big_ref, lam_ref,
             xc_ref, h_ref, ya_ref, px_ref, ph_ref):
        @pl.when(pl.program_id(1) == 0)
        def _():
            px_ref[...] = jnp.zeros_like(px_ref)
            ph_ref[...] = jnp.zeros_like(ph_ref)

        row = lax.broadcasted_iota(jnp.int32, (tm, W), 0)
        row8 = lax.broadcasted_iota(jnp.int32, (8, W), 0)
        x = xr_ref[...]
        xc = _conv_fwd(x, px_ref[...], cw_ref, cb_ref[...], row, row8, tm)
        sp, _ = _softplus_neg(lam_ref[...])
        _, r, i, a, mult = _lru_gates(xc, wrg_ref, brg_ref[...], wig_ref, big_ref[...], sp)
        bterm = mult * (i * xc)
        acum, hloc = _scan_down(a, bterm, row, tm)
        h = hloc + acum * ph_ref[7:8, :]
        h_ref[...] = h
        xc_ref[...] = xc
        gelu, _ = _gelu_and_grad(gr_ref[...])
        ya_ref[...] = (h * gelu).astype(BF16)
        px_ref[...] = xr_ref[tm - 8:tm, :]
        ph_ref[...] = h_ref[tm - 8:tm, :]

    tile = pl.BlockSpec((tm, W), lambda s, t: (s * nt + t, 0))
    return _call(
        body, phases=phases, name="rnn_fwd", grid=(n_seq, nt),
        in_specs=[tile, tile, _const((CONV_W, W)), _const((1, W)), _const((8, LANES, LANES)), _const((1, W)),
                  _const((8, LANES, LANES)), _const((1, W)), _const((1, W))],
        out_specs=[tile, tile, tile],
        out_shape=[jax.ShapeDtypeStruct((T, W), F32), jax.ShapeDtypeStruct((T, W), F32),
                   jax.ShapeDtypeStruct((T, W), BF16)],
        scratch_shapes=[pltpu.VMEM((8, W), F32), pltpu.VMEM((8, W), F32)],
    )(xr, gr, conv_w, conv_b, wrg2, b_rg, wig2, b_ig, lam)


def _rnn_bwd(dya, xr, gr, xc, h, conv_w, wrg2, b_rg, wig2, b_ig, lam, n_seq, S, tm, phases=()):
    T = xr.shape[0]
    nt = S // tm
    W = D_MODEL
    nb8 = tm // 8

    def body(dya_ref, xr_ref, gr_ref, xc_ref, h_ref, xprev_ref, hprev_ref, cw_ref, wrg_ref, brg_ref, wig_ref,
             big_ref, lam_ref, dxr_ref, dgr_ref, vec_ref, dwrg_ref, dwig_ref, cg_ref, ndxc_ref, tmp_ref):
        s, ti = pl.program_id(0), pl.program_id(1)

        @pl.when((s == 0) & (ti == 0))
        def _():
            vec_ref[...] = jnp.zeros_like(vec_ref)
            dwrg_ref[...] = jnp.zeros_like(dwrg_ref)
            dwig_ref[...] = jnp.zeros_like(dwig_ref)

        @pl.when(ti == 0)
        def _():
            cg_ref[...] = jnp.zeros_like(cg_ref)
            ndxc_ref[...] = jnp.zeros_like(ndxc_ref)

        first = ti == nt - 1
        row = lax.broadcasted_iota(jnp.int32, (tm, W), 0)
        row8 = lax.broadcasted_iota(jnp.int32, (8, W), 0)
        x = xr_ref[...]
        xc = xc_ref[...]
        hv = h_ref[...]
        xprev = jnp.where(first, 0.0, xprev_ref[...])
        hprev = jnp.where(first, 0.0, hprev_ref[...])
        sp, dsp_dlam = _softplus_neg(lam_ref[...])
        xcb, r, i, a, mult = _lru_gates(xc, wrg_ref, brg_ref[...], wig_ref, big_ref[...], sp)

        gelu, dgelu = _gelu_and_grad(gr_ref[...])
        dya_v = dya_ref[...]
        dgr_ref[...] = (dya_v * hv * dgelu).astype(BF16)
        dh = dya_v * gelu
        c = jnp.where(row < tm - 1, pltpu.roll(a, tm - 1, 0), 1.0)
        ccum, gloc = _scan_up(c, dh, row, tm)
        G = gloc + ccum * cg_ref[0:1, :]
        tmp_ref[...] = a * G
        cg_ref[...] = tmp_ref[0:8, :]

        h_m1 = _shift_down(hv, hprev, 1, row, row8, tm)
        ixc = i * xc
        dixc = G * mult
        dlog_a = (G * h_m1) * a - (G * ixc) * (a * a / mult)
        dr = dlog_a * ((-LRU_C) * sp)
        di = dixc * xc
        drg = dr * r * (1.0 - r)
        dig = di * i * (1.0 - i)
        vec_ref[7:8, :] += jnp.sum(dlog_a * ((-LRU_C) * r), axis=0, keepdims=True) * dsp_dlam
        vec_ref[5:6, :] += jnp.sum(drg, axis=0, keepdims=True)
        vec_ref[6:7, :] += jnp.sum(dig, axis=0, keepdims=True)
        drgb = drg.astype(BF16)
        digb = dig.astype(BF16)
        dxc = dixc * i + _blockdiag_dot(drgb, wrg_ref, True) + _blockdiag_dot(digb, wig_ref, True)
        for b in range(W // LANES):
            sl = slice(b * LANES, (b + 1) * LANES)
            dwrg_ref[b] += _dot_tn(xcb[:, sl], drgb[:, sl])
            dwig_ref[b] += _dot_tn(xcb[:, sl], digb[:, sl])

        vec_ref[4:5, :] += jnp.sum(dxc, axis=0, keepdims=True)
        vec_ref[3:4, :] += jnp.sum(dxc * x, axis=0, keepdims=True)
        dxr = cw_ref[CONV_W - 1:CONV_W, :] * dxc
        nxt = ndxc_ref[...]
        for sft in range(1, CONV_W):
            j = CONV_W - 1 - sft
            vec_ref[j:j + 1, :] += jnp.sum(dxc * _shift_down(x, xprev, sft, row, row8, tm), axis=0, keepdims=True)
            dxr = dxr + cw_ref[j:j + 1, :] * _shift_up(dxc, nxt, sft, row8, tm)
        dxr_ref[...] = dxr.astype(BF16)
        tmp_ref[...] = dxc
        ndxc_ref[...] = tmp_ref[0:8, :]

    rev = lambda s, t: (s * nt + nt - 1 - t, 0)
    tile = pl.BlockSpec((tm, W), rev)
    prev8 = pl.BlockSpec((8, W), lambda s, t: (jnp.maximum((s * nt + nt - 1 - t) * nb8 - 1, 0), 0))
    return _call(
        body, phases=phases, name="rnn_bwd", grid=(n_seq, nt),
        in_specs=[tile, tile, tile, tile, tile, prev8, prev8, _const((CONV_W, W)), _const((8, LANES, LANES)),
                  _const((1, W)), _const((8, LANES, LANES)), _const((1, W)), _const((1, W))],
        out_specs=[tile, tile, _const((16, W)), _const((8, LANES, LANES)), _const((8, LANES, LANES))],
        out_shape=[jax.ShapeDtypeStruct((T, W), BF16), jax.ShapeDtypeStruct((T, W), BF16),
                   jax.ShapeDtypeStruct((16, W), F32), jax.ShapeDtypeStruct((8, LANES, LANES), F32),
                   jax.ShapeDtypeStruct((8, LANES, LANES), F32)],
        scratch_shapes=[pltpu.VMEM((8, W), F32), pltpu.VMEM((8, W), F32), pltpu.VMEM((tm, W), F32)],
    )(dya, xr, gr, xc, h, xr, h, conv_w, wrg2, b_rg, wig2, b_ig, lam)


def _head_swap(t, lane):
    w = t.shape[1]
    return jnp.where(lane % HEAD_DIM < HEAD_DIM // 2, pltpu.roll(t, w - HEAD_DIM // 2, 1),
                     pltpu.roll(t, HEAD_DIM // 2, 1))


def _qk_prep(t, gain, cosf, sins, ind, indt, lane):
    ms = _split_dot(t * t, ind) * (1.0 / HEAD_DIM)
    rstd = _split_dot(lax.rsqrt(ms + NORM_EPS), indt)
    tn = (t * rstd) * gain
    return tn * cosf + _head_swap(tn, lane) * sins, rstd


def _qk_prep_bwd(dy, t, rstd, gain, cosf, sins, ind, indt, lane):
    dtn = dy * cosf + _head_swap(dy * sins, lane)
    dgain = jnp.sum(dtn * (t * rstd), axis=0, keepdims=True)
    dn = dtn * gain
    m = _split_dot(_split_dot(dn * t, ind), indt) * (1.0 / HEAD_DIM)
    return rstd * dn - t * (rstd * rstd * rstd * m), dgain


def _attn_mask(blk_idx):
    qi = lax.broadcasted_iota(jnp.int32, (WINDOW, 2 * WINDOW), 0)
    ci = lax.broadcasted_iota(jnp.int32, (WINDOW, 2 * WINDOW), 1)
    diff = WINDOW + qi - ci
    return (diff >= 0) & (diff < WINDOW) & ((ci >= WINDOW) | (blk_idx > 0))


def _pair_operands(kc, vc, lane128):
    out = []
    for m in range(KV_W // LANES):
        k2 = kc[:, m * LANES:(m + 1) * LANES]
        v2 = vc[:, m * LANES:(m + 1) * LANES]
        out.append((k2.astype(BF16), pltpu.roll(k2, HEAD_DIM, 1).astype(BF16),
                    v2.astype(BF16), pltpu.roll(v2, HEAD_DIM, 1).astype(BF16)))
    return out


def _softmax_sink(s, mask, sink):
    s = jnp.where(mask, s, -1e30)
    mx = jnp.maximum(jnp.max(s, axis=-1, keepdims=True), sink)
    e = jnp.where(mask, jnp.exp(s - mx), 0.0)
    es = jnp.exp(sink - mx)
    inv = 1.0 / (jnp.sum(e, axis=-1, keepdims=True) + es)
    return e * inv, es * inv


def _attn_fwd(q, k, v, qg, kg, sinks, cosf, sins, ind_q, ind_qt, ind_k, ind_kt, n_seq, S, phases=()):
    T = q.shape[0]
    nblk = S // WINDOW
    W = D_MODEL

    def body(sink_ref, q_ref, k_ref, v_ref, qg_ref, kg_ref, cos_ref, sin_ref, iq_ref, iqt_ref, ik_ref, ikt_ref,
             o_ref, kc_ref, vc_ref):
        n = pl.program_id(1)

        @pl.when(n == 0)
        def _():
            kc_ref[...] = jnp.zeros_like(kc_ref)
            vc_ref[...] = jnp.zeros_like(vc_ref)

        lane = lax.broadcasted_iota(jnp.int32, (WINDOW, W), 1)
        lane_k = lane[:, :KV_W]
        lane128 = lane[:, :LANES]
        cosf, sinv = cos_ref[...], sin_ref[...]
        qr, _ = _qk_prep(q_ref[...], qg_ref[...], cosf, sinv, iq_ref[...], iqt_ref[...], lane)
        kr, _ = _qk_prep(k_ref[...], kg_ref[...], cosf[:, :KV_W], sinv[:, :KV_W], ik_ref[...], ikt_ref[...], lane_k)
        kc_ref[WINDOW:2 * WINDOW, :] = kr
        vc_ref[WINDOW:2 * WINDOW, :] = v_ref[...]
        ops = _pair_operands(kc_ref[...], vc_ref[...], lane128)
        mask = _attn_mask(n)
        lo = lane128 < HEAD_DIM
        scale = HEAD_DIM ** -0.5
        for i in range(N_HEADS // 2):
            kvh = i // 2
            k2, k2r, v2, v2r = ops[kvh // 2]
            if kvh % 2:
                k2, k2r, v2, v2r = k2r, k2, v2r, v2
            qp = qr[:, i * LANES:(i + 1) * LANES]
            q_lo = jnp.where(lo, qp, 0.0).astype(BF16)
            q_hi = jnp.where(lo, 0.0, qp).astype(BF16)
            p_lo, _ = _softmax_sink(_dot_nt(q_lo, k2) * scale, mask, sink_ref[2 * i])
            p_hi, _ = _softmax_sink(_dot_nt(q_hi, k2r) * scale, mask, sink_ref[2 * i + 1])
            o_pair = jnp.where(lo, _dot(p_lo.astype(BF16), v2), _dot(p_hi.astype(BF16), v2r))
            o_ref[:, i * LANES:(i + 1) * LANES] = o_pair.astype(BF16)
        kc_ref[0:WINDOW, :] = kc_ref[WINDOW:2 * WINDOW, :]
        vc_ref[0:WINDOW, :] = vc_ref[WINDOW:2 * WINDOW, :]

    blk = lambda w: pl.BlockSpec((WINDOW, w), lambda s, n: (s * nblk + n, 0))
    pos = pl.BlockSpec((WINDOW, W), lambda s, n: (n, 0))
    outs, extra = _call(
        body, phases=phases, name="attn_fwd", grid=(n_seq, nblk),
        in_specs=[pl.BlockSpec(memory_space=pltpu.SMEM), blk(W), blk(KV_W), blk(KV_W), _const((1, W)),
                  _const((1, KV_W)), pos, pos, _const((W, LANES)), _const((LANES, W)), _const((KV_W, LANES)),
                  _const((LANES, KV_W))],
        out_specs=blk(W), out_shape=jax.ShapeDtypeStruct((T, W), BF16),
        scratch_shapes=[pltpu.VMEM((2 * WINDOW, KV_W), F32), pltpu.VMEM((2 * WINDOW, KV_W), F32)],
    )(sinks, q, k, v, qg, kg, cosf, sins, ind_q, ind_qt, ind_k, ind_kt)
    return outs[0], extra


def _attn_bwd(do, q, k, v, qg, kg, sinks, cosf, sins, ind_q, ind_qt, ind_k, ind_kt, n_seq, S, phases=()):
    T = q.shape[0]
    nblk = S // WINDOW
    W = D_MODEL

    def body(sink_ref, do_ref, q_ref, k_ref, v_ref, qg_ref, kg_ref, cos_ref, sin_ref, iq_ref, iqt_ref, ik_ref,
             ikt_ref, dq_ref, dkc_ref, dkp_ref, dvc_ref, dvp_ref, dqg_ref, dsk_ref, kc_ref, vc_ref, dqr_ref,
             dk_ref, dv_ref):
        s_id, n = pl.program_id(0), pl.program_id(1)

        @pl.when((s_id == 0) & (n == 0))
        def _():
            dqg_ref[...] = jnp.zeros_like(dqg_ref)
            dsk_ref[...] = jnp.zeros_like(dsk_ref)

        @pl.when(n == 0)
        def _():
            kc_ref[...] = jnp.zeros_like(kc_ref)
            vc_ref[...] = jnp.zeros_like(vc_ref)

        lane = lax.broadcasted_iota(jnp.int32, (WINDOW, W), 1)
        lane_k = lane[:, :KV_W]
        lane128 = lane[:, :LANES]
        cosf, sinv = cos_ref[...], sin_ref[...]
        qv = q_ref[...]
        qr, q_rstd = _qk_prep(qv, qg_ref[...], cosf, sinv, iq_ref[...], iqt_ref[...], lane)
        kr, _ = _qk_prep(k_ref[...], kg_ref[...], cosf[:, :KV_W], sinv[:, :KV_W], ik_ref[...], ikt_ref[...], lane_k)
        kc_ref[WINDOW:2 * WINDOW, :] = kr
        vc_ref[WINDOW:2 * WINDOW, :] = v_ref[...]
        ops = _pair_operands(kc_ref[...], vc_ref[...], lane128)
        mask = _attn_mask(n)
        lo = lane128 < HEAD_DIM
        lo2 = lax.broadcasted_iota(jnp.int32, (2 * WINDOW, LANES), 1) < HEAD_DIM
        scale = HEAD_DIM ** -0.5
        dk_ref[...] = jnp.zeros_like(dk_ref)
        dv_ref[...] = jnp.zeros_like(dv_ref)
        dsk = jnp.zeros((WINDOW, LANES), F32)
        for i in range(N_HEADS // 2):
            kvh = i // 2
            m = kvh // 2
            k2, k2r, v2, v2r = ops[m]
            if kvh % 2:
                k2, k2r, v2, v2r = k2r, k2, v2r, v2
            qp = qr[:, i * LANES:(i + 1) * LANES]
            dop = do_ref[:, i * LANES:(i + 1) * LANES]
            dq_pair = None
            for half in range(2):
                sel = lo if half == 0 else ~lo
                kk, vv = (k2, v2) if half == 0 else (k2r, v2r)
                qh = jnp.where(sel, qp, 0.0).astype(BF16)
                doh = jnp.where(sel, dop, 0.0).astype(BF16)
                p, ps = _softmax_sink(_dot_nt(qh, kk) * scale, mask, sink_ref[2 * i + half])
                dp = _dot_nt(doh, vv)
                dd = jnp.sum(p * dp, axis=-1, keepdims=True)
                ds = (p * (dp - dd) * scale).astype(BF16)
                dsk = dsk + jnp.where(lane128 == 2 * i + half, -(ps * dd), 0.0)
                dq_h = _dot(ds, kk)
                dq_pair = dq_h if half == 0 else jnp.where(lo, dq_pair, dq_h)
                dk_h = _dot_tn(ds, qh)
                dv_h = _dot_tn(p.astype(BF16), doh)
                own_lo = (kvh % 2 == 0)
                if (half == 0) != own_lo:
                    dk_h = pltpu.roll(dk_h, HEAD_DIM, 1)
                    dv_h = pltpu.roll(dv_h, HEAD_DIM, 1)
                dk_ref[:, m * LANES:(m + 1) * LANES] += dk_h
                dv_ref[:, m * LANES:(m + 1) * LANES] += dv_h
            dqr_ref[:, i * LANES:(i + 1) * LANES] = dq_pair
        dsk_ref[...] += dsk
        dq, dqg = _qk_prep_bwd(dqr_ref[...], qv, q_rstd, qg_ref[...], cosf, sinv, iq_ref[...], iqt_ref[...], lane)
        dq_ref[...] = dq.astype(BF16)
        dqg_ref[...] += dqg
        dkp_ref[...] = dk_ref[0:WINDOW, :]
        dkc_ref[...] = dk_ref[WINDOW:2 * WINDOW, :]
        dvp_ref[...] = dv_ref[0:WINDOW, :]
        dvc_ref[...] = dv_ref[WINDOW:2 * WINDOW, :]
        kc_ref[0:WINDOW, :] = kc_ref[WINDOW:2 * WINDOW, :]
        vc_ref[0:WINDOW, :] = vc_ref[WINDOW:2 * WINDOW, :]

    blk = lambda w: pl.BlockSpec((WINDOW, w), lambda s, n: (s * nblk + n, 0))
    pos = pl.BlockSpec((WINDOW, W), lambda s, n: (n, 0))
    kv_out = jax.ShapeDtypeStruct((T, KV_W), F32)
    return _call(
        body, phases=phases, name="attn_bwd", grid=(n_seq, nblk),
        in_specs=[pl.BlockSpec(memory_space=pltpu.SMEM), blk(W), blk(W), blk(KV_W), blk(KV_W), _const((1, W)),
                  _const((1, KV_W)), pos, pos, _const((W, LANES)), _const((LANES, W)), _const((KV_W, LANES)),
                  _const((LANES, KV_W))],
        out_specs=[blk(W), blk(KV_W), blk(KV_W), blk(KV_W), blk(KV_W), _const((1, W)), _const((WINDOW, LANES))],
        out_shape=[jax.ShapeDtypeStruct((T, W), BF16), kv_out, kv_out, kv_out, kv_out,
                   jax.ShapeDtypeStruct((1, W), F32), jax.ShapeDtypeStruct((WINDOW, LANES), F32)],
        scratch_shapes=[pltpu.VMEM((2 * WINDOW, KV_W), F32), pltpu.VMEM((2 * WINDOW, KV_W), F32),
                        pltpu.VMEM((WINDOW, W), F32), pltpu.VMEM((2 * WINDOW, KV_W), F32),
                        pltpu.VMEM((2 * WINDOW, KV_W), F32)],
    )(sinks, do, q, k, v, qg, kg, cosf, sins, ind_q, ind_qt, ind_k, ind_kt)


def _kv_bwd(dkc, dkp, dvc, dvp, k, kg, cosf, sins, ind_k, ind_kt, n_seq, S, phases=()):
    T = k.shape[0]
    nblk = S // WINDOW

    def body(dkc_ref, dkp_ref, dvc_ref, dvp_ref, k_ref, kg_ref, cos_ref, sin_ref, ik_ref, ikt_ref,
             dk_ref, dv_ref, dkg_ref):
        s_id, n = pl.program_id(0), pl.program_id(1)

        @pl.when((s_id == 0) & (n == 0))
        def _():
            dkg_ref[...] = jnp.zeros_like(dkg_ref)

        has_next = n < nblk - 1
        lane = lax.broadcasted_iota(jnp.int32, (WINDOW, KV_W), 1)
        dkr = dkc_ref[...] + jnp.where(has_next, dkp_ref[...], 0.0)
        dv_ref[...] = (dvc_ref[...] + jnp.where(has_next, dvp_ref[...], 0.0)).astype(BF16)
        cosf, sinv = cos_ref[:, :KV_W], sin_ref[:, :KV_W]
        kv = k_ref[...]
        _, rstd = _qk_prep(kv, kg_ref[...], cosf, sinv, ik_ref[...], ikt_ref[...], lane)
        dk, dkg = _qk_prep_bwd(dkr, kv, rstd, kg_ref[...], cosf, sinv, ik_ref[...], ikt_ref[...], lane)
        dk_ref[...] = dk.astype(BF16)
        dkg_ref[...] += dkg

    cur = pl.BlockSpec((WINDOW, KV_W), lambda s, n: (s * nblk + n, 0))
    nxt = pl.BlockSpec((WINDOW, KV_W), lambda s, n: (s * nblk + jnp.minimum(n + 1, nblk - 1), 0))
    pos = pl.BlockSpec((WINDOW, D_MODEL), lambda s, n: (n, 0))
    return _call(
        body, phases=phases, name="kv_bwd", grid=(n_seq, nblk),
        in_specs=[cur, nxt, cur, nxt, cur, _const((1, KV_W)), pos, pos, _const((KV_W, LANES)),
                  _const((LANES, KV_W))],
        out_specs=[cur, cur, _const((1, KV_W))],
        out_shape=[jax.ShapeDtypeStruct((T, KV_W), BF16), jax.ShapeDtypeStruct((T, KV_W), BF16),
                   jax.ShapeDtypeStruct((1, KV_W), F32)],
    )(dkc, dkp, dvc, dvp, k, kg, cosf, sins, ind_k, ind_kt)


def _merge_fwd(x, ya, o, ga, gb, w_rnn, w_attn, w_out, tm, phases=()):
    T = x.shape[0]
    W = D_MODEL

    def body(x_ref, ya_ref, o_ref, ga_ref, gb_ref, wr_ref, wa_ref, wo_ref, x1_ref, mg_ref, yao_ref, ybo_ref):
        y_a = _dot(ya_ref[...], wr_ref[...])
        y_b = _dot(o_ref[...], wa_ref[...])
        yao_ref[...] = y_a
        ybo_ref[...] = y_b
        mg = (_sigmoid(ga_ref[...]) * y_a + _sigmoid(gb_ref[...]) * y_b).astype(BF16)
        mg_ref[...] = mg
        x1_ref[...] = x_ref[...] + _dot(mg, wo_ref[...])

    row = pl.BlockSpec((tm, W), lambda i: (i, 0))
    sq = _const((W, W))
    return _call(
        body, phases=phases, name="merge_fwd", grid=(T // tm,),
        in_specs=[row, row, row, row, row, sq, sq, sq], out_specs=[row, row, row, row],
        out_shape=[jax.ShapeDtypeStruct((T, W), F32), jax.ShapeDtypeStruct((T, W), BF16),
                   jax.ShapeDtypeStruct((T, W), F32), jax.ShapeDtypeStruct((T, W), F32)],
    )(x, ya, o, ga, gb, w_rnn, w_attn, w_out)


def _merge_bwd(dx1, ga, gb, y_a, y_b, w_rnn, w_attn, w_out, tm, phases=()):
    T = dx1.shape[0]
    W = D_MODEL

    def body(dx1_ref, ga_ref, gb_ref, ya_ref, yb_ref, wr_ref, wa_ref, wo_ref,
             dga_ref, dgb_ref, dya_ref, dyb_ref, dyain_ref, do_ref):
        dm = _dot_nt(dx1_ref[...].astype(BF16), wo_ref[...])
        sa = _sigmoid(ga_ref[...])
        sb = _sigmoid(gb_ref[...])
        dga_ref[...] = (dm * ya_ref[...] * (sa * (1.0 - sa))).astype(BF16)
        dgb_ref[...] = (dm * yb_ref[...] * (sb * (1.0 - sb))).astype(BF16)
        dya = (dm * sa).astype(BF16)
        dyb = (dm * sb).astype(BF16)
        dya_ref[...] = dya
        dyb_ref[...] = dyb
        dyain_ref[...] = _dot_nt(dya, wr_ref[...])
        do_ref[...] = _dot_nt(dyb, wa_ref[...])

    row = pl.BlockSpec((tm, W), lambda i: (i, 0))
    sq = _const((W, W))
    b16 = jax.ShapeDtypeStruct((T, W), BF16)
    f32 = jax.ShapeDtypeStruct((T, W), F32)
    return _call(
        body, phases=phases, name="merge_bwd", grid=(T // tm,),
        in_specs=[row, row, row, row, row, sq, sq, sq], out_specs=[row] * 6,
        out_shape=[b16, b16, b16, b16, f32, f32],
    )(dx1, ga, gb, y_a, y_b, w_rnn, w_attn, w_out)


def _mlp_fwd(x1, g_mlp, w_up, w_down, tm, phases=()):
    T = x1.shape[0]
    W = D_MODEL

    def body(x_ref, g_ref, wu_ref, wd_ref, x2_ref, hm_ref, u_ref, act_ref):
        xv = x_ref[...]
        hm, _ = _rms_fwd(xv, g_ref[...])
        hmb = hm.astype(BF16)
        hm_ref[...] = hmb
        for j in range(N_CHIPS):
            u = _dot(hmb, wu_ref[j])
            u_ref[:, j * W:(j + 1) * W] = u
            ru = jnp.maximum(u, 0.0)
            act_ref[:, j * W:(j + 1) * W] = (ru * ru).astype(BF16)
        x2_ref[...] = xv + _dot(act_ref[...], wd_ref[...])

    row = lambda w: pl.BlockSpec((tm, w), lambda i: (i, 0))
    return _call(
        body, phases=phases, name="mlp_fwd", grid=(T // tm,),
        in_specs=[row(W), _const((1, W)), _const((N_CHIPS, W, W)), _const((D_FF, W))],
        out_specs=[row(W), row(W), row(D_FF), row(D_FF)],
        out_shape=[jax.ShapeDtypeStruct((T, W), F32), jax.ShapeDtypeStruct((T, W), BF16),
                   jax.ShapeDtypeStruct((T, D_FF), F32), jax.ShapeDtypeStruct((T, D_FF), BF16)],
    )(x1, g_mlp, w_up, w_down)


def _mlp_bwd(dx2, u, x1, g_mlp, w_up, w_down, tm, phases=()):
    T = x1.shape[0]
    W = D_MODEL

    def body(dx2_ref, u_ref, x_ref, g_ref, wu_ref, wd_ref, dx1_ref, du_ref, dg_ref):
        @pl.when(pl.program_id(0) == 0)
        def _():
            dg_ref[...] = jnp.zeros_like(dg_ref)

        dx2 = dx2_ref[...]
        dact = _dot_nt(dx2.astype(BF16), wd_ref[...])
        du_ref[...] = (dact * (2.0 * jnp.maximum(u_ref[...], 0.0))).astype(BF16)
        dhm = jnp.zeros((tm, W), F32)
        for j in range(N_CHIPS):
            dhm = dhm + _dot_nt(du_ref[:, j * W:(j + 1) * W], wu_ref[j])
        xv = x_ref[...]
        g = g_ref[...]
        _, r = _rms_fwd(xv, g)
        dx, dg = _rms_bwd(dhm, xv, r, g)
        dx1_ref[...] = dx2 + dx
        dg_ref[...] += dg

    row = lambda w: pl.BlockSpec((tm, w), lambda i: (i, 0))
    return _call(
        body, phases=phases, name="mlp_bwd", grid=(T // tm,),
        in_specs=[row(W), row(D_FF), row(W), _const((1, W)), _const((N_CHIPS, W, W)), _const((D_FF, W))],
        out_specs=[row(W), row(D_FF), _const((1, W))],
        out_shape=[jax.ShapeDtypeStruct((T, W), F32), jax.ShapeDtypeStruct((T, D_FF), BF16),
                   jax.ShapeDtypeStruct((1, W), F32)],
    )(dx2, u, x1, g_mlp, w_up, w_down)


def _ple_loss(x2, p, target, g_ple, w_gate, w_proj, tm, phases=()):
    T = x2.shape[0]
    W = D_MODEL
    cw = W // N_CHIPS

    def body(x_ref, p_ref, t_ref, g_ref, wg_ref, wp_ref, loss_ref, dx2_ref, pb_ref, de_ref, hp_ref, dtg_ref, dg_ref):
        @pl.when(pl.program_id(0) == 0)
        def _():
            dg_ref[...] = jnp.zeros_like(dg_ref)
            loss_ref[...] = jnp.zeros_like(loss_ref)

        xv = x_ref[...]
        g = g_ref[...]
        pb = p_ref[...].astype(BF16)
        pb_ref[...] = pb
        e = jnp.concatenate([_dot(pb, wp_ref[j]) for j in range(N_CHIPS)], axis=1)
        hp, r = _rms_fwd(xv, g)
        hpb = hp.astype(BF16)
        hp_ref[...] = hpb
        sg = _sigmoid(_dot(hpb, wg_ref[...]))
        diff = (xv + e * sg) - t_ref[...]
        loss_ref[...] += jnp.sum(diff * diff) * (0.5 / W)
        dx3 = diff * (1.0 / W)
        de_ref[...] = (dx3 * sg).astype(BF16)
        dtg = (dx3 * e * (sg * (1.0 - sg))).astype(BF16)
        dtg_ref[...] = dtg
        dx, dg = _rms_bwd(_dot_nt(dtg, wg_ref[...]), xv, r, g)
        dx2_ref[...] = dx3 + dx
        dg_ref[...] += dg

    row = lambda w: pl.BlockSpec((tm, w), lambda i: (i, 0))
    b16 = lambda w: jax.ShapeDtypeStruct((T, w), BF16)
    return _call(
        body, phases=phases, name="ple_loss", grid=(T // tm,),
        in_specs=[row(W), row(PLE_DIM), row(W), _const((1, W)), _const((W, W)), _const((N_CHIPS, PLE_DIM, cw))],
        out_specs=[_const((8, LANES)), row(W), row(PLE_DIM), row(W), row(W), row(W), _const((1, W))],
        out_shape=[jax.ShapeDtypeStruct((8, LANES), F32), jax.ShapeDtypeStruct((T, W), F32), b16(PLE_DIM),
                   b16(W), b16(W), b16(W), jax.ShapeDtypeStruct((1, W), F32)],
    )(x2, p, target, g_ple, w_gate, w_proj)


def _adamw(w, g, m, v, name, tr, phases=()):
    R, C = w.shape
    c1 = 1.0 / (1.0 - ADAM_B1 ** ADAM_STEP)
    c2 = 1.0 / (1.0 - ADAM_B2 ** ADAM_STEP)

    def body(w_ref, g_ref, m_ref, v_ref, d_ref, nm_ref, nv_ref):
        gv = g_ref[...]
        nm = ADAM_B1 * m_ref[...] + (1.0 - ADAM_B1) * gv
        nv = ADAM_B2 * v_ref[...] + (1.0 - ADAM_B2) * (gv * gv)
        nm_ref[...] = nm
        nv_ref[...] = nv
        d_ref[...] = (-ADAM_LR) * ((nm * c1) / (jnp.sqrt(nv * c2) + ADAM_EPS) + ADAM_WD * w_ref[...])

    row = pl.BlockSpec((tr, C), lambda i: (i, 0))
    sds = jax.ShapeDtypeStruct((R, C), F32)
    return _call(
        body, phases=phases, name=name, grid=(R // tr,), in_specs=[row] * 4, out_specs=[row] * 3,
        out_shape=[sds] * 3,
    )(w, g, m, v)


def _indicator(width):
    ind = np.zeros((width, LANES), np.float32)
    ind[np.arange(width), np.arange(width) // HEAD_DIM] = 1.0
    return jnp.asarray(ind, BF16), jnp.asarray(ind.T, BF16)


def _rope_tables(S):
    inv = ROPE_THETA ** (-jnp.arange(0, HEAD_DIM, 2, dtype=F32) / HEAD_DIM)
    ang = jnp.arange(S, dtype=F32)[:, None] * inv[None, :]
    cos, sin = jnp.cos(ang), jnp.sin(ang)
    cosf = jnp.tile(jnp.concatenate([cos, cos], axis=1), (1, N_HEADS))
    sins = jnp.tile(jnp.concatenate([-sin, sin], axis=1), (1, N_HEADS))
    return cosf, sins


def _pair_blockdiag(w):
    w4 = w.reshape(8, 2, HEAD_DIM, HEAD_DIM)
    eye = jnp.eye(2, dtype=w.dtype)
    return jnp.einsum("bpij,pq->bpiqj", w4, eye).reshape(8, LANES, LANES)


def _pair_blockdiag_extract(g):
    g5 = g.reshape(8, 2, HEAD_DIM, 2, HEAD_DIM)
    return jnp.stack([g5[:, 0, :, 0, :], g5[:, 1, :, 1, :]], axis=1).reshape(16, HEAD_DIM, HEAD_DIM)


def _pair_sum(parts, sibs, name):
    n = len(parts)
    dims = [(p.shape[1] // 2, p.shape[2]) for p in parts]

    def body(*refs):
        p_r, s_r, send_r, own_r, mine_r, sem = (refs[0:n], refs[n:2 * n], refs[2 * n:3 * n], refs[3 * n:4 * n],
                                                refs[4 * n:5 * n], refs[5 * n])
        x, y, c, chips = _mesh_pos()
        me = 2 * x + y
        loads = []
        for i, (R, _) in enumerate(dims):
            mine, _ = _half_rows(c, R)
            cp = pltpu.make_async_copy(p_r[i].at[:, mine, :], mine_r[i], sem.at[i])
            cp.start()
            loads.append(cp)
        for i in range(n):
            loads[i].wait()
            for j, (cx, cy) in enumerate(chips):
                k = 2 * cx + cy
                send_r[i][j] = (mine_r[i][k] + s_r[i][k]).astype(BF16)
            own_r[i][...] = mine_r[i][me] + s_r[i][me]

    vm = pl.BlockSpec(memory_space=pltpu.VMEM)
    out = pl.pallas_call(
        body, name=name, in_specs=[pl.BlockSpec(memory_space=pl.ANY)] * n + [vm] * n, out_specs=[vm] * (2 * n),
        out_shape=[jax.ShapeDtypeStruct((3, R, C), BF16) for R, C in dims]
        + [jax.ShapeDtypeStruct((R, C), F32) for R, C in dims],
        scratch_shapes=[pltpu.VMEM((N_CHIPS, R, C), F32) for R, C in dims] + [pltpu.SemaphoreType.DMA((n,))],
        compiler_params=pltpu.CompilerParams(vmem_limit_bytes=VMEM_LIMIT),
    )(*parts, *sibs)
    return out[:n], out[n:]


def _chip_sum(owns, recvs, name):
    n = len(owns)
    dims = [o.shape for o in owns]

    def body(*refs):
        own_r, recv_r, red_r, stage_r, sem = refs[0:n], refs[n:2 * n], refs[2 * n:3 * n], refs[3 * n:4 * n], refs[4 * n]
        x, y, c, _ = _mesh_pos()
        me = 2 * x + y
        stores = []
        for i, (R, _) in enumerate(dims):
            acc = None
            for k in range(N_CHIPS):
                term = jnp.where(me == k, own_r[i][...], recv_r[i][_peer_slot(k, x, y)].astype(F32))
                acc = term if acc is None else acc + term
            stage_r[i][...] = acc
            mine, _ = _half_rows(c, R)
            cp = pltpu.make_async_copy(stage_r[i], red_r[i].at[mine, :], sem.at[i])
            cp.start()
            stores.append(cp)
        for cp in stores:
            cp.wait()

    vm = pl.BlockSpec(memory_space=pltpu.VMEM)
    return pl.pallas_call(
        body, name=name, in_specs=[vm] * (2 * n), out_specs=[pl.BlockSpec(memory_space=pl.ANY)] * n,
        out_shape=[jax.ShapeDtypeStruct((2 * R, C), F32) for R, C in dims],
        scratch_shapes=[pltpu.VMEM((R, C), F32) for R, C in dims] + [pltpu.SemaphoreType.DMA((n,))],
        compiler_params=pltpu.CompilerParams(vmem_limit_bytes=VMEM_LIMIT),
    )(*owns, *recvs)


def _gather_bf16(shard, name):
    R2, C = shard.shape
    R = R2 // 2

    def body(s_ref, o_ref, send_sems, recv_sems):
        x, y, c, chips = _mesh_pos()
        me = 2 * x + y
        mine = pl.ds(pl.multiple_of(c * R, R), R)
        theirs = pl.ds(pl.multiple_of((1 - c) * R, R), R)
        o_ref[me] = s_ref[...].astype(BF16)

        def copy(k, chip, rows, to):
            blk = o_ref.at[chip, rows]
            return pltpu.make_async_remote_copy(src_ref=blk, dst_ref=blk, send_sem=send_sems.at[k],
                                                recv_sem=recv_sems.at[k], device_id=to, device_id_type=MESH_ID)

        first = [copy(j, me, mine, (cx, cy, c)) for j, (cx, cy) in enumerate(chips)]
        for cp in first:
            cp.start()
        passed = []
        for j, (cx, cy) in enumerate(chips):
            copy(j, 2 * cx + cy, mine, (x, y, c)).wait_recv()
            cp = copy(3 + j, 2 * cx + cy, mine, (x, y, 1 - c))
            cp.start()
            passed.append(cp)
        for j, (cx, cy) in enumerate(chips):
            copy(3 + j, 2 * cx + cy, theirs, (x, y, c)).wait_recv()
        for cp in first + passed:
            cp.wait_send()

    return pl.pallas_call(
        body, name=name, out_shape=jax.ShapeDtypeStruct((N_CHIPS, R2, C), BF16),
        in_specs=[pl.BlockSpec(memory_space=pltpu.VMEM)], out_specs=pl.BlockSpec(memory_space=pltpu.VMEM),
        scratch_shapes=[pltpu.SemaphoreType.DMA((6,)), pltpu.SemaphoreType.DMA((6,))],
        compiler_params=pltpu.CompilerParams(vmem_limit_bytes=VMEM_LIMIT),
    )(shard)


def _reduce_scatter(partial, name):
    _, R2, C = partial.shape
    R = R2 // 2

    def body(p_ref, o_ref, mine_ref, sib_ref, out_ref, in_ref, loc_sem, pair_sems, send_sems, recv_sems, fin_sems):
        x, y, c, chips = _mesh_pos()
        me = 2 * x + y
        rows_c = pl.ds(pl.multiple_of(c * R, R), R)
        rows_o = pl.ds(pl.multiple_of((1 - c) * R, R), R)
        sibling = (x, y, 1 - c)
        loc = pltpu.make_async_copy(p_ref.at[:, rows_c, :], mine_ref, loc_sem)
        pair = pltpu.make_async_remote_copy(src_ref=p_ref.at[:, rows_o, :], dst_ref=sib_ref, send_sem=pair_sems.at[0],
                                            recv_sem=pair_sems.at[1], device_id=sibling, device_id_type=MESH_ID)
        loc.start()
        pair.start()
        loc.wait()
        pair.wait()
        sends = []
        for j, (cx, cy) in enumerate(chips):
            k = 2 * cx + cy
            out_ref[j] = (mine_ref[k] + sib_ref[k]).astype(BF16)
            cp = pltpu.make_async_remote_copy(src_ref=out_ref.at[j], dst_ref=in_ref.at[j], send_sem=send_sems.at[j],
                                              recv_sem=recv_sems.at[j], device_id=(cx, cy, c), device_id_type=MESH_ID)
            cp.start()
            sends.append(cp)
        own = mine_ref[me] + sib_ref[me]
        for cp in sends:
            cp.wait_recv()
        acc = None
        for k in range(N_CHIPS):
            term = jnp.where(me == k, own, in_ref[_peer_slot(k, x, y)].astype(F32))
            acc = term if acc is None else acc + term
        o_ref[rows_c, :] = acc
        fin = pltpu.make_async_remote_copy(src_ref=o_ref.at[rows_c, :], dst_ref=o_ref.at[rows_c, :],
                                           send_sem=fin_sems.at[0], recv_sem=fin_sems.at[1], device_id=sibling,
                                           device_id_type=MESH_ID)
        fin.start()
        fin.wait_send()
        pltpu.make_async_remote_copy(src_ref=o_ref.at[rows_o, :], dst_ref=o_ref.at[rows_o, :], send_sem=fin_sems.at[0],
                                     recv_sem=fin_sems.at[1], device_id=sibling, device_id_type=MESH_ID).wait_recv()
        for cp in sends:
            cp.wait_send()

    return pl.pallas_call(
        body, name=name, out_shape=jax.ShapeDtypeStruct((R2, C), F32),
        in_specs=[pl.BlockSpec(memory_space=pl.ANY)], out_specs=pl.BlockSpec(memory_space=pltpu.VMEM),
        scratch_shapes=[pltpu.VMEM((N_CHIPS, R, C), F32), pltpu.VMEM((N_CHIPS, R, C), F32),
                        pltpu.VMEM((3, R, C), BF16), pltpu.VMEM((3, R, C), BF16),
                        pltpu.SemaphoreType.DMA, pltpu.SemaphoreType.DMA((2,)), pltpu.SemaphoreType.DMA((3,)),
                        pltpu.SemaphoreType.DMA((3,)), pltpu.SemaphoreType.DMA((2,))],
        compiler_params=pltpu.CompilerParams(vmem_limit_bytes=VMEM_LIMIT),
    )(partial)


def _allreduce_small(buf, name):
    shape = buf.shape

    def body(b_ref, o_ref, sib_ref, pair_ref, in_ref, pair_sems, send_sems, recv_sems):
        x, y, c, chips = _mesh_pos()
        me = 2 * x + y
        pair = pltpu.make_async_remote_copy(src_ref=b_ref, dst_ref=sib_ref, send_sem=pair_sems.at[0],
                                            recv_sem=pair_sems.at[1], device_id=(x, y, 1 - c), device_id_type=MESH_ID)
        pair.start()
        pair.wait()
        pair_ref[...] = b_ref[...] + sib_ref[...]
        sends = []
        for j, (cx, cy) in enumerate(chips):
            cp = pltpu.make_async_remote_copy(src_ref=pair_ref, dst_ref=in_ref.at[j], send_sem=send_sems.at[j],
                                              recv_sem=recv_sems.at[j], device_id=(cx, cy, c), device_id_type=MESH_ID)
            cp.start()
            sends.append(cp)
        for cp in sends:
            cp.wait_recv()
        acc = None
        for k in range(N_CHIPS):
            term = jnp.where(me == k, pair_ref[...], in_ref[_peer_slot(k, x, y)])
            acc = term if acc is None else acc + term
        o_ref[...] = acc
        for cp in sends:
            cp.wait_send()

    return pl.pallas_call(
        body, name=name, out_shape=jax.ShapeDtypeStruct(shape, F32),
        in_specs=[pl.BlockSpec(memory_space=pltpu.VMEM)], out_specs=pl.BlockSpec(memory_space=pltpu.VMEM),
        scratch_shapes=[pltpu.VMEM(shape, F32), pltpu.VMEM(shape, F32), pltpu.VMEM((3,) + shape, F32),
                        pltpu.SemaphoreType.DMA((2,)), pltpu.SemaphoreType.DMA((3,)), pltpu.SemaphoreType.DMA((3,))],
        compiler_params=pltpu.CompilerParams(vmem_limit_bytes=VMEM_LIMIT),
    )(buf)


def _adamw_small(ws, gs, ms, vs):
    n = len(ws)
    c1 = 1.0 / (1.0 - ADAM_B1 ** ADAM_STEP)
    c2 = 1.0 / (1.0 - ADAM_B2 ** ADAM_STEP)

    def body(*refs):
        w_r, g_r, m_r, v_r = refs[0:n], refs[n:2 * n], refs[2 * n:3 * n], refs[3 * n:4 * n]
        d_r, nm_r, nv_r = refs[4 * n:5 * n], refs[5 * n:6 * n], refs[6 * n:7 * n]
        for i in range(n):
            gv = g_r[i][...]
            nm = ADAM_B1 * m_r[i][...] + (1.0 - ADAM_B1) * gv
            nv = ADAM_B2 * v_r[i][...] + (1.0 - ADAM_B2) * (gv * gv)
            nm_r[i][...] = nm
            nv_r[i][...] = nv
            d_r[i][...] = (-ADAM_LR) * ((nm * c1) / (jnp.sqrt(nv * c2) + ADAM_EPS) + ADAM_WD * w_r[i][...])

    vm = pl.BlockSpec(memory_space=pltpu.VMEM)
    sds = [jax.ShapeDtypeStruct(w.shape, F32) for w in ws]
    out = pl.pallas_call(body, name="adamw_small", in_specs=[vm] * (4 * n), out_specs=[vm] * (3 * n),
                         out_shape=sds * 3)(*ws, *gs, *ms, *vs)
    return out[0:n], out[n:2 * n], out[2 * n:3 * n]


_BIG = ("w_in", "w_rnn_proj", "w_attn_proj", "w_out", "w_up", "w_down", "w_ple_gate", "w_ple_proj")
_SMALL = ("g_mix", "conv_w", "conv_b", "w_rg", "b_rg", "w_ig", "b_ig", "lru_lambda", "q_gain", "k_gain", "sinks",
          "g_mlp", "g_ple")
_WEIGHTS = ("g_mix", "w_in", "conv_w", "conv_b", "w_rg", "b_rg", "w_ig", "b_ig", "lru_lambda", "w_rnn_proj",
            "q_gain", "k_gain", "sinks", "w_attn_proj", "w_out", "g_mlp", "w_up", "w_down", "g_ple", "w_ple_gate",
            "w_ple_proj")


def _pad_row(v):
    v = v.reshape(1, -1)
    return jnp.pad(v, ((0, 0), (0, D_MODEL - v.shape[1])))


def kernel(x, p, g_mix, w_in, conv_w, conv_b, w_rg, b_rg, w_ig, b_ig, lru_lambda, w_rnn_proj, q_gain, k_gain, sinks, w_attn_proj, w_out, g_mlp, w_up, w_down, g_ple, w_ple_gate, w_ple_proj, loss_target, m_g_mix, m_w_in, m_conv_w, m_conv_b, m_w_rg, m_b_rg, m_w_ig, m_b_ig, m_lru_lambda, m_w_rnn_proj, m_q_gain, m_k_gain, m_sinks, m_w_attn_proj, m_w_out, m_g_mlp, m_w_up, m_w_down, m_g_ple, m_w_ple_gate, m_w_ple_proj, v_g_mix, v_w_in, v_conv_w, v_conv_b, v_w_rg, v_b_rg, v_w_ig, v_b_ig, v_lru_lambda, v_w_rnn_proj, v_q_gain, v_k_gain, v_sinks, v_w_attn_proj, v_w_out, v_g_mlp, v_w_up, v_w_down, v_g_ple, v_w_ple_gate, v_w_ple_proj):
    w = dict(g_mix=g_mix, w_in=w_in, conv_w=conv_w, conv_b=conv_b, w_rg=w_rg, b_rg=b_rg, w_ig=w_ig, b_ig=b_ig,
             lru_lambda=lru_lambda, w_rnn_proj=w_rnn_proj, q_gain=q_gain, k_gain=k_gain, sinks=sinks,
             w_attn_proj=w_attn_proj, w_out=w_out, g_mlp=g_mlp, w_up=w_up, w_down=w_down, g_ple=g_ple,
             w_ple_gate=w_ple_gate, w_ple_proj=w_ple_proj)
    m = dict(g_mix=m_g_mix, w_in=m_w_in, conv_w=m_conv_w, conv_b=m_conv_b, w_rg=m_w_rg, b_rg=m_b_rg, w_ig=m_w_ig,
             b_ig=m_b_ig, lru_lambda=m_lru_lambda, w_rnn_proj=m_w_rnn_proj, q_gain=m_q_gain, k_gain=m_k_gain,
             sinks=m_sinks, w_attn_proj=m_w_attn_proj, w_out=m_w_out, g_mlp=m_g_mlp, w_up=m_w_up, w_down=m_w_down,
             g_ple=m_g_ple, w_ple_gate=m_w_ple_gate, w_ple_proj=m_w_ple_proj)
    v = dict(g_mix=v_g_mix, w_in=v_w_in, conv_w=v_conv_w, conv_b=v_conv_b, w_rg=v_w_rg, b_rg=v_b_rg, w_ig=v_w_ig,
             b_ig=v_b_ig, lru_lambda=v_lru_lambda, w_rnn_proj=v_w_rnn_proj, q_gain=v_q_gain, k_gain=v_k_gain,
             sinks=v_sinks, w_attn_proj=v_w_attn_proj, w_out=v_w_out, g_mlp=v_g_mlp, w_up=v_w_up, w_down=v_w_down,
             g_ple=v_g_ple, w_ple_gate=v_w_ple_gate, w_ple_proj=v_w_ple_proj)
    n_seq, S, _ = x.shape
    T = n_seq * S
    chip = 2 * lax.axis_index("x") + lax.axis_index("y")

    tm, tm_rnn = 512, 256
    xf, pf, tf = x.reshape(T, D_MODEL), p.reshape(T, PLE_DIM), loss_target.reshape(T, D_MODEL)
    first = lambda outs: [o[0] for o in outs]

    w_in_g = _gather_bf16(w["w_in"][0], "gather_w_in")
    wb = {name: w[name][0].astype(BF16) for name in _BIG if name != "w_in"}
    grp_mix, grp_mlp, grp_ple = ("w_rnn_proj", "w_attn_proj", "w_out"), ("w_up", "w_down"), ("w_ple_gate", "w_ple_proj")

    cw_full = jnp.zeros((8, D_MODEL), F32)
    cw_full = lax.dynamic_update_slice(cw_full, conv_w[0], (0, chip * (D_MODEL // N_CHIPS)))
    cw_full = _allreduce_small(0.5 * cw_full.reshape(64, LANES), "allgather_conv_w").reshape(8, D_MODEL)[0:CONV_W]

    cosf, sins = _rope_tables(S)
    ind_q, ind_qt = _indicator(D_MODEL)
    ind_k, ind_kt = _indicator(KV_W)
    wrg2 = _pair_blockdiag(w_rg[0]).astype(BF16)
    wig2 = _pair_blockdiag(w_ig[0]).astype(BF16)
    qg = jnp.tile(q_gain, (1, N_HEADS))
    kg = jnp.tile(k_gain, (1, N_KV))
    sk = sinks.reshape(N_HEADS)
    rnn_w = (cw_full, conv_b, wrg2, b_rg, wig2, b_ig, lru_lambda)
    attn_c = (qg, kg, sk, cosf, sins, ind_q, ind_qt, ind_k, ind_kt, n_seq, S)

    (h0, xr, gr, zq, zk, zv, ga, gb), ph = _inproj_fwd(xf, g_mix, w_in_g, tm,
                                                     phases=[_ph_gather_send(wb[n]) for n in grp_mix])
    g_mixw = first(ph)
    (xc, h, ya), ph = _rnn_fwd(xr, gr, *rnn_w, n_seq, S, tm_rnn,
                               phases=[_ph_gather_pass(g) for g in g_mixw] + [_ph_gather_send(wb[n]) for n in grp_mlp])
    g_mixw, g_mlpw = first(ph[:3]), first(ph[3:])
    o, ph = _attn_fwd(zq, zk, zv, *attn_c,
                      phases=[_ph_gather_pass(g) for g in g_mlpw] + [_ph_gather_send(wb[n]) for n in grp_ple])
    g_mlpw, g_plew = first(ph[:2]), first(ph[2:])
    wr, wa, wo = (g.reshape(D_MODEL, D_MODEL) for g in g_mixw)
    wu, wd = g_mlpw[0], g_mlpw[1].reshape(D_FF, D_MODEL)
    (x1, merged, y_a, y_b), ph = _merge_fwd(xf, ya, o, ga, gb, wr, wa, wo, tm,
                                            phases=[_ph_gather_pass(g) for g in g_plew])
    wpg, wpp = first(ph)
    wpg = wpg.reshape(D_MODEL, D_MODEL)
    (x2, hm, u, act), _ = _mlp_fwd(x1, g_mlp, wu, wd, tm // 2)
    (loss_t, dx2, pb, de, hp, dtg, dg_ple), _ = _ple_loss(x2, pf, tf, g_ple, wpg, wpp, tm)
    loss = lax.psum(loss_t[0, 0], ("x", "y", "c"))

    chipmajor = lambda g: g.reshape(N_CHIPS, g.shape[-2] // N_CHIPS, g.shape[-1]) if g.ndim == 2 else g
    part_ple = [chipmajor(_wgrad(hp, dtg, "wgrad_ple_gate", False, D_MODEL, tm)[0]),
                _wgrad(pb, de, "wgrad_ple_proj", True, D_MODEL // N_CHIPS, tm)[0]]
    (dx1, du, dg_mlp), ph = _mlp_bwd(dx2, u, x1, g_mlp, wu, wd, tm // 2, phases=[_ph_pair_send(g) for g in part_ple])
    send_ple, own_ple = _pair_sum(part_ple, first(ph), "pair_sum_ple")
    dw_down, ph = _wgrad(act, dx2, "wgrad_down", False, D_MODEL // 2, tm, phases=[_ph_chip_send(s) for s in send_ple])
    red_ple = _chip_sum(own_ple, first(ph), "chip_sum_ple")
    part_mlp = [_wgrad(hm, du, "wgrad_up", True, D_MODEL, tm)[0], chipmajor(dw_down)]
    (dga, dgb, dya, dyb, dyain, do), ph = _merge_bwd(
        dx1, ga, gb, y_a, y_b, wr, wa, wo, tm,
        phases=[_ph_half_swap(r) for r in red_ple] + [_ph_pair_send(g) for g in part_mlp])
    red_ple = first(ph[:2])
    send_mlp, own_mlp = _pair_sum(part_mlp, first(ph[2:]), "pair_sum_mlp")
    part_mix = [chipmajor(_wgrad(ya, dya, "wgrad_rnn_proj", False, D_MODEL, tm)[0]),
                chipmajor(_wgrad(o, dyb, "wgrad_attn_proj", False, D_MODEL, tm)[0]),
                chipmajor(_wgrad(merged, dx1, "wgrad_out", False, D_MODEL, tm)[0])]
    (dxr, dgr, vec, dwrg2, dwig2), ph = _rnn_bwd(
        dyain, xr, gr, xc, h, cw_full, wrg2, b_rg, wig2, b_ig, lru_lambda, n_seq, S, tm_rnn,
        phases=[_ph_chip_send(s) for s in send_mlp] + [_ph_pair_send(g) for g in part_mix])
    red_mlp = _chip_sum(own_mlp, first(ph[:2]), "chip_sum_mlp")
    send_mix, own_mix = _pair_sum(part_mix, first(ph[2:]), "pair_sum_mix")
    (dq, dkc, dkp, dvc, dvp, dqg, dsk), ph = _attn_bwd(
        do, zq, zk, zv, *attn_c, phases=[_ph_half_swap(r) for r in red_mlp] + [_ph_chip_send(s) for s in send_mix])
    red_mlp = first(ph[:2])
    red_mix = _chip_sum(own_mix, first(ph[2:]), "chip_sum_mix")
    (dk, dv, dkg), _ = _kv_bwd(dkc, dkp, dvc, dvp, zk, kg, cosf, sins, ind_k, ind_kt, n_seq, S)
    (grad_x, dz, dg_mix), ph = _inproj_bwd([dxr, dgr, dq, dk, dv, dga, dgb], w_in_g, xf, g_mix, dx1, tm,
                                           phases=[_ph_half_swap(r) for r in red_mix])
    red_mix = first(ph)
    red_in = _reduce_scatter(_wgrad(h0, dz, "wgrad_in", True, IN_BLK, tm)[0], "reduce_w_in")
    reduced = dict(zip(grp_ple + grp_mlp + grp_mix + ("w_in",), red_ple + red_mlp + red_mix + [red_in]))
    grads = {
        "g_mix": dg_mix[0], "g_mlp": dg_mlp[0], "g_ple": dg_ple[0],
        "conv_w": vec[0:CONV_W], "conv_b": vec[4], "b_rg": vec[5], "b_ig": vec[6], "lru_lambda": vec[7],
        "w_rg": _pair_blockdiag_extract(dwrg2), "w_ig": _pair_blockdiag_extract(dwig2),
        "q_gain": dqg.reshape(N_HEADS, HEAD_DIM).sum(0), "k_gain": dkg.reshape(N_KV, HEAD_DIM).sum(0),
        "sinks": dsk.sum(0)[:N_HEADS],
    }

    rows = [grads["conv_w"], _pad_row(grads["conv_b"]), _pad_row(grads["b_rg"]), _pad_row(grads["b_ig"]),
            _pad_row(grads["lru_lambda"]), _pad_row(grads["g_mix"]), _pad_row(grads["g_mlp"]),
            _pad_row(grads["g_ple"]), _pad_row(grads["q_gain"]), _pad_row(grads["k_gain"]), _pad_row(grads["sinks"]),
            jnp.zeros((2, D_MODEL), F32)]
    vecs = jnp.concatenate(rows, axis=0)
    packed = jnp.concatenate([vecs.reshape(-1, LANES), grads["w_rg"].reshape(-1, LANES),
                              grads["w_ig"].reshape(-1, LANES)], axis=0)
    red = _allreduce_small(packed, "allreduce_small")
    nv = vecs.size // LANES
    rvec = red[0:nv].reshape(16, D_MODEL)
    nw = grads["w_rg"].size // LANES
    sg = {
        "conv_w": lax.dynamic_slice(rvec[0:CONV_W], (0, chip * (D_MODEL // N_CHIPS)), (CONV_W, D_MODEL // N_CHIPS)),
        "conv_b": rvec[4], "b_rg": rvec[5], "b_ig": rvec[6], "lru_lambda": rvec[7], "g_mix": rvec[8],
        "g_mlp": rvec[9], "g_ple": rvec[10], "q_gain": rvec[11, :HEAD_DIM], "k_gain": rvec[12, :HEAD_DIM],
        "sinks": rvec[13, :N_HEADS], "w_rg": red[nv:nv + nw], "w_ig": red[nv + nw:nv + 2 * nw],
    }
    sg = {k: sg[k].reshape(w[k].shape) for k in _SMALL}
    d_s, m_s, v_s = _adamw_small([w[k] for k in _SMALL], [sg[k] for k in _SMALL], [m[k] for k in _SMALL],
                                 [v[k] for k in _SMALL])
    grad, delta, new_m, new_v = dict(sg), dict(zip(_SMALL, d_s)), dict(zip(_SMALL, m_s)), dict(zip(_SMALL, v_s))

    for name in _BIG:
        shape = w[name].shape
        g = reduced[name]
        (d, nm, nvv), _ = _adamw(w[name][0], g, m[name][0], v[name][0], "adamw_" + name, 128)
        grad[name], delta[name], new_m[name], new_v[name] = (a.reshape(shape) for a in (g, d, nm, nvv))

    return (loss, grad_x.reshape(x.shape), *[grad[k] for k in _WEIGHTS], *[delta[k] for k in _WEIGHTS],
            *[new_m[k] for k in _WEIGHTS], *[new_v[k] for k in _WEIGHTS])
```

```python
import functools
import math

import numpy as np
import jax
import jax.numpy as jnp
from jax import lax
from jax.experimental import pallas as pl
from jax.experimental.pallas import tpu as pltpu

F32 = jnp.float32
BF16 = jnp.bfloat16

D_MODEL = 1024
N_HEADS = 16
N_KV = 4
HEAD_DIM = 64
KV_W = N_KV * HEAD_DIM
D_FF = 4096
PLE_DIM = 256
WINDOW = 128
CONV_W = 4
LRU_C = 8.0
NORM_EPS = 1e-6
ROPE_THETA = 10000.0
N_CHIPS = 4
IN_TOTAL = 5632
IN_BLK = IN_TOTAL // N_CHIPS
IN_SEGS = (0, 1024, 2048, 3072, 3328, 3584, 4608, 5632)

ADAM_LR = 0.001
ADAM_B1 = 0.9
ADAM_B2 = 0.999
ADAM_EPS = 1e-08
ADAM_WD = 0.01
ADAM_STEP = 10

LANES = 128
VMEM_LIMIT = 56 * 1024 * 1024
MESH_ID = pl.DeviceIdType.MESH


def _dot(a, b):
    return jnp.dot(a, b, preferred_element_type=F32)


def _dot_nt(a, b):
    return lax.dot_general(a, b, (((1,), (1,)), ((), ())), preferred_element_type=F32)


def _dot_tn(a, b):
    return lax.dot_general(a, b, (((0,), (0,)), ((), ())), preferred_element_type=F32)


def _split_dot(x, ind):
    hi = x.astype(BF16)
    lo = (x - hi.astype(F32)).astype(BF16)
    return _dot(hi, ind) + _dot(lo, ind)


def _sigmoid(x):
    return 1.0 / (1.0 + jnp.exp(-x))


_GELU_C = math.sqrt(2.0 / math.pi)


def _gelu_and_grad(g):
    inner = _GELU_C * (g + 0.044715 * g * g * g)
    t = jnp.tanh(inner)
    gelu = 0.5 * g * (1.0 + t)
    dgelu = 0.5 * (1.0 + t) + 0.5 * g * (1.0 - t * t) * _GELU_C * (1.0 + 3.0 * 0.044715 * g * g)
    return gelu, dgelu


def _const(shape):
    nd = len(shape)
    return pl.BlockSpec(shape, lambda *_: (0,) * nd)


def _params(n_grid, vmem=VMEM_LIMIT):
    return pltpu.CompilerParams(dimension_semantics=("arbitrary",) * n_grid, vmem_limit_bytes=vmem)


def _rms_fwd(x, g):
    r = lax.rsqrt(jnp.mean(x * x, axis=-1, keepdims=True) + NORM_EPS)
    return (x * r) * g, r


def _rms_bwd(dy, x, r, g):
    dn = dy * g
    dx = r * dn - x * (r * r * r * jnp.mean(dn * x, axis=-1, keepdims=True))
    dg = jnp.sum(dy * (x * r), axis=0, keepdims=True)
    return dx, dg


def _seg_pieces(blk_lo, blk_hi):
    out = []
    for s in range(7):
        lo, hi = max(blk_lo, IN_SEGS[s]), min(blk_hi, IN_SEGS[s + 1])
        if lo < hi:
            out.append((s, lo - IN_SEGS[s], hi - IN_SEGS[s], lo - blk_lo))
    return out


def _mesh_pos():
    x, y, c = lax.axis_index("x"), lax.axis_index("y"), lax.axis_index("c")
    other_chips = [(1 - x, y), (x, 1 - y), (1 - x, 1 - y)]
    return x, y, c, other_chips


def _peer_slot(k, x, y):
    dx = jnp.bitwise_xor(k // 2, x)
    dy = jnp.bitwise_xor(k % 2, y)
    return jnp.maximum(dx + 2 * dy - 1, 0)


def _half_rows(c, R):
    return pl.ds(pl.multiple_of(c * R, R), R), pl.ds(pl.multiple_of((1 - c) * R, R), R)


def _remote(src, dst, sems, to):
    return pltpu.make_async_remote_copy(src_ref=src, dst_ref=dst, send_sem=sems[0], recv_sem=sems[1],
                                        device_id=to, device_id_type=MESH_ID)


class _Phase:
    def __init__(self, ins, inout, outs, n_remote, n_local, build):
        self.ins, self.inout, self.outs = list(ins), list(inout), list(outs)
        self.n_remote, self.n_local, self.build = n_remote, n_local, build


def _ph_gather_send(wb):
    R2, C = wb.shape
    R = R2 // 2

    def build(ins, outs, rsem, lsem):
        (w_ref,), (g_ref,) = ins, outs
        x, y, c, chips = _mesh_pos()
        me = 2 * x + y
        mine, _ = _half_rows(c, R)
        loc = [pltpu.make_async_copy(w_ref, g_ref.at[me], lsem(0))]
        outg = [_remote(w_ref.at[mine], g_ref.at[me, mine], rsem(j), (cx, cy, c)) for j, (cx, cy) in enumerate(chips)]
        inc = [functools.partial(_remote, w_ref.at[mine], g_ref.at[2 * cx + cy, mine], rsem(j), (x, y, c))
               for j, (cx, cy) in enumerate(chips)]
        return loc, outg, inc

    return _Phase([wb], [], [jax.ShapeDtypeStruct((N_CHIPS, R2, C), BF16)], 3, 1, build)


def _ph_gather_pass(gath):
    _, R2, C = gath.shape
    R = R2 // 2

    def build(ins, outs, rsem, lsem):
        (g_ref,) = outs
        x, y, c, chips = _mesh_pos()
        mine, theirs = _half_rows(c, R)
        outg, inc = [], []
        for j, (cx, cy) in enumerate(chips):
            blk = g_ref.at[2 * cx + cy, mine]
            outg.append(_remote(blk, blk, rsem(j), (x, y, 1 - c)))
            got = g_ref.at[2 * cx + cy, theirs]
            inc.append(functools.partial(_remote, got, got, rsem(j), (x, y, c)))
        return [], outg, inc

    return _Phase([], [gath], [], 3, 0, build)


def _ph_pair_send(partial):
    _, R2, C = partial.shape
    R = R2 // 2

    def build(ins, outs, rsem, lsem):
        (p_ref,), (s_ref,) = ins, outs
        x, y, c, _ = _mesh_pos()
        _, theirs = _half_rows(c, R)
        src = p_ref.at[:, theirs, :]
        return ([], [_remote(src, s_ref, rsem(0), (x, y, 1 - c))],
                [functools.partial(_remote, src, s_ref, rsem(0), (x, y, c))])

    return _Phase([partial], [], [jax.ShapeDtypeStruct((N_CHIPS, R, C), F32)], 1, 0, build)


def _ph_chip_send(sendb):
    def build(ins, outs, rsem, lsem):
        (s_ref,), (r_ref,) = ins, outs
        x, y, c, chips = _mesh_pos()
        outg = [_remote(s_ref.at[j], r_ref.at[j], rsem(j), (cx, cy, c)) for j, (cx, cy) in enumerate(chips)]
        inc = [functools.partial(_remote, s_ref.at[j], r_ref.at[j], rsem(j), (x, y, c)) for j in range(3)]
        return [], outg, inc

    return _Phase([sendb], [], [jax.ShapeDtypeStruct(sendb.shape, sendb.dtype)], 3, 0, build)


def _ph_half_swap(red):
    R2, C = red.shape
    R = R2 // 2

    def build(ins, outs, rsem, lsem):
        (r_ref,) = outs
        x, y, c, _ = _mesh_pos()
        mine, theirs = _half_rows(c, R)
        return ([], [_remote(r_ref.at[mine], r_ref.at[mine], rsem(0), (x, y, 1 - c))],
                [functools.partial(_remote, r_ref.at[theirs], r_ref.at[theirs], rsem(0), (x, y, c))])

    return _Phase([], [red], [], 1, 0, build)


def _call(body, *, name, grid, in_specs, out_specs, out_shape, scratch_shapes=(), phases=()):
    single = not isinstance(out_specs, (list, tuple))
    out_specs = [out_specs] if single else list(out_specs)
    out_shape = [out_shape] if single else list(out_shape)
    n_in, n_out, n_scr = len(in_specs), len(out_specs), len(scratch_shapes)
    if not phases:
        call = pl.pallas_call(body, name=name, grid=grid, in_specs=in_specs, out_specs=out_specs,
                              out_shape=out_shape, scratch_shapes=list(scratch_shapes),
                              compiler_params=_params(len(grid)))
        return lambda *operands: (list(call(*operands)), [])

    ex_in, ex_out, aliases, spans = [], [], {}, []
    for ph in phases:
        i0, o0 = len(ex_in), len(ex_out)
        ex_in += ph.ins
        for a in ph.inout:
            aliases[n_in + len(ex_in)] = n_out + len(ex_out)
            ex_in.append(a)
            ex_out.append(jax.ShapeDtypeStruct(a.shape, a.dtype))
        ex_out += ph.outs
        spans.append((i0, len(ph.ins), o0, len(ex_out) - o0))
    n_remote = sum(ph.n_remote for ph in phases)
    n_local = max(sum(ph.n_local for ph in phases), 1)

    def wrapped(*refs):
        base_in, xin = refs[:n_in], refs[n_in:n_in + len(ex_in)]
        o0 = n_in + len(ex_in)
        base_out, xout = refs[o0:o0 + n_out], refs[o0 + n_out:o0 + n_out + len(ex_out)]
        scr = refs[o0 + n_out + len(ex_out):]
        send_sems, recv_sems, loc_sems = scr[n_scr:]
        first = functools.reduce(jnp.logical_and, [pl.program_id(i) == 0 for i in range(len(grid))])
        last = functools.reduce(jnp.logical_and, [pl.program_id(i) == grid[i] - 1 for i in range(len(grid))])

        def copies():
            out, r0, l0 = [], 0, 0
            for ph, (i0, ni, p0, no) in zip(phases, spans):
                rsem = lambda k, r0=r0: (send_sems.at[r0 + k], recv_sems.at[r0 + k])
                lsem = lambda k, l0=l0: loc_sems.at[l0 + k]
                out.append(ph.build(xin[i0:i0 + ni], xout[p0:p0 + no], rsem, lsem))
                r0, l0 = r0 + ph.n_remote, l0 + ph.n_local
            return out

        @pl.when(first)
        def _():
            for loc, outg, _ in copies():
                for cp in loc + outg:
                    cp.start()

        body(*base_in, *base_out, *scr[:n_scr])

        @pl.when(last)
        def _():
            for loc, outg, inc in copies():
                for make in inc:
                    make().wait_recv()
                for cp in outg:
                    cp.wait_send()
                for cp in loc:
                    cp.wait()

    hbm = pl.BlockSpec(memory_space=pl.ANY)
    call = pl.pallas_call(
        wrapped, name=name, grid=grid, in_specs=list(in_specs) + [hbm] * len(ex_in),
        out_specs=out_specs + [hbm] * len(ex_out), out_shape=out_shape + ex_out,
        scratch_shapes=list(scratch_shapes) + [pltpu.SemaphoreType.DMA((n_remote,)), pltpu.SemaphoreType.DMA((n_remote,)),
                                              pltpu.SemaphoreType.DMA((n_local,))],
        input_output_aliases=aliases, compiler_params=_params(len(grid)))

    def run(*operands):
        res = call(*operands, *ex_in)
        extra = res[n_out:]
        return list(res[:n_out]), [list(extra[p0:p0 + no]) for (_, _, p0, no) in spans]

    return run


def _inproj_fwd(x, g_mix, w_in, tm, phases=()):
    T = x.shape[0]
    widths = [IN_SEGS[i + 1] - IN_SEGS[i] for i in range(7)]

    def body(x_ref, g_ref, w_ref, h_ref, *z_refs):
        h, _ = _rms_fwd(x_ref[...], g_ref[...])
        hb = h.astype(BF16)
        h_ref[...] = hb
        for j in range(N_CHIPS):
            zj = _dot(hb, w_ref[j])
            for s, lo, hi, off in _seg_pieces(j * IN_BLK, (j + 1) * IN_BLK):
                z_refs[s][:, lo:hi] = zj[:, off:off + hi - lo]

    return _call(
        body, phases=phases, name="inproj_fwd", grid=(T // tm,),
        in_specs=[pl.BlockSpec((tm, D_MODEL), lambda i: (i, 0)), _const((1, D_MODEL)),
                  _const((N_CHIPS, D_MODEL, IN_BLK))],
        out_specs=[pl.BlockSpec((tm, D_MODEL), lambda i: (i, 0))]
        + [pl.BlockSpec((tm, w), lambda i: (i, 0)) for w in widths],
        out_shape=[jax.ShapeDtypeStruct((T, D_MODEL), BF16)]
        + [jax.ShapeDtypeStruct((T, w), F32) for w in widths],
    )(x, g_mix, w_in)


def _inproj_bwd(dz_parts, w_in, x, g_mix, dx1, tm, phases=()):
    T = x.shape[0]
    widths = [IN_SEGS[i + 1] - IN_SEGS[i] for i in range(7)]

    def body(*refs):
        p_refs = refs[:7]
        w_ref, x_ref, g_ref, dx1_ref, gx_ref, dg_ref, dz_ref = refs[7:]

        @pl.when(pl.program_id(0) == 0)
        def _():
            dg_ref[...] = jnp.zeros_like(dg_ref)

        for s in range(7):
            dz_ref[:, IN_SEGS[s]:IN_SEGS[s + 1]] = p_refs[s][...]
        dh = jnp.zeros((tm, D_MODEL), F32)
        for j in range(N_CHIPS):
            dh = dh + _dot_nt(dz_ref[:, j * IN_BLK:(j + 1) * IN_BLK], w_ref[j])
        xv = x_ref[...]
        g = g_ref[...]
        _, r = _rms_fwd(xv, g)
        dx, dg = _rms_bwd(dh, xv, r, g)
        gx_ref[...] = dx1_ref[...] + dx
        dg_ref[...] += dg

    row = lambda w: pl.BlockSpec((tm, w), lambda i: (i, 0))
    return _call(
        body, phases=phases, name="inproj_bwd", grid=(T // tm,),
        in_specs=[row(w) for w in widths]
        + [_const((N_CHIPS, D_MODEL, IN_BLK)), row(D_MODEL), _const((1, D_MODEL)), row(D_MODEL)],
        out_specs=[row(D_MODEL), _const((1, D_MODEL))],
        out_shape=[jax.ShapeDtypeStruct((T, D_MODEL), F32), jax.ShapeDtypeStruct((1, D_MODEL), F32)],
        scratch_shapes=[pltpu.VMEM((tm, IN_TOTAL), BF16)],
    )(*dz_parts, w_in, x, g_mix, dx1)


def _wgrad_in(h0, dz_parts, tm):
    T = h0.shape[0]
    widths = [IN_SEGS[i + 1] - IN_SEGS[i] for i in range(7)]

    def body(*refs):
        h_ref, p_refs, o_ref, acc_ref, sem = refs[0], refs[1:8], refs[8], refs[9], refs[10]
        t = pl.program_id(0)

        @pl.when(t == 0)
        def _():
            acc_ref[...] = jnp.zeros_like(acc_ref)

        hv = h_ref[...]
        for j in range(N_CHIPS):
            for s, lo, hi, off in _seg_pieces(j * IN_BLK, (j + 1) * IN_BLK):
                acc_ref[j, :, off:off + hi - lo] += _dot_tn(hv, p_refs[s][:, lo:hi])

        @pl.when(t == T // tm - 1)
        def _():
            cp = pltpu.make_async_copy(acc_ref, o_ref, sem)
            cp.start()
            cp.wait()

    row = lambda w: pl.BlockSpec((tm, w), lambda i: (i, 0))
    return pl.pallas_call(
        body, name="wgrad_in", grid=(T // tm,), in_specs=[row(D_MODEL)] + [row(w) for w in widths],
        out_specs=pl.BlockSpec(memory_space=pl.ANY),
        out_shape=jax.ShapeDtypeStruct((N_CHIPS, D_MODEL, IN_BLK), F32),
        scratch_shapes=[pltpu.VMEM((N_CHIPS, D_MODEL, IN_BLK), F32), pltpu.SemaphoreType.DMA],
        compiler_params=_params(1),
    )(h0, *dz_parts)


def _wgrad(a, g, name, blocked, cn, tm, phases=()):
    T, K = a.shape
    N = g.shape[1]
    nb = N // cn

    def body(a_ref, g_ref, o_ref):
        @pl.when(pl.program_id(1) == 0)
        def _():
            o_ref[...] = jnp.zeros_like(o_ref)

        o_ref[...] += _dot_tn(a_ref[...].astype(BF16), g_ref[...].astype(BF16))

    if blocked:
        out_spec = pl.BlockSpec((None, K, cn), lambda j, t: (j, 0, 0))
        out_shape = jax.ShapeDtypeStruct((nb, K, cn), F32)
    else:
        out_spec = pl.BlockSpec((K, cn), lambda j, t: (0, j))
        out_shape = jax.ShapeDtypeStruct((K, N), F32)
    outs, extra = _call(
        body, phases=phases, name=name, grid=(nb, T // tm),
        in_specs=[pl.BlockSpec((tm, K), lambda j, t: (t, 0)), pl.BlockSpec((tm, cn), lambda j, t: (t, j))],
        out_specs=out_spec, out_shape=out_shape,
    )(a, g)
    return outs[0], extra


def _shift_down(x, prev8, sft, row, row8, tm):
    xs = pltpu.roll(x, sft, 0)
    top = jnp.where(row8 < sft, pltpu.roll(prev8, sft, 0), xs[0:8])
    return jnp.concatenate([top, xs[8:]], axis=0)


def _shift_up(x, next8, sft, row8, tm):
    xs = pltpu.roll(x, tm - sft, 0)
    bot = jnp.where(row8 >= 8 - sft, pltpu.roll(next8, 8 - sft, 0), xs[tm - 8:tm])
    return jnp.concatenate([xs[0:tm - 8], bot], axis=0)


def _conv_fwd(x, prev8, cw_ref, cb, row, row8, tm):
    xc = cb + cw_ref[CONV_W - 1:CONV_W, :] * x
    for sft in range(1, CONV_W):
        j = CONV_W - 1 - sft
        xc = xc + cw_ref[j:j + 1, :] * _shift_down(x, prev8, sft, row, row8, tm)
    return xc


def _blockdiag_dot(xb, w_ref, transpose):
    outs = []
    for b in range(D_MODEL // LANES):
        xs = xb[:, b * LANES:(b + 1) * LANES]
        outs.append(_dot_nt(xs, w_ref[b]) if transpose else _dot(xs, w_ref[b]))
    return jnp.concatenate(outs, axis=1)


def _softplus_neg(lam):
    e = jnp.exp(-jnp.abs(lam))
    u = 1.0 + e
    log1p_e = jnp.where(u == 1.0, e, jnp.log(u) * (e / (u - 1.0)))
    sp = jnp.maximum(-lam, 0.0) + log1p_e
    return sp, -_sigmoid(-lam)


def _lru_gates(xc, wrg_ref, brg, wig_ref, big, sp):
    xcb = xc.astype(BF16)
    r = _sigmoid(_blockdiag_dot(xcb, wrg_ref, False) + brg)
    i = _sigmoid(_blockdiag_dot(xcb, wig_ref, False) + big)
    log_a = (-LRU_C) * r * sp
    a = jnp.exp(log_a)
    t = jnp.tanh(log_a)
    one_m_a2 = (-2.0) * t / (1.0 - t)
    mult = jnp.sqrt(one_m_a2)
    return xcb, r, i, a, mult


def _scan_down(a, b, row, tm):
    d = 1
    while d < tm:
        keep = row >= d
        a_s = jnp.where(keep, pltpu.roll(a, d, 0), 1.0)
        b_s = jnp.where(keep, pltpu.roll(b, d, 0), 0.0)
        b = a * b_s + b
        a = a * a_s
        d *= 2
    return a, b


def _scan_up(c, b, row, tm):
    d = 1
    while d < tm:
        keep = row < tm - d
        c_s = jnp.where(keep, pltpu.roll(c, tm - d, 0), 1.0)
        b_s = jnp.where(keep, pltpu.roll(b, tm - d, 0), 0.0)
        b = c * b_s + b
        c = c * c_s
        d *= 2
    return c, b


def _rnn_fwd(xr, gr, conv_w, conv_b, wrg2, b_rg, wig2, b_ig, lam, n_seq, S, tm, phases=()):
    T = xr.shape[0]
    nt = S // tm
    W = D_MODEL

    def body(xr_ref, gr_ref, cw_ref, cb_ref, wrg_ref, brg_ref, wig_ref, big_ref, lam_ref,
             xc_ref, h_ref, ya_ref, px_ref, ph_ref):
        @pl.when(pl.program_id(1) == 0)
        def _():
            px_ref[...] = jnp.zeros_like(px_ref)
            ph_ref[...] = jnp.zeros_like(ph_ref)

        row = lax.broadcasted_iota(jnp.int32, (tm, W), 0)
        row8 = lax.broadcasted_iota(jnp.int32, (8, W), 0)
        x = xr_ref[...]
        xc = _conv_fwd(x, px_ref[...], cw_ref, cb_ref[...], row, row8, tm)
        sp, _ = _softplus_neg(lam_ref[...])
        _, r, i, a, mult = _lru_gates(xc, wrg_ref, brg_ref[...], wig_ref, big_ref[...], sp)
        bterm = mult * (i * xc)
        acum, hloc = _scan_down(a, bterm, row, tm)
        h = hloc + acum * ph_ref[7:8, :]
        h_ref[...] = h
        xc_ref[...] = xc
        gelu, _ = _gelu_and_grad(gr_ref[...])
        ya_ref[...] = (h * gelu).astype(BF16)
        px_ref[...] = xr_ref[tm - 8:tm, :]
        ph_ref[...] = h_ref[tm - 8:tm, :]

    tile = pl.BlockSpec((tm, W), lambda s, t: (s * nt + t, 0))
    return _call(
        body, phases=phases, name="rnn_fwd", grid=(n_seq, nt),
        in_specs=[tile, tile, _const((CONV_W, W)), _const((1, W)), _const((8, LANES, LANES)), _const((1, W)),
                  _const((8, LANES, LANES)), _const((1, W)), _const((1, W))],
        out_specs=[tile, tile, tile],
        out_shape=[jax.ShapeDtypeStruct((T, W), F32), jax.ShapeDtypeStruct((T, W), F32),
                   jax.ShapeDtypeStruct((T, W), BF16)],
        scratch_shapes=[pltpu.VMEM((8, W), F32), pltpu.VMEM((8, W), F32)],
    )(xr, gr, conv_w, conv_b, wrg2, b_rg, wig2, b_ig, lam)


def _rnn_bwd(dya, xr, gr, xc, h, conv_w, wrg2, b_rg, wig2, b_ig, lam, n_seq, S, tm, phases=()):
    T = xr.shape[0]
    nt = S // tm
    W = D_MODEL
    nb8 = tm // 8

    def body(dya_ref, xr_ref, gr_ref, xc_ref, h_ref, xprev_ref, hprev_ref, cw_ref, wrg_ref, brg_ref, wig_ref,
             big_ref, lam_ref, dxr_ref, dgr_ref, vec_ref, dwrg_ref, dwig_ref, cg_ref, ndxc_ref, tmp_ref):
        s, ti = pl.program_id(0), pl.program_id(1)

        @pl.when((s == 0) & (ti == 0))
        def _():
            vec_ref[...] = jnp.zeros_like(vec_ref)
            dwrg_ref[...] = jnp.zeros_like(dwrg_ref)
            dwig_ref[...] = jnp.zeros_like(dwig_ref)

        @pl.when(ti == 0)
        def _():
            cg_ref[...] = jnp.zeros_like(cg_ref)
            ndxc_ref[...] = jnp.zeros_like(ndxc_ref)

        first = ti == nt - 1
        row = lax.broadcasted_iota(jnp.int32, (tm, W), 0)
        row8 = lax.broadcasted_iota(jnp.int32, (8, W), 0)
        x = xr_ref[...]
        xc = xc_ref[...]
        hv = h_ref[...]
        xprev = jnp.where(first, 0.0, xprev_ref[...])
        hprev = jnp.where(first, 0.0, hprev_ref[...])
        sp, dsp_dlam = _softplus_neg(lam_ref[...])
        xcb, r, i, a, mult = _lru_gates(xc, wrg_ref, brg_ref[...], wig_ref, big_ref[...], sp)

        gelu, dgelu = _gelu_and_grad(gr_ref[...])
        dya_v = dya_ref[...]
        dgr_ref[...] = (dya_v * hv * dgelu).astype(BF16)
        dh = dya_v * gelu
        c = jnp.where(row < tm - 1, pltpu.roll(a, tm - 1, 0), 1.0)
        ccum, gloc = _scan_up(c, dh, row, tm)
        G = gloc + ccum * cg_ref[0:1, :]
        tmp_ref[...] = a * G
        cg_ref[...] = tmp_ref[0:8, :]

        h_m1 = _shift_down(hv, hprev, 1, row, row8, tm)
        ixc = i * xc
        dixc = G * mult
        dlog_a = (G * h_m1) * a - (G * ixc) * (a * a / mult)
        dr = dlog_a * ((-LRU_C) * sp)
        di = dixc * xc
        drg = dr * r * (1.0 - r)
        dig = di * i * (1.0 - i)
        vec_ref[7:8, :] += jnp.sum(dlog_a * ((-LRU_C) * r), axis=0, keepdims=True) * dsp_dlam
        vec_ref[5:6, :] += jnp.sum(drg, axis=0, keepdims=True)
        vec_ref[6:7, :] += jnp.sum(dig, axis=0, keepdims=True)
        drgb = drg.astype(BF16)
        digb = dig.astype(BF16)
        dxc = dixc * i + _blockdiag_dot(drgb, wrg_ref, True) + _blockdiag_dot(digb, wig_ref, True)
        for b in range(W // LANES):
            sl = slice(b * LANES, (b + 1) * LANES)
            dwrg_ref[b] += _dot_tn(xcb[:, sl], drgb[:, sl])
            dwig_ref[b] += _dot_tn(xcb[:, sl], digb[:, sl])

        vec_ref[4:5, :] += jnp.sum(dxc, axis=0, keepdims=True)
        vec_ref[3:4, :] += jnp.sum(dxc * x, axis=0, keepdims=True)
        dxr = cw_ref[CONV_W - 1:CONV_W, :] * dxc
        nxt = ndxc_ref[...]
        for sft in range(1, CONV_W):
            j = CONV_W - 1 - sft
            vec_ref[j:j + 1, :] += jnp.sum(dxc * _shift_down(x, xprev, sft, row, row8, tm), axis=0, keepdims=True)
            dxr = dxr + cw_ref[j:j + 1, :] * _shift_up(dxc, nxt, sft, row8, tm)
        dxr_ref[...] = dxr.astype(BF16)
        tmp_ref[...] = dxc
        ndxc_ref[...] = tmp_ref[0:8, :]

    rev = lambda s, t: (s * nt + nt - 1 - t, 0)
    tile = pl.BlockSpec((tm, W), rev)
    prev8 = pl.BlockSpec((8, W), lambda s, t: (jnp.maximum((s * nt + nt - 1 - t) * nb8 - 1, 0), 0))
    return _call(
        body, phases=phases, name="rnn_bwd", grid=(n_seq, nt),
        in_specs=[tile, tile, tile, tile, tile, prev8, prev8, _const((CONV_W, W)), _const((8, LANES, LANES)),
                  _const((1, W)), _const((8, LANES, LANES)), _const((1, W)), _const((1, W))],
        out_specs=[tile, tile, _const((16, W)), _const((8, LANES, LANES)), _const((8, LANES, LANES))],
        out_shape=[jax.ShapeDtypeStruct((T, W), BF16), jax.ShapeDtypeStruct((T, W), BF16),
                   jax.ShapeDtypeStruct((16, W), F32), jax.ShapeDtypeStruct((8, LANES, LANES), F32),
                   jax.ShapeDtypeStruct((8, LANES, LANES), F32)],
        scratch_shapes=[pltpu.VMEM((8, W), F32), pltpu.VMEM((8, W), F32), pltpu.VMEM((tm, W), F32)],
    )(dya, xr, gr, xc, h, xr, h, conv_w, wrg2, b_rg, wig2, b_ig, lam)


def _head_swap(t, lane):
    w = t.shape[1]
    return jnp.where(lane % HEAD_DIM < HEAD_DIM // 2, pltpu.roll(t, w - HEAD_DIM // 2, 1),
                     pltpu.roll(t, HEAD_DIM // 2, 1))


def _qk_prep(t, gain, cosf, sins, ind, indt, lane):
    ms = _split_dot(t * t, ind) * (1.0 / HEAD_DIM)
    rstd = _split_dot(lax.rsqrt(ms + NORM_EPS), indt)
    tn = (t * rstd) * gain
    return tn * cosf + _head_swap(tn, lane) * sins, rstd


def _qk_prep_bwd(dy, t, rstd, gain, cosf, sins, ind, indt, lane):
    dtn = dy * cosf + _head_swap(dy * sins, lane)
    dgain = jnp.sum(dtn * (t * rstd), axis=0, keepdims=True)
    dn = dtn * gain
    m = _split_dot(_split_dot(dn * t, ind), indt) * (1.0 / HEAD_DIM)
    return rstd * dn - t * (rstd * rstd * rstd * m), dgain


def _attn_mask(blk_idx):
    qi = lax.broadcasted_iota(jnp.int32, (WINDOW, 2 * WINDOW), 0)
    ci = lax.broadcasted_iota(jnp.int32, (WINDOW, 2 * WINDOW), 1)
    diff = WINDOW + qi - ci
    return (diff >= 0) & (diff < WINDOW) & ((ci >= WINDOW) | (blk_idx > 0))


def _pair_operands(kc, vc, lane128):
    out = []
    for m in range(KV_W // LANES):
        k2 = kc[:, m * LANES:(m + 1) * LANES]
        v2 = vc[:, m * LANES:(m + 1) * LANES]
        out.append((k2.astype(BF16), pltpu.roll(k2, HEAD_DIM, 1).astype(BF16),
                    v2.astype(BF16), pltpu.roll(v2, HEAD_DIM, 1).astype(BF16)))
    return out


def _softmax_sink(s, mask, sink):
    s = jnp.where(mask, s, -1e30)
    mx = jnp.maximum(jnp.max(s, axis=-1, keepdims=True), sink)
    e = jnp.where(mask, jnp.exp(s - mx), 0.0)
    es = jnp.exp(sink - mx)
    inv = 1.0 / (jnp.sum(e, axis=-1, keepdims=True) + es)
    return e * inv, es * inv


def _attn_fwd(q, k, v, qg, kg, sinks, cosf, sins, ind_q, ind_qt, ind_k, ind_kt, n_seq, S, phases=()):
    T = q.shape[0]
    nblk = S // WINDOW
    W = D_MODEL

    def body(sink_ref, q_ref, k_ref, v_ref, qg_ref, kg_ref, cos_ref, sin_ref, iq_ref, iqt_ref, ik_ref, ikt_ref,
             o_ref, kc_ref, vc_ref):
        n = pl.program_id(1)

        @pl.when(n == 0)
        def _():
            kc_ref[...] = jnp.zeros_like(kc_ref)
            vc_ref[...] = jnp.zeros_like(vc_ref)

        lane = lax.broadcasted_iota(jnp.int32, (WINDOW, W), 1)
        lane_k = lane[:, :KV_W]
        lane128 = lane[:, :LANES]
        cosf, sinv = cos_ref[...], sin_ref[...]
        qr, _ = _qk_prep(q_ref[...], qg_ref[...], cosf, sinv, iq_ref[...], iqt_ref[...], lane)
        kr, _ = _qk_prep(k_ref[...], kg_ref[...], cosf[:, :KV_W], sinv[:, :KV_W], ik_ref[...], ikt_ref[...], lane_k)
        kc_ref[WINDOW:2 * WINDOW, :] = kr
        vc_ref[WINDOW:2 * WINDOW, :] = v_ref[...]
        ops = _pair_operands(kc_ref[...], vc_ref[...], lane128)
        mask = _attn_mask(n)
        lo = lane128 < HEAD_DIM
        scale = HEAD_DIM ** -0.5
        for i in range(N_HEADS // 2):
            kvh = i // 2
            k2, k2r, v2, v2r = ops[kvh // 2]
            if kvh % 2:
                k2, k2r, v2, v2r = k2r, k2, v2r, v2
            qp = qr[:, i * LANES:(i + 1) * LANES]
            q_lo = jnp.where(lo, qp, 0.0).astype(BF16)
            q_hi = jnp.where(lo, 0.0, qp).astype(BF16)
            p_lo, _ = _softmax_sink(_dot_nt(q_lo, k2) * scale, mask, sink_ref[2 * i])
            p_hi, _ = _softmax_sink(_dot_nt(q_hi, k2r) * scale, mask, sink_ref[2 * i + 1])
            o_pair = jnp.where(lo, _dot(p_lo.astype(BF16), v2), _dot(p_hi.astype(BF16), v2r))
            o_ref[:, i * LANES:(i + 1) * LANES] = o_pair.astype(BF16)
        kc_ref[0:WINDOW, :] = kc_ref[WINDOW:2 * WINDOW, :]
        vc_ref[0:WINDOW, :] = vc_ref[WINDOW:2 * WINDOW, :]

    blk = lambda w: pl.BlockSpec((WINDOW, w), lambda s, n: (s * nblk + n, 0))
    pos = pl.BlockSpec((WINDOW, W), lambda s, n: (n, 0))
    outs, extra = _call(
        body, phases=phases, name="attn_fwd", grid=(n_seq, nblk),
        in_specs=[pl.BlockSpec(memory_space=pltpu.SMEM), blk(W), blk(KV_W), blk(KV_W), _const((1, W)),
                  _const((1, KV_W)), pos, pos, _const((W, LANES)), _const((LANES, W)), _const((KV_W, LANES)),
                  _const((LANES, KV_W))],
        out_specs=blk(W), out_shape=jax.ShapeDtypeStruct((T, W), BF16),
        scratch_shapes=[pltpu.VMEM((2 * WINDOW, KV_W), F32), pltpu.VMEM((2 * WINDOW, KV_W), F32)],
    )(sinks, q, k, v, qg, kg, cosf, sins, ind_q, ind_qt, ind_k, ind_kt)
    return outs[0], extra


def _attn_bwd(do, q, k, v, qg, kg, sinks, cosf, sins, ind_q, ind_qt, ind_k, ind_kt, n_seq, S, phases=()):
    T = q.shape[0]
    nblk = S // WINDOW
    W = D_MODEL

    def body(sink_ref, do_ref, q_ref, k_ref, v_ref, qg_ref, kg_ref, cos_ref, sin_ref, iq_ref, iqt_ref, ik_ref,
             ikt_ref, dq_ref, dkc_ref, dkp_ref, dvc_ref, dvp_ref, dqg_ref, dsk_ref, kc_ref, vc_ref, dqr_ref,
             dk_ref, dv_ref):
        s_id, n = pl.program_id(0), pl.program_id(1)

        @pl.when((s_id == 0) & (n == 0))
        def _():
            dqg_ref[...] = jnp.zeros_like(dqg_ref)
            dsk_ref[...] = jnp.zeros_like(dsk_ref)

        @pl.when(n == 0)
        def _():
            kc_ref[...] = jnp.zeros_like(kc_ref)
            vc_ref[...] = jnp.zeros_like(vc_ref)

        lane = lax.broadcasted_iota(jnp.int32, (WINDOW, W), 1)
        lane_k = lane[:, :KV_W]
        lane128 = lane[:, :LANES]
        cosf, sinv = cos_ref[...], sin_ref[...]
        qv = q_ref[...]
        qr, q_rstd = _qk_prep(qv, qg_ref[...], cosf, sinv, iq_ref[...], iqt_ref[...], lane)
        kr, _ = _qk_prep(k_ref[...], kg_ref[...], cosf[:, :KV_W], sinv[:, :KV_W], ik_ref[...], ikt_ref[...], lane_k)
        kc_ref[WINDOW:2 * WINDOW, :] = kr
        vc_ref[WINDOW:2 * WINDOW, :] = v_ref[...]
        ops = _pair_operands(kc_ref[...], vc_ref[...], lane128)
        mask = _attn_mask(n)
        lo = lane128 < HEAD_DIM
        lo2 = lax.broadcasted_iota(jnp.int32, (2 * WINDOW, LANES), 1) < HEAD_DIM
        scale = HEAD_DIM ** -0.5
        dk_ref[...] = jnp.zeros_like(dk_ref)
        dv_ref[...] = jnp.zeros_like(dv_ref)
        dsk = jnp.zeros((WINDOW, LANES), F32)
        for i in range(N_HEADS // 2):
            kvh = i // 2
            m = kvh // 2
            k2, k2r, v2, v2r = ops[m]
            if kvh % 2:
                k2, k2r, v2, v2r = k2r, k2, v2r, v2
            qp = qr[:, i * LANES:(i + 1) * LANES]
            dop = do_ref[:, i * LANES:(i + 1) * LANES]
            dq_pair = None
            for half in range(2):
                sel = lo if half == 0 else ~lo
                kk, vv = (k2, v2) if half == 0 else (k2r, v2r)
                qh = jnp.where(sel, qp, 0.0).astype(BF16)
                doh = jnp.where(sel, dop, 0.0).astype(BF16)
                p, ps = _softmax_sink(_dot_nt(qh, kk) * scale, mask, sink_ref[2 * i + half])
                dp = _dot_nt(doh, vv)
                dd = jnp.sum(p * dp, axis=-1, keepdims=True)
                ds = (p * (dp - dd) * scale).astype(BF16)
                dsk = dsk + jnp.where(lane128 == 2 * i + half, -(ps * dd), 0.0)
                dq_h = _dot(ds, kk)
                dq_pair = dq_h if half == 0 else jnp.where(lo, dq_pair, dq_h)
                dk_h = _dot_tn(ds, qh)
                dv_h = _dot_tn(p.astype(BF16), doh)
                own_lo = (kvh % 2 == 0)
                if (half == 0) != own_lo:
                    dk_h = pltpu.roll(dk_h, HEAD_DIM, 1)
                    dv_h = pltpu.roll(dv_h, HEAD_DIM, 1)
                dk_ref[:, m * LANES:(m + 1) * LANES] += dk_h
                dv_ref[:, m * LANES:(m + 1) * LANES] += dv_h
            dqr_ref[:, i * LANES:(i + 1) * LANES] = dq_pair
        dsk_ref[...] += dsk
        dq, dqg = _qk_prep_bwd(dqr_ref[...], qv, q_rstd, qg_ref[...], cosf, sinv, iq_ref[...], iqt_ref[...], lane)
        dq_ref[...] = dq.astype(BF16)
        dqg_ref[...] += dqg
        dkp_ref[...] = dk_ref[0:WINDOW, :]
        dkc_ref[...] = dk_ref[WINDOW:2 * WINDOW, :]
        dvp_ref[...] = dv_ref[0:WINDOW, :]
        dvc_ref[...] = dv_ref[WINDOW:2 * WINDOW, :]
        kc_ref[0:WINDOW, :] = kc_ref[WINDOW:2 * WINDOW, :]
        vc_ref[0:WINDOW, :] = vc_ref[WINDOW:2 * WINDOW, :]

    blk = lambda w: pl.BlockSpec((WINDOW, w), lambda s, n: (s * nblk + n, 0))
    pos = pl.BlockSpec((WINDOW, W), lambda s, n: (n, 0))
    kv_out = jax.ShapeDtypeStruct((T, KV_W), F32)
    return _call(
        body, phases=phases, name="attn_bwd", grid=(n_seq, nblk),
        in_specs=[pl.BlockSpec(memory_space=pltpu.SMEM), blk(W), blk(W), blk(KV_W), blk(KV_W), _const((1, W)),
                  _const((1, KV_W)), pos, pos, _const((W, LANES)), _const((LANES, W)), _const((KV_W, LANES)),
                  _const((LANES, KV_W))],
        out_specs=[blk(W), blk(KV_W), blk(KV_W), blk(KV_W), blk(KV_W), _const((1, W)), _const((WINDOW, LANES))],
        out_shape=[jax.ShapeDtypeStruct((T, W), BF16), kv_out, kv_out, kv_out, kv_out,
                   jax.ShapeDtypeStruct((1, W), F32), jax.ShapeDtypeStruct((WINDOW, LANES), F32)],
        scratch_shapes=[pltpu.VMEM((2 * WINDOW, KV_W), F32), pltpu.VMEM((2 * WINDOW, KV_W), F32),
                        pltpu.VMEM((WINDOW, W), F32), pltpu.VMEM((2 * WINDOW, KV_W), F32),
                        pltpu.VMEM((2 * WINDOW, KV_W), F32)],
    )(sinks, do, q, k, v, qg, kg, cosf, sins, ind_q, ind_qt, ind_k, ind_kt)


def _kv_bwd(dkc, dkp, dvc, dvp, k, kg, cosf, sins, ind_k, ind_kt, n_seq, S, phases=()):
    T = k.shape[0]
    nblk = S // WINDOW

    def body(dkc_ref, dkp_ref, dvc_ref, dvp_ref, k_ref, kg_ref, cos_ref, sin_ref, ik_ref, ikt_ref,
             dk_ref, dv_ref, dkg_ref):
        s_id, n = pl.program_id(0), pl.program_id(1)

        @pl.when((s_id == 0) & (n == 0))
        def _():
            dkg_ref[...] = jnp.zeros_like(dkg_ref)

        has_next = n < nblk - 1
        lane = lax.broadcasted_iota(jnp.int32, (WINDOW, KV_W), 1)
        dkr = dkc_ref[...] + jnp.where(has_next, dkp_ref[...], 0.0)
        dv_ref[...] = (dvc_ref[...] + jnp.where(has_next, dvp_ref[...], 0.0)).astype(BF16)
        cosf, sinv = cos_ref[:, :KV_W], sin_ref[:, :KV_W]
        kv = k_ref[...]
        _, rstd = _qk_prep(kv, kg_ref[...], cosf, sinv, ik_ref[...], ikt_ref[...], lane)
        dk, dkg = _qk_prep_bwd(dkr, kv, rstd, kg_ref[...], cosf, sinv, ik_ref[...], ikt_ref[...], lane)
        dk_ref[...] = dk.astype(BF16)
        dkg_ref[...] += dkg

    cur = pl.BlockSpec((WINDOW, KV_W), lambda s, n: (s * nblk + n, 0))
    nxt = pl.BlockSpec((WINDOW, KV_W), lambda s, n: (s * nblk + jnp.minimum(n + 1, nblk - 1), 0))
    pos = pl.BlockSpec((WINDOW, D_MODEL), lambda s, n: (n, 0))
    return _call(
        body, phases=phases, name="kv_bwd", grid=(n_seq, nblk),
        in_specs=[cur, nxt, cur, nxt, cur, _const((1, KV_W)), pos, pos, _const((KV_W, LANES)),
                  _const((LANES, KV_W))],
        out_specs=[cur, cur, _const((1, KV_W))],
        out_shape=[jax.ShapeDtypeStruct((T, KV_W), BF16), jax.ShapeDtypeStruct((T, KV_W), BF16),
                   jax.ShapeDtypeStruct((1, KV_W), F32)],
    )(dkc, dkp, dvc, dvp, k, kg, cosf, sins, ind_k, ind_kt)


def _merge_fwd(x, ya, o, ga, gb, w_rnn, w_attn, w_out, tm, phases=()):
    T = x.shape[0]
    W = D_MODEL

    def body(x_ref, ya_ref, o_ref, ga_ref, gb_ref, wr_ref, wa_ref, wo_ref, x1_ref, mg_ref, yao_ref, ybo_ref):
        y_a = _dot(ya_ref[...], wr_ref[...])
        y_b = _dot(o_ref[...], wa_ref[...])
        yao_ref[...] = y_a
        ybo_ref[...] = y_b
        mg = (_sigmoid(ga_ref[...]) * y_a + _sigmoid(gb_ref[...]) * y_b).astype(BF16)
        mg_ref[...] = mg
        x1_ref[...] = x_ref[...] + _dot(mg, wo_ref[...])

    row = pl.BlockSpec((tm, W), lambda i: (i, 0))
    sq = _const((W, W))
    return _call(
        body, phases=phases, name="merge_fwd", grid=(T // tm,),
        in_specs=[row, row, row, row, row, sq, sq, sq], out_specs=[row, row, row, row],
        out_shape=[jax.ShapeDtypeStruct((T, W), F32), jax.ShapeDtypeStruct((T, W), BF16),
                   jax.ShapeDtypeStruct((T, W), F32), jax.ShapeDtypeStruct((T, W), F32)],
    )(x, ya, o, ga, gb, w_rnn, w_attn, w_out)


def _merge_bwd(dx1, ga, gb, y_a, y_b, w_rnn, w_attn, w_out, tm, phases=()):
    T = dx1.shape[0]
    W = D_MODEL

    def body(dx1_ref, ga_ref, gb_ref, ya_ref, yb_ref, wr_ref, wa_ref, wo_ref,
             dga_ref, dgb_ref, dya_ref, dyb_ref, dyain_ref, do_ref):
        dm = _dot_nt(dx1_ref[...].astype(BF16), wo_ref[...])
        sa = _sigmoid(ga_ref[...])
        sb = _sigmoid(gb_ref[...])
        dga_ref[...] = (dm * ya_ref[...] * (sa * (1.0 - sa))).astype(BF16)
        dgb_ref[...] = (dm * yb_ref[...] * (sb * (1.0 - sb))).astype(BF16)
        dya = (dm * sa).astype(BF16)
        dyb = (dm * sb).astype(BF16)
        dya_ref[...] = dya
        dyb_ref[...] = dyb
        dyain_ref[...] = _dot_nt(dya, wr_ref[...])
        do_ref[...] = _dot_nt(dyb, wa_ref[...])

    row = pl.BlockSpec((tm, W), lambda i: (i, 0))
    sq = _const((W, W))
    b16 = jax.ShapeDtypeStruct((T, W), BF16)
    f32 = jax.ShapeDtypeStruct((T, W), F32)
    return _call(
        body, phases=phases, name="merge_bwd", grid=(T // tm,),
        in_specs=[row, row, row, row, row, sq, sq, sq], out_specs=[row] * 6,
        out_shape=[b16, b16, b16, b16, f32, f32],
    )(dx1, ga, gb, y_a, y_b, w_rnn, w_attn, w_out)


def _mlp_fwd(x1, g_mlp, w_up, w_down, tm, phases=()):
    T = x1.shape[0]
    W = D_MODEL

    def body(x_ref, g_ref, wu_ref, wd_ref, x2_ref, hm_ref, u_ref, act_ref):
        xv = x_ref[...]
        hm, _ = _rms_fwd(xv, g_ref[...])
        hmb = hm.astype(BF16)
        hm_ref[...] = hmb
        for j in range(N_CHIPS):
            u = _dot(hmb, wu_ref[j])
            u_ref[:, j * W:(j + 1) * W] = u
            ru = jnp.maximum(u, 0.0)
            act_ref[:, j * W:(j + 1) * W] = (ru * ru).astype(BF16)
        x2_ref[...] = xv + _dot(act_ref[...], wd_ref[...])

    row = lambda w: pl.BlockSpec((tm, w), lambda i: (i, 0))
    return _call(
        body, phases=phases, name="mlp_fwd", grid=(T // tm,),
        in_specs=[row(W), _const((1, W)), _const((N_CHIPS, W, W)), _const((D_FF, W))],
        out_specs=[row(W), row(W), row(D_FF), row(D_FF)],
        out_shape=[jax.ShapeDtypeStruct((T, W), F32), jax.ShapeDtypeStruct((T, W), BF16),
                   jax.ShapeDtypeStruct((T, D_FF), F32), jax.ShapeDtypeStruct((T, D_FF), BF16)],
    )(x1, g_mlp, w_up, w_down)


def _mlp_bwd(dx2, u, x1, g_mlp, w_up, w_down, tm, phases=()):
    T = x1.shape[0]
    W = D_MODEL

    def body(dx2_ref, u_ref, x_ref, g_ref, wu_ref, wd_ref, dx1_ref, du_ref, dg_ref):
        @pl.when(pl.program_id(0) == 0)
        def _():
            dg_ref[...] = jnp.zeros_like(dg_ref)

        dx2 = dx2_ref[...]
        dact = _dot_nt(dx2.astype(BF16), wd_ref[...])
        du_ref[...] = (dact * (2.0 * jnp.maximum(u_ref[...], 0.0))).astype(BF16)
        dhm = jnp.zeros((tm, W), F32)
        for j in range(N_CHIPS):
            dhm = dhm + _dot_nt(du_ref[:, j * W:(j + 1) * W], wu_ref[j])
        xv = x_ref[...]
        g = g_ref[...]
        _, r = _rms_fwd(xv, g)
        dx, dg = _rms_bwd(dhm, xv, r, g)
        dx1_ref[...] = dx2 + dx
        dg_ref[...] += dg

    row = lambda w: pl.BlockSpec((tm, w), lambda i: (i, 0))
    return _call(
        body, phases=phases, name="mlp_bwd", grid=(T // tm,),
        in_specs=[row(W), row(D_FF), row(W), _const((1, W)), _const((N_CHIPS, W, W)), _const((D_FF, W))],
        out_specs=[row(W), row(D_FF), _const((1, W))],
        out_shape=[jax.ShapeDtypeStruct((T, W), F32), jax.ShapeDtypeStruct((T, D_FF), BF16),
                   jax.ShapeDtypeStruct((1, W), F32)],
    )(dx2, u, x1, g_mlp, w_up, w_down)


def _ple_loss(x2, p, target, g_ple, w_gate, w_proj, tm, phases=()):
    T = x2.shape[0]
    W = D_MODEL
    cw = W // N_CHIPS

    def body(x_ref, p_ref, t_ref, g_ref, wg_ref, wp_ref, loss_ref, dx2_ref, pb_ref, de_ref, hp_ref, dtg_ref, dg_ref):
        @pl.when(pl.program_id(0) == 0)
        def _():
            dg_ref[...] = jnp.zeros_like(dg_ref)
            loss_ref[...] = jnp.zeros_like(loss_ref)

        xv = x_ref[...]
        g = g_ref[...]
        pb = p_ref[...].astype(BF16)
        pb_ref[...] = pb
        e = jnp.concatenate([_dot(pb, wp_ref[j]) for j in range(N_CHIPS)], axis=1)
        hp, r = _rms_fwd(xv, g)
        hpb = hp.astype(BF16)
        hp_ref[...] = hpb
        sg = _sigmoid(_dot(hpb, wg_ref[...]))
        diff = (xv + e * sg) - t_ref[...]
        loss_ref[...] += jnp.sum(diff * diff) * (0.5 / W)
        dx3 = diff * (1.0 / W)
        de_ref[...] = (dx3 * sg).astype(BF16)
        dtg = (dx3 * e * (sg * (1.0 - sg))).astype(BF16)
        dtg_ref[...] = dtg
        dx, dg = _rms_bwd(_dot_nt(dtg, wg_ref[...]), xv, r, g)
        dx2_ref[...] = dx3 + dx
        dg_ref[...] += dg

    row = lambda w: pl.BlockSpec((tm, w), lambda i: (i, 0))
    b16 = lambda w: jax.ShapeDtypeStruct((T, w), BF16)
    return _call(
        body, phases=phases, name="ple_loss", grid=(T // tm,),
        in_specs=[row(W), row(PLE_DIM), row(W), _const((1, W)), _const((W, W)), _const((N_CHIPS, PLE_DIM, cw))],
        out_specs=[_const((8, LANES)), row(W), row(PLE_DIM), row(W), row(W), row(W), _const((1, W))],
        out_shape=[jax.ShapeDtypeStruct((8, LANES), F32), jax.ShapeDtypeStruct((T, W), F32), b16(PLE_DIM),
                   b16(W), b16(W), b16(W), jax.ShapeDtypeStruct((1, W), F32)],
    )(x2, p, target, g_ple, w_gate, w_proj)


def _adamw(w, g, m, v, name, tr, phases=()):
    R, C = w.shape
    c1 = 1.0 / (1.0 - ADAM_B1 ** ADAM_STEP)
    c2 = 1.0 / (1.0 - ADAM_B2 ** ADAM_STEP)

    def body(w_ref, g_ref, m_ref, v_ref, go_ref, d_ref, nm_ref, nv_ref):
        gv = g_ref[...]
        go_ref[...] = gv
        nm = ADAM_B1 * m_ref[...] + (1.0 - ADAM_B1) * gv
        nv = ADAM_B2 * v_ref[...] + (1.0 - ADAM_B2) * (gv * gv)
        nm_ref[...] = nm
        nv_ref[...] = nv
        d_ref[...] = (-ADAM_LR) * ((nm * c1) / (jnp.sqrt(nv * c2) + ADAM_EPS) + ADAM_WD * w_ref[...])

    row = pl.BlockSpec((tr, C), lambda i: (i, 0))
    sds = jax.ShapeDtypeStruct((R, C), F32)
    return _call(
        body, phases=phases, name=name, grid=(R // tr,), in_specs=[row] * 4, out_specs=[row] * 4,
        out_shape=[sds] * 4,
    )(w, g, m, v)


def _indicator(width):
    ind = np.zeros((width, LANES), np.float32)
    ind[np.arange(width), np.arange(width) // HEAD_DIM] = 1.0
    return jnp.asarray(ind, BF16), jnp.asarray(ind.T, BF16)


def _rope_tables(S):
    inv = ROPE_THETA ** (-jnp.arange(0, HEAD_DIM, 2, dtype=F32) / HEAD_DIM)
    ang = jnp.arange(S, dtype=F32)[:, None] * inv[None, :]
    cos, sin = jnp.cos(ang), jnp.sin(ang)
    cosf = jnp.tile(jnp.concatenate([cos, cos], axis=1), (1, N_HEADS))
    sins = jnp.tile(jnp.concatenate([-sin, sin], axis=1), (1, N_HEADS))
    return cosf, sins


def _pair_blockdiag(w):
    w4 = w.reshape(8, 2, HEAD_DIM, HEAD_DIM)
    eye = jnp.eye(2, dtype=w.dtype)
    return jnp.einsum("bpij,pq->bpiqj", w4, eye).reshape(8, LANES, LANES)


def _pair_blockdiag_extract(g):
    g5 = g.reshape(8, 2, HEAD_DIM, 2, HEAD_DIM)
    return jnp.stack([g5[:, 0, :, 0, :], g5[:, 1, :, 1, :]], axis=1).reshape(16, HEAD_DIM, HEAD_DIM)


def _pair_sum(parts, sibs, name):
    n = len(parts)
    dims = [(p.shape[1] // 2, p.shape[2]) for p in parts]

    def body(*refs):
        p_r, s_r, send_r, own_r, mine_r, sem = (refs[0:n], refs[n:2 * n], refs[2 * n:3 * n], refs[3 * n:4 * n],
                                                refs[4 * n:5 * n], refs[5 * n])
        x, y, c, chips = _mesh_pos()
        me = 2 * x + y
        loads = []
        for i, (R, _) in enumerate(dims):
            mine, _ = _half_rows(c, R)
            cp = pltpu.make_async_copy(p_r[i].at[:, mine, :], mine_r[i], sem.at[i])
            cp.start()
            loads.append(cp)
        for i in range(n):
            loads[i].wait()
            for j, (cx, cy) in enumerate(chips):
                k = 2 * cx + cy
                send_r[i][j] = (mine_r[i][k] + s_r[i][k]).astype(BF16)
            own_r[i][...] = mine_r[i][me] + s_r[i][me]

    vm = pl.BlockSpec(memory_space=pltpu.VMEM)
    out = pl.pallas_call(
        body, name=name, in_specs=[pl.BlockSpec(memory_space=pl.ANY)] * n + [vm] * n, out_specs=[vm] * (2 * n),
        out_shape=[jax.ShapeDtypeStruct((3, R, C), BF16) for R, C in dims]
        + [jax.ShapeDtypeStruct((R, C), F32) for R, C in dims],
        scratch_shapes=[pltpu.VMEM((N_CHIPS, R, C), F32) for R, C in dims] + [pltpu.SemaphoreType.DMA((n,))],
        compiler_params=pltpu.CompilerParams(vmem_limit_bytes=VMEM_LIMIT),
    )(*parts, *sibs)
    return out[:n], out[n:]


def _chip_sum(owns, recvs, name):
    n = len(owns)
    dims = [o.shape for o in owns]

    def body(*refs):
        own_r, recv_r, red_r, stage_r, sem = refs[0:n], refs[n:2 * n], refs[2 * n:3 * n], refs[3 * n:4 * n], refs[4 * n]
        x, y, c, _ = _mesh_pos()
        me = 2 * x + y
        stores = []
        for i, (R, _) in enumerate(dims):
            acc = None
            for k in range(N_CHIPS):
                term = jnp.where(me == k, own_r[i][...], recv_r[i][_peer_slot(k, x, y)].astype(F32))
                acc = term if acc is None else acc + term
            stage_r[i][...] = acc
            mine, _ = _half_rows(c, R)
            cp = pltpu.make_async_copy(stage_r[i], red_r[i].at[mine, :], sem.at[i])
            cp.start()
            stores.append(cp)
        for cp in stores:
            cp.wait()

    vm = pl.BlockSpec(memory_space=pltpu.VMEM)
    return pl.pallas_call(
        body, name=name, in_specs=[vm] * (2 * n), out_specs=[pl.BlockSpec(memory_space=pl.ANY)] * n,
        out_shape=[jax.ShapeDtypeStruct((2 * R, C), F32) for R, C in dims],
        scratch_shapes=[pltpu.VMEM((R, C), F32) for R, C in dims] + [pltpu.SemaphoreType.DMA((n,))],
        compiler_params=pltpu.CompilerParams(vmem_limit_bytes=VMEM_LIMIT),
    )(*owns, *recvs)


def _gather_bf16(shard, name):
    R2, C = shard.shape
    R = R2 // 2

    def body(s_ref, o_ref, send_sems, recv_sems):
        x, y, c, chips = _mesh_pos()
        me = 2 * x + y
        mine = pl.ds(pl.multiple_of(c * R, R), R)
        theirs = pl.ds(pl.multiple_of((1 - c) * R, R), R)
        o_ref[me] = s_ref[...].astype(BF16)

        def copy(k, chip, rows, to):
            blk = o_ref.at[chip, rows]
            return pltpu.make_async_remote_copy(src_ref=blk, dst_ref=blk, send_sem=send_sems.at[k],
                                                recv_sem=recv_sems.at[k], device_id=to, device_id_type=MESH_ID)

        first = [copy(j, me, mine, (cx, cy, c)) for j, (cx, cy) in enumerate(chips)]
        for cp in first:
            cp.start()
        passed = []
        for j, (cx, cy) in enumerate(chips):
            copy(j, 2 * cx + cy, mine, (x, y, c)).wait_recv()
            cp = copy(3 + j, 2 * cx + cy, mine, (x, y, 1 - c))
            cp.start()
            passed.append(cp)
        for j, (cx, cy) in enumerate(chips):
            copy(3 + j, 2 * cx + cy, theirs, (x, y, c)).wait_recv()
        for cp in first + passed:
            cp.wait_send()

    return pl.pallas_call(
        body, name=name, out_shape=jax.ShapeDtypeStruct((N_CHIPS, R2, C), BF16),
        in_specs=[pl.BlockSpec(memory_space=pltpu.VMEM)], out_specs=pl.BlockSpec(memory_space=pltpu.VMEM),
        scratch_shapes=[pltpu.SemaphoreType.DMA((6,)), pltpu.SemaphoreType.DMA((6,))],
        compiler_params=pltpu.CompilerParams(vmem_limit_bytes=VMEM_LIMIT),
    )(shard)


def _pair_exchange_sum(partial, name):
    _, R2, C = partial.shape
    R = R2 // 2

    def body(p_ref, send_ref, own_ref, mine_ref, sib_ref, loc_sems, send_sems, recv_sems):
        x, y, c, chips = _mesh_pos()
        me = 2 * x + y
        mine, theirs = _half_rows(c, R)
        order = [2 * cx + cy for cx, cy in chips] + [me]
        locs, pairs = [], []
        for i, k in enumerate(order):
            loc = pltpu.make_async_copy(p_ref.at[k, mine, :], mine_ref.at[i], loc_sems.at[i])
            pair = _remote(p_ref.at[k, theirs, :], sib_ref.at[i], (send_sems.at[i], recv_sems.at[i]), (x, y, 1 - c))
            loc.start()
            pair.start()
            locs.append(loc)
            pairs.append(pair)
        for i in range(N_CHIPS):
            locs[i].wait()
            pairs[i].wait_recv()
            total = mine_ref[i] + sib_ref[i]
            if i < 3:
                send_ref[i] = total.astype(BF16)
            else:
                own_ref[...] = total
        for pair in pairs:
            pair.wait_send()

    vm = pl.BlockSpec(memory_space=pltpu.VMEM)
    return pl.pallas_call(
        body, name=name, in_specs=[pl.BlockSpec(memory_space=pl.ANY)], out_specs=[vm, vm],
        out_shape=[jax.ShapeDtypeStruct((3, R, C), BF16), jax.ShapeDtypeStruct((R, C), F32)],
        scratch_shapes=[pltpu.VMEM((N_CHIPS, R, C), F32), pltpu.VMEM((N_CHIPS, R, C), F32),
                        pltpu.SemaphoreType.DMA((N_CHIPS,)), pltpu.SemaphoreType.DMA((N_CHIPS,)),
                        pltpu.SemaphoreType.DMA((N_CHIPS,))],
        compiler_params=pltpu.CompilerParams(vmem_limit_bytes=VMEM_LIMIT),
    )(partial)


def _allreduce_small(buf, name):
    shape = buf.shape

    def body(b_ref, o_ref, sib_ref, pair_ref, in_ref, pair_sems, send_sems, recv_sems):
        x, y, c, chips = _mesh_pos()
        me = 2 * x + y
        pair = pltpu.make_async_remote_copy(src_ref=b_ref, dst_ref=sib_ref, send_sem=pair_sems.at[0],
                                            recv_sem=pair_sems.at[1], device_id=(x, y, 1 - c), device_id_type=MESH_ID)
        pair.start()
        pair.wait()
        pair_ref[...] = b_ref[...] + sib_ref[...]
        sends = []
        for j, (cx, cy) in enumerate(chips):
            cp = pltpu.make_async_remote_copy(src_ref=pair_ref, dst_ref=in_ref.at[j], send_sem=send_sems.at[j],
                                              recv_sem=recv_sems.at[j], device_id=(cx, cy, c), device_id_type=MESH_ID)
            cp.start()
            sends.append(cp)
        for cp in sends:
            cp.wait_recv()
        acc = None
        for k in range(N_CHIPS):
            term = jnp.where(me == k, pair_ref[...], in_ref[_peer_slot(k, x, y)])
            acc = term if acc is None else acc + term
        o_ref[...] = acc
        for cp in sends:
            cp.wait_send()

    return pl.pallas_call(
        body, name=name, out_shape=jax.ShapeDtypeStruct(shape, F32),
        in_specs=[pl.BlockSpec(memory_space=pltpu.VMEM)], out_specs=pl.BlockSpec(memory_space=pltpu.VMEM),
        scratch_shapes=[pltpu.VMEM(shape, F32), pltpu.VMEM(shape, F32), pltpu.VMEM((3,) + shape, F32),
                        pltpu.SemaphoreType.DMA((2,)), pltpu.SemaphoreType.DMA((3,)), pltpu.SemaphoreType.DMA((3,))],
        compiler_params=pltpu.CompilerParams(vmem_limit_bytes=VMEM_LIMIT),
    )(buf)


def _adamw_small(ws, gs, ms, vs):
    n = len(ws)
    c1 = 1.0 / (1.0 - ADAM_B1 ** ADAM_STEP)
    c2 = 1.0 / (1.0 - ADAM_B2 ** ADAM_STEP)

    def body(*refs):
        w_r, g_r, m_r, v_r = refs[0:n], refs[n:2 * n], refs[2 * n:3 * n], refs[3 * n:4 * n]
        d_r, nm_r, nv_r = refs[4 * n:5 * n], refs[5 * n:6 * n], refs[6 * n:7 * n]
        for i in range(n):
            gv = g_r[i][...]
            nm = ADAM_B1 * m_r[i][...] + (1.0 - ADAM_B1) * gv
            nv = ADAM_B2 * v_r[i][...] + (1.0 - ADAM_B2) * (gv * gv)
            nm_r[i][...] = nm
            nv_r[i][...] = nv
            d_r[i][...] = (-ADAM_LR) * ((nm * c1) / (jnp.sqrt(nv * c2) + ADAM_EPS) + ADAM_WD * w_r[i][...])

    vm = pl.BlockSpec(memory_space=pltpu.VMEM)
    sds = [jax.ShapeDtypeStruct(w.shape, F32) for w in ws]
    out = pl.pallas_call(body, name="adamw_small", in_specs=[vm] * (4 * n), out_specs=[vm] * (3 * n),
                         out_shape=sds * 3)(*ws, *gs, *ms, *vs)
    return out[0:n], out[n:2 * n], out[2 * n:3 * n]


_BIG = ("w_in", "w_rnn_proj", "w_attn_proj", "w_out", "w_up", "w_down", "w_ple_gate", "w_ple_proj")
_SMALL = ("g_mix", "conv_w", "conv_b", "w_rg", "b_rg", "w_ig", "b_ig", "lru_lambda", "q_gain", "k_gain", "sinks",
          "g_mlp", "g_ple")
_WEIGHTS = ("g_mix", "w_in", "conv_w", "conv_b", "w_rg", "b_rg", "w_ig", "b_ig", "lru_lambda", "w_rnn_proj",
            "q_gain", "k_gain", "sinks", "w_attn_proj", "w_out", "g_mlp", "w_up", "w_down", "g_ple", "w_ple_gate",
            "w_ple_proj")


def _pad_row(v):
    v = v.reshape(1, -1)
    return jnp.pad(v, ((0, 0), (0, D_MODEL - v.shape[1])))


def kernel(x, p, g_mix, w_in, conv_w, conv_b, w_rg, b_rg, w_ig, b_ig, lru_lambda, w_rnn_proj, q_gain, k_gain, sinks, w_attn_proj, w_out, g_mlp, w_up, w_down, g_ple, w_ple_gate, w_ple_proj, loss_target, m_g_mix, m_w_in, m_conv_w, m_conv_b, m_w_rg, m_b_rg, m_w_ig, m_b_ig, m_lru_lambda, m_w_rnn_proj, m_q_gain, m_k_gain, m_sinks, m_w_attn_proj, m_w_out, m_g_mlp, m_w_up, m_w_down, m_g_ple, m_w_ple_gate, m_w_ple_proj, v_g_mix, v_w_in, v_conv_w, v_conv_b, v_w_rg, v_b_rg, v_w_ig, v_b_ig, v_lru_lambda, v_w_rnn_proj, v_q_gain, v_k_gain, v_sinks, v_w_attn_proj, v_w_out, v_g_mlp, v_w_up, v_w_down, v_g_ple, v_w_ple_gate, v_w_ple_proj):
    w = dict(g_mix=g_mix, w_in=w_in, conv_w=conv_w, conv_b=conv_b, w_rg=w_rg, b_rg=b_rg, w_ig=w_ig, b_ig=b_ig,
             lru_lambda=lru_lambda, w_rnn_proj=w_rnn_proj, q_gain=q_gain, k_gain=k_gain, sinks=sinks,
             w_attn_proj=w_attn_proj, w_out=w_out, g_mlp=g_mlp, w_up=w_up, w_down=w_down, g_ple=g_ple,
             w_ple_gate=w_ple_gate, w_ple_proj=w_ple_proj)
    m = dict(g_mix=m_g_mix, w_in=m_w_in, conv_w=m_conv_w, conv_b=m_conv_b, w_rg=m_w_rg, b_rg=m_b_rg, w_ig=m_w_ig,
             b_ig=m_b_ig, lru_lambda=m_lru_lambda, w_rnn_proj=m_w_rnn_proj, q_gain=m_q_gain, k_gain=m_k_gain,
             sinks=m_sinks, w_attn_proj=m_w_attn_proj, w_out=m_w_out, g_mlp=m_g_mlp, w_up=m_w_up, w_down=m_w_down,
             g_ple=m_g_ple, w_ple_gate=m_w_ple_gate, w_ple_proj=m_w_ple_proj)
    v = dict(g_mix=v_g_mix, w_in=v_w_in, conv_w=v_conv_w, conv_b=v_conv_b, w_rg=v_w_rg, b_rg=v_b_rg, w_ig=v_w_ig,
             b_ig=v_b_ig, lru_lambda=v_lru_lambda, w_rnn_proj=v_w_rnn_proj, q_gain=v_q_gain, k_gain=v_k_gain,
             sinks=v_sinks, w_attn_proj=v_w_attn_proj, w_out=v_w_out, g_mlp=v_g_mlp, w_up=v_w_up, w_down=v_w_down,
             g_ple=v_g_ple, w_ple_gate=v_w_ple_gate, w_ple_proj=v_w_ple_proj)
    n_seq, S, _ = x.shape
    T = n_seq * S
    chip = 2 * lax.axis_index("x") + lax.axis_index("y")

    tm, tm_rnn = 512, 256
    xf, pf, tf = x.reshape(T, D_MODEL), p.reshape(T, PLE_DIM), loss_target.reshape(T, D_MODEL)
    first = lambda outs: [o[0] for o in outs]

    w_in_g = _gather_bf16(w["w_in"][0], "gather_w_in")
    wb = {name: w[name][0].astype(BF16) for name in _BIG if name != "w_in"}
    grp_mix, grp_mlp, grp_ple = ("w_rnn_proj", "w_attn_proj", "w_out"), ("w_up", "w_down"), ("w_ple_gate", "w_ple_proj")

    cw_full = jnp.zeros((8, D_MODEL), F32)
    cw_full = lax.dynamic_update_slice(cw_full, conv_w[0], (0, chip * (D_MODEL // N_CHIPS)))
    cw_full = _allreduce_small(0.5 * cw_full.reshape(64, LANES), "allgather_conv_w").reshape(8, D_MODEL)[0:CONV_W]

    cosf, sins = _rope_tables(S)
    ind_q, ind_qt = _indicator(D_MODEL)
    ind_k, ind_kt = _indicator(KV_W)
    wrg2 = _pair_blockdiag(w_rg[0]).astype(BF16)
    wig2 = _pair_blockdiag(w_ig[0]).astype(BF16)
    qg = jnp.tile(q_gain, (1, N_HEADS))
    kg = jnp.tile(k_gain, (1, N_KV))
    sk = sinks.reshape(N_HEADS)
    rnn_w = (cw_full, conv_b, wrg2, b_rg, wig2, b_ig, lru_lambda)
    attn_c = (qg, kg, sk, cosf, sins, ind_q, ind_qt, ind_k, ind_kt, n_seq, S)

    (h0, xr, gr, zq, zk, zv, ga, gb), ph = _inproj_fwd(xf, g_mix, w_in_g, tm,
                                                     phases=[_ph_gather_send(wb[n]) for n in grp_mix])
    g_mixw = first(ph)
    o, ph = _attn_fwd(zq, zk, zv, *attn_c,
                      phases=[_ph_gather_pass(g) for g in g_mixw] + [_ph_gather_send(wb[n]) for n in grp_mlp])
    g_mixw, g_mlpw = first(ph[:3]), first(ph[3:])
    (xc, h, ya), ph = _rnn_fwd(xr, gr, *rnn_w, n_seq, S, tm_rnn,
                               phases=[_ph_gather_pass(g) for g in g_mlpw] + [_ph_gather_send(wb[n]) for n in grp_ple])
    g_mlpw, g_plew = first(ph[:2]), first(ph[2:])
    wr, wa, wo = (g.reshape(D_MODEL, D_MODEL) for g in g_mixw)
    wu, wd = g_mlpw[0], g_mlpw[1].reshape(D_FF, D_MODEL)
    (x1, merged, y_a, y_b), ph = _merge_fwd(xf, ya, o, ga, gb, wr, wa, wo, tm,
                                            phases=[_ph_gather_pass(g) for g in g_plew])
    wpg, wpp = first(ph)
    wpg = wpg.reshape(D_MODEL, D_MODEL)
    (x2, hm, u, act), _ = _mlp_fwd(x1, g_mlp, wu, wd, tm // 2)
    (loss_t, dx2, pb, de, hp, dtg, dg_ple), _ = _ple_loss(x2, pf, tf, g_ple, wpg, wpp, tm)
    loss = lax.psum(loss_t[0, 0], ("x", "y", "c"))

    chipmajor = lambda g: g.reshape(N_CHIPS, g.shape[-2] // N_CHIPS, g.shape[-1]) if g.ndim == 2 else g
    part_ple = [chipmajor(_wgrad(hp, dtg, "wgrad_ple_gate", False, D_MODEL, tm)[0]),
                _wgrad(pb, de, "wgrad_ple_proj", True, D_MODEL // N_CHIPS, tm)[0]]
    (dx1, du, dg_mlp), ph = _mlp_bwd(dx2, u, x1, g_mlp, wu, wd, tm // 2, phases=[_ph_pair_send(g) for g in part_ple])
    send_ple, own_ple = _pair_sum(part_ple, first(ph), "pair_sum_ple")
    dw_down, ph = _wgrad(act, dx2, "wgrad_down", False, D_MODEL // 2, tm, phases=[_ph_chip_send(s) for s in send_ple])
    red_ple = _chip_sum(own_ple, first(ph), "chip_sum_ple")
    part_mlp = [_wgrad(hm, du, "wgrad_up", True, D_MODEL, tm)[0], chipmajor(dw_down)]
    (dga, dgb, dya, dyb, dyain, do), ph = _merge_bwd(
        dx1, ga, gb, y_a, y_b, wr, wa, wo, tm,
        phases=[_ph_half_swap(r) for r in red_ple] + [_ph_pair_send(g) for g in part_mlp])
    red_ple = first(ph[:2])
    send_mlp, own_mlp = _pair_sum(part_mlp, first(ph[2:]), "pair_sum_mlp")
    part_mix = [chipmajor(_wgrad(ya, dya, "wgrad_rnn_proj", False, D_MODEL, tm)[0]),
                chipmajor(_wgrad(o, dyb, "wgrad_attn_proj", False, D_MODEL, tm)[0]),
                chipmajor(_wgrad(merged, dx1, "wgrad_out", False, D_MODEL, tm)[0])]
    (dxr, dgr, vec, dwrg2, dwig2), ph = _rnn_bwd(
        dyain, xr, gr, xc, h, cw_full, wrg2, b_rg, wig2, b_ig, lru_lambda, n_seq, S, tm_rnn,
        phases=[_ph_chip_send(s) for s in send_mlp] + [_ph_pair_send(g) for g in part_mix])
    red_mlp = _chip_sum(own_mlp, first(ph[:2]), "chip_sum_mlp")
    send_mix, own_mix = _pair_sum(part_mix, first(ph[2:]), "pair_sum_mix")
    (dq, dkc, dkp, dvc, dvp, dqg, dsk), ph = _attn_bwd(
        do, zq, zk, zv, *attn_c, phases=[_ph_half_swap(r) for r in red_mlp] + [_ph_chip_send(s) for s in send_mix])
    red_mlp = first(ph[:2])
    red_mix = _chip_sum(own_mix, first(ph[2:]), "chip_sum_mix")
    (dk, dv, dkg), _ = _kv_bwd(dkc, dkp, dvc, dvp, zk, kg, cosf, sins, ind_k, ind_kt, n_seq, S)
    dz_parts = [dxr, dgr, dq, dk, dv, dga, dgb]
    send_in, own_in = _pair_exchange_sum(_wgrad_in(h0, dz_parts, tm), "pair_sum_in")
    (grad_x, dg_mix), ph = _inproj_bwd(dz_parts, w_in_g, xf, g_mix, dx1, tm,
                                       phases=[_ph_half_swap(r) for r in red_mix] + [_ph_chip_send(send_in)])
    red_mix = first(ph[:3])
    red_in = _chip_sum([own_in], first(ph[3:]), "chip_sum_in")
    reduced = dict(zip(grp_ple + grp_mlp + grp_mix, red_ple + red_mlp + red_mix))
    grads = {
        "g_mix": dg_mix[0], "g_mlp": dg_mlp[0], "g_ple": dg_ple[0],
        "conv_w": vec[0:CONV_W], "conv_b": vec[4], "b_rg": vec[5], "b_ig": vec[6], "lru_lambda": vec[7],
        "w_rg": _pair_blockdiag_extract(dwrg2), "w_ig": _pair_blockdiag_extract(dwig2),
        "q_gain": dqg.reshape(N_HEADS, HEAD_DIM).sum(0), "k_gain": dkg.reshape(N_KV, HEAD_DIM).sum(0),
        "sinks": dsk.sum(0)[:N_HEADS],
    }

    rows = [grads["conv_w"], _pad_row(grads["conv_b"]), _pad_row(grads["b_rg"]), _pad_row(grads["b_ig"]),
            _pad_row(grads["lru_lambda"]), _pad_row(grads["g_mix"]), _pad_row(grads["g_mlp"]),
            _pad_row(grads["g_ple"]), _pad_row(grads["q_gain"]), _pad_row(grads["k_gain"]), _pad_row(grads["sinks"]),
            jnp.zeros((2, D_MODEL), F32)]
    vecs = jnp.concatenate(rows, axis=0)
    packed = jnp.concatenate([vecs.reshape(-1, LANES), grads["w_rg"].reshape(-1, LANES),
                              grads["w_ig"].reshape(-1, LANES)], axis=0)
    red = _allreduce_small(packed, "allreduce_small")
    nv = vecs.size // LANES
    rvec = red[0:nv].reshape(16, D_MODEL)
    nw = grads["w_rg"].size // LANES
    sg = {
        "conv_w": lax.dynamic_slice(rvec[0:CONV_W], (0, chip * (D_MODEL // N_CHIPS)), (CONV_W, D_MODEL // N_CHIPS)),
        "conv_b": rvec[4], "b_rg": rvec[5], "b_ig": rvec[6], "lru_lambda": rvec[7], "g_mix": rvec[8],
        "g_mlp": rvec[9], "g_ple": rvec[10], "q_gain": rvec[11, :HEAD_DIM], "k_gain": rvec[12, :HEAD_DIM],
        "sinks": rvec[13, :N_HEADS], "w_rg": red[nv:nv + nw], "w_ig": red[nv + nw:nv + 2 * nw],
    }
    sg = {k: sg[k].reshape(w[k].shape) for k in _SMALL}
    d_s, m_s, v_s = _adamw_small([w[k] for k in _SMALL], [sg[k] for k in _SMALL], [m[k] for k in _SMALL],
                                 [v[k] for k in _SMALL])
    grad, delta, new_m, new_v = dict(sg), dict(zip(_SMALL, d_s)), dict(zip(_SMALL, m_s)), dict(zip(_SMALL, v_s))

    for name in ("w_up", "w_down", "w_rnn_proj", "w_attn_proj", "w_out", "w_ple_gate", "w_ple_proj", "w_in"):
        shape = w[name].shape
        outs, ph = _adamw(w[name][0], reduced[name], m[name][0], v[name][0], "adamw_" + name, 128,
                          phases=[_ph_half_swap(r) for r in red_in] if name == "w_up" else ())
        if name == "w_up":
            reduced["w_in"] = ph[0][0]
        grad[name], delta[name], new_m[name], new_v[name] = (a.reshape(shape) for a in outs)

    return (loss, grad_x.reshape(x.shape), *[grad[k] for k in _WEIGHTS], *[delta[k] for k in _WEIGHTS],
            *[new_m[k] for k in _WEIGHTS], *[new_v[k] for k in _WEIGHTS])
```

```python
import functools
import math

import numpy as np
import jax
import jax.numpy as jnp
from jax import lax
from jax.experimental import pallas as pl
from jax.experimental.pallas import tpu as pltpu

F32 = jnp.float32
BF16 = jnp.bfloat16

D_MODEL = 1024
N_HEADS = 16
N_KV = 4
HEAD_DIM = 64
KV_W = N_KV * HEAD_DIM
D_FF = 4096
PLE_DIM = 256
WINDOW = 128
CONV_W = 4
LRU_C = 8.0
NORM_EPS = 1e-6
ROPE_THETA = 10000.0
N_CHIPS = 4
IN_TOTAL = 5632
IN_BLK = IN_TOTAL // N_CHIPS
IN_SEGS = (0, 1024, 2048, 3072, 3328, 3584, 4608, 5632)

ADAM_LR = 0.001
ADAM_B1 = 0.9
ADAM_B2 = 0.999
ADAM_EPS = 1e-08
ADAM_WD = 0.01
ADAM_STEP = 10

LANES = 128
VMEM_LIMIT = 56 * 1024 * 1024
MESH_ID = pl.DeviceIdType.MESH


def _dot(a, b):
    return jnp.dot(a, b, preferred_element_type=F32)


def _dot_nt(a, b):
    return lax.dot_general(a, b, (((1,), (1,)), ((), ())), preferred_element_type=F32)


def _dot_tn(a, b):
    return lax.dot_general(a, b, (((0,), (0,)), ((), ())), preferred_element_type=F32)


def _split_dot(x, ind):
    hi = x.astype(BF16)
    lo = (x - hi.astype(F32)).astype(BF16)
    return _dot(hi, ind) + _dot(lo, ind)


def _sigmoid(x):
    return 1.0 / (1.0 + jnp.exp(-x))


_GELU_C = math.sqrt(2.0 / math.pi)


def _gelu_and_grad(g):
    inner = _GELU_C * (g + 0.044715 * g * g * g)
    t = jnp.tanh(inner)
    gelu = 0.5 * g * (1.0 + t)
    dgelu = 0.5 * (1.0 + t) + 0.5 * g * (1.0 - t * t) * _GELU_C * (1.0 + 3.0 * 0.044715 * g * g)
    return gelu, dgelu


def _const(shape):
    nd = len(shape)
    return pl.BlockSpec(shape, lambda *_: (0,) * nd)


def _params(n_grid, vmem=VMEM_LIMIT):
    return pltpu.CompilerParams(dimension_semantics=("arbitrary",) * n_grid, vmem_limit_bytes=vmem)


def _rms_fwd(x, g):
    r = lax.rsqrt(jnp.mean(x * x, axis=-1, keepdims=True) + NORM_EPS)
    return (x * r) * g, r


def _rms_bwd(dy, x, r, g):
    dn = dy * g
    dx = r * dn - x * (r * r * r * jnp.mean(dn * x, axis=-1, keepdims=True))
    dg = jnp.sum(dy * (x * r), axis=0, keepdims=True)
    return dx, dg


def _seg_pieces(blk_lo, blk_hi):
    out = []
    for s in range(7):
        lo, hi = max(blk_lo, IN_SEGS[s]), min(blk_hi, IN_SEGS[s + 1])
        if lo < hi:
            out.append((s, lo - IN_SEGS[s], hi - IN_SEGS[s], lo - blk_lo))
    return out


def _mesh_pos():
    x, y, c = lax.axis_index("x"), lax.axis_index("y"), lax.axis_index("c")
    other_chips = [(1 - x, y), (x, 1 - y), (1 - x, 1 - y)]
    return x, y, c, other_chips


def _peer_slot(k, x, y):
    dx = jnp.bitwise_xor(k // 2, x)
    dy = jnp.bitwise_xor(k % 2, y)
    return jnp.maximum(dx + 2 * dy - 1, 0)


def _half_rows(c, R):
    return pl.ds(pl.multiple_of(c * R, R), R), pl.ds(pl.multiple_of((1 - c) * R, R), R)


def _remote(src, dst, sems, to):
    return pltpu.make_async_remote_copy(src_ref=src, dst_ref=dst, send_sem=sems[0], recv_sem=sems[1],
                                        device_id=to, device_id_type=MESH_ID)


class _Phase:
    def __init__(self, ins, inout, outs, n_remote, n_local, build):
        self.ins, self.inout, self.outs = list(ins), list(inout), list(outs)
        self.n_remote, self.n_local, self.build = n_remote, n_local, build


def _ph_gather_send(wb):
    R2, C = wb.shape
    R = R2 // 2

    def build(ins, outs, rsem, lsem):
        (w_ref,), (g_ref,) = ins, outs
        x, y, c, chips = _mesh_pos()
        me = 2 * x + y
        mine, _ = _half_rows(c, R)
        loc = [pltpu.make_async_copy(w_ref, g_ref.at[me], lsem(0))]
        outg = [_remote(w_ref.at[mine], g_ref.at[me, mine], rsem(j), (cx, cy, c)) for j, (cx, cy) in enumerate(chips)]
        inc = [functools.partial(_remote, w_ref.at[mine], g_ref.at[2 * cx + cy, mine], rsem(j), (x, y, c))
               for j, (cx, cy) in enumerate(chips)]
        return loc, outg, inc

    return _Phase([wb], [], [jax.ShapeDtypeStruct((N_CHIPS, R2, C), BF16)], 3, 1, build)


def _ph_gather_pass(gath):
    _, R2, C = gath.shape
    R = R2 // 2

    def build(ins, outs, rsem, lsem):
        (g_ref,) = outs
        x, y, c, chips = _mesh_pos()
        mine, theirs = _half_rows(c, R)
        outg, inc = [], []
        for j, (cx, cy) in enumerate(chips):
            blk = g_ref.at[2 * cx + cy, mine]
            outg.append(_remote(blk, blk, rsem(j), (x, y, 1 - c)))
            got = g_ref.at[2 * cx + cy, theirs]
            inc.append(functools.partial(_remote, got, got, rsem(j), (x, y, c)))
        return [], outg, inc

    return _Phase([], [gath], [], 3, 0, build)


def _ph_pair_send(partial):
    _, R2, C = partial.shape
    R = R2 // 2

    def build(ins, outs, rsem, lsem):
        (p_ref,), (s_ref,) = ins, outs
        x, y, c, _ = _mesh_pos()
        _, theirs = _half_rows(c, R)
        src = p_ref.at[:, theirs, :]
        return ([], [_remote(src, s_ref, rsem(0), (x, y, 1 - c))],
                [functools.partial(_remote, src, s_ref, rsem(0), (x, y, c))])

    return _Phase([partial], [], [jax.ShapeDtypeStruct((N_CHIPS, R, C), F32)], 1, 0, build)


def _ph_chip_send(sendb):
    def build(ins, outs, rsem, lsem):
        (s_ref,), (r_ref,) = ins, outs
        x, y, c, chips = _mesh_pos()
        outg = [_remote(s_ref.at[j], r_ref.at[j], rsem(j), (cx, cy, c)) for j, (cx, cy) in enumerate(chips)]
        inc = [functools.partial(_remote, s_ref.at[j], r_ref.at[j], rsem(j), (x, y, c)) for j in range(3)]
        return [], outg, inc

    return _Phase([sendb], [], [jax.ShapeDtypeStruct(sendb.shape, sendb.dtype)], 3, 0, build)


def _ph_half_swap(red):
    R2, C = red.shape
    R = R2 // 2

    def build(ins, outs, rsem, lsem):
        (r_ref,) = outs
        x, y, c, _ = _mesh_pos()
        mine, theirs = _half_rows(c, R)
        return ([], [_remote(r_ref.at[mine], r_ref.at[mine], rsem(0), (x, y, 1 - c))],
                [functools.partial(_remote, r_ref.at[theirs], r_ref.at[theirs], rsem(0), (x, y, c))])

    return _Phase([], [red], [], 1, 0, build)


def _call(body, *, name, grid, in_specs, out_specs, out_shape, scratch_shapes=(), phases=()):
    single = not isinstance(out_specs, (list, tuple))
    out_specs = [out_specs] if single else list(out_specs)
    out_shape = [out_shape] if single else list(out_shape)
    n_in, n_out, n_scr = len(in_specs), len(out_specs), len(scratch_shapes)
    if not phases:
        call = pl.pallas_call(body, name=name, grid=grid, in_specs=in_specs, out_specs=out_specs,
                              out_shape=out_shape, scratch_shapes=list(scratch_shapes),
                              compiler_params=_params(len(grid)))
        return lambda *operands: (list(call(*operands)), [])

    ex_in, ex_out, aliases, spans = [], [], {}, []
    for ph in phases:
        i0, o0 = len(ex_in), len(ex_out)
        ex_in += ph.ins
        for a in ph.inout:
            aliases[n_in + len(ex_in)] = n_out + len(ex_out)
            ex_in.append(a)
            ex_out.append(jax.ShapeDtypeStruct(a.shape, a.dtype))
        ex_out += ph.outs
        spans.append((i0, len(ph.ins), o0, len(ex_out) - o0))
    n_remote = sum(ph.n_remote for ph in phases)
    n_local = max(sum(ph.n_local for ph in phases), 1)

    def wrapped(*refs):
        base_in, xin = refs[:n_in], refs[n_in:n_in + len(ex_in)]
        o0 = n_in + len(ex_in)
        base_out, xout = refs[o0:o0 + n_out], refs[o0 + n_out:o0 + n_out + len(ex_out)]
        scr = refs[o0 + n_out + len(ex_out):]
        send_sems, recv_sems, loc_sems = scr[n_scr:]
        first = functools.reduce(jnp.logical_and, [pl.program_id(i) == 0 for i in range(len(grid))])
        last = functools.reduce(jnp.logical_and, [pl.program_id(i) == grid[i] - 1 for i in range(len(grid))])

        def copies():
            out, r0, l0 = [], 0, 0
            for ph, (i0, ni, p0, no) in zip(phases, spans):
                rsem = lambda k, r0=r0: (send_sems.at[r0 + k], recv_sems.at[r0 + k])
                lsem = lambda k, l0=l0: loc_sems.at[l0 + k]
                out.append(ph.build(xin[i0:i0 + ni], xout[p0:p0 + no], rsem, lsem))
                r0, l0 = r0 + ph.n_remote, l0 + ph.n_local
            return out

        @pl.when(first)
        def _():
            for loc, outg, _ in copies():
                for cp in loc + outg:
                    cp.start()

        body(*base_in, *base_out, *scr[:n_scr])

        @pl.when(last)
        def _():
            for loc, outg, inc in copies():
                for make in inc:
                    make().wait_recv()
                for cp in outg:
                    cp.wait_send()
                for cp in loc:
                    cp.wait()

    hbm = pl.BlockSpec(memory_space=pl.ANY)
    call = pl.pallas_call(
        wrapped, name=name, grid=grid, in_specs=list(in_specs) + [hbm] * len(ex_in),
        out_specs=out_specs + [hbm] * len(ex_out), out_shape=out_shape + ex_out,
        scratch_shapes=list(scratch_shapes) + [pltpu.SemaphoreType.DMA((n_remote,)), pltpu.SemaphoreType.DMA((n_remote,)),
                                              pltpu.SemaphoreType.DMA((n_local,))],
        input_output_aliases=aliases, compiler_params=_params(len(grid)))

    def run(*operands):
        res = call(*operands, *ex_in)
        extra = res[n_out:]
        return list(res[:n_out]), [list(extra[p0:p0 + no]) for (_, _, p0, no) in spans]

    return run


def _inproj_fwd(x, g_mix, w_in, tm, phases=()):
    T = x.shape[0]
    widths = [IN_SEGS[i + 1] - IN_SEGS[i] for i in range(7)]

    def body(x_ref, g_ref, w_ref, h_ref, *z_refs):
        h, _ = _rms_fwd(x_ref[...], g_ref[...])
        hb = h.astype(BF16)
        h_ref[...] = hb
        for j in range(N_CHIPS):
            zj = _dot(hb, w_ref[j])
            for s, lo, hi, off in _seg_pieces(j * IN_BLK, (j + 1) * IN_BLK):
                z_refs[s][:, lo:hi] = zj[:, off:off + hi - lo]

    return _call(
        body, phases=phases, name="inproj_fwd", grid=(T // tm,),
        in_specs=[pl.BlockSpec((tm, D_MODEL), lambda i: (i, 0)), _const((1, D_MODEL)),
                  _const((N_CHIPS, D_MODEL, IN_BLK))],
        out_specs=[pl.BlockSpec((tm, D_MODEL), lambda i: (i, 0))]
        + [pl.BlockSpec((tm, w), lambda i: (i, 0)) for w in widths],
        out_shape=[jax.ShapeDtypeStruct((T, D_MODEL), BF16)]
        + [jax.ShapeDtypeStruct((T, w), F32) for w in widths],
    )(x, g_mix, w_in)


def _inproj_bwd(dz_parts, w_in, x, g_mix, dx1, tm, phases=()):
    T = x.shape[0]
    widths = [IN_SEGS[i + 1] - IN_SEGS[i] for i in range(7)]

    def body(*refs):
        p_refs = refs[:7]
        w_ref, x_ref, g_ref, dx1_ref, gx_ref, dg_ref, dz_ref = refs[7:]

        @pl.when(pl.program_id(0) == 0)
        def _():
            dg_ref[...] = jnp.zeros_like(dg_ref)

        for s in range(7):
            dz_ref[:, IN_SEGS[s]:IN_SEGS[s + 1]] = p_refs[s][...]
        dh = jnp.zeros((tm, D_MODEL), F32)
        for j in range(N_CHIPS):
            dh = dh + _dot_nt(dz_ref[:, j * IN_BLK:(j + 1) * IN_BLK], w_ref[j])
        xv = x_ref[...]
        g = g_ref[...]
        _, r = _rms_fwd(xv, g)
        dx, dg = _rms_bwd(dh, xv, r, g)
        gx_ref[...] = dx1_ref[...] + dx
        dg_ref[...] += dg

    row = lambda w: pl.BlockSpec((tm, w), lambda i: (i, 0))
    return _call(
        body, phases=phases, name="inproj_bwd", grid=(T // tm,),
        in_specs=[row(w) for w in widths]
        + [_const((N_CHIPS, D_MODEL, IN_BLK)), row(D_MODEL), _const((1, D_MODEL)), row(D_MODEL)],
        out_specs=[row(D_MODEL), _const((1, D_MODEL))],
        out_shape=[jax.ShapeDtypeStruct((T, D_MODEL), F32), jax.ShapeDtypeStruct((1, D_MODEL), F32)],
        scratch_shapes=[pltpu.VMEM((tm, IN_TOTAL), BF16)],
    )(*dz_parts, w_in, x, g_mix, dx1)


def _wgrad_in(h0, dz_parts, tm):
    T = h0.shape[0]
    widths = [IN_SEGS[i + 1] - IN_SEGS[i] for i in range(7)]

    def body(*refs):
        h_ref, p_refs, o_ref, acc_ref, sem = refs[0], refs[1:8], refs[8], refs[9], refs[10]
        t = pl.program_id(0)

        @pl.when(t == 0)
        def _():
            acc_ref[...] = jnp.zeros_like(acc_ref)

        hv = h_ref[...]
        for j in range(N_CHIPS):
            for s, lo, hi, off in _seg_pieces(j * IN_BLK, (j + 1) * IN_BLK):
                acc_ref[j, :, off:off + hi - lo] += _dot_tn(hv, p_refs[s][:, lo:hi])

        @pl.when(t == T // tm - 1)
        def _():
            cp = pltpu.make_async_copy(acc_ref, o_ref, sem)
            cp.start()
            cp.wait()

    row = lambda w: pl.BlockSpec((tm, w), lambda i: (i, 0))
    return pl.pallas_call(
        body, name="wgrad_in", grid=(T // tm,), in_specs=[row(D_MODEL)] + [row(w) for w in widths],
        out_specs=pl.BlockSpec(memory_space=pl.ANY),
        out_shape=jax.ShapeDtypeStruct((N_CHIPS, D_MODEL, IN_BLK), F32),
        scratch_shapes=[pltpu.VMEM((N_CHIPS, D_MODEL, IN_BLK), F32), pltpu.SemaphoreType.DMA],
        compiler_params=_params(1),
    )(h0, *dz_parts)


def _wgrad(a, g, name, blocked, cn, tm, phases=()):
    T, K = a.shape
    N = g.shape[1]
    nb = N // cn

    def body(a_ref, g_ref, o_ref):
        @pl.when(pl.program_id(1) == 0)
        def _():
            o_ref[...] = jnp.zeros_like(o_ref)

        o_ref[...] += _dot_tn(a_ref[...].astype(BF16), g_ref[...].astype(BF16))

    if blocked:
        out_spec = pl.BlockSpec((None, K, cn), lambda j, t: (j, 0, 0))
        out_shape = jax.ShapeDtypeStruct((nb, K, cn), F32)
    else:
        out_spec = pl.BlockSpec((K, cn), lambda j, t: (0, j))
        out_shape = jax.ShapeDtypeStruct((K, N), F32)
    outs, extra = _call(
        body, phases=phases, name=name, grid=(nb, T // tm),
        in_specs=[pl.BlockSpec((tm, K), lambda j, t: (t, 0)), pl.BlockSpec((tm, cn), lambda j, t: (t, j))],
        out_specs=out_spec, out_shape=out_shape,
    )(a, g)
    return outs[0], extra


def _shift_down(x, prev8, sft, row, row8, tm):
    xs = pltpu.roll(x, sft, 0)
    top = jnp.where(row8 < sft, pltpu.roll(prev8, sft, 0), xs[0:8])
    return jnp.concatenate([top, xs[8:]], axis=0)


def _shift_up(x, next8, sft, row8, tm):
    xs = pltpu.roll(x, tm - sft, 0)
    bot = jnp.where(row8 >= 8 - sft, pltpu.roll(next8, 8 - sft, 0), xs[tm - 8:tm])
    return jnp.concatenate([xs[0:tm - 8], bot], axis=0)


def _conv_fwd(x, prev8, cw_ref, cb, row, row8, tm):
    xc = cb + cw_ref[CONV_W - 1:CONV_W, :] * x
    for sft in range(1, CONV_W):
        j = CONV_W - 1 - sft
        xc = xc + cw_ref[j:j + 1, :] * _shift_down(x, prev8, sft, row, row8, tm)
    return xc


def _blockdiag_dot(xb, w_ref, transpose):
    outs = []
    for b in range(D_MODEL // LANES):
        xs = xb[:, b * LANES:(b + 1) * LANES]
        outs.append(_dot_nt(xs, w_ref[b]) if transpose else _dot(xs, w_ref[b]))
    return jnp.concatenate(outs, axis=1)


def _softplus_neg(lam):
    e = jnp.exp(-jnp.abs(lam))
    u = 1.0 + e
    log1p_e = jnp.where(u == 1.0, e, jnp.log(u) * (e / (u - 1.0)))
    sp = jnp.maximum(-lam, 0.0) + log1p_e
    return sp, -_sigmoid(-lam)


def _lru_gates(xc, wrg_ref, brg, wig_ref, big, sp):
    xcb = xc.astype(BF16)
    r = _sigmoid(_blockdiag_dot(xcb, wrg_ref, False) + brg)
    i = _sigmoid(_blockdiag_dot(xcb, wig_ref, False) + big)
    log_a = (-LRU_C) * r * sp
    a = jnp.exp(log_a)
    t = jnp.tanh(log_a)
    one_m_a2 = (-2.0) * t / (1.0 - t)
    mult = jnp.sqrt(one_m_a2)
    return xcb, r, i, a, mult


def _scan_down(a, b, row, tm):
    d = 1
    while d < tm:
        keep = row >= d
        a_s = jnp.where(keep, pltpu.roll(a, d, 0), 1.0)
        b_s = jnp.where(keep, pltpu.roll(b, d, 0), 0.0)
        b = a * b_s + b
        a = a * a_s
        d *= 2
    return a, b


def _scan_up(c, b, row, tm):
    d = 1
    while d < tm:
        keep = row < tm - d
        c_s = jnp.where(keep, pltpu.roll(c, tm - d, 0), 1.0)
        b_s = jnp.where(keep, pltpu.roll(b, tm - d, 0), 0.0)
        b = c * b_s + b
        c = c * c_s
        d *= 2
    return c, b


def _rnn_fwd(xr, gr, conv_w, conv_b, wrg2, b_rg, wig2, b_ig, lam, n_seq, S, tm, phases=()):
    T = xr.shape[0]
    nt = S // tm
    W = D_MODEL

    def body(xr_ref, gr_ref, cw_ref, cb_ref, wrg_ref, brg_ref, wig_ref, big_ref, lam_ref,
             xc_ref, h_ref, ya_ref, px_ref, ph_ref):
        @pl.when(pl.program_id(1) == 0)
        def _():
            px_ref[...] = jnp.zeros_like(px_ref)
            ph_ref[...] = jnp.zeros_like(ph_ref)

        row = lax.broadcasted_iota(jnp.int32, (tm, W), 0)
        row8 = lax.broadcasted_iota(jnp.int32, (8, W), 0)
        x = xr_ref[...]
        xc = _conv_fwd(x, px_ref[...], cw_ref, cb_ref[...], row, row8, tm)
        sp, _ = _softplus_neg(lam_ref[...])
        _, r, i, a, mult = _lru_gates(xc, wrg_ref, brg_ref[...], wig_ref, big_ref[...], sp)
        bterm = mult * (i * xc)
        acum, hloc = _scan_down(a, bterm, row, tm)
        h = hloc + acum * ph_ref[7:8, :]
        h_ref[...] = h
        xc_ref[...] = xc
        gelu, _ = _gelu_and_grad(gr_ref[...])
        ya_ref[...] = (h * gelu).astype(BF16)
        px_ref[...] = xr_ref[tm - 8:tm, :]
        ph_ref[...] = h_ref[tm - 8:tm, :]

    tile = pl.BlockSpec((tm, W), lambda s, t: (s * nt + t, 0))
    return _call(
        body, phases=phases, name="rnn_fwd", grid=(n_seq, nt),
        in_specs=[tile, tile, _const((CONV_W, W)), _const((1, W)), _const((8, LANES, LANES)), _const((1, W)),
                  _const((8, LANES, LANES)), _const((1, W)), _const((1, W))],
        out_specs=[tile, tile, tile],
        out_shape=[jax.ShapeDtypeStruct((T, W), F32), jax.ShapeDtypeStruct((T, W), F32),
                   jax.ShapeDtypeStruct((T, W), BF16)],
        scratch_shapes=[pltpu.VMEM((8, W), F32), pltpu.VMEM((8, W), F32)],
    )(xr, gr, conv_w, conv_b, wrg2, b_rg, wig2, b_ig, lam)


def _rnn_bwd(dya, xr, gr, xc, h, conv_w, wrg2, b_rg, wig2, b_ig, lam, n_seq, S, tm, phases=()):
    T = xr.shape[0]
    nt = S // tm
    W = D_MODEL
    nb8 = tm // 8

    def body(dya_ref, xr_ref, gr_ref, xc_ref, h_ref, xprev_ref, hprev_ref, cw_ref, wrg_ref, brg_ref, wig_ref,
             big_ref, lam_ref, dxr_ref, dgr_ref, vec_ref, dwrg_ref, dwig_ref, cg_ref, ndxc_ref, tmp_ref):
        s, ti = pl.program_id(0), pl.program_id(1)

        @pl.when((s == 0) & (ti == 0))
        def _():
            vec_ref[...] = jnp.zeros_like(vec_ref)
            dwrg_ref[...] = jnp.zeros_like(dwrg_ref)
            dwig_ref[...] = jnp.zeros_like(dwig_ref)

        @pl.when(ti == 0)
        def _():
            cg_ref[...] = jnp.zeros_like(cg_ref)
            ndxc_ref[...] = jnp.zeros_like(ndxc_ref)

        first = ti == nt - 1
        row = lax.broadcasted_iota(jnp.int32, (tm, W), 0)
        row8 = lax.broadcasted_iota(jnp.int32, (8, W), 0)
        x = xr_ref[...]
        xc = xc_ref[...]
        hv = h_ref[...]
        xprev = jnp.where(first, 0.0, xprev_ref[...])
        hprev = jnp.where(first, 0.0, hprev_ref[...])
        sp, dsp_dlam = _softplus_neg(lam_ref[...])
        xcb, r, i, a, mult = _lru_gates(xc, wrg_ref, brg_ref[...], wig_ref, big_ref[...], sp)

        gelu, dgelu = _gelu_and_grad(gr_ref[...])
        dya_v = dya_ref[...]
        dgr_ref[...] = (dya_v * hv * dgelu).astype(BF16)
        dh = dya_v * gelu
        c = jnp.where(row < tm - 1, pltpu.roll(a, tm - 1, 0), 1.0)
        ccum, gloc = _scan_up(c, dh, row, tm)
        G = gloc + ccum * cg_ref[0:1, :]
        tmp_ref[...] = a * G
        cg_ref[...] = tmp_ref[0:8, :]

        h_m1 = _shift_down(hv, hprev, 1, row, row8, tm)
        ixc = i * xc
        dixc = G * mult
        dlog_a = (G * h_m1) * a - (G * ixc) * (a * a / mult)
        dr = dlog_a * ((-LRU_C) * sp)
        di = dixc * xc
        drg = dr * r * (1.0 - r)
        dig = di * i * (1.0 - i)
        vec_ref[7:8, :] += jnp.sum(dlog_a * ((-LRU_C) * r), axis=0, keepdims=True) * dsp_dlam
        vec_ref[5:6, :] += jnp.sum(drg, axis=0, keepdims=True)
        vec_ref[6:7, :] += jnp.sum(dig, axis=0, keepdims=True)
        drgb = drg.astype(BF16)
        digb = dig.astype(BF16)
        dxc = dixc * i + _blockdiag_dot(drgb, wrg_ref, True) + _blockdiag_dot(digb, wig_ref, True)
        for b in range(W // LANES):
            sl = slice(b * LANES, (b + 1) * LANES)
            dwrg_ref[b] += _dot_tn(xcb[:, sl], drgb[:, sl])
            dwig_ref[b] += _dot_tn(xcb[:, sl], digb[:, sl])

        vec_ref[4:5, :] += jnp.sum(dxc, axis=0, keepdims=True)
        vec_ref[3:4, :] += jnp.sum(dxc * x, axis=0, keepdims=True)
        dxr = cw_ref[CONV_W - 1:CONV_W, :] * dxc
        nxt = ndxc_ref[...]
        for sft in range(1, CONV_W):
            j = CONV_W - 1 - sft
            vec_ref[j:j + 1, :] += jnp.sum(dxc * _shift_down(x, xprev, sft, row, row8, tm), axis=0, keepdims=True)
            dxr = dxr + cw_ref[j:j + 1, :] * _shift_up(dxc, nxt, sft, row8, tm)
        dxr_ref[...] = dxr.astype(BF16)
        tmp_ref[...] = dxc
        ndxc_ref[...] = tmp_ref[0:8, :]

    rev = lambda s, t: (s * nt + nt - 1 - t, 0)
    tile = pl.BlockSpec((tm, W), rev)
    prev8 = pl.BlockSpec((8, W), lambda s, t: (jnp.maximum((s * nt + nt - 1 - t) * nb8 - 1, 0), 0))
    return _call(
        body, phases=phases, name="rnn_bwd", grid=(n_seq, nt),
        in_specs=[tile, tile, tile, tile, tile, prev8, prev8, _const((CONV_W, W)), _const((8, LANES, LANES)),
                  _const((1, W)), _const((8, LANES, LANES)), _const((1, W)), _const((1, W))],
        out_specs=[tile, tile, _const((16, W)), _const((8, LANES, LANES)), _const((8, LANES, LANES))],
        out_shape=[jax.ShapeDtypeStruct((T, W), BF16), jax.ShapeDtypeStruct((T, W), BF16),
                   jax.ShapeDtypeStruct((16, W), F32), jax.ShapeDtypeStruct((8, LANES, LANES), F32),
                   jax.ShapeDtypeStruct((8, LANES, LANES), F32)],
        scratch_shapes=[pltpu.VMEM((8, W), F32), pltpu.VMEM((8, W), F32), pltpu.VMEM((tm, W), F32)],
    )(dya, xr, gr, xc, h, xr, h, conv_w, wrg2, b_rg, wig2, b_ig, lam)


def _head_swap(t, lane):
    w = t.shape[1]
    return jnp.where(lane % HEAD_DIM < HEAD_DIM // 2, pltpu.roll(t, w - HEAD_DIM // 2, 1),
                     pltpu.roll(t, HEAD_DIM // 2, 1))


def _qk_prep(t, gain, cosf, sins, ind, indt, lane):
    ms = _split_dot(t * t, ind) * (1.0 / HEAD_DIM)
    rstd = _split_dot(lax.rsqrt(ms + NORM_EPS), indt)
    tn = (t * rstd) * gain
    return tn * cosf + _head_swap(tn, lane) * sins, rstd


def _qk_prep_bwd(dy, t, rstd, gain, cosf, sins, ind, indt, lane):
    dtn = dy * cosf + _head_swap(dy * sins, lane)
    dgain = jnp.sum(dtn * (t * rstd), axis=0, keepdims=True)
    dn = dtn * gain
    m = _split_dot(_split_dot(dn * t, ind), indt) * (1.0 / HEAD_DIM)
    return rstd * dn - t * (rstd * rstd * rstd * m), dgain


def _attn_mask(blk_idx):
    qi = lax.broadcasted_iota(jnp.int32, (WINDOW, 2 * WINDOW), 0)
    ci = lax.broadcasted_iota(jnp.int32, (WINDOW, 2 * WINDOW), 1)
    diff = WINDOW + qi - ci
    return (diff >= 0) & (diff < WINDOW) & ((ci >= WINDOW) | (blk_idx > 0))


def _stack_heads(t, kvh, lo):
    parts = []
    for i in (2 * kvh, 2 * kvh + 1):
        tp = t[:, i * LANES:(i + 1) * LANES]
        parts += [jnp.where(lo, tp, 0.0), jnp.where(lo, 0.0, tp)]
    return jnp.concatenate(parts, axis=0).astype(BF16)


def _unstack_heads(ts, lo):
    w = WINDOW
    return jnp.where(lo, ts[0:w], ts[w:2 * w]), jnp.where(lo, ts[2 * w:3 * w], ts[3 * w:4 * w])


def _dup_head(t, kvh, lo2):
    m = kvh // 2
    t2 = t[:, m * LANES:(m + 1) * LANES]
    t2r = pltpu.roll(t2, HEAD_DIM, 1)
    return (jnp.where(lo2, t2, t2r) if kvh % 2 == 0 else jnp.where(lo2, t2r, t2)).astype(BF16)


def _fold_head(ts, kvh, lo2):
    tot = ts + pltpu.roll(ts, HEAD_DIM, 1)
    own = lo2 if kvh % 2 == 0 else ~lo2
    return jnp.where(own, tot, 0.0)


def _softmax_sink(s, mask, sink):
    s = jnp.where(mask, s, -1e30)
    mx = jnp.maximum(jnp.max(s, axis=-1, keepdims=True), sink)
    e = jnp.exp(s - mx)
    es = jnp.exp(sink - mx)
    inv = 1.0 / (jnp.sum(e, axis=-1, keepdims=True) + es)
    return e * inv, es * inv


def _attn_fwd(q, k, v, qg, kg, sinks, cosf, sins, ind_q, ind_qt, ind_k, ind_kt, n_seq, S, phases=()):
    T = q.shape[0]
    nblk = S // WINDOW
    W = D_MODEL

    def body(sink_ref, q_ref, k_ref, v_ref, qg_ref, kg_ref, cos_ref, sin_ref, iq_ref, iqt_ref, ik_ref, ikt_ref,
             o_ref, kc_ref, vc_ref, s_ref, p_ref):
        n = pl.program_id(1)

        @pl.when(n == 0)
        def _():
            kc_ref[...] = jnp.zeros_like(kc_ref)
            vc_ref[...] = jnp.zeros_like(vc_ref)

        lane = lax.broadcasted_iota(jnp.int32, (WINDOW, W), 1)
        lo = lane[:, :LANES] < HEAD_DIM
        lo2 = lax.broadcasted_iota(jnp.int32, (2 * WINDOW, LANES), 1) < HEAD_DIM
        cosf, sinv = cos_ref[...], sin_ref[...]
        qr, _ = _qk_prep(q_ref[...], qg_ref[...], cosf, sinv, iq_ref[...], iqt_ref[...], lane)
        kr, _ = _qk_prep(k_ref[...], kg_ref[...], cosf[:, :KV_W], sinv[:, :KV_W], ik_ref[...], ikt_ref[...],
                         lane[:, :KV_W])
        kc_ref[WINDOW:2 * WINDOW, :] = kr
        vc_ref[WINDOW:2 * WINDOW, :] = v_ref[...]
        kc, vc = kc_ref[...], vc_ref[...]
        mask = _attn_mask(n)
        qr = qr * HEAD_DIM ** -0.5
        for kvh in range(N_KV):
            b = kvh % 2
            s_ref[b] = _dot_nt(_stack_heads(qr, kvh, lo), _dup_head(kc, kvh, lo2))
            for r in range(4):
                rows = slice(r * WINDOW, (r + 1) * WINDOW)
                p, _ = _softmax_sink(s_ref[b, rows, :], mask, sink_ref[4 * kvh + r])
                p_ref[b, rows, :] = p.astype(BF16)
            o0, o1 = _unstack_heads(_dot(p_ref[b], _dup_head(vc, kvh, lo2)), lo)
            o_ref[:, (2 * kvh) * LANES:(2 * kvh + 1) * LANES] = o0.astype(BF16)
            o_ref[:, (2 * kvh + 1) * LANES:(2 * kvh + 2) * LANES] = o1.astype(BF16)
        kc_ref[0:WINDOW, :] = kr
        vc_ref[0:WINDOW, :] = v_ref[...]

    blk = lambda w: pl.BlockSpec((WINDOW, w), lambda s, n: (s * nblk + n, 0))
    pos = pl.BlockSpec((WINDOW, W), lambda s, n: (n, 0))
    outs, extra = _call(
        body, phases=phases, name="attn_fwd", grid=(n_seq, nblk),
        in_specs=[pl.BlockSpec(memory_space=pltpu.SMEM), blk(W), blk(KV_W), blk(KV_W), _const((1, W)),
                  _const((1, KV_W)), pos, pos, _const((W, LANES)), _const((LANES, W)), _const((KV_W, LANES)),
                  _const((LANES, KV_W))],
        out_specs=blk(W), out_shape=jax.ShapeDtypeStruct((T, W), BF16),
        scratch_shapes=[pltpu.VMEM((2 * WINDOW, KV_W), F32), pltpu.VMEM((2 * WINDOW, KV_W), F32),
                        pltpu.VMEM((2, 4 * WINDOW, 2 * WINDOW), F32), pltpu.VMEM((2, 4 * WINDOW, 2 * WINDOW), BF16)],
    )(sinks, q, k, v, qg, kg, cosf, sins, ind_q, ind_qt, ind_k, ind_kt)
    return outs[0], extra


def _attn_bwd(do, q, k, v, qg, kg, sinks, cosf, sins, ind_q, ind_qt, ind_k, ind_kt, n_seq, S, phases=()):
    T = q.shape[0]
    nblk = S // WINDOW
    W = D_MODEL

    def body(sink_ref, do_ref, q_ref, k_ref, v_ref, qg_ref, kg_ref, cos_ref, sin_ref, iq_ref, iqt_ref, ik_ref,
             ikt_ref, dq_ref, dkc_ref, dkp_ref, dvc_ref, dvp_ref, dqg_ref, dsk_ref, kc_ref, vc_ref, dqr_ref,
             dk_ref, dv_ref, s_ref, dp_ref, p_ref, ds_ref):
        s_id, n = pl.program_id(0), pl.program_id(1)

        @pl.when((s_id == 0) & (n == 0))
        def _():
            dqg_ref[...] = jnp.zeros_like(dqg_ref)
            dsk_ref[...] = jnp.zeros_like(dsk_ref)

        @pl.when(n == 0)
        def _():
            kc_ref[...] = jnp.zeros_like(kc_ref)
            vc_ref[...] = jnp.zeros_like(vc_ref)

        lane = lax.broadcasted_iota(jnp.int32, (WINDOW, W), 1)
        lane_k = lane[:, :KV_W]
        lane128 = lane[:, :LANES]
        cosf, sinv = cos_ref[...], sin_ref[...]
        qv = q_ref[...]
        qr, q_rstd = _qk_prep(qv, qg_ref[...], cosf, sinv, iq_ref[...], iqt_ref[...], lane)
        kr, _ = _qk_prep(k_ref[...], kg_ref[...], cosf[:, :KV_W], sinv[:, :KV_W], ik_ref[...], ikt_ref[...], lane_k)
        kc_ref[WINDOW:2 * WINDOW, :] = kr
        vc_ref[WINDOW:2 * WINDOW, :] = v_ref[...]
        kc, vc = kc_ref[...], vc_ref[...]
        dov = do_ref[...]
        mask = _attn_mask(n)
        lo = lane128 < HEAD_DIM
        lo2 = lax.broadcasted_iota(jnp.int32, (2 * WINDOW, LANES), 1) < HEAD_DIM
        scale = HEAD_DIM ** -0.5
        qr = qr * scale
        dk_ref[...] = jnp.zeros_like(dk_ref)
        dv_ref[...] = jnp.zeros_like(dv_ref)
        dsk = jnp.zeros((WINDOW, LANES), F32)
        for kvh in range(N_KV):
            m, b = kvh // 2, kvh % 2
            qs = _stack_heads(qr, kvh, lo)
            dos = _stack_heads(dov, kvh, lo)
            kd = _dup_head(kc, kvh, lo2)
            vd = _dup_head(vc, kvh, lo2)
            s_ref[b] = _dot_nt(qs, kd)
            dp_ref[b] = _dot_nt(dos, vd)
            for r in range(4):
                rows = slice(r * WINDOW, (r + 1) * WINDOW)
                p, ps = _softmax_sink(s_ref[b, rows, :], mask, sink_ref[4 * kvh + r])
                dp = dp_ref[b, rows, :]
                dd = jnp.sum(p * dp, axis=-1, keepdims=True)
                p_ref[b, rows, :] = p.astype(BF16)
                ds_ref[b, rows, :] = (p * (dp - dd)).astype(BF16)
                dsk = dsk - jnp.where(lane128 == 4 * kvh + r, ps * dd, 0.0)
            dq0, dq1 = _unstack_heads(_dot(ds_ref[b], kd) * scale, lo)
            dqr_ref[:, (2 * kvh) * LANES:(2 * kvh + 1) * LANES] = dq0
            dqr_ref[:, (2 * kvh + 1) * LANES:(2 * kvh + 2) * LANES] = dq1
            dk_ref[:, m * LANES:(m + 1) * LANES] += _fold_head(_dot_tn(ds_ref[b], qs), kvh, lo2)
            dv_ref[:, m * LANES:(m + 1) * LANES] += _fold_head(_dot_tn(p_ref[b], dos), kvh, lo2)
        dsk_ref[...] += dsk
        dq, dqg = _qk_prep_bwd(dqr_ref[...], qv, q_rstd, qg_ref[...], cosf, sinv, iq_ref[...], iqt_ref[...], lane)
        dq_ref[...] = dq.astype(BF16)
        dqg_ref[...] += dqg
        dkp_ref[...] = dk_ref[0:WINDOW, :]
        dkc_ref[...] = dk_ref[WINDOW:2 * WINDOW, :]
        dvp_ref[...] = dv_ref[0:WINDOW, :]
        dvc_ref[...] = dv_ref[WINDOW:2 * WINDOW, :]
        kc_ref[0:WINDOW, :] = kr
        vc_ref[0:WINDOW, :] = v_ref[...]

    blk = lambda w: pl.BlockSpec((WINDOW, w), lambda s, n: (s * nblk + n, 0))
    pos = pl.BlockSpec((WINDOW, W), lambda s, n: (n, 0))
    kv_out = jax.ShapeDtypeStruct((T, KV_W), F32)
    stage = lambda dt: pltpu.VMEM((2, 4 * WINDOW, 2 * WINDOW), dt)
    return _call(
        body, phases=phases, name="attn_bwd", grid=(n_seq, nblk),
        in_specs=[pl.BlockSpec(memory_space=pltpu.SMEM), blk(W), blk(W), blk(KV_W), blk(KV_W), _const((1, W)),
                  _const((1, KV_W)), pos, pos, _const((W, LANES)), _const((LANES, W)), _const((KV_W, LANES)),
                  _const((LANES, KV_W))],
        out_specs=[blk(W), blk(KV_W), blk(KV_W), blk(KV_W), blk(KV_W), _const((1, W)), _const((WINDOW, LANES))],
        out_shape=[jax.ShapeDtypeStruct((T, W), BF16), kv_out, kv_out, kv_out, kv_out,
                   jax.ShapeDtypeStruct((1, W), F32), jax.ShapeDtypeStruct((WINDOW, LANES), F32)],
        scratch_shapes=[pltpu.VMEM((2 * WINDOW, KV_W), F32), pltpu.VMEM((2 * WINDOW, KV_W), F32),
                        pltpu.VMEM((WINDOW, W), F32), pltpu.VMEM((2 * WINDOW, KV_W), F32),
                        pltpu.VMEM((2 * WINDOW, KV_W), F32), stage(F32), stage(F32), stage(BF16), stage(BF16)],
    )(sinks, do, q, k, v, qg, kg, cosf, sins, ind_q, ind_qt, ind_k, ind_kt)


def _kv_bwd(dkc, dkp, dvc, dvp, k, kg, cosf, sins, ind_k, ind_kt, n_seq, S, phases=()):
    T = k.shape[0]
    nblk = S // WINDOW

    def body(dkc_ref, dkp_ref, dvc_ref, dvp_ref, k_ref, kg_ref, cos_ref, sin_ref, ik_ref, ikt_ref,
             dk_ref, dv_ref, dkg_ref):
        s_id, n = pl.program_id(0), pl.program_id(1)

        @pl.when((s_id == 0) & (n == 0))
        def _():
            dkg_ref[...] = jnp.zeros_like(dkg_ref)

        has_next = n < nblk - 1
        lane = lax.broadcasted_iota(jnp.int32, (WINDOW, KV_W), 1)
        dkr = dkc_ref[...] + jnp.where(has_next, dkp_ref[...], 0.0)
        dv_ref[...] = (dvc_ref[...] + jnp.where(has_next, dvp_ref[...], 0.0)).astype(BF16)
        cosf, sinv = cos_ref[:, :KV_W], sin_ref[:, :KV_W]
        kv = k_ref[...]
        _, rstd = _qk_prep(kv, kg_ref[...], cosf, sinv, ik_ref[...], ikt_ref[...], lane)
        dk, dkg = _qk_prep_bwd(dkr, kv, rstd, kg_ref[...], cosf, sinv, ik_ref[...], ikt_ref[...], lane)
        dk_ref[...] = dk.astype(BF16)
        dkg_ref[...] += dkg

    cur = pl.BlockSpec((WINDOW, KV_W), lambda s, n: (s * nblk + n, 0))
    nxt = pl.BlockSpec((WINDOW, KV_W), lambda s, n: (s * nblk + jnp.minimum(n + 1, nblk - 1), 0))
    pos = pl.BlockSpec((WINDOW, D_MODEL), lambda s, n: (n, 0))
    return _call(
        body, phases=phases, name="kv_bwd", grid=(n_seq, nblk),
        in_specs=[cur, nxt, cur, nxt, cur, _const((1, KV_W)), pos, pos, _const((KV_W, LANES)),
                  _const((LANES, KV_W))],
        out_specs=[cur, cur, _const((1, KV_W))],
        out_shape=[jax.ShapeDtypeStruct((T, KV_W), BF16), jax.ShapeDtypeStruct((T, KV_W), BF16),
                   jax.ShapeDtypeStruct((1, KV_W), F32)],
    )(dkc, dkp, dvc, dvp, k, kg, cosf, sins, ind_k, ind_kt)


def _merge_fwd(x, ya, o, ga, gb, w_rnn, w_attn, w_out, tm, phases=()):
    T = x.shape[0]
    W = D_MODEL

    def body(x_ref, ya_ref, o_ref, ga_ref, gb_ref, wr_ref, wa_ref, wo_ref, x1_ref, mg_ref, yao_ref, ybo_ref):
        y_a = _dot(ya_ref[...], wr_ref[...])
        y_b = _dot(o_ref[...], wa_ref[...])
        yao_ref[...] = y_a
        ybo_ref[...] = y_b
        mg = (_sigmoid(ga_ref[...]) * y_a + _sigmoid(gb_ref[...]) * y_b).astype(BF16)
        mg_ref[...] = mg
        x1_ref[...] = x_ref[...] + _dot(mg, wo_ref[...])

    row = pl.BlockSpec((tm, W), lambda i: (i, 0))
    sq = _const((W, W))
    return _call(
        body, phases=phases, name="merge_fwd", grid=(T // tm,),
        in_specs=[row, row, row, row, row, sq, sq, sq], out_specs=[row, row, row, row],
        out_shape=[jax.ShapeDtypeStruct((T, W), F32), jax.ShapeDtypeStruct((T, W), BF16),
                   jax.ShapeDtypeStruct((T, W), F32), jax.ShapeDtypeStruct((T, W), F32)],
    )(x, ya, o, ga, gb, w_rnn, w_attn, w_out)


def _merge_bwd(dx1, ga, gb, y_a, y_b, w_rnn, w_attn, w_out, tm, phases=()):
    T = dx1.shape[0]
    W = D_MODEL

    def body(dx1_ref, ga_ref, gb_ref, ya_ref, yb_ref, wr_ref, wa_ref, wo_ref,
             dga_ref, dgb_ref, dya_ref, dyb_ref, dyain_ref, do_ref):
        dm = _dot_nt(dx1_ref[...].astype(BF16), wo_ref[...])
        sa = _sigmoid(ga_ref[...])
        sb = _sigmoid(gb_ref[...])
        dga_ref[...] = (dm * ya_ref[...] * (sa * (1.0 - sa))).astype(BF16)
        dgb_ref[...] = (dm * yb_ref[...] * (sb * (1.0 - sb))).astype(BF16)
        dya = (dm * sa).astype(BF16)
        dyb = (dm * sb).astype(BF16)
        dya_ref[...] = dya
        dyb_ref[...] = dyb
        dyain_ref[...] = _dot_nt(dya, wr_ref[...])
        do_ref[...] = _dot_nt(dyb, wa_ref[...])

    row = pl.BlockSpec((tm, W), lambda i: (i, 0))
    sq = _const((W, W))
    b16 = jax.ShapeDtypeStruct((T, W), BF16)
    f32 = jax.ShapeDtypeStruct((T, W), F32)
    return _call(
        body, phases=phases, name="merge_bwd", grid=(T // tm,),
        in_specs=[row, row, row, row, row, sq, sq, sq], out_specs=[row] * 6,
        out_shape=[b16, b16, b16, b16, f32, f32],
    )(dx1, ga, gb, y_a, y_b, w_rnn, w_attn, w_out)


def _mlp_fwd(x1, g_mlp, w_up, w_down, tm, phases=()):
    T = x1.shape[0]
    W = D_MODEL

    def body(x_ref, g_ref, wu_ref, wd_ref, x2_ref, hm_ref, u_ref, act_ref):
        xv = x_ref[...]
        hm, _ = _rms_fwd(xv, g_ref[...])
        hmb = hm.astype(BF16)
        hm_ref[...] = hmb
        for j in range(N_CHIPS):
            u = _dot(hmb, wu_ref[j])
            u_ref[:, j * W:(j + 1) * W] = u
            ru = jnp.maximum(u, 0.0)
            act_ref[:, j * W:(j + 1) * W] = (ru * ru).astype(BF16)
        x2_ref[...] = xv + _dot(act_ref[...], wd_ref[...])

    row = lambda w: pl.BlockSpec((tm, w), lambda i: (i, 0))
    return _call(
        body, phases=phases, name="mlp_fwd", grid=(T // tm,),
        in_specs=[row(W), _const((1, W)), _const((N_CHIPS, W, W)), _const((D_FF, W))],
        out_specs=[row(W), row(W), row(D_FF), row(D_FF)],
        out_shape=[jax.ShapeDtypeStruct((T, W), F32), jax.ShapeDtypeStruct((T, W), BF16),
                   jax.ShapeDtypeStruct((T, D_FF), F32), jax.ShapeDtypeStruct((T, D_FF), BF16)],
    )(x1, g_mlp, w_up, w_down)


def _mlp_bwd(dx2, u, x1, g_mlp, w_up, w_down, tm, phases=()):
    T = x1.shape[0]
    W = D_MODEL

    def body(dx2_ref, u_ref, x_ref, g_ref, wu_ref, wd_ref, dx1_ref, du_ref, dg_ref):
        @pl.when(pl.program_id(0) == 0)
        def _():
            dg_ref[...] = jnp.zeros_like(dg_ref)

        dx2 = dx2_ref[...]
        dact = _dot_nt(dx2.astype(BF16), wd_ref[...])
        du_ref[...] = (dact * (2.0 * jnp.maximum(u_ref[...], 0.0))).astype(BF16)
        dhm = jnp.zeros((tm, W), F32)
        for j in range(N_CHIPS):
            dhm = dhm + _dot_nt(du_ref[:, j * W:(j + 1) * W], wu_ref[j])
        xv = x_ref[...]
        g = g_ref[...]
        _, r = _rms_fwd(xv, g)
        dx, dg = _rms_bwd(dhm, xv, r, g)
        dx1_ref[...] = dx2 + dx
        dg_ref[...] += dg

    row = lambda w: pl.BlockSpec((tm, w), lambda i: (i, 0))
    return _call(
        body, phases=phases, name="mlp_bwd", grid=(T // tm,),
        in_specs=[row(W), row(D_FF), row(W), _const((1, W)), _const((N_CHIPS, W, W)), _const((D_FF, W))],
        out_specs=[row(W), row(D_FF), _const((1, W))],
        out_shape=[jax.ShapeDtypeStruct((T, W), F32), jax.ShapeDtypeStruct((T, D_FF), BF16),
                   jax.ShapeDtypeStruct((1, W), F32)],
    )(dx2, u, x1, g_mlp, w_up, w_down)


def _ple_loss(x2, p, target, g_ple, w_gate, w_proj, tm, phases=()):
    T = x2.shape[0]
    W = D_MODEL
    cw = W // N_CHIPS

    def body(x_ref, p_ref, t_ref, g_ref, wg_ref, wp_ref, loss_ref, dx2_ref, pb_ref, de_ref, hp_ref, dtg_ref, dg_ref):
        @pl.when(pl.program_id(0) == 0)
        def _():
            dg_ref[...] = jnp.zeros_like(dg_ref)
            loss_ref[...] = jnp.zeros_like(loss_ref)

        xv = x_ref[...]
        g = g_ref[...]
        pb = p_ref[...].astype(BF16)
        pb_ref[...] = pb
        e = jnp.concatenate([_dot(pb, wp_ref[j]) for j in range(N_CHIPS)], axis=1)
        hp, r = _rms_fwd(xv, g)
        hpb = hp.astype(BF16)
        hp_ref[...] = hpb
        sg = _sigmoid(_dot(hpb, wg_ref[...]))
        diff = (xv + e * sg) - t_ref[...]
        loss_ref[...] += jnp.sum(diff * diff) * (0.5 / W)
        dx3 = diff * (1.0 / W)
        de_ref[...] = (dx3 * sg).astype(BF16)
        dtg = (dx3 * e * (sg * (1.0 - sg))).astype(BF16)
        dtg_ref[...] = dtg
        dx, dg = _rms_bwd(_dot_nt(dtg, wg_ref[...]), xv, r, g)
        dx2_ref[...] = dx3 + dx
        dg_ref[...] += dg

    row = lambda w: pl.BlockSpec((tm, w), lambda i: (i, 0))
    b16 = lambda w: jax.ShapeDtypeStruct((T, w), BF16)
    return _call(
        body, phases=phases, name="ple_loss", grid=(T // tm,),
        in_specs=[row(W), row(PLE_DIM), row(W), _const((1, W)), _const((W, W)), _const((N_CHIPS, PLE_DIM, cw))],
        out_specs=[_const((8, LANES)), row(W), row(PLE_DIM), row(W), row(W), row(W), _const((1, W))],
        out_shape=[jax.ShapeDtypeStruct((8, LANES), F32), jax.ShapeDtypeStruct((T, W), F32), b16(PLE_DIM),
                   b16(W), b16(W), b16(W), jax.ShapeDtypeStruct((1, W), F32)],
    )(x2, p, target, g_ple, w_gate, w_proj)


def _adamw(w, g, m, v, name, tr, phases=()):
    R, C = w.shape
    c1 = 1.0 / (1.0 - ADAM_B1 ** ADAM_STEP)
    c2 = 1.0 / (1.0 - ADAM_B2 ** ADAM_STEP)

    def body(w_ref, g_ref, m_ref, v_ref, go_ref, d_ref, nm_ref, nv_ref):
        gv = g_ref[...]
        go_ref[...] = gv
        nm = ADAM_B1 * m_ref[...] + (1.0 - ADAM_B1) * gv
        nv = ADAM_B2 * v_ref[...] + (1.0 - ADAM_B2) * (gv * gv)
        nm_ref[...] = nm
        nv_ref[...] = nv
        d_ref[...] = (-ADAM_LR) * ((nm * c1) / (jnp.sqrt(nv * c2) + ADAM_EPS) + ADAM_WD * w_ref[...])

    row = pl.BlockSpec((tr, C), lambda i: (i, 0))
    sds = jax.ShapeDtypeStruct((R, C), F32)
    return _call(
        body, phases=phases, name=name, grid=(R // tr,), in_specs=[row] * 4, out_specs=[row] * 4,
        out_shape=[sds] * 4,
    )(w, g, m, v)


def _indicator(width):
    ind = np.zeros((width, LANES), np.float32)
    ind[np.arange(width), np.arange(width) // HEAD_DIM] = 1.0
    return jnp.asarray(ind, BF16), jnp.asarray(ind.T, BF16)


def _rope_tables(S):
    inv = ROPE_THETA ** (-jnp.arange(0, HEAD_DIM, 2, dtype=F32) / HEAD_DIM)
    ang = jnp.arange(S, dtype=F32)[:, None] * inv[None, :]
    cos, sin = jnp.cos(ang), jnp.sin(ang)
    cosf = jnp.tile(jnp.concatenate([cos, cos], axis=1), (1, N_HEADS))
    sins = jnp.tile(jnp.concatenate([-sin, sin], axis=1), (1, N_HEADS))
    return cosf, sins


def _pair_blockdiag(w):
    w4 = w.reshape(8, 2, HEAD_DIM, HEAD_DIM)
    eye = jnp.eye(2, dtype=w.dtype)
    return jnp.einsum("bpij,pq->bpiqj", w4, eye).reshape(8, LANES, LANES)


def _pair_blockdiag_extract(g):
    g5 = g.reshape(8, 2, HEAD_DIM, 2, HEAD_DIM)
    return jnp.stack([g5[:, 0, :, 0, :], g5[:, 1, :, 1, :]], axis=1).reshape(16, HEAD_DIM, HEAD_DIM)


def _pair_sum(parts, sibs, name):
    n = len(parts)
    dims = [(p.shape[1] // 2, p.shape[2]) for p in parts]

    def body(*refs):
        p_r, s_r, send_r, own_r, mine_r, sem = (refs[0:n], refs[n:2 * n], refs[2 * n:3 * n], refs[3 * n:4 * n],
                                                refs[4 * n:5 * n], refs[5 * n])
        x, y, c, chips = _mesh_pos()
        me = 2 * x + y
        loads = []
        for i, (R, _) in enumerate(dims):
            mine, _ = _half_rows(c, R)
            cp = pltpu.make_async_copy(p_r[i].at[:, mine, :], mine_r[i], sem.at[i])
            cp.start()
            loads.append(cp)
        for i in range(n):
            loads[i].wait()
            for j, (cx, cy) in enumerate(chips):
                k = 2 * cx + cy
                send_r[i][j] = (mine_r[i][k] + s_r[i][k]).astype(BF16)
            own_r[i][...] = mine_r[i][me] + s_r[i][me]

    vm = pl.BlockSpec(memory_space=pltpu.VMEM)
    out = pl.pallas_call(
        body, name=name, in_specs=[pl.BlockSpec(memory_space=pl.ANY)] * n + [vm] * n, out_specs=[vm] * (2 * n),
        out_shape=[jax.ShapeDtypeStruct((3, R, C), BF16) for R, C in dims]
        + [jax.ShapeDtypeStruct((R, C), F32) for R, C in dims],
        scratch_shapes=[pltpu.VMEM((N_CHIPS, R, C), F32) for R, C in dims] + [pltpu.SemaphoreType.DMA((n,))],
        compiler_params=pltpu.CompilerParams(vmem_limit_bytes=VMEM_LIMIT),
    )(*parts, *sibs)
    return out[:n], out[n:]


def _chip_sum(owns, recvs, name):
    n = len(owns)
    dims = [o.shape for o in owns]

    def body(*refs):
        own_r, recv_r, red_r, stage_r, sem = refs[0:n], refs[n:2 * n], refs[2 * n:3 * n], refs[3 * n:4 * n], refs[4 * n]
        x, y, c, _ = _mesh_pos()
        me = 2 * x + y
        stores = []
        for i, (R, _) in enumerate(dims):
            acc = None
            for k in range(N_CHIPS):
                term = jnp.where(me == k, own_r[i][...], recv_r[i][_peer_slot(k, x, y)].astype(F32))
                acc = term if acc is None else acc + term
            stage_r[i][...] = acc
            mine, _ = _half_rows(c, R)
            cp = pltpu.make_async_copy(stage_r[i], red_r[i].at[mine, :], sem.at[i])
            cp.start()
            stores.append(cp)
        for cp in stores:
            cp.wait()

    vm = pl.BlockSpec(memory_space=pltpu.VMEM)
    return pl.pallas_call(
        body, name=name, in_specs=[vm] * (2 * n), out_specs=[pl.BlockSpec(memory_space=pl.ANY)] * n,
        out_shape=[jax.ShapeDtypeStruct((2 * R, C), F32) for R, C in dims],
        scratch_shapes=[pltpu.VMEM((R, C), F32) for R, C in dims] + [pltpu.SemaphoreType.DMA((n,))],
        compiler_params=pltpu.CompilerParams(vmem_limit_bytes=VMEM_LIMIT),
    )(*owns, *recvs)


def _gather_bf16(shard, name):
    R2, C = shard.shape
    R = R2 // 2

    def body(s_ref, o_ref, send_sems, recv_sems):
        x, y, c, chips = _mesh_pos()
        me = 2 * x + y
        mine = pl.ds(pl.multiple_of(c * R, R), R)
        theirs = pl.ds(pl.multiple_of((1 - c) * R, R), R)
        o_ref[me] = s_ref[...].astype(BF16)

        def copy(k, chip, rows, to):
            blk = o_ref.at[chip, rows]
            return pltpu.make_async_remote_copy(src_ref=blk, dst_ref=blk, send_sem=send_sems.at[k],
                                                recv_sem=recv_sems.at[k], device_id=to, device_id_type=MESH_ID)

        first = [copy(j, me, mine, (cx, cy, c)) for j, (cx, cy) in enumerate(chips)]
        for cp in first:
            cp.start()
        passed = []
        for j, (cx, cy) in enumerate(chips):
            copy(j, 2 * cx + cy, mine, (x, y, c)).wait_recv()
            cp = copy(3 + j, 2 * cx + cy, mine, (x, y, 1 - c))
            cp.start()
            passed.append(cp)
        for j, (cx, cy) in enumerate(chips):
            copy(3 + j, 2 * cx + cy, theirs, (x, y, c)).wait_recv()
        for cp in first + passed:
            cp.wait_send()

    return pl.pallas_call(
        body, name=name, out_shape=jax.ShapeDtypeStruct((N_CHIPS, R2, C), BF16),
        in_specs=[pl.BlockSpec(memory_space=pltpu.VMEM)], out_specs=pl.BlockSpec(memory_space=pltpu.VMEM),
        scratch_shapes=[pltpu.SemaphoreType.DMA((6,)), pltpu.SemaphoreType.DMA((6,))],
        compiler_params=pltpu.CompilerParams(vmem_limit_bytes=VMEM_LIMIT),
    )(shard)


def _pair_exchange_sum(partial, name):
    _, R2, C = partial.shape
    R = R2 // 2

    def body(p_ref, send_ref, own_ref, mine_ref, sib_ref, loc_sems, send_sems, recv_sems):
        x, y, c, chips = _mesh_pos()
        me = 2 * x + y
        mine, theirs = _half_rows(c, R)
        order = [2 * cx + cy for cx, cy in chips] + [me]
        locs, pairs = [], []
        for i, k in enumerate(order):
            loc = pltpu.make_async_copy(p_ref.at[k, mine, :], mine_ref.at[i], loc_sems.at[i])
            pair = _remote(p_ref.at[k, theirs, :], sib_ref.at[i], (send_sems.at[i], recv_sems.at[i]), (x, y, 1 - c))
            loc.start()
            pair.start()
            locs.append(loc)
            pairs.append(pair)
        for i in range(N_CHIPS):
            locs[i].wait()
            pairs[i].wait_recv()
            total = mine_ref[i] + sib_ref[i]
            if i < 3:
                send_ref[i] = total.astype(BF16)
            else:
                own_ref[...] = total
        for pair in pairs:
            pair.wait_send()

    vm = pl.BlockSpec(memory_space=pltpu.VMEM)
    return pl.pallas_call(
        body, name=name, in_specs=[pl.BlockSpec(memory_space=pl.ANY)], out_specs=[vm, vm],
        out_shape=[jax.ShapeDtypeStruct((3, R, C), BF16), jax.ShapeDtypeStruct((R, C), F32)],
        scratch_shapes=[pltpu.VMEM((N_CHIPS, R, C), F32), pltpu.VMEM((N_CHIPS, R, C), F32),
                        pltpu.SemaphoreType.DMA((N_CHIPS,)), pltpu.SemaphoreType.DMA((N_CHIPS,)),
                        pltpu.SemaphoreType.DMA((N_CHIPS,))],
        compiler_params=pltpu.CompilerParams(vmem_limit_bytes=VMEM_LIMIT),
    )(partial)


def _allreduce_small(buf, name):
    shape = buf.shape

    def body(b_ref, o_ref, sib_ref, pair_ref, in_ref, pair_sems, send_sems, recv_sems):
        x, y, c, chips = _mesh_pos()
        me = 2 * x + y
        pair = pltpu.make_async_remote_copy(src_ref=b_ref, dst_ref=sib_ref, send_sem=pair_sems.at[0],
                                            recv_sem=pair_sems.at[1], device_id=(x, y, 1 - c), device_id_type=MESH_ID)
        pair.start()
        pair.wait()
        pair_ref[...] = b_ref[...] + sib_ref[...]
        sends = []
        for j, (cx, cy) in enumerate(chips):
            cp = pltpu.make_async_remote_copy(src_ref=pair_ref, dst_ref=in_ref.at[j], send_sem=send_sems.at[j],
                                              recv_sem=recv_sems.at[j], device_id=(cx, cy, c), device_id_type=MESH_ID)
            cp.start()
            sends.append(cp)
        for cp in sends:
            cp.wait_recv()
        acc = None
        for k in range(N_CHIPS):
            term = jnp.where(me == k, pair_ref[...], in_ref[_peer_slot(k, x, y)])
            acc = term if acc is None else acc + term
        o_ref[...] = acc
        for cp in sends:
            cp.wait_send()

    return pl.pallas_call(
        body, name=name, out_shape=jax.ShapeDtypeStruct(shape, F32),
        in_specs=[pl.BlockSpec(memory_space=pltpu.VMEM)], out_specs=pl.BlockSpec(memory_space=pltpu.VMEM),
        scratch_shapes=[pltpu.VMEM(shape, F32), pltpu.VMEM(shape, F32), pltpu.VMEM((3,) + shape, F32),
                        pltpu.SemaphoreType.DMA((2,)), pltpu.SemaphoreType.DMA((3,)), pltpu.SemaphoreType.DMA((3,))],
        compiler_params=pltpu.CompilerParams(vmem_limit_bytes=VMEM_LIMIT),
    )(buf)


def _adamw_small(ws, gs, ms, vs):
    n = len(ws)
    c1 = 1.0 / (1.0 - ADAM_B1 ** ADAM_STEP)
    c2 = 1.0 / (1.0 - ADAM_B2 ** ADAM_STEP)

    def body(*refs):
        w_r, g_r, m_r, v_r = refs[0:n], refs[n:2 * n], refs[2 * n:3 * n], refs[3 * n:4 * n]
        d_r, nm_r, nv_r = refs[4 * n:5 * n], refs[5 * n:6 * n], refs[6 * n:7 * n]
        for i in range(n):
            gv = g_r[i][...]
            nm = ADAM_B1 * m_r[i][...] + (1.0 - ADAM_B1) * gv
            nv = ADAM_B2 * v_r[i][...] + (1.0 - ADAM_B2) * (gv * gv)
            nm_r[i][...] = nm
            nv_r[i][...] = nv
            d_r[i][...] = (-ADAM_LR) * ((nm * c1) / (jnp.sqrt(nv * c2) + ADAM_EPS) + ADAM_WD * w_r[i][...])

    vm = pl.BlockSpec(memory_space=pltpu.VMEM)
    sds = [jax.ShapeDtypeStruct(w.shape, F32) for w in ws]
    out = pl.pallas_call(body, name="adamw_small", in_specs=[vm] * (4 * n), out_specs=[vm] * (3 * n),
                         out_shape=sds * 3)(*ws, *gs, *ms, *vs)
    return out[0:n], out[n:2 * n], out[2 * n:3 * n]


_BIG = ("w_in", "w_rnn_proj", "w_attn_proj", "w_out", "w_up", "w_down", "w_ple_gate", "w_ple_proj")
_SMALL = ("g_mix", "conv_w", "conv_b", "w_rg", "b_rg", "w_ig", "b_ig", "lru_lambda", "q_gain", "k_gain", "sinks",
          "g_mlp", "g_ple")
_WEIGHTS = ("g_mix", "w_in", "conv_w", "conv_b", "w_rg", "b_rg", "w_ig", "b_ig", "lru_lambda", "w_rnn_proj",
            "q_gain", "k_gain", "sinks", "w_attn_proj", "w_out", "g_mlp", "w_up", "w_down", "g_ple", "w_ple_gate",
            "w_ple_proj")


def _pad_row(v):
    v = v.reshape(1, -1)
    return jnp.pad(v, ((0, 0), (0, D_MODEL - v.shape[1])))


def kernel(x, p, g_mix, w_in, conv_w, conv_b, w_rg, b_rg, w_ig, b_ig, lru_lambda, w_rnn_proj, q_gain, k_gain, sinks, w_attn_proj, w_out, g_mlp, w_up, w_down, g_ple, w_ple_gate, w_ple_proj, loss_target, m_g_mix, m_w_in, m_conv_w, m_conv_b, m_w_rg, m_b_rg, m_w_ig, m_b_ig, m_lru_lambda, m_w_rnn_proj, m_q_gain, m_k_gain, m_sinks, m_w_attn_proj, m_w_out, m_g_mlp, m_w_up, m_w_down, m_g_ple, m_w_ple_gate, m_w_ple_proj, v_g_mix, v_w_in, v_conv_w, v_conv_b, v_w_rg, v_b_rg, v_w_ig, v_b_ig, v_lru_lambda, v_w_rnn_proj, v_q_gain, v_k_gain, v_sinks, v_w_attn_proj, v_w_out, v_g_mlp, v_w_up, v_w_down, v_g_ple, v_w_ple_gate, v_w_ple_proj):
    w = dict(g_mix=g_mix, w_in=w_in, conv_w=conv_w, conv_b=conv_b, w_rg=w_rg, b_rg=b_rg, w_ig=w_ig, b_ig=b_ig,
             lru_lambda=lru_lambda, w_rnn_proj=w_rnn_proj, q_gain=q_gain, k_gain=k_gain, sinks=sinks,
             w_attn_proj=w_attn_proj, w_out=w_out, g_mlp=g_mlp, w_up=w_up, w_down=w_down, g_ple=g_ple,
             w_ple_gate=w_ple_gate, w_ple_proj=w_ple_proj)
    m = dict(g_mix=m_g_mix, w_in=m_w_in, conv_w=m_conv_w, conv_b=m_conv_b, w_rg=m_w_rg, b_rg=m_b_rg, w_ig=m_w_ig,
             b_ig=m_b_ig, lru_lambda=m_lru_lambda, w_rnn_proj=m_w_rnn_proj, q_gain=m_q_gain, k_gain=m_k_gain,
             sinks=m_sinks, w_attn_proj=m_w_attn_proj, w_out=m_w_out, g_mlp=m_g_mlp, w_up=m_w_up, w_down=m_w_down,
             g_ple=m_g_ple, w_ple_gate=m_w_ple_gate, w_ple_proj=m_w_ple_proj)
    v = dict(g_mix=v_g_mix, w_in=v_w_in, conv_w=v_conv_w, conv_b=v_conv_b, w_rg=v_w_rg, b_rg=v_b_rg, w_ig=v_w_ig,
             b_ig=v_b_ig, lru_lambda=v_lru_lambda, w_rnn_proj=v_w_rnn_proj, q_gain=v_q_gain, k_gain=v_k_gain,
             sinks=v_sinks, w_attn_proj=v_w_attn_proj, w_out=v_w_out, g_mlp=v_g_mlp, w_up=v_w_up, w_down=v_w_down,
             g_ple=v_g_ple, w_ple_gate=v_w_ple_gate, w_ple_proj=v_w_ple_proj)
    n_seq, S, _ = x.shape
    T = n_seq * S
    chip = 2 * lax.axis_index("x") + lax.axis_index("y")

    tm, tm_rnn = 512, 256
    xf, pf, tf = x.reshape(T, D_MODEL), p.reshape(T, PLE_DIM), loss_target.reshape(T, D_MODEL)
    first = lambda outs: [o[0] for o in outs]

    w_in_g = _gather_bf16(w["w_in"][0], "gather_w_in")
    wb = {name: w[name][0].astype(BF16) for name in _BIG if name != "w_in"}
    grp_mix, grp_mlp, grp_ple = ("w_rnn_proj", "w_attn_proj", "w_out"), ("w_up", "w_down"), ("w_ple_gate", "w_ple_proj")

    cw_full = jnp.zeros((8, D_MODEL), F32)
    cw_full = lax.dynamic_update_slice(cw_full, conv_w[0], (0, chip * (D_MODEL // N_CHIPS)))
    cw_full = _allreduce_small(0.5 * cw_full.reshape(64, LANES), "allgather_conv_w").reshape(8, D_MODEL)[0:CONV_W]

    cosf, sins = _rope_tables(S)
    ind_q, ind_qt = _indicator(D_MODEL)
    ind_k, ind_kt = _indicator(KV_W)
    wrg2 = _pair_blockdiag(w_rg[0]).astype(BF16)
    wig2 = _pair_blockdiag(w_ig[0]).astype(BF16)
    qg = jnp.tile(q_gain, (1, N_HEADS))
    kg = jnp.tile(k_gain, (1, N_KV))
    sk = sinks.reshape(N_HEADS)
    rnn_w = (cw_full, conv_b, wrg2, b_rg, wig2, b_ig, lru_lambda)
    attn_c = (qg, kg, sk, cosf, sins, ind_q, ind_qt, ind_k, ind_kt, n_seq, S)

    (h0, xr, gr, zq, zk, zv, ga, gb), ph = _inproj_fwd(xf, g_mix, w_in_g, tm,
                                                     phases=[_ph_gather_send(wb[n]) for n in grp_mix])
    g_mixw = first(ph)
    o, ph = _attn_fwd(zq, zk, zv, *attn_c,
                      phases=[_ph_gather_pass(g) for g in g_mixw] + [_ph_gather_send(wb[n]) for n in grp_mlp])
    g_mixw, g_mlpw = first(ph[:3]), first(ph[3:])
    (xc, h, ya), ph = _rnn_fwd(xr, gr, *rnn_w, n_seq, S, tm_rnn,
                               phases=[_ph_gather_pass(g) for g in g_mlpw] + [_ph_gather_send(wb[n]) for n in grp_ple])
    g_mlpw, g_plew = first(ph[:2]), first(ph[2:])
    wr, wa, wo = (g.reshape(D_MODEL, D_MODEL) for g in g_mixw)
    wu, wd = g_mlpw[0], g_mlpw[1].reshape(D_FF, D_MODEL)
    (x1, merged, y_a, y_b), ph = _merge_fwd(xf, ya, o, ga, gb, wr, wa, wo, tm,
                                            phases=[_ph_gather_pass(g) for g in g_plew])
    wpg, wpp = first(ph)
    wpg = wpg.reshape(D_MODEL, D_MODEL)
    (x2, hm, u, act), _ = _mlp_fwd(x1, g_mlp, wu, wd, tm // 2)
    (loss_t, dx2, pb, de, hp, dtg, dg_ple), _ = _ple_loss(x2, pf, tf, g_ple, wpg, wpp, tm)
    loss = lax.psum(loss_t[0, 0], ("x", "y", "c"))

    chipmajor = lambda g: g.reshape(N_CHIPS, g.shape[-2] // N_CHIPS, g.shape[-1]) if g.ndim == 2 else g
    part_ple = [chipmajor(_wgrad(hp, dtg, "wgrad_ple_gate", False, D_MODEL, tm)[0]),
                _wgrad(pb, de, "wgrad_ple_proj", True, D_MODEL // N_CHIPS, tm)[0]]
    (dx1, du, dg_mlp), ph = _mlp_bwd(dx2, u, x1, g_mlp, wu, wd, tm // 2, phases=[_ph_pair_send(g) for g in part_ple])
    send_ple, own_ple = _pair_sum(part_ple, first(ph), "pair_sum_ple")
    dw_down, ph = _wgrad(act, dx2, "wgrad_down", False, D_MODEL // 2, tm, phases=[_ph_chip_send(s) for s in send_ple])
    red_ple = _chip_sum(own_ple, first(ph), "chip_sum_ple")
    part_mlp = [_wgrad(hm, du, "wgrad_up", True, D_MODEL, tm)[0], chipmajor(dw_down)]
    (dga, dgb, dya, dyb, dyain, do), ph = _merge_bwd(
        dx1, ga, gb, y_a, y_b, wr, wa, wo, tm,
        phases=[_ph_half_swap(r) for r in red_ple] + [_ph_pair_send(g) for g in part_mlp])
    red_ple = first(ph[:2])
    send_mlp, own_mlp = _pair_sum(part_mlp, first(ph[2:]), "pair_sum_mlp")
    part_mix = [chipmajor(_wgrad(ya, dya, "wgrad_rnn_proj", False, D_MODEL, tm)[0]),
                chipmajor(_wgrad(o, dyb, "wgrad_attn_proj", False, D_MODEL, tm)[0]),
                chipmajor(_wgrad(merged, dx1, "wgrad_out", False, D_MODEL, tm)[0])]
    (dxr, dgr, vec, dwrg2, dwig2), ph = _rnn_bwd(
        dyain, xr, gr, xc, h, cw_full, wrg2, b_rg, wig2, b_ig, lru_lambda, n_seq, S, tm_rnn,
        phases=[_ph_chip_send(s) for s in send_mlp] + [_ph_pair_send(g) for g in part_mix])
    red_mlp = _chip_sum(own_mlp, first(ph[:2]), "chip_sum_mlp")
    send_mix, own_mix = _pair_sum(part_mix, first(ph[2:]), "pair_sum_mix")
    (dq, dkc, dkp, dvc, dvp, dqg, dsk), ph = _attn_bwd(
        do, zq, zk, zv, *attn_c, phases=[_ph_half_swap(r) for r in red_mlp] + [_ph_chip_send(s) for s in send_mix])
    red_mlp = first(ph[:2])
    red_mix = _chip_sum(own_mix, first(ph[2:]), "chip_sum_mix")
    (dk, dv, dkg), _ = _kv_bwd(dkc, dkp, dvc, dvp, zk, kg, cosf, sins, ind_k, ind_kt, n_seq, S)
    dz_parts = [dxr, dgr, dq, dk, dv, dga, dgb]
    send_in, own_in = _pair_exchange_sum(_wgrad_in(h0, dz_parts, tm), "pair_sum_in")
    (grad_x, dg_mix), ph = _inproj_bwd(dz_parts, w_in_g, xf, g_mix, dx1, tm,
                                       phases=[_ph_half_swap(r) for r in red_mix] + [_ph_chip_send(send_in)])
    red_mix = first(ph[:3])
    red_in = _chip_sum([own_in], first(ph[3:]), "chip_sum_in")
    reduced = dict(zip(grp_ple + grp_mlp + grp_mix, red_ple + red_mlp + red_mix))
    grads = {
        "g_mix": dg_mix[0], "g_mlp": dg_mlp[0], "g_ple": dg_ple[0],
        "conv_w": vec[0:CONV_W], "conv_b": vec[4], "b_rg": vec[5], "b_ig": vec[6], "lru_lambda": vec[7],
        "w_rg": _pair_blockdiag_extract(dwrg2), "w_ig": _pair_blockdiag_extract(dwig2),
        "q_gain": dqg.reshape(N_HEADS, HEAD_DIM).sum(0), "k_gain": dkg.reshape(N_KV, HEAD_DIM).sum(0),
        "sinks": dsk.sum(0)[:N_HEADS],
    }

    rows = [grads["conv_w"], _pad_row(grads["conv_b"]), _pad_row(grads["b_rg"]), _pad_row(grads["b_ig"]),
            _pad_row(grads["lru_lambda"]), _pad_row(grads["g_mix"]), _pad_row(grads["g_mlp"]),
            _pad_row(grads["g_ple"]), _pad_row(grads["q_gain"]), _pad_row(grads["k_gain"]), _pad_row(grads["sinks"]),
            jnp.zeros((2, D_MODEL), F32)]
    vecs = jnp.concatenate(rows, axis=0)
    packed = jnp.concatenate([vecs.reshape(-1, LANES), grads["w_rg"].reshape(-1, LANES),
                              grads["w_ig"].reshape(-1, LANES)], axis=0)
    red = _allreduce_small(packed, "allreduce_small")
    nv = vecs.size // LANES
    rvec = red[0:nv].reshape(16, D_MODEL)
    nw = grads["w_rg"].size // LANES
    sg = {
        "conv_w": lax.dynamic_slice(rvec[0:CONV_W], (0, chip * (D_MODEL // N_CHIPS)), (CONV_W, D_MODEL // N_CHIPS)),
        "conv_b": rvec[4], "b_rg": rvec[5], "b_ig": rvec[6], "lru_lambda": rvec[7], "g_mix": rvec[8],
        "g_mlp": rvec[9], "g_ple": rvec[10], "q_gain": rvec[11, :HEAD_DIM], "k_gain": rvec[12, :HEAD_DIM],
        "sinks": rvec[13, :N_HEADS], "w_rg": red[nv:nv + nw], "w_ig": red[nv + nw:nv + 2 * nw],
    }
    sg = {k: sg[k].reshape(w[k].shape) for k in _SMALL}
    d_s, m_s, v_s = _adamw_small([w[k] for k in _SMALL], [sg[k] for k in _SMALL], [m[k] for k in _SMALL],
                                 [v[k] for k in _SMALL])
    grad, delta, new_m, new_v = dict(sg), dict(zip(_SMALL, d_s)), dict(zip(_SMALL, m_s)), dict(zip(_SMALL, v_s))

    for name in ("w_up", "w_down", "w_rnn_proj", "w_attn_proj", "w_out", "w_ple_gate", "w_ple_proj", "w_in"):
        shape = w[name].shape
        outs, ph = _adamw(w[name][0], reduced[name], m[name][0], v[name][0], "adamw_" + name, 128,
                          phases=[_ph_half_swap(r) for r in red_in] if name == "w_up" else ())
        if name == "w_up":
            reduced["w_in"] = ph[0][0]
        grad[name], delta[name], new_m[name], new_v[name] = (a.reshape(shape) for a in outs)

    return (loss, grad_x.reshape(x.shape), *[grad[k] for k in _WEIGHTS], *[delta[k] for k in _WEIGHTS],
            *[new_m[k] for k in _WEIGHTS], *[new_v[k] for k in _WEIGHTS])
```

```python
import functools
import math

import numpy as np
import jax
import jax.numpy as jnp
from jax import lax
from jax.experimental import pallas as pl
from jax.experimental.pallas import tpu as pltpu

F32 = jnp.float32
BF16 = jnp.bfloat16

D_MODEL = 1024
N_HEADS = 16
N_KV = 4
HEAD_DIM = 64
KV_W = N_KV * HEAD_DIM
D_FF = 4096
PLE_DIM = 256
WINDOW = 128
CONV_W = 4
LRU_C = 8.0
NORM_EPS = 1e-6
ROPE_THETA = 10000.0
N_CHIPS = 4
IN_TOTAL = 5632
IN_BLK = IN_TOTAL // N_CHIPS
IN_SEGS = (0, 1024, 2048, 3072, 3328, 3584, 4608, 5632)

ADAM_LR = 0.001
ADAM_B1 = 0.9
ADAM_B2 = 0.999
ADAM_EPS = 1e-08
ADAM_WD = 0.01
ADAM_STEP = 10

LANES = 128
VMEM_LIMIT = 56 * 1024 * 1024
MESH_ID = pl.DeviceIdType.MESH


def _dot(a, b):
    return jnp.dot(a, b, preferred_element_type=F32)


def _dot_nt(a, b):
    return lax.dot_general(a, b, (((1,), (1,)), ((), ())), preferred_element_type=F32)


def _dot_tn(a, b):
    return lax.dot_general(a, b, (((0,), (0,)), ((), ())), preferred_element_type=F32)


def _split_dot(x, ind):
    hi = x.astype(BF16)
    lo = (x - hi.astype(F32)).astype(BF16)
    return _dot(hi, ind) + _dot(lo, ind)


def _sigmoid(x):
    return 1.0 / (1.0 + jnp.exp(-x))


_GELU_C = math.sqrt(2.0 / math.pi)


def _gelu_and_grad(g):
    inner = _GELU_C * (g + 0.044715 * g * g * g)
    t = jnp.tanh(inner)
    gelu = 0.5 * g * (1.0 + t)
    dgelu = 0.5 * (1.0 + t) + 0.5 * g * (1.0 - t * t) * _GELU_C * (1.0 + 3.0 * 0.044715 * g * g)
    return gelu, dgelu


def _const(shape):
    nd = len(shape)
    return pl.BlockSpec(shape, lambda *_: (0,) * nd)


def _params(n_grid, vmem=VMEM_LIMIT):
    return pltpu.CompilerParams(dimension_semantics=("arbitrary",) * n_grid, vmem_limit_bytes=vmem)


def _rms_fwd(x, g):
    r = lax.rsqrt(jnp.mean(x * x, axis=-1, keepdims=True) + NORM_EPS)
    return (x * r) * g, r


def _rms_bwd(dy, x, r, g):
    dn = dy * g
    dx = r * dn - x * (r * r * r * jnp.mean(dn * x, axis=-1, keepdims=True))
    dg = jnp.sum(dy * (x * r), axis=0, keepdims=True)
    return dx, dg


def _seg_pieces(blk_lo, blk_hi):
    out = []
    for s in range(7):
        lo, hi = max(blk_lo, IN_SEGS[s]), min(blk_hi, IN_SEGS[s + 1])
        if lo < hi:
            out.append((s, lo - IN_SEGS[s], hi - IN_SEGS[s], lo - blk_lo))
    return out


def _mesh_pos():
    x, y, c = lax.axis_index("x"), lax.axis_index("y"), lax.axis_index("c")
    other_chips = [(1 - x, y), (x, 1 - y), (1 - x, 1 - y)]
    return x, y, c, other_chips


def _peer_slot(k, x, y):
    dx = jnp.bitwise_xor(k // 2, x)
    dy = jnp.bitwise_xor(k % 2, y)
    return jnp.maximum(dx + 2 * dy - 1, 0)


def _half_rows(c, R):
    return pl.ds(pl.multiple_of(c * R, R), R), pl.ds(pl.multiple_of((1 - c) * R, R), R)


def _remote(src, dst, sems, to):
    return pltpu.make_async_remote_copy(src_ref=src, dst_ref=dst, send_sem=sems[0], recv_sem=sems[1],
                                        device_id=to, device_id_type=MESH_ID)


class _Phase:
    def __init__(self, ins, inout, outs, n_remote, n_local, build):
        self.ins, self.inout, self.outs = list(ins), list(inout), list(outs)
        self.n_remote, self.n_local, self.build = n_remote, n_local, build


def _ph_gather_send(wb):
    R2, C = wb.shape
    R = R2 // 2

    def build(ins, outs, rsem, lsem):
        (w_ref,), (g_ref,) = ins, outs
        x, y, c, chips = _mesh_pos()
        me = 2 * x + y
        mine, _ = _half_rows(c, R)
        loc = [pltpu.make_async_copy(w_ref, g_ref.at[me], lsem(0))]
        outg = [_remote(w_ref.at[mine], g_ref.at[me, mine], rsem(j), (cx, cy, c)) for j, (cx, cy) in enumerate(chips)]
        inc = [functools.partial(_remote, w_ref.at[mine], g_ref.at[2 * cx + cy, mine], rsem(j), (x, y, c))
               for j, (cx, cy) in enumerate(chips)]
        return loc, outg, inc

    return _Phase([wb], [], [jax.ShapeDtypeStruct((N_CHIPS, R2, C), BF16)], 3, 1, build)


def _ph_gather_pass(gath):
    _, R2, C = gath.shape
    R = R2 // 2

    def build(ins, outs, rsem, lsem):
        (g_ref,) = outs
        x, y, c, chips = _mesh_pos()
        mine, theirs = _half_rows(c, R)
        outg, inc = [], []
        for j, (cx, cy) in enumerate(chips):
            blk = g_ref.at[2 * cx + cy, mine]
            outg.append(_remote(blk, blk, rsem(j), (x, y, 1 - c)))
            got = g_ref.at[2 * cx + cy, theirs]
            inc.append(functools.partial(_remote, got, got, rsem(j), (x, y, c)))
        return [], outg, inc

    return _Phase([], [gath], [], 3, 0, build)


def _ph_pair_send(partial):
    _, R2, C = partial.shape
    R = R2 // 2

    def build(ins, outs, rsem, lsem):
        (p_ref,), (s_ref,) = ins, outs
        x, y, c, _ = _mesh_pos()
        _, theirs = _half_rows(c, R)
        src = p_ref.at[:, theirs, :]
        return ([], [_remote(src, s_ref, rsem(0), (x, y, 1 - c))],
                [functools.partial(_remote, src, s_ref, rsem(0), (x, y, c))])

    return _Phase([partial], [], [jax.ShapeDtypeStruct((N_CHIPS, R, C), F32)], 1, 0, build)


def _ph_chip_send(sendb):
    def build(ins, outs, rsem, lsem):
        (s_ref,), (r_ref,) = ins, outs
        x, y, c, chips = _mesh_pos()
        outg = [_remote(s_ref.at[j], r_ref.at[j], rsem(j), (cx, cy, c)) for j, (cx, cy) in enumerate(chips)]
        inc = [functools.partial(_remote, s_ref.at[j], r_ref.at[j], rsem(j), (x, y, c)) for j in range(3)]
        return [], outg, inc

    return _Phase([sendb], [], [jax.ShapeDtypeStruct(sendb.shape, sendb.dtype)], 3, 0, build)


def _ph_half_swap(red):
    R2, C = red.shape
    R = R2 // 2

    def build(ins, outs, rsem, lsem):
        (r_ref,) = outs
        x, y, c, _ = _mesh_pos()
        mine, theirs = _half_rows(c, R)
        return ([], [_remote(r_ref.at[mine], r_ref.at[mine], rsem(0), (x, y, 1 - c))],
                [functools.partial(_remote, r_ref.at[theirs], r_ref.at[theirs], rsem(0), (x, y, c))])

    return _Phase([], [red], [], 1, 0, build)


def _call(body, *, name, grid, in_specs, out_specs, out_shape, scratch_shapes=(), phases=()):
    single = not isinstance(out_specs, (list, tuple))
    out_specs = [out_specs] if single else list(out_specs)
    out_shape = [out_shape] if single else list(out_shape)
    n_in, n_out, n_scr = len(in_specs), len(out_specs), len(scratch_shapes)
    if not phases:
        call = pl.pallas_call(body, name=name, grid=grid, in_specs=in_specs, out_specs=out_specs,
                              out_shape=out_shape, scratch_shapes=list(scratch_shapes),
                              compiler_params=_params(len(grid)))
        return lambda *operands: (list(call(*operands)), [])

    ex_in, ex_out, aliases, spans = [], [], {}, []
    for ph in phases:
        i0, o0 = len(ex_in), len(ex_out)
        ex_in += ph.ins
        for a in ph.inout:
            aliases[n_in + len(ex_in)] = n_out + len(ex_out)
            ex_in.append(a)
            ex_out.append(jax.ShapeDtypeStruct(a.shape, a.dtype))
        ex_out += ph.outs
        spans.append((i0, len(ph.ins), o0, len(ex_out) - o0))
    n_remote = sum(ph.n_remote for ph in phases)
    n_local = max(sum(ph.n_local for ph in phases), 1)

    def wrapped(*refs):
        base_in, xin = refs[:n_in], refs[n_in:n_in + len(ex_in)]
        o0 = n_in + len(ex_in)
        base_out, xout = refs[o0:o0 + n_out], refs[o0 + n_out:o0 + n_out + len(ex_out)]
        scr = refs[o0 + n_out + len(ex_out):]
        send_sems, recv_sems, loc_sems = scr[n_scr:]
        first = functools.reduce(jnp.logical_and, [pl.program_id(i) == 0 for i in range(len(grid))])
        last = functools.reduce(jnp.logical_and, [pl.program_id(i) == grid[i] - 1 for i in range(len(grid))])

        def copies():
            out, r0, l0 = [], 0, 0
            for ph, (i0, ni, p0, no) in zip(phases, spans):
                rsem = lambda k, r0=r0: (send_sems.at[r0 + k], recv_sems.at[r0 + k])
                lsem = lambda k, l0=l0: loc_sems.at[l0 + k]
                out.append(ph.build(xin[i0:i0 + ni], xout[p0:p0 + no], rsem, lsem))
                r0, l0 = r0 + ph.n_remote, l0 + ph.n_local
            return out

        @pl.when(first)
        def _():
            for loc, outg, _ in copies():
                for cp in loc + outg:
                    cp.start()

        body(*base_in, *base_out, *scr[:n_scr])

        @pl.when(last)
        def _():
            for loc, outg, inc in copies():
                for make in inc:
                    make().wait_recv()
                for cp in outg:
                    cp.wait_send()
                for cp in loc:
                    cp.wait()

    hbm = pl.BlockSpec(memory_space=pl.ANY)
    call = pl.pallas_call(
        wrapped, name=name, grid=grid, in_specs=list(in_specs) + [hbm] * len(ex_in),
        out_specs=out_specs + [hbm] * len(ex_out), out_shape=out_shape + ex_out,
        scratch_shapes=list(scratch_shapes) + [pltpu.SemaphoreType.DMA((n_remote,)), pltpu.SemaphoreType.DMA((n_remote,)),
                                              pltpu.SemaphoreType.DMA((n_local,))],
        input_output_aliases=aliases, compiler_params=_params(len(grid)))

    def run(*operands):
        res = call(*operands, *ex_in)
        extra = res[n_out:]
        return list(res[:n_out]), [list(extra[p0:p0 + no]) for (_, _, p0, no) in spans]

    return run


def _inproj_fwd(x, g_mix, w_in, tm, phases=()):
    T = x.shape[0]
    widths = [IN_SEGS[i + 1] - IN_SEGS[i] for i in range(7)]

    def body(x_ref, g_ref, w_ref, h_ref, *z_refs):
        h, _ = _rms_fwd(x_ref[...], g_ref[...])
        hb = h.astype(BF16)
        h_ref[...] = hb
        for j in range(N_CHIPS):
            zj = _dot(hb, w_ref[j])
            for s, lo, hi, off in _seg_pieces(j * IN_BLK, (j + 1) * IN_BLK):
                z_refs[s][:, lo:hi] = zj[:, off:off + hi - lo]

    return _call(
        body, phases=phases, name="inproj_fwd", grid=(T // tm,),
        in_specs=[pl.BlockSpec((tm, D_MODEL), lambda i: (i, 0)), _const((1, D_MODEL)),
                  _const((N_CHIPS, D_MODEL, IN_BLK))],
        out_specs=[pl.BlockSpec((tm, D_MODEL), lambda i: (i, 0))]
        + [pl.BlockSpec((tm, w), lambda i: (i, 0)) for w in widths],
        out_shape=[jax.ShapeDtypeStruct((T, D_MODEL), BF16)]
        + [jax.ShapeDtypeStruct((T, w), F32) for w in widths],
    )(x, g_mix, w_in)


def _inproj_bwd(dz_parts, w_in, x, g_mix, dx1, tm, phases=()):
    T = x.shape[0]
    widths = [IN_SEGS[i + 1] - IN_SEGS[i] for i in range(7)]

    def body(*refs):
        p_refs = refs[:7]
        w_ref, x_ref, g_ref, dx1_ref, gx_ref, dg_ref, dz_ref = refs[7:]

        @pl.when(pl.program_id(0) == 0)
        def _():
            dg_ref[...] = jnp.zeros_like(dg_ref)

        for s in range(7):
            dz_ref[:, IN_SEGS[s]:IN_SEGS[s + 1]] = p_refs[s][...]
        dh = jnp.zeros((tm, D_MODEL), F32)
        for j in range(N_CHIPS):
            dh = dh + _dot_nt(dz_ref[:, j * IN_BLK:(j + 1) * IN_BLK], w_ref[j])
        xv = x_ref[...]
        g = g_ref[...]
        _, r = _rms_fwd(xv, g)
        dx, dg = _rms_bwd(dh, xv, r, g)
        gx_ref[...] = dx1_ref[...] + dx
        dg_ref[...] += dg

    row = lambda w: pl.BlockSpec((tm, w), lambda i: (i, 0))
    return _call(
        body, phases=phases, name="inproj_bwd", grid=(T // tm,),
        in_specs=[row(w) for w in widths]
        + [_const((N_CHIPS, D_MODEL, IN_BLK)), row(D_MODEL), _const((1, D_MODEL)), row(D_MODEL)],
        out_specs=[row(D_MODEL), _const((1, D_MODEL))],
        out_shape=[jax.ShapeDtypeStruct((T, D_MODEL), F32), jax.ShapeDtypeStruct((1, D_MODEL), F32)],
        scratch_shapes=[pltpu.VMEM((tm, IN_TOTAL), BF16)],
    )(*dz_parts, w_in, x, g_mix, dx1)


def _wgrad_in(h0, dz_parts, tm):
    T = h0.shape[0]
    widths = [IN_SEGS[i + 1] - IN_SEGS[i] for i in range(7)]

    def body(*refs):
        h_ref, p_refs, o_ref, acc_ref, sem = refs[0], refs[1:8], refs[8], refs[9], refs[10]
        t = pl.program_id(0)

        @pl.when(t == 0)
        def _():
            acc_ref[...] = jnp.zeros_like(acc_ref)

        hv = h_ref[...]
        for j in range(N_CHIPS):
            for s, lo, hi, off in _seg_pieces(j * IN_BLK, (j + 1) * IN_BLK):
                acc_ref[j, :, off:off + hi - lo] += _dot_tn(hv, p_refs[s][:, lo:hi])

        @pl.when(t == T // tm - 1)
        def _():
            cp = pltpu.make_async_copy(acc_ref, o_ref, sem)
            cp.start()
            cp.wait()

    row = lambda w: pl.BlockSpec((tm, w), lambda i: (i, 0))
    return pl.pallas_call(
        body, name="wgrad_in", grid=(T // tm,), in_specs=[row(D_MODEL)] + [row(w) for w in widths],
        out_specs=pl.BlockSpec(memory_space=pl.ANY),
        out_shape=jax.ShapeDtypeStruct((N_CHIPS, D_MODEL, IN_BLK), F32),
        scratch_shapes=[pltpu.VMEM((N_CHIPS, D_MODEL, IN_BLK), F32), pltpu.SemaphoreType.DMA],
        compiler_params=_params(1),
    )(h0, *dz_parts)


def _wgrad(a, g, name, blocked, cn, tm, phases=()):
    T, K = a.shape
    N = g.shape[1]
    nb = N // cn

    def body(a_ref, g_ref, o_ref):
        @pl.when(pl.program_id(1) == 0)
        def _():
            o_ref[...] = jnp.zeros_like(o_ref)

        o_ref[...] += _dot_tn(a_ref[...].astype(BF16), g_ref[...].astype(BF16))

    if blocked:
        out_spec = pl.BlockSpec((None, K, cn), lambda j, t: (j, 0, 0))
        out_shape = jax.ShapeDtypeStruct((nb, K, cn), F32)
    else:
        out_spec = pl.BlockSpec((K, cn), lambda j, t: (0, j))
        out_shape = jax.ShapeDtypeStruct((K, N), F32)
    outs, extra = _call(
        body, phases=phases, name=name, grid=(nb, T // tm),
        in_specs=[pl.BlockSpec((tm, K), lambda j, t: (t, 0)), pl.BlockSpec((tm, cn), lambda j, t: (t, j))],
        out_specs=out_spec, out_shape=out_shape,
    )(a, g)
    return outs[0], extra


def _shift_down(x, prev8, sft, row, row8, tm):
    xs = pltpu.roll(x, sft, 0)
    top = jnp.where(row8 < sft, pltpu.roll(prev8, sft, 0), xs[0:8])
    return jnp.concatenate([top, xs[8:]], axis=0)


def _shift_up(x, next8, sft, row8, tm):
    xs = pltpu.roll(x, tm - sft, 0)
    bot = jnp.where(row8 >= 8 - sft, pltpu.roll(next8, 8 - sft, 0), xs[tm - 8:tm])
    return jnp.concatenate([xs[0:tm - 8], bot], axis=0)


def _conv_fwd(x, prev8, cw_ref, cb, row, row8, tm):
    xc = cb + cw_ref[CONV_W - 1:CONV_W, :] * x
    for sft in range(1, CONV_W):
        j = CONV_W - 1 - sft
        xc = xc + cw_ref[j:j + 1, :] * _shift_down(x, prev8, sft, row, row8, tm)
    return xc


def _blockdiag_dot(xb, w_ref, transpose):
    outs = []
    for b in range(D_MODEL // LANES):
        xs = xb[:, b * LANES:(b + 1) * LANES]
        outs.append(_dot_nt(xs, w_ref[b]) if transpose else _dot(xs, w_ref[b]))
    return jnp.concatenate(outs, axis=1)


def _softplus_neg(lam):
    e = jnp.exp(-jnp.abs(lam))
    u = 1.0 + e
    log1p_e = jnp.where(u == 1.0, e, jnp.log(u) * (e / (u - 1.0)))
    sp = jnp.maximum(-lam, 0.0) + log1p_e
    return sp, -_sigmoid(-lam)


def _lru_gates(xc, wrg_ref, brg, wig_ref, big, sp):
    xcb = xc.astype(BF16)
    r = _sigmoid(_blockdiag_dot(xcb, wrg_ref, False) + brg)
    i = _sigmoid(_blockdiag_dot(xcb, wig_ref, False) + big)
    log_a = (-LRU_C) * r * sp
    a = jnp.exp(log_a)
    t = jnp.tanh(log_a)
    one_m_a2 = (-2.0) * t / (1.0 - t)
    mult = jnp.sqrt(one_m_a2)
    return xcb, r, i, a, mult


def _scan_down(a, b, row, tm):
    d = 1
    while d < tm:
        keep = row >= d
        a_s = jnp.where(keep, pltpu.roll(a, d, 0), 1.0)
        b_s = jnp.where(keep, pltpu.roll(b, d, 0), 0.0)
        b = a * b_s + b
        a = a * a_s
        d *= 2
    return a, b


def _scan_up(c, b, row, tm):
    d = 1
    while d < tm:
        keep = row < tm - d
        c_s = jnp.where(keep, pltpu.roll(c, tm - d, 0), 1.0)
        b_s = jnp.where(keep, pltpu.roll(b, tm - d, 0), 0.0)
        b = c * b_s + b
        c = c * c_s
        d *= 2
    return c, b


def _rnn_fwd(xr, gr, conv_w, conv_b, wrg2, b_rg, wig2, b_ig, lam, n_seq, S, tm, phases=()):
    T = xr.shape[0]
    nt = S // tm
    W = D_MODEL

    def body(xr_ref, gr_ref, cw_ref, cb_ref, wrg_ref, brg_ref, wig_ref, big_ref, lam_ref,
             xc_ref, h_ref, ya_ref, px_ref, ph_ref):
        @pl.when(pl.program_id(1) == 0)
        def _():
            px_ref[...] = jnp.zeros_like(px_ref)
            ph_ref[...] = jnp.zeros_like(ph_ref)

        row = lax.broadcasted_iota(jnp.int32, (tm, W), 0)
        row8 = lax.broadcasted_iota(jnp.int32, (8, W), 0)
        x = xr_ref[...]
        xc = _conv_fwd(x, px_ref[...], cw_ref, cb_ref[...], row, row8, tm)
        sp, _ = _softplus_neg(lam_ref[...])
        _, r, i, a, mult = _lru_gates(xc, wrg_ref, brg_ref[...], wig_ref, big_ref[...], sp)
        bterm = mult * (i * xc)
        acum, hloc = _scan_down(a, bterm, row, tm)
        h = hloc + acum * ph_ref[7:8, :]
        h_ref[...] = h
        xc_ref[...] = xc
        gelu, _ = _gelu_and_grad(gr_ref[...])
        ya_ref[...] = (h * gelu).astype(BF16)
        px_ref[...] = xr_ref[tm - 8:tm, :]
        ph_ref[...] = h_ref[tm - 8:tm, :]

    tile = pl.BlockSpec((tm, W), lambda s, t: (s * nt + t, 0))
    return _call(
        body, phases=phases, name="rnn_fwd", grid=(n_seq, nt),
        in_specs=[tile, tile, _const((CONV_W, W)), _const((1, W)), _const((8, LANES, LANES)), _const((1, W)),
                  _const((8, LANES, LANES)), _const((1, W)), _const((1, W))],
        out_specs=[tile, tile, tile],
        out_shape=[jax.ShapeDtypeStruct((T, W), F32), jax.ShapeDtypeStruct((T, W), F32),
                   jax.ShapeDtypeStruct((T, W), BF16)],
        scratch_shapes=[pltpu.VMEM((8, W), F32), pltpu.VMEM((8, W), F32)],
    )(xr, gr, conv_w, conv_b, wrg2, b_rg, wig2, b_ig, lam)


def _rnn_bwd(dya, xr, gr, xc, h, conv_w, wrg2, b_rg, wig2, b_ig, lam, n_seq, S, tm, phases=()):
    T = xr.shape[0]
    nt = S // tm
    W = D_MODEL
    nb8 = tm // 8

    def body(dya_ref, xr_ref, gr_ref, xc_ref, h_ref, xprev_ref, hprev_ref, cw_ref, wrg_ref, brg_ref, wig_ref,
             big_ref, lam_ref, dxr_ref, dgr_ref, vec_ref, dwrg_ref, dwig_ref, cg_ref, ndxc_ref, tmp_ref):
        s, ti = pl.program_id(0), pl.program_id(1)

        @pl.when((s == 0) & (ti == 0))
        def _():
            vec_ref[...] = jnp.zeros_like(vec_ref)
            dwrg_ref[...] = jnp.zeros_like(dwrg_ref)
            dwig_ref[...] = jnp.zeros_like(dwig_ref)

        @pl.when(ti == 0)
        def _():
            cg_ref[...] = jnp.zeros_like(cg_ref)
            ndxc_ref[...] = jnp.zeros_like(ndxc_ref)

        first = ti == nt - 1
        row = lax.broadcasted_iota(jnp.int32, (tm, W), 0)
        row8 = lax.broadcasted_iota(jnp.int32, (8, W), 0)
        x = xr_ref[...]
        xc = xc_ref[...]
        hv = h_ref[...]
        xprev = jnp.where(first, 0.0, xprev_ref[...])
        hprev = jnp.where(first, 0.0, hprev_ref[...])
        sp, dsp_dlam = _softplus_neg(lam_ref[...])
        xcb, r, i, a, mult = _lru_gates(xc, wrg_ref, brg_ref[...], wig_ref, big_ref[...], sp)

        gelu, dgelu = _gelu_and_grad(gr_ref[...])
        dya_v = dya_ref[...]
        dgr_ref[...] = (dya_v * hv * dgelu).astype(BF16)
        dh = dya_v * gelu
        c = jnp.where(row < tm - 1, pltpu.roll(a, tm - 1, 0), 1.0)
        ccum, gloc = _scan_up(c, dh, row, tm)
        G = gloc + ccum * cg_ref[0:1, :]
        tmp_ref[...] = a * G
        cg_ref[...] = tmp_ref[0:8, :]

        h_m1 = _shift_down(hv, hprev, 1, row, row8, tm)
        ixc = i * xc
        dixc = G * mult
        dlog_a = (G * h_m1) * a - (G * ixc) * (a * a / mult)
        dr = dlog_a * ((-LRU_C) * sp)
        di = dixc * xc
        drg = dr * r * (1.0 - r)
        dig = di * i * (1.0 - i)
        vec_ref[7:8, :] += jnp.sum(dlog_a * ((-LRU_C) * r), axis=0, keepdims=True) * dsp_dlam
        vec_ref[5:6, :] += jnp.sum(drg, axis=0, keepdims=True)
        vec_ref[6:7, :] += jnp.sum(dig, axis=0, keepdims=True)
        drgb = drg.astype(BF16)
        digb = dig.astype(BF16)
        dxc = dixc * i + _blockdiag_dot(drgb, wrg_ref, True) + _blockdiag_dot(digb, wig_ref, True)
        for b in range(W // LANES):
            sl = slice(b * LANES, (b + 1) * LANES)
            dwrg_ref[b] += _dot_tn(xcb[:, sl], drgb[:, sl])
            dwig_ref[b] += _dot_tn(xcb[:, sl], digb[:, sl])

        vec_ref[4:5, :] += jnp.sum(dxc, axis=0, keepdims=True)
        vec_ref[3:4, :] += jnp.sum(dxc * x, axis=0, keepdims=True)
        dxr = cw_ref[CONV_W - 1:CONV_W, :] * dxc
        nxt = ndxc_ref[...]
        for sft in range(1, CONV_W):
            j = CONV_W - 1 - sft
            vec_ref[j:j + 1, :] += jnp.sum(dxc * _shift_down(x, xprev, sft, row, row8, tm), axis=0, keepdims=True)
            dxr = dxr + cw_ref[j:j + 1, :] * _shift_up(dxc, nxt, sft, row8, tm)
        dxr_ref[...] = dxr.astype(BF16)
        tmp_ref[...] = dxc
        ndxc_ref[...] = tmp_ref[0:8, :]

    rev = lambda s, t: (s * nt + nt - 1 - t, 0)
    tile = pl.BlockSpec((tm, W), rev)
    prev8 = pl.BlockSpec((8, W), lambda s, t: (jnp.maximum((s * nt + nt - 1 - t) * nb8 - 1, 0), 0))
    return _call(
        body, phases=phases, name="rnn_bwd", grid=(n_seq, nt),
        in_specs=[tile, tile, tile, tile, tile, prev8, prev8, _const((CONV_W, W)), _const((8, LANES, LANES)),
                  _const((1, W)), _const((8, LANES, LANES)), _const((1, W)), _const((1, W))],
        out_specs=[tile, tile, _const((16, W)), _const((8, LANES, LANES)), _const((8, LANES, LANES))],
        out_shape=[jax.ShapeDtypeStruct((T, W), BF16), jax.ShapeDtypeStruct((T, W), BF16),
                   jax.ShapeDtypeStruct((16, W), F32), jax.ShapeDtypeStruct((8, LANES, LANES), F32),
                   jax.ShapeDtypeStruct((8, LANES, LANES), F32)],
        scratch_shapes=[pltpu.VMEM((8, W), F32), pltpu.VMEM((8, W), F32), pltpu.VMEM((tm, W), F32)],
    )(dya, xr, gr, xc, h, xr, h, conv_w, wrg2, b_rg, wig2, b_ig, lam)


def _head_swap(t, lane):
    w = t.shape[1]
    return jnp.where(lane % HEAD_DIM < HEAD_DIM // 2, pltpu.roll(t, w - HEAD_DIM // 2, 1),
                     pltpu.roll(t, HEAD_DIM // 2, 1))


def _qk_prep(t, gain, cosf, sins, ind, indt, lane):
    ms = _split_dot(t * t, ind) * (1.0 / HEAD_DIM)
    rstd = _split_dot(lax.rsqrt(ms + NORM_EPS), indt)
    tn = (t * rstd) * gain
    return tn * cosf + _head_swap(tn, lane) * sins, rstd


def _qk_prep_bwd(dy, t, rstd, gain, cosf, sins, ind, indt, lane):
    dtn = dy * cosf + _head_swap(dy * sins, lane)
    dgain = jnp.sum(dtn * (t * rstd), axis=0, keepdims=True)
    dn = dtn * gain
    m = _split_dot(_split_dot(dn * t, ind), indt) * (1.0 / HEAD_DIM)
    return rstd * dn - t * (rstd * rstd * rstd * m), dgain


def _attn_mask(blk_idx):
    qi = lax.broadcasted_iota(jnp.int32, (WINDOW, 2 * WINDOW), 0)
    ci = lax.broadcasted_iota(jnp.int32, (WINDOW, 2 * WINDOW), 1)
    diff = WINDOW + qi - ci
    return (diff >= 0) & (diff < WINDOW) & ((ci >= WINDOW) | (blk_idx > 0))


def _stack_heads(t, kvh, lo):
    parts = []
    for i in (2 * kvh, 2 * kvh + 1):
        tp = t[:, i * LANES:(i + 1) * LANES]
        parts += [jnp.where(lo, tp, 0.0), jnp.where(lo, 0.0, tp)]
    return jnp.concatenate(parts, axis=0).astype(BF16)


def _unstack_heads(ts, lo):
    w = WINDOW
    return jnp.where(lo, ts[0:w], ts[w:2 * w]), jnp.where(lo, ts[2 * w:3 * w], ts[3 * w:4 * w])


def _dup_head(t, kvh, lo2):
    m = kvh // 2
    t2 = t[:, m * LANES:(m + 1) * LANES]
    t2r = pltpu.roll(t2, HEAD_DIM, 1)
    return (jnp.where(lo2, t2, t2r) if kvh % 2 == 0 else jnp.where(lo2, t2r, t2)).astype(BF16)


def _fold_head(ts, kvh, lo2):
    tot = ts + pltpu.roll(ts, HEAD_DIM, 1)
    own = lo2 if kvh % 2 == 0 else ~lo2
    return jnp.where(own, tot, 0.0)


def _softmax_sink(s, mask, sink):
    s = jnp.where(mask, s, -1e30)
    mx = jnp.maximum(jnp.max(s, axis=-1, keepdims=True), sink)
    e = jnp.exp(s - mx)
    es = jnp.exp(sink - mx)
    inv = 1.0 / (jnp.sum(e, axis=-1, keepdims=True) + es)
    return e * inv, es * inv


def _attn_fwd(q, k, v, qg, kg, sinks, cosf, sins, ind_q, ind_qt, ind_k, ind_kt, n_seq, S, phases=()):
    T = q.shape[0]
    nblk = S // WINDOW
    W = D_MODEL

    def body(sink_ref, q_ref, k_ref, v_ref, qg_ref, kg_ref, cos_ref, sin_ref, iq_ref, iqt_ref, ik_ref, ikt_ref,
             o_ref, kc_ref, vc_ref, s_ref, p_ref):
        n = pl.program_id(1)

        @pl.when(n == 0)
        def _():
            kc_ref[...] = jnp.zeros_like(kc_ref)
            vc_ref[...] = jnp.zeros_like(vc_ref)

        lane = lax.broadcasted_iota(jnp.int32, (WINDOW, W), 1)
        lo = lane[:, :LANES] < HEAD_DIM
        lo2 = lax.broadcasted_iota(jnp.int32, (2 * WINDOW, LANES), 1) < HEAD_DIM
        cosf, sinv = jnp.tile(cos_ref[...], (1, W // LANES)), jnp.tile(sin_ref[...], (1, W // LANES))
        qr, _ = _qk_prep(q_ref[...], qg_ref[...], cosf, sinv, iq_ref[...], iqt_ref[...], lane)
        kr, _ = _qk_prep(k_ref[...], kg_ref[...], cosf[:, :KV_W], sinv[:, :KV_W], ik_ref[...], ikt_ref[...],
                         lane[:, :KV_W])
        kc_ref[WINDOW:2 * WINDOW, :] = kr
        vc_ref[WINDOW:2 * WINDOW, :] = v_ref[...]
        kc, vc = kc_ref[...], vc_ref[...]
        mask = _attn_mask(n)
        qr = qr * HEAD_DIM ** -0.5
        for kvh in range(N_KV):
            b = kvh % 2
            s_ref[b] = _dot_nt(_stack_heads(qr, kvh, lo), _dup_head(kc, kvh, lo2))
            for r in range(4):
                rows = slice(r * WINDOW, (r + 1) * WINDOW)
                p, _ = _softmax_sink(s_ref[b, rows, :], mask, sink_ref[4 * kvh + r])
                p_ref[b, rows, :] = p.astype(BF16)
            o0, o1 = _unstack_heads(_dot(p_ref[b], _dup_head(vc, kvh, lo2)), lo)
            o_ref[:, (2 * kvh) * LANES:(2 * kvh + 1) * LANES] = o0.astype(BF16)
            o_ref[:, (2 * kvh + 1) * LANES:(2 * kvh + 2) * LANES] = o1.astype(BF16)
        kc_ref[0:WINDOW, :] = kr
        vc_ref[0:WINDOW, :] = v_ref[...]

    blk = lambda w: pl.BlockSpec((WINDOW, w), lambda s, n: (s * nblk + n, 0))
    pos = pl.BlockSpec((WINDOW, LANES), lambda s, n: (n, 0))
    outs, extra = _call(
        body, phases=phases, name="attn_fwd", grid=(n_seq, nblk),
        in_specs=[pl.BlockSpec(memory_space=pltpu.SMEM), blk(W), blk(KV_W), blk(KV_W), _const((1, W)),
                  _const((1, KV_W)), pos, pos, _const((W, LANES)), _const((LANES, W)), _const((KV_W, LANES)),
                  _const((LANES, KV_W))],
        out_specs=blk(W), out_shape=jax.ShapeDtypeStruct((T, W), BF16),
        scratch_shapes=[pltpu.VMEM((2 * WINDOW, KV_W), F32), pltpu.VMEM((2 * WINDOW, KV_W), F32),
                        pltpu.VMEM((2, 4 * WINDOW, 2 * WINDOW), F32), pltpu.VMEM((2, 4 * WINDOW, 2 * WINDOW), BF16)],
    )(sinks, q, k, v, qg, kg, cosf, sins, ind_q, ind_qt, ind_k, ind_kt)
    return outs[0], extra


def _attn_bwd(do, q, k, v, qg, kg, sinks, cosf, sins, ind_q, ind_qt, ind_k, ind_kt, n_seq, S, phases=()):
    T = q.shape[0]
    nblk = S // WINDOW
    W = D_MODEL

    def body(sink_ref, do_ref, q_ref, k_ref, v_ref, qg_ref, kg_ref, cos_ref, sin_ref, iq_ref, iqt_ref, ik_ref,
             ikt_ref, dq_ref, dkc_ref, dkp_ref, dvc_ref, dvp_ref, dqg_ref, dsk_ref, kc_ref, vc_ref, dqr_ref,
             dk_ref, dv_ref, s_ref, dp_ref, p_ref, ds_ref):
        s_id, n = pl.program_id(0), pl.program_id(1)

        @pl.when((s_id == 0) & (n == 0))
        def _():
            dqg_ref[...] = jnp.zeros_like(dqg_ref)
            dsk_ref[...] = jnp.zeros_like(dsk_ref)

        @pl.when(n == 0)
        def _():
            kc_ref[...] = jnp.zeros_like(kc_ref)
            vc_ref[...] = jnp.zeros_like(vc_ref)

        lane = lax.broadcasted_iota(jnp.int32, (WINDOW, W), 1)
        lane_k = lane[:, :KV_W]
        lane128 = lane[:, :LANES]
        cosf, sinv = jnp.tile(cos_ref[...], (1, W // LANES)), jnp.tile(sin_ref[...], (1, W // LANES))
        qv = q_ref[...]
        qr, q_rstd = _qk_prep(qv, qg_ref[...], cosf, sinv, iq_ref[...], iqt_ref[...], lane)
        kr, _ = _qk_prep(k_ref[...], kg_ref[...], cosf[:, :KV_W], sinv[:, :KV_W], ik_ref[...], ikt_ref[...], lane_k)
        kc_ref[WINDOW:2 * WINDOW, :] = kr
        vc_ref[WINDOW:2 * WINDOW, :] = v_ref[...]
        kc, vc = kc_ref[...], vc_ref[...]
        dov = do_ref[...]
        mask = _attn_mask(n)
        lo = lane128 < HEAD_DIM
        lo2 = lax.broadcasted_iota(jnp.int32, (2 * WINDOW, LANES), 1) < HEAD_DIM
        scale = HEAD_DIM ** -0.5
        qr = qr * scale
        dk_ref[...] = jnp.zeros_like(dk_ref)
        dv_ref[...] = jnp.zeros_like(dv_ref)
        dsk = jnp.zeros((WINDOW, LANES), F32)
        for kvh in range(N_KV):
            m, b = kvh // 2, kvh % 2
            qs = _stack_heads(qr, kvh, lo)
            dos = _stack_heads(dov, kvh, lo)
            kd = _dup_head(kc, kvh, lo2)
            vd = _dup_head(vc, kvh, lo2)
            s_ref[b] = _dot_nt(qs, kd)
            dp_ref[b] = _dot_nt(dos, vd)
            for r in range(4):
                rows = slice(r * WINDOW, (r + 1) * WINDOW)
                p, ps = _softmax_sink(s_ref[b, rows, :], mask, sink_ref[4 * kvh + r])
                dp = dp_ref[b, rows, :]
                dd = jnp.sum(p * dp, axis=-1, keepdims=True)
                p_ref[b, rows, :] = p.astype(BF16)
                ds_ref[b, rows, :] = (p * (dp - dd)).astype(BF16)
                dsk = dsk - jnp.where(lane128 == 4 * kvh + r, ps * dd, 0.0)
            dq0, dq1 = _unstack_heads(_dot(ds_ref[b], kd) * scale, lo)
            dqr_ref[:, (2 * kvh) * LANES:(2 * kvh + 1) * LANES] = dq0
            dqr_ref[:, (2 * kvh + 1) * LANES:(2 * kvh + 2) * LANES] = dq1
            dk_ref[:, m * LANES:(m + 1) * LANES] += _fold_head(_dot_tn(ds_ref[b], qs), kvh, lo2)
            dv_ref[:, m * LANES:(m + 1) * LANES] += _fold_head(_dot_tn(p_ref[b], dos), kvh, lo2)
        dsk_ref[...] += dsk
        dq, dqg = _qk_prep_bwd(dqr_ref[...], qv, q_rstd, qg_ref[...], cosf, sinv, iq_ref[...], iqt_ref[...], lane)
        dq_ref[...] = dq.astype(BF16)
        dqg_ref[...] += dqg
        dkp_ref[...] = dk_ref[0:WINDOW, :]
        dkc_ref[...] = dk_ref[WINDOW:2 * WINDOW, :]
        dvp_ref[...] = dv_ref[0:WINDOW, :]
        dvc_ref[...] = dv_ref[WINDOW:2 * WINDOW, :]
        kc_ref[0:WINDOW, :] = kr
        vc_ref[0:WINDOW, :] = v_ref[...]

    blk = lambda w: pl.BlockSpec((WINDOW, w), lambda s, n: (s * nblk + n, 0))
    pos = pl.BlockSpec((WINDOW, LANES), lambda s, n: (n, 0))
    kv_out = jax.ShapeDtypeStruct((T, KV_W), F32)
    stage = lambda dt: pltpu.VMEM((2, 4 * WINDOW, 2 * WINDOW), dt)
    return _call(
        body, phases=phases, name="attn_bwd", grid=(n_seq, nblk),
        in_specs=[pl.BlockSpec(memory_space=pltpu.SMEM), blk(W), blk(W), blk(KV_W), blk(KV_W), _const((1, W)),
                  _const((1, KV_W)), pos, pos, _const((W, LANES)), _const((LANES, W)), _const((KV_W, LANES)),
                  _const((LANES, KV_W))],
        out_specs=[blk(W), blk(KV_W), blk(KV_W), blk(KV_W), blk(KV_W), _const((1, W)), _const((WINDOW, LANES))],
        out_shape=[jax.ShapeDtypeStruct((T, W), BF16), kv_out, kv_out, kv_out, kv_out,
                   jax.ShapeDtypeStruct((1, W), F32), jax.ShapeDtypeStruct((WINDOW, LANES), F32)],
        scratch_shapes=[pltpu.VMEM((2 * WINDOW, KV_W), F32), pltpu.VMEM((2 * WINDOW, KV_W), F32),
                        pltpu.VMEM((WINDOW, W), F32), pltpu.VMEM((2 * WINDOW, KV_W), F32),
                        pltpu.VMEM((2 * WINDOW, KV_W), F32), stage(F32), stage(F32), stage(BF16), stage(BF16)],
    )(sinks, do, q, k, v, qg, kg, cosf, sins, ind_q, ind_qt, ind_k, ind_kt)


def _kv_bwd(dkc, dkp, dvc, dvp, k, kg, cosf, sins, ind_k, ind_kt, n_seq, S, phases=()):
    T = k.shape[0]
    nblk = S // WINDOW

    def body(dkc_ref, dkp_ref, dvc_ref, dvp_ref, k_ref, kg_ref, cos_ref, sin_ref, ik_ref, ikt_ref,
             dk_ref, dv_ref, dkg_ref):
        s_id, n = pl.program_id(0), pl.program_id(1)

        @pl.when((s_id == 0) & (n == 0))
        def _():
            dkg_ref[...] = jnp.zeros_like(dkg_ref)

        has_next = n < nblk - 1
        lane = lax.broadcasted_iota(jnp.int32, (WINDOW, KV_W), 1)
        dkr = dkc_ref[...] + jnp.where(has_next, dkp_ref[...], 0.0)
        dv_ref[...] = (dvc_ref[...] + jnp.where(has_next, dvp_ref[...], 0.0)).astype(BF16)
        cosf, sinv = jnp.tile(cos_ref[...], (1, KV_W // LANES)), jnp.tile(sin_ref[...], (1, KV_W // LANES))
        kv = k_ref[...]
        _, rstd = _qk_prep(kv, kg_ref[...], cosf, sinv, ik_ref[...], ikt_ref[...], lane)
        dk, dkg = _qk_prep_bwd(dkr, kv, rstd, kg_ref[...], cosf, sinv, ik_ref[...], ikt_ref[...], lane)
        dk_ref[...] = dk.astype(BF16)
        dkg_ref[...] += dkg

    cur = pl.BlockSpec((WINDOW, KV_W), lambda s, n: (s * nblk + n, 0))
    nxt = pl.BlockSpec((WINDOW, KV_W), lambda s, n: (s * nblk + jnp.minimum(n + 1, nblk - 1), 0))
    pos = pl.BlockSpec((WINDOW, LANES), lambda s, n: (n, 0))
    return _call(
        body, phases=phases, name="kv_bwd", grid=(n_seq, nblk),
        in_specs=[cur, nxt, cur, nxt, cur, _const((1, KV_W)), pos, pos, _const((KV_W, LANES)),
                  _const((LANES, KV_W))],
        out_specs=[cur, cur, _const((1, KV_W))],
        out_shape=[jax.ShapeDtypeStruct((T, KV_W), BF16), jax.ShapeDtypeStruct((T, KV_W), BF16),
                   jax.ShapeDtypeStruct((1, KV_W), F32)],
    )(dkc, dkp, dvc, dvp, k, kg, cosf, sins, ind_k, ind_kt)


def _merge_fwd(x, ya, o, ga, gb, w_rnn, w_attn, w_out, tm, phases=()):
    T = x.shape[0]
    W = D_MODEL

    def body(x_ref, ya_ref, o_ref, ga_ref, gb_ref, wr_ref, wa_ref, wo_ref, x1_ref, mg_ref, yao_ref, ybo_ref):
        y_a = _dot(ya_ref[...], wr_ref[...])
        y_b = _dot(o_ref[...], wa_ref[...])
        yao_ref[...] = y_a
        ybo_ref[...] = y_b
        mg = (_sigmoid(ga_ref[...]) * y_a + _sigmoid(gb_ref[...]) * y_b).astype(BF16)
        mg_ref[...] = mg
        x1_ref[...] = x_ref[...] + _dot(mg, wo_ref[...])

    row = pl.BlockSpec((tm, W), lambda i: (i, 0))
    sq = _const((W, W))
    return _call(
        body, phases=phases, name="merge_fwd", grid=(T // tm,),
        in_specs=[row, row, row, row, row, sq, sq, sq], out_specs=[row, row, row, row],
        out_shape=[jax.ShapeDtypeStruct((T, W), F32), jax.ShapeDtypeStruct((T, W), BF16),
                   jax.ShapeDtypeStruct((T, W), F32), jax.ShapeDtypeStruct((T, W), F32)],
    )(x, ya, o, ga, gb, w_rnn, w_attn, w_out)


def _merge_bwd(dx1, ga, gb, y_a, y_b, w_rnn, w_attn, w_out, tm, phases=()):
    T = dx1.shape[0]
    W = D_MODEL

    def body(dx1_ref, ga_ref, gb_ref, ya_ref, yb_ref, wr_ref, wa_ref, wo_ref,
             dga_ref, dgb_ref, dya_ref, dyb_ref, dyain_ref, do_ref):
        dm = _dot_nt(dx1_ref[...].astype(BF16), wo_ref[...])
        sa = _sigmoid(ga_ref[...])
        sb = _sigmoid(gb_ref[...])
        dga_ref[...] = (dm * ya_ref[...] * (sa * (1.0 - sa))).astype(BF16)
        dgb_ref[...] = (dm * yb_ref[...] * (sb * (1.0 - sb))).astype(BF16)
        dya = (dm * sa).astype(BF16)
        dyb = (dm * sb).astype(BF16)
        dya_ref[...] = dya
        dyb_ref[...] = dyb
        dyain_ref[...] = _dot_nt(dya, wr_ref[...])
        do_ref[...] = _dot_nt(dyb, wa_ref[...])

    row = pl.BlockSpec((tm, W), lambda i: (i, 0))
    sq = _const((W, W))
    b16 = jax.ShapeDtypeStruct((T, W), BF16)
    f32 = jax.ShapeDtypeStruct((T, W), F32)
    return _call(
        body, phases=phases, name="merge_bwd", grid=(T // tm,),
        in_specs=[row, row, row, row, row, sq, sq, sq], out_specs=[row] * 6,
        out_shape=[b16, b16, b16, b16, f32, f32],
    )(dx1, ga, gb, y_a, y_b, w_rnn, w_attn, w_out)


def _mlp_fwd(x1, g_mlp, w_up, w_down, tm, phases=()):
    T = x1.shape[0]
    W = D_MODEL

    def body(x_ref, g_ref, wu_ref, wd_ref, x2_ref, hm_ref, u_ref, act_ref):
        xv = x_ref[...]
        hm, _ = _rms_fwd(xv, g_ref[...])
        hmb = hm.astype(BF16)
        hm_ref[...] = hmb
        for j in range(N_CHIPS):
            u = _dot(hmb, wu_ref[j])
            u_ref[:, j * W:(j + 1) * W] = u
            ru = jnp.maximum(u, 0.0)
            act_ref[:, j * W:(j + 1) * W] = (ru * ru).astype(BF16)
        x2_ref[...] = xv + _dot(act_ref[...], wd_ref[...])

    row = lambda w: pl.BlockSpec((tm, w), lambda i: (i, 0))
    return _call(
        body, phases=phases, name="mlp_fwd", grid=(T // tm,),
        in_specs=[row(W), _const((1, W)), _const((N_CHIPS, W, W)), _const((D_FF, W))],
        out_specs=[row(W), row(W), row(D_FF), row(D_FF)],
        out_shape=[jax.ShapeDtypeStruct((T, W), F32), jax.ShapeDtypeStruct((T, W), BF16),
                   jax.ShapeDtypeStruct((T, D_FF), F32), jax.ShapeDtypeStruct((T, D_FF), BF16)],
    )(x1, g_mlp, w_up, w_down)


def _mlp_bwd(dx2, u, x1, g_mlp, w_up, w_down, tm, phases=()):
    T = x1.shape[0]
    W = D_MODEL

    def body(dx2_ref, u_ref, x_ref, g_ref, wu_ref, wd_ref, dx1_ref, du_ref, dg_ref):
        @pl.when(pl.program_id(0) == 0)
        def _():
            dg_ref[...] = jnp.zeros_like(dg_ref)

        dx2 = dx2_ref[...]
        dact = _dot_nt(dx2.astype(BF16), wd_ref[...])
        du_ref[...] = (dact * (2.0 * jnp.maximum(u_ref[...], 0.0))).astype(BF16)
        dhm = jnp.zeros((tm, W), F32)
        for j in range(N_CHIPS):
            dhm = dhm + _dot_nt(du_ref[:, j * W:(j + 1) * W], wu_ref[j])
        xv = x_ref[...]
        g = g_ref[...]
        _, r = _rms_fwd(xv, g)
        dx, dg = _rms_bwd(dhm, xv, r, g)
        dx1_ref[...] = dx2 + dx
        dg_ref[...] += dg

    row = lambda w: pl.BlockSpec((tm, w), lambda i: (i, 0))
    return _call(
        body, phases=phases, name="mlp_bwd", grid=(T // tm,),
        in_specs=[row(W), row(D_FF), row(W), _const((1, W)), _const((N_CHIPS, W, W)), _const((D_FF, W))],
        out_specs=[row(W), row(D_FF), _const((1, W))],
        out_shape=[jax.ShapeDtypeStruct((T, W), F32), jax.ShapeDtypeStruct((T, D_FF), BF16),
                   jax.ShapeDtypeStruct((1, W), F32)],
    )(dx2, u, x1, g_mlp, w_up, w_down)


def _ple_loss(x2, p, target, g_ple, w_gate, w_proj, tm, phases=()):
    T = x2.shape[0]
    W = D_MODEL
    cw = W // N_CHIPS

    def body(x_ref, p_ref, t_ref, g_ref, wg_ref, wp_ref, loss_ref, dx2_ref, pb_ref, de_ref, hp_ref, dtg_ref, dg_ref):
        @pl.when(pl.program_id(0) == 0)
        def _():
            dg_ref[...] = jnp.zeros_like(dg_ref)
            loss_ref[...] = jnp.zeros_like(loss_ref)

        xv = x_ref[...]
        g = g_ref[...]
        pb = p_ref[...].astype(BF16)
        pb_ref[...] = pb
        e = jnp.concatenate([_dot(pb, wp_ref[j]) for j in range(N_CHIPS)], axis=1)
        hp, r = _rms_fwd(xv, g)
        hpb = hp.astype(BF16)
        hp_ref[...] = hpb
        sg = _sigmoid(_dot(hpb, wg_ref[...]))
        diff = (xv + e * sg) - t_ref[...]
        loss_ref[...] += jnp.sum(diff * diff) * (0.5 / W)
        dx3 = diff * (1.0 / W)
        de_ref[...] = (dx3 * sg).astype(BF16)
        dtg = (dx3 * e * (sg * (1.0 - sg))).astype(BF16)
        dtg_ref[...] = dtg
        dx, dg = _rms_bwd(_dot_nt(dtg, wg_ref[...]), xv, r, g)
        dx2_ref[...] = dx3 + dx
        dg_ref[...] += dg

    row = lambda w: pl.BlockSpec((tm, w), lambda i: (i, 0))
    b16 = lambda w: jax.ShapeDtypeStruct((T, w), BF16)
    return _call(
        body, phases=phases, name="ple_loss", grid=(T // tm,),
        in_specs=[row(W), row(PLE_DIM), row(W), _const((1, W)), _const((W, W)), _const((N_CHIPS, PLE_DIM, cw))],
        out_specs=[_const((8, LANES)), row(W), row(PLE_DIM), row(W), row(W), row(W), _const((1, W))],
        out_shape=[jax.ShapeDtypeStruct((8, LANES), F32), jax.ShapeDtypeStruct((T, W), F32), b16(PLE_DIM),
                   b16(W), b16(W), b16(W), jax.ShapeDtypeStruct((1, W), F32)],
    )(x2, p, target, g_ple, w_gate, w_proj)


def _adamw(w, g, m, v, name, tr, phases=()):
    R, C = w.shape
    c1 = 1.0 / (1.0 - ADAM_B1 ** ADAM_STEP)
    c2 = 1.0 / (1.0 - ADAM_B2 ** ADAM_STEP)

    def body(w_ref, g_ref, m_ref, v_ref, go_ref, d_ref, nm_ref, nv_ref):
        gv = g_ref[...]
        go_ref[...] = gv
        nm = ADAM_B1 * m_ref[...] + (1.0 - ADAM_B1) * gv
        nv = ADAM_B2 * v_ref[...] + (1.0 - ADAM_B2) * (gv * gv)
        nm_ref[...] = nm
        nv_ref[...] = nv
        d_ref[...] = (-ADAM_LR) * ((nm * c1) / (jnp.sqrt(nv * c2) + ADAM_EPS) + ADAM_WD * w_ref[...])

    row = pl.BlockSpec((tr, C), lambda i: (i, 0))
    sds = jax.ShapeDtypeStruct((R, C), F32)
    return _call(
        body, phases=phases, name=name, grid=(R // tr,), in_specs=[row] * 4, out_specs=[row] * 4,
        out_shape=[sds] * 4,
    )(w, g, m, v)


def _indicator(width):
    ind = np.zeros((width, LANES), np.float32)
    ind[np.arange(width), np.arange(width) // HEAD_DIM] = 1.0
    return jnp.asarray(ind, BF16), jnp.asarray(ind.T, BF16)


def _rope_tables(S):
    inv = ROPE_THETA ** (-jnp.arange(0, HEAD_DIM, 2, dtype=F32) / HEAD_DIM)
    ang = jnp.arange(S, dtype=F32)[:, None] * inv[None, :]
    cos, sin = jnp.cos(ang), jnp.sin(ang)
    cosf = jnp.tile(jnp.concatenate([cos, cos], axis=1), (1, LANES // HEAD_DIM))
    sins = jnp.tile(jnp.concatenate([-sin, sin], axis=1), (1, LANES // HEAD_DIM))
    return cosf, sins


def _pair_blockdiag(w):
    w4 = w.reshape(8, 2, HEAD_DIM, HEAD_DIM)
    eye = jnp.eye(2, dtype=w.dtype)
    return jnp.einsum("bpij,pq->bpiqj", w4, eye).reshape(8, LANES, LANES)


def _pair_blockdiag_extract(g):
    g5 = g.reshape(8, 2, HEAD_DIM, 2, HEAD_DIM)
    return jnp.stack([g5[:, 0, :, 0, :], g5[:, 1, :, 1, :]], axis=1).reshape(16, HEAD_DIM, HEAD_DIM)


def _pair_sum(parts, sibs, name):
    n = len(parts)
    dims = [(p.shape[1] // 2, p.shape[2]) for p in parts]

    def body(*refs):
        p_r, s_r, send_r, own_r, mine_r, sem = (refs[0:n], refs[n:2 * n], refs[2 * n:3 * n], refs[3 * n:4 * n],
                                                refs[4 * n:5 * n], refs[5 * n])
        x, y, c, chips = _mesh_pos()
        me = 2 * x + y
        loads = []
        for i, (R, _) in enumerate(dims):
            mine, _ = _half_rows(c, R)
            cp = pltpu.make_async_copy(p_r[i].at[:, mine, :], mine_r[i], sem.at[i])
            cp.start()
            loads.append(cp)
        for i in range(n):
            loads[i].wait()
            for j, (cx, cy) in enumerate(chips):
                k = 2 * cx + cy
                send_r[i][j] = (mine_r[i][k] + s_r[i][k]).astype(BF16)
            own_r[i][...] = mine_r[i][me] + s_r[i][me]

    vm = pl.BlockSpec(memory_space=pltpu.VMEM)
    out = pl.pallas_call(
        body, name=name, in_specs=[pl.BlockSpec(memory_space=pl.ANY)] * n + [vm] * n, out_specs=[vm] * (2 * n),
        out_shape=[jax.ShapeDtypeStruct((3, R, C), BF16) for R, C in dims]
        + [jax.ShapeDtypeStruct((R, C), F32) for R, C in dims],
        scratch_shapes=[pltpu.VMEM((N_CHIPS, R, C), F32) for R, C in dims] + [pltpu.SemaphoreType.DMA((n,))],
        compiler_params=pltpu.CompilerParams(vmem_limit_bytes=VMEM_LIMIT),
    )(*parts, *sibs)
    return out[:n], out[n:]


def _chip_sum(owns, recvs, name):
    n = len(owns)
    dims = [o.shape for o in owns]

    def body(*refs):
        own_r, recv_r, red_r, stage_r, sem = refs[0:n], refs[n:2 * n], refs[2 * n:3 * n], refs[3 * n:4 * n], refs[4 * n]
        x, y, c, _ = _mesh_pos()
        me = 2 * x + y
        stores = []
        for i, (R, _) in enumerate(dims):
            acc = None
            for k in range(N_CHIPS):
                term = jnp.where(me == k, own_r[i][...], recv_r[i][_peer_slot(k, x, y)].astype(F32))
                acc = term if acc is None else acc + term
            stage_r[i][...] = acc
            mine, _ = _half_rows(c, R)
            cp = pltpu.make_async_copy(stage_r[i], red_r[i].at[mine, :], sem.at[i])
            cp.start()
            stores.append(cp)
        for cp in stores:
            cp.wait()

    vm = pl.BlockSpec(memory_space=pltpu.VMEM)
    return pl.pallas_call(
        body, name=name, in_specs=[vm] * (2 * n), out_specs=[pl.BlockSpec(memory_space=pl.ANY)] * n,
        out_shape=[jax.ShapeDtypeStruct((2 * R, C), F32) for R, C in dims],
        scratch_shapes=[pltpu.VMEM((R, C), F32) for R, C in dims] + [pltpu.SemaphoreType.DMA((n,))],
        compiler_params=pltpu.CompilerParams(vmem_limit_bytes=VMEM_LIMIT),
    )(*owns, *recvs)


def _gather_bf16(shard, name):
    R2, C = shard.shape
    R = R2 // 2

    def body(s_ref, o_ref, send_sems, recv_sems):
        x, y, c, chips = _mesh_pos()
        me = 2 * x + y
        mine = pl.ds(pl.multiple_of(c * R, R), R)
        theirs = pl.ds(pl.multiple_of((1 - c) * R, R), R)
        o_ref[me] = s_ref[...].astype(BF16)

        def copy(k, chip, rows, to):
            blk = o_ref.at[chip, rows]
            return pltpu.make_async_remote_copy(src_ref=blk, dst_ref=blk, send_sem=send_sems.at[k],
                                                recv_sem=recv_sems.at[k], device_id=to, device_id_type=MESH_ID)

        first = [copy(j, me, mine, (cx, cy, c)) for j, (cx, cy) in enumerate(chips)]
        for cp in first:
            cp.start()
        passed = []
        for j, (cx, cy) in enumerate(chips):
            copy(j, 2 * cx + cy, mine, (x, y, c)).wait_recv()
            cp = copy(3 + j, 2 * cx + cy, mine, (x, y, 1 - c))
            cp.start()
            passed.append(cp)
        for j, (cx, cy) in enumerate(chips):
            copy(3 + j, 2 * cx + cy, theirs, (x, y, c)).wait_recv()
        for cp in first + passed:
            cp.wait_send()

    return pl.pallas_call(
        body, name=name, out_shape=jax.ShapeDtypeStruct((N_CHIPS, R2, C), BF16),
        in_specs=[pl.BlockSpec(memory_space=pltpu.VMEM)], out_specs=pl.BlockSpec(memory_space=pltpu.VMEM),
        scratch_shapes=[pltpu.SemaphoreType.DMA((6,)), pltpu.SemaphoreType.DMA((6,))],
        compiler_params=pltpu.CompilerParams(vmem_limit_bytes=VMEM_LIMIT),
    )(shard)


def _pair_exchange_sum(partial, name):
    _, R2, C = partial.shape
    R = R2 // 2

    def body(p_ref, send_ref, own_ref, mine_ref, sib_ref, loc_sems, send_sems, recv_sems):
        x, y, c, chips = _mesh_pos()
        me = 2 * x + y
        mine, theirs = _half_rows(c, R)
        order = [2 * cx + cy for cx, cy in chips] + [me]
        locs, pairs = [], []
        for i, k in enumerate(order):
            loc = pltpu.make_async_copy(p_ref.at[k, mine, :], mine_ref.at[i], loc_sems.at[i])
            pair = _remote(p_ref.at[k, theirs, :], sib_ref.at[i], (send_sems.at[i], recv_sems.at[i]), (x, y, 1 - c))
            loc.start()
            pair.start()
            locs.append(loc)
            pairs.append(pair)
        for i in range(N_CHIPS):
            locs[i].wait()
            pairs[i].wait_recv()
            total = mine_ref[i] + sib_ref[i]
            if i < 3:
                send_ref[i] = total.astype(BF16)
            else:
                own_ref[...] = total
        for pair in pairs:
            pair.wait_send()

    vm = pl.BlockSpec(memory_space=pltpu.VMEM)
    return pl.pallas_call(
        body, name=name, in_specs=[pl.BlockSpec(memory_space=pl.ANY)], out_specs=[vm, vm],
        out_shape=[jax.ShapeDtypeStruct((3, R, C), BF16), jax.ShapeDtypeStruct((R, C), F32)],
        scratch_shapes=[pltpu.VMEM((N_CHIPS, R, C), F32), pltpu.VMEM((N_CHIPS, R, C), F32),
                        pltpu.SemaphoreType.DMA((N_CHIPS,)), pltpu.SemaphoreType.DMA((N_CHIPS,)),
                        pltpu.SemaphoreType.DMA((N_CHIPS,))],
        compiler_params=pltpu.CompilerParams(vmem_limit_bytes=VMEM_LIMIT),
    )(partial)


def _allreduce_small(buf, name):
    shape = buf.shape

    def body(b_ref, o_ref, sib_ref, pair_ref, in_ref, pair_sems, send_sems, recv_sems):
        x, y, c, chips = _mesh_pos()
        me = 2 * x + y
        pair = pltpu.make_async_remote_copy(src_ref=b_ref, dst_ref=sib_ref, send_sem=pair_sems.at[0],
                                            recv_sem=pair_sems.at[1], device_id=(x, y, 1 - c), device_id_type=MESH_ID)
        pair.start()
        pair.wait()
        pair_ref[...] = b_ref[...] + sib_ref[...]
        sends = []
        for j, (cx, cy) in enumerate(chips):
            cp = pltpu.make_async_remote_copy(src_ref=pair_ref, dst_ref=in_ref.at[j], send_sem=send_sems.at[j],
                                              recv_sem=recv_sems.at[j], device_id=(cx, cy, c), device_id_type=MESH_ID)
            cp.start()
            sends.append(cp)
        for cp in sends:
            cp.wait_recv()
        acc = None
        for k in range(N_CHIPS):
            term = jnp.where(me == k, pair_ref[...], in_ref[_peer_slot(k, x, y)])
            acc = term if acc is None else acc + term
        o_ref[...] = acc
        for cp in sends:
            cp.wait_send()

    return pl.pallas_call(
        body, name=name, out_shape=jax.ShapeDtypeStruct(shape, F32),
        in_specs=[pl.BlockSpec(memory_space=pltpu.VMEM)], out_specs=pl.BlockSpec(memory_space=pltpu.VMEM),
        scratch_shapes=[pltpu.VMEM(shape, F32), pltpu.VMEM(shape, F32), pltpu.VMEM((3,) + shape, F32),
                        pltpu.SemaphoreType.DMA((2,)), pltpu.SemaphoreType.DMA((3,)), pltpu.SemaphoreType.DMA((3,))],
        compiler_params=pltpu.CompilerParams(vmem_limit_bytes=VMEM_LIMIT),
    )(buf)


def _adamw_small(ws, gs, ms, vs):
    n = len(ws)
    c1 = 1.0 / (1.0 - ADAM_B1 ** ADAM_STEP)
    c2 = 1.0 / (1.0 - ADAM_B2 ** ADAM_STEP)

    def body(*refs):
        w_r, g_r, m_r, v_r = refs[0:n], refs[n:2 * n], refs[2 * n:3 * n], refs[3 * n:4 * n]
        d_r, nm_r, nv_r = refs[4 * n:5 * n], refs[5 * n:6 * n], refs[6 * n:7 * n]
        for i in range(n):
            gv = g_r[i][...]
            nm = ADAM_B1 * m_r[i][...] + (1.0 - ADAM_B1) * gv
            nv = ADAM_B2 * v_r[i][...] + (1.0 - ADAM_B2) * (gv * gv)
            nm_r[i][...] = nm
            nv_r[i][...] = nv
            d_r[i][...] = (-ADAM_LR) * ((nm * c1) / (jnp.sqrt(nv * c2) + ADAM_EPS) + ADAM_WD * w_r[i][...])

    vm = pl.BlockSpec(memory_space=pltpu.VMEM)
    sds = [jax.ShapeDtypeStruct(w.shape, F32) for w in ws]
    out = pl.pallas_call(body, name="adamw_small", in_specs=[vm] * (4 * n), out_specs=[vm] * (3 * n),
                         out_shape=sds * 3)(*ws, *gs, *ms, *vs)
    return out[0:n], out[n:2 * n], out[2 * n:3 * n]


_BIG = ("w_in", "w_rnn_proj", "w_attn_proj", "w_out", "w_up", "w_down", "w_ple_gate", "w_ple_proj")
_SMALL = ("g_mix", "conv_w", "conv_b", "w_rg", "b_rg", "w_ig", "b_ig", "lru_lambda", "q_gain", "k_gain", "sinks",
          "g_mlp", "g_ple")
_WEIGHTS = ("g_mix", "w_in", "conv_w", "conv_b", "w_rg", "b_rg", "w_ig", "b_ig", "lru_lambda", "w_rnn_proj",
            "q_gain", "k_gain", "sinks", "w_attn_proj", "w_out", "g_mlp", "w_up", "w_down", "g_ple", "w_ple_gate",
            "w_ple_proj")


def _pad_row(v):
    v = v.reshape(1, -1)
    return jnp.pad(v, ((0, 0), (0, D_MODEL - v.shape[1])))


def kernel(x, p, g_mix, w_in, conv_w, conv_b, w_rg, b_rg, w_ig, b_ig, lru_lambda, w_rnn_proj, q_gain, k_gain, sinks, w_attn_proj, w_out, g_mlp, w_up, w_down, g_ple, w_ple_gate, w_ple_proj, loss_target, m_g_mix, m_w_in, m_conv_w, m_conv_b, m_w_rg, m_b_rg, m_w_ig, m_b_ig, m_lru_lambda, m_w_rnn_proj, m_q_gain, m_k_gain, m_sinks, m_w_attn_proj, m_w_out, m_g_mlp, m_w_up, m_w_down, m_g_ple, m_w_ple_gate, m_w_ple_proj, v_g_mix, v_w_in, v_conv_w, v_conv_b, v_w_rg, v_b_rg, v_w_ig, v_b_ig, v_lru_lambda, v_w_rnn_proj, v_q_gain, v_k_gain, v_sinks, v_w_attn_proj, v_w_out, v_g_mlp, v_w_up, v_w_down, v_g_ple, v_w_ple_gate, v_w_ple_proj):
    w = dict(g_mix=g_mix, w_in=w_in, conv_w=conv_w, conv_b=conv_b, w_rg=w_rg, b_rg=b_rg, w_ig=w_ig, b_ig=b_ig,
             lru_lambda=lru_lambda, w_rnn_proj=w_rnn_proj, q_gain=q_gain, k_gain=k_gain, sinks=sinks,
             w_attn_proj=w_attn_proj, w_out=w_out, g_mlp=g_mlp, w_up=w_up, w_down=w_down, g_ple=g_ple,
             w_ple_gate=w_ple_gate, w_ple_proj=w_ple_proj)
    m = dict(g_mix=m_g_mix, w_in=m_w_in, conv_w=m_conv_w, conv_b=m_conv_b, w_rg=m_w_rg, b_rg=m_b_rg, w_ig=m_w_ig,
             b_ig=m_b_ig, lru_lambda=m_lru_lambda, w_rnn_proj=m_w_rnn_proj, q_gain=m_q_gain, k_gain=m_k_gain,
             sinks=m_sinks, w_attn_proj=m_w_attn_proj, w_out=m_w_out, g_mlp=m_g_mlp, w_up=m_w_up, w_down=m_w_down,
             g_ple=m_g_ple, w_ple_gate=m_w_ple_gate, w_ple_proj=m_w_ple_proj)
    v = dict(g_mix=v_g_mix, w_in=v_w_in, conv_w=v_conv_w, conv_b=v_conv_b, w_rg=v_w_rg, b_rg=v_b_rg, w_ig=v_w_ig,
             b_ig=v_b_ig, lru_lambda=v_lru_lambda, w_rnn_proj=v_w_rnn_proj, q_gain=v_q_gain, k_gain=v_k_gain,
             sinks=v_sinks, w_attn_proj=v_w_attn_proj, w_out=v_w_out, g_mlp=v_g_mlp, w_up=v_w_up, w_down=v_w_down,
             g_ple=v_g_ple, w_ple_gate=v_w_ple_gate, w_ple_proj=v_w_ple_proj)
    n_seq, S, _ = x.shape
    T = n_seq * S
    chip = 2 * lax.axis_index("x") + lax.axis_index("y")

    tm, tm_rnn = 512, 256
    xf, pf, tf = x.reshape(T, D_MODEL), p.reshape(T, PLE_DIM), loss_target.reshape(T, D_MODEL)
    first = lambda outs: [o[0] for o in outs]

    w_in_g = _gather_bf16(w["w_in"][0], "gather_w_in")
    wb = {name: w[name][0].astype(BF16) for name in _BIG if name != "w_in"}
    grp_mix, grp_mlp, grp_ple = ("w_rnn_proj", "w_attn_proj", "w_out"), ("w_up", "w_down"), ("w_ple_gate", "w_ple_proj")

    cw_full = jnp.zeros((8, D_MODEL), F32)
    cw_full = lax.dynamic_update_slice(cw_full, conv_w[0], (0, chip * (D_MODEL // N_CHIPS)))
    cw_full = _allreduce_small(0.5 * cw_full.reshape(64, LANES), "allgather_conv_w").reshape(8, D_MODEL)[0:CONV_W]

    cosf, sins = _rope_tables(S)
    ind_q, ind_qt = _indicator(D_MODEL)
    ind_k, ind_kt = _indicator(KV_W)
    wrg2 = _pair_blockdiag(w_rg[0]).astype(BF16)
    wig2 = _pair_blockdiag(w_ig[0]).astype(BF16)
    qg = jnp.tile(q_gain, (1, N_HEADS))
    kg = jnp.tile(k_gain, (1, N_KV))
    sk = sinks.reshape(N_HEADS)
    rnn_w = (cw_full, conv_b, wrg2, b_rg, wig2, b_ig, lru_lambda)
    attn_c = (qg, kg, sk, cosf, sins, ind_q, ind_qt, ind_k, ind_kt, n_seq, S)

    (h0, xr, gr, zq, zk, zv, ga, gb), ph = _inproj_fwd(xf, g_mix, w_in_g, tm,
                                                     phases=[_ph_gather_send(wb[n]) for n in grp_mix])
    g_mixw = first(ph)
    o, ph = _attn_fwd(zq, zk, zv, *attn_c,
                      phases=[_ph_gather_pass(g) for g in g_mixw] + [_ph_gather_send(wb["w_up"])])
    g_mixw, wu = first(ph[:3]), ph[3][0]
    (xc, h, ya), ph = _rnn_fwd(xr, gr, *rnn_w, n_seq, S, tm_rnn,
                               phases=[_ph_gather_pass(wu), _ph_gather_send(wb["w_down"])])
    wu, wd = ph[0][0], ph[1][0]
    wr, wa, wo = (g.reshape(D_MODEL, D_MODEL) for g in g_mixw)
    (x1, merged, y_a, y_b), ph = _merge_fwd(xf, ya, o, ga, gb, wr, wa, wo, tm,
                                            phases=[_ph_gather_pass(wd)] + [_ph_gather_send(wb[n]) for n in grp_ple])
    wd, g_plew = ph[0][0].reshape(D_FF, D_MODEL), first(ph[1:])
    (x2, hm, u, act), ph = _mlp_fwd(x1, g_mlp, wu, wd, tm // 2, phases=[_ph_gather_pass(g) for g in g_plew])
    wpg, wpp = first(ph)
    wpg = wpg.reshape(D_MODEL, D_MODEL)
    (loss_t, dx2, pb, de, hp, dtg, dg_ple), _ = _ple_loss(x2, pf, tf, g_ple, wpg, wpp, tm)

    chipmajor = lambda g: g.reshape(N_CHIPS, g.shape[-2] // N_CHIPS, g.shape[-1]) if g.ndim == 2 else g
    part_ple = [chipmajor(_wgrad(hp, dtg, "wgrad_ple_gate", False, D_MODEL, tm)[0]),
                _wgrad(pb, de, "wgrad_ple_proj", True, D_MODEL // N_CHIPS, tm)[0]]
    (dx1, du, dg_mlp), ph = _mlp_bwd(dx2, u, x1, g_mlp, wu, wd, tm // 2, phases=[_ph_pair_send(g) for g in part_ple])
    send_ple, own_ple = _pair_sum(part_ple, first(ph), "pair_sum_ple")
    dw_down, ph = _wgrad(act, dx2, "wgrad_down", False, D_MODEL // 2, tm, phases=[_ph_chip_send(s) for s in send_ple])
    red_ple = _chip_sum(own_ple, first(ph), "chip_sum_ple")
    part_mlp = [_wgrad(hm, du, "wgrad_up", True, D_MODEL, tm)[0], chipmajor(dw_down)]
    (dga, dgb, dya, dyb, dyain, do), ph = _merge_bwd(
        dx1, ga, gb, y_a, y_b, wr, wa, wo, tm,
        phases=[_ph_half_swap(r) for r in red_ple] + [_ph_pair_send(g) for g in part_mlp])
    red_ple = first(ph[:2])
    send_mlp, own_mlp = _pair_sum(part_mlp, first(ph[2:]), "pair_sum_mlp")
    part_mix = [chipmajor(_wgrad(ya, dya, "wgrad_rnn_proj", False, D_MODEL, tm)[0]),
                chipmajor(_wgrad(o, dyb, "wgrad_attn_proj", False, D_MODEL, tm)[0]),
                chipmajor(_wgrad(merged, dx1, "wgrad_out", False, D_MODEL, tm)[0])]
    (dxr, dgr, vec, dwrg2, dwig2), ph = _rnn_bwd(
        dyain, xr, gr, xc, h, cw_full, wrg2, b_rg, wig2, b_ig, lru_lambda, n_seq, S, tm_rnn,
        phases=[_ph_chip_send(s) for s in send_mlp] + [_ph_pair_send(g) for g in part_mix])
    red_mlp = _chip_sum(own_mlp, first(ph[:2]), "chip_sum_mlp")
    send_mix, own_mix = _pair_sum(part_mix, first(ph[2:]), "pair_sum_mix")
    (dq, dkc, dkp, dvc, dvp, dqg, dsk), ph = _attn_bwd(
        do, zq, zk, zv, *attn_c, phases=[_ph_half_swap(r) for r in red_mlp] + [_ph_chip_send(s) for s in send_mix])
    red_mlp = first(ph[:2])
    red_mix = _chip_sum(own_mix, first(ph[2:]), "chip_sum_mix")
    (dk, dv, dkg), _ = _kv_bwd(dkc, dkp, dvc, dvp, zk, kg, cosf, sins, ind_k, ind_kt, n_seq, S)
    dz_parts = [dxr, dgr, dq, dk, dv, dga, dgb]
    send_in, own_in = _pair_exchange_sum(_wgrad_in(h0, dz_parts, tm), "pair_sum_in")
    (grad_x, dg_mix), ph = _inproj_bwd(dz_parts, w_in_g, xf, g_mix, dx1, tm,
                                       phases=[_ph_half_swap(r) for r in red_mix] + [_ph_chip_send(send_in)])
    red_mix = first(ph[:3])
    red_in = _chip_sum([own_in], first(ph[3:]), "chip_sum_in")
    reduced = dict(zip(grp_ple + grp_mlp + grp_mix, red_ple + red_mlp + red_mix))
    grads = {
        "g_mix": dg_mix[0], "g_mlp": dg_mlp[0], "g_ple": dg_ple[0],
        "conv_w": vec[0:CONV_W], "conv_b": vec[4], "b_rg": vec[5], "b_ig": vec[6], "lru_lambda": vec[7],
        "w_rg": _pair_blockdiag_extract(dwrg2), "w_ig": _pair_blockdiag_extract(dwig2),
        "q_gain": dqg.reshape(N_HEADS, HEAD_DIM).sum(0), "k_gain": dkg.reshape(N_KV, HEAD_DIM).sum(0),
        "sinks": dsk.sum(0)[:N_HEADS],
    }

    rows = [grads["conv_w"], _pad_row(grads["conv_b"]), _pad_row(grads["b_rg"]), _pad_row(grads["b_ig"]),
            _pad_row(grads["lru_lambda"]), _pad_row(grads["g_mix"]), _pad_row(grads["g_mlp"]),
            _pad_row(grads["g_ple"]), _pad_row(grads["q_gain"]), _pad_row(grads["k_gain"]), _pad_row(grads["sinks"]),
            _pad_row(loss_t[0:1, 0:1]), jnp.zeros((1, D_MODEL), F32)]
    vecs = jnp.concatenate(rows, axis=0)
    packed = jnp.concatenate([vecs.reshape(-1, LANES), grads["w_rg"].reshape(-1, LANES),
                              grads["w_ig"].reshape(-1, LANES)], axis=0)
    red = _allreduce_small(packed, "allreduce_small")
    nv = vecs.size // LANES
    rvec = red[0:nv].reshape(16, D_MODEL)
    loss = rvec[14, 0]
    nw = grads["w_rg"].size // LANES
    sg = {
        "conv_w": lax.dynamic_slice(rvec[0:CONV_W], (0, chip * (D_MODEL // N_CHIPS)), (CONV_W, D_MODEL // N_CHIPS)),
        "conv_b": rvec[4], "b_rg": rvec[5], "b_ig": rvec[6], "lru_lambda": rvec[7], "g_mix": rvec[8],
        "g_mlp": rvec[9], "g_ple": rvec[10], "q_gain": rvec[11, :HEAD_DIM], "k_gain": rvec[12, :HEAD_DIM],
        "sinks": rvec[13, :N_HEADS], "w_rg": red[nv:nv + nw], "w_ig": red[nv + nw:nv + 2 * nw],
    }
    sg = {k: sg[k].reshape(w[k].shape) for k in _SMALL}
    d_s, m_s, v_s = _adamw_small([w[k] for k in _SMALL], [sg[k] for k in _SMALL], [m[k] for k in _SMALL],
                                 [v[k] for k in _SMALL])
    grad, delta, new_m, new_v = dict(sg), dict(zip(_SMALL, d_s)), dict(zip(_SMALL, m_s)), dict(zip(_SMALL, v_s))

    for name in ("w_ple_proj", "w_up", "w_down", "w_rnn_proj", "w_attn_proj", "w_out", "w_ple_gate", "w_in"):
        shape = w[name].shape
        outs, ph = _adamw(w[name][0], reduced[name], m[name][0], v[name][0], "adamw_" + name, 128,
                          phases=[_ph_half_swap(r) for r in red_in] if name == "w_ple_proj" else ())
        if name == "w_ple_proj":
            reduced["w_in"] = ph[0][0]
        grad[name], delta[name], new_m[name], new_v[name] = (a.reshape(shape) for a in outs)

    return (loss, grad_x.reshape(x.shape), *[grad[k] for k in _WEIGHTS], *[delta[k] for k in _WEIGHTS],
            *[new_m[k] for k in _WEIGHTS], *[new_v[k] for k in _WEIGHTS])
```

```python
import functools
import math

import numpy as np
import jax
import jax.numpy as jnp
from jax import lax
from jax.experimental import pallas as pl
from jax.experimental.pallas import tpu as pltpu

F32 = jnp.float32
BF16 = jnp.bfloat16

D_MODEL = 1024
N_HEADS = 16
N_KV = 4
HEAD_DIM = 64
KV_W = N_KV * HEAD_DIM
D_FF = 4096
PLE_DIM = 256
WINDOW = 128
CONV_W = 4
LRU_C = 8.0
NORM_EPS = 1e-6
ROPE_THETA = 10000.0
N_CHIPS = 4
IN_TOTAL = 5632
IN_BLK = IN_TOTAL // N_CHIPS
IN_SEGS = (0, 1024, 2048, 3072, 3328, 3584, 4608, 5632)

ADAM_LR = 0.001
ADAM_B1 = 0.9
ADAM_B2 = 0.999
ADAM_EPS = 1e-08
ADAM_WD = 0.01
ADAM_STEP = 10

LANES = 128
VMEM_LIMIT = 56 * 1024 * 1024
MESH_ID = pl.DeviceIdType.MESH


def _dot(a, b):
    return jnp.dot(a, b, preferred_element_type=F32)


def _dot_nt(a, b):
    return lax.dot_general(a, b, (((1,), (1,)), ((), ())), preferred_element_type=F32)


def _dot_tn(a, b):
    return lax.dot_general(a, b, (((0,), (0,)), ((), ())), preferred_element_type=F32)


def _split_dot(x, ind):
    hi = x.astype(BF16)
    lo = (x - hi.astype(F32)).astype(BF16)
    return _dot(hi, ind) + _dot(lo, ind)


def _sigmoid(x):
    return 1.0 / (1.0 + jnp.exp(-x))


_GELU_C = math.sqrt(2.0 / math.pi)


def _gelu_and_grad(g):
    inner = _GELU_C * (g + 0.044715 * g * g * g)
    t = jnp.tanh(inner)
    gelu = 0.5 * g * (1.0 + t)
    dgelu = 0.5 * (1.0 + t) + 0.5 * g * (1.0 - t * t) * _GELU_C * (1.0 + 3.0 * 0.044715 * g * g)
    return gelu, dgelu


def _const(shape):
    nd = len(shape)
    return pl.BlockSpec(shape, lambda *_: (0,) * nd)


def _params(n_grid, vmem=VMEM_LIMIT):
    return pltpu.CompilerParams(dimension_semantics=("arbitrary",) * n_grid, vmem_limit_bytes=vmem)


def _rms_fwd(x, g):
    r = lax.rsqrt(jnp.mean(x * x, axis=-1, keepdims=True) + NORM_EPS)
    return (x * r) * g, r


def _rms_bwd(dy, x, r, g):
    dn = dy * g
    dx = r * dn - x * (r * r * r * jnp.mean(dn * x, axis=-1, keepdims=True))
    dg = jnp.sum(dy * (x * r), axis=0, keepdims=True)
    return dx, dg


def _seg_pieces(blk_lo, blk_hi):
    out = []
    for s in range(7):
        lo, hi = max(blk_lo, IN_SEGS[s]), min(blk_hi, IN_SEGS[s + 1])
        if lo < hi:
            out.append((s, lo - IN_SEGS[s], hi - IN_SEGS[s], lo - blk_lo))
    return out


def _mesh_pos():
    x, y, c = lax.axis_index("x"), lax.axis_index("y"), lax.axis_index("c")
    other_chips = [(1 - x, y), (x, 1 - y), (1 - x, 1 - y)]
    return x, y, c, other_chips


def _peer_slot(k, x, y):
    dx = jnp.bitwise_xor(k // 2, x)
    dy = jnp.bitwise_xor(k % 2, y)
    return jnp.maximum(dx + 2 * dy - 1, 0)


def _half_rows(c, R):
    return pl.ds(pl.multiple_of(c * R, R), R), pl.ds(pl.multiple_of((1 - c) * R, R), R)


def _remote(src, dst, sems, to):
    return pltpu.make_async_remote_copy(src_ref=src, dst_ref=dst, send_sem=sems[0], recv_sem=sems[1],
                                        device_id=to, device_id_type=MESH_ID)


class _Phase:
    def __init__(self, ins, inout, outs, n_remote, n_local, build):
        self.ins, self.inout, self.outs = list(ins), list(inout), list(outs)
        self.n_remote, self.n_local, self.build = n_remote, n_local, build


def _ph_gather_send(wb):
    R2, C = wb.shape
    R = R2 // 2

    def build(ins, outs, rsem, lsem):
        (w_ref,), (g_ref,) = ins, outs
        x, y, c, chips = _mesh_pos()
        me = 2 * x + y
        mine, _ = _half_rows(c, R)
        loc = [pltpu.make_async_copy(w_ref, g_ref.at[me], lsem(0))]
        outg = [_remote(w_ref.at[mine], g_ref.at[me, mine], rsem(j), (cx, cy, c)) for j, (cx, cy) in enumerate(chips)]
        inc = [functools.partial(_remote, w_ref.at[mine], g_ref.at[2 * cx + cy, mine], rsem(j), (x, y, c))
               for j, (cx, cy) in enumerate(chips)]
        return loc, outg, inc

    return _Phase([wb], [], [jax.ShapeDtypeStruct((N_CHIPS, R2, C), BF16)], 3, 1, build)


def _ph_gather_pass(gath):
    _, R2, C = gath.shape
    R = R2 // 2

    def build(ins, outs, rsem, lsem):
        (g_ref,) = outs
        x, y, c, chips = _mesh_pos()
        mine, theirs = _half_rows(c, R)
        outg, inc = [], []
        for j, (cx, cy) in enumerate(chips):
            blk = g_ref.at[2 * cx + cy, mine]
            outg.append(_remote(blk, blk, rsem(j), (x, y, 1 - c)))
            got = g_ref.at[2 * cx + cy, theirs]
            inc.append(functools.partial(_remote, got, got, rsem(j), (x, y, c)))
        return [], outg, inc

    return _Phase([], [gath], [], 3, 0, build)


def _ph_pair_send(partial):
    _, R2, C = partial.shape
    R = R2 // 2

    def build(ins, outs, rsem, lsem):
        (p_ref,), (s_ref,) = ins, outs
        x, y, c, _ = _mesh_pos()
        _, theirs = _half_rows(c, R)
        src = p_ref.at[:, theirs, :]
        return ([], [_remote(src, s_ref, rsem(0), (x, y, 1 - c))],
                [functools.partial(_remote, src, s_ref, rsem(0), (x, y, c))])

    return _Phase([partial], [], [jax.ShapeDtypeStruct((N_CHIPS, R, C), F32)], 1, 0, build)


def _ph_chip_send(sendb):
    def build(ins, outs, rsem, lsem):
        (s_ref,), (r_ref,) = ins, outs
        x, y, c, chips = _mesh_pos()
        outg = [_remote(s_ref.at[j], r_ref.at[j], rsem(j), (cx, cy, c)) for j, (cx, cy) in enumerate(chips)]
        inc = [functools.partial(_remote, s_ref.at[j], r_ref.at[j], rsem(j), (x, y, c)) for j in range(3)]
        return [], outg, inc

    return _Phase([sendb], [], [jax.ShapeDtypeStruct(sendb.shape, sendb.dtype)], 3, 0, build)


def _ph_half_swap(red):
    R2, C = red.shape
    R = R2 // 2

    def build(ins, outs, rsem, lsem):
        (r_ref,) = outs
        x, y, c, _ = _mesh_pos()
        mine, theirs = _half_rows(c, R)
        return ([], [_remote(r_ref.at[mine], r_ref.at[mine], rsem(0), (x, y, 1 - c))],
                [functools.partial(_remote, r_ref.at[theirs], r_ref.at[theirs], rsem(0), (x, y, c))])

    return _Phase([], [red], [], 1, 0, build)


def _call(body, *, name, grid, in_specs, out_specs, out_shape, scratch_shapes=(), phases=()):
    single = not isinstance(out_specs, (list, tuple))
    out_specs = [out_specs] if single else list(out_specs)
    out_shape = [out_shape] if single else list(out_shape)
    n_in, n_out, n_scr = len(in_specs), len(out_specs), len(scratch_shapes)
    if not phases:
        call = pl.pallas_call(body, name=name, grid=grid, in_specs=in_specs, out_specs=out_specs,
                              out_shape=out_shape, scratch_shapes=list(scratch_shapes),
                              compiler_params=_params(len(grid)))
        return lambda *operands: (list(call(*operands)), [])

    ex_in, ex_out, aliases, spans = [], [], {}, []
    for ph in phases:
        i0, o0 = len(ex_in), len(ex_out)
        ex_in += ph.ins
        for a in ph.inout:
            aliases[n_in + len(ex_in)] = n_out + len(ex_out)
            ex_in.append(a)
            ex_out.append(jax.ShapeDtypeStruct(a.shape, a.dtype))
        ex_out += ph.outs
        spans.append((i0, len(ph.ins), o0, len(ex_out) - o0))
    n_remote = sum(ph.n_remote for ph in phases)
    n_local = max(sum(ph.n_local for ph in phases), 1)

    def wrapped(*refs):
        base_in, xin = refs[:n_in], refs[n_in:n_in + len(ex_in)]
        o0 = n_in + len(ex_in)
        base_out, xout = refs[o0:o0 + n_out], refs[o0 + n_out:o0 + n_out + len(ex_out)]
        scr = refs[o0 + n_out + len(ex_out):]
        send_sems, recv_sems, loc_sems = scr[n_scr:]
        first = functools.reduce(jnp.logical_and, [pl.program_id(i) == 0 for i in range(len(grid))])
        last = functools.reduce(jnp.logical_and, [pl.program_id(i) == grid[i] - 1 for i in range(len(grid))])

        def copies():
            out, r0, l0 = [], 0, 0
            for ph, (i0, ni, p0, no) in zip(phases, spans):
                rsem = lambda k, r0=r0: (send_sems.at[r0 + k], recv_sems.at[r0 + k])
                lsem = lambda k, l0=l0: loc_sems.at[l0 + k]
                out.append(ph.build(xin[i0:i0 + ni], xout[p0:p0 + no], rsem, lsem))
                r0, l0 = r0 + ph.n_remote, l0 + ph.n_local
            return out

        @pl.when(first)
        def _():
            for loc, outg, _ in copies():
                for cp in loc + outg:
                    cp.start()

        body(*base_in, *base_out, *scr[:n_scr])

        @pl.when(last)
        def _():
            for loc, outg, inc in copies():
                for make in inc:
                    make().wait_recv()
                for cp in outg:
                    cp.wait_send()
                for cp in loc:
                    cp.wait()

    hbm = pl.BlockSpec(memory_space=pl.ANY)
    call = pl.pallas_call(
        wrapped, name=name, grid=grid, in_specs=list(in_specs) + [hbm] * len(ex_in),
        out_specs=out_specs + [hbm] * len(ex_out), out_shape=out_shape + ex_out,
        scratch_shapes=list(scratch_shapes) + [pltpu.SemaphoreType.DMA((n_remote,)), pltpu.SemaphoreType.DMA((n_remote,)),
                                              pltpu.SemaphoreType.DMA((n_local,))],
        input_output_aliases=aliases, compiler_params=_params(len(grid)))

    def run(*operands):
        res = call(*operands, *ex_in)
        extra = res[n_out:]
        return list(res[:n_out]), [list(extra[p0:p0 + no]) for (_, _, p0, no) in spans]

    return run


def _inproj_fwd(x, g_mix, w_in, tm, phases=()):
    T = x.shape[0]
    widths = [IN_SEGS[i + 1] - IN_SEGS[i] for i in range(7)]

    def body(x_ref, g_ref, w_ref, h_ref, *z_refs):
        h, _ = _rms_fwd(x_ref[...], g_ref[...])
        hb = h.astype(BF16)
        h_ref[...] = hb
        for j in range(N_CHIPS):
            zj = _dot(hb, w_ref[j])
            for s, lo, hi, off in _seg_pieces(j * IN_BLK, (j + 1) * IN_BLK):
                z_refs[s][:, lo:hi] = zj[:, off:off + hi - lo]

    return _call(
        body, phases=phases, name="inproj_fwd", grid=(T // tm,),
        in_specs=[pl.BlockSpec((tm, D_MODEL), lambda i: (i, 0)), _const((1, D_MODEL)),
                  _const((N_CHIPS, D_MODEL, IN_BLK))],
        out_specs=[pl.BlockSpec((tm, D_MODEL), lambda i: (i, 0))]
        + [pl.BlockSpec((tm, w), lambda i: (i, 0)) for w in widths],
        out_shape=[jax.ShapeDtypeStruct((T, D_MODEL), BF16)]
        + [jax.ShapeDtypeStruct((T, w), F32) for w in widths],
    )(x, g_mix, w_in)


def _inproj_bwd(dz_parts, w_in, x, g_mix, dx1, tm, phases=()):
    T = x.shape[0]
    widths = [IN_SEGS[i + 1] - IN_SEGS[i] for i in range(7)]

    def body(*refs):
        p_refs = refs[:7]
        w_ref, x_ref, g_ref, dx1_ref, gx_ref, dg_ref, dz_ref = refs[7:]

        @pl.when(pl.program_id(0) == 0)
        def _():
            dg_ref[...] = jnp.zeros_like(dg_ref)

        for s in range(7):
            dz_ref[:, IN_SEGS[s]:IN_SEGS[s + 1]] = p_refs[s][...]
        dh = jnp.zeros((tm, D_MODEL), F32)
        for j in range(N_CHIPS):
            dh = dh + _dot_nt(dz_ref[:, j * IN_BLK:(j + 1) * IN_BLK], w_ref[j])
        xv = x_ref[...]
        g = g_ref[...]
        _, r = _rms_fwd(xv, g)
        dx, dg = _rms_bwd(dh, xv, r, g)
        gx_ref[...] = dx1_ref[...] + dx
        dg_ref[...] += dg

    row = lambda w: pl.BlockSpec((tm, w), lambda i: (i, 0))
    return _call(
        body, phases=phases, name="inproj_bwd", grid=(T // tm,),
        in_specs=[row(w) for w in widths]
        + [_const((N_CHIPS, D_MODEL, IN_BLK)), row(D_MODEL), _const((1, D_MODEL)), row(D_MODEL)],
        out_specs=[row(D_MODEL), _const((1, D_MODEL))],
        out_shape=[jax.ShapeDtypeStruct((T, D_MODEL), F32), jax.ShapeDtypeStruct((1, D_MODEL), F32)],
        scratch_shapes=[pltpu.VMEM((tm, IN_TOTAL), BF16)],
    )(*dz_parts, w_in, x, g_mix, dx1)


def _wgrad_in(h0, dz_parts, tm):
    T = h0.shape[0]
    widths = [IN_SEGS[i + 1] - IN_SEGS[i] for i in range(7)]

    def body(*refs):
        h_ref, p_refs, o_ref, acc_ref, sem = refs[0], refs[1:8], refs[8], refs[9], refs[10]
        t = pl.program_id(0)

        @pl.when(t == 0)
        def _():
            acc_ref[...] = jnp.zeros_like(acc_ref)

        hv = h_ref[...]
        for j in range(N_CHIPS):
            for s, lo, hi, off in _seg_pieces(j * IN_BLK, (j + 1) * IN_BLK):
                acc_ref[j, :, off:off + hi - lo] += _dot_tn(hv, p_refs[s][:, lo:hi])

        @pl.when(t == T // tm - 1)
        def _():
            cp = pltpu.make_async_copy(acc_ref, o_ref, sem)
            cp.start()
            cp.wait()

    row = lambda w: pl.BlockSpec((tm, w), lambda i: (i, 0))
    return pl.pallas_call(
        body, name="wgrad_in", grid=(T // tm,), in_specs=[row(D_MODEL)] + [row(w) for w in widths],
        out_specs=pl.BlockSpec(memory_space=pl.ANY),
        out_shape=jax.ShapeDtypeStruct((N_CHIPS, D_MODEL, IN_BLK), F32),
        scratch_shapes=[pltpu.VMEM((N_CHIPS, D_MODEL, IN_BLK), F32), pltpu.SemaphoreType.DMA],
        compiler_params=_params(1),
    )(h0, *dz_parts)


def _wgrad(a, g, name, blocked, cn, tm, phases=()):
    T, K = a.shape
    N = g.shape[1]
    nb = N // cn

    def body(a_ref, g_ref, o_ref):
        @pl.when(pl.program_id(1) == 0)
        def _():
            o_ref[...] = jnp.zeros_like(o_ref)

        o_ref[...] += _dot_tn(a_ref[...].astype(BF16), g_ref[...].astype(BF16))

    if blocked:
        out_spec = pl.BlockSpec((None, K, cn), lambda j, t: (j, 0, 0))
        out_shape = jax.ShapeDtypeStruct((nb, K, cn), F32)
    else:
        out_spec = pl.BlockSpec((K, cn), lambda j, t: (0, j))
        out_shape = jax.ShapeDtypeStruct((K, N), F32)
    outs, extra = _call(
        body, phases=phases, name=name, grid=(nb, T // tm),
        in_specs=[pl.BlockSpec((tm, K), lambda j, t: (t, 0)), pl.BlockSpec((tm, cn), lambda j, t: (t, j))],
        out_specs=out_spec, out_shape=out_shape,
    )(a, g)
    return outs[0], extra


def _shift_down(x, prev8, sft, row, row8, tm):
    xs = pltpu.roll(x, sft, 0)
    top = jnp.where(row8 < sft, pltpu.roll(prev8, sft, 0), xs[0:8])
    return jnp.concatenate([top, xs[8:]], axis=0)


def _shift_up(x, next8, sft, row8, tm):
    xs = pltpu.roll(x, tm - sft, 0)
    bot = jnp.where(row8 >= 8 - sft, pltpu.roll(next8, 8 - sft, 0), xs[tm - 8:tm])
    return jnp.concatenate([xs[0:tm - 8], bot], axis=0)


def _conv_fwd(x, prev8, cw_ref, cb, row, row8, tm):
    xc = cb + cw_ref[CONV_W - 1:CONV_W, :] * x
    for sft in range(1, CONV_W):
        j = CONV_W - 1 - sft
        xc = xc + cw_ref[j:j + 1, :] * _shift_down(x, prev8, sft, row, row8, tm)
    return xc


def _blockdiag_dot(xb, w_ref, transpose):
    outs = []
    for b in range(D_MODEL // LANES):
        xs = xb[:, b * LANES:(b + 1) * LANES]
        outs.append(_dot_nt(xs, w_ref[b]) if transpose else _dot(xs, w_ref[b]))
    return jnp.concatenate(outs, axis=1)


def _softplus_neg(lam):
    e = jnp.exp(-jnp.abs(lam))
    u = 1.0 + e
    log1p_e = jnp.where(u == 1.0, e, jnp.log(u) * (e / (u - 1.0)))
    sp = jnp.maximum(-lam, 0.0) + log1p_e
    return sp, -_sigmoid(-lam)


def _lru_gates(xc, wrg_ref, brg, wig_ref, big, sp):
    xcb = xc.astype(BF16)
    r = _sigmoid(_blockdiag_dot(xcb, wrg_ref, False) + brg)
    i = _sigmoid(_blockdiag_dot(xcb, wig_ref, False) + big)
    log_a = (-LRU_C) * r * sp
    a = jnp.exp(log_a)
    t = jnp.tanh(log_a)
    one_m_a2 = (-2.0) * t / (1.0 - t)
    mult = jnp.sqrt(one_m_a2)
    return xcb, r, i, a, mult


def _scan_down(a, b, row, tm):
    d = 1
    while d < tm:
        if d < 8:
            keep = row >= d
            a_s = jnp.where(keep, pltpu.roll(a, d, 0), 1.0)
            b_s = jnp.where(keep, pltpu.roll(b, d, 0), 0.0)
            b = a * b_s + b
            a = a * a_s
        else:
            b = jnp.concatenate([b[:d], a[d:] * b[:-d] + b[d:]], axis=0)
            a = jnp.concatenate([a[:d], a[d:] * a[:-d]], axis=0)
        d *= 2
    return a, b


def _scan_up(c, b, row, tm):
    d = 1
    while d < tm:
        if d < 8:
            keep = row < tm - d
            c_s = jnp.where(keep, pltpu.roll(c, tm - d, 0), 1.0)
            b_s = jnp.where(keep, pltpu.roll(b, tm - d, 0), 0.0)
            b = c * b_s + b
            c = c * c_s
        else:
            b = jnp.concatenate([c[:-d] * b[d:] + b[:-d], b[-d:]], axis=0)
            c = jnp.concatenate([c[:-d] * c[d:], c[-d:]], axis=0)
        d *= 2
    return c, b


def _rnn_fwd(xr, gr, conv_w, conv_b, wrg2, b_rg, wig2, b_ig, lam, n_seq, S, tm, phases=()):
    T = xr.shape[0]
    nt = S // tm
    W = D_MODEL

    def body(xr_ref, gr_ref, cw_ref, cb_ref, wrg_ref, brg_ref, wig_ref, big_ref, lam_ref,
             xc_ref, h_ref, ya_ref, px_ref, ph_ref):
        @pl.when(pl.program_id(1) == 0)
        def _():
            px_ref[...] = jnp.zeros_like(px_ref)
            ph_ref[...] = jnp.zeros_like(ph_ref)

        row = lax.broadcasted_iota(jnp.int32, (tm, W), 0)
        row8 = lax.broadcasted_iota(jnp.int32, (8, W), 0)
        x = xr_ref[...]
        xc = _conv_fwd(x, px_ref[...], cw_ref, cb_ref[...], row, row8, tm)
        sp, _ = _softplus_neg(lam_ref[...])
        _, r, i, a, mult = _lru_gates(xc, wrg_ref, brg_ref[...], wig_ref, big_ref[...], sp)
        bterm = mult * (i * xc)
        acum, hloc = _scan_down(a, bterm, row, tm)
        h = hloc + acum * ph_ref[7:8, :]
        h_ref[...] = h
        xc_ref[...] = xc
        gelu, _ = _gelu_and_grad(gr_ref[...])
        ya_ref[...] = (h * gelu).astype(BF16)
        px_ref[...] = xr_ref[tm - 8:tm, :]
        ph_ref[...] = h_ref[tm - 8:tm, :]

    tile = pl.BlockSpec((tm, W), lambda s, t: (s * nt + t, 0))
    return _call(
        body, phases=phases, name="rnn_fwd", grid=(n_seq, nt),
        in_specs=[tile, tile, _const((CONV_W, W)), _const((1, W)), _const((8, LANES, LANES)), _const((1, W)),
                  _const((8, LANES, LANES)), _const((1, W)), _const((1, W))],
        out_specs=[tile, tile, tile],
        out_shape=[jax.ShapeDtypeStruct((T, W), F32), jax.ShapeDtypeStruct((T, W), F32),
                   jax.ShapeDtypeStruct((T, W), BF16)],
        scratch_shapes=[pltpu.VMEM((8, W), F32), pltpu.VMEM((8, W), F32)],
    )(xr, gr, conv_w, conv_b, wrg2, b_rg, wig2, b_ig, lam)


def _rnn_bwd(dya, xr, gr, xc, h, conv_w, wrg2, b_rg, wig2, b_ig, lam, n_seq, S, tm, phases=()):
    T = xr.shape[0]
    nt = S // tm
    W = D_MODEL
    nb8 = tm // 8

    def body(dya_ref, xr_ref, gr_ref, xc_ref, h_ref, xprev_ref, hprev_ref, cw_ref, wrg_ref, brg_ref, wig_ref,
             big_ref, lam_ref, dxr_ref, dgr_ref, vec_ref, dwrg_ref, dwig_ref, cg_ref, ndxc_ref, tmp_ref):
        s, ti = pl.program_id(0), pl.program_id(1)

        @pl.when((s == 0) & (ti == 0))
        def _():
            vec_ref[...] = jnp.zeros_like(vec_ref)
            dwrg_ref[...] = jnp.zeros_like(dwrg_ref)
            dwig_ref[...] = jnp.zeros_like(dwig_ref)

        @pl.when(ti == 0)
        def _():
            cg_ref[...] = jnp.zeros_like(cg_ref)
            ndxc_ref[...] = jnp.zeros_like(ndxc_ref)

        first = ti == nt - 1
        row = lax.broadcasted_iota(jnp.int32, (tm, W), 0)
        row8 = lax.broadcasted_iota(jnp.int32, (8, W), 0)
        x = xr_ref[...]
        xc = xc_ref[...]
        hv = h_ref[...]
        xprev = jnp.where(first, 0.0, xprev_ref[...])
        hprev = jnp.where(first, 0.0, hprev_ref[...])
        sp, dsp_dlam = _softplus_neg(lam_ref[...])
        xcb, r, i, a, mult = _lru_gates(xc, wrg_ref, brg_ref[...], wig_ref, big_ref[...], sp)

        gelu, dgelu = _gelu_and_grad(gr_ref[...])
        dya_v = dya_ref[...]
        dgr_ref[...] = (dya_v * hv * dgelu).astype(BF16)
        dh = dya_v * gelu
        c = jnp.where(row < tm - 1, pltpu.roll(a, tm - 1, 0), 1.0)
        ccum, gloc = _scan_up(c, dh, row, tm)
        G = gloc + ccum * cg_ref[0:1, :]
        tmp_ref[...] = a * G
        cg_ref[...] = tmp_ref[0:8, :]

        h_m1 = _shift_down(hv, hprev, 1, row, row8, tm)
        ixc = i * xc
        dixc = G * mult
        dlog_a = (G * h_m1) * a - (G * ixc) * (a * a / mult)
        dr = dlog_a * ((-LRU_C) * sp)
        di = dixc * xc
        drg = dr * r * (1.0 - r)
        dig = di * i * (1.0 - i)
        vec_ref[7:8, :] += jnp.sum(dlog_a * ((-LRU_C) * r), axis=0, keepdims=True) * dsp_dlam
        vec_ref[5:6, :] += jnp.sum(drg, axis=0, keepdims=True)
        vec_ref[6:7, :] += jnp.sum(dig, axis=0, keepdims=True)
        drgb = drg.astype(BF16)
        digb = dig.astype(BF16)
        dxc = dixc * i + _blockdiag_dot(drgb, wrg_ref, True) + _blockdiag_dot(digb, wig_ref, True)
        for b in range(W // LANES):
            sl = slice(b * LANES, (b + 1) * LANES)
            dwrg_ref[b] += _dot_tn(xcb[:, sl], drgb[:, sl])
            dwig_ref[b] += _dot_tn(xcb[:, sl], digb[:, sl])

        vec_ref[4:5, :] += jnp.sum(dxc, axis=0, keepdims=True)
        vec_ref[3:4, :] += jnp.sum(dxc * x, axis=0, keepdims=True)
        dxr = cw_ref[CONV_W - 1:CONV_W, :] * dxc
        nxt = ndxc_ref[...]
        for sft in range(1, CONV_W):
            j = CONV_W - 1 - sft
            vec_ref[j:j + 1, :] += jnp.sum(dxc * _shift_down(x, xprev, sft, row, row8, tm), axis=0, keepdims=True)
            dxr = dxr + cw_ref[j:j + 1, :] * _shift_up(dxc, nxt, sft, row8, tm)
        dxr_ref[...] = dxr.astype(BF16)
        tmp_ref[...] = dxc
        ndxc_ref[...] = tmp_ref[0:8, :]

    rev = lambda s, t: (s * nt + nt - 1 - t, 0)
    tile = pl.BlockSpec((tm, W), rev)
    prev8 = pl.BlockSpec((8, W), lambda s, t: (jnp.maximum((s * nt + nt - 1 - t) * nb8 - 1, 0), 0))
    return _call(
        body, phases=phases, name="rnn_bwd", grid=(n_seq, nt),
        in_specs=[tile, tile, tile, tile, tile, prev8, prev8, _const((CONV_W, W)), _const((8, LANES, LANES)),
                  _const((1, W)), _const((8, LANES, LANES)), _const((1, W)), _const((1, W))],
        out_specs=[tile, tile, _const((16, W)), _const((8, LANES, LANES)), _const((8, LANES, LANES))],
        out_shape=[jax.ShapeDtypeStruct((T, W), BF16), jax.ShapeDtypeStruct((T, W), BF16),
                   jax.ShapeDtypeStruct((16, W), F32), jax.ShapeDtypeStruct((8, LANES, LANES), F32),
                   jax.ShapeDtypeStruct((8, LANES, LANES), F32)],
        scratch_shapes=[pltpu.VMEM((8, W), F32), pltpu.VMEM((8, W), F32), pltpu.VMEM((tm, W), F32)],
    )(dya, xr, gr, xc, h, xr, h, conv_w, wrg2, b_rg, wig2, b_ig, lam)


def _head_swap(t, lane):
    w = t.shape[1]
    return jnp.where(lane % HEAD_DIM < HEAD_DIM // 2, pltpu.roll(t, w - HEAD_DIM // 2, 1),
                     pltpu.roll(t, HEAD_DIM // 2, 1))


def _qk_prep(t, gain, cosf, sins, ind, indt, lane):
    ms = _split_dot(t * t, ind) * (1.0 / HEAD_DIM)
    rstd = _split_dot(lax.rsqrt(ms + NORM_EPS), indt)
    tn = (t * rstd) * gain
    return tn * cosf + _head_swap(tn, lane) * sins, rstd


def _qk_prep_bwd(dy, t, rstd, gain, cosf, sins, ind, indt, lane):
    dtn = dy * cosf + _head_swap(dy * sins, lane)
    dgain = jnp.sum(dtn * (t * rstd), axis=0, keepdims=True)
    dn = dtn * gain
    m = _split_dot(_split_dot(dn * t, ind), indt) * (1.0 / HEAD_DIM)
    return rstd * dn - t * (rstd * rstd * rstd * m), dgain


def _attn_mask(blk_idx):
    qi = lax.broadcasted_iota(jnp.int32, (WINDOW, 2 * WINDOW), 0)
    ci = lax.broadcasted_iota(jnp.int32, (WINDOW, 2 * WINDOW), 1)
    diff = WINDOW + qi - ci
    return (diff >= 0) & (diff < WINDOW) & ((ci >= WINDOW) | (blk_idx > 0))


def _stack_heads(t, kvh, lo):
    parts = []
    for i in (2 * kvh, 2 * kvh + 1):
        tp = t[:, i * LANES:(i + 1) * LANES]
        parts += [jnp.where(lo, tp, 0.0), jnp.where(lo, 0.0, tp)]
    return jnp.concatenate(parts, axis=0).astype(BF16)


def _unstack_heads(ts, lo):
    w = WINDOW
    return jnp.where(lo, ts[0:w], ts[w:2 * w]), jnp.where(lo, ts[2 * w:3 * w], ts[3 * w:4 * w])


def _dup_head(t, kvh, lo2):
    m = kvh // 2
    t2 = t[:, m * LANES:(m + 1) * LANES]
    t2r = pltpu.roll(t2, HEAD_DIM, 1)
    return (jnp.where(lo2, t2, t2r) if kvh % 2 == 0 else jnp.where(lo2, t2r, t2)).astype(BF16)


def _fold_head(ts, kvh, lo2):
    tot = ts + pltpu.roll(ts, HEAD_DIM, 1)
    own = lo2 if kvh % 2 == 0 else ~lo2
    return jnp.where(own, tot, 0.0)


def _softmax_sink(s, mask, sink):
    s = jnp.where(mask, s, -1e30)
    mx = jnp.maximum(jnp.max(s, axis=-1, keepdims=True), sink)
    e = jnp.exp(s - mx)
    es = jnp.exp(sink - mx)
    inv = 1.0 / (jnp.sum(e, axis=-1, keepdims=True) + es)
    return e * inv, es * inv


def _attn_fwd(q, k, v, qg, kg, sinks, cosf, sins, ind_q, ind_qt, ind_k, ind_kt, n_seq, S, phases=()):
    T = q.shape[0]
    nblk = S // WINDOW
    W = D_MODEL

    def body(sink_ref, q_ref, k_ref, v_ref, qg_ref, kg_ref, cos_ref, sin_ref, iq_ref, iqt_ref, ik_ref, ikt_ref,
             o_ref, kc_ref, vc_ref, s_ref, p_ref):
        n = pl.program_id(1)

        @pl.when(n == 0)
        def _():
            kc_ref[...] = jnp.zeros_like(kc_ref)
            vc_ref[...] = jnp.zeros_like(vc_ref)

        lane = lax.broadcasted_iota(jnp.int32, (WINDOW, W), 1)
        lo = lane[:, :LANES] < HEAD_DIM
        lo2 = lax.broadcasted_iota(jnp.int32, (2 * WINDOW, LANES), 1) < HEAD_DIM
        cosf, sinv = jnp.tile(cos_ref[...], (1, W // LANES)), jnp.tile(sin_ref[...], (1, W // LANES))
        qr, _ = _qk_prep(q_ref[...], qg_ref[...], cosf, sinv, iq_ref[...], iqt_ref[...], lane)
        kr, _ = _qk_prep(k_ref[...], kg_ref[...], cosf[:, :KV_W], sinv[:, :KV_W], ik_ref[...], ikt_ref[...],
                         lane[:, :KV_W])
        kc_ref[WINDOW:2 * WINDOW, :] = kr
        vc_ref[WINDOW:2 * WINDOW, :] = v_ref[...]
        kc, vc = kc_ref[...], vc_ref[...]
        mask = _attn_mask(n)
        qr = qr * HEAD_DIM ** -0.5
        for kvh in range(N_KV):
            b = kvh % 2
            s_ref[b] = _dot_nt(_stack_heads(qr, kvh, lo), _dup_head(kc, kvh, lo2))
            for r in range(4):
                rows = slice(r * WINDOW, (r + 1) * WINDOW)
                p, _ = _softmax_sink(s_ref[b, rows, :], mask, sink_ref[4 * kvh + r])
                p_ref[b, rows, :] = p.astype(BF16)
            o0, o1 = _unstack_heads(_dot(p_ref[b], _dup_head(vc, kvh, lo2)), lo)
            o_ref[:, (2 * kvh) * LANES:(2 * kvh + 1) * LANES] = o0.astype(BF16)
            o_ref[:, (2 * kvh + 1) * LANES:(2 * kvh + 2) * LANES] = o1.astype(BF16)
        kc_ref[0:WINDOW, :] = kr
        vc_ref[0:WINDOW, :] = v_ref[...]

    blk = lambda w: pl.BlockSpec((WINDOW, w), lambda s, n: (s * nblk + n, 0))
    pos = pl.BlockSpec((WINDOW, LANES), lambda s, n: (n, 0))
    outs, extra = _call(
        body, phases=phases, name="attn_fwd", grid=(n_seq, nblk),
        in_specs=[pl.BlockSpec(memory_space=pltpu.SMEM), blk(W), blk(KV_W), blk(KV_W), _const((1, W)),
                  _const((1, KV_W)), pos, pos, _const((W, LANES)), _const((LANES, W)), _const((KV_W, LANES)),
                  _const((LANES, KV_W))],
        out_specs=blk(W), out_shape=jax.ShapeDtypeStruct((T, W), BF16),
        scratch_shapes=[pltpu.VMEM((2 * WINDOW, KV_W), F32), pltpu.VMEM((2 * WINDOW, KV_W), F32),
                        pltpu.VMEM((2, 4 * WINDOW, 2 * WINDOW), F32), pltpu.VMEM((2, 4 * WINDOW, 2 * WINDOW), BF16)],
    )(sinks, q, k, v, qg, kg, cosf, sins, ind_q, ind_qt, ind_k, ind_kt)
    return outs[0], extra


def _attn_bwd(do, q, k, v, qg, kg, sinks, cosf, sins, ind_q, ind_qt, ind_k, ind_kt, n_seq, S, phases=()):
    T = q.shape[0]
    nblk = S // WINDOW
    W = D_MODEL

    def body(sink_ref, do_ref, q_ref, k_ref, v_ref, qg_ref, kg_ref, cos_ref, sin_ref, iq_ref, iqt_ref, ik_ref,
             ikt_ref, dq_ref, dkc_ref, dkp_ref, dvc_ref, dvp_ref, dqg_ref, dsk_ref, kc_ref, vc_ref, dqr_ref,
             dk_ref, dv_ref, s_ref, dp_ref, p_ref, ds_ref):
        s_id, n = pl.program_id(0), pl.program_id(1)

        @pl.when((s_id == 0) & (n == 0))
        def _():
            dqg_ref[...] = jnp.zeros_like(dqg_ref)
            dsk_ref[...] = jnp.zeros_like(dsk_ref)

        @pl.when(n == 0)
        def _():
            kc_ref[...] = jnp.zeros_like(kc_ref)
            vc_ref[...] = jnp.zeros_like(vc_ref)

        lane = lax.broadcasted_iota(jnp.int32, (WINDOW, W), 1)
        lane_k = lane[:, :KV_W]
        lane128 = lane[:, :LANES]
        cosf, sinv = jnp.tile(cos_ref[...], (1, W // LANES)), jnp.tile(sin_ref[...], (1, W // LANES))
        qv = q_ref[...]
        qr, q_rstd = _qk_prep(qv, qg_ref[...], cosf, sinv, iq_ref[...], iqt_ref[...], lane)
        kr, _ = _qk_prep(k_ref[...], kg_ref[...], cosf[:, :KV_W], sinv[:, :KV_W], ik_ref[...], ikt_ref[...], lane_k)
        kc_ref[WINDOW:2 * WINDOW, :] = kr
        vc_ref[WINDOW:2 * WINDOW, :] = v_ref[...]
        kc, vc = kc_ref[...], vc_ref[...]
        dov = do_ref[...]
        mask = _attn_mask(n)
        lo = lane128 < HEAD_DIM
        lo2 = lax.broadcasted_iota(jnp.int32, (2 * WINDOW, LANES), 1) < HEAD_DIM
        scale = HEAD_DIM ** -0.5
        qr = qr * scale
        dk_ref[...] = jnp.zeros_like(dk_ref)
        dv_ref[...] = jnp.zeros_like(dv_ref)
        dsk = jnp.zeros((WINDOW, LANES), F32)
        for kvh in range(N_KV):
            m, b = kvh // 2, kvh % 2
            qs = _stack_heads(qr, kvh, lo)
            dos = _stack_heads(dov, kvh, lo)
            kd = _dup_head(kc, kvh, lo2)
            vd = _dup_head(vc, kvh, lo2)
            s_ref[b] = _dot_nt(qs, kd)
            dp_ref[b] = _dot_nt(dos, vd)
            for r in range(4):
                rows = slice(r * WINDOW, (r + 1) * WINDOW)
                p, ps = _softmax_sink(s_ref[b, rows, :], mask, sink_ref[4 * kvh + r])
                dp = dp_ref[b, rows, :]
                dd = jnp.sum(p * dp, axis=-1, keepdims=True)
                p_ref[b, rows, :] = p.astype(BF16)
                ds_ref[b, rows, :] = (p * (dp - dd)).astype(BF16)
                dsk = dsk - jnp.where(lane128 == 4 * kvh + r, ps * dd, 0.0)
            dq0, dq1 = _unstack_heads(_dot(ds_ref[b], kd) * scale, lo)
            dqr_ref[:, (2 * kvh) * LANES:(2 * kvh + 1) * LANES] = dq0
            dqr_ref[:, (2 * kvh + 1) * LANES:(2 * kvh + 2) * LANES] = dq1
            dk_ref[:, m * LANES:(m + 1) * LANES] += _fold_head(_dot_tn(ds_ref[b], qs), kvh, lo2)
            dv_ref[:, m * LANES:(m + 1) * LANES] += _fold_head(_dot_tn(p_ref[b], dos), kvh, lo2)
        dsk_ref[...] += dsk
        dq, dqg = _qk_prep_bwd(dqr_ref[...], qv, q_rstd, qg_ref[...], cosf, sinv, iq_ref[...], iqt_ref[...], lane)
        dq_ref[...] = dq.astype(BF16)
        dqg_ref[...] += dqg
        dkp_ref[...] = dk_ref[0:WINDOW, :]
        dkc_ref[...] = dk_ref[WINDOW:2 * WINDOW, :]
        dvp_ref[...] = dv_ref[0:WINDOW, :]
        dvc_ref[...] = dv_ref[WINDOW:2 * WINDOW, :]
        kc_ref[0:WINDOW, :] = kr
        vc_ref[0:WINDOW, :] = v_ref[...]

    blk = lambda w: pl.BlockSpec((WINDOW, w), lambda s, n: (s * nblk + n, 0))
    pos = pl.BlockSpec((WINDOW, LANES), lambda s, n: (n, 0))
    kv_out = jax.ShapeDtypeStruct((T, KV_W), F32)
    stage = lambda dt: pltpu.VMEM((2, 4 * WINDOW, 2 * WINDOW), dt)
    return _call(
        body, phases=phases, name="attn_bwd", grid=(n_seq, nblk),
        in_specs=[pl.BlockSpec(memory_space=pltpu.SMEM), blk(W), blk(W), blk(KV_W), blk(KV_W), _const((1, W)),
                  _const((1, KV_W)), pos, pos, _const((W, LANES)), _const((LANES, W)), _const((KV_W, LANES)),
                  _const((LANES, KV_W))],
        out_specs=[blk(W), blk(KV_W), blk(KV_W), blk(KV_W), blk(KV_W), _const((1, W)), _const((WINDOW, LANES))],
        out_shape=[jax.ShapeDtypeStruct((T, W), BF16), kv_out, kv_out, kv_out, kv_out,
                   jax.ShapeDtypeStruct((1, W), F32), jax.ShapeDtypeStruct((WINDOW, LANES), F32)],
        scratch_shapes=[pltpu.VMEM((2 * WINDOW, KV_W), F32), pltpu.VMEM((2 * WINDOW, KV_W), F32),
                        pltpu.VMEM((WINDOW, W), F32), pltpu.VMEM((2 * WINDOW, KV_W), F32),
                        pltpu.VMEM((2 * WINDOW, KV_W), F32), stage(F32), stage(F32), stage(BF16), stage(BF16)],
    )(sinks, do, q, k, v, qg, kg, cosf, sins, ind_q, ind_qt, ind_k, ind_kt)


def _kv_bwd(dkc, dkp, dvc, dvp, k, kg, cosf, sins, ind_k, ind_kt, n_seq, S, phases=()):
    T = k.shape[0]
    nblk = S // WINDOW

    def body(dkc_ref, dkp_ref, dvc_ref, dvp_ref, k_ref, kg_ref, cos_ref, sin_ref, ik_ref, ikt_ref,
             dk_ref, dv_ref, dkg_ref):
        s_id, n = pl.program_id(0), pl.program_id(1)

        @pl.when((s_id == 0) & (n == 0))
        def _():
            dkg_ref[...] = jnp.zeros_like(dkg_ref)

        has_next = n < nblk - 1
        lane = lax.broadcasted_iota(jnp.int32, (WINDOW, KV_W), 1)
        dkr = dkc_ref[...] + jnp.where(has_next, dkp_ref[...], 0.0)
        dv_ref[...] = (dvc_ref[...] + jnp.where(has_next, dvp_ref[...], 0.0)).astype(BF16)
        cosf, sinv = jnp.tile(cos_ref[...], (1, KV_W // LANES)), jnp.tile(sin_ref[...], (1, KV_W // LANES))
        kv = k_ref[...]
        _, rstd = _qk_prep(kv, kg_ref[...], cosf, sinv, ik_ref[...], ikt_ref[...], lane)
        dk, dkg = _qk_prep_bwd(dkr, kv, rstd, kg_ref[...], cosf, sinv, ik_ref[...], ikt_ref[...], lane)
        dk_ref[...] = dk.astype(BF16)
        dkg_ref[...] += dkg

    cur = pl.BlockSpec((WINDOW, KV_W), lambda s, n: (s * nblk + n, 0))
    nxt = pl.BlockSpec((WINDOW, KV_W), lambda s, n: (s * nblk + jnp.minimum(n + 1, nblk - 1), 0))
    pos = pl.BlockSpec((WINDOW, LANES), lambda s, n: (n, 0))
    return _call(
        body, phases=phases, name="kv_bwd", grid=(n_seq, nblk),
        in_specs=[cur, nxt, cur, nxt, cur, _const((1, KV_W)), pos, pos, _const((KV_W, LANES)),
                  _const((LANES, KV_W))],
        out_specs=[cur, cur, _const((1, KV_W))],
        out_shape=[jax.ShapeDtypeStruct((T, KV_W), BF16), jax.ShapeDtypeStruct((T, KV_W), BF16),
                   jax.ShapeDtypeStruct((1, KV_W), F32)],
    )(dkc, dkp, dvc, dvp, k, kg, cosf, sins, ind_k, ind_kt)


def _merge_fwd(x, ya, o, ga, gb, w_rnn, w_attn, w_out, tm, phases=()):
    T = x.shape[0]
    W = D_MODEL

    def body(x_ref, ya_ref, o_ref, ga_ref, gb_ref, wr_ref, wa_ref, wo_ref, x1_ref, mg_ref, yao_ref, ybo_ref):
        y_a = _dot(ya_ref[...], wr_ref[...])
        y_b = _dot(o_ref[...], wa_ref[...])
        yao_ref[...] = y_a
        ybo_ref[...] = y_b
        mg = (_sigmoid(ga_ref[...]) * y_a + _sigmoid(gb_ref[...]) * y_b).astype(BF16)
        mg_ref[...] = mg
        x1_ref[...] = x_ref[...] + _dot(mg, wo_ref[...])

    row = pl.BlockSpec((tm, W), lambda i: (i, 0))
    sq = _const((W, W))
    return _call(
        body, phases=phases, name="merge_fwd", grid=(T // tm,),
        in_specs=[row, row, row, row, row, sq, sq, sq], out_specs=[row, row, row, row],
        out_shape=[jax.ShapeDtypeStruct((T, W), F32), jax.ShapeDtypeStruct((T, W), BF16),
                   jax.ShapeDtypeStruct((T, W), F32), jax.ShapeDtypeStruct((T, W), F32)],
    )(x, ya, o, ga, gb, w_rnn, w_attn, w_out)


def _merge_bwd(dx1, ga, gb, y_a, y_b, w_rnn, w_attn, w_out, tm, phases=()):
    T = dx1.shape[0]
    W = D_MODEL

    def body(dx1_ref, ga_ref, gb_ref, ya_ref, yb_ref, wr_ref, wa_ref, wo_ref,
             dga_ref, dgb_ref, dya_ref, dyb_ref, dyain_ref, do_ref):
        dm = _dot_nt(dx1_ref[...].astype(BF16), wo_ref[...])
        sa = _sigmoid(ga_ref[...])
        sb = _sigmoid(gb_ref[...])
        dga_ref[...] = (dm * ya_ref[...] * (sa * (1.0 - sa))).astype(BF16)
        dgb_ref[...] = (dm * yb_ref[...] * (sb * (1.0 - sb))).astype(BF16)
        dya = (dm * sa).astype(BF16)
        dyb = (dm * sb).astype(BF16)
        dya_ref[...] = dya
        dyb_ref[...] = dyb
        dyain_ref[...] = _dot_nt(dya, wr_ref[...])
        do_ref[...] = _dot_nt(dyb, wa_ref[...])

    row = pl.BlockSpec((tm, W), lambda i: (i, 0))
    sq = _const((W, W))
    b16 = jax.ShapeDtypeStruct((T, W), BF16)
    f32 = jax.ShapeDtypeStruct((T, W), F32)
    return _call(
        body, phases=phases, name="merge_bwd", grid=(T // tm,),
        in_specs=[row, row, row, row, row, sq, sq, sq], out_specs=[row] * 6,
        out_shape=[b16, b16, b16, b16, f32, f32],
    )(dx1, ga, gb, y_a, y_b, w_rnn, w_attn, w_out)


def _mlp_fwd(x1, g_mlp, w_up, w_down, tm, phases=()):
    T = x1.shape[0]
    W = D_MODEL

    def body(x_ref, g_ref, wu_ref, wd_ref, x2_ref, hm_ref, u_ref, act_ref):
        xv = x_ref[...]
        hm, _ = _rms_fwd(xv, g_ref[...])
        hmb = hm.astype(BF16)
        hm_ref[...] = hmb
        for j in range(N_CHIPS):
            u = _dot(hmb, wu_ref[j])
            u_ref[:, j * W:(j + 1) * W] = u
            ru = jnp.maximum(u, 0.0)
            act_ref[:, j * W:(j + 1) * W] = (ru * ru).astype(BF16)
        x2_ref[...] = xv + _dot(act_ref[...], wd_ref[...])

    row = lambda w: pl.BlockSpec((tm, w), lambda i: (i, 0))
    return _call(
        body, phases=phases, name="mlp_fwd", grid=(T // tm,),
        in_specs=[row(W), _const((1, W)), _const((N_CHIPS, W, W)), _const((D_FF, W))],
        out_specs=[row(W), row(W), row(D_FF), row(D_FF)],
        out_shape=[jax.ShapeDtypeStruct((T, W), F32), jax.ShapeDtypeStruct((T, W), BF16),
                   jax.ShapeDtypeStruct((T, D_FF), F32), jax.ShapeDtypeStruct((T, D_FF), BF16)],
    )(x1, g_mlp, w_up, w_down)


def _mlp_bwd(dx2, u, x1, g_mlp, w_up, w_down, tm, phases=()):
    T = x1.shape[0]
    W = D_MODEL

    def body(dx2_ref, u_ref, x_ref, g_ref, wu_ref, wd_ref, dx1_ref, du_ref, dg_ref):
        @pl.when(pl.program_id(0) == 0)
        def _():
            dg_ref[...] = jnp.zeros_like(dg_ref)

        dx2 = dx2_ref[...]
        dact = _dot_nt(dx2.astype(BF16), wd_ref[...])
        du_ref[...] = (dact * (2.0 * jnp.maximum(u_ref[...], 0.0))).astype(BF16)
        dhm = jnp.zeros((tm, W), F32)
        for j in range(N_CHIPS):
            dhm = dhm + _dot_nt(du_ref[:, j * W:(j + 1) * W], wu_ref[j])
        xv = x_ref[...]
        g = g_ref[...]
        _, r = _rms_fwd(xv, g)
        dx, dg = _rms_bwd(dhm, xv, r, g)
        dx1_ref[...] = dx2 + dx
        dg_ref[...] += dg

    row = lambda w: pl.BlockSpec((tm, w), lambda i: (i, 0))
    return _call(
        body, phases=phases, name="mlp_bwd", grid=(T // tm,),
        in_specs=[row(W), row(D_FF), row(W), _const((1, W)), _const((N_CHIPS, W, W)), _const((D_FF, W))],
        out_specs=[row(W), row(D_FF), _const((1, W))],
        out_shape=[jax.ShapeDtypeStruct((T, W), F32), jax.ShapeDtypeStruct((T, D_FF), BF16),
                   jax.ShapeDtypeStruct((1, W), F32)],
    )(dx2, u, x1, g_mlp, w_up, w_down)


def _ple_loss(x2, p, target, g_ple, w_gate, w_proj, tm, phases=()):
    T = x2.shape[0]
    W = D_MODEL
    cw = W // N_CHIPS

    def body(x_ref, p_ref, t_ref, g_ref, wg_ref, wp_ref, loss_ref, dx2_ref, pb_ref, de_ref, hp_ref, dtg_ref, dg_ref):
        @pl.when(pl.program_id(0) == 0)
        def _():
            dg_ref[...] = jnp.zeros_like(dg_ref)
            loss_ref[...] = jnp.zeros_like(loss_ref)

        xv = x_ref[...]
        g = g_ref[...]
        pb = p_ref[...].astype(BF16)
        pb_ref[...] = pb
        e = jnp.concatenate([_dot(pb, wp_ref[j]) for j in range(N_CHIPS)], axis=1)
        hp, r = _rms_fwd(xv, g)
        hpb = hp.astype(BF16)
        hp_ref[...] = hpb
        sg = _sigmoid(_dot(hpb, wg_ref[...]))
        diff = (xv + e * sg) - t_ref[...]
        loss_ref[...] += jnp.sum(diff * diff) * (0.5 / W)
        dx3 = diff * (1.0 / W)
        de_ref[...] = (dx3 * sg).astype(BF16)
        dtg = (dx3 * e * (sg * (1.0 - sg))).astype(BF16)
        dtg_ref[...] = dtg
        dx, dg = _rms_bwd(_dot_nt(dtg, wg_ref[...]), xv, r, g)
        dx2_ref[...] = dx3 + dx
        dg_ref[...] += dg

    row = lambda w: pl.BlockSpec((tm, w), lambda i: (i, 0))
    b16 = lambda w: jax.ShapeDtypeStruct((T, w), BF16)
    return _call(
        body, phases=phases, name="ple_loss", grid=(T // tm,),
        in_specs=[row(W), row(PLE_DIM), row(W), _const((1, W)), _const((W, W)), _const((N_CHIPS, PLE_DIM, cw))],
        out_specs=[_const((8, LANES)), row(W), row(PLE_DIM), row(W), row(W), row(W), _const((1, W))],
        out_shape=[jax.ShapeDtypeStruct((8, LANES), F32), jax.ShapeDtypeStruct((T, W), F32), b16(PLE_DIM),
                   b16(W), b16(W), b16(W), jax.ShapeDtypeStruct((1, W), F32)],
    )(x2, p, target, g_ple, w_gate, w_proj)


def _adamw(w, g, m, v, name, tr, phases=()):
    R, C = w.shape
    c1 = 1.0 / (1.0 - ADAM_B1 ** ADAM_STEP)
    c2 = 1.0 / (1.0 - ADAM_B2 ** ADAM_STEP)

    def body(w_ref, g_ref, m_ref, v_ref, go_ref, d_ref, nm_ref, nv_ref):
        gv = g_ref[...]
        go_ref[...] = gv
        nm = ADAM_B1 * m_ref[...] + (1.0 - ADAM_B1) * gv
        nv = ADAM_B2 * v_ref[...] + (1.0 - ADAM_B2) * (gv * gv)
        nm_ref[...] = nm
        nv_ref[...] = nv
        d_ref[...] = (-ADAM_LR) * ((nm * c1) / (jnp.sqrt(nv * c2) + ADAM_EPS) + ADAM_WD * w_ref[...])

    row = pl.BlockSpec((tr, C), lambda i: (i, 0))
    sds = jax.ShapeDtypeStruct((R, C), F32)
    return _call(
        body, phases=phases, name=name, grid=(R // tr,), in_specs=[row] * 4, out_specs=[row] * 4,
        out_shape=[sds] * 4,
    )(w, g, m, v)


def _indicator(width):
    ind = np.zeros((width, LANES), np.float32)
    ind[np.arange(width), np.arange(width) // HEAD_DIM] = 1.0
    return jnp.asarray(ind, BF16), jnp.asarray(ind.T, BF16)


def _rope_tables(S):
    inv = ROPE_THETA ** (-jnp.arange(0, HEAD_DIM, 2, dtype=F32) / HEAD_DIM)
    ang = jnp.arange(S, dtype=F32)[:, None] * inv[None, :]
    cos, sin = jnp.cos(ang), jnp.sin(ang)
    cosf = jnp.tile(jnp.concatenate([cos, cos], axis=1), (1, LANES // HEAD_DIM))
    sins = jnp.tile(jnp.concatenate([-sin, sin], axis=1), (1, LANES // HEAD_DIM))
    return cosf, sins


def _pair_blockdiag(w):
    w4 = w.reshape(8, 2, HEAD_DIM, HEAD_DIM)
    eye = jnp.eye(2, dtype=w.dtype)
    return jnp.einsum("bpij,pq->bpiqj", w4, eye).reshape(8, LANES, LANES)


def _pair_blockdiag_extract(g):
    g5 = g.reshape(8, 2, HEAD_DIM, 2, HEAD_DIM)
    return jnp.stack([g5[:, 0, :, 0, :], g5[:, 1, :, 1, :]], axis=1).reshape(16, HEAD_DIM, HEAD_DIM)


def _pair_sum(parts, sibs, name):
    n = len(parts)
    dims = [(p.shape[1] // 2, p.shape[2]) for p in parts]

    def body(*refs):
        p_r, s_r, send_r, own_r, mine_r, sem = (refs[0:n], refs[n:2 * n], refs[2 * n:3 * n], refs[3 * n:4 * n],
                                                refs[4 * n:5 * n], refs[5 * n])
        x, y, c, chips = _mesh_pos()
        me = 2 * x + y
        loads = []
        for i, (R, _) in enumerate(dims):
            mine, _ = _half_rows(c, R)
            cp = pltpu.make_async_copy(p_r[i].at[:, mine, :], mine_r[i], sem.at[i])
            cp.start()
            loads.append(cp)
        for i in range(n):
            loads[i].wait()
            for j, (cx, cy) in enumerate(chips):
                k = 2 * cx + cy
                send_r[i][j] = (mine_r[i][k] + s_r[i][k]).astype(BF16)
            own_r[i][...] = mine_r[i][me] + s_r[i][me]

    vm = pl.BlockSpec(memory_space=pltpu.VMEM)
    out = pl.pallas_call(
        body, name=name, in_specs=[pl.BlockSpec(memory_space=pl.ANY)] * n + [vm] * n, out_specs=[vm] * (2 * n),
        out_shape=[jax.ShapeDtypeStruct((3, R, C), BF16) for R, C in dims]
        + [jax.ShapeDtypeStruct((R, C), F32) for R, C in dims],
        scratch_shapes=[pltpu.VMEM((N_CHIPS, R, C), F32) for R, C in dims] + [pltpu.SemaphoreType.DMA((n,))],
        compiler_params=pltpu.CompilerParams(vmem_limit_bytes=VMEM_LIMIT),
    )(*parts, *sibs)
    return out[:n], out[n:]


def _chip_sum(owns, recvs, name):
    n = len(owns)
    dims = [o.shape for o in owns]

    def body(*refs):
        own_r, recv_r, red_r, stage_r, sem = refs[0:n], refs[n:2 * n], refs[2 * n:3 * n], refs[3 * n:4 * n], refs[4 * n]
        x, y, c, _ = _mesh_pos()
        me = 2 * x + y
        stores = []
        for i, (R, _) in enumerate(dims):
            acc = None
            for k in range(N_CHIPS):
                term = jnp.where(me == k, own_r[i][...], recv_r[i][_peer_slot(k, x, y)].astype(F32))
                acc = term if acc is None else acc + term
            stage_r[i][...] = acc
            mine, _ = _half_rows(c, R)
            cp = pltpu.make_async_copy(stage_r[i], red_r[i].at[mine, :], sem.at[i])
            cp.start()
            stores.append(cp)
        for cp in stores:
            cp.wait()

    vm = pl.BlockSpec(memory_space=pltpu.VMEM)
    return pl.pallas_call(
        body, name=name, in_specs=[vm] * (2 * n), out_specs=[pl.BlockSpec(memory_space=pl.ANY)] * n,
        out_shape=[jax.ShapeDtypeStruct((2 * R, C), F32) for R, C in dims],
        scratch_shapes=[pltpu.VMEM((R, C), F32) for R, C in dims] + [pltpu.SemaphoreType.DMA((n,))],
        compiler_params=pltpu.CompilerParams(vmem_limit_bytes=VMEM_LIMIT),
    )(*owns, *recvs)


def _gather_bf16(shard, name):
    R2, C = shard.shape
    R = R2 // 2

    def body(s_ref, o_ref, send_sems, recv_sems):
        x, y, c, chips = _mesh_pos()
        me = 2 * x + y
        mine = pl.ds(pl.multiple_of(c * R, R), R)
        theirs = pl.ds(pl.multiple_of((1 - c) * R, R), R)
        o_ref[me] = s_ref[...].astype(BF16)

        def copy(k, chip, rows, to):
            blk = o_ref.at[chip, rows]
            return pltpu.make_async_remote_copy(src_ref=blk, dst_ref=blk, send_sem=send_sems.at[k],
                                                recv_sem=recv_sems.at[k], device_id=to, device_id_type=MESH_ID)

        first = [copy(j, me, mine, (cx, cy, c)) for j, (cx, cy) in enumerate(chips)]
        for cp in first:
            cp.start()
        passed = []
        for j, (cx, cy) in enumerate(chips):
            copy(j, 2 * cx + cy, mine, (x, y, c)).wait_recv()
            cp = copy(3 + j, 2 * cx + cy, mine, (x, y, 1 - c))
            cp.start()
            passed.append(cp)
        for j, (cx, cy) in enumerate(chips):
            copy(3 + j, 2 * cx + cy, theirs, (x, y, c)).wait_recv()
        for cp in first + passed:
            cp.wait_send()

    return pl.pallas_call(
        body, name=name, out_shape=jax.ShapeDtypeStruct((N_CHIPS, R2, C), BF16),
        in_specs=[pl.BlockSpec(memory_space=pltpu.VMEM)], out_specs=pl.BlockSpec(memory_space=pltpu.VMEM),
        scratch_shapes=[pltpu.SemaphoreType.DMA((6,)), pltpu.SemaphoreType.DMA((6,))],
        compiler_params=pltpu.CompilerParams(vmem_limit_bytes=VMEM_LIMIT),
    )(shard)


def _pair_exchange_sum(partial, name):
    _, R2, C = partial.shape
    R = R2 // 2

    def body(p_ref, send_ref, own_ref, mine_ref, sib_ref, loc_sems, send_sems, recv_sems):
        x, y, c, chips = _mesh_pos()
        me = 2 * x + y
        mine, theirs = _half_rows(c, R)
        order = [2 * cx + cy for cx, cy in chips] + [me]
        locs, pairs = [], []
        for i, k in enumerate(order):
            loc = pltpu.make_async_copy(p_ref.at[k, mine, :], mine_ref.at[i], loc_sems.at[i])
            pair = _remote(p_ref.at[k, theirs, :], sib_ref.at[i], (send_sems.at[i], recv_sems.at[i]), (x, y, 1 - c))
            loc.start()
            pair.start()
            locs.append(loc)
            pairs.append(pair)
        for i in range(N_CHIPS):
            locs[i].wait()
            pairs[i].wait_recv()
            total = mine_ref[i] + sib_ref[i]
            if i < 3:
                send_ref[i] = total.astype(BF16)
            else:
                own_ref[...] = total
        for pair in pairs:
            pair.wait_send()

    vm = pl.BlockSpec(memory_space=pltpu.VMEM)
    return pl.pallas_call(
        body, name=name, in_specs=[pl.BlockSpec(memory_space=pl.ANY)], out_specs=[vm, vm],
        out_shape=[jax.ShapeDtypeStruct((3, R, C), BF16), jax.ShapeDtypeStruct((R, C), F32)],
        scratch_shapes=[pltpu.VMEM((N_CHIPS, R, C), F32), pltpu.VMEM((N_CHIPS, R, C), F32),
                        pltpu.SemaphoreType.DMA((N_CHIPS,)), pltpu.SemaphoreType.DMA((N_CHIPS,)),
                        pltpu.SemaphoreType.DMA((N_CHIPS,))],
        compiler_params=pltpu.CompilerParams(vmem_limit_bytes=VMEM_LIMIT),
    )(partial)


def _allreduce_small(buf, name):
    shape = buf.shape

    def body(b_ref, o_ref, sib_ref, pair_ref, in_ref, pair_sems, send_sems, recv_sems):
        x, y, c, chips = _mesh_pos()
        me = 2 * x + y
        pair = pltpu.make_async_remote_copy(src_ref=b_ref, dst_ref=sib_ref, send_sem=pair_sems.at[0],
                                            recv_sem=pair_sems.at[1], device_id=(x, y, 1 - c), device_id_type=MESH_ID)
        pair.start()
        pair.wait()
        pair_ref[...] = b_ref[...] + sib_ref[...]
        sends = []
        for j, (cx, cy) in enumerate(chips):
            cp = pltpu.make_async_remote_copy(src_ref=pair_ref, dst_ref=in_ref.at[j], send_sem=send_sems.at[j],
                                              recv_sem=recv_sems.at[j], device_id=(cx, cy, c), device_id_type=MESH_ID)
            cp.start()
            sends.append(cp)
        for cp in sends:
            cp.wait_recv()
        acc = None
        for k in range(N_CHIPS):
            term = jnp.where(me == k, pair_ref[...], in_ref[_peer_slot(k, x, y)])
            acc = term if acc is None else acc + term
        o_ref[...] = acc
        for cp in sends:
            cp.wait_send()

    return pl.pallas_call(
        body, name=name, out_shape=jax.ShapeDtypeStruct(shape, F32),
        in_specs=[pl.BlockSpec(memory_space=pltpu.VMEM)], out_specs=pl.BlockSpec(memory_space=pltpu.VMEM),
        scratch_shapes=[pltpu.VMEM(shape, F32), pltpu.VMEM(shape, F32), pltpu.VMEM((3,) + shape, F32),
                        pltpu.SemaphoreType.DMA((2,)), pltpu.SemaphoreType.DMA((3,)), pltpu.SemaphoreType.DMA((3,))],
        compiler_params=pltpu.CompilerParams(vmem_limit_bytes=VMEM_LIMIT),
    )(buf)


def _adamw_small(ws, gs, ms, vs):
    n = len(ws)
    c1 = 1.0 / (1.0 - ADAM_B1 ** ADAM_STEP)
    c2 = 1.0 / (1.0 - ADAM_B2 ** ADAM_STEP)

    def body(*refs):
        w_r, g_r, m_r, v_r = refs[0:n], refs[n:2 * n], refs[2 * n:3 * n], refs[3 * n:4 * n]
        d_r, nm_r, nv_r = refs[4 * n:5 * n], refs[5 * n:6 * n], refs[6 * n:7 * n]
        for i in range(n):
            gv = g_r[i][...]
            nm = ADAM_B1 * m_r[i][...] + (1.0 - ADAM_B1) * gv
            nv = ADAM_B2 * v_r[i][...] + (1.0 - ADAM_B2) * (gv * gv)
            nm_r[i][...] = nm
            nv_r[i][...] = nv
            d_r[i][...] = (-ADAM_LR) * ((nm * c1) / (jnp.sqrt(nv * c2) + ADAM_EPS) + ADAM_WD * w_r[i][...])

    vm = pl.BlockSpec(memory_space=pltpu.VMEM)
    sds = [jax.ShapeDtypeStruct(w.shape, F32) for w in ws]
    out = pl.pallas_call(body, name="adamw_small", in_specs=[vm] * (4 * n), out_specs=[vm] * (3 * n),
                         out_shape=sds * 3)(*ws, *gs, *ms, *vs)
    return out[0:n], out[n:2 * n], out[2 * n:3 * n]


_BIG = ("w_in", "w_rnn_proj", "w_attn_proj", "w_out", "w_up", "w_down", "w_ple_gate", "w_ple_proj")
_SMALL = ("g_mix", "conv_w", "conv_b", "w_rg", "b_rg", "w_ig", "b_ig", "lru_lambda", "q_gain", "k_gain", "sinks",
          "g_mlp", "g_ple")
_WEIGHTS = ("g_mix", "w_in", "conv_w", "conv_b", "w_rg", "b_rg", "w_ig", "b_ig", "lru_lambda", "w_rnn_proj",
            "q_gain", "k_gain", "sinks", "w_attn_proj", "w_out", "g_mlp", "w_up", "w_down", "g_ple", "w_ple_gate",
            "w_ple_proj")


def _pad_row(v):
    v = v.reshape(1, -1)
    return jnp.pad(v, ((0, 0), (0, D_MODEL - v.shape[1])))


def kernel(x, p, g_mix, w_in, conv_w, conv_b, w_rg, b_rg, w_ig, b_ig, lru_lambda, w_rnn_proj, q_gain, k_gain, sinks, w_attn_proj, w_out, g_mlp, w_up, w_down, g_ple, w_ple_gate, w_ple_proj, loss_target, m_g_mix, m_w_in, m_conv_w, m_conv_b, m_w_rg, m_b_rg, m_w_ig, m_b_ig, m_lru_lambda, m_w_rnn_proj, m_q_gain, m_k_gain, m_sinks, m_w_attn_proj, m_w_out, m_g_mlp, m_w_up, m_w_down, m_g_ple, m_w_ple_gate, m_w_ple_proj, v_g_mix, v_w_in, v_conv_w, v_conv_b, v_w_rg, v_b_rg, v_w_ig, v_b_ig, v_lru_lambda, v_w_rnn_proj, v_q_gain, v_k_gain, v_sinks, v_w_attn_proj, v_w_out, v_g_mlp, v_w_up, v_w_down, v_g_ple, v_w_ple_gate, v_w_ple_proj):
    w = dict(g_mix=g_mix, w_in=w_in, conv_w=conv_w, conv_b=conv_b, w_rg=w_rg, b_rg=b_rg, w_ig=w_ig, b_ig=b_ig,
             lru_lambda=lru_lambda, w_rnn_proj=w_rnn_proj, q_gain=q_gain, k_gain=k_gain, sinks=sinks,
             w_attn_proj=w_attn_proj, w_out=w_out, g_mlp=g_mlp, w_up=w_up, w_down=w_down, g_ple=g_ple,
             w_ple_gate=w_ple_gate, w_ple_proj=w_ple_proj)
    m = dict(g_mix=m_g_mix, w_in=m_w_in, conv_w=m_conv_w, conv_b=m_conv_b, w_rg=m_w_rg, b_rg=m_b_rg, w_ig=m_w_ig,
             b_ig=m_b_ig, lru_lambda=m_lru_lambda, w_rnn_proj=m_w_rnn_proj, q_gain=m_q_gain, k_gain=m_k_gain,
             sinks=m_sinks, w_attn_proj=m_w_attn_proj, w_out=m_w_out, g_mlp=m_g_mlp, w_up=m_w_up, w_down=m_w_down,
             g_ple=m_g_ple, w_ple_gate=m_w_ple_gate, w_ple_proj=m_w_ple_proj)
    v = dict(g_mix=v_g_mix, w_in=v_w_in, conv_w=v_conv_w, conv_b=v_conv_b, w_rg=v_w_rg, b_rg=v_b_rg, w_ig=v_w_ig,
             b_ig=v_b_ig, lru_lambda=v_lru_lambda, w_rnn_proj=v_w_rnn_proj, q_gain=v_q_gain, k_gain=v_k_gain,
             sinks=v_sinks, w_attn_proj=v_w_attn_proj, w_out=v_w_out, g_mlp=v_g_mlp, w_up=v_w_up, w_down=v_w_down,
             g_ple=v_g_ple, w_ple_gate=v_w_ple_gate, w_ple_proj=v_w_ple_proj)
    n_seq, S, _ = x.shape
    T = n_seq * S
    chip = 2 * lax.axis_index("x") + lax.axis_index("y")

    tm, tm_rnn = 512, 256
    xf, pf, tf = x.reshape(T, D_MODEL), p.reshape(T, PLE_DIM), loss_target.reshape(T, D_MODEL)
    first = lambda outs: [o[0] for o in outs]

    w_in_g = _gather_bf16(w["w_in"][0], "gather_w_in")
    wb = {name: w[name][0].astype(BF16) for name in _BIG if name != "w_in"}
    grp_mix, grp_mlp, grp_ple = ("w_rnn_proj", "w_attn_proj", "w_out"), ("w_up", "w_down"), ("w_ple_gate", "w_ple_proj")

    cw_full = jnp.zeros((8, D_MODEL), F32)
    cw_full = lax.dynamic_update_slice(cw_full, conv_w[0], (0, chip * (D_MODEL // N_CHIPS)))
    cw_full = _allreduce_small(0.5 * cw_full.reshape(64, LANES), "allgather_conv_w").reshape(8, D_MODEL)[0:CONV_W]

    cosf, sins = _rope_tables(S)
    ind_q, ind_qt = _indicator(D_MODEL)
    ind_k, ind_kt = _indicator(KV_W)
    wrg2 = _pair_blockdiag(w_rg[0]).astype(BF16)
    wig2 = _pair_blockdiag(w_ig[0]).astype(BF16)
    qg = jnp.tile(q_gain, (1, N_HEADS))
    kg = jnp.tile(k_gain, (1, N_KV))
    sk = sinks.reshape(N_HEADS)
    rnn_w = (cw_full, conv_b, wrg2, b_rg, wig2, b_ig, lru_lambda)
    attn_c = (qg, kg, sk, cosf, sins, ind_q, ind_qt, ind_k, ind_kt, n_seq, S)

    (h0, xr, gr, zq, zk, zv, ga, gb), ph = _inproj_fwd(xf, g_mix, w_in_g, tm,
                                                     phases=[_ph_gather_send(wb[n]) for n in grp_mix])
    g_mixw = first(ph)
    o, ph = _attn_fwd(zq, zk, zv, *attn_c,
                      phases=[_ph_gather_pass(g) for g in g_mixw] + [_ph_gather_send(wb["w_up"])])
    g_mixw, wu = first(ph[:3]), ph[3][0]
    (xc, h, ya), ph = _rnn_fwd(xr, gr, *rnn_w, n_seq, S, tm_rnn,
                               phases=[_ph_gather_pass(wu), _ph_gather_send(wb["w_down"])])
    wu, wd = ph[0][0], ph[1][0]
    wr, wa, wo = (g.reshape(D_MODEL, D_MODEL) for g in g_mixw)
    (x1, merged, y_a, y_b), ph = _merge_fwd(xf, ya, o, ga, gb, wr, wa, wo, tm,
                                            phases=[_ph_gather_pass(wd)] + [_ph_gather_send(wb[n]) for n in grp_ple])
    wd, g_plew = ph[0][0].reshape(D_FF, D_MODEL), first(ph[1:])
    (x2, hm, u, act), ph = _mlp_fwd(x1, g_mlp, wu, wd, tm // 2, phases=[_ph_gather_pass(g) for g in g_plew])
    wpg, wpp = first(ph)
    wpg = wpg.reshape(D_MODEL, D_MODEL)
    (loss_t, dx2, pb, de, hp, dtg, dg_ple), _ = _ple_loss(x2, pf, tf, g_ple, wpg, wpp, tm)

    chipmajor = lambda g: g.reshape(N_CHIPS, g.shape[-2] // N_CHIPS, g.shape[-1]) if g.ndim == 2 else g
    tmw = min(2 * tm, T)
    dw_pp = _wgrad(pb, de, "wgrad_ple_proj", False, D_MODEL, tmw)[0]
    part_ple = [chipmajor(_wgrad(hp, dtg, "wgrad_ple_gate", False, D_MODEL, tmw)[0]),
                dw_pp.reshape(PLE_DIM, N_CHIPS, D_MODEL // N_CHIPS).transpose(1, 0, 2)]
    (dx1, du, dg_mlp), ph = _mlp_bwd(dx2, u, x1, g_mlp, wu, wd, tm // 2, phases=[_ph_pair_send(g) for g in part_ple])
    send_ple, own_ple = _pair_sum(part_ple, first(ph), "pair_sum_ple")
    dw_down, ph = _wgrad(act, dx2, "wgrad_down", False, D_MODEL // 2, tmw, phases=[_ph_chip_send(s) for s in send_ple])
    red_ple = _chip_sum(own_ple, first(ph), "chip_sum_ple")
    part_mlp = [_wgrad(hm, du, "wgrad_up", True, D_MODEL, tmw)[0], chipmajor(dw_down)]
    (dga, dgb, dya, dyb, dyain, do), ph = _merge_bwd(
        dx1, ga, gb, y_a, y_b, wr, wa, wo, tm,
        phases=[_ph_half_swap(r) for r in red_ple] + [_ph_pair_send(g) for g in part_mlp])
    red_ple = first(ph[:2])
    send_mlp, own_mlp = _pair_sum(part_mlp, first(ph[2:]), "pair_sum_mlp")
    part_mix = [chipmajor(_wgrad(ya, dya, "wgrad_rnn_proj", False, D_MODEL, tmw)[0]),
                chipmajor(_wgrad(o, dyb, "wgrad_attn_proj", False, D_MODEL, tmw)[0]),
                chipmajor(_wgrad(merged, dx1, "wgrad_out", False, D_MODEL, tmw)[0])]
    (dxr, dgr, vec, dwrg2, dwig2), ph = _rnn_bwd(
        dyain, xr, gr, xc, h, cw_full, wrg2, b_rg, wig2, b_ig, lru_lambda, n_seq, S, tm_rnn,
        phases=[_ph_chip_send(s) for s in send_mlp] + [_ph_pair_send(g) for g in part_mix])
    red_mlp = _chip_sum(own_mlp, first(ph[:2]), "chip_sum_mlp")
    send_mix, own_mix = _pair_sum(part_mix, first(ph[2:]), "pair_sum_mix")
    (dq, dkc, dkp, dvc, dvp, dqg, dsk), ph = _attn_bwd(
        do, zq, zk, zv, *attn_c, phases=[_ph_half_swap(r) for r in red_mlp] + [_ph_chip_send(s) for s in send_mix])
    red_mlp = first(ph[:2])
    red_mix = _chip_sum(own_mix, first(ph[2:]), "chip_sum_mix")
    (dk, dv, dkg), _ = _kv_bwd(dkc, dkp, dvc, dvp, zk, kg, cosf, sins, ind_k, ind_kt, n_seq, S)
    dz_parts = [dxr, dgr, dq, dk, dv, dga, dgb]
    send_in, own_in = _pair_exchange_sum(_wgrad_in(h0, dz_parts, tm), "pair_sum_in")
    (grad_x, dg_mix), ph = _inproj_bwd(dz_parts, w_in_g, xf, g_mix, dx1, tm,
                                       phases=[_ph_half_swap(r) for r in red_mix] + [_ph_chip_send(send_in)])
    red_mix = first(ph[:3])
    red_in = _chip_sum([own_in], first(ph[3:]), "chip_sum_in")
    reduced = dict(zip(grp_ple + grp_mlp + grp_mix, red_ple + red_mlp + red_mix))
    grads = {
        "g_mix": dg_mix[0], "g_mlp": dg_mlp[0], "g_ple": dg_ple[0],
        "conv_w": vec[0:CONV_W], "conv_b": vec[4], "b_rg": vec[5], "b_ig": vec[6], "lru_lambda": vec[7],
        "w_rg": _pair_blockdiag_extract(dwrg2), "w_ig": _pair_blockdiag_extract(dwig2),
        "q_gain": dqg.reshape(N_HEADS, HEAD_DIM).sum(0), "k_gain": dkg.reshape(N_KV, HEAD_DIM).sum(0),
        "sinks": dsk.sum(0)[:N_HEADS],
    }

    rows = [grads["conv_w"], _pad_row(grads["conv_b"]), _pad_row(grads["b_rg"]), _pad_row(grads["b_ig"]),
            _pad_row(grads["lru_lambda"]), _pad_row(grads["g_mix"]), _pad_row(grads["g_mlp"]),
            _pad_row(grads["g_ple"]), _pad_row(grads["q_gain"]), _pad_row(grads["k_gain"]), _pad_row(grads["sinks"]),
            _pad_row(loss_t[0:1, 0:1]), jnp.zeros((1, D_MODEL), F32)]
    vecs = jnp.concatenate(rows, axis=0)
    packed = jnp.concatenate([vecs.reshape(-1, LANES), grads["w_rg"].reshape(-1, LANES),
                              grads["w_ig"].reshape(-1, LANES)], axis=0)
    red = _allreduce_small(packed, "allreduce_small")
    nv = vecs.size // LANES
    rvec = red[0:nv].reshape(16, D_MODEL)
    loss = rvec[14, 0]
    nw = grads["w_rg"].size // LANES
    sg = {
        "conv_w": lax.dynamic_slice(rvec[0:CONV_W], (0, chip * (D_MODEL // N_CHIPS)), (CONV_W, D_MODEL // N_CHIPS)),
        "conv_b": rvec[4], "b_rg": rvec[5], "b_ig": rvec[6], "lru_lambda": rvec[7], "g_mix": rvec[8],
        "g_mlp": rvec[9], "g_ple": rvec[10], "q_gain": rvec[11, :HEAD_DIM], "k_gain": rvec[12, :HEAD_DIM],
        "sinks": rvec[13, :N_HEADS], "w_rg": red[nv:nv + nw], "w_ig": red[nv + nw:nv + 2 * nw],
    }
    sg = {k: sg[k].reshape(w[k].shape) for k in _SMALL}
    d_s, m_s, v_s = _adamw_small([w[k] for k in _SMALL], [sg[k] for k in _SMALL], [m[k] for k in _SMALL],
                                 [v[k] for k in _SMALL])
    grad, delta, new_m, new_v = dict(sg), dict(zip(_SMALL, d_s)), dict(zip(_SMALL, m_s)), dict(zip(_SMALL, v_s))

    for name in ("w_ple_proj", "w_up", "w_down", "w_rnn_proj", "w_attn_proj", "w_out", "w_ple_gate", "w_in"):
        shape = w[name].shape
        outs, ph = _adamw(w[name][0], reduced[name], m[name][0], v[name][0], "adamw_" + name, 128,
                          phases=[_ph_half_swap(r) for r in red_in] if name == "w_ple_proj" else ())
        if name == "w_ple_proj":
            reduced["w_in"] = ph[0][0]
        grad[name], delta[name], new_m[name], new_v[name] = (a.reshape(shape) for a in outs)

    return (loss, grad_x.reshape(x.shape), *[grad[k] for k in _WEIGHTS], *[delta[k] for k in _WEIGHTS],
            *[new_m[k] for k in _WEIGHTS], *[new_v[k] for k in _WEIGHTS])
```

```python
import functools
import math

import numpy as np
import jax
import jax.numpy as jnp
from jax import lax
from jax.experimental import pallas as pl
from jax.experimental.pallas import tpu as pltpu

F32 = jnp.float32
BF16 = jnp.bfloat16

D_MODEL = 1024
N_HEADS = 16
N_KV = 4
HEAD_DIM = 64
KV_W = N_KV * HEAD_DIM
D_FF = 4096
PLE_DIM = 256
WINDOW = 128
CONV_W = 4
LRU_C = 8.0
NORM_EPS = 1e-6
ROPE_THETA = 10000.0
N_CHIPS = 4
IN_TOTAL = 5632
IN_BLK = IN_TOTAL // N_CHIPS
IN_SEGS = (0, 1024, 2048, 3072, 3328, 3584, 4608, 5632)

ADAM_LR = 0.001
ADAM_B1 = 0.9
ADAM_B2 = 0.999
ADAM_EPS = 1e-08
ADAM_WD = 0.01
ADAM_STEP = 10

LANES = 128
VMEM_LIMIT = 56 * 1024 * 1024
MESH_ID = pl.DeviceIdType.MESH


def _dot(a, b):
    return jnp.dot(a, b, preferred_element_type=F32)


def _dot_nt(a, b):
    return lax.dot_general(a, b, (((1,), (1,)), ((), ())), preferred_element_type=F32)


def _dot_tn(a, b):
    return lax.dot_general(a, b, (((0,), (0,)), ((), ())), preferred_element_type=F32)


def _split_dot(x, ind):
    hi = x.astype(BF16)
    lo = (x - hi.astype(F32)).astype(BF16)
    return _dot(hi, ind) + _dot(lo, ind)


def _sigmoid(x):
    return 1.0 / (1.0 + jnp.exp(-x))


_GELU_C = math.sqrt(2.0 / math.pi)


def _gelu_and_grad(g):
    inner = _GELU_C * (g + 0.044715 * g * g * g)
    t = jnp.tanh(inner)
    gelu = 0.5 * g * (1.0 + t)
    dgelu = 0.5 * (1.0 + t) + 0.5 * g * (1.0 - t * t) * _GELU_C * (1.0 + 3.0 * 0.044715 * g * g)
    return gelu, dgelu


def _const(shape):
    nd = len(shape)
    return pl.BlockSpec(shape, lambda *_: (0,) * nd)


def _params(n_grid, vmem=VMEM_LIMIT):
    return pltpu.CompilerParams(dimension_semantics=("arbitrary",) * n_grid, vmem_limit_bytes=vmem)


def _rms_fwd(x, g):
    r = lax.rsqrt(jnp.mean(x * x, axis=-1, keepdims=True) + NORM_EPS)
    return (x * r) * g, r


def _rms_bwd(dy, x, r, g):
    dn = dy * g
    dx = r * dn - x * (r * r * r * jnp.mean(dn * x, axis=-1, keepdims=True))
    dg = jnp.sum(dy * (x * r), axis=0, keepdims=True)
    return dx, dg


def _seg_pieces(blk_lo, blk_hi):
    out = []
    for s in range(7):
        lo, hi = max(blk_lo, IN_SEGS[s]), min(blk_hi, IN_SEGS[s + 1])
        if lo < hi:
            out.append((s, lo - IN_SEGS[s], hi - IN_SEGS[s], lo - blk_lo))
    return out


def _mesh_pos():
    x, y, c = lax.axis_index("x"), lax.axis_index("y"), lax.axis_index("c")
    other_chips = [(1 - x, y), (x, 1 - y), (1 - x, 1 - y)]
    return x, y, c, other_chips


def _peer_slot(k, x, y):
    dx = jnp.bitwise_xor(k // 2, x)
    dy = jnp.bitwise_xor(k % 2, y)
    return jnp.maximum(dx + 2 * dy - 1, 0)


def _half_rows(c, R):
    return pl.ds(pl.multiple_of(c * R, R), R), pl.ds(pl.multiple_of((1 - c) * R, R), R)


def _remote(src, dst, sems, to):
    return pltpu.make_async_remote_copy(src_ref=src, dst_ref=dst, send_sem=sems[0], recv_sem=sems[1],
                                        device_id=to, device_id_type=MESH_ID)


class _Phase:
    def __init__(self, ins, inout, outs, n_remote, n_local, build):
        self.ins, self.inout, self.outs = list(ins), list(inout), list(outs)
        self.n_remote, self.n_local, self.build = n_remote, n_local, build


def _ph_gather_send(wb):
    R2, C = wb.shape
    R = R2 // 2

    def build(ins, outs, rsem, lsem):
        (w_ref,), (g_ref,) = ins, outs
        x, y, c, chips = _mesh_pos()
        me = 2 * x + y
        mine, _ = _half_rows(c, R)
        loc = [pltpu.make_async_copy(w_ref, g_ref.at[me], lsem(0))]
        outg = [_remote(w_ref.at[mine], g_ref.at[me, mine], rsem(j), (cx, cy, c)) for j, (cx, cy) in enumerate(chips)]
        inc = [functools.partial(_remote, w_ref.at[mine], g_ref.at[2 * cx + cy, mine], rsem(j), (x, y, c))
               for j, (cx, cy) in enumerate(chips)]
        return loc, outg, inc

    return _Phase([wb], [], [jax.ShapeDtypeStruct((N_CHIPS, R2, C), BF16)], 3, 1, build)


def _ph_gather_pass(gath):
    _, R2, C = gath.shape
    R = R2 // 2

    def build(ins, outs, rsem, lsem):
        (g_ref,) = outs
        x, y, c, chips = _mesh_pos()
        mine, theirs = _half_rows(c, R)
        outg, inc = [], []
        for j, (cx, cy) in enumerate(chips):
            blk = g_ref.at[2 * cx + cy, mine]
            outg.append(_remote(blk, blk, rsem(j), (x, y, 1 - c)))
            got = g_ref.at[2 * cx + cy, theirs]
            inc.append(functools.partial(_remote, got, got, rsem(j), (x, y, c)))
        return [], outg, inc

    return _Phase([], [gath], [], 3, 0, build)


def _ph_pair_send(partial):
    _, R2, C = partial.shape
    R = R2 // 2

    def build(ins, outs, rsem, lsem):
        (p_ref,), (s_ref,) = ins, outs
        x, y, c, _ = _mesh_pos()
        _, theirs = _half_rows(c, R)
        src = p_ref.at[:, theirs, :]
        return ([], [_remote(src, s_ref, rsem(0), (x, y, 1 - c))],
                [functools.partial(_remote, src, s_ref, rsem(0), (x, y, c))])

    return _Phase([partial], [], [jax.ShapeDtypeStruct((N_CHIPS, R, C), F32)], 1, 0, build)


def _ph_chip_send(sendb):
    def build(ins, outs, rsem, lsem):
        (s_ref,), (r_ref,) = ins, outs
        x, y, c, chips = _mesh_pos()
        outg = [_remote(s_ref.at[j], r_ref.at[j], rsem(j), (cx, cy, c)) for j, (cx, cy) in enumerate(chips)]
        inc = [functools.partial(_remote, s_ref.at[j], r_ref.at[j], rsem(j), (x, y, c)) for j in range(3)]
        return [], outg, inc

    return _Phase([sendb], [], [jax.ShapeDtypeStruct(sendb.shape, sendb.dtype)], 3, 0, build)


def _ph_half_swap(red):
    R2, C = red.shape
    R = R2 // 2

    def build(ins, outs, rsem, lsem):
        (r_ref,) = outs
        x, y, c, _ = _mesh_pos()
        mine, theirs = _half_rows(c, R)
        return ([], [_remote(r_ref.at[mine], r_ref.at[mine], rsem(0), (x, y, 1 - c))],
                [functools.partial(_remote, r_ref.at[theirs], r_ref.at[theirs], rsem(0), (x, y, c))])

    return _Phase([], [red], [], 1, 0, build)


def _call(body, *, name, grid, in_specs, out_specs, out_shape, scratch_shapes=(), phases=()):
    single = not isinstance(out_specs, (list, tuple))
    out_specs = [out_specs] if single else list(out_specs)
    out_shape = [out_shape] if single else list(out_shape)
    n_in, n_out, n_scr = len(in_specs), len(out_specs), len(scratch_shapes)
    if not phases:
        call = pl.pallas_call(body, name=name, grid=grid, in_specs=in_specs, out_specs=out_specs,
                              out_shape=out_shape, scratch_shapes=list(scratch_shapes),
                              compiler_params=_params(len(grid)))
        return lambda *operands: (list(call(*operands)), [])

    ex_in, ex_out, aliases, spans = [], [], {}, []
    for ph in phases:
        i0, o0 = len(ex_in), len(ex_out)
        ex_in += ph.ins
        for a in ph.inout:
            aliases[n_in + len(ex_in)] = n_out + len(ex_out)
            ex_in.append(a)
            ex_out.append(jax.ShapeDtypeStruct(a.shape, a.dtype))
        ex_out += ph.outs
        spans.append((i0, len(ph.ins), o0, len(ex_out) - o0))
    n_remote = sum(ph.n_remote for ph in phases)
    n_local = max(sum(ph.n_local for ph in phases), 1)

    def wrapped(*refs):
        base_in, xin = refs[:n_in], refs[n_in:n_in + len(ex_in)]
        o0 = n_in + len(ex_in)
        base_out, xout = refs[o0:o0 + n_out], refs[o0 + n_out:o0 + n_out + len(ex_out)]
        scr = refs[o0 + n_out + len(ex_out):]
        send_sems, recv_sems, loc_sems = scr[n_scr:]
        first = functools.reduce(jnp.logical_and, [pl.program_id(i) == 0 for i in range(len(grid))])
        last = functools.reduce(jnp.logical_and, [pl.program_id(i) == grid[i] - 1 for i in range(len(grid))])

        def copies():
            out, r0, l0 = [], 0, 0
            for ph, (i0, ni, p0, no) in zip(phases, spans):
                rsem = lambda k, r0=r0: (send_sems.at[r0 + k], recv_sems.at[r0 + k])
                lsem = lambda k, l0=l0: loc_sems.at[l0 + k]
                out.append(ph.build(xin[i0:i0 + ni], xout[p0:p0 + no], rsem, lsem))
                r0, l0 = r0 + ph.n_remote, l0 + ph.n_local
            return out

        @pl.when(first)
        def _():
            for loc, outg, _ in copies():
                for cp in loc + outg:
                    cp.start()

        body(*base_in, *base_out, *scr[:n_scr])

        @pl.when(last)
        def _():
            for loc, outg, inc in copies():
                for make in inc:
                    make().wait_recv()
                for cp in outg:
                    cp.wait_send()
                for cp in loc:
                    cp.wait()

    hbm = pl.BlockSpec(memory_space=pl.ANY)
    call = pl.pallas_call(
        wrapped, name=name, grid=grid, in_specs=list(in_specs) + [hbm] * len(ex_in),
        out_specs=out_specs + [hbm] * len(ex_out), out_shape=out_shape + ex_out,
        scratch_shapes=list(scratch_shapes) + [pltpu.SemaphoreType.DMA((n_remote,)), pltpu.SemaphoreType.DMA((n_remote,)),
                                              pltpu.SemaphoreType.DMA((n_local,))],
        input_output_aliases=aliases, compiler_params=_params(len(grid)))

    def run(*operands):
        res = call(*operands, *ex_in)
        extra = res[n_out:]
        return list(res[:n_out]), [list(extra[p0:p0 + no]) for (_, _, p0, no) in spans]

    return run


def _inproj_fwd(x, g_mix, w_in, tm, phases=()):
    T = x.shape[0]
    widths = [IN_SEGS[i + 1] - IN_SEGS[i] for i in range(7)]

    def body(x_ref, g_ref, w_ref, h_ref, *z_refs):
        h, _ = _rms_fwd(x_ref[...], g_ref[...])
        hb = h.astype(BF16)
        h_ref[...] = hb
        for j in range(N_CHIPS):
            zj = _dot(hb, w_ref[j])
            for s, lo, hi, off in _seg_pieces(j * IN_BLK, (j + 1) * IN_BLK):
                z_refs[s][:, lo:hi] = zj[:, off:off + hi - lo]

    return _call(
        body, phases=phases, name="inproj_fwd", grid=(T // tm,),
        in_specs=[pl.BlockSpec((tm, D_MODEL), lambda i: (i, 0)), _const((1, D_MODEL)),
                  _const((N_CHIPS, D_MODEL, IN_BLK))],
        out_specs=[pl.BlockSpec((tm, D_MODEL), lambda i: (i, 0))]
        + [pl.BlockSpec((tm, w), lambda i: (i, 0)) for w in widths],
        out_shape=[jax.ShapeDtypeStruct((T, D_MODEL), BF16)]
        + [jax.ShapeDtypeStruct((T, w), F32) for w in widths],
    )(x, g_mix, w_in)


def _inproj_bwd(dz_parts, w_in, x, g_mix, dx1, tm, phases=()):
    T = x.shape[0]
    widths = [IN_SEGS[i + 1] - IN_SEGS[i] for i in range(7)]

    def body(*refs):
        p_refs = refs[:7]
        w_ref, x_ref, g_ref, dx1_ref, gx_ref, dg_ref, dz_ref = refs[7:]

        @pl.when(pl.program_id(0) == 0)
        def _():
            dg_ref[...] = jnp.zeros_like(dg_ref)

        for s in range(7):
            dz_ref[:, IN_SEGS[s]:IN_SEGS[s + 1]] = p_refs[s][...]
        dh = jnp.zeros((tm, D_MODEL), F32)
        for j in range(N_CHIPS):
            dh = dh + _dot_nt(dz_ref[:, j * IN_BLK:(j + 1) * IN_BLK], w_ref[j])
        xv = x_ref[...]
        g = g_ref[...]
        _, r = _rms_fwd(xv, g)
        dx, dg = _rms_bwd(dh, xv, r, g)
        gx_ref[...] = dx1_ref[...] + dx
        dg_ref[...] += dg

    row = lambda w: pl.BlockSpec((tm, w), lambda i: (i, 0))
    return _call(
        body, phases=phases, name="inproj_bwd", grid=(T // tm,),
        in_specs=[row(w) for w in widths]
        + [_const((N_CHIPS, D_MODEL, IN_BLK)), row(D_MODEL), _const((1, D_MODEL)), row(D_MODEL)],
        out_specs=[row(D_MODEL), _const((1, D_MODEL))],
        out_shape=[jax.ShapeDtypeStruct((T, D_MODEL), F32), jax.ShapeDtypeStruct((1, D_MODEL), F32)],
        scratch_shapes=[pltpu.VMEM((tm, IN_TOTAL), BF16)],
    )(*dz_parts, w_in, x, g_mix, dx1)


def _wgrad_in(h0, dz_parts, tm):
    T = h0.shape[0]
    widths = [IN_SEGS[i + 1] - IN_SEGS[i] for i in range(7)]

    def body(*refs):
        h_ref, p_refs, o_ref, acc_ref, sem = refs[0], refs[1:8], refs[8], refs[9], refs[10]
        t = pl.program_id(0)

        @pl.when(t == 0)
        def _():
            acc_ref[...] = jnp.zeros_like(acc_ref)

        hv = h_ref[...]
        for j in range(N_CHIPS):
            for s, lo, hi, off in _seg_pieces(j * IN_BLK, (j + 1) * IN_BLK):
                acc_ref[j, :, off:off + hi - lo] += _dot_tn(hv, p_refs[s][:, lo:hi])

        @pl.when(t == T // tm - 1)
        def _():
            cp = pltpu.make_async_copy(acc_ref, o_ref, sem)
            cp.start()
            cp.wait()

    row = lambda w: pl.BlockSpec((tm, w), lambda i: (i, 0))
    return pl.pallas_call(
        body, name="wgrad_in", grid=(T // tm,), in_specs=[row(D_MODEL)] + [row(w) for w in widths],
        out_specs=pl.BlockSpec(memory_space=pl.ANY),
        out_shape=jax.ShapeDtypeStruct((N_CHIPS, D_MODEL, IN_BLK), F32),
        scratch_shapes=[pltpu.VMEM((N_CHIPS, D_MODEL, IN_BLK), F32), pltpu.SemaphoreType.DMA],
        compiler_params=_params(1),
    )(h0, *dz_parts)


def _wgrad(a, g, name, blocked, cn, tm, phases=()):
    T, K = a.shape
    N = g.shape[1]
    nb = N // cn

    def body(a_ref, g_ref, o_ref):
        @pl.when(pl.program_id(1) == 0)
        def _():
            o_ref[...] = jnp.zeros_like(o_ref)

        o_ref[...] += _dot_tn(a_ref[...].astype(BF16), g_ref[...].astype(BF16))

    if blocked:
        out_spec = pl.BlockSpec((None, K, cn), lambda j, t: (j, 0, 0))
        out_shape = jax.ShapeDtypeStruct((nb, K, cn), F32)
    else:
        out_spec = pl.BlockSpec((K, cn), lambda j, t: (0, j))
        out_shape = jax.ShapeDtypeStruct((K, N), F32)
    outs, extra = _call(
        body, phases=phases, name=name, grid=(nb, T // tm),
        in_specs=[pl.BlockSpec((tm, K), lambda j, t: (t, 0)), pl.BlockSpec((tm, cn), lambda j, t: (t, j))],
        out_specs=out_spec, out_shape=out_shape,
    )(a, g)
    return outs[0], extra


def _shift_down(x, prev8, sft, row, row8, tm):
    xs = pltpu.roll(x, sft, 0)
    top = jnp.where(row8 < sft, pltpu.roll(prev8, sft, 0), xs[0:8])
    return jnp.concatenate([top, xs[8:]], axis=0)


def _shift_up(x, next8, sft, row8, tm):
    xs = pltpu.roll(x, tm - sft, 0)
    bot = jnp.where(row8 >= 8 - sft, pltpu.roll(next8, 8 - sft, 0), xs[tm - 8:tm])
    return jnp.concatenate([xs[0:tm - 8], bot], axis=0)


def _conv_fwd(x, prev8, cw_ref, cb, row, row8, tm):
    xc = cb + cw_ref[CONV_W - 1:CONV_W, :] * x
    for sft in range(1, CONV_W):
        j = CONV_W - 1 - sft
        xc = xc + cw_ref[j:j + 1, :] * _shift_down(x, prev8, sft, row, row8, tm)
    return xc


def _blockdiag_dot(xb, w_ref, transpose):
    outs = []
    for b in range(D_MODEL // LANES):
        xs = xb[:, b * LANES:(b + 1) * LANES]
        outs.append(_dot_nt(xs, w_ref[b]) if transpose else _dot(xs, w_ref[b]))
    return jnp.concatenate(outs, axis=1)


def _softplus_neg(lam):
    e = jnp.exp(-jnp.abs(lam))
    u = 1.0 + e
    log1p_e = jnp.where(u == 1.0, e, jnp.log(u) * (e / (u - 1.0)))
    sp = jnp.maximum(-lam, 0.0) + log1p_e
    return sp, -_sigmoid(-lam)


def _lru_gates(xc, wrg_ref, brg, wig_ref, big, sp):
    xcb = xc.astype(BF16)
    r = _sigmoid(_blockdiag_dot(xcb, wrg_ref, False) + brg)
    i = _sigmoid(_blockdiag_dot(xcb, wig_ref, False) + big)
    log_a = (-LRU_C) * r * sp
    a = jnp.exp(log_a)
    t = jnp.tanh(log_a)
    one_m_a2 = (-2.0) * t / (1.0 - t)
    mult = jnp.sqrt(one_m_a2)
    return xcb, r, i, a, mult


def _scan_down(a, b, row, tm):
    d = 1
    while d < tm:
        if d < 8:
            keep = row >= d
            a_s = jnp.where(keep, pltpu.roll(a, d, 0), 1.0)
            b_s = jnp.where(keep, pltpu.roll(b, d, 0), 0.0)
            b = a * b_s + b
            a = a * a_s
        else:
            b = jnp.concatenate([b[:d], a[d:] * b[:-d] + b[d:]], axis=0)
            a = jnp.concatenate([a[:d], a[d:] * a[:-d]], axis=0)
        d *= 2
    return a, b


def _scan_up(c, b, row, tm):
    d = 1
    while d < tm:
        if d < 8:
            keep = row < tm - d
            c_s = jnp.where(keep, pltpu.roll(c, tm - d, 0), 1.0)
            b_s = jnp.where(keep, pltpu.roll(b, tm - d, 0), 0.0)
            b = c * b_s + b
            c = c * c_s
        else:
            b = jnp.concatenate([c[:-d] * b[d:] + b[:-d], b[-d:]], axis=0)
            c = jnp.concatenate([c[:-d] * c[d:], c[-d:]], axis=0)
        d *= 2
    return c, b


def _rnn_fwd(xr, gr, conv_w, conv_b, wrg2, b_rg, wig2, b_ig, lam, n_seq, S, tm, phases=()):
    T = xr.shape[0]
    nt = S // tm
    W = D_MODEL

    def body(xr_ref, gr_ref, cw_ref, cb_ref, wrg_ref, brg_ref, wig_ref, big_ref, lam_ref,
             xc_ref, h_ref, ya_ref, px_ref, ph_ref):
        @pl.when(pl.program_id(1) == 0)
        def _():
            px_ref[...] = jnp.zeros_like(px_ref)
            ph_ref[...] = jnp.zeros_like(ph_ref)

        row = lax.broadcasted_iota(jnp.int32, (tm, W), 0)
        row8 = lax.broadcasted_iota(jnp.int32, (8, W), 0)
        x = xr_ref[...]
        xc = _conv_fwd(x, px_ref[...], cw_ref, cb_ref[...], row, row8, tm)
        sp, _ = _softplus_neg(lam_ref[...])
        _, r, i, a, mult = _lru_gates(xc, wrg_ref, brg_ref[...], wig_ref, big_ref[...], sp)
        bterm = mult * (i * xc)
        acum, hloc = _scan_down(a, bterm, row, tm)
        h = hloc + acum * ph_ref[7:8, :]
        h_ref[...] = h
        xc_ref[...] = xc
        gelu, _ = _gelu_and_grad(gr_ref[...])
        ya_ref[...] = (h * gelu).astype(BF16)
        px_ref[...] = xr_ref[tm - 8:tm, :]
        ph_ref[...] = h_ref[tm - 8:tm, :]

    tile = pl.BlockSpec((tm, W), lambda s, t: (s * nt + t, 0))
    return _call(
        body, phases=phases, name="rnn_fwd", grid=(n_seq, nt),
        in_specs=[tile, tile, _const((CONV_W, W)), _const((1, W)), _const((8, LANES, LANES)), _const((1, W)),
                  _const((8, LANES, LANES)), _const((1, W)), _const((1, W))],
        out_specs=[tile, tile, tile],
        out_shape=[jax.ShapeDtypeStruct((T, W), F32), jax.ShapeDtypeStruct((T, W), F32),
                   jax.ShapeDtypeStruct((T, W), BF16)],
        scratch_shapes=[pltpu.VMEM((8, W), F32), pltpu.VMEM((8, W), F32)],
    )(xr, gr, conv_w, conv_b, wrg2, b_rg, wig2, b_ig, lam)


def _rnn_bwd(dya, xr, gr, xc, h, conv_w, wrg2, b_rg, wig2, b_ig, lam, n_seq, S, tm, phases=()):
    T = xr.shape[0]
    nt = S // tm
    W = D_MODEL
    nb8 = tm // 8

    def body(dya_ref, xr_ref, gr_ref, xc_ref, h_ref, xprev_ref, hprev_ref, cw_ref, wrg_ref, brg_ref, wig_ref,
             big_ref, lam_ref, dxr_ref, dgr_ref, vec_ref, dwrg_ref, dwig_ref, cg_ref, ndxc_ref, tmp_ref):
        s, ti = pl.program_id(0), pl.program_id(1)

        @pl.when((s == 0) & (ti == 0))
        def _():
            vec_ref[...] = jnp.zeros_like(vec_ref)
            dwrg_ref[...] = jnp.zeros_like(dwrg_ref)
            dwig_ref[...] = jnp.zeros_like(dwig_ref)

        @pl.when(ti == 0)
        def _():
            cg_ref[...] = jnp.zeros_like(cg_ref)
            ndxc_ref[...] = jnp.zeros_like(ndxc_ref)

        first = ti == nt - 1
        row = lax.broadcasted_iota(jnp.int32, (tm, W), 0)
        row8 = lax.broadcasted_iota(jnp.int32, (8, W), 0)
        x = xr_ref[...]
        xc = xc_ref[...]
        hv = h_ref[...]
        xprev = jnp.where(first, 0.0, xprev_ref[...])
        hprev = jnp.where(first, 0.0, hprev_ref[...])
        sp, dsp_dlam = _softplus_neg(lam_ref[...])
        xcb, r, i, a, mult = _lru_gates(xc, wrg_ref, brg_ref[...], wig_ref, big_ref[...], sp)

        gelu, dgelu = _gelu_and_grad(gr_ref[...])
        dya_v = dya_ref[...]
        dgr_ref[...] = (dya_v * hv * dgelu).astype(BF16)
        dh = dya_v * gelu
        c = jnp.where(row < tm - 1, pltpu.roll(a, tm - 1, 0), 1.0)
        ccum, gloc = _scan_up(c, dh, row, tm)
        G = gloc + ccum * cg_ref[0:1, :]
        tmp_ref[...] = a * G
        cg_ref[...] = tmp_ref[0:8, :]

        h_m1 = _shift_down(hv, hprev, 1, row, row8, tm)
        ixc = i * xc
        dixc = G * mult
        dlog_a = (G * h_m1) * a - (G * ixc) * (a * a / mult)
        dr = dlog_a * ((-LRU_C) * sp)
        di = dixc * xc
        drg = dr * r * (1.0 - r)
        dig = di * i * (1.0 - i)
        vec_ref[7:8, :] += jnp.sum(dlog_a * ((-LRU_C) * r), axis=0, keepdims=True) * dsp_dlam
        vec_ref[5:6, :] += jnp.sum(drg, axis=0, keepdims=True)
        vec_ref[6:7, :] += jnp.sum(dig, axis=0, keepdims=True)
        drgb = drg.astype(BF16)
        digb = dig.astype(BF16)
        dxc = dixc * i + _blockdiag_dot(drgb, wrg_ref, True) + _blockdiag_dot(digb, wig_ref, True)
        for b in range(W // LANES):
            sl = slice(b * LANES, (b + 1) * LANES)
            dwrg_ref[b] += _dot_tn(xcb[:, sl], drgb[:, sl])
            dwig_ref[b] += _dot_tn(xcb[:, sl], digb[:, sl])

        vec_ref[4:5, :] += jnp.sum(dxc, axis=0, keepdims=True)
        vec_ref[3:4, :] += jnp.sum(dxc * x, axis=0, keepdims=True)
        dxr = cw_ref[CONV_W - 1:CONV_W, :] * dxc
        nxt = ndxc_ref[...]
        for sft in range(1, CONV_W):
            j = CONV_W - 1 - sft
            vec_ref[j:j + 1, :] += jnp.sum(dxc * _shift_down(x, xprev, sft, row, row8, tm), axis=0, keepdims=True)
            dxr = dxr + cw_ref[j:j + 1, :] * _shift_up(dxc, nxt, sft, row8, tm)
        dxr_ref[...] = dxr.astype(BF16)
        tmp_ref[...] = dxc
        ndxc_ref[...] = tmp_ref[0:8, :]

    rev = lambda s, t: (s * nt + nt - 1 - t, 0)
    tile = pl.BlockSpec((tm, W), rev)
    prev8 = pl.BlockSpec((8, W), lambda s, t: (jnp.maximum((s * nt + nt - 1 - t) * nb8 - 1, 0), 0))
    return _call(
        body, phases=phases, name="rnn_bwd", grid=(n_seq, nt),
        in_specs=[tile, tile, tile, tile, tile, prev8, prev8, _const((CONV_W, W)), _const((8, LANES, LANES)),
                  _const((1, W)), _const((8, LANES, LANES)), _const((1, W)), _const((1, W))],
        out_specs=[tile, tile, _const((16, W)), _const((8, LANES, LANES)), _const((8, LANES, LANES))],
        out_shape=[jax.ShapeDtypeStruct((T, W), BF16), jax.ShapeDtypeStruct((T, W), BF16),
                   jax.ShapeDtypeStruct((16, W), F32), jax.ShapeDtypeStruct((8, LANES, LANES), F32),
                   jax.ShapeDtypeStruct((8, LANES, LANES), F32)],
        scratch_shapes=[pltpu.VMEM((8, W), F32), pltpu.VMEM((8, W), F32), pltpu.VMEM((tm, W), F32)],
    )(dya, xr, gr, xc, h, xr, h, conv_w, wrg2, b_rg, wig2, b_ig, lam)


def _head_swap(t, lane):
    w = t.shape[1]
    return jnp.where(lane % HEAD_DIM < HEAD_DIM // 2, pltpu.roll(t, w - HEAD_DIM // 2, 1),
                     pltpu.roll(t, HEAD_DIM // 2, 1))


def _qk_prep(t, gain, cosf, sins, ind, indt, lane):
    ms = _split_dot(t * t, ind) * (1.0 / HEAD_DIM)
    rstd = _split_dot(lax.rsqrt(ms + NORM_EPS), indt)
    tn = (t * rstd) * gain
    return tn * cosf + _head_swap(tn, lane) * sins, rstd


def _qk_prep_bwd(dy, t, rstd, gain, cosf, sins, ind, indt, lane):
    dtn = dy * cosf + _head_swap(dy * sins, lane)
    dgain = jnp.sum(dtn * (t * rstd), axis=0, keepdims=True)
    dn = dtn * gain
    m = _split_dot(_split_dot(dn * t, ind), indt) * (1.0 / HEAD_DIM)
    return rstd * dn - t * (rstd * rstd * rstd * m), dgain


def _attn_mask(blk_idx):
    qi = lax.broadcasted_iota(jnp.int32, (WINDOW, 2 * WINDOW), 0)
    ci = lax.broadcasted_iota(jnp.int32, (WINDOW, 2 * WINDOW), 1)
    diff = WINDOW + qi - ci
    return (diff >= 0) & (diff < WINDOW) & ((ci >= WINDOW) | (blk_idx > 0))


def _attn_mask_t(blk_idx):
    ci = lax.broadcasted_iota(jnp.int32, (2 * WINDOW, WINDOW), 0)
    qi = lax.broadcasted_iota(jnp.int32, (2 * WINDOW, WINDOW), 1)
    diff = WINDOW + qi - ci
    return (diff >= 0) & (diff < WINDOW) & ((ci >= WINDOW) | (blk_idx > 0))


def _stack_heads(t, kvh, lo):
    parts = []
    for i in (2 * kvh, 2 * kvh + 1):
        tp = t[:, i * LANES:(i + 1) * LANES]
        parts += [jnp.where(lo, tp, 0.0), jnp.where(lo, 0.0, tp)]
    return jnp.concatenate(parts, axis=0).astype(BF16)


def _unstack_heads(ts, lo):
    w = WINDOW
    return jnp.where(lo, ts[0:w], ts[w:2 * w]), jnp.where(lo, ts[2 * w:3 * w], ts[3 * w:4 * w])


def _dup_head(t, kvh, lo2):
    m = kvh // 2
    t2 = t[:, m * LANES:(m + 1) * LANES]
    t2r = pltpu.roll(t2, HEAD_DIM, 1)
    return (jnp.where(lo2, t2, t2r) if kvh % 2 == 0 else jnp.where(lo2, t2r, t2)).astype(BF16)


def _fold_head(ts, kvh, lo2):
    tot = ts + pltpu.roll(ts, HEAD_DIM, 1)
    own = lo2 if kvh % 2 == 0 else ~lo2
    return jnp.where(own, tot, 0.0)


KEY_CHUNKS = tuple(slice(i * 64, (i + 1) * 64) for i in range(2 * WINDOW // 64))


def _fold8(x, op):
    return op(x.reshape(x.shape[0] // 8, 8, x.shape[1]), axis=0)


def _softmax_stats(s_ref, b, cols, sink):
    m8 = None
    for c in KEY_CHUNKS:
        t = _fold8(s_ref[b, c, cols], jnp.max)
        m8 = t if m8 is None else jnp.maximum(m8, t)
    mx = jnp.maximum(jnp.max(m8, axis=0, keepdims=True), sink)
    d8 = None
    for c in KEY_CHUNKS:
        t = _fold8(jnp.exp(s_ref[b, c, cols] - mx), jnp.sum)
        d8 = t if d8 is None else d8 + t
    es = jnp.exp(sink - mx)
    inv = 1.0 / (jnp.sum(d8, axis=0, keepdims=True) + es)
    return mx, inv, es * inv


def _attn_fwd(q, k, v, qg, kg, sinks, cosf, sins, ind_q, ind_qt, ind_k, ind_kt, n_seq, S, phases=()):
    T = q.shape[0]
    nblk = S // WINDOW
    W = D_MODEL

    def body(sink_ref, q_ref, k_ref, v_ref, qg_ref, kg_ref, cos_ref, sin_ref, iq_ref, iqt_ref, ik_ref, ikt_ref,
             o_ref, kc_ref, vc_ref, s_ref, p_ref, qs_ref, kd_ref, vd_ref):
        n = pl.program_id(1)

        @pl.when(n == 0)
        def _():
            kc_ref[...] = jnp.zeros_like(kc_ref)
            vc_ref[...] = jnp.zeros_like(vc_ref)

        lane = lax.broadcasted_iota(jnp.int32, (WINDOW, W), 1)
        lo = lane[:, :LANES] < HEAD_DIM
        lo2 = lax.broadcasted_iota(jnp.int32, (2 * WINDOW, LANES), 1) < HEAD_DIM
        cosf, sinv = jnp.tile(cos_ref[...], (1, W // LANES)), jnp.tile(sin_ref[...], (1, W // LANES))
        qr, _ = _qk_prep(q_ref[...], qg_ref[...], cosf, sinv, iq_ref[...], iqt_ref[...], lane)
        kr, _ = _qk_prep(k_ref[...], kg_ref[...], cosf[:, :KV_W], sinv[:, :KV_W], ik_ref[...], ikt_ref[...],
                         lane[:, :KV_W])
        kc_ref[WINDOW:2 * WINDOW, :] = kr
        vc_ref[WINDOW:2 * WINDOW, :] = v_ref[...]
        kc, vc = kc_ref[...], vc_ref[...]
        mask = jnp.tile(_attn_mask_t(n), (1, 4))
        qr = qr * HEAD_DIM ** -0.5
        for kvh in range(N_KV):
            qs_ref[kvh] = _stack_heads(qr, kvh, lo)
            kd_ref[kvh] = _dup_head(kc, kvh, lo2)
            vd_ref[kvh] = _dup_head(vc, kvh, lo2)

        def scores(kvh):
            s_ref[kvh % 2] = jnp.where(mask, _dot_nt(kd_ref[kvh], qs_ref[kvh]), -1e30)

        def softmax(kvh):
            b = kvh % 2
            for r in range(4):
                cols = slice(r * WINDOW, (r + 1) * WINDOW)
                mx, inv, _ = _softmax_stats(s_ref, b, cols, sink_ref[4 * kvh + r])
                for c in KEY_CHUNKS:
                    p_ref[b, c, cols] = (jnp.exp(s_ref[b, c, cols] - mx) * inv).astype(BF16)

        def output(kvh):
            o0, o1 = _unstack_heads(_dot_tn(p_ref[kvh % 2], vd_ref[kvh]), lo)
            o_ref[:, (2 * kvh) * LANES:(2 * kvh + 1) * LANES] = o0.astype(BF16)
            o_ref[:, (2 * kvh + 1) * LANES:(2 * kvh + 2) * LANES] = o1.astype(BF16)

        scores(0)
        for kvh in range(N_KV):
            if kvh + 1 < N_KV:
                scores(kvh + 1)
            softmax(kvh)
            output(kvh)
        kc_ref[0:WINDOW, :] = kr
        vc_ref[0:WINDOW, :] = v_ref[...]

    blk = lambda w: pl.BlockSpec((WINDOW, w), lambda s, n: (s * nblk + n, 0))
    pos = pl.BlockSpec((WINDOW, LANES), lambda s, n: (n, 0))
    outs, extra = _call(
        body, phases=phases, name="attn_fwd", grid=(n_seq, nblk),
        in_specs=[pl.BlockSpec(memory_space=pltpu.SMEM), blk(W), blk(KV_W), blk(KV_W), _const((1, W)),
                  _const((1, KV_W)), pos, pos, _const((W, LANES)), _const((LANES, W)), _const((KV_W, LANES)),
                  _const((LANES, KV_W))],
        out_specs=blk(W), out_shape=jax.ShapeDtypeStruct((T, W), BF16),
        scratch_shapes=[pltpu.VMEM((2 * WINDOW, KV_W), F32), pltpu.VMEM((2 * WINDOW, KV_W), F32),
                        pltpu.VMEM((2, 2 * WINDOW, 4 * WINDOW), F32), pltpu.VMEM((2, 2 * WINDOW, 4 * WINDOW), BF16),
                        pltpu.VMEM((N_KV, 4 * WINDOW, LANES), BF16), pltpu.VMEM((N_KV, 2 * WINDOW, LANES), BF16),
                        pltpu.VMEM((N_KV, 2 * WINDOW, LANES), BF16)],
    )(sinks, q, k, v, qg, kg, cosf, sins, ind_q, ind_qt, ind_k, ind_kt)
    return outs[0], extra


def _attn_bwd(do, q, k, v, qg, kg, sinks, cosf, sins, ind_q, ind_qt, ind_k, ind_kt, n_seq, S, phases=()):
    T = q.shape[0]
    nblk = S // WINDOW
    W = D_MODEL

    def body(sink_ref, do_ref, q_ref, k_ref, v_ref, qg_ref, kg_ref, cos_ref, sin_ref, iq_ref, iqt_ref, ik_ref,
             ikt_ref, dq_ref, dkc_ref, dkp_ref, dvc_ref, dvp_ref, dqg_ref, dsk_ref, kc_ref, vc_ref, dqr_ref,
             dk_ref, dv_ref, s_ref, dp_ref, p_ref, ds_ref, qs_ref, dos_ref, kd_ref, vd_ref):
        s_id, n = pl.program_id(0), pl.program_id(1)

        @pl.when((s_id == 0) & (n == 0))
        def _():
            dqg_ref[...] = jnp.zeros_like(dqg_ref)
            dsk_ref[...] = jnp.zeros_like(dsk_ref)

        @pl.when(n == 0)
        def _():
            kc_ref[...] = jnp.zeros_like(kc_ref)
            vc_ref[...] = jnp.zeros_like(vc_ref)

        lane = lax.broadcasted_iota(jnp.int32, (WINDOW, W), 1)
        lane_k = lane[:, :KV_W]
        lane128 = lane[:, :LANES]
        cosf, sinv = jnp.tile(cos_ref[...], (1, W // LANES)), jnp.tile(sin_ref[...], (1, W // LANES))
        qv = q_ref[...]
        qr, q_rstd = _qk_prep(qv, qg_ref[...], cosf, sinv, iq_ref[...], iqt_ref[...], lane)
        kr, _ = _qk_prep(k_ref[...], kg_ref[...], cosf[:, :KV_W], sinv[:, :KV_W], ik_ref[...], ikt_ref[...], lane_k)
        kc_ref[WINDOW:2 * WINDOW, :] = kr
        vc_ref[WINDOW:2 * WINDOW, :] = v_ref[...]
        kc, vc = kc_ref[...], vc_ref[...]
        dov = do_ref[...]
        mask = jnp.tile(_attn_mask_t(n), (1, 4))
        lo = lane128 < HEAD_DIM
        lo2 = lax.broadcasted_iota(jnp.int32, (2 * WINDOW, LANES), 1) < HEAD_DIM
        scale = HEAD_DIM ** -0.5
        qr = qr * scale
        dk_ref[...] = jnp.zeros_like(dk_ref)
        dv_ref[...] = jnp.zeros_like(dv_ref)
        for kvh in range(N_KV):
            qs_ref[kvh] = _stack_heads(qr, kvh, lo)
            dos_ref[kvh] = _stack_heads(dov, kvh, lo)
            kd_ref[kvh] = _dup_head(kc, kvh, lo2)
            vd_ref[kvh] = _dup_head(vc, kvh, lo2)

        def scores(kvh):
            b = kvh % 2
            s_ref[b] = jnp.where(mask, _dot_nt(kd_ref[kvh], qs_ref[kvh]), -1e30)
            dp_ref[b] = _dot_nt(vd_ref[kvh], dos_ref[kvh])

        def softmax(kvh):
            b = kvh % 2
            for r in range(4):
                cols = slice(r * WINDOW, (r + 1) * WINDOW)
                head = 4 * kvh + r
                mx, inv, ps = _softmax_stats(s_ref, b, cols, sink_ref[head])
                g8 = None
                for c in KEY_CHUNKS:
                    t = _fold8(jnp.exp(s_ref[b, c, cols] - mx) * dp_ref[b, c, cols], jnp.sum)
                    g8 = t if g8 is None else g8 + t
                dd = jnp.sum(g8, axis=0, keepdims=True) * inv
                for c in KEY_CHUNKS:
                    p = jnp.exp(s_ref[b, c, cols] - mx) * inv
                    p_ref[b, c, cols] = p.astype(BF16)
                    ds_ref[b, c, cols] = (p * (dp_ref[b, c, cols] - dd)).astype(BF16)
                dsk_ref[head:head + 1, :] -= ps * dd

        def grads(kvh):
            m, b = kvh // 2, kvh % 2
            dq0, dq1 = _unstack_heads(_dot_tn(ds_ref[b], kd_ref[kvh]) * scale, lo)
            dqr_ref[:, (2 * kvh) * LANES:(2 * kvh + 1) * LANES] = dq0
            dqr_ref[:, (2 * kvh + 1) * LANES:(2 * kvh + 2) * LANES] = dq1
            dk_ref[:, m * LANES:(m + 1) * LANES] += _fold_head(_dot(ds_ref[b], qs_ref[kvh]), kvh, lo2)
            dv_ref[:, m * LANES:(m + 1) * LANES] += _fold_head(_dot(p_ref[b], dos_ref[kvh]), kvh, lo2)

        scores(0)
        for kvh in range(N_KV):
            if kvh + 1 < N_KV:
                scores(kvh + 1)
            softmax(kvh)
            grads(kvh)
        dq, dqg = _qk_prep_bwd(dqr_ref[...], qv, q_rstd, qg_ref[...], cosf, sinv, iq_ref[...], iqt_ref[...], lane)
        dq_ref[...] = dq.astype(BF16)
        dqg_ref[...] += dqg
        dkp_ref[...] = dk_ref[0:WINDOW, :]
        dkc_ref[...] = dk_ref[WINDOW:2 * WINDOW, :]
        dvp_ref[...] = dv_ref[0:WINDOW, :]
        dvc_ref[...] = dv_ref[WINDOW:2 * WINDOW, :]
        kc_ref[0:WINDOW, :] = kr
        vc_ref[0:WINDOW, :] = v_ref[...]

    blk = lambda w: pl.BlockSpec((WINDOW, w), lambda s, n: (s * nblk + n, 0))
    pos = pl.BlockSpec((WINDOW, LANES), lambda s, n: (n, 0))
    kv_out = jax.ShapeDtypeStruct((T, KV_W), F32)
    stage = lambda dt: pltpu.VMEM((2, 2 * WINDOW, 4 * WINDOW), dt)
    return _call(
        body, phases=phases, name="attn_bwd", grid=(n_seq, nblk),
        in_specs=[pl.BlockSpec(memory_space=pltpu.SMEM), blk(W), blk(W), blk(KV_W), blk(KV_W), _const((1, W)),
                  _const((1, KV_W)), pos, pos, _const((W, LANES)), _const((LANES, W)), _const((KV_W, LANES)),
                  _const((LANES, KV_W))],
        out_specs=[blk(W), blk(KV_W), blk(KV_W), blk(KV_W), blk(KV_W), _const((1, W)), _const((N_HEADS, LANES))],
        out_shape=[jax.ShapeDtypeStruct((T, W), BF16), kv_out, kv_out, kv_out, kv_out,
                   jax.ShapeDtypeStruct((1, W), F32), jax.ShapeDtypeStruct((N_HEADS, LANES), F32)],
        scratch_shapes=[pltpu.VMEM((2 * WINDOW, KV_W), F32), pltpu.VMEM((2 * WINDOW, KV_W), F32),
                        pltpu.VMEM((WINDOW, W), F32), pltpu.VMEM((2 * WINDOW, KV_W), F32),
                        pltpu.VMEM((2 * WINDOW, KV_W), F32), stage(F32), stage(F32), stage(BF16), stage(BF16),
                        pltpu.VMEM((N_KV, 4 * WINDOW, LANES), BF16), pltpu.VMEM((N_KV, 4 * WINDOW, LANES), BF16),
                        pltpu.VMEM((N_KV, 2 * WINDOW, LANES), BF16), pltpu.VMEM((N_KV, 2 * WINDOW, LANES), BF16)],
    )(sinks, do, q, k, v, qg, kg, cosf, sins, ind_q, ind_qt, ind_k, ind_kt)


def _kv_bwd(dkc, dkp, dvc, dvp, k, kg, cosf, sins, ind_k, ind_kt, n_seq, S, phases=()):
    T = k.shape[0]
    nblk = S // WINDOW

    def body(dkc_ref, dkp_ref, dvc_ref, dvp_ref, k_ref, kg_ref, cos_ref, sin_ref, ik_ref, ikt_ref,
             dk_ref, dv_ref, dkg_ref):
        s_id, n = pl.program_id(0), pl.program_id(1)

        @pl.when((s_id == 0) & (n == 0))
        def _():
            dkg_ref[...] = jnp.zeros_like(dkg_ref)

        has_next = n < nblk - 1
        lane = lax.broadcasted_iota(jnp.int32, (WINDOW, KV_W), 1)
        dkr = dkc_ref[...] + jnp.where(has_next, dkp_ref[...], 0.0)
        dv_ref[...] = (dvc_ref[...] + jnp.where(has_next, dvp_ref[...], 0.0)).astype(BF16)
        cosf, sinv = jnp.tile(cos_ref[...], (1, KV_W // LANES)), jnp.tile(sin_ref[...], (1, KV_W // LANES))
        kv = k_ref[...]
        _, rstd = _qk_prep(kv, kg_ref[...], cosf, sinv, ik_ref[...], ikt_ref[...], lane)
        dk, dkg = _qk_prep_bwd(dkr, kv, rstd, kg_ref[...], cosf, sinv, ik_ref[...], ikt_ref[...], lane)
        dk_ref[...] = dk.astype(BF16)
        dkg_ref[...] += dkg

    cur = pl.BlockSpec((WINDOW, KV_W), lambda s, n: (s * nblk + n, 0))
    nxt = pl.BlockSpec((WINDOW, KV_W), lambda s, n: (s * nblk + jnp.minimum(n + 1, nblk - 1), 0))
    pos = pl.BlockSpec((WINDOW, LANES), lambda s, n: (n, 0))
    return _call(
        body, phases=phases, name="kv_bwd", grid=(n_seq, nblk),
        in_specs=[cur, nxt, cur, nxt, cur, _const((1, KV_W)), pos, pos, _const((KV_W, LANES)),
                  _const((LANES, KV_W))],
        out_specs=[cur, cur, _const((1, KV_W))],
        out_shape=[jax.ShapeDtypeStruct((T, KV_W), BF16), jax.ShapeDtypeStruct((T, KV_W), BF16),
                   jax.ShapeDtypeStruct((1, KV_W), F32)],
    )(dkc, dkp, dvc, dvp, k, kg, cosf, sins, ind_k, ind_kt)


def _merge_fwd(x, ya, o, ga, gb, w_rnn, w_attn, w_out, tm, phases=()):
    T = x.shape[0]
    W = D_MODEL

    def body(x_ref, ya_ref, o_ref, ga_ref, gb_ref, wr_ref, wa_ref, wo_ref, x1_ref, mg_ref, yao_ref, ybo_ref):
        y_a = _dot(ya_ref[...], wr_ref[...])
        y_b = _dot(o_ref[...], wa_ref[...])
        yao_ref[...] = y_a
        ybo_ref[...] = y_b
        mg = (_sigmoid(ga_ref[...]) * y_a + _sigmoid(gb_ref[...]) * y_b).astype(BF16)
        mg_ref[...] = mg
        x1_ref[...] = x_ref[...] + _dot(mg, wo_ref[...])

    row = pl.BlockSpec((tm, W), lambda i: (i, 0))
    sq = _const((W, W))
    return _call(
        body, phases=phases, name="merge_fwd", grid=(T // tm,),
        in_specs=[row, row, row, row, row, sq, sq, sq], out_specs=[row, row, row, row],
        out_shape=[jax.ShapeDtypeStruct((T, W), F32), jax.ShapeDtypeStruct((T, W), BF16),
                   jax.ShapeDtypeStruct((T, W), F32), jax.ShapeDtypeStruct((T, W), F32)],
    )(x, ya, o, ga, gb, w_rnn, w_attn, w_out)


def _merge_bwd(dx1, ga, gb, y_a, y_b, w_rnn, w_attn, w_out, tm, phases=()):
    T = dx1.shape[0]
    W = D_MODEL

    def body(dx1_ref, ga_ref, gb_ref, ya_ref, yb_ref, wr_ref, wa_ref, wo_ref,
             dga_ref, dgb_ref, dya_ref, dyb_ref, dyain_ref, do_ref):
        dm = _dot_nt(dx1_ref[...].astype(BF16), wo_ref[...])
        sa = _sigmoid(ga_ref[...])
        sb = _sigmoid(gb_ref[...])
        dga_ref[...] = (dm * ya_ref[...] * (sa * (1.0 - sa))).astype(BF16)
        dgb_ref[...] = (dm * yb_ref[...] * (sb * (1.0 - sb))).astype(BF16)
        dya = (dm * sa).astype(BF16)
        dyb = (dm * sb).astype(BF16)
        dya_ref[...] = dya
        dyb_ref[...] = dyb
        dyain_ref[...] = _dot_nt(dya, wr_ref[...])
        do_ref[...] = _dot_nt(dyb, wa_ref[...])

    row = pl.BlockSpec((tm, W), lambda i: (i, 0))
    sq = _const((W, W))
    b16 = jax.ShapeDtypeStruct((T, W), BF16)
    f32 = jax.ShapeDtypeStruct((T, W), F32)
    return _call(
        body, phases=phases, name="merge_bwd", grid=(T // tm,),
        in_specs=[row, row, row, row, row, sq, sq, sq], out_specs=[row] * 6,
        out_shape=[b16, b16, b16, b16, f32, f32],
    )(dx1, ga, gb, y_a, y_b, w_rnn, w_attn, w_out)


def _mlp_fwd(x1, g_mlp, w_up, w_down, tm, phases=()):
    T = x1.shape[0]
    W = D_MODEL

    def body(x_ref, g_ref, wu_ref, wd_ref, x2_ref, hm_ref, u_ref, act_ref):
        xv = x_ref[...]
        hm, _ = _rms_fwd(xv, g_ref[...])
        hmb = hm.astype(BF16)
        hm_ref[...] = hmb
        for j in range(N_CHIPS):
            u = _dot(hmb, wu_ref[j])
            u_ref[:, j * W:(j + 1) * W] = u
            ru = jnp.maximum(u, 0.0)
            act_ref[:, j * W:(j + 1) * W] = (ru * ru).astype(BF16)
        x2_ref[...] = xv + _dot(act_ref[...], wd_ref[...])

    row = lambda w: pl.BlockSpec((tm, w), lambda i: (i, 0))
    return _call(
        body, phases=phases, name="mlp_fwd", grid=(T // tm,),
        in_specs=[row(W), _const((1, W)), _const((N_CHIPS, W, W)), _const((D_FF, W))],
        out_specs=[row(W), row(W), row(D_FF), row(D_FF)],
        out_shape=[jax.ShapeDtypeStruct((T, W), F32), jax.ShapeDtypeStruct((T, W), BF16),
                   jax.ShapeDtypeStruct((T, D_FF), F32), jax.ShapeDtypeStruct((T, D_FF), BF16)],
    )(x1, g_mlp, w_up, w_down)


def _mlp_bwd(dx2, u, x1, g_mlp, w_up, w_down, tm, phases=()):
    T = x1.shape[0]
    W = D_MODEL

    def body(dx2_ref, u_ref, x_ref, g_ref, wu_ref, wd_ref, dx1_ref, du_ref, dg_ref):
        @pl.when(pl.program_id(0) == 0)
        def _():
            dg_ref[...] = jnp.zeros_like(dg_ref)

        dx2 = dx2_ref[...]
        dact = _dot_nt(dx2.astype(BF16), wd_ref[...])
        du_ref[...] = (dact * (2.0 * jnp.maximum(u_ref[...], 0.0))).astype(BF16)
        dhm = jnp.zeros((tm, W), F32)
        for j in range(N_CHIPS):
            dhm = dhm + _dot_nt(du_ref[:, j * W:(j + 1) * W], wu_ref[j])
        xv = x_ref[...]
        g = g_ref[...]
        _, r = _rms_fwd(xv, g)
        dx, dg = _rms_bwd(dhm, xv, r, g)
        dx1_ref[...] = dx2 + dx
        dg_ref[...] += dg

    row = lambda w: pl.BlockSpec((tm, w), lambda i: (i, 0))
    return _call(
        body, phases=phases, name="mlp_bwd", grid=(T // tm,),
        in_specs=[row(W), row(D_FF), row(W), _const((1, W)), _const((N_CHIPS, W, W)), _const((D_FF, W))],
        out_specs=[row(W), row(D_FF), _const((1, W))],
        out_shape=[jax.ShapeDtypeStruct((T, W), F32), jax.ShapeDtypeStruct((T, D_FF), BF16),
                   jax.ShapeDtypeStruct((1, W), F32)],
    )(dx2, u, x1, g_mlp, w_up, w_down)


def _ple_loss(x2, p, target, g_ple, w_gate, w_proj, tm, phases=()):
    T = x2.shape[0]
    W = D_MODEL
    cw = W // N_CHIPS

    def body(x_ref, p_ref, t_ref, g_ref, wg_ref, wp_ref, loss_ref, dx2_ref, pb_ref, de_ref, hp_ref, dtg_ref, dg_ref):
        @pl.when(pl.program_id(0) == 0)
        def _():
            dg_ref[...] = jnp.zeros_like(dg_ref)
            loss_ref[...] = jnp.zeros_like(loss_ref)

        xv = x_ref[...]
        g = g_ref[...]
        pb = p_ref[...].astype(BF16)
        pb_ref[...] = pb
        e = jnp.concatenate([_dot(pb, wp_ref[j]) for j in range(N_CHIPS)], axis=1)
        hp, r = _rms_fwd(xv, g)
        hpb = hp.astype(BF16)
        hp_ref[...] = hpb
        sg = _sigmoid(_dot(hpb, wg_ref[...]))
        diff = (xv + e * sg) - t_ref[...]
        loss_ref[...] += jnp.sum(diff * diff) * (0.5 / W)
        dx3 = diff * (1.0 / W)
        de_ref[...] = (dx3 * sg).astype(BF16)
        dtg = (dx3 * e * (sg * (1.0 - sg))).astype(BF16)
        dtg_ref[...] = dtg
        dx, dg = _rms_bwd(_dot_nt(dtg, wg_ref[...]), xv, r, g)
        dx2_ref[...] = dx3 + dx
        dg_ref[...] += dg

    row = lambda w: pl.BlockSpec((tm, w), lambda i: (i, 0))
    b16 = lambda w: jax.ShapeDtypeStruct((T, w), BF16)
    return _call(
        body, phases=phases, name="ple_loss", grid=(T // tm,),
        in_specs=[row(W), row(PLE_DIM), row(W), _const((1, W)), _const((W, W)), _const((N_CHIPS, PLE_DIM, cw))],
        out_specs=[_const((8, LANES)), row(W), row(PLE_DIM), row(W), row(W), row(W), _const((1, W))],
        out_shape=[jax.ShapeDtypeStruct((8, LANES), F32), jax.ShapeDtypeStruct((T, W), F32), b16(PLE_DIM),
                   b16(W), b16(W), b16(W), jax.ShapeDtypeStruct((1, W), F32)],
    )(x2, p, target, g_ple, w_gate, w_proj)


def _adamw(w, g, m, v, name, tr, phases=()):
    R, C = w.shape
    c1 = 1.0 / (1.0 - ADAM_B1 ** ADAM_STEP)
    c2 = 1.0 / (1.0 - ADAM_B2 ** ADAM_STEP)

    def body(w_ref, g_ref, m_ref, v_ref, go_ref, d_ref, nm_ref, nv_ref):
        gv = g_ref[...]
        go_ref[...] = gv
        nm = ADAM_B1 * m_ref[...] + (1.0 - ADAM_B1) * gv
        nv = ADAM_B2 * v_ref[...] + (1.0 - ADAM_B2) * (gv * gv)
        nm_ref[...] = nm
        nv_ref[...] = nv
        d_ref[...] = (-ADAM_LR) * ((nm * c1) / (jnp.sqrt(nv * c2) + ADAM_EPS) + ADAM_WD * w_ref[...])

    row = pl.BlockSpec((tr, C), lambda i: (i, 0))
    sds = jax.ShapeDtypeStruct((R, C), F32)
    return _call(
        body, phases=phases, name=name, grid=(R // tr,), in_specs=[row] * 4, out_specs=[row] * 4,
        out_shape=[sds] * 4,
    )(w, g, m, v)


def _indicator(width):
    ind = np.zeros((width, LANES), np.float32)
    ind[np.arange(width), np.arange(width) // HEAD_DIM] = 1.0
    return jnp.asarray(ind, BF16), jnp.asarray(ind.T, BF16)


def _rope_tables(S):
    inv = ROPE_THETA ** (-jnp.arange(0, HEAD_DIM, 2, dtype=F32) / HEAD_DIM)
    ang = jnp.arange(S, dtype=F32)[:, None] * inv[None, :]
    cos, sin = jnp.cos(ang), jnp.sin(ang)
    cosf = jnp.tile(jnp.concatenate([cos, cos], axis=1), (1, LANES // HEAD_DIM))
    sins = jnp.tile(jnp.concatenate([-sin, sin], axis=1), (1, LANES // HEAD_DIM))
    return cosf, sins


def _pair_blockdiag(w):
    w4 = w.reshape(8, 2, HEAD_DIM, HEAD_DIM)
    eye = jnp.eye(2, dtype=w.dtype)
    return jnp.einsum("bpij,pq->bpiqj", w4, eye).reshape(8, LANES, LANES)


def _pair_blockdiag_extract(g):
    g5 = g.reshape(8, 2, HEAD_DIM, 2, HEAD_DIM)
    return jnp.stack([g5[:, 0, :, 0, :], g5[:, 1, :, 1, :]], axis=1).reshape(16, HEAD_DIM, HEAD_DIM)


def _pair_sum(parts, sibs, name):
    n = len(parts)
    dims = [(p.shape[1] // 2, p.shape[2]) for p in parts]

    def body(*refs):
        p_r, s_r, send_r, own_r, mine_r, sem = (refs[0:n], refs[n:2 * n], refs[2 * n:3 * n], refs[3 * n:4 * n],
                                                refs[4 * n:5 * n], refs[5 * n])
        x, y, c, chips = _mesh_pos()
        me = 2 * x + y
        loads = []
        for i, (R, _) in enumerate(dims):
            mine, _ = _half_rows(c, R)
            cp = pltpu.make_async_copy(p_r[i].at[:, mine, :], mine_r[i], sem.at[i])
            cp.start()
            loads.append(cp)
        for i in range(n):
            loads[i].wait()
            for j, (cx, cy) in enumerate(chips):
                k = 2 * cx + cy
                send_r[i][j] = (mine_r[i][k] + s_r[i][k]).astype(BF16)
            own_r[i][...] = mine_r[i][me] + s_r[i][me]

    vm = pl.BlockSpec(memory_space=pltpu.VMEM)
    out = pl.pallas_call(
        body, name=name, in_specs=[pl.BlockSpec(memory_space=pl.ANY)] * n + [vm] * n, out_specs=[vm] * (2 * n),
        out_shape=[jax.ShapeDtypeStruct((3, R, C), BF16) for R, C in dims]
        + [jax.ShapeDtypeStruct((R, C), F32) for R, C in dims],
        scratch_shapes=[pltpu.VMEM((N_CHIPS, R, C), F32) for R, C in dims] + [pltpu.SemaphoreType.DMA((n,))],
        compiler_params=pltpu.CompilerParams(vmem_limit_bytes=VMEM_LIMIT),
    )(*parts, *sibs)
    return out[:n], out[n:]


def _chip_sum(owns, recvs, name):
    n = len(owns)
    dims = [o.shape for o in owns]

    def body(*refs):
        own_r, recv_r, red_r, stage_r, sem = refs[0:n], refs[n:2 * n], refs[2 * n:3 * n], refs[3 * n:4 * n], refs[4 * n]
        x, y, c, _ = _mesh_pos()
        me = 2 * x + y
        stores = []
        for i, (R, _) in enumerate(dims):
            acc = None
            for k in range(N_CHIPS):
                term = jnp.where(me == k, own_r[i][...], recv_r[i][_peer_slot(k, x, y)].astype(F32))
                acc = term if acc is None else acc + term
            stage_r[i][...] = acc
            mine, _ = _half_rows(c, R)
            cp = pltpu.make_async_copy(stage_r[i], red_r[i].at[mine, :], sem.at[i])
            cp.start()
            stores.append(cp)
        for cp in stores:
            cp.wait()

    vm = pl.BlockSpec(memory_space=pltpu.VMEM)
    return pl.pallas_call(
        body, name=name, in_specs=[vm] * (2 * n), out_specs=[pl.BlockSpec(memory_space=pl.ANY)] * n,
        out_shape=[jax.ShapeDtypeStruct((2 * R, C), F32) for R, C in dims],
        scratch_shapes=[pltpu.VMEM((R, C), F32) for R, C in dims] + [pltpu.SemaphoreType.DMA((n,))],
        compiler_params=pltpu.CompilerParams(vmem_limit_bytes=VMEM_LIMIT),
    )(*owns, *recvs)


def _gather_bf16(shard, name):
    R2, C = shard.shape
    R = R2 // 2

    def body(s_ref, o_ref, send_sems, recv_sems):
        x, y, c, chips = _mesh_pos()
        me = 2 * x + y
        mine = pl.ds(pl.multiple_of(c * R, R), R)
        theirs = pl.ds(pl.multiple_of((1 - c) * R, R), R)
        o_ref[me] = s_ref[...].astype(BF16)

        def copy(k, chip, rows, to):
            blk = o_ref.at[chip, rows]
            return pltpu.make_async_remote_copy(src_ref=blk, dst_ref=blk, send_sem=send_sems.at[k],
                                                recv_sem=recv_sems.at[k], device_id=to, device_id_type=MESH_ID)

        first = [copy(j, me, mine, (cx, cy, c)) for j, (cx, cy) in enumerate(chips)]
        for cp in first:
            cp.start()
        passed = []
        for j, (cx, cy) in enumerate(chips):
            copy(j, 2 * cx + cy, mine, (x, y, c)).wait_recv()
            cp = copy(3 + j, 2 * cx + cy, mine, (x, y, 1 - c))
            cp.start()
            passed.append(cp)
        for j, (cx, cy) in enumerate(chips):
            copy(3 + j, 2 * cx + cy, theirs, (x, y, c)).wait_recv()
        for cp in first + passed:
            cp.wait_send()

    return pl.pallas_call(
        body, name=name, out_shape=jax.ShapeDtypeStruct((N_CHIPS, R2, C), BF16),
        in_specs=[pl.BlockSpec(memory_space=pltpu.VMEM)], out_specs=pl.BlockSpec(memory_space=pltpu.VMEM),
        scratch_shapes=[pltpu.SemaphoreType.DMA((6,)), pltpu.SemaphoreType.DMA((6,))],
        compiler_params=pltpu.CompilerParams(vmem_limit_bytes=VMEM_LIMIT),
    )(shard)


def _pair_exchange_sum(partial, name):
    _, R2, C = partial.shape
    R = R2 // 2

    def body(p_ref, send_ref, own_ref, mine_ref, sib_ref, loc_sems, send_sems, recv_sems):
        x, y, c, chips = _mesh_pos()
        me = 2 * x + y
        mine, theirs = _half_rows(c, R)
        order = [2 * cx + cy for cx, cy in chips] + [me]
        locs, pairs = [], []
        for i, k in enumerate(order):
            loc = pltpu.make_async_copy(p_ref.at[k, mine, :], mine_ref.at[i], loc_sems.at[i])
            pair = _remote(p_ref.at[k, theirs, :], sib_ref.at[i], (send_sems.at[i], recv_sems.at[i]), (x, y, 1 - c))
            loc.start()
            pair.start()
            locs.append(loc)
            pairs.append(pair)
        for i in range(N_CHIPS):
            locs[i].wait()
            pairs[i].wait_recv()
            total = mine_ref[i] + sib_ref[i]
            if i < 3:
                send_ref[i] = total.astype(BF16)
            else:
                own_ref[...] = total
        for pair in pairs:
            pair.wait_send()

    vm = pl.BlockSpec(memory_space=pltpu.VMEM)
    return pl.pallas_call(
        body, name=name, in_specs=[pl.BlockSpec(memory_space=pl.ANY)], out_specs=[vm, vm],
        out_shape=[jax.ShapeDtypeStruct((3, R, C), BF16), jax.ShapeDtypeStruct((R, C), F32)],
        scratch_shapes=[pltpu.VMEM((N_CHIPS, R, C), F32), pltpu.VMEM((N_CHIPS, R, C), F32),
                        pltpu.SemaphoreType.DMA((N_CHIPS,)), pltpu.SemaphoreType.DMA((N_CHIPS,)),
                        pltpu.SemaphoreType.DMA((N_CHIPS,))],
        compiler_params=pltpu.CompilerParams(vmem_limit_bytes=VMEM_LIMIT),
    )(partial)


def _allreduce_small(buf, name):
    shape = buf.shape

    def body(b_ref, o_ref, sib_ref, pair_ref, in_ref, pair_sems, send_sems, recv_sems):
        x, y, c, chips = _mesh_pos()
        me = 2 * x + y
        pair = pltpu.make_async_remote_copy(src_ref=b_ref, dst_ref=sib_ref, send_sem=pair_sems.at[0],
                                            recv_sem=pair_sems.at[1], device_id=(x, y, 1 - c), device_id_type=MESH_ID)
        pair.start()
        pair.wait()
        pair_ref[...] = b_ref[...] + sib_ref[...]
        sends = []
        for j, (cx, cy) in enumerate(chips):
            cp = pltpu.make_async_remote_copy(src_ref=pair_ref, dst_ref=in_ref.at[j], send_sem=send_sems.at[j],
                                              recv_sem=recv_sems.at[j], device_id=(cx, cy, c), device_id_type=MESH_ID)
            cp.start()
            sends.append(cp)
        for cp in sends:
            cp.wait_recv()
        acc = None
        for k in range(N_CHIPS):
            term = jnp.where(me == k, pair_ref[...], in_ref[_peer_slot(k, x, y)])
            acc = term if acc is None else acc + term
        o_ref[...] = acc
        for cp in sends:
            cp.wait_send()

    return pl.pallas_call(
        body, name=name, out_shape=jax.ShapeDtypeStruct(shape, F32),
        in_specs=[pl.BlockSpec(memory_space=pltpu.VMEM)], out_specs=pl.BlockSpec(memory_space=pltpu.VMEM),
        scratch_shapes=[pltpu.VMEM(shape, F32), pltpu.VMEM(shape, F32), pltpu.VMEM((3,) + shape, F32),
                        pltpu.SemaphoreType.DMA((2,)), pltpu.SemaphoreType.DMA((3,)), pltpu.SemaphoreType.DMA((3,))],
        compiler_params=pltpu.CompilerParams(vmem_limit_bytes=VMEM_LIMIT),
    )(buf)


def _adamw_small(ws, gs, ms, vs):
    n = len(ws)
    c1 = 1.0 / (1.0 - ADAM_B1 ** ADAM_STEP)
    c2 = 1.0 / (1.0 - ADAM_B2 ** ADAM_STEP)

    def body(*refs):
        w_r, g_r, m_r, v_r = refs[0:n], refs[n:2 * n], refs[2 * n:3 * n], refs[3 * n:4 * n]
        d_r, nm_r, nv_r = refs[4 * n:5 * n], refs[5 * n:6 * n], refs[6 * n:7 * n]
        for i in range(n):
            gv = g_r[i][...]
            nm = ADAM_B1 * m_r[i][...] + (1.0 - ADAM_B1) * gv
            nv = ADAM_B2 * v_r[i][...] + (1.0 - ADAM_B2) * (gv * gv)
            nm_r[i][...] = nm
            nv_r[i][...] = nv
            d_r[i][...] = (-ADAM_LR) * ((nm * c1) / (jnp.sqrt(nv * c2) + ADAM_EPS) + ADAM_WD * w_r[i][...])

    vm = pl.BlockSpec(memory_space=pltpu.VMEM)
    sds = [jax.ShapeDtypeStruct(w.shape, F32) for w in ws]
    out = pl.pallas_call(body, name="adamw_small", in_specs=[vm] * (4 * n), out_specs=[vm] * (3 * n),
                         out_shape=sds * 3)(*ws, *gs, *ms, *vs)
    return out[0:n], out[n:2 * n], out[2 * n:3 * n]


_BIG = ("w_in", "w_rnn_proj", "w_attn_proj", "w_out", "w_up", "w_down", "w_ple_gate", "w_ple_proj")
_SMALL = ("g_mix", "conv_w", "conv_b", "w_rg", "b_rg", "w_ig", "b_ig", "lru_lambda", "q_gain", "k_gain", "sinks",
          "g_mlp", "g_ple")
_WEIGHTS = ("g_mix", "w_in", "conv_w", "conv_b", "w_rg", "b_rg", "w_ig", "b_ig", "lru_lambda", "w_rnn_proj",
            "q_gain", "k_gain", "sinks", "w_attn_proj", "w_out", "g_mlp", "w_up", "w_down", "g_ple", "w_ple_gate",
            "w_ple_proj")


def _pad_row(v):
    v = v.reshape(1, -1)
    return jnp.pad(v, ((0, 0), (0, D_MODEL - v.shape[1])))


def kernel(x, p, g_mix, w_in, conv_w, conv_b, w_rg, b_rg, w_ig, b_ig, lru_lambda, w_rnn_proj, q_gain, k_gain, sinks, w_attn_proj, w_out, g_mlp, w_up, w_down, g_ple, w_ple_gate, w_ple_proj, loss_target, m_g_mix, m_w_in, m_conv_w, m_conv_b, m_w_rg, m_b_rg, m_w_ig, m_b_ig, m_lru_lambda, m_w_rnn_proj, m_q_gain, m_k_gain, m_sinks, m_w_attn_proj, m_w_out, m_g_mlp, m_w_up, m_w_down, m_g_ple, m_w_ple_gate, m_w_ple_proj, v_g_mix, v_w_in, v_conv_w, v_conv_b, v_w_rg, v_b_rg, v_w_ig, v_b_ig, v_lru_lambda, v_w_rnn_proj, v_q_gain, v_k_gain, v_sinks, v_w_attn_proj, v_w_out, v_g_mlp, v_w_up, v_w_down, v_g_ple, v_w_ple_gate, v_w_ple_proj):
    w = dict(g_mix=g_mix, w_in=w_in, conv_w=conv_w, conv_b=conv_b, w_rg=w_rg, b_rg=b_rg, w_ig=w_ig, b_ig=b_ig,
             lru_lambda=lru_lambda, w_rnn_proj=w_rnn_proj, q_gain=q_gain, k_gain=k_gain, sinks=sinks,
             w_attn_proj=w_attn_proj, w_out=w_out, g_mlp=g_mlp, w_up=w_up, w_down=w_down, g_ple=g_ple,
             w_ple_gate=w_ple_gate, w_ple_proj=w_ple_proj)
    m = dict(g_mix=m_g_mix, w_in=m_w_in, conv_w=m_conv_w, conv_b=m_conv_b, w_rg=m_w_rg, b_rg=m_b_rg, w_ig=m_w_ig,
             b_ig=m_b_ig, lru_lambda=m_lru_lambda, w_rnn_proj=m_w_rnn_proj, q_gain=m_q_gain, k_gain=m_k_gain,
             sinks=m_sinks, w_attn_proj=m_w_attn_proj, w_out=m_w_out, g_mlp=m_g_mlp, w_up=m_w_up, w_down=m_w_down,
             g_ple=m_g_ple, w_ple_gate=m_w_ple_gate, w_ple_proj=m_w_ple_proj)
    v = dict(g_mix=v_g_mix, w_in=v_w_in, conv_w=v_conv_w, conv_b=v_conv_b, w_rg=v_w_rg, b_rg=v_b_rg, w_ig=v_w_ig,
             b_ig=v_b_ig, lru_lambda=v_lru_lambda, w_rnn_proj=v_w_rnn_proj, q_gain=v_q_gain, k_gain=v_k_gain,
             sinks=v_sinks, w_attn_proj=v_w_attn_proj, w_out=v_w_out, g_mlp=v_g_mlp, w_up=v_w_up, w_down=v_w_down,
             g_ple=v_g_ple, w_ple_gate=v_w_ple_gate, w_ple_proj=v_w_ple_proj)
    n_seq, S, _ = x.shape
    T = n_seq * S
    chip = 2 * lax.axis_index("x") + lax.axis_index("y")

    tm, tm_rnn = 512, 256
    xf, pf, tf = x.reshape(T, D_MODEL), p.reshape(T, PLE_DIM), loss_target.reshape(T, D_MODEL)
    first = lambda outs: [o[0] for o in outs]

    w_in_g = _gather_bf16(w["w_in"][0], "gather_w_in")
    wb = {name: w[name][0].astype(BF16) for name in _BIG if name != "w_in"}
    grp_mix, grp_mlp, grp_ple = ("w_rnn_proj", "w_attn_proj", "w_out"), ("w_up", "w_down"), ("w_ple_gate", "w_ple_proj")

    cw_full = jnp.zeros((8, D_MODEL), F32)
    cw_full = lax.dynamic_update_slice(cw_full, conv_w[0], (0, chip * (D_MODEL // N_CHIPS)))
    cw_full = _allreduce_small(0.5 * cw_full.reshape(64, LANES), "allgather_conv_w").reshape(8, D_MODEL)[0:CONV_W]

    cosf, sins = _rope_tables(S)
    ind_q, ind_qt = _indicator(D_MODEL)
    ind_k, ind_kt = _indicator(KV_W)
    wrg2 = _pair_blockdiag(w_rg[0]).astype(BF16)
    wig2 = _pair_blockdiag(w_ig[0]).astype(BF16)
    qg = jnp.tile(q_gain, (1, N_HEADS))
    kg = jnp.tile(k_gain, (1, N_KV))
    sk = sinks.reshape(N_HEADS)
    rnn_w = (cw_full, conv_b, wrg2, b_rg, wig2, b_ig, lru_lambda)
    attn_c = (qg, kg, sk, cosf, sins, ind_q, ind_qt, ind_k, ind_kt, n_seq, S)

    (h0, xr, gr, zq, zk, zv, ga, gb), ph = _inproj_fwd(xf, g_mix, w_in_g, tm,
                                                     phases=[_ph_gather_send(wb[n]) for n in grp_mix])
    g_mixw = first(ph)
    o, ph = _attn_fwd(zq, zk, zv, *attn_c,
                      phases=[_ph_gather_pass(g) for g in g_mixw] + [_ph_gather_send(wb["w_up"])])
    g_mixw, wu = first(ph[:3]), ph[3][0]
    (xc, h, ya), ph = _rnn_fwd(xr, gr, *rnn_w, n_seq, S, tm_rnn,
                               phases=[_ph_gather_pass(wu), _ph_gather_send(wb["w_down"])])
    wu, wd = ph[0][0], ph[1][0]
    wr, wa, wo = (g.reshape(D_MODEL, D_MODEL) for g in g_mixw)
    (x1, merged, y_a, y_b), ph = _merge_fwd(xf, ya, o, ga, gb, wr, wa, wo, tm,
                                            phases=[_ph_gather_pass(wd)] + [_ph_gather_send(wb[n]) for n in grp_ple])
    wd, g_plew = ph[0][0].reshape(D_FF, D_MODEL), first(ph[1:])
    (x2, hm, u, act), ph = _mlp_fwd(x1, g_mlp, wu, wd, tm // 2, phases=[_ph_gather_pass(g) for g in g_plew])
    wpg, wpp = first(ph)
    wpg = wpg.reshape(D_MODEL, D_MODEL)
    (loss_t, dx2, pb, de, hp, dtg, dg_ple), _ = _ple_loss(x2, pf, tf, g_ple, wpg, wpp, tm)

    chipmajor = lambda g: g.reshape(N_CHIPS, g.shape[-2] // N_CHIPS, g.shape[-1]) if g.ndim == 2 else g
    tmw = min(2 * tm, T)
    dw_pp = _wgrad(pb, de, "wgrad_ple_proj", False, D_MODEL, tmw)[0]
    part_ple = [chipmajor(_wgrad(hp, dtg, "wgrad_ple_gate", False, D_MODEL, tmw)[0]),
                dw_pp.reshape(PLE_DIM, N_CHIPS, D_MODEL // N_CHIPS).transpose(1, 0, 2)]
    (dx1, du, dg_mlp), ph = _mlp_bwd(dx2, u, x1, g_mlp, wu, wd, tm // 2, phases=[_ph_pair_send(g) for g in part_ple])
    send_ple, own_ple = _pair_sum(part_ple, first(ph), "pair_sum_ple")
    dw_down, ph = _wgrad(act, dx2, "wgrad_down", False, D_MODEL // 2, tmw, phases=[_ph_chip_send(s) for s in send_ple])
    red_ple = _chip_sum(own_ple, first(ph), "chip_sum_ple")
    part_mlp = [_wgrad(hm, du, "wgrad_up", True, D_MODEL, tmw)[0], chipmajor(dw_down)]
    (dga, dgb, dya, dyb, dyain, do), ph = _merge_bwd(
        dx1, ga, gb, y_a, y_b, wr, wa, wo, tm,
        phases=[_ph_half_swap(r) for r in red_ple] + [_ph_pair_send(g) for g in part_mlp])
    red_ple = first(ph[:2])
    send_mlp, own_mlp = _pair_sum(part_mlp, first(ph[2:]), "pair_sum_mlp")
    part_mix = [chipmajor(_wgrad(ya, dya, "wgrad_rnn_proj", False, D_MODEL, tmw)[0]),
                chipmajor(_wgrad(o, dyb, "wgrad_attn_proj", False, D_MODEL, tmw)[0]),
                chipmajor(_wgrad(merged, dx1, "wgrad_out", False, D_MODEL, tmw)[0])]
    (dxr, dgr, vec, dwrg2, dwig2), ph = _rnn_bwd(
        dyain, xr, gr, xc, h, cw_full, wrg2, b_rg, wig2, b_ig, lru_lambda, n_seq, S, tm_rnn,
        phases=[_ph_chip_send(s) for s in send_mlp] + [_ph_pair_send(g) for g in part_mix])
    red_mlp = _chip_sum(own_mlp, first(ph[:2]), "chip_sum_mlp")
    send_mix, own_mix = _pair_sum(part_mix, first(ph[2:]), "pair_sum_mix")
    (dq, dkc, dkp, dvc, dvp, dqg, dsk), ph = _attn_bwd(
        do, zq, zk, zv, *attn_c, phases=[_ph_half_swap(r) for r in red_mlp] + [_ph_chip_send(s) for s in send_mix])
    red_mlp = first(ph[:2])
    red_mix = _chip_sum(own_mix, first(ph[2:]), "chip_sum_mix")
    (dk, dv, dkg), _ = _kv_bwd(dkc, dkp, dvc, dvp, zk, kg, cosf, sins, ind_k, ind_kt, n_seq, S)
    dz_parts = [dxr, dgr, dq, dk, dv, dga, dgb]
    send_in, own_in = _pair_exchange_sum(_wgrad_in(h0, dz_parts, tm), "pair_sum_in")
    (grad_x, dg_mix), ph = _inproj_bwd(dz_parts, w_in_g, xf, g_mix, dx1, tm,
                                       phases=[_ph_half_swap(r) for r in red_mix] + [_ph_chip_send(send_in)])
    red_mix = first(ph[:3])
    red_in = _chip_sum([own_in], first(ph[3:]), "chip_sum_in")
    reduced = dict(zip(grp_ple + grp_mlp + grp_mix, red_ple + red_mlp + red_mix))
    grads = {
        "g_mix": dg_mix[0], "g_mlp": dg_mlp[0], "g_ple": dg_ple[0],
        "conv_w": vec[0:CONV_W], "conv_b": vec[4], "b_rg": vec[5], "b_ig": vec[6], "lru_lambda": vec[7],
        "w_rg": _pair_blockdiag_extract(dwrg2), "w_ig": _pair_blockdiag_extract(dwig2),
        "q_gain": dqg.reshape(N_HEADS, HEAD_DIM).sum(0), "k_gain": dkg.reshape(N_KV, HEAD_DIM).sum(0),
        "sinks": dsk.sum(1),
    }

    rows = [grads["conv_w"], _pad_row(grads["conv_b"]), _pad_row(grads["b_rg"]), _pad_row(grads["b_ig"]),
            _pad_row(grads["lru_lambda"]), _pad_row(grads["g_mix"]), _pad_row(grads["g_mlp"]),
            _pad_row(grads["g_ple"]), _pad_row(grads["q_gain"]), _pad_row(grads["k_gain"]), _pad_row(grads["sinks"]),
            _pad_row(loss_t[0:1, 0:1]), jnp.zeros((1, D_MODEL), F32)]
    vecs = jnp.concatenate(rows, axis=0)
    packed = jnp.concatenate([vecs.reshape(-1, LANES), grads["w_rg"].reshape(-1, LANES),
                              grads["w_ig"].reshape(-1, LANES)], axis=0)
    red = _allreduce_small(packed, "allreduce_small")
    nv = vecs.size // LANES
    rvec = red[0:nv].reshape(16, D_MODEL)
    loss = rvec[14, 0]
    nw = grads["w_rg"].size // LANES
    sg = {
        "conv_w": lax.dynamic_slice(rvec[0:CONV_W], (0, chip * (D_MODEL // N_CHIPS)), (CONV_W, D_MODEL // N_CHIPS)),
        "conv_b": rvec[4], "b_rg": rvec[5], "b_ig": rvec[6], "lru_lambda": rvec[7], "g_mix": rvec[8],
        "g_mlp": rvec[9], "g_ple": rvec[10], "q_gain": rvec[11, :HEAD_DIM], "k_gain": rvec[12, :HEAD_DIM],
        "sinks": rvec[13, :N_HEADS], "w_rg": red[nv:nv + nw], "w_ig": red[nv + nw:nv + 2 * nw],
    }
    sg = {k: sg[k].reshape(w[k].shape) for k in _SMALL}
    d_s, m_s, v_s = _adamw_small([w[k] for k in _SMALL], [sg[k] for k in _SMALL], [m[k] for k in _SMALL],
                                 [v[k] for k in _SMALL])
    grad, delta, new_m, new_v = dict(sg), dict(zip(_SMALL, d_s)), dict(zip(_SMALL, m_s)), dict(zip(_SMALL, v_s))

    for name in ("w_ple_proj", "w_up", "w_down", "w_rnn_proj", "w_attn_proj", "w_out", "w_ple_gate", "w_in"):
        shape = w[name].shape
        outs, ph = _adamw(w[name][0], reduced[name], m[name][0], v[name][0], "adamw_" + name, 128,
                          phases=[_ph_half_swap(r) for r in red_in] if name == "w_ple_proj" else ())
        if name == "w_ple_proj":
            reduced["w_in"] = ph[0][0]
        grad[name], delta[name], new_m[name], new_v[name] = (a.reshape(shape) for a in outs)

    return (loss, grad_x.reshape(x.shape), *[grad[k] for k in _WEIGHTS], *[delta[k] for k in _WEIGHTS],
            *[new_m[k] for k in _WEIGHTS], *[new_v[k] for k in _WEIGHTS])
```

```python
import functools
import math

import numpy as np
import jax
import jax.numpy as jnp
from jax import lax
from jax.experimental import pallas as pl
from jax.experimental.pallas import tpu as pltpu

F32 = jnp.float32
BF16 = jnp.bfloat16

D_MODEL = 1024
N_HEADS = 16
N_KV = 4
HEAD_DIM = 64
KV_W = N_KV * HEAD_DIM
D_FF = 4096
PLE_DIM = 256
WINDOW = 128
CONV_W = 4
LRU_C = 8.0
NORM_EPS = 1e-6
ROPE_THETA = 10000.0
N_CHIPS = 4
IN_TOTAL = 5632
IN_BLK = IN_TOTAL // N_CHIPS
IN_SEGS = (0, 1024, 2048, 3072, 3328, 3584, 4608, 5632)

ADAM_LR = 0.001
ADAM_B1 = 0.9
ADAM_B2 = 0.999
ADAM_EPS = 1e-08
ADAM_WD = 0.01
ADAM_STEP = 10

LANES = 128
VMEM_LIMIT = 56 * 1024 * 1024
MESH_ID = pl.DeviceIdType.MESH


def _dot(a, b):
    return jnp.dot(a, b, preferred_element_type=F32)


def _dot_nt(a, b):
    return lax.dot_general(a, b, (((1,), (1,)), ((), ())), preferred_element_type=F32)


def _dot_tn(a, b):
    return lax.dot_general(a, b, (((0,), (0,)), ((), ())), preferred_element_type=F32)


def _split_dot(x, ind):
    hi = x.astype(BF16)
    lo = (x - hi.astype(F32)).astype(BF16)
    return _dot(hi, ind) + _dot(lo, ind)


def _sigmoid(x):
    return 1.0 / (1.0 + jnp.exp(-x))


_GELU_C = math.sqrt(2.0 / math.pi)


def _gelu_and_grad(g):
    inner = _GELU_C * (g + 0.044715 * g * g * g)
    t = jnp.tanh(inner)
    gelu = 0.5 * g * (1.0 + t)
    dgelu = 0.5 * (1.0 + t) + 0.5 * g * (1.0 - t * t) * _GELU_C * (1.0 + 3.0 * 0.044715 * g * g)
    return gelu, dgelu


def _const(shape):
    nd = len(shape)
    return pl.BlockSpec(shape, lambda *_: (0,) * nd)


def _params(n_grid, vmem=VMEM_LIMIT):
    return pltpu.CompilerParams(dimension_semantics=("arbitrary",) * n_grid, vmem_limit_bytes=vmem)


def _rms_fwd(x, g):
    r = lax.rsqrt(jnp.mean(x * x, axis=-1, keepdims=True) + NORM_EPS)
    return (x * r) * g, r


def _rms_bwd(dy, x, r, g):
    dn = dy * g
    dx = r * dn - x * (r * r * r * jnp.mean(dn * x, axis=-1, keepdims=True))
    dg = jnp.sum(dy * (x * r), axis=0, keepdims=True)
    return dx, dg


def _seg_pieces(blk_lo, blk_hi):
    out = []
    for s in range(7):
        lo, hi = max(blk_lo, IN_SEGS[s]), min(blk_hi, IN_SEGS[s + 1])
        if lo < hi:
            out.append((s, lo - IN_SEGS[s], hi - IN_SEGS[s], lo - blk_lo))
    return out


def _mesh_pos():
    x, y, c = lax.axis_index("x"), lax.axis_index("y"), lax.axis_index("c")
    other_chips = [(1 - x, y), (x, 1 - y), (1 - x, 1 - y)]
    return x, y, c, other_chips


def _peer_slot(k, x, y):
    dx = jnp.bitwise_xor(k // 2, x)
    dy = jnp.bitwise_xor(k % 2, y)
    return jnp.maximum(dx + 2 * dy - 1, 0)


def _half_rows(c, R):
    return pl.ds(pl.multiple_of(c * R, R), R), pl.ds(pl.multiple_of((1 - c) * R, R), R)


def _remote(src, dst, sems, to):
    return pltpu.make_async_remote_copy(src_ref=src, dst_ref=dst, send_sem=sems[0], recv_sem=sems[1],
                                        device_id=to, device_id_type=MESH_ID)


class _Phase:
    def __init__(self, ins, inout, outs, n_remote, n_local, build):
        self.ins, self.inout, self.outs = list(ins), list(inout), list(outs)
        self.n_remote, self.n_local, self.build = n_remote, n_local, build


def _ph_gather_send(wb):
    R2, C = wb.shape
    R = R2 // 2

    def build(ins, outs, rsem, lsem):
        (w_ref,), (g_ref,) = ins, outs
        x, y, c, chips = _mesh_pos()
        me = 2 * x + y
        mine, _ = _half_rows(c, R)
        loc = [pltpu.make_async_copy(w_ref, g_ref.at[me], lsem(0))]
        outg = [_remote(w_ref.at[mine], g_ref.at[me, mine], rsem(j), (cx, cy, c)) for j, (cx, cy) in enumerate(chips)]
        inc = [functools.partial(_remote, w_ref.at[mine], g_ref.at[2 * cx + cy, mine], rsem(j), (x, y, c))
               for j, (cx, cy) in enumerate(chips)]
        return loc, outg, inc

    return _Phase([wb], [], [jax.ShapeDtypeStruct((N_CHIPS, R2, C), BF16)], 3, 1, build)


def _ph_gather_pass(gath):
    _, R2, C = gath.shape
    R = R2 // 2

    def build(ins, outs, rsem, lsem):
        (g_ref,) = outs
        x, y, c, chips = _mesh_pos()
        mine, theirs = _half_rows(c, R)
        outg, inc = [], []
        for j, (cx, cy) in enumerate(chips):
            blk = g_ref.at[2 * cx + cy, mine]
            outg.append(_remote(blk, blk, rsem(j), (x, y, 1 - c)))
            got = g_ref.at[2 * cx + cy, theirs]
            inc.append(functools.partial(_remote, got, got, rsem(j), (x, y, c)))
        return [], outg, inc

    return _Phase([], [gath], [], 3, 0, build)


def _ph_pair_send(partial):
    _, R2, C = partial.shape
    R = R2 // 2

    def build(ins, outs, rsem, lsem):
        (p_ref,), (s_ref,) = ins, outs
        x, y, c, _ = _mesh_pos()
        _, theirs = _half_rows(c, R)
        src = p_ref.at[:, theirs, :]
        return ([], [_remote(src, s_ref, rsem(0), (x, y, 1 - c))],
                [functools.partial(_remote, src, s_ref, rsem(0), (x, y, c))])

    return _Phase([partial], [], [jax.ShapeDtypeStruct((N_CHIPS, R, C), F32)], 1, 0, build)


def _ph_chip_send(sendb):
    def build(ins, outs, rsem, lsem):
        (s_ref,), (r_ref,) = ins, outs
        x, y, c, chips = _mesh_pos()
        outg = [_remote(s_ref.at[j], r_ref.at[j], rsem(j), (cx, cy, c)) for j, (cx, cy) in enumerate(chips)]
        inc = [functools.partial(_remote, s_ref.at[j], r_ref.at[j], rsem(j), (x, y, c)) for j in range(3)]
        return [], outg, inc

    return _Phase([sendb], [], [jax.ShapeDtypeStruct(sendb.shape, sendb.dtype)], 3, 0, build)


def _ph_half_swap(red):
    R2, C = red.shape
    R = R2 // 2

    def build(ins, outs, rsem, lsem):
        (r_ref,) = outs
        x, y, c, _ = _mesh_pos()
        mine, theirs = _half_rows(c, R)
        return ([], [_remote(r_ref.at[mine], r_ref.at[mine], rsem(0), (x, y, 1 - c))],
                [functools.partial(_remote, r_ref.at[theirs], r_ref.at[theirs], rsem(0), (x, y, c))])

    return _Phase([], [red], [], 1, 0, build)


def _call(body, *, name, grid, in_specs, out_specs, out_shape, scratch_shapes=(), phases=()):
    single = not isinstance(out_specs, (list, tuple))
    out_specs = [out_specs] if single else list(out_specs)
    out_shape = [out_shape] if single else list(out_shape)
    n_in, n_out, n_scr = len(in_specs), len(out_specs), len(scratch_shapes)
    if not phases:
        call = pl.pallas_call(body, name=name, grid=grid, in_specs=in_specs, out_specs=out_specs,
                              out_shape=out_shape, scratch_shapes=list(scratch_shapes),
                              compiler_params=_params(len(grid)))
        return lambda *operands: (list(call(*operands)), [])

    ex_in, ex_out, aliases, spans = [], [], {}, []
    for ph in phases:
        i0, o0 = len(ex_in), len(ex_out)
        ex_in += ph.ins
        for a in ph.inout:
            aliases[n_in + len(ex_in)] = n_out + len(ex_out)
            ex_in.append(a)
            ex_out.append(jax.ShapeDtypeStruct(a.shape, a.dtype))
        ex_out += ph.outs
        spans.append((i0, len(ph.ins), o0, len(ex_out) - o0))
    n_remote = sum(ph.n_remote for ph in phases)
    n_local = max(sum(ph.n_local for ph in phases), 1)

    def wrapped(*refs):
        base_in, xin = refs[:n_in], refs[n_in:n_in + len(ex_in)]
        o0 = n_in + len(ex_in)
        base_out, xout = refs[o0:o0 + n_out], refs[o0 + n_out:o0 + n_out + len(ex_out)]
        scr = refs[o0 + n_out + len(ex_out):]
        send_sems, recv_sems, loc_sems = scr[n_scr:]
        first = functools.reduce(jnp.logical_and, [pl.program_id(i) == 0 for i in range(len(grid))])
        last = functools.reduce(jnp.logical_and, [pl.program_id(i) == grid[i] - 1 for i in range(len(grid))])

        def copies():
            out, r0, l0 = [], 0, 0
            for ph, (i0, ni, p0, no) in zip(phases, spans):
                rsem = lambda k, r0=r0: (send_sems.at[r0 + k], recv_sems.at[r0 + k])
                lsem = lambda k, l0=l0: loc_sems.at[l0 + k]
                out.append(ph.build(xin[i0:i0 + ni], xout[p0:p0 + no], rsem, lsem))
                r0, l0 = r0 + ph.n_remote, l0 + ph.n_local
            return out

        @pl.when(first)
        def _():
            for loc, outg, _ in copies():
                for cp in loc + outg:
                    cp.start()

        body(*base_in, *base_out, *scr[:n_scr])

        @pl.when(last)
        def _():
            for loc, outg, inc in copies():
                for make in inc:
                    make().wait_recv()
                for cp in outg:
                    cp.wait_send()
                for cp in loc:
                    cp.wait()

    hbm = pl.BlockSpec(memory_space=pl.ANY)
    call = pl.pallas_call(
        wrapped, name=name, grid=grid, in_specs=list(in_specs) + [hbm] * len(ex_in),
        out_specs=out_specs + [hbm] * len(ex_out), out_shape=out_shape + ex_out,
        scratch_shapes=list(scratch_shapes) + [pltpu.SemaphoreType.DMA((n_remote,)), pltpu.SemaphoreType.DMA((n_remote,)),
                                              pltpu.SemaphoreType.DMA((n_local,))],
        input_output_aliases=aliases, compiler_params=_params(len(grid)))

    def run(*operands):
        res = call(*operands, *ex_in)
        extra = res[n_out:]
        return list(res[:n_out]), [list(extra[p0:p0 + no]) for (_, _, p0, no) in spans]

    return run


def _inproj_fwd(x, g_mix, w_in, tm, phases=()):
    T = x.shape[0]
    widths = [IN_SEGS[i + 1] - IN_SEGS[i] for i in range(7)]

    def body(x_ref, g_ref, w_ref, h_ref, *z_refs):
        h, _ = _rms_fwd(x_ref[...], g_ref[...])
        hb = h.astype(BF16)
        h_ref[...] = hb
        for j in range(N_CHIPS):
            zj = _dot(hb, w_ref[j])
            for s, lo, hi, off in _seg_pieces(j * IN_BLK, (j + 1) * IN_BLK):
                z_refs[s][:, lo:hi] = zj[:, off:off + hi - lo]

    return _call(
        body, phases=phases, name="inproj_fwd", grid=(T // tm,),
        in_specs=[pl.BlockSpec((tm, D_MODEL), lambda i: (i, 0)), _const((1, D_MODEL)),
                  _const((N_CHIPS, D_MODEL, IN_BLK))],
        out_specs=[pl.BlockSpec((tm, D_MODEL), lambda i: (i, 0))]
        + [pl.BlockSpec((tm, w), lambda i: (i, 0)) for w in widths],
        out_shape=[jax.ShapeDtypeStruct((T, D_MODEL), BF16)]
        + [jax.ShapeDtypeStruct((T, w), F32) for w in widths],
    )(x, g_mix, w_in)


def _inproj_bwd(dz_parts, w_in, x, g_mix, dx1, tm, phases=()):
    T = x.shape[0]
    widths = [IN_SEGS[i + 1] - IN_SEGS[i] for i in range(7)]

    def body(*refs):
        p_refs = refs[:7]
        w_ref, x_ref, g_ref, dx1_ref, gx_ref, dg_ref, dz_ref = refs[7:]

        @pl.when(pl.program_id(0) == 0)
        def _():
            dg_ref[...] = jnp.zeros_like(dg_ref)

        for s in range(7):
            dz_ref[:, IN_SEGS[s]:IN_SEGS[s + 1]] = p_refs[s][...]
        dh = jnp.zeros((tm, D_MODEL), F32)
        for j in range(N_CHIPS):
            dh = dh + _dot_nt(dz_ref[:, j * IN_BLK:(j + 1) * IN_BLK], w_ref[j])
        xv = x_ref[...]
        g = g_ref[...]
        _, r = _rms_fwd(xv, g)
        dx, dg = _rms_bwd(dh, xv, r, g)
        gx_ref[...] = dx1_ref[...] + dx
        dg_ref[...] += dg

    row = lambda w: pl.BlockSpec((tm, w), lambda i: (i, 0))
    return _call(
        body, phases=phases, name="inproj_bwd", grid=(T // tm,),
        in_specs=[row(w) for w in widths]
        + [_const((N_CHIPS, D_MODEL, IN_BLK)), row(D_MODEL), _const((1, D_MODEL)), row(D_MODEL)],
        out_specs=[row(D_MODEL), _const((1, D_MODEL))],
        out_shape=[jax.ShapeDtypeStruct((T, D_MODEL), F32), jax.ShapeDtypeStruct((1, D_MODEL), F32)],
        scratch_shapes=[pltpu.VMEM((tm, IN_TOTAL), BF16)],
    )(*dz_parts, w_in, x, g_mix, dx1)


def _wgrad_in(h0, dz_parts, tm):
    T = h0.shape[0]
    widths = [IN_SEGS[i + 1] - IN_SEGS[i] for i in range(7)]

    def body(*refs):
        h_ref, p_refs, o_ref, acc_ref, sem = refs[0], refs[1:8], refs[8], refs[9], refs[10]
        t = pl.program_id(0)

        @pl.when(t == 0)
        def _():
            acc_ref[...] = jnp.zeros_like(acc_ref)

        hv = h_ref[...]
        for j in range(N_CHIPS):
            for s, lo, hi, off in _seg_pieces(j * IN_BLK, (j + 1) * IN_BLK):
                acc_ref[j, :, off:off + hi - lo] += _dot_tn(hv, p_refs[s][:, lo:hi])

        @pl.when(t == T // tm - 1)
        def _():
            cp = pltpu.make_async_copy(acc_ref, o_ref, sem)
            cp.start()
            cp.wait()

    row = lambda w: pl.BlockSpec((tm, w), lambda i: (i, 0))
    return pl.pallas_call(
        body, name="wgrad_in", grid=(T // tm,), in_specs=[row(D_MODEL)] + [row(w) for w in widths],
        out_specs=pl.BlockSpec(memory_space=pl.ANY),
        out_shape=jax.ShapeDtypeStruct((N_CHIPS, D_MODEL, IN_BLK), F32),
        scratch_shapes=[pltpu.VMEM((N_CHIPS, D_MODEL, IN_BLK), F32), pltpu.SemaphoreType.DMA],
        compiler_params=_params(1),
    )(h0, *dz_parts)


def _wgrad(a, g, name, blocked, cn, tm, phases=()):
    T, K = a.shape
    N = g.shape[1]
    nb = N // cn

    def body(a_ref, g_ref, o_ref):
        @pl.when(pl.program_id(1) == 0)
        def _():
            o_ref[...] = jnp.zeros_like(o_ref)

        o_ref[...] += _dot_tn(a_ref[...].astype(BF16), g_ref[...].astype(BF16))

    if blocked:
        out_spec = pl.BlockSpec((None, K, cn), lambda j, t: (j, 0, 0))
        out_shape = jax.ShapeDtypeStruct((nb, K, cn), F32)
    else:
        out_spec = pl.BlockSpec((K, cn), lambda j, t: (0, j))
        out_shape = jax.ShapeDtypeStruct((K, N), F32)
    outs, extra = _call(
        body, phases=phases, name=name, grid=(nb, T // tm),
        in_specs=[pl.BlockSpec((tm, K), lambda j, t: (t, 0)), pl.BlockSpec((tm, cn), lambda j, t: (t, j))],
        out_specs=out_spec, out_shape=out_shape,
    )(a, g)
    return outs[0], extra


def _shift_down(x, prev8, sft, row, row8, tm):
    xs = pltpu.roll(x, sft, 0)
    top = jnp.where(row8 < sft, pltpu.roll(prev8, sft, 0), xs[0:8])
    return jnp.concatenate([top, xs[8:]], axis=0)


def _shift_up(x, next8, sft, row8, tm):
    xs = pltpu.roll(x, tm - sft, 0)
    bot = jnp.where(row8 >= 8 - sft, pltpu.roll(next8, 8 - sft, 0), xs[tm - 8:tm])
    return jnp.concatenate([xs[0:tm - 8], bot], axis=0)


def _conv_fwd(x, prev8, cw_ref, cb, row, row8, tm):
    xc = cb + cw_ref[CONV_W - 1:CONV_W, :] * x
    for sft in range(1, CONV_W):
        j = CONV_W - 1 - sft
        xc = xc + cw_ref[j:j + 1, :] * _shift_down(x, prev8, sft, row, row8, tm)
    return xc


def _blockdiag_dot(xb, w_ref, transpose):
    outs = []
    for b in range(D_MODEL // LANES):
        xs = xb[:, b * LANES:(b + 1) * LANES]
        outs.append(_dot_nt(xs, w_ref[b]) if transpose else _dot(xs, w_ref[b]))
    return jnp.concatenate(outs, axis=1)


def _softplus_neg(lam):
    e = jnp.exp(-jnp.abs(lam))
    u = 1.0 + e
    log1p_e = jnp.where(u == 1.0, e, jnp.log(u) * (e / (u - 1.0)))
    sp = jnp.maximum(-lam, 0.0) + log1p_e
    return sp, -_sigmoid(-lam)


def _lru_gates(xc, wrg_ref, brg, wig_ref, big, sp):
    xcb = xc.astype(BF16)
    r = _sigmoid(_blockdiag_dot(xcb, wrg_ref, False) + brg)
    i = _sigmoid(_blockdiag_dot(xcb, wig_ref, False) + big)
    log_a = (-LRU_C) * r * sp
    a = jnp.exp(log_a)
    t = jnp.tanh(log_a)
    one_m_a2 = (-2.0) * t / (1.0 - t)
    mult = jnp.sqrt(one_m_a2)
    return xcb, r, i, a, mult


def _scan_down(a, b, row, tm):
    d = 1
    while d < tm:
        if d < 8:
            keep = row >= d
            a_s = jnp.where(keep, pltpu.roll(a, d, 0), 1.0)
            b_s = jnp.where(keep, pltpu.roll(b, d, 0), 0.0)
            b = a * b_s + b
            a = a * a_s
        else:
            b = jnp.concatenate([b[:d], a[d:] * b[:-d] + b[d:]], axis=0)
            a = jnp.concatenate([a[:d], a[d:] * a[:-d]], axis=0)
        d *= 2
    return a, b


def _scan_up(c, b, row, tm):
    d = 1
    while d < tm:
        if d < 8:
            keep = row < tm - d
            c_s = jnp.where(keep, pltpu.roll(c, tm - d, 0), 1.0)
            b_s = jnp.where(keep, pltpu.roll(b, tm - d, 0), 0.0)
            b = c * b_s + b
            c = c * c_s
        else:
            b = jnp.concatenate([c[:-d] * b[d:] + b[:-d], b[-d:]], axis=0)
            c = jnp.concatenate([c[:-d] * c[d:], c[-d:]], axis=0)
        d *= 2
    return c, b


def _rnn_fwd(xr, gr, conv_w, conv_b, wrg2, b_rg, wig2, b_ig, lam, n_seq, S, tm, phases=()):
    T = xr.shape[0]
    nt = S // tm
    W = D_MODEL

    def body(xr_ref, gr_ref, cw_ref, cb_ref, wrg_ref, brg_ref, wig_ref, big_ref, lam_ref,
             xc_ref, h_ref, ya_ref, px_ref, ph_ref):
        @pl.when(pl.program_id(1) == 0)
        def _():
            px_ref[...] = jnp.zeros_like(px_ref)
            ph_ref[...] = jnp.zeros_like(ph_ref)

        row = lax.broadcasted_iota(jnp.int32, (tm, W), 0)
        row8 = lax.broadcasted_iota(jnp.int32, (8, W), 0)
        x = xr_ref[...]
        xc = _conv_fwd(x, px_ref[...], cw_ref, cb_ref[...], row, row8, tm)
        sp, _ = _softplus_neg(lam_ref[...])
        _, r, i, a, mult = _lru_gates(xc, wrg_ref, brg_ref[...], wig_ref, big_ref[...], sp)
        bterm = mult * (i * xc)
        acum, hloc = _scan_down(a, bterm, row, tm)
        h = hloc + acum * ph_ref[7:8, :]
        h_ref[...] = h
        xc_ref[...] = xc
        gelu, _ = _gelu_and_grad(gr_ref[...])
        ya_ref[...] = (h * gelu).astype(BF16)
        px_ref[...] = xr_ref[tm - 8:tm, :]
        ph_ref[...] = h_ref[tm - 8:tm, :]

    tile = pl.BlockSpec((tm, W), lambda s, t: (s * nt + t, 0))
    return _call(
        body, phases=phases, name="rnn_fwd", grid=(n_seq, nt),
        in_specs=[tile, tile, _const((CONV_W, W)), _const((1, W)), _const((8, LANES, LANES)), _const((1, W)),
                  _const((8, LANES, LANES)), _const((1, W)), _const((1, W))],
        out_specs=[tile, tile, tile],
        out_shape=[jax.ShapeDtypeStruct((T, W), F32), jax.ShapeDtypeStruct((T, W), F32),
                   jax.ShapeDtypeStruct((T, W), BF16)],
        scratch_shapes=[pltpu.VMEM((8, W), F32), pltpu.VMEM((8, W), F32)],
    )(xr, gr, conv_w, conv_b, wrg2, b_rg, wig2, b_ig, lam)


def _rnn_bwd(dya, xr, gr, xc, h, conv_w, wrg2, b_rg, wig2, b_ig, lam, n_seq, S, tm, phases=()):
    T = xr.shape[0]
    nt = S // tm
    W = D_MODEL
    nb8 = tm // 8

    def body(dya_ref, xr_ref, gr_ref, xc_ref, h_ref, xprev_ref, hprev_ref, cw_ref, wrg_ref, brg_ref, wig_ref,
             big_ref, lam_ref, dxr_ref, dgr_ref, vec_ref, dwrg_ref, dwig_ref, cg_ref, ndxc_ref, tmp_ref):
        s, ti = pl.program_id(0), pl.program_id(1)

        @pl.when((s == 0) & (ti == 0))
        def _():
            vec_ref[...] = jnp.zeros_like(vec_ref)
            dwrg_ref[...] = jnp.zeros_like(dwrg_ref)
            dwig_ref[...] = jnp.zeros_like(dwig_ref)

        @pl.when(ti == 0)
        def _():
            cg_ref[...] = jnp.zeros_like(cg_ref)
            ndxc_ref[...] = jnp.zeros_like(ndxc_ref)

        first = ti == nt - 1
        row = lax.broadcasted_iota(jnp.int32, (tm, W), 0)
        row8 = lax.broadcasted_iota(jnp.int32, (8, W), 0)
        x = xr_ref[...]
        xc = xc_ref[...]
        hv = h_ref[...]
        xprev = jnp.where(first, 0.0, xprev_ref[...])
        hprev = jnp.where(first, 0.0, hprev_ref[...])
        sp, dsp_dlam = _softplus_neg(lam_ref[...])
        xcb, r, i, a, mult = _lru_gates(xc, wrg_ref, brg_ref[...], wig_ref, big_ref[...], sp)

        gelu, dgelu = _gelu_and_grad(gr_ref[...])
        dya_v = dya_ref[...]
        dgr_ref[...] = (dya_v * hv * dgelu).astype(BF16)
        dh = dya_v * gelu
        c = jnp.where(row < tm - 1, pltpu.roll(a, tm - 1, 0), 1.0)
        ccum, gloc = _scan_up(c, dh, row, tm)
        G = gloc + ccum * cg_ref[0:1, :]
        tmp_ref[...] = a * G
        cg_ref[...] = tmp_ref[0:8, :]

        h_m1 = _shift_down(hv, hprev, 1, row, row8, tm)
        ixc = i * xc
        dixc = G * mult
        dlog_a = (G * h_m1) * a - (G * ixc) * (a * a / mult)
        dr = dlog_a * ((-LRU_C) * sp)
        di = dixc * xc
        drg = dr * r * (1.0 - r)
        dig = di * i * (1.0 - i)
        vec_ref[7:8, :] += jnp.sum(dlog_a * ((-LRU_C) * r), axis=0, keepdims=True) * dsp_dlam
        vec_ref[5:6, :] += jnp.sum(drg, axis=0, keepdims=True)
        vec_ref[6:7, :] += jnp.sum(dig, axis=0, keepdims=True)
        drgb = drg.astype(BF16)
        digb = dig.astype(BF16)
        dxc = dixc * i + _blockdiag_dot(drgb, wrg_ref, True) + _blockdiag_dot(digb, wig_ref, True)
        for b in range(W // LANES):
            sl = slice(b * LANES, (b + 1) * LANES)
            dwrg_ref[b] += _dot_tn(xcb[:, sl], drgb[:, sl])
            dwig_ref[b] += _dot_tn(xcb[:, sl], digb[:, sl])

        vec_ref[4:5, :] += jnp.sum(dxc, axis=0, keepdims=True)
        vec_ref[3:4, :] += jnp.sum(dxc * x, axis=0, keepdims=True)
        dxr = cw_ref[CONV_W - 1:CONV_W, :] * dxc
        nxt = ndxc_ref[...]
        for sft in range(1, CONV_W):
            j = CONV_W - 1 - sft
            vec_ref[j:j + 1, :] += jnp.sum(dxc * _shift_down(x, xprev, sft, row, row8, tm), axis=0, keepdims=True)
            dxr = dxr + cw_ref[j:j + 1, :] * _shift_up(dxc, nxt, sft, row8, tm)
        dxr_ref[...] = dxr.astype(BF16)
        tmp_ref[...] = dxc
        ndxc_ref[...] = tmp_ref[0:8, :]

    rev = lambda s, t: (s * nt + nt - 1 - t, 0)
    tile = pl.BlockSpec((tm, W), rev)
    prev8 = pl.BlockSpec((8, W), lambda s, t: (jnp.maximum((s * nt + nt - 1 - t) * nb8 - 1, 0), 0))
    return _call(
        body, phases=phases, name="rnn_bwd", grid=(n_seq, nt),
        in_specs=[tile, tile, tile, tile, tile, prev8, prev8, _const((CONV_W, W)), _const((8, LANES, LANES)),
                  _const((1, W)), _const((8, LANES, LANES)), _const((1, W)), _const((1, W))],
        out_specs=[tile, tile, _const((16, W)), _const((8, LANES, LANES)), _const((8, LANES, LANES))],
        out_shape=[jax.ShapeDtypeStruct((T, W), BF16), jax.ShapeDtypeStruct((T, W), BF16),
                   jax.ShapeDtypeStruct((16, W), F32), jax.ShapeDtypeStruct((8, LANES, LANES), F32),
                   jax.ShapeDtypeStruct((8, LANES, LANES), F32)],
        scratch_shapes=[pltpu.VMEM((8, W), F32), pltpu.VMEM((8, W), F32), pltpu.VMEM((tm, W), F32)],
    )(dya, xr, gr, xc, h, xr, h, conv_w, wrg2, b_rg, wig2, b_ig, lam)


def _head_swap(t, lane):
    w = t.shape[1]
    return jnp.where(lane % HEAD_DIM < HEAD_DIM // 2, pltpu.roll(t, w - HEAD_DIM // 2, 1),
                     pltpu.roll(t, HEAD_DIM // 2, 1))


def _qk_prep(t, gain, cosf, sins, ind, indt, lane):
    ms = _split_dot(t * t, ind) * (1.0 / HEAD_DIM)
    rstd = _split_dot(lax.rsqrt(ms + NORM_EPS), indt)
    tn = (t * rstd) * gain
    return tn * cosf + _head_swap(tn, lane) * sins, rstd


def _qk_prep_bwd(dy, t, rstd, gain, cosf, sins, ind, indt, lane):
    dtn = dy * cosf + _head_swap(dy * sins, lane)
    dgain = jnp.sum(dtn * (t * rstd), axis=0, keepdims=True)
    dn = dtn * gain
    m = _split_dot(_split_dot(dn * t, ind), indt) * (1.0 / HEAD_DIM)
    return rstd * dn - t * (rstd * rstd * rstd * m), dgain


def _attn_mask_t(blk_idx):
    ci = lax.broadcasted_iota(jnp.int32, (2 * WINDOW, WINDOW), 0)
    qi = lax.broadcasted_iota(jnp.int32, (2 * WINDOW, WINDOW), 1)
    diff = WINDOW + qi - ci
    return (diff >= 0) & (diff < WINDOW) & ((ci >= WINDOW) | (blk_idx > 0))


def _stack_heads(t, kvh, lo):
    parts = []
    for i in (2 * kvh, 2 * kvh + 1):
        tp = t[:, i * LANES:(i + 1) * LANES]
        parts += [jnp.where(lo, tp, 0.0), jnp.where(lo, 0.0, tp)]
    return jnp.concatenate(parts, axis=0).astype(BF16)


def _unstack_heads(ts, lo):
    w = WINDOW
    return jnp.where(lo, ts[0:w], ts[w:2 * w]), jnp.where(lo, ts[2 * w:3 * w], ts[3 * w:4 * w])


def _dup_head(t, kvh, lo2):
    m = kvh // 2
    t2 = t[:, m * LANES:(m + 1) * LANES]
    t2r = pltpu.roll(t2, HEAD_DIM, 1)
    return (jnp.where(lo2, t2, t2r) if kvh % 2 == 0 else jnp.where(lo2, t2r, t2)).astype(BF16)


def _fold_head(ts, kvh, lo2):
    tot = ts + pltpu.roll(ts, HEAD_DIM, 1)
    own = lo2 if kvh % 2 == 0 else ~lo2
    return jnp.where(own, tot, 0.0)


KEY_CHUNKS = tuple(slice(i * 64, (i + 1) * 64) for i in range(2 * WINDOW // 64))


def _fold8(x, op):
    return op(x.reshape(x.shape[0] // 8, 8, x.shape[1]), axis=0)


def _softmax_stats(s_ref, b, cols, sink):
    m8 = None
    for c in KEY_CHUNKS:
        t = _fold8(s_ref[b, c, cols], jnp.max)
        m8 = t if m8 is None else jnp.maximum(m8, t)
    mx = jnp.maximum(jnp.max(m8, axis=0, keepdims=True), sink)
    d8 = None
    for c in KEY_CHUNKS:
        t = _fold8(jnp.exp(s_ref[b, c, cols] - mx), jnp.sum)
        d8 = t if d8 is None else d8 + t
    es = jnp.exp(sink - mx)
    inv = 1.0 / (jnp.sum(d8, axis=0, keepdims=True) + es)
    return mx, inv, es * inv


def _attn_fwd(q, k, v, qg, kg, sinks, cosf, sins, ind_q, ind_qt, ind_k, ind_kt, n_seq, S, phases=()):
    T = q.shape[0]
    nblk = S // WINDOW
    W = D_MODEL

    def body(sink_ref, q_ref, k_ref, v_ref, qg_ref, kg_ref, cos_ref, sin_ref, iq_ref, iqt_ref, ik_ref, ikt_ref,
             o_ref, kc_ref, vc_ref, s_ref, p_ref, qs_ref, kd_ref, vd_ref):
        n = pl.program_id(1)

        @pl.when(n == 0)
        def _():
            kc_ref[...] = jnp.zeros_like(kc_ref)
            vc_ref[...] = jnp.zeros_like(vc_ref)

        lane = lax.broadcasted_iota(jnp.int32, (WINDOW, W), 1)
        lo = lane[:, :LANES] < HEAD_DIM
        lo2 = lax.broadcasted_iota(jnp.int32, (2 * WINDOW, LANES), 1) < HEAD_DIM
        cosf, sinv = jnp.tile(cos_ref[...], (1, W // LANES)), jnp.tile(sin_ref[...], (1, W // LANES))
        qr, _ = _qk_prep(q_ref[...], qg_ref[...], cosf, sinv, iq_ref[...], iqt_ref[...], lane)
        kr, _ = _qk_prep(k_ref[...], kg_ref[...], cosf[:, :KV_W], sinv[:, :KV_W], ik_ref[...], ikt_ref[...],
                         lane[:, :KV_W])
        kc_ref[WINDOW:2 * WINDOW, :] = kr
        vc_ref[WINDOW:2 * WINDOW, :] = v_ref[...]
        kc, vc = kc_ref[...], vc_ref[...]
        mask = jnp.tile(_attn_mask_t(n), (1, 4))
        qr = qr * HEAD_DIM ** -0.5
        for kvh in range(N_KV):
            qs_ref[kvh] = _stack_heads(qr, kvh, lo)
            kd_ref[kvh] = _dup_head(kc, kvh, lo2)
            vd_ref[kvh] = _dup_head(vc, kvh, lo2)

        def scores(kvh):
            s_ref[kvh % 2] = jnp.where(mask, _dot_nt(kd_ref[kvh], qs_ref[kvh]), -1e30)

        def softmax(kvh):
            b = kvh % 2
            for r in range(4):
                cols = slice(r * WINDOW, (r + 1) * WINDOW)
                mx, inv, _ = _softmax_stats(s_ref, b, cols, sink_ref[4 * kvh + r])
                for c in KEY_CHUNKS:
                    p_ref[b, c, cols] = (jnp.exp(s_ref[b, c, cols] - mx) * inv).astype(BF16)

        def output(kvh):
            o0, o1 = _unstack_heads(_dot_tn(p_ref[kvh % 2], vd_ref[kvh]), lo)
            o_ref[:, (2 * kvh) * LANES:(2 * kvh + 1) * LANES] = o0.astype(BF16)
            o_ref[:, (2 * kvh + 1) * LANES:(2 * kvh + 2) * LANES] = o1.astype(BF16)

        scores(0)
        for kvh in range(N_KV):
            if kvh + 1 < N_KV:
                scores(kvh + 1)
            softmax(kvh)
            output(kvh)
        kc_ref[0:WINDOW, :] = kr
        vc_ref[0:WINDOW, :] = v_ref[...]

    blk = lambda w: pl.BlockSpec((WINDOW, w), lambda s, n: (s * nblk + n, 0))
    pos = pl.BlockSpec((WINDOW, LANES), lambda s, n: (n, 0))
    outs, extra = _call(
        body, phases=phases, name="attn_fwd", grid=(n_seq, nblk),
        in_specs=[pl.BlockSpec(memory_space=pltpu.SMEM), blk(W), blk(KV_W), blk(KV_W), _const((1, W)),
                  _const((1, KV_W)), pos, pos, _const((W, LANES)), _const((LANES, W)), _const((KV_W, LANES)),
                  _const((LANES, KV_W))],
        out_specs=blk(W), out_shape=jax.ShapeDtypeStruct((T, W), BF16),
        scratch_shapes=[pltpu.VMEM((2 * WINDOW, KV_W), F32), pltpu.VMEM((2 * WINDOW, KV_W), F32),
                        pltpu.VMEM((2, 2 * WINDOW, 4 * WINDOW), F32), pltpu.VMEM((2, 2 * WINDOW, 4 * WINDOW), BF16),
                        pltpu.VMEM((N_KV, 4 * WINDOW, LANES), BF16), pltpu.VMEM((N_KV, 2 * WINDOW, LANES), BF16),
                        pltpu.VMEM((N_KV, 2 * WINDOW, LANES), BF16)],
    )(sinks, q, k, v, qg, kg, cosf, sins, ind_q, ind_qt, ind_k, ind_kt)
    return outs[0], extra


def _attn_bwd(do, q, k, v, qg, kg, sinks, cosf, sins, ind_q, ind_qt, ind_k, ind_kt, n_seq, S, phases=()):
    T = q.shape[0]
    nblk = S // WINDOW
    W = D_MODEL

    def body(sink_ref, do_ref, q_ref, k_ref, v_ref, qg_ref, kg_ref, cos_ref, sin_ref, iq_ref, iqt_ref, ik_ref,
             ikt_ref, dq_ref, dkc_ref, dkp_ref, dvc_ref, dvp_ref, dqg_ref, dsk_ref, kc_ref, vc_ref, dqr_ref,
             dk_ref, dv_ref, s_ref, dp_ref, p_ref, ds_ref, qs_ref, dos_ref, kd_ref, vd_ref):
        s_id, n = pl.program_id(0), pl.program_id(1)

        @pl.when((s_id == 0) & (n == 0))
        def _():
            dqg_ref[...] = jnp.zeros_like(dqg_ref)
            dsk_ref[...] = jnp.zeros_like(dsk_ref)

        @pl.when(n == 0)
        def _():
            kc_ref[...] = jnp.zeros_like(kc_ref)
            vc_ref[...] = jnp.zeros_like(vc_ref)

        lane = lax.broadcasted_iota(jnp.int32, (WINDOW, W), 1)
        lane_k = lane[:, :KV_W]
        lane128 = lane[:, :LANES]
        cosf, sinv = jnp.tile(cos_ref[...], (1, W // LANES)), jnp.tile(sin_ref[...], (1, W // LANES))
        qv = q_ref[...]
        qr, q_rstd = _qk_prep(qv, qg_ref[...], cosf, sinv, iq_ref[...], iqt_ref[...], lane)
        kr, _ = _qk_prep(k_ref[...], kg_ref[...], cosf[:, :KV_W], sinv[:, :KV_W], ik_ref[...], ikt_ref[...], lane_k)
        kc_ref[WINDOW:2 * WINDOW, :] = kr
        vc_ref[WINDOW:2 * WINDOW, :] = v_ref[...]
        kc, vc = kc_ref[...], vc_ref[...]
        dov = do_ref[...]
        mask = jnp.tile(_attn_mask_t(n), (1, 4))
        lo = lane128 < HEAD_DIM
        lo2 = lax.broadcasted_iota(jnp.int32, (2 * WINDOW, LANES), 1) < HEAD_DIM
        scale = HEAD_DIM ** -0.5
        qr = qr * scale
        dk_ref[...] = jnp.zeros_like(dk_ref)
        dv_ref[...] = jnp.zeros_like(dv_ref)
        for kvh in range(N_KV):
            qs_ref[kvh] = _stack_heads(qr, kvh, lo)
            dos_ref[kvh] = _stack_heads(dov, kvh, lo)
            kd_ref[kvh] = _dup_head(kc, kvh, lo2)
            vd_ref[kvh] = _dup_head(vc, kvh, lo2)

        def scores(kvh):
            b = kvh % 2
            s_ref[b] = jnp.where(mask, _dot_nt(kd_ref[kvh], qs_ref[kvh]), -1e30)
            dp_ref[b] = _dot_nt(vd_ref[kvh], dos_ref[kvh])

        def softmax(kvh):
            b = kvh % 2
            for r in range(4):
                cols = slice(r * WINDOW, (r + 1) * WINDOW)
                head = 4 * kvh + r
                mx, inv, ps = _softmax_stats(s_ref, b, cols, sink_ref[head])
                g8 = None
                for c in KEY_CHUNKS:
                    t = _fold8(jnp.exp(s_ref[b, c, cols] - mx) * dp_ref[b, c, cols], jnp.sum)
                    g8 = t if g8 is None else g8 + t
                dd = jnp.sum(g8, axis=0, keepdims=True) * inv
                for c in KEY_CHUNKS:
                    p = jnp.exp(s_ref[b, c, cols] - mx) * inv
                    p_ref[b, c, cols] = p.astype(BF16)
                    ds_ref[b, c, cols] = (p * (dp_ref[b, c, cols] - dd)).astype(BF16)
                dsk_ref[head:head + 1, :] -= ps * dd

        def grads(kvh):
            m, b = kvh // 2, kvh % 2
            dq0, dq1 = _unstack_heads(_dot_tn(ds_ref[b], kd_ref[kvh]) * scale, lo)
            dqr_ref[:, (2 * kvh) * LANES:(2 * kvh + 1) * LANES] = dq0
            dqr_ref[:, (2 * kvh + 1) * LANES:(2 * kvh + 2) * LANES] = dq1
            dk_ref[:, m * LANES:(m + 1) * LANES] += _fold_head(_dot(ds_ref[b], qs_ref[kvh]), kvh, lo2)
            dv_ref[:, m * LANES:(m + 1) * LANES] += _fold_head(_dot(p_ref[b], dos_ref[kvh]), kvh, lo2)

        scores(0)
        for kvh in range(N_KV):
            if kvh + 1 < N_KV:
                scores(kvh + 1)
            softmax(kvh)
            grads(kvh)
        dq, dqg = _qk_prep_bwd(dqr_ref[...], qv, q_rstd, qg_ref[...], cosf, sinv, iq_ref[...], iqt_ref[...], lane)
        dq_ref[...] = dq.astype(BF16)
        dqg_ref[...] += dqg
        dkp_ref[...] = dk_ref[0:WINDOW, :]
        dkc_ref[...] = dk_ref[WINDOW:2 * WINDOW, :]
        dvp_ref[...] = dv_ref[0:WINDOW, :]
        dvc_ref[...] = dv_ref[WINDOW:2 * WINDOW, :]
        kc_ref[0:WINDOW, :] = kr
        vc_ref[0:WINDOW, :] = v_ref[...]

    blk = lambda w: pl.BlockSpec((WINDOW, w), lambda s, n: (s * nblk + n, 0))
    pos = pl.BlockSpec((WINDOW, LANES), lambda s, n: (n, 0))
    kv_out = jax.ShapeDtypeStruct((T, KV_W), F32)
    stage = lambda dt: pltpu.VMEM((2, 2 * WINDOW, 4 * WINDOW), dt)
    return _call(
        body, phases=phases, name="attn_bwd", grid=(n_seq, nblk),
        in_specs=[pl.BlockSpec(memory_space=pltpu.SMEM), blk(W), blk(W), blk(KV_W), blk(KV_W), _const((1, W)),
                  _const((1, KV_W)), pos, pos, _const((W, LANES)), _const((LANES, W)), _const((KV_W, LANES)),
                  _const((LANES, KV_W))],
        out_specs=[blk(W), blk(KV_W), blk(KV_W), blk(KV_W), blk(KV_W), _const((1, W)), _const((N_HEADS, LANES))],
        out_shape=[jax.ShapeDtypeStruct((T, W), BF16), kv_out, kv_out, kv_out, kv_out,
                   jax.ShapeDtypeStruct((1, W), F32), jax.ShapeDtypeStruct((N_HEADS, LANES), F32)],
        scratch_shapes=[pltpu.VMEM((2 * WINDOW, KV_W), F32), pltpu.VMEM((2 * WINDOW, KV_W), F32),
                        pltpu.VMEM((WINDOW, W), F32), pltpu.VMEM((2 * WINDOW, KV_W), F32),
                        pltpu.VMEM((2 * WINDOW, KV_W), F32), stage(F32), stage(F32), stage(BF16), stage(BF16),
                        pltpu.VMEM((N_KV, 4 * WINDOW, LANES), BF16), pltpu.VMEM((N_KV, 4 * WINDOW, LANES), BF16),
                        pltpu.VMEM((N_KV, 2 * WINDOW, LANES), BF16), pltpu.VMEM((N_KV, 2 * WINDOW, LANES), BF16)],
    )(sinks, do, q, k, v, qg, kg, cosf, sins, ind_q, ind_qt, ind_k, ind_kt)


def _kv_bwd(dkc, dkp, dvc, dvp, k, kg, cosf, sins, ind_k, ind_kt, n_seq, S, phases=()):
    T = k.shape[0]
    nblk = S // WINDOW

    def body(dkc_ref, dkp_ref, dvc_ref, dvp_ref, k_ref, kg_ref, cos_ref, sin_ref, ik_ref, ikt_ref,
             dk_ref, dv_ref, dkg_ref):
        s_id, n = pl.program_id(0), pl.program_id(1)

        @pl.when((s_id == 0) & (n == 0))
        def _():
            dkg_ref[...] = jnp.zeros_like(dkg_ref)

        has_next = n < nblk - 1
        lane = lax.broadcasted_iota(jnp.int32, (WINDOW, KV_W), 1)
        dkr = dkc_ref[...] + jnp.where(has_next, dkp_ref[...], 0.0)
        dv_ref[...] = (dvc_ref[...] + jnp.where(has_next, dvp_ref[...], 0.0)).astype(BF16)
        cosf, sinv = jnp.tile(cos_ref[...], (1, KV_W // LANES)), jnp.tile(sin_ref[...], (1, KV_W // LANES))
        kv = k_ref[...]
        _, rstd = _qk_prep(kv, kg_ref[...], cosf, sinv, ik_ref[...], ikt_ref[...], lane)
        dk, dkg = _qk_prep_bwd(dkr, kv, rstd, kg_ref[...], cosf, sinv, ik_ref[...], ikt_ref[...], lane)
        dk_ref[...] = dk.astype(BF16)
        dkg_ref[...] += dkg

    cur = pl.BlockSpec((WINDOW, KV_W), lambda s, n: (s * nblk + n, 0))
    nxt = pl.BlockSpec((WINDOW, KV_W), lambda s, n: (s * nblk + jnp.minimum(n + 1, nblk - 1), 0))
    pos = pl.BlockSpec((WINDOW, LANES), lambda s, n: (n, 0))
    return _call(
        body, phases=phases, name="kv_bwd", grid=(n_seq, nblk),
        in_specs=[cur, nxt, cur, nxt, cur, _const((1, KV_W)), pos, pos, _const((KV_W, LANES)),
                  _const((LANES, KV_W))],
        out_specs=[cur, cur, _const((1, KV_W))],
        out_shape=[jax.ShapeDtypeStruct((T, KV_W), BF16), jax.ShapeDtypeStruct((T, KV_W), BF16),
                   jax.ShapeDtypeStruct((1, KV_W), F32)],
    )(dkc, dkp, dvc, dvp, k, kg, cosf, sins, ind_k, ind_kt)


def _merge_fwd(x, ya, o, ga, gb, w_rnn, w_attn, w_out, tm, phases=()):
    T = x.shape[0]
    W = D_MODEL

    def body(x_ref, ya_ref, o_ref, ga_ref, gb_ref, wr_ref, wa_ref, wo_ref, x1_ref, mg_ref, yao_ref, ybo_ref):
        y_a = _dot(ya_ref[...], wr_ref[...])
        y_b = _dot(o_ref[...], wa_ref[...])
        yao_ref[...] = y_a
        ybo_ref[...] = y_b
        mg = (_sigmoid(ga_ref[...]) * y_a + _sigmoid(gb_ref[...]) * y_b).astype(BF16)
        mg_ref[...] = mg
        x1_ref[...] = x_ref[...] + _dot(mg, wo_ref[...])

    row = pl.BlockSpec((tm, W), lambda i: (i, 0))
    sq = _const((W, W))
    return _call(
        body, phases=phases, name="merge_fwd", grid=(T // tm,),
        in_specs=[row, row, row, row, row, sq, sq, sq], out_specs=[row, row, row, row],
        out_shape=[jax.ShapeDtypeStruct((T, W), F32), jax.ShapeDtypeStruct((T, W), BF16),
                   jax.ShapeDtypeStruct((T, W), F32), jax.ShapeDtypeStruct((T, W), F32)],
    )(x, ya, o, ga, gb, w_rnn, w_attn, w_out)


def _merge_bwd(dx1, ga, gb, y_a, y_b, w_rnn, w_attn, w_out, tm, phases=()):
    T = dx1.shape[0]
    W = D_MODEL

    def body(dx1_ref, ga_ref, gb_ref, ya_ref, yb_ref, wr_ref, wa_ref, wo_ref,
             dga_ref, dgb_ref, dya_ref, dyb_ref, dyain_ref, do_ref):
        dm = _dot_nt(dx1_ref[...].astype(BF16), wo_ref[...])
        sa = _sigmoid(ga_ref[...])
        sb = _sigmoid(gb_ref[...])
        dga_ref[...] = (dm * ya_ref[...] * (sa * (1.0 - sa))).astype(BF16)
        dgb_ref[...] = (dm * yb_ref[...] * (sb * (1.0 - sb))).astype(BF16)
        dya = (dm * sa).astype(BF16)
        dyb = (dm * sb).astype(BF16)
        dya_ref[...] = dya
        dyb_ref[...] = dyb
        dyain_ref[...] = _dot_nt(dya, wr_ref[...])
        do_ref[...] = _dot_nt(dyb, wa_ref[...])

    row = pl.BlockSpec((tm, W), lambda i: (i, 0))
    sq = _const((W, W))
    b16 = jax.ShapeDtypeStruct((T, W), BF16)
    f32 = jax.ShapeDtypeStruct((T, W), F32)
    return _call(
        body, phases=phases, name="merge_bwd", grid=(T // tm,),
        in_specs=[row, row, row, row, row, sq, sq, sq], out_specs=[row] * 6,
        out_shape=[b16, b16, b16, b16, f32, f32],
    )(dx1, ga, gb, y_a, y_b, w_rnn, w_attn, w_out)


def _mlp_fwd(x1, g_mlp, w_up, w_down, tm, phases=()):
    T = x1.shape[0]
    W = D_MODEL

    def body(x_ref, g_ref, wu_ref, wd_ref, x2_ref, hm_ref, u_ref, act_ref):
        xv = x_ref[...]
        hm, _ = _rms_fwd(xv, g_ref[...])
        hmb = hm.astype(BF16)
        hm_ref[...] = hmb
        for j in range(N_CHIPS):
            u = _dot(hmb, wu_ref[j])
            u_ref[:, j * W:(j + 1) * W] = u
            ru = jnp.maximum(u, 0.0)
            act_ref[:, j * W:(j + 1) * W] = (ru * ru).astype(BF16)
        x2_ref[...] = xv + _dot(act_ref[...], wd_ref[...])

    row = lambda w: pl.BlockSpec((tm, w), lambda i: (i, 0))
    return _call(
        body, phases=phases, name="mlp_fwd", grid=(T // tm,),
        in_specs=[row(W), _const((1, W)), _const((N_CHIPS, W, W)), _const((D_FF, W))],
        out_specs=[row(W), row(W), row(D_FF), row(D_FF)],
        out_shape=[jax.ShapeDtypeStruct((T, W), F32), jax.ShapeDtypeStruct((T, W), BF16),
                   jax.ShapeDtypeStruct((T, D_FF), F32), jax.ShapeDtypeStruct((T, D_FF), BF16)],
    )(x1, g_mlp, w_up, w_down)


def _mlp_bwd(dx2, u, x1, g_mlp, w_up, w_down, tm, phases=()):
    T = x1.shape[0]
    W = D_MODEL

    def body(dx2_ref, u_ref, x_ref, g_ref, wu_ref, wd_ref, dx1_ref, du_ref, dg_ref):
        @pl.when(pl.program_id(0) == 0)
        def _():
            dg_ref[...] = jnp.zeros_like(dg_ref)

        dx2 = dx2_ref[...]
        dact = _dot_nt(dx2.astype(BF16), wd_ref[...])
        du_ref[...] = (dact * (2.0 * jnp.maximum(u_ref[...], 0.0))).astype(BF16)
        dhm = jnp.zeros((tm, W), F32)
        for j in range(N_CHIPS):
            dhm = dhm + _dot_nt(du_ref[:, j * W:(j + 1) * W], wu_ref[j])
        xv = x_ref[...]
        g = g_ref[...]
        _, r = _rms_fwd(xv, g)
        dx, dg = _rms_bwd(dhm, xv, r, g)
        dx1_ref[...] = dx2 + dx
        dg_ref[...] += dg

    row = lambda w: pl.BlockSpec((tm, w), lambda i: (i, 0))
    return _call(
        body, phases=phases, name="mlp_bwd", grid=(T // tm,),
        in_specs=[row(W), row(D_FF), row(W), _const((1, W)), _const((N_CHIPS, W, W)), _const((D_FF, W))],
        out_specs=[row(W), row(D_FF), _const((1, W))],
        out_shape=[jax.ShapeDtypeStruct((T, W), F32), jax.ShapeDtypeStruct((T, D_FF), BF16),
                   jax.ShapeDtypeStruct((1, W), F32)],
    )(dx2, u, x1, g_mlp, w_up, w_down)


def _ple_loss(x2, p, target, g_ple, w_gate, w_proj, tm, phases=()):
    T = x2.shape[0]
    W = D_MODEL
    cw = W // N_CHIPS

    def body(x_ref, p_ref, t_ref, g_ref, wg_ref, wp_ref, loss_ref, dx2_ref, pb_ref, de_ref, hp_ref, dtg_ref, dg_ref):
        @pl.when(pl.program_id(0) == 0)
        def _():
            dg_ref[...] = jnp.zeros_like(dg_ref)
            loss_ref[...] = jnp.zeros_like(loss_ref)

        xv = x_ref[...]
        g = g_ref[...]
        pb = p_ref[...].astype(BF16)
        pb_ref[...] = pb
        e = jnp.concatenate([_dot(pb, wp_ref[j]) for j in range(N_CHIPS)], axis=1)
        hp, r = _rms_fwd(xv, g)
        hpb = hp.astype(BF16)
        hp_ref[...] = hpb
        sg = _sigmoid(_dot(hpb, wg_ref[...]))
        diff = (xv + e * sg) - t_ref[...]
        loss_ref[...] += jnp.sum(diff * diff) * (0.5 / W)
        dx3 = diff * (1.0 / W)
        de_ref[...] = (dx3 * sg).astype(BF16)
        dtg = (dx3 * e * (sg * (1.0 - sg))).astype(BF16)
        dtg_ref[...] = dtg
        dx, dg = _rms_bwd(_dot_nt(dtg, wg_ref[...]), xv, r, g)
        dx2_ref[...] = dx3 + dx
        dg_ref[...] += dg

    row = lambda w: pl.BlockSpec((tm, w), lambda i: (i, 0))
    b16 = lambda w: jax.ShapeDtypeStruct((T, w), BF16)
    return _call(
        body, phases=phases, name="ple_loss", grid=(T // tm,),
        in_specs=[row(W), row(PLE_DIM), row(W), _const((1, W)), _const((W, W)), _const((N_CHIPS, PLE_DIM, cw))],
        out_specs=[_const((8, LANES)), row(W), row(PLE_DIM), row(W), row(W), row(W), _const((1, W))],
        out_shape=[jax.ShapeDtypeStruct((8, LANES), F32), jax.ShapeDtypeStruct((T, W), F32), b16(PLE_DIM),
                   b16(W), b16(W), b16(W), jax.ShapeDtypeStruct((1, W), F32)],
    )(x2, p, target, g_ple, w_gate, w_proj)


def _adamw(w, g, m, v, name, tr, phases=()):
    R, C = w.shape
    c1 = 1.0 / (1.0 - ADAM_B1 ** ADAM_STEP)
    c2 = 1.0 / (1.0 - ADAM_B2 ** ADAM_STEP)

    def body(w_ref, g_ref, m_ref, v_ref, go_ref, d_ref, nm_ref, nv_ref):
        gv = g_ref[...]
        go_ref[...] = gv
        nm = ADAM_B1 * m_ref[...] + (1.0 - ADAM_B1) * gv
        nv = ADAM_B2 * v_ref[...] + (1.0 - ADAM_B2) * (gv * gv)
        nm_ref[...] = nm
        nv_ref[...] = nv
        d_ref[...] = (-ADAM_LR) * ((nm * c1) / (jnp.sqrt(nv * c2) + ADAM_EPS) + ADAM_WD * w_ref[...])

    row = pl.BlockSpec((tr, C), lambda i: (i, 0))
    sds = jax.ShapeDtypeStruct((R, C), F32)
    return _call(
        body, phases=phases, name=name, grid=(R // tr,), in_specs=[row] * 4, out_specs=[row] * 4,
        out_shape=[sds] * 4,
    )(w, g, m, v)


def _indicator(width):
    ind = np.zeros((width, LANES), np.float32)
    ind[np.arange(width), np.arange(width) // HEAD_DIM] = 1.0
    return jnp.asarray(ind, BF16), jnp.asarray(ind.T, BF16)


def _rope_tables(S):
    inv = ROPE_THETA ** (-jnp.arange(0, HEAD_DIM, 2, dtype=F32) / HEAD_DIM)
    ang = jnp.arange(S, dtype=F32)[:, None] * inv[None, :]
    cos, sin = jnp.cos(ang), jnp.sin(ang)
    cosf = jnp.tile(jnp.concatenate([cos, cos], axis=1), (1, LANES // HEAD_DIM))
    sins = jnp.tile(jnp.concatenate([-sin, sin], axis=1), (1, LANES // HEAD_DIM))
    return cosf, sins


def _pair_blockdiag(w):
    w4 = w.reshape(8, 2, HEAD_DIM, HEAD_DIM)
    eye = jnp.eye(2, dtype=w.dtype)
    return jnp.einsum("bpij,pq->bpiqj", w4, eye).reshape(8, LANES, LANES)


def _pair_blockdiag_extract(g):
    g5 = g.reshape(8, 2, HEAD_DIM, 2, HEAD_DIM)
    return jnp.stack([g5[:, 0, :, 0, :], g5[:, 1, :, 1, :]], axis=1).reshape(16, HEAD_DIM, HEAD_DIM)


def _pair_sum(parts, sibs, name):
    n = len(parts)
    dims = [(p.shape[1] // 2, p.shape[2]) for p in parts]

    def body(*refs):
        p_r, s_r, send_r, own_r, mine_r, sem = (refs[0:n], refs[n:2 * n], refs[2 * n:3 * n], refs[3 * n:4 * n],
                                                refs[4 * n:5 * n], refs[5 * n])
        x, y, c, chips = _mesh_pos()
        me = 2 * x + y
        loads = []
        for i, (R, _) in enumerate(dims):
            mine, _ = _half_rows(c, R)
            cp = pltpu.make_async_copy(p_r[i].at[:, mine, :], mine_r[i], sem.at[i])
            cp.start()
            loads.append(cp)
        for i in range(n):
            loads[i].wait()
            for j, (cx, cy) in enumerate(chips):
                k = 2 * cx + cy
                send_r[i][j] = (mine_r[i][k] + s_r[i][k]).astype(BF16)
            own_r[i][...] = mine_r[i][me] + s_r[i][me]

    vm = pl.BlockSpec(memory_space=pltpu.VMEM)
    out = pl.pallas_call(
        body, name=name, in_specs=[pl.BlockSpec(memory_space=pl.ANY)] * n + [vm] * n, out_specs=[vm] * (2 * n),
        out_shape=[jax.ShapeDtypeStruct((3, R, C), BF16) for R, C in dims]
        + [jax.ShapeDtypeStruct((R, C), F32) for R, C in dims],
        scratch_shapes=[pltpu.VMEM((N_CHIPS, R, C), F32) for R, C in dims] + [pltpu.SemaphoreType.DMA((n,))],
        compiler_params=pltpu.CompilerParams(vmem_limit_bytes=VMEM_LIMIT),
    )(*parts, *sibs)
    return out[:n], out[n:]


def _chip_sum(owns, recvs, name):
    n = len(owns)
    dims = [o.shape for o in owns]

    def body(*refs):
        own_r, recv_r, red_r, stage_r, sem = refs[0:n], refs[n:2 * n], refs[2 * n:3 * n], refs[3 * n:4 * n], refs[4 * n]
        x, y, c, _ = _mesh_pos()
        me = 2 * x + y
        stores = []
        for i, (R, _) in enumerate(dims):
            for k_me in range(N_CHIPS):

                @pl.when(me == k_me)
                def _():
                    acc = None
                    for k in range(N_CHIPS):
                        slot = ((k // 2) ^ (k_me // 2)) + 2 * ((k % 2) ^ (k_me % 2)) - 1
                        term = own_r[i][...] if k == k_me else recv_r[i][slot].astype(F32)
                        acc = term if acc is None else acc + term
                    stage_r[i][...] = acc

            mine, _ = _half_rows(c, R)
            cp = pltpu.make_async_copy(stage_r[i], red_r[i].at[mine, :], sem.at[i])
            cp.start()
            stores.append(cp)
        for cp in stores:
            cp.wait()

    vm = pl.BlockSpec(memory_space=pltpu.VMEM)
    return pl.pallas_call(
        body, name=name, in_specs=[vm] * (2 * n), out_specs=[pl.BlockSpec(memory_space=pl.ANY)] * n,
        out_shape=[jax.ShapeDtypeStruct((2 * R, C), F32) for R, C in dims],
        scratch_shapes=[pltpu.VMEM((R, C), F32) for R, C in dims] + [pltpu.SemaphoreType.DMA((n,))],
        compiler_params=pltpu.CompilerParams(vmem_limit_bytes=VMEM_LIMIT),
    )(*owns, *recvs)


def _gather_bf16(shard, name):
    R2, C = shard.shape
    R = R2 // 2

    def body(s_ref, o_ref, send_sems, recv_sems):
        x, y, c, chips = _mesh_pos()
        me = 2 * x + y
        mine = pl.ds(pl.multiple_of(c * R, R), R)
        theirs = pl.ds(pl.multiple_of((1 - c) * R, R), R)
        o_ref[me] = s_ref[...].astype(BF16)

        def copy(k, chip, rows, to):
            blk = o_ref.at[chip, rows]
            return pltpu.make_async_remote_copy(src_ref=blk, dst_ref=blk, send_sem=send_sems.at[k],
                                                recv_sem=recv_sems.at[k], device_id=to, device_id_type=MESH_ID)

        first = [copy(j, me, mine, (cx, cy, c)) for j, (cx, cy) in enumerate(chips)]
        for cp in first:
            cp.start()
        passed = []
        for j, (cx, cy) in enumerate(chips):
            copy(j, 2 * cx + cy, mine, (x, y, c)).wait_recv()
            cp = copy(3 + j, 2 * cx + cy, mine, (x, y, 1 - c))
            cp.start()
            passed.append(cp)
        for j, (cx, cy) in enumerate(chips):
            copy(3 + j, 2 * cx + cy, theirs, (x, y, c)).wait_recv()
        for cp in first + passed:
            cp.wait_send()

    return pl.pallas_call(
        body, name=name, out_shape=jax.ShapeDtypeStruct((N_CHIPS, R2, C), BF16),
        in_specs=[pl.BlockSpec(memory_space=pltpu.VMEM)], out_specs=pl.BlockSpec(memory_space=pltpu.VMEM),
        scratch_shapes=[pltpu.SemaphoreType.DMA((6,)), pltpu.SemaphoreType.DMA((6,))],
        compiler_params=pltpu.CompilerParams(vmem_limit_bytes=VMEM_LIMIT),
    )(shard)


def _pair_exchange_sum(partial, name):
    _, R2, C = partial.shape
    R = R2 // 2

    def body(p_ref, send_ref, own_ref, mine_ref, sib_ref, loc_sems, send_sems, recv_sems):
        x, y, c, chips = _mesh_pos()
        me = 2 * x + y
        mine, theirs = _half_rows(c, R)
        order = [2 * cx + cy for cx, cy in chips] + [me]
        locs, pairs = [], []
        for i, k in enumerate(order):
            loc = pltpu.make_async_copy(p_ref.at[k, mine, :], mine_ref.at[i], loc_sems.at[i])
            pair = _remote(p_ref.at[k, theirs, :], sib_ref.at[i], (send_sems.at[i], recv_sems.at[i]), (x, y, 1 - c))
            loc.start()
            pair.start()
            locs.append(loc)
            pairs.append(pair)
        for i in range(N_CHIPS):
            locs[i].wait()
            pairs[i].wait_recv()
            total = mine_ref[i] + sib_ref[i]
            if i < 3:
                send_ref[i] = total.astype(BF16)
            else:
                own_ref[...] = total
        for pair in pairs:
            pair.wait_send()

    vm = pl.BlockSpec(memory_space=pltpu.VMEM)
    return pl.pallas_call(
        body, name=name, in_specs=[pl.BlockSpec(memory_space=pl.ANY)], out_specs=[vm, vm],
        out_shape=[jax.ShapeDtypeStruct((3, R, C), BF16), jax.ShapeDtypeStruct((R, C), F32)],
        scratch_shapes=[pltpu.VMEM((N_CHIPS, R, C), F32), pltpu.VMEM((N_CHIPS, R, C), F32),
                        pltpu.SemaphoreType.DMA((N_CHIPS,)), pltpu.SemaphoreType.DMA((N_CHIPS,)),
                        pltpu.SemaphoreType.DMA((N_CHIPS,))],
        compiler_params=pltpu.CompilerParams(vmem_limit_bytes=VMEM_LIMIT),
    )(partial)


def _allreduce_small(buf, name):
    rows, width = buf.shape
    h = rows // 2

    def body(b_ref, o_ref, sib_ref, pair_ref, in_ref, pair_sems, send_sems, recv_sems, fin_sems):
        x, y, c, chips = _mesh_pos()
        me = 2 * x + y
        mine, theirs = _half_rows(c, h)
        sibling = (x, y, 1 - c)
        pair = _remote(b_ref.at[theirs], sib_ref, (pair_sems.at[0], pair_sems.at[1]), sibling)
        pair.start()
        pair.wait()
        pair_ref[...] = b_ref[mine, :] + sib_ref[...]
        sends = []
        for j, (cx, cy) in enumerate(chips):
            cp = _remote(pair_ref, in_ref.at[j], (send_sems.at[j], recv_sems.at[j]), (cx, cy, c))
            cp.start()
            sends.append(cp)
        for cp in sends:
            cp.wait_recv()
        acc = None
        for k in range(N_CHIPS):
            term = jnp.where(me == k, pair_ref[...], in_ref[_peer_slot(k, x, y)])
            acc = term if acc is None else acc + term
        o_ref[mine, :] = acc
        fin = _remote(o_ref.at[mine], o_ref.at[mine], (fin_sems.at[0], fin_sems.at[1]), sibling)
        fin.start()
        fin.wait_send()
        _remote(o_ref.at[theirs], o_ref.at[theirs], (fin_sems.at[0], fin_sems.at[1]), sibling).wait_recv()
        for cp in sends:
            cp.wait_send()

    return pl.pallas_call(
        body, name=name, out_shape=jax.ShapeDtypeStruct((rows, width), F32),
        in_specs=[pl.BlockSpec(memory_space=pltpu.VMEM)], out_specs=pl.BlockSpec(memory_space=pltpu.VMEM),
        scratch_shapes=[pltpu.VMEM((h, width), F32), pltpu.VMEM((h, width), F32), pltpu.VMEM((3, h, width), F32),
                        pltpu.SemaphoreType.DMA((2,)), pltpu.SemaphoreType.DMA((3,)), pltpu.SemaphoreType.DMA((3,)),
                        pltpu.SemaphoreType.DMA((2,))],
        compiler_params=pltpu.CompilerParams(vmem_limit_bytes=VMEM_LIMIT),
    )(buf)


def _adamw_small(ws, gs, ms, vs):
    n = len(ws)
    c1 = 1.0 / (1.0 - ADAM_B1 ** ADAM_STEP)
    c2 = 1.0 / (1.0 - ADAM_B2 ** ADAM_STEP)

    def body(*refs):
        w_r, g_r, m_r, v_r = refs[0:n], refs[n:2 * n], refs[2 * n:3 * n], refs[3 * n:4 * n]
        d_r, nm_r, nv_r = refs[4 * n:5 * n], refs[5 * n:6 * n], refs[6 * n:7 * n]
        for i in range(n):
            gv = g_r[i][...]
            nm = ADAM_B1 * m_r[i][...] + (1.0 - ADAM_B1) * gv
            nv = ADAM_B2 * v_r[i][...] + (1.0 - ADAM_B2) * (gv * gv)
            nm_r[i][...] = nm
            nv_r[i][...] = nv
            d_r[i][...] = (-ADAM_LR) * ((nm * c1) / (jnp.sqrt(nv * c2) + ADAM_EPS) + ADAM_WD * w_r[i][...])

    vm = pl.BlockSpec(memory_space=pltpu.VMEM)
    sds = [jax.ShapeDtypeStruct(w.shape, F32) for w in ws]
    out = pl.pallas_call(body, name="adamw_small", in_specs=[vm] * (4 * n), out_specs=[vm] * (3 * n),
                         out_shape=sds * 3)(*ws, *gs, *ms, *vs)
    return out[0:n], out[n:2 * n], out[2 * n:3 * n]


_BIG = ("w_in", "w_rnn_proj", "w_attn_proj", "w_out", "w_up", "w_down", "w_ple_gate", "w_ple_proj")
_SMALL = ("g_mix", "conv_w", "conv_b", "w_rg", "b_rg", "w_ig", "b_ig", "lru_lambda", "q_gain", "k_gain", "sinks",
          "g_mlp", "g_ple")
_WEIGHTS = ("g_mix", "w_in", "conv_w", "conv_b", "w_rg", "b_rg", "w_ig", "b_ig", "lru_lambda", "w_rnn_proj",
            "q_gain", "k_gain", "sinks", "w_attn_proj", "w_out", "g_mlp", "w_up", "w_down", "g_ple", "w_ple_gate",
            "w_ple_proj")


def _pad_row(v):
    v = v.reshape(1, -1)
    return jnp.pad(v, ((0, 0), (0, D_MODEL - v.shape[1])))


def kernel(x, p, g_mix, w_in, conv_w, conv_b, w_rg, b_rg, w_ig, b_ig, lru_lambda, w_rnn_proj, q_gain, k_gain, sinks, w_attn_proj, w_out, g_mlp, w_up, w_down, g_ple, w_ple_gate, w_ple_proj, loss_target, m_g_mix, m_w_in, m_conv_w, m_conv_b, m_w_rg, m_b_rg, m_w_ig, m_b_ig, m_lru_lambda, m_w_rnn_proj, m_q_gain, m_k_gain, m_sinks, m_w_attn_proj, m_w_out, m_g_mlp, m_w_up, m_w_down, m_g_ple, m_w_ple_gate, m_w_ple_proj, v_g_mix, v_w_in, v_conv_w, v_conv_b, v_w_rg, v_b_rg, v_w_ig, v_b_ig, v_lru_lambda, v_w_rnn_proj, v_q_gain, v_k_gain, v_sinks, v_w_attn_proj, v_w_out, v_g_mlp, v_w_up, v_w_down, v_g_ple, v_w_ple_gate, v_w_ple_proj):
    w = dict(g_mix=g_mix, w_in=w_in, conv_w=conv_w, conv_b=conv_b, w_rg=w_rg, b_rg=b_rg, w_ig=w_ig, b_ig=b_ig,
             lru_lambda=lru_lambda, w_rnn_proj=w_rnn_proj, q_gain=q_gain, k_gain=k_gain, sinks=sinks,
             w_attn_proj=w_attn_proj, w_out=w_out, g_mlp=g_mlp, w_up=w_up, w_down=w_down, g_ple=g_ple,
             w_ple_gate=w_ple_gate, w_ple_proj=w_ple_proj)
    m = dict(g_mix=m_g_mix, w_in=m_w_in, conv_w=m_conv_w, conv_b=m_conv_b, w_rg=m_w_rg, b_rg=m_b_rg, w_ig=m_w_ig,
             b_ig=m_b_ig, lru_lambda=m_lru_lambda, w_rnn_proj=m_w_rnn_proj, q_gain=m_q_gain, k_gain=m_k_gain,
             sinks=m_sinks, w_attn_proj=m_w_attn_proj, w_out=m_w_out, g_mlp=m_g_mlp, w_up=m_w_up, w_down=m_w_down,
             g_ple=m_g_ple, w_ple_gate=m_w_ple_gate, w_ple_proj=m_w_ple_proj)
    v = dict(g_mix=v_g_mix, w_in=v_w_in, conv_w=v_conv_w, conv_b=v_conv_b, w_rg=v_w_rg, b_rg=v_b_rg, w_ig=v_w_ig,
             b_ig=v_b_ig, lru_lambda=v_lru_lambda, w_rnn_proj=v_w_rnn_proj, q_gain=v_q_gain, k_gain=v_k_gain,
             sinks=v_sinks, w_attn_proj=v_w_attn_proj, w_out=v_w_out, g_mlp=v_g_mlp, w_up=v_w_up, w_down=v_w_down,
             g_ple=v_g_ple, w_ple_gate=v_w_ple_gate, w_ple_proj=v_w_ple_proj)
    n_seq, S, _ = x.shape
    T = n_seq * S
    chip = 2 * lax.axis_index("x") + lax.axis_index("y")

    tm, tm_rnn = 512, 256
    xf, pf, tf = x.reshape(T, D_MODEL), p.reshape(T, PLE_DIM), loss_target.reshape(T, D_MODEL)
    first = lambda outs: [o[0] for o in outs]

    w_in_g = _gather_bf16(w["w_in"][0], "gather_w_in")
    wb = {name: w[name][0].astype(BF16) for name in _BIG if name != "w_in"}
    grp_mix, grp_mlp, grp_ple = ("w_rnn_proj", "w_attn_proj", "w_out"), ("w_up", "w_down"), ("w_ple_gate", "w_ple_proj")

    cw_full = jnp.zeros((8, D_MODEL), F32)
    cw_full = lax.dynamic_update_slice(cw_full, conv_w[0], (0, chip * (D_MODEL // N_CHIPS)))
    cw_full = _allreduce_small(0.5 * cw_full.reshape(64, LANES), "allgather_conv_w").reshape(8, D_MODEL)[0:CONV_W]

    cosf, sins = _rope_tables(S)
    ind_q, ind_qt = _indicator(D_MODEL)
    ind_k, ind_kt = _indicator(KV_W)
    wrg2 = _pair_blockdiag(w_rg[0]).astype(BF16)
    wig2 = _pair_blockdiag(w_ig[0]).astype(BF16)
    qg = jnp.tile(q_gain, (1, N_HEADS))
    kg = jnp.tile(k_gain, (1, N_KV))
    sk = sinks.reshape(N_HEADS)
    rnn_w = (cw_full, conv_b, wrg2, b_rg, wig2, b_ig, lru_lambda)
    attn_c = (qg, kg, sk, cosf, sins, ind_q, ind_qt, ind_k, ind_kt, n_seq, S)

    (h0, xr, gr, zq, zk, zv, ga, gb), ph = _inproj_fwd(xf, g_mix, w_in_g, tm,
                                                     phases=[_ph_gather_send(wb[n]) for n in grp_mix + grp_ple])
    g_small = first(ph)
    o, ph = _attn_fwd(zq, zk, zv, *attn_c,
                      phases=[_ph_gather_pass(g) for g in g_small] + [_ph_gather_send(wb["w_up"])])
    g_small, wu = first(ph[:5]), ph[5][0]
    (xc, h, ya), ph = _rnn_fwd(xr, gr, *rnn_w, n_seq, S, tm_rnn,
                               phases=[_ph_gather_pass(wu), _ph_gather_send(wb["w_down"])])
    wu, wd = ph[0][0], ph[1][0]
    wr, wa, wo, wpg = (g.reshape(D_MODEL, D_MODEL) for g in g_small[:4])
    wpp = g_small[4]
    (x1, merged, y_a, y_b), ph = _merge_fwd(xf, ya, o, ga, gb, wr, wa, wo, tm, phases=[_ph_gather_pass(wd)])
    wd = ph[0][0].reshape(D_FF, D_MODEL)
    (x2, hm, u, act), _ = _mlp_fwd(x1, g_mlp, wu, wd, tm // 2)
    (loss_t, dx2, pb, de, hp, dtg, dg_ple), _ = _ple_loss(x2, pf, tf, g_ple, wpg, wpp, tm)

    chipmajor = lambda g: g.reshape(N_CHIPS, g.shape[-2] // N_CHIPS, g.shape[-1]) if g.ndim == 2 else g
    tmw = min(2 * tm, T)
    dw_pp = _wgrad(pb, de, "wgrad_ple_proj", False, D_MODEL, tmw)[0]
    part_ple = [chipmajor(_wgrad(hp, dtg, "wgrad_ple_gate", False, D_MODEL, tmw)[0]),
                dw_pp.reshape(PLE_DIM, N_CHIPS, D_MODEL // N_CHIPS).transpose(1, 0, 2)]
    (dx1, du, dg_mlp), ph = _mlp_bwd(dx2, u, x1, g_mlp, wu, wd, tm // 2, phases=[_ph_pair_send(g) for g in part_ple])
    send_ple, own_ple = _pair_sum(part_ple, first(ph), "pair_sum_ple")
    dw_down, ph = _wgrad(act, dx2, "wgrad_down", False, D_MODEL // 2, tmw, phases=[_ph_chip_send(s) for s in send_ple])
    red_ple = _chip_sum(own_ple, first(ph), "chip_sum_ple")
    part_mlp = [_wgrad(hm, du, "wgrad_up", True, D_MODEL, tmw)[0], chipmajor(dw_down)]
    (dga, dgb, dya, dyb, dyain, do), _ = _merge_bwd(dx1, ga, gb, y_a, y_b, wr, wa, wo, tm)
    dw_rnn, ph_up = _wgrad(ya, dya, "wgrad_rnn_proj", False, D_MODEL, tmw, phases=[_ph_pair_send(part_mlp[0])])
    dw_attn, ph_down = _wgrad(o, dyb, "wgrad_attn_proj", False, D_MODEL, tmw, phases=[_ph_pair_send(part_mlp[1])])
    dw_out, ph = _wgrad(merged, dx1, "wgrad_out", False, D_MODEL, tmw, phases=[_ph_half_swap(r) for r in red_ple])
    red_ple = first(ph)
    send_mlp, own_mlp = _pair_sum(part_mlp, [ph_up[0][0], ph_down[0][0]], "pair_sum_mlp")
    part_mix = [chipmajor(dw_rnn), chipmajor(dw_attn), chipmajor(dw_out)]
    (dxr, dgr, vec, dwrg2, dwig2), ph = _rnn_bwd(
        dyain, xr, gr, xc, h, cw_full, wrg2, b_rg, wig2, b_ig, lru_lambda, n_seq, S, tm_rnn,
        phases=[_ph_chip_send(s) for s in send_mlp] + [_ph_pair_send(g) for g in part_mix])
    red_mlp = _chip_sum(own_mlp, first(ph[:2]), "chip_sum_mlp")
    send_mix, own_mix = _pair_sum(part_mix, first(ph[2:]), "pair_sum_mix")
    (dq, dkc, dkp, dvc, dvp, dqg, dsk), ph = _attn_bwd(
        do, zq, zk, zv, *attn_c, phases=[_ph_half_swap(r) for r in red_mlp] + [_ph_chip_send(s) for s in send_mix])
    red_mlp = first(ph[:2])
    red_mix = _chip_sum(own_mix, first(ph[2:]), "chip_sum_mix")
    (dk, dv, dkg), _ = _kv_bwd(dkc, dkp, dvc, dvp, zk, kg, cosf, sins, ind_k, ind_kt, n_seq, S)
    dz_parts = [dxr, dgr, dq, dk, dv, dga, dgb]
    send_in, own_in = _pair_exchange_sum(_wgrad_in(h0, dz_parts, tm), "pair_sum_in")
    (grad_x, dg_mix), ph = _inproj_bwd(dz_parts, w_in_g, xf, g_mix, dx1, tm,
                                       phases=[_ph_half_swap(r) for r in red_mix] + [_ph_chip_send(send_in)])
    red_mix = first(ph[:3])
    red_in = _chip_sum([own_in], first(ph[3:]), "chip_sum_in")
    reduced = dict(zip(grp_ple + grp_mlp + grp_mix, red_ple + red_mlp + red_mix))
    grads = {
        "g_mix": dg_mix[0], "g_mlp": dg_mlp[0], "g_ple": dg_ple[0],
        "conv_w": vec[0:CONV_W], "conv_b": vec[4], "b_rg": vec[5], "b_ig": vec[6], "lru_lambda": vec[7],
        "w_rg": _pair_blockdiag_extract(dwrg2), "w_ig": _pair_blockdiag_extract(dwig2),
        "q_gain": dqg.reshape(N_HEADS, HEAD_DIM).sum(0), "k_gain": dkg.reshape(N_KV, HEAD_DIM).sum(0),
        "sinks": dsk.sum(1),
    }

    rows = [grads["conv_w"], _pad_row(grads["conv_b"]), _pad_row(grads["b_rg"]), _pad_row(grads["b_ig"]),
            _pad_row(grads["lru_lambda"]), _pad_row(grads["g_mix"]), _pad_row(grads["g_mlp"]),
            _pad_row(grads["g_ple"]), _pad_row(grads["q_gain"]), _pad_row(grads["k_gain"]), _pad_row(grads["sinks"]),
            _pad_row(loss_t[0:1, 0:1]), jnp.zeros((1, D_MODEL), F32)]
    vecs = jnp.concatenate(rows, axis=0)
    packed = jnp.concatenate([vecs.reshape(-1, LANES), grads["w_rg"].reshape(-1, LANES),
                              grads["w_ig"].reshape(-1, LANES)], axis=0)
    red = _allreduce_small(packed, "allreduce_small")
    nv = vecs.size // LANES
    rvec = red[0:nv].reshape(16, D_MODEL)
    loss = rvec[14, 0]
    nw = grads["w_rg"].size // LANES
    sg = {
        "conv_w": lax.dynamic_slice(rvec[0:CONV_W], (0, chip * (D_MODEL // N_CHIPS)), (CONV_W, D_MODEL // N_CHIPS)),
        "conv_b": rvec[4], "b_rg": rvec[5], "b_ig": rvec[6], "lru_lambda": rvec[7], "g_mix": rvec[8],
        "g_mlp": rvec[9], "g_ple": rvec[10], "q_gain": rvec[11, :HEAD_DIM], "k_gain": rvec[12, :HEAD_DIM],
        "sinks": rvec[13, :N_HEADS], "w_rg": red[nv:nv + nw], "w_ig": red[nv + nw:nv + 2 * nw],
    }
    sg = {k: sg[k].reshape(w[k].shape) for k in _SMALL}
    d_s, m_s, v_s = _adamw_small([w[k] for k in _SMALL], [sg[k] for k in _SMALL], [m[k] for k in _SMALL],
                                 [v[k] for k in _SMALL])
    grad, delta, new_m, new_v = dict(sg), dict(zip(_SMALL, d_s)), dict(zip(_SMALL, m_s)), dict(zip(_SMALL, v_s))

    for name in ("w_ple_proj", "w_up", "w_down", "w_rnn_proj", "w_attn_proj", "w_out", "w_ple_gate", "w_in"):
        shape = w[name].shape
        outs, ph = _adamw(w[name][0], reduced[name], m[name][0], v[name][0], "adamw_" + name, 128,
                          phases=[_ph_half_swap(r) for r in red_in] if name == "w_ple_proj" else ())
        if name == "w_ple_proj":
            reduced["w_in"] = ph[0][0]
        grad[name], delta[name], new_m[name], new_v[name] = (a.reshape(shape) for a in outs)

    return (loss, grad_x.reshape(x.shape), *[grad[k] for k in _WEIGHTS], *[delta[k] for k in _WEIGHTS],
            *[new_m[k] for k in _WEIGHTS], *[new_v[k] for k in _WEIGHTS])
```

```python
import functools
import math

import numpy as np
import jax
import jax.numpy as jnp
from jax import lax
from jax.experimental import pallas as pl
from jax.experimental.pallas import tpu as pltpu

F32 = jnp.float32
BF16 = jnp.bfloat16

D_MODEL = 1024
N_HEADS = 16
N_KV = 4
HEAD_DIM = 64
KV_W = N_KV * HEAD_DIM
D_FF = 4096
PLE_DIM = 256
WINDOW = 128
CONV_W = 4
LRU_C = 8.0
NORM_EPS = 1e-6
ROPE_THETA = 10000.0
N_CHIPS = 4
IN_TOTAL = 5632
IN_BLK = IN_TOTAL // N_CHIPS
IN_SEGS = (0, 1024, 2048, 3072, 3328, 3584, 4608, 5632)

ADAM_LR = 0.001
ADAM_B1 = 0.9
ADAM_B2 = 0.999
ADAM_EPS = 1e-08
ADAM_WD = 0.01
ADAM_STEP = 10

LANES = 128
VMEM_LIMIT = 56 * 1024 * 1024
MESH_ID = pl.DeviceIdType.MESH


def _dot(a, b):
    return jnp.dot(a, b, preferred_element_type=F32)


def _dot_nt(a, b):
    return lax.dot_general(a, b, (((1,), (1,)), ((), ())), preferred_element_type=F32)


def _dot_tn(a, b):
    return lax.dot_general(a, b, (((0,), (0,)), ((), ())), preferred_element_type=F32)


def _split_dot(x, ind):
    hi = x.astype(BF16)
    lo = (x - hi.astype(F32)).astype(BF16)
    return _dot(hi, ind) + _dot(lo, ind)


def _sigmoid(x):
    return 1.0 / (1.0 + jnp.exp(-x))


_GELU_C = math.sqrt(2.0 / math.pi)


def _gelu_and_grad(g):
    inner = _GELU_C * (g + 0.044715 * g * g * g)
    t = jnp.tanh(inner)
    gelu = 0.5 * g * (1.0 + t)
    dgelu = 0.5 * (1.0 + t) + 0.5 * g * (1.0 - t * t) * _GELU_C * (1.0 + 3.0 * 0.044715 * g * g)
    return gelu, dgelu


def _const(shape):
    nd = len(shape)
    return pl.BlockSpec(shape, lambda *_: (0,) * nd)


def _params(n_grid, vmem=VMEM_LIMIT):
    return pltpu.CompilerParams(dimension_semantics=("arbitrary",) * n_grid, vmem_limit_bytes=vmem)


def _rms_fwd(x, g):
    r = lax.rsqrt(jnp.mean(x * x, axis=-1, keepdims=True) + NORM_EPS)
    return (x * r) * g, r


def _rms_bwd(dy, x, r, g):
    dn = dy * g
    dx = r * dn - x * (r * r * r * jnp.mean(dn * x, axis=-1, keepdims=True))
    dg = jnp.sum(dy * (x * r), axis=0, keepdims=True)
    return dx, dg


def _seg_pieces(blk_lo, blk_hi):
    out = []
    for s in range(7):
        lo, hi = max(blk_lo, IN_SEGS[s]), min(blk_hi, IN_SEGS[s + 1])
        if lo < hi:
            out.append((s, lo - IN_SEGS[s], hi - IN_SEGS[s], lo - blk_lo))
    return out


def _mesh_pos():
    x, y, c = lax.axis_index("x"), lax.axis_index("y"), lax.axis_index("c")
    other_chips = [(1 - x, y), (x, 1 - y), (1 - x, 1 - y)]
    return x, y, c, other_chips


def _peer_slot(k, x, y):
    dx = jnp.bitwise_xor(k // 2, x)
    dy = jnp.bitwise_xor(k % 2, y)
    return jnp.maximum(dx + 2 * dy - 1, 0)


def _half_rows(c, R):
    return pl.ds(pl.multiple_of(c * R, R), R), pl.ds(pl.multiple_of((1 - c) * R, R), R)


def _remote(src, dst, sems, to):
    return pltpu.make_async_remote_copy(src_ref=src, dst_ref=dst, send_sem=sems[0], recv_sem=sems[1],
                                        device_id=to, device_id_type=MESH_ID)


class _Phase:
    def __init__(self, ins, inout, outs, n_remote, n_local, build):
        self.ins, self.inout, self.outs = list(ins), list(inout), list(outs)
        self.n_remote, self.n_local, self.build = n_remote, n_local, build


def _ph_gather_send(wb):
    R2, C = wb.shape
    R = R2 // 2

    def build(ins, outs, rsem, lsem):
        (w_ref,), (g_ref,) = ins, outs
        x, y, c, chips = _mesh_pos()
        me = 2 * x + y
        mine, _ = _half_rows(c, R)
        loc = [pltpu.make_async_copy(w_ref, g_ref.at[me], lsem(0))]
        outg = [_remote(w_ref.at[mine], g_ref.at[me, mine], rsem(j), (cx, cy, c)) for j, (cx, cy) in enumerate(chips)]
        inc = [functools.partial(_remote, w_ref.at[mine], g_ref.at[2 * cx + cy, mine], rsem(j), (x, y, c))
               for j, (cx, cy) in enumerate(chips)]
        return loc, outg, inc

    return _Phase([wb], [], [jax.ShapeDtypeStruct((N_CHIPS, R2, C), wb.dtype)], 3, 1, build)


def _ph_gather_pass(gath):
    _, R2, C = gath.shape
    R = R2 // 2

    def build(ins, outs, rsem, lsem):
        (g_ref,) = outs
        x, y, c, chips = _mesh_pos()
        mine, theirs = _half_rows(c, R)
        outg, inc = [], []
        for j, (cx, cy) in enumerate(chips):
            blk = g_ref.at[2 * cx + cy, mine]
            outg.append(_remote(blk, blk, rsem(j), (x, y, 1 - c)))
            got = g_ref.at[2 * cx + cy, theirs]
            inc.append(functools.partial(_remote, got, got, rsem(j), (x, y, c)))
        return [], outg, inc

    return _Phase([], [gath], [], 3, 0, build)


def _ph_pair_send(partial):
    _, R2, C = partial.shape
    R = R2 // 2

    def build(ins, outs, rsem, lsem):
        (p_ref,), (s_ref,) = ins, outs
        x, y, c, _ = _mesh_pos()
        _, theirs = _half_rows(c, R)
        src = p_ref.at[:, theirs, :]
        return ([], [_remote(src, s_ref, rsem(0), (x, y, 1 - c))],
                [functools.partial(_remote, src, s_ref, rsem(0), (x, y, c))])

    return _Phase([partial], [], [jax.ShapeDtypeStruct((N_CHIPS, R, C), F32)], 1, 0, build)


def _ph_chip_send(sendb):
    def build(ins, outs, rsem, lsem):
        (s_ref,), (r_ref,) = ins, outs
        x, y, c, chips = _mesh_pos()
        outg = [_remote(s_ref.at[j], r_ref.at[j], rsem(j), (cx, cy, c)) for j, (cx, cy) in enumerate(chips)]
        inc = [functools.partial(_remote, s_ref.at[j], r_ref.at[j], rsem(j), (x, y, c)) for j in range(3)]
        return [], outg, inc

    return _Phase([sendb], [], [jax.ShapeDtypeStruct(sendb.shape, sendb.dtype)], 3, 0, build)


def _ph_half_swap(red):
    R2, C = red.shape
    R = R2 // 2

    def build(ins, outs, rsem, lsem):
        (r_ref,) = outs
        x, y, c, _ = _mesh_pos()
        mine, theirs = _half_rows(c, R)
        return ([], [_remote(r_ref.at[mine], r_ref.at[mine], rsem(0), (x, y, 1 - c))],
                [functools.partial(_remote, r_ref.at[theirs], r_ref.at[theirs], rsem(0), (x, y, c))])

    return _Phase([], [red], [], 1, 0, build)


def _call(body, *, name, grid, in_specs, out_specs, out_shape, scratch_shapes=(), phases=()):
    single = not isinstance(out_specs, (list, tuple))
    out_specs = [out_specs] if single else list(out_specs)
    out_shape = [out_shape] if single else list(out_shape)
    n_in, n_out, n_scr = len(in_specs), len(out_specs), len(scratch_shapes)
    if not phases:
        call = pl.pallas_call(body, name=name, grid=grid, in_specs=in_specs, out_specs=out_specs,
                              out_shape=out_shape, scratch_shapes=list(scratch_shapes),
                              compiler_params=_params(len(grid)))
        return lambda *operands: (list(call(*operands)), [])

    ex_in, ex_out, aliases, spans = [], [], {}, []
    for ph in phases:
        i0, o0 = len(ex_in), len(ex_out)
        ex_in += ph.ins
        for a in ph.inout:
            aliases[n_in + len(ex_in)] = n_out + len(ex_out)
            ex_in.append(a)
            ex_out.append(jax.ShapeDtypeStruct(a.shape, a.dtype))
        ex_out += ph.outs
        spans.append((i0, len(ph.ins), o0, len(ex_out) - o0))
    n_remote = sum(ph.n_remote for ph in phases)
    n_local = max(sum(ph.n_local for ph in phases), 1)

    def wrapped(*refs):
        base_in, xin = refs[:n_in], refs[n_in:n_in + len(ex_in)]
        o0 = n_in + len(ex_in)
        base_out, xout = refs[o0:o0 + n_out], refs[o0 + n_out:o0 + n_out + len(ex_out)]
        scr = refs[o0 + n_out + len(ex_out):]
        send_sems, recv_sems, loc_sems = scr[n_scr:]
        first = functools.reduce(jnp.logical_and, [pl.program_id(i) == 0 for i in range(len(grid))])
        last = functools.reduce(jnp.logical_and, [pl.program_id(i) == grid[i] - 1 for i in range(len(grid))])

        def copies():
            out, r0, l0 = [], 0, 0
            for ph, (i0, ni, p0, no) in zip(phases, spans):
                rsem = lambda k, r0=r0: (send_sems.at[r0 + k], recv_sems.at[r0 + k])
                lsem = lambda k, l0=l0: loc_sems.at[l0 + k]
                out.append(ph.build(xin[i0:i0 + ni], xout[p0:p0 + no], rsem, lsem))
                r0, l0 = r0 + ph.n_remote, l0 + ph.n_local
            return out

        @pl.when(first)
        def _():
            for loc, outg, _ in copies():
                for cp in loc + outg:
                    cp.start()

        body(*base_in, *base_out, *scr[:n_scr])

        @pl.when(last)
        def _():
            for loc, outg, inc in copies():
                for make in inc:
                    make().wait_recv()
                for cp in outg:
                    cp.wait_send()
                for cp in loc:
                    cp.wait()

    hbm = pl.BlockSpec(memory_space=pl.ANY)
    call = pl.pallas_call(
        wrapped, name=name, grid=grid, in_specs=list(in_specs) + [hbm] * len(ex_in),
        out_specs=out_specs + [hbm] * len(ex_out), out_shape=out_shape + ex_out,
        scratch_shapes=list(scratch_shapes) + [pltpu.SemaphoreType.DMA((n_remote,)), pltpu.SemaphoreType.DMA((n_remote,)),
                                              pltpu.SemaphoreType.DMA((n_local,))],
        input_output_aliases=aliases, compiler_params=_params(len(grid)))

    def run(*operands):
        res = call(*operands, *ex_in)
        extra = res[n_out:]
        return list(res[:n_out]), [list(extra[p0:p0 + no]) for (_, _, p0, no) in spans]

    return run


def _inproj_fwd(x, g_mix, w_in, tm, phases=()):
    T = x.shape[0]
    widths = [IN_SEGS[i + 1] - IN_SEGS[i] for i in range(7)]

    def body(x_ref, g_ref, w_ref, h_ref, *z_refs):
        h, _ = _rms_fwd(x_ref[...], g_ref[...])
        hb = h.astype(BF16)
        h_ref[...] = hb
        for j in range(N_CHIPS):
            zj = _dot(hb, w_ref[j])
            for s, lo, hi, off in _seg_pieces(j * IN_BLK, (j + 1) * IN_BLK):
                z_refs[s][:, lo:hi] = zj[:, off:off + hi - lo]

    return _call(
        body, phases=phases, name="inproj_fwd", grid=(T // tm,),
        in_specs=[pl.BlockSpec((tm, D_MODEL), lambda i: (i, 0)), _const((1, D_MODEL)),
                  _const((N_CHIPS, D_MODEL, IN_BLK))],
        out_specs=[pl.BlockSpec((tm, D_MODEL), lambda i: (i, 0))]
        + [pl.BlockSpec((tm, w), lambda i: (i, 0)) for w in widths],
        out_shape=[jax.ShapeDtypeStruct((T, D_MODEL), BF16)]
        + [jax.ShapeDtypeStruct((T, w), F32) for w in widths],
    )(x, g_mix, w_in)


def _inproj_bwd(dz_parts, w_in, x, g_mix, dx1, tm, phases=()):
    T = x.shape[0]
    widths = [IN_SEGS[i + 1] - IN_SEGS[i] for i in range(7)]

    def body(*refs):
        p_refs = refs[:7]
        w_ref, x_ref, g_ref, dx1_ref, gx_ref, dg_ref, dz_ref = refs[7:]

        @pl.when(pl.program_id(0) == 0)
        def _():
            dg_ref[...] = jnp.zeros_like(dg_ref)

        for s in range(7):
            dz_ref[:, IN_SEGS[s]:IN_SEGS[s + 1]] = p_refs[s][...]
        dh = jnp.zeros((tm, D_MODEL), F32)
        for j in range(N_CHIPS):
            dh = dh + _dot_nt(dz_ref[:, j * IN_BLK:(j + 1) * IN_BLK], w_ref[j])
        xv = x_ref[...]
        g = g_ref[...]
        _, r = _rms_fwd(xv, g)
        dx, dg = _rms_bwd(dh, xv, r, g)
        gx_ref[...] = dx1_ref[...] + dx
        dg_ref[...] += dg

    row = lambda w: pl.BlockSpec((tm, w), lambda i: (i, 0))
    return _call(
        body, phases=phases, name="inproj_bwd", grid=(T // tm,),
        in_specs=[row(w) for w in widths]
        + [_const((N_CHIPS, D_MODEL, IN_BLK)), row(D_MODEL), _const((1, D_MODEL)), row(D_MODEL)],
        out_specs=[row(D_MODEL), _const((1, D_MODEL))],
        out_shape=[jax.ShapeDtypeStruct((T, D_MODEL), F32), jax.ShapeDtypeStruct((1, D_MODEL), F32)],
        scratch_shapes=[pltpu.VMEM((tm, IN_TOTAL), BF16)],
    )(*dz_parts, w_in, x, g_mix, dx1)


def _wgrad_in(h0, dz_parts, tm):
    T = h0.shape[0]
    widths = [IN_SEGS[i + 1] - IN_SEGS[i] for i in range(7)]

    def body(*refs):
        h_ref, p_refs, o_ref, acc_ref, sem = refs[0], refs[1:8], refs[8], refs[9], refs[10]
        t = pl.program_id(0)

        @pl.when(t == 0)
        def _():
            acc_ref[...] = jnp.zeros_like(acc_ref)

        hv = h_ref[...]
        for j in range(N_CHIPS):
            for s, lo, hi, off in _seg_pieces(j * IN_BLK, (j + 1) * IN_BLK):
                acc_ref[j, :, off:off + hi - lo] += _dot_tn(hv, p_refs[s][:, lo:hi])

        @pl.when(t == T // tm - 1)
        def _():
            cp = pltpu.make_async_copy(acc_ref, o_ref, sem)
            cp.start()
            cp.wait()

    row = lambda w: pl.BlockSpec((tm, w), lambda i: (i, 0))
    return pl.pallas_call(
        body, name="wgrad_in", grid=(T // tm,), in_specs=[row(D_MODEL)] + [row(w) for w in widths],
        out_specs=pl.BlockSpec(memory_space=pl.ANY),
        out_shape=jax.ShapeDtypeStruct((N_CHIPS, D_MODEL, IN_BLK), F32),
        scratch_shapes=[pltpu.VMEM((N_CHIPS, D_MODEL, IN_BLK), F32), pltpu.SemaphoreType.DMA],
        compiler_params=_params(1),
    )(h0, *dz_parts)


def _wgrad(a, g, name, blocked, cn, tm, phases=()):
    T, K = a.shape
    N = g.shape[1]
    nb = N // cn

    def body(a_ref, g_ref, o_ref):
        @pl.when(pl.program_id(1) == 0)
        def _():
            o_ref[...] = jnp.zeros_like(o_ref)

        o_ref[...] += _dot_tn(a_ref[...].astype(BF16), g_ref[...].astype(BF16))

    if blocked:
        out_spec = pl.BlockSpec((None, K, cn), lambda j, t: (j, 0, 0))
        out_shape = jax.ShapeDtypeStruct((nb, K, cn), F32)
    else:
        out_spec = pl.BlockSpec((K, cn), lambda j, t: (0, j))
        out_shape = jax.ShapeDtypeStruct((K, N), F32)
    outs, extra = _call(
        body, phases=phases, name=name, grid=(nb, T // tm),
        in_specs=[pl.BlockSpec((tm, K), lambda j, t: (t, 0)), pl.BlockSpec((tm, cn), lambda j, t: (t, j))],
        out_specs=out_spec, out_shape=out_shape,
    )(a, g)
    return outs[0], extra


def _shift_down(x, prev8, sft, row, row8, tm):
    xs = pltpu.roll(x, sft, 0)
    top = jnp.where(row8 < sft, pltpu.roll(prev8, sft, 0), xs[0:8])
    return jnp.concatenate([top, xs[8:]], axis=0)


def _shift_up(x, next8, sft, row8, tm):
    xs = pltpu.roll(x, tm - sft, 0)
    bot = jnp.where(row8 >= 8 - sft, pltpu.roll(next8, 8 - sft, 0), xs[tm - 8:tm])
    return jnp.concatenate([xs[0:tm - 8], bot], axis=0)


def _conv_fwd(x, prev8, cw_ref, cb, row, row8, tm):
    xc = cb + cw_ref[CONV_W - 1:CONV_W, :] * x
    for sft in range(1, CONV_W):
        j = CONV_W - 1 - sft
        xc = xc + cw_ref[j:j + 1, :] * _shift_down(x, prev8, sft, row, row8, tm)
    return xc


def _blockdiag_dot(xb, w_ref, transpose):
    outs = []
    for b in range(D_MODEL // LANES):
        xs = xb[:, b * LANES:(b + 1) * LANES]
        outs.append(_dot_nt(xs, w_ref[b]) if transpose else _dot(xs, w_ref[b]))
    return jnp.concatenate(outs, axis=1)


def _softplus_neg(lam):
    e = jnp.exp(-jnp.abs(lam))
    u = 1.0 + e
    log1p_e = jnp.where(u == 1.0, e, jnp.log(u) * (e / (u - 1.0)))
    sp = jnp.maximum(-lam, 0.0) + log1p_e
    return sp, -_sigmoid(-lam)


def _lru_gates(xc, wrg_ref, brg, wig_ref, big, sp):
    xcb = xc.astype(BF16)
    r = _sigmoid(_blockdiag_dot(xcb, wrg_ref, False) + brg)
    i = _sigmoid(_blockdiag_dot(xcb, wig_ref, False) + big)
    log_a = (-LRU_C) * r * sp
    a = jnp.exp(log_a)
    t = jnp.tanh(log_a)
    one_m_a2 = (-2.0) * t / (1.0 - t)
    mult = jnp.sqrt(one_m_a2)
    return xcb, r, i, a, mult


def _scan_down(a, b, row, tm):
    d = 1
    while d < tm:
        if d < 8:
            keep = row >= d
            a_s = jnp.where(keep, pltpu.roll(a, d, 0), 1.0)
            b_s = jnp.where(keep, pltpu.roll(b, d, 0), 0.0)
            b = a * b_s + b
            a = a * a_s
        else:
            b = jnp.concatenate([b[:d], a[d:] * b[:-d] + b[d:]], axis=0)
            a = jnp.concatenate([a[:d], a[d:] * a[:-d]], axis=0)
        d *= 2
    return a, b


def _scan_up(c, b, row, tm):
    d = 1
    while d < tm:
        if d < 8:
            keep = row < tm - d
            c_s = jnp.where(keep, pltpu.roll(c, tm - d, 0), 1.0)
            b_s = jnp.where(keep, pltpu.roll(b, tm - d, 0), 0.0)
            b = c * b_s + b
            c = c * c_s
        else:
            b = jnp.concatenate([c[:-d] * b[d:] + b[:-d], b[-d:]], axis=0)
            c = jnp.concatenate([c[:-d] * c[d:], c[-d:]], axis=0)
        d *= 2
    return c, b


def _rnn_fwd(xr, gr, conv_w, conv_b, wrg2, b_rg, wig2, b_ig, lam, n_seq, S, tm, phases=()):
    T = xr.shape[0]
    nt = S // tm
    W = D_MODEL

    def body(xr_ref, gr_ref, cw_ref, cb_ref, wrg_ref, brg_ref, wig_ref, big_ref, lam_ref,
             xc_ref, h_ref, r_ref, i_ref, a_ref, mult_ref, ya_ref, px_ref, ph_ref):
        @pl.when(pl.program_id(1) == 0)
        def _():
            px_ref[...] = jnp.zeros_like(px_ref)
            ph_ref[...] = jnp.zeros_like(ph_ref)

        row = lax.broadcasted_iota(jnp.int32, (tm, W), 0)
        row8 = lax.broadcasted_iota(jnp.int32, (8, W), 0)
        x = xr_ref[...]
        xc = _conv_fwd(x, px_ref[...], cw_ref, cb_ref[...], row, row8, tm)
        sp, _ = _softplus_neg(lam_ref[...])
        _, r, i, a, mult = _lru_gates(xc, wrg_ref, brg_ref[...], wig_ref, big_ref[...], sp)
        r_ref[...], i_ref[...], a_ref[...], mult_ref[...] = r, i, a, mult
        bterm = mult * (i * xc)
        acum, hloc = _scan_down(a, bterm, row, tm)
        h = hloc + acum * ph_ref[7:8, :]
        h_ref[...] = h
        xc_ref[...] = xc
        gelu, _ = _gelu_and_grad(gr_ref[...])
        ya_ref[...] = (h * gelu).astype(BF16)
        px_ref[...] = xr_ref[tm - 8:tm, :]
        ph_ref[...] = h_ref[tm - 8:tm, :]

    tile = pl.BlockSpec((tm, W), lambda s, t: (s * nt + t, 0))
    return _call(
        body, phases=phases, name="rnn_fwd", grid=(n_seq, nt),
        in_specs=[tile, tile, _const((CONV_W, W)), _const((1, W)), _const((8, LANES, LANES)), _const((1, W)),
                  _const((8, LANES, LANES)), _const((1, W)), _const((1, W))],
        out_specs=[tile] * 7,
        out_shape=[jax.ShapeDtypeStruct((T, W), F32)] * 6 + [jax.ShapeDtypeStruct((T, W), BF16)],
        scratch_shapes=[pltpu.VMEM((8, W), F32), pltpu.VMEM((8, W), F32)],
    )(xr, gr, conv_w, conv_b, wrg2, b_rg, wig2, b_ig, lam)


def _rnn_bwd(dya, xr, gr, xc, h, gates, conv_w, wrg2, wig2, lam, n_seq, S, tm, phases=()):
    T = xr.shape[0]
    nt = S // tm
    W = D_MODEL
    nb8 = tm // 8

    def body(dya_ref, xr_ref, gr_ref, xc_ref, h_ref, r_ref, i_ref, a_ref, mult_ref, xprev_ref, hprev_ref, cw_ref,
             wrg_ref, wig_ref, lam_ref, dxr_ref, dgr_ref, vec_ref, dwrg_ref, dwig_ref, cg_ref, ndxc_ref, tmp_ref):
        s, ti = pl.program_id(0), pl.program_id(1)

        @pl.when((s == 0) & (ti == 0))
        def _():
            vec_ref[...] = jnp.zeros_like(vec_ref)
            dwrg_ref[...] = jnp.zeros_like(dwrg_ref)
            dwig_ref[...] = jnp.zeros_like(dwig_ref)

        @pl.when(ti == 0)
        def _():
            cg_ref[...] = jnp.zeros_like(cg_ref)
            ndxc_ref[...] = jnp.zeros_like(ndxc_ref)

        first = ti == nt - 1
        row = lax.broadcasted_iota(jnp.int32, (tm, W), 0)
        row8 = lax.broadcasted_iota(jnp.int32, (8, W), 0)
        x = xr_ref[...]
        xc = xc_ref[...]
        hv = h_ref[...]
        xprev = jnp.where(first, 0.0, xprev_ref[...])
        hprev = jnp.where(first, 0.0, hprev_ref[...])
        sp, dsp_dlam = _softplus_neg(lam_ref[...])
        xcb = xc.astype(BF16)
        r, i, a, mult = r_ref[...], i_ref[...], a_ref[...], mult_ref[...]

        gelu, dgelu = _gelu_and_grad(gr_ref[...])
        dya_v = dya_ref[...]
        dgr_ref[...] = (dya_v * hv * dgelu).astype(BF16)
        dh = dya_v * gelu
        c = jnp.where(row < tm - 1, pltpu.roll(a, tm - 1, 0), 1.0)
        ccum, gloc = _scan_up(c, dh, row, tm)
        G = gloc + ccum * cg_ref[0:1, :]
        tmp_ref[...] = a * G
        cg_ref[...] = tmp_ref[0:8, :]

        h_m1 = _shift_down(hv, hprev, 1, row, row8, tm)
        ixc = i * xc
        dixc = G * mult
        dlog_a = (G * h_m1) * a - (G * ixc) * (a * a / mult)
        dr = dlog_a * ((-LRU_C) * sp)
        di = dixc * xc
        drg = dr * r * (1.0 - r)
        dig = di * i * (1.0 - i)
        vec_ref[7:8, :] += jnp.sum(dlog_a * ((-LRU_C) * r), axis=0, keepdims=True) * dsp_dlam
        vec_ref[5:6, :] += jnp.sum(drg, axis=0, keepdims=True)
        vec_ref[6:7, :] += jnp.sum(dig, axis=0, keepdims=True)
        drgb = drg.astype(BF16)
        digb = dig.astype(BF16)
        dxc = dixc * i + _blockdiag_dot(drgb, wrg_ref, True) + _blockdiag_dot(digb, wig_ref, True)
        for b in range(W // LANES):
            sl = slice(b * LANES, (b + 1) * LANES)
            dwrg_ref[b] += _dot_tn(xcb[:, sl], drgb[:, sl])
            dwig_ref[b] += _dot_tn(xcb[:, sl], digb[:, sl])

        vec_ref[4:5, :] += jnp.sum(dxc, axis=0, keepdims=True)
        vec_ref[3:4, :] += jnp.sum(dxc * x, axis=0, keepdims=True)
        dxr = cw_ref[CONV_W - 1:CONV_W, :] * dxc
        nxt = ndxc_ref[...]
        for sft in range(1, CONV_W):
            j = CONV_W - 1 - sft
            vec_ref[j:j + 1, :] += jnp.sum(dxc * _shift_down(x, xprev, sft, row, row8, tm), axis=0, keepdims=True)
            dxr = dxr + cw_ref[j:j + 1, :] * _shift_up(dxc, nxt, sft, row8, tm)
        dxr_ref[...] = dxr.astype(BF16)
        tmp_ref[...] = dxc
        ndxc_ref[...] = tmp_ref[0:8, :]

    rev = lambda s, t: (s * nt + nt - 1 - t, 0)
    tile = pl.BlockSpec((tm, W), rev)
    prev8 = pl.BlockSpec((8, W), lambda s, t: (jnp.maximum((s * nt + nt - 1 - t) * nb8 - 1, 0), 0))
    return _call(
        body, phases=phases, name="rnn_bwd", grid=(n_seq, nt),
        in_specs=[tile] * 9 + [prev8, prev8, _const((CONV_W, W)), _const((8, LANES, LANES)),
                               _const((8, LANES, LANES)), _const((1, W))],
        out_specs=[tile, tile, _const((16, W)), _const((8, LANES, LANES)), _const((8, LANES, LANES))],
        out_shape=[jax.ShapeDtypeStruct((T, W), BF16), jax.ShapeDtypeStruct((T, W), BF16),
                   jax.ShapeDtypeStruct((16, W), F32), jax.ShapeDtypeStruct((8, LANES, LANES), F32),
                   jax.ShapeDtypeStruct((8, LANES, LANES), F32)],
        scratch_shapes=[pltpu.VMEM((8, W), F32), pltpu.VMEM((8, W), F32), pltpu.VMEM((tm, W), F32)],
    )(dya, xr, gr, xc, h, *gates, xr, h, conv_w, wrg2, wig2, lam)


def _head_swap(t, lane):
    w = t.shape[1]
    return jnp.where(lane % HEAD_DIM < HEAD_DIM // 2, pltpu.roll(t, w - HEAD_DIM // 2, 1),
                     pltpu.roll(t, HEAD_DIM // 2, 1))


def _qk_prep(t, gain, cosf, sins, ind, indt, lane):
    ms = _split_dot(t * t, ind) * (1.0 / HEAD_DIM)
    rstd = _split_dot(lax.rsqrt(ms + NORM_EPS), indt)
    tn = (t * rstd) * gain
    return tn * cosf + _head_swap(tn, lane) * sins, rstd


def _qk_prep_bwd(dy, t, rstd, gain, cosf, sins, ind, indt, lane):
    dtn = dy * cosf + _head_swap(dy * sins, lane)
    dgain = jnp.sum(dtn * (t * rstd), axis=0, keepdims=True)
    dn = dtn * gain
    m = _split_dot(_split_dot(dn * t, ind), indt) * (1.0 / HEAD_DIM)
    return rstd * dn - t * (rstd * rstd * rstd * m), dgain


def _attn_mask_t(blk_idx):
    ci = lax.broadcasted_iota(jnp.int32, (2 * WINDOW, WINDOW), 0)
    qi = lax.broadcasted_iota(jnp.int32, (2 * WINDOW, WINDOW), 1)
    diff = WINDOW + qi - ci
    return (diff >= 0) & (diff < WINDOW) & ((ci >= WINDOW) | (blk_idx > 0))


def _stack_heads(t, kvh, lo):
    parts = []
    for i in (2 * kvh, 2 * kvh + 1):
        tp = t[:, i * LANES:(i + 1) * LANES]
        parts += [jnp.where(lo, tp, 0.0), jnp.where(lo, 0.0, tp)]
    return jnp.concatenate(parts, axis=0).astype(BF16)


def _unstack_heads(ts, lo):
    w = WINDOW
    return jnp.where(lo, ts[0:w], ts[w:2 * w]), jnp.where(lo, ts[2 * w:3 * w], ts[3 * w:4 * w])


def _dup_head(t, kvh, lo2):
    m = kvh // 2
    t2 = t[:, m * LANES:(m + 1) * LANES]
    t2r = pltpu.roll(t2, HEAD_DIM, 1)
    return (jnp.where(lo2, t2, t2r) if kvh % 2 == 0 else jnp.where(lo2, t2r, t2)).astype(BF16)


def _fold_head(ts, kvh, lo2):
    tot = ts + pltpu.roll(ts, HEAD_DIM, 1)
    own = lo2 if kvh % 2 == 0 else ~lo2
    return jnp.where(own, tot, 0.0)


KEY_CHUNKS = tuple(slice(i * 64, (i + 1) * 64) for i in range(2 * WINDOW // 64))


def _fold8(x, op):
    return op(x.reshape(x.shape[0] // 8, 8, x.shape[1]), axis=0)


def _softmax_stats(s_ref, b, cols, sink):
    m8 = None
    for c in KEY_CHUNKS:
        t = _fold8(s_ref[b, c, cols], jnp.max)
        m8 = t if m8 is None else jnp.maximum(m8, t)
    mx = jnp.maximum(jnp.max(m8, axis=0, keepdims=True), sink)
    d8 = None
    for c in KEY_CHUNKS:
        t = _fold8(jnp.exp(s_ref[b, c, cols] - mx), jnp.sum)
        d8 = t if d8 is None else d8 + t
    es = jnp.exp(sink - mx)
    inv = 1.0 / (jnp.sum(d8, axis=0, keepdims=True) + es)
    return mx, inv, es * inv


def _attn_fwd(q, k, v, qg, kg, sinks, cosf, sins, ind_q, ind_qt, ind_k, ind_kt, n_seq, S, phases=()):
    T = q.shape[0]
    nblk = S // WINDOW
    W = D_MODEL

    def body(sink_ref, q_ref, k_ref, v_ref, qg_ref, kg_ref, cos_ref, sin_ref, iq_ref, iqt_ref, ik_ref, ikt_ref,
             o_ref, kc_ref, vc_ref, s_ref, p_ref, qs_ref, kd_ref, vd_ref):
        n = pl.program_id(1)

        @pl.when(n == 0)
        def _():
            kc_ref[...] = jnp.zeros_like(kc_ref)
            vc_ref[...] = jnp.zeros_like(vc_ref)

        lane = lax.broadcasted_iota(jnp.int32, (WINDOW, W), 1)
        lo = lane[:, :LANES] < HEAD_DIM
        lo2 = lax.broadcasted_iota(jnp.int32, (2 * WINDOW, LANES), 1) < HEAD_DIM
        cosf, sinv = jnp.tile(cos_ref[...], (1, W // LANES)), jnp.tile(sin_ref[...], (1, W // LANES))
        qr, _ = _qk_prep(q_ref[...], qg_ref[...], cosf, sinv, iq_ref[...], iqt_ref[...], lane)
        kr, _ = _qk_prep(k_ref[...], kg_ref[...], cosf[:, :KV_W], sinv[:, :KV_W], ik_ref[...], ikt_ref[...],
                         lane[:, :KV_W])
        kc_ref[WINDOW:2 * WINDOW, :] = kr
        vc_ref[WINDOW:2 * WINDOW, :] = v_ref[...]
        kc, vc = kc_ref[...], vc_ref[...]
        mask = jnp.tile(_attn_mask_t(n), (1, 4))
        qr = qr * HEAD_DIM ** -0.5
        for kvh in range(N_KV):
            qs_ref[kvh] = _stack_heads(qr, kvh, lo)
            kd_ref[kvh] = _dup_head(kc, kvh, lo2)
            vd_ref[kvh] = _dup_head(vc, kvh, lo2)

        def scores(kvh):
            s_ref[kvh % 2] = jnp.where(mask, _dot_nt(kd_ref[kvh], qs_ref[kvh]), -1e30)

        def softmax(kvh):
            b = kvh % 2
            for r in range(4):
                cols = slice(r * WINDOW, (r + 1) * WINDOW)
                mx, inv, _ = _softmax_stats(s_ref, b, cols, sink_ref[4 * kvh + r])
                for c in KEY_CHUNKS:
                    p_ref[b, c, cols] = (jnp.exp(s_ref[b, c, cols] - mx) * inv).astype(BF16)

        def output(kvh):
            o0, o1 = _unstack_heads(_dot_tn(p_ref[kvh % 2], vd_ref[kvh]), lo)
            o_ref[:, (2 * kvh) * LANES:(2 * kvh + 1) * LANES] = o0.astype(BF16)
            o_ref[:, (2 * kvh + 1) * LANES:(2 * kvh + 2) * LANES] = o1.astype(BF16)

        scores(0)
        for kvh in range(N_KV):
            if kvh + 1 < N_KV:
                scores(kvh + 1)
            softmax(kvh)
            output(kvh)
        kc_ref[0:WINDOW, :] = kr
        vc_ref[0:WINDOW, :] = v_ref[...]

    blk = lambda w: pl.BlockSpec((WINDOW, w), lambda s, n: (s * nblk + n, 0))
    pos = pl.BlockSpec((WINDOW, LANES), lambda s, n: (n, 0))
    outs, extra = _call(
        body, phases=phases, name="attn_fwd", grid=(n_seq, nblk),
        in_specs=[pl.BlockSpec(memory_space=pltpu.SMEM), blk(W), blk(KV_W), blk(KV_W), _const((1, W)),
                  _const((1, KV_W)), pos, pos, _const((W, LANES)), _const((LANES, W)), _const((KV_W, LANES)),
                  _const((LANES, KV_W))],
        out_specs=blk(W), out_shape=jax.ShapeDtypeStruct((T, W), BF16),
        scratch_shapes=[pltpu.VMEM((2 * WINDOW, KV_W), F32), pltpu.VMEM((2 * WINDOW, KV_W), F32),
                        pltpu.VMEM((2, 2 * WINDOW, 4 * WINDOW), F32), pltpu.VMEM((2, 2 * WINDOW, 4 * WINDOW), BF16),
                        pltpu.VMEM((N_KV, 4 * WINDOW, LANES), BF16), pltpu.VMEM((N_KV, 2 * WINDOW, LANES), BF16),
                        pltpu.VMEM((N_KV, 2 * WINDOW, LANES), BF16)],
    )(sinks, q, k, v, qg, kg, cosf, sins, ind_q, ind_qt, ind_k, ind_kt)
    return outs[0], extra


def _attn_bwd(do, q, k, v, qg, kg, sinks, cosf, sins, ind_q, ind_qt, ind_k, ind_kt, n_seq, S, phases=()):
    T = q.shape[0]
    nblk = S // WINDOW
    W = D_MODEL

    def body(sink_ref, do_ref, q_ref, k_ref, v_ref, qg_ref, kg_ref, cos_ref, sin_ref, iq_ref, iqt_ref, ik_ref,
             ikt_ref, dq_ref, dkc_ref, dkp_ref, dvc_ref, dvp_ref, dqg_ref, dsk_ref, kc_ref, vc_ref, dqr_ref,
             dk_ref, dv_ref, s_ref, dp_ref, p_ref, ds_ref, qs_ref, dos_ref, kd_ref, vd_ref):
        s_id, n = pl.program_id(0), pl.program_id(1)

        @pl.when((s_id == 0) & (n == 0))
        def _():
            dqg_ref[...] = jnp.zeros_like(dqg_ref)
            dsk_ref[...] = jnp.zeros_like(dsk_ref)

        @pl.when(n == 0)
        def _():
            kc_ref[...] = jnp.zeros_like(kc_ref)
            vc_ref[...] = jnp.zeros_like(vc_ref)

        lane = lax.broadcasted_iota(jnp.int32, (WINDOW, W), 1)
        lane_k = lane[:, :KV_W]
        lane128 = lane[:, :LANES]
        cosf, sinv = jnp.tile(cos_ref[...], (1, W // LANES)), jnp.tile(sin_ref[...], (1, W // LANES))
        qv = q_ref[...]
        qr, q_rstd = _qk_prep(qv, qg_ref[...], cosf, sinv, iq_ref[...], iqt_ref[...], lane)
        kr, _ = _qk_prep(k_ref[...], kg_ref[...], cosf[:, :KV_W], sinv[:, :KV_W], ik_ref[...], ikt_ref[...], lane_k)
        kc_ref[WINDOW:2 * WINDOW, :] = kr
        vc_ref[WINDOW:2 * WINDOW, :] = v_ref[...]
        kc, vc = kc_ref[...], vc_ref[...]
        dov = do_ref[...]
        mask = jnp.tile(_attn_mask_t(n), (1, 4))
        lo = lane128 < HEAD_DIM
        lo2 = lax.broadcasted_iota(jnp.int32, (2 * WINDOW, LANES), 1) < HEAD_DIM
        scale = HEAD_DIM ** -0.5
        qr = qr * scale
        dk_ref[...] = jnp.zeros_like(dk_ref)
        dv_ref[...] = jnp.zeros_like(dv_ref)
        for kvh in range(N_KV):
            qs_ref[kvh] = _stack_heads(qr, kvh, lo)
            dos_ref[kvh] = _stack_heads(dov, kvh, lo)
            kd_ref[kvh] = _dup_head(kc, kvh, lo2)
            vd_ref[kvh] = _dup_head(vc, kvh, lo2)

        def scores(kvh):
            b = kvh % 2
            s_ref[b] = jnp.where(mask, _dot_nt(kd_ref[kvh], qs_ref[kvh]), -1e30)
            dp_ref[b] = _dot_nt(vd_ref[kvh], dos_ref[kvh])

        def softmax(kvh):
            b = kvh % 2
            for r in range(4):
                cols = slice(r * WINDOW, (r + 1) * WINDOW)
                head = 4 * kvh + r
                mx, inv, ps = _softmax_stats(s_ref, b, cols, sink_ref[head])
                g8 = None
                for c in KEY_CHUNKS:
                    t = _fold8(jnp.exp(s_ref[b, c, cols] - mx) * dp_ref[b, c, cols], jnp.sum)
                    g8 = t if g8 is None else g8 + t
                dd = jnp.sum(g8, axis=0, keepdims=True) * inv
                for c in KEY_CHUNKS:
                    p = jnp.exp(s_ref[b, c, cols] - mx) * inv
                    p_ref[b, c, cols] = p.astype(BF16)
                    ds_ref[b, c, cols] = (p * (dp_ref[b, c, cols] - dd)).astype(BF16)
                dsk_ref[head:head + 1, :] -= ps * dd

        def grads(kvh):
            m, b = kvh // 2, kvh % 2
            dq0, dq1 = _unstack_heads(_dot_tn(ds_ref[b], kd_ref[kvh]) * scale, lo)
            dqr_ref[:, (2 * kvh) * LANES:(2 * kvh + 1) * LANES] = dq0
            dqr_ref[:, (2 * kvh + 1) * LANES:(2 * kvh + 2) * LANES] = dq1
            dk_ref[:, m * LANES:(m + 1) * LANES] += _fold_head(_dot(ds_ref[b], qs_ref[kvh]), kvh, lo2)
            dv_ref[:, m * LANES:(m + 1) * LANES] += _fold_head(_dot(p_ref[b], dos_ref[kvh]), kvh, lo2)

        scores(0)
        for kvh in range(N_KV):
            if kvh + 1 < N_KV:
                scores(kvh + 1)
            softmax(kvh)
            grads(kvh)
        dq, dqg = _qk_prep_bwd(dqr_ref[...], qv, q_rstd, qg_ref[...], cosf, sinv, iq_ref[...], iqt_ref[...], lane)
        dq_ref[...] = dq.astype(BF16)
        dqg_ref[...] += dqg
        dkp_ref[...] = dk_ref[0:WINDOW, :]
        dkc_ref[...] = dk_ref[WINDOW:2 * WINDOW, :]
        dvp_ref[...] = dv_ref[0:WINDOW, :]
        dvc_ref[...] = dv_ref[WINDOW:2 * WINDOW, :]
        kc_ref[0:WINDOW, :] = kr
        vc_ref[0:WINDOW, :] = v_ref[...]

    blk = lambda w: pl.BlockSpec((WINDOW, w), lambda s, n: (s * nblk + n, 0))
    pos = pl.BlockSpec((WINDOW, LANES), lambda s, n: (n, 0))
    kv_out = jax.ShapeDtypeStruct((T, KV_W), F32)
    stage = lambda dt: pltpu.VMEM((2, 2 * WINDOW, 4 * WINDOW), dt)
    return _call(
        body, phases=phases, name="attn_bwd", grid=(n_seq, nblk),
        in_specs=[pl.BlockSpec(memory_space=pltpu.SMEM), blk(W), blk(W), blk(KV_W), blk(KV_W), _const((1, W)),
                  _const((1, KV_W)), pos, pos, _const((W, LANES)), _const((LANES, W)), _const((KV_W, LANES)),
                  _const((LANES, KV_W))],
        out_specs=[blk(W), blk(KV_W), blk(KV_W), blk(KV_W), blk(KV_W), _const((1, W)), _const((N_HEADS, LANES))],
        out_shape=[jax.ShapeDtypeStruct((T, W), BF16), kv_out, kv_out, kv_out, kv_out,
                   jax.ShapeDtypeStruct((1, W), F32), jax.ShapeDtypeStruct((N_HEADS, LANES), F32)],
        scratch_shapes=[pltpu.VMEM((2 * WINDOW, KV_W), F32), pltpu.VMEM((2 * WINDOW, KV_W), F32),
                        pltpu.VMEM((WINDOW, W), F32), pltpu.VMEM((2 * WINDOW, KV_W), F32),
                        pltpu.VMEM((2 * WINDOW, KV_W), F32), stage(F32), stage(F32), stage(BF16), stage(BF16),
                        pltpu.VMEM((N_KV, 4 * WINDOW, LANES), BF16), pltpu.VMEM((N_KV, 4 * WINDOW, LANES), BF16),
                        pltpu.VMEM((N_KV, 2 * WINDOW, LANES), BF16), pltpu.VMEM((N_KV, 2 * WINDOW, LANES), BF16)],
    )(sinks, do, q, k, v, qg, kg, cosf, sins, ind_q, ind_qt, ind_k, ind_kt)


def _kv_bwd(dkc, dkp, dvc, dvp, k, kg, cosf, sins, ind_k, ind_kt, n_seq, S, phases=()):
    T = k.shape[0]
    nblk = S // WINDOW

    def body(dkc_ref, dkp_ref, dvc_ref, dvp_ref, k_ref, kg_ref, cos_ref, sin_ref, ik_ref, ikt_ref,
             dk_ref, dv_ref, dkg_ref):
        s_id, n = pl.program_id(0), pl.program_id(1)

        @pl.when((s_id == 0) & (n == 0))
        def _():
            dkg_ref[...] = jnp.zeros_like(dkg_ref)

        has_next = n < nblk - 1
        lane = lax.broadcasted_iota(jnp.int32, (WINDOW, KV_W), 1)
        dkr = dkc_ref[...] + jnp.where(has_next, dkp_ref[...], 0.0)
        dv_ref[...] = (dvc_ref[...] + jnp.where(has_next, dvp_ref[...], 0.0)).astype(BF16)
        cosf, sinv = jnp.tile(cos_ref[...], (1, KV_W // LANES)), jnp.tile(sin_ref[...], (1, KV_W // LANES))
        kv = k_ref[...]
        _, rstd = _qk_prep(kv, kg_ref[...], cosf, sinv, ik_ref[...], ikt_ref[...], lane)
        dk, dkg = _qk_prep_bwd(dkr, kv, rstd, kg_ref[...], cosf, sinv, ik_ref[...], ikt_ref[...], lane)
        dk_ref[...] = dk.astype(BF16)
        dkg_ref[...] += dkg

    cur = pl.BlockSpec((WINDOW, KV_W), lambda s, n: (s * nblk + n, 0))
    nxt = pl.BlockSpec((WINDOW, KV_W), lambda s, n: (s * nblk + jnp.minimum(n + 1, nblk - 1), 0))
    pos = pl.BlockSpec((WINDOW, LANES), lambda s, n: (n, 0))
    return _call(
        body, phases=phases, name="kv_bwd", grid=(n_seq, nblk),
        in_specs=[cur, nxt, cur, nxt, cur, _const((1, KV_W)), pos, pos, _const((KV_W, LANES)),
                  _const((LANES, KV_W))],
        out_specs=[cur, cur, _const((1, KV_W))],
        out_shape=[jax.ShapeDtypeStruct((T, KV_W), BF16), jax.ShapeDtypeStruct((T, KV_W), BF16),
                   jax.ShapeDtypeStruct((1, KV_W), F32)],
    )(dkc, dkp, dvc, dvp, k, kg, cosf, sins, ind_k, ind_kt)


def _merge_fwd(x, ya, o, ga, gb, w_rnn, w_attn, w_out, tm, phases=()):
    T = x.shape[0]
    W = D_MODEL

    def body(x_ref, ya_ref, o_ref, ga_ref, gb_ref, wr_ref, wa_ref, wo_ref, x1_ref, mg_ref, yao_ref, ybo_ref):
        y_a = _dot(ya_ref[...], wr_ref[...])
        y_b = _dot(o_ref[...], wa_ref[...])
        yao_ref[...] = y_a
        ybo_ref[...] = y_b
        mg = (_sigmoid(ga_ref[...]) * y_a + _sigmoid(gb_ref[...]) * y_b).astype(BF16)
        mg_ref[...] = mg
        x1_ref[...] = x_ref[...] + _dot(mg, wo_ref[...])

    row = pl.BlockSpec((tm, W), lambda i: (i, 0))
    sq = _const((W, W))
    return _call(
        body, phases=phases, name="merge_fwd", grid=(T // tm,),
        in_specs=[row, row, row, row, row, sq, sq, sq], out_specs=[row, row, row, row],
        out_shape=[jax.ShapeDtypeStruct((T, W), F32), jax.ShapeDtypeStruct((T, W), BF16),
                   jax.ShapeDtypeStruct((T, W), F32), jax.ShapeDtypeStruct((T, W), F32)],
    )(x, ya, o, ga, gb, w_rnn, w_attn, w_out)


def _merge_bwd(dx1, ga, gb, y_a, y_b, w_rnn, w_attn, w_out, tm, phases=()):
    T = dx1.shape[0]
    W = D_MODEL

    def body(dx1_ref, ga_ref, gb_ref, ya_ref, yb_ref, wr_ref, wa_ref, wo_ref,
             dga_ref, dgb_ref, dya_ref, dyb_ref, dyain_ref, do_ref):
        dm = _dot_nt(dx1_ref[...].astype(BF16), wo_ref[...])
        sa = _sigmoid(ga_ref[...])
        sb = _sigmoid(gb_ref[...])
        dga_ref[...] = (dm * ya_ref[...] * (sa * (1.0 - sa))).astype(BF16)
        dgb_ref[...] = (dm * yb_ref[...] * (sb * (1.0 - sb))).astype(BF16)
        dya = (dm * sa).astype(BF16)
        dyb = (dm * sb).astype(BF16)
        dya_ref[...] = dya
        dyb_ref[...] = dyb
        dyain_ref[...] = _dot_nt(dya, wr_ref[...])
        do_ref[...] = _dot_nt(dyb, wa_ref[...])

    row = pl.BlockSpec((tm, W), lambda i: (i, 0))
    sq = _const((W, W))
    b16 = jax.ShapeDtypeStruct((T, W), BF16)
    f32 = jax.ShapeDtypeStruct((T, W), F32)
    return _call(
        body, phases=phases, name="merge_bwd", grid=(T // tm,),
        in_specs=[row, row, row, row, row, sq, sq, sq], out_specs=[row] * 6,
        out_shape=[b16, b16, b16, b16, f32, f32],
    )(dx1, ga, gb, y_a, y_b, w_rnn, w_attn, w_out)


def _mlp_fwd(x1, g_mlp, w_up, w_down, tm, phases=()):
    T = x1.shape[0]
    W = D_MODEL

    def body(x_ref, g_ref, wu_ref, wd_ref, x2_ref, hm_ref, u_ref, act_ref):
        xv = x_ref[...]
        hm, _ = _rms_fwd(xv, g_ref[...])
        hmb = hm.astype(BF16)
        hm_ref[...] = hmb
        for j in range(N_CHIPS):
            u = _dot(hmb, wu_ref[j])
            u_ref[:, j * W:(j + 1) * W] = u
            ru = jnp.maximum(u, 0.0)
            act_ref[:, j * W:(j + 1) * W] = (ru * ru).astype(BF16)
        x2_ref[...] = xv + _dot(act_ref[...], wd_ref[...])

    row = lambda w: pl.BlockSpec((tm, w), lambda i: (i, 0))
    return _call(
        body, phases=phases, name="mlp_fwd", grid=(T // tm,),
        in_specs=[row(W), _const((1, W)), _const((N_CHIPS, W, W)), _const((D_FF, W))],
        out_specs=[row(W), row(W), row(D_FF), row(D_FF)],
        out_shape=[jax.ShapeDtypeStruct((T, W), F32), jax.ShapeDtypeStruct((T, W), BF16),
                   jax.ShapeDtypeStruct((T, D_FF), F32), jax.ShapeDtypeStruct((T, D_FF), BF16)],
    )(x1, g_mlp, w_up, w_down)


def _mlp_bwd(dx2, u, x1, g_mlp, w_up, w_down, tm, phases=()):
    T = x1.shape[0]
    W = D_MODEL

    def body(dx2_ref, u_ref, x_ref, g_ref, wu_ref, wd_ref, dx1_ref, du_ref, dg_ref):
        @pl.when(pl.program_id(0) == 0)
        def _():
            dg_ref[...] = jnp.zeros_like(dg_ref)

        dx2 = dx2_ref[...]
        dact = _dot_nt(dx2.astype(BF16), wd_ref[...])
        du_ref[...] = (dact * (2.0 * jnp.maximum(u_ref[...], 0.0))).astype(BF16)
        dhm = jnp.zeros((tm, W), F32)
        for j in range(N_CHIPS):
            dhm = dhm + _dot_nt(du_ref[:, j * W:(j + 1) * W], wu_ref[j])
        xv = x_ref[...]
        g = g_ref[...]
        _, r = _rms_fwd(xv, g)
        dx, dg = _rms_bwd(dhm, xv, r, g)
        dx1_ref[...] = dx2 + dx
        dg_ref[...] += dg

    row = lambda w: pl.BlockSpec((tm, w), lambda i: (i, 0))
    return _call(
        body, phases=phases, name="mlp_bwd", grid=(T // tm,),
        in_specs=[row(W), row(D_FF), row(W), _const((1, W)), _const((N_CHIPS, W, W)), _const((D_FF, W))],
        out_specs=[row(W), row(D_FF), _const((1, W))],
        out_shape=[jax.ShapeDtypeStruct((T, W), F32), jax.ShapeDtypeStruct((T, D_FF), BF16),
                   jax.ShapeDtypeStruct((1, W), F32)],
    )(dx2, u, x1, g_mlp, w_up, w_down)


def _ple_loss(x2, p, target, g_ple, w_gate, w_proj, tm, phases=()):
    T = x2.shape[0]
    W = D_MODEL
    cw = W // N_CHIPS

    def body(x_ref, p_ref, t_ref, g_ref, wg_ref, wp_ref, loss_ref, dx2_ref, pb_ref, de_ref, hp_ref, dtg_ref, dg_ref):
        @pl.when(pl.program_id(0) == 0)
        def _():
            dg_ref[...] = jnp.zeros_like(dg_ref)
            loss_ref[...] = jnp.zeros_like(loss_ref)

        xv = x_ref[...]
        g = g_ref[...]
        pb = p_ref[...].astype(BF16)
        pb_ref[...] = pb
        e = jnp.concatenate([_dot(pb, wp_ref[j]) for j in range(N_CHIPS)], axis=1)
        hp, r = _rms_fwd(xv, g)
        hpb = hp.astype(BF16)
        hp_ref[...] = hpb
        sg = _sigmoid(_dot(hpb, wg_ref[...]))
        diff = (xv + e * sg) - t_ref[...]
        loss_ref[...] += jnp.sum(diff * diff) * (0.5 / W)
        dx3 = diff * (1.0 / W)
        de_ref[...] = (dx3 * sg).astype(BF16)
        dtg = (dx3 * e * (sg * (1.0 - sg))).astype(BF16)
        dtg_ref[...] = dtg
        dx, dg = _rms_bwd(_dot_nt(dtg, wg_ref[...]), xv, r, g)
        dx2_ref[...] = dx3 + dx
        dg_ref[...] += dg

    row = lambda w: pl.BlockSpec((tm, w), lambda i: (i, 0))
    b16 = lambda w: jax.ShapeDtypeStruct((T, w), BF16)
    return _call(
        body, phases=phases, name="ple_loss", grid=(T // tm,),
        in_specs=[row(W), row(PLE_DIM), row(W), _const((1, W)), _const((W, W)), _const((N_CHIPS, PLE_DIM, cw))],
        out_specs=[_const((8, LANES)), row(W), row(PLE_DIM), row(W), row(W), row(W), _const((1, W))],
        out_shape=[jax.ShapeDtypeStruct((8, LANES), F32), jax.ShapeDtypeStruct((T, W), F32), b16(PLE_DIM),
                   b16(W), b16(W), b16(W), jax.ShapeDtypeStruct((1, W), F32)],
    )(x2, p, target, g_ple, w_gate, w_proj)


def _adamw(w, g, m, v, name, tr, phases=()):
    R, C = w.shape
    c1 = 1.0 / (1.0 - ADAM_B1 ** ADAM_STEP)
    c2 = 1.0 / (1.0 - ADAM_B2 ** ADAM_STEP)

    def body(w_ref, g_ref, m_ref, v_ref, go_ref, d_ref, nm_ref, nv_ref):
        gv = g_ref[...]
        go_ref[...] = gv
        nm = ADAM_B1 * m_ref[...] + (1.0 - ADAM_B1) * gv
        nv = ADAM_B2 * v_ref[...] + (1.0 - ADAM_B2) * (gv * gv)
        nm_ref[...] = nm
        nv_ref[...] = nv
        d_ref[...] = (-ADAM_LR) * ((nm * c1) / (jnp.sqrt(nv * c2) + ADAM_EPS) + ADAM_WD * w_ref[...])

    row = pl.BlockSpec((tr, C), lambda i: (i, 0))
    sds = jax.ShapeDtypeStruct((R, C), F32)
    return _call(
        body, phases=phases, name=name, grid=(R // tr,), in_specs=[row] * 4, out_specs=[row] * 4,
        out_shape=[sds] * 4,
    )(w, g, m, v)


def _indicator(width):
    ind = np.zeros((width, LANES), np.float32)
    ind[np.arange(width), np.arange(width) // HEAD_DIM] = 1.0
    return jnp.asarray(ind, BF16), jnp.asarray(ind.T, BF16)


def _rope_tables(S):
    inv = ROPE_THETA ** (-jnp.arange(0, HEAD_DIM, 2, dtype=F32) / HEAD_DIM)
    ang = jnp.arange(S, dtype=F32)[:, None] * inv[None, :]
    cos, sin = jnp.cos(ang), jnp.sin(ang)
    cosf = jnp.tile(jnp.concatenate([cos, cos], axis=1), (1, LANES // HEAD_DIM))
    sins = jnp.tile(jnp.concatenate([-sin, sin], axis=1), (1, LANES // HEAD_DIM))
    return cosf, sins


def _pair_blockdiag(w):
    w4 = w.reshape(8, 2, HEAD_DIM, HEAD_DIM)
    eye = jnp.eye(2, dtype=w.dtype)
    return jnp.einsum("bpij,pq->bpiqj", w4, eye).reshape(8, LANES, LANES)


def _pair_blockdiag_extract(g):
    g5 = g.reshape(8, 2, HEAD_DIM, 2, HEAD_DIM)
    return jnp.stack([g5[:, 0, :, 0, :], g5[:, 1, :, 1, :]], axis=1).reshape(16, HEAD_DIM, HEAD_DIM)


def _pair_sum(parts, sibs, name):
    n = len(parts)
    dims = [(p.shape[1] // 2, p.shape[2]) for p in parts]

    def body(*refs):
        p_r, s_r, send_r, own_r, mine_r, sem = (refs[0:n], refs[n:2 * n], refs[2 * n:3 * n], refs[3 * n:4 * n],
                                                refs[4 * n:5 * n], refs[5 * n])
        x, y, c, chips = _mesh_pos()
        me = 2 * x + y
        loads = []
        for i, (R, _) in enumerate(dims):
            mine, _ = _half_rows(c, R)
            cp = pltpu.make_async_copy(p_r[i].at[:, mine, :], mine_r[i], sem.at[i])
            cp.start()
            loads.append(cp)
        for i in range(n):
            loads[i].wait()
            for j, (cx, cy) in enumerate(chips):
                k = 2 * cx + cy
                send_r[i][j] = (mine_r[i][k] + s_r[i][k]).astype(BF16)
            own_r[i][...] = mine_r[i][me] + s_r[i][me]

    vm = pl.BlockSpec(memory_space=pltpu.VMEM)
    out = pl.pallas_call(
        body, name=name, in_specs=[pl.BlockSpec(memory_space=pl.ANY)] * n + [vm] * n, out_specs=[vm] * (2 * n),
        out_shape=[jax.ShapeDtypeStruct((3, R, C), BF16) for R, C in dims]
        + [jax.ShapeDtypeStruct((R, C), F32) for R, C in dims],
        scratch_shapes=[pltpu.VMEM((N_CHIPS, R, C), F32) for R, C in dims] + [pltpu.SemaphoreType.DMA((n,))],
        compiler_params=pltpu.CompilerParams(vmem_limit_bytes=VMEM_LIMIT),
    )(*parts, *sibs)
    return out[:n], out[n:]


def _chip_sum(owns, recvs, name):
    n = len(owns)
    dims = [o.shape for o in owns]

    def body(*refs):
        own_r, recv_r, red_r, stage_r, sem = refs[0:n], refs[n:2 * n], refs[2 * n:3 * n], refs[3 * n:4 * n], refs[4 * n]
        x, y, c, _ = _mesh_pos()
        me = 2 * x + y
        stores = []
        for i, (R, _) in enumerate(dims):
            for k_me in range(N_CHIPS):

                @pl.when(me == k_me)
                def _():
                    acc = None
                    for k in range(N_CHIPS):
                        slot = ((k // 2) ^ (k_me // 2)) + 2 * ((k % 2) ^ (k_me % 2)) - 1
                        term = own_r[i][...] if k == k_me else recv_r[i][slot].astype(F32)
                        acc = term if acc is None else acc + term
                    stage_r[i][...] = acc

            mine, _ = _half_rows(c, R)
            cp = pltpu.make_async_copy(stage_r[i], red_r[i].at[mine, :], sem.at[i])
            cp.start()
            stores.append(cp)
        for cp in stores:
            cp.wait()

    vm = pl.BlockSpec(memory_space=pltpu.VMEM)
    return pl.pallas_call(
        body, name=name, in_specs=[vm] * (2 * n), out_specs=[pl.BlockSpec(memory_space=pl.ANY)] * n,
        out_shape=[jax.ShapeDtypeStruct((2 * R, C), F32) for R, C in dims],
        scratch_shapes=[pltpu.VMEM((R, C), F32) for R, C in dims] + [pltpu.SemaphoreType.DMA((n,))],
        compiler_params=pltpu.CompilerParams(vmem_limit_bytes=VMEM_LIMIT),
    )(*owns, *recvs)


def _gather_bf16(shard, name):
    R2, C = shard.shape
    R = R2 // 2

    def body(s_ref, o_ref, send_sems, recv_sems):
        x, y, c, chips = _mesh_pos()
        me = 2 * x + y
        mine = pl.ds(pl.multiple_of(c * R, R), R)
        theirs = pl.ds(pl.multiple_of((1 - c) * R, R), R)
        o_ref[me] = s_ref[...].astype(BF16)

        def copy(k, chip, rows, to):
            blk = o_ref.at[chip, rows]
            return pltpu.make_async_remote_copy(src_ref=blk, dst_ref=blk, send_sem=send_sems.at[k],
                                                recv_sem=recv_sems.at[k], device_id=to, device_id_type=MESH_ID)

        first = [copy(j, me, mine, (cx, cy, c)) for j, (cx, cy) in enumerate(chips)]
        for cp in first:
            cp.start()
        passed = []
        for j, (cx, cy) in enumerate(chips):
            copy(j, 2 * cx + cy, mine, (x, y, c)).wait_recv()
            cp = copy(3 + j, 2 * cx + cy, mine, (x, y, 1 - c))
            cp.start()
            passed.append(cp)
        for j, (cx, cy) in enumerate(chips):
            copy(3 + j, 2 * cx + cy, theirs, (x, y, c)).wait_recv()
        for cp in first + passed:
            cp.wait_send()

    return pl.pallas_call(
        body, name=name, out_shape=jax.ShapeDtypeStruct((N_CHIPS, R2, C), BF16),
        in_specs=[pl.BlockSpec(memory_space=pltpu.VMEM)], out_specs=pl.BlockSpec(memory_space=pltpu.VMEM),
        scratch_shapes=[pltpu.SemaphoreType.DMA((6,)), pltpu.SemaphoreType.DMA((6,))],
        compiler_params=pltpu.CompilerParams(vmem_limit_bytes=VMEM_LIMIT),
    )(shard)


def _pair_exchange_sum(partial, name):
    _, R2, C = partial.shape
    R = R2 // 2

    def body(p_ref, send_ref, own_ref, mine_ref, sib_ref, loc_sems, send_sems, recv_sems):
        x, y, c, chips = _mesh_pos()
        me = 2 * x + y
        mine, theirs = _half_rows(c, R)
        order = [2 * cx + cy for cx, cy in chips] + [me]
        locs, pairs = [], []
        for i, k in enumerate(order):
            loc = pltpu.make_async_copy(p_ref.at[k, mine, :], mine_ref.at[i], loc_sems.at[i])
            pair = _remote(p_ref.at[k, theirs, :], sib_ref.at[i], (send_sems.at[i], recv_sems.at[i]), (x, y, 1 - c))
            loc.start()
            pair.start()
            locs.append(loc)
            pairs.append(pair)
        for i in range(N_CHIPS):
            locs[i].wait()
            pairs[i].wait_recv()
            total = mine_ref[i] + sib_ref[i]
            if i < 3:
                send_ref[i] = total.astype(BF16)
            else:
                own_ref[...] = total
        for pair in pairs:
            pair.wait_send()

    vm = pl.BlockSpec(memory_space=pltpu.VMEM)
    return pl.pallas_call(
        body, name=name, in_specs=[pl.BlockSpec(memory_space=pl.ANY)], out_specs=[vm, vm],
        out_shape=[jax.ShapeDtypeStruct((3, R, C), BF16), jax.ShapeDtypeStruct((R, C), F32)],
        scratch_shapes=[pltpu.VMEM((N_CHIPS, R, C), F32), pltpu.VMEM((N_CHIPS, R, C), F32),
                        pltpu.SemaphoreType.DMA((N_CHIPS,)), pltpu.SemaphoreType.DMA((N_CHIPS,)),
                        pltpu.SemaphoreType.DMA((N_CHIPS,))],
        compiler_params=pltpu.CompilerParams(vmem_limit_bytes=VMEM_LIMIT),
    )(partial)


def _allreduce_small(buf, name):
    rows, width = buf.shape
    h = rows // 2

    def body(b_ref, o_ref, sib_ref, pair_ref, in_ref, pair_sems, send_sems, recv_sems, fin_sems):
        x, y, c, chips = _mesh_pos()
        me = 2 * x + y
        mine, theirs = _half_rows(c, h)
        sibling = (x, y, 1 - c)
        pair = _remote(b_ref.at[theirs], sib_ref, (pair_sems.at[0], pair_sems.at[1]), sibling)
        pair.start()
        pair.wait()
        pair_ref[...] = b_ref[mine, :] + sib_ref[...]
        sends = []
        for j, (cx, cy) in enumerate(chips):
            cp = _remote(pair_ref, in_ref.at[j], (send_sems.at[j], recv_sems.at[j]), (cx, cy, c))
            cp.start()
            sends.append(cp)
        for cp in sends:
            cp.wait_recv()
        acc = None
        for k in range(N_CHIPS):
            term = jnp.where(me == k, pair_ref[...], in_ref[_peer_slot(k, x, y)])
            acc = term if acc is None else acc + term
        o_ref[mine, :] = acc
        fin = _remote(o_ref.at[mine], o_ref.at[mine], (fin_sems.at[0], fin_sems.at[1]), sibling)
        fin.start()
        fin.wait_send()
        _remote(o_ref.at[theirs], o_ref.at[theirs], (fin_sems.at[0], fin_sems.at[1]), sibling).wait_recv()
        for cp in sends:
            cp.wait_send()

    return pl.pallas_call(
        body, name=name, out_shape=jax.ShapeDtypeStruct((rows, width), F32),
        in_specs=[pl.BlockSpec(memory_space=pltpu.VMEM)], out_specs=pl.BlockSpec(memory_space=pltpu.VMEM),
        scratch_shapes=[pltpu.VMEM((h, width), F32), pltpu.VMEM((h, width), F32), pltpu.VMEM((3, h, width), F32),
                        pltpu.SemaphoreType.DMA((2,)), pltpu.SemaphoreType.DMA((3,)), pltpu.SemaphoreType.DMA((3,)),
                        pltpu.SemaphoreType.DMA((2,))],
        compiler_params=pltpu.CompilerParams(vmem_limit_bytes=VMEM_LIMIT),
    )(buf)


def _adamw_small(ws, gs, ms, vs):
    n = len(ws)
    c1 = 1.0 / (1.0 - ADAM_B1 ** ADAM_STEP)
    c2 = 1.0 / (1.0 - ADAM_B2 ** ADAM_STEP)

    def body(*refs):
        w_r, g_r, m_r, v_r = refs[0:n], refs[n:2 * n], refs[2 * n:3 * n], refs[3 * n:4 * n]
        d_r, nm_r, nv_r = refs[4 * n:5 * n], refs[5 * n:6 * n], refs[6 * n:7 * n]
        for i in range(n):
            gv = g_r[i][...]
            nm = ADAM_B1 * m_r[i][...] + (1.0 - ADAM_B1) * gv
            nv = ADAM_B2 * v_r[i][...] + (1.0 - ADAM_B2) * (gv * gv)
            nm_r[i][...] = nm
            nv_r[i][...] = nv
            d_r[i][...] = (-ADAM_LR) * ((nm * c1) / (jnp.sqrt(nv * c2) + ADAM_EPS) + ADAM_WD * w_r[i][...])

    vm = pl.BlockSpec(memory_space=pltpu.VMEM)
    sds = [jax.ShapeDtypeStruct(w.shape, F32) for w in ws]
    out = pl.pallas_call(body, name="adamw_small", in_specs=[vm] * (4 * n), out_specs=[vm] * (3 * n),
                         out_shape=sds * 3)(*ws, *gs, *ms, *vs)
    return out[0:n], out[n:2 * n], out[2 * n:3 * n]


_BIG = ("w_in", "w_rnn_proj", "w_attn_proj", "w_out", "w_up", "w_down", "w_ple_gate", "w_ple_proj")
_SMALL = ("g_mix", "conv_w", "conv_b", "w_rg", "b_rg", "w_ig", "b_ig", "lru_lambda", "q_gain", "k_gain", "sinks",
          "g_mlp", "g_ple")
_WEIGHTS = ("g_mix", "w_in", "conv_w", "conv_b", "w_rg", "b_rg", "w_ig", "b_ig", "lru_lambda", "w_rnn_proj",
            "q_gain", "k_gain", "sinks", "w_attn_proj", "w_out", "g_mlp", "w_up", "w_down", "g_ple", "w_ple_gate",
            "w_ple_proj")


def _pad_row(v):
    v = v.reshape(1, -1)
    return jnp.pad(v, ((0, 0), (0, D_MODEL - v.shape[1])))


def kernel(x, p, g_mix, w_in, conv_w, conv_b, w_rg, b_rg, w_ig, b_ig, lru_lambda, w_rnn_proj, q_gain, k_gain, sinks, w_attn_proj, w_out, g_mlp, w_up, w_down, g_ple, w_ple_gate, w_ple_proj, loss_target, m_g_mix, m_w_in, m_conv_w, m_conv_b, m_w_rg, m_b_rg, m_w_ig, m_b_ig, m_lru_lambda, m_w_rnn_proj, m_q_gain, m_k_gain, m_sinks, m_w_attn_proj, m_w_out, m_g_mlp, m_w_up, m_w_down, m_g_ple, m_w_ple_gate, m_w_ple_proj, v_g_mix, v_w_in, v_conv_w, v_conv_b, v_w_rg, v_b_rg, v_w_ig, v_b_ig, v_lru_lambda, v_w_rnn_proj, v_q_gain, v_k_gain, v_sinks, v_w_attn_proj, v_w_out, v_g_mlp, v_w_up, v_w_down, v_g_ple, v_w_ple_gate, v_w_ple_proj):
    w = dict(g_mix=g_mix, w_in=w_in, conv_w=conv_w, conv_b=conv_b, w_rg=w_rg, b_rg=b_rg, w_ig=w_ig, b_ig=b_ig,
             lru_lambda=lru_lambda, w_rnn_proj=w_rnn_proj, q_gain=q_gain, k_gain=k_gain, sinks=sinks,
             w_attn_proj=w_attn_proj, w_out=w_out, g_mlp=g_mlp, w_up=w_up, w_down=w_down, g_ple=g_ple,
             w_ple_gate=w_ple_gate, w_ple_proj=w_ple_proj)
    m = dict(g_mix=m_g_mix, w_in=m_w_in, conv_w=m_conv_w, conv_b=m_conv_b, w_rg=m_w_rg, b_rg=m_b_rg, w_ig=m_w_ig,
             b_ig=m_b_ig, lru_lambda=m_lru_lambda, w_rnn_proj=m_w_rnn_proj, q_gain=m_q_gain, k_gain=m_k_gain,
             sinks=m_sinks, w_attn_proj=m_w_attn_proj, w_out=m_w_out, g_mlp=m_g_mlp, w_up=m_w_up, w_down=m_w_down,
             g_ple=m_g_ple, w_ple_gate=m_w_ple_gate, w_ple_proj=m_w_ple_proj)
    v = dict(g_mix=v_g_mix, w_in=v_w_in, conv_w=v_conv_w, conv_b=v_conv_b, w_rg=v_w_rg, b_rg=v_b_rg, w_ig=v_w_ig,
             b_ig=v_b_ig, lru_lambda=v_lru_lambda, w_rnn_proj=v_w_rnn_proj, q_gain=v_q_gain, k_gain=v_k_gain,
             sinks=v_sinks, w_attn_proj=v_w_attn_proj, w_out=v_w_out, g_mlp=v_g_mlp, w_up=v_w_up, w_down=v_w_down,
             g_ple=v_g_ple, w_ple_gate=v_w_ple_gate, w_ple_proj=v_w_ple_proj)
    n_seq, S, _ = x.shape
    T = n_seq * S
    chip = 2 * lax.axis_index("x") + lax.axis_index("y")

    tm, tm_rnn = 512, 256
    xf, pf, tf = x.reshape(T, D_MODEL), p.reshape(T, PLE_DIM), loss_target.reshape(T, D_MODEL)
    first = lambda outs: [o[0] for o in outs]

    w_in_g = _gather_bf16(w["w_in"][0], "gather_w_in")
    wb = {name: w[name][0].astype(BF16) for name in _BIG if name != "w_in"}
    grp_mix, grp_mlp, grp_ple = ("w_rnn_proj", "w_attn_proj", "w_out"), ("w_up", "w_down"), ("w_ple_gate", "w_ple_proj")

    wb["conv_w"] = jnp.pad(conv_w[0], ((0, 16 - CONV_W), (0, 0)))

    cosf, sins = _rope_tables(S)
    ind_q, ind_qt = _indicator(D_MODEL)
    ind_k, ind_kt = _indicator(KV_W)
    wrg2 = _pair_blockdiag(w_rg[0]).astype(BF16)
    wig2 = _pair_blockdiag(w_ig[0]).astype(BF16)
    qg = jnp.tile(q_gain, (1, N_HEADS))
    kg = jnp.tile(k_gain, (1, N_KV))
    sk = sinks.reshape(N_HEADS)
    attn_c = (qg, kg, sk, cosf, sins, ind_q, ind_qt, ind_k, ind_kt, n_seq, S)

    (h0, xr, gr, zq, zk, zv, ga, gb), ph = _inproj_fwd(xf, g_mix, w_in_g, tm,
                                                     phases=[_ph_gather_send(wb[n])
                                                             for n in grp_mix + grp_ple + ("conv_w",)])
    g_small = first(ph)
    o, ph = _attn_fwd(zq, zk, zv, *attn_c,
                      phases=[_ph_gather_pass(g) for g in g_small] + [_ph_gather_send(wb["w_up"])])
    g_small, wu = first(ph[:6]), ph[6][0]
    cw_full = g_small[5][:, :CONV_W, :].transpose(1, 0, 2).reshape(CONV_W, D_MODEL)
    rnn_w = (cw_full, conv_b, wrg2, b_rg, wig2, b_ig, lru_lambda)
    (xc, h, *gates, ya), ph = _rnn_fwd(xr, gr, *rnn_w, n_seq, S, tm_rnn,
                               phases=[_ph_gather_pass(wu), _ph_gather_send(wb["w_down"])])
    wu, wd = ph[0][0], ph[1][0]
    wr, wa, wo, wpg = (g.reshape(D_MODEL, D_MODEL) for g in g_small[:4])
    wpp = g_small[4]
    (x1, merged, y_a, y_b), ph = _merge_fwd(xf, ya, o, ga, gb, wr, wa, wo, tm, phases=[_ph_gather_pass(wd)])
    wd = ph[0][0].reshape(D_FF, D_MODEL)
    (x2, hm, u, act), _ = _mlp_fwd(x1, g_mlp, wu, wd, tm // 2)
    (loss_t, dx2, pb, de, hp, dtg, dg_ple), _ = _ple_loss(x2, pf, tf, g_ple, wpg, wpp, tm)

    chipmajor = lambda g: g.reshape(N_CHIPS, g.shape[-2] // N_CHIPS, g.shape[-1]) if g.ndim == 2 else g
    tmw = min(2 * tm, T)
    dw_pp = _wgrad(pb, de, "wgrad_ple_proj", False, D_MODEL, tmw)[0]
    part_ple = [chipmajor(_wgrad(hp, dtg, "wgrad_ple_gate", False, D_MODEL, tmw)[0]),
                dw_pp.reshape(PLE_DIM, N_CHIPS, D_MODEL // N_CHIPS).transpose(1, 0, 2)]
    (dx1, du, dg_mlp), ph = _mlp_bwd(dx2, u, x1, g_mlp, wu, wd, tm // 2, phases=[_ph_pair_send(g) for g in part_ple])
    send_ple, own_ple = _pair_sum(part_ple, first(ph), "pair_sum_ple")
    dw_down, ph = _wgrad(act, dx2, "wgrad_down", False, D_MODEL // 2, tmw, phases=[_ph_chip_send(s) for s in send_ple])
    red_ple = _chip_sum(own_ple, first(ph), "chip_sum_ple")
    part_mlp = [_wgrad(hm, du, "wgrad_up", True, D_MODEL, tmw)[0], chipmajor(dw_down)]
    (dga, dgb, dya, dyb, dyain, do), _ = _merge_bwd(dx1, ga, gb, y_a, y_b, wr, wa, wo, tm)
    dw_rnn, ph_up = _wgrad(ya, dya, "wgrad_rnn_proj", False, D_MODEL, tmw, phases=[_ph_pair_send(part_mlp[0])])
    dw_attn, ph_down = _wgrad(o, dyb, "wgrad_attn_proj", False, D_MODEL, tmw, phases=[_ph_pair_send(part_mlp[1])])
    dw_out, ph = _wgrad(merged, dx1, "wgrad_out", False, D_MODEL, tmw, phases=[_ph_half_swap(r) for r in red_ple])
    red_ple = first(ph)
    send_mlp, own_mlp = _pair_sum(part_mlp, [ph_up[0][0], ph_down[0][0]], "pair_sum_mlp")
    part_mix = [chipmajor(dw_rnn), chipmajor(dw_attn), chipmajor(dw_out)]
    (dxr, dgr, vec, dwrg2, dwig2), ph = _rnn_bwd(
        dyain, xr, gr, xc, h, gates, cw_full, wrg2, wig2, lru_lambda, n_seq, S, tm_rnn,
        phases=[_ph_chip_send(s) for s in send_mlp] + [_ph_pair_send(g) for g in part_mix])
    red_mlp = _chip_sum(own_mlp, first(ph[:2]), "chip_sum_mlp")
    send_mix, own_mix = _pair_sum(part_mix, first(ph[2:]), "pair_sum_mix")
    (dq, dkc, dkp, dvc, dvp, dqg, dsk), ph = _attn_bwd(
        do, zq, zk, zv, *attn_c, phases=[_ph_half_swap(r) for r in red_mlp] + [_ph_chip_send(s) for s in send_mix])
    red_mlp = first(ph[:2])
    red_mix = _chip_sum(own_mix, first(ph[2:]), "chip_sum_mix")
    (dk, dv, dkg), _ = _kv_bwd(dkc, dkp, dvc, dvp, zk, kg, cosf, sins, ind_k, ind_kt, n_seq, S)
    dz_parts = [dxr, dgr, dq, dk, dv, dga, dgb]
    send_in, own_in = _pair_exchange_sum(_wgrad_in(h0, dz_parts, tm), "pair_sum_in")
    (grad_x, dg_mix), ph = _inproj_bwd(dz_parts, w_in_g, xf, g_mix, dx1, tm,
                                       phases=[_ph_half_swap(r) for r in red_mix] + [_ph_chip_send(send_in)])
    red_mix = first(ph[:3])
    red_in = _chip_sum([own_in], first(ph[3:]), "chip_sum_in")
    reduced = dict(zip(grp_ple + grp_mlp + grp_mix, red_ple + red_mlp + red_mix))
    grads = {
        "g_mix": dg_mix[0], "g_mlp": dg_mlp[0], "g_ple": dg_ple[0],
        "conv_w": vec[0:CONV_W], "conv_b": vec[4], "b_rg": vec[5], "b_ig": vec[6], "lru_lambda": vec[7],
        "w_rg": _pair_blockdiag_extract(dwrg2), "w_ig": _pair_blockdiag_extract(dwig2),
        "q_gain": dqg.reshape(N_HEADS, HEAD_DIM).sum(0), "k_gain": dkg.reshape(N_KV, HEAD_DIM).sum(0),
        "sinks": dsk.sum(1),
    }

    rows = [grads["conv_w"], _pad_row(grads["conv_b"]), _pad_row(grads["b_rg"]), _pad_row(grads["b_ig"]),
            _pad_row(grads["lru_lambda"]), _pad_row(grads["g_mix"]), _pad_row(grads["g_mlp"]),
            _pad_row(grads["g_ple"]), _pad_row(grads["q_gain"]), _pad_row(grads["k_gain"]), _pad_row(grads["sinks"]),
            _pad_row(loss_t[0:1, 0:1]), jnp.zeros((1, D_MODEL), F32)]
    vecs = jnp.concatenate(rows, axis=0)
    packed = jnp.concatenate([vecs.reshape(-1, LANES), grads["w_rg"].reshape(-1, LANES),
                              grads["w_ig"].reshape(-1, LANES)], axis=0)
    red = _allreduce_small(packed, "allreduce_small")
    nv = vecs.size // LANES
    rvec = red[0:nv].reshape(16, D_MODEL)
    loss = rvec[14, 0]
    nw = grads["w_rg"].size // LANES
    sg = {
        "conv_w": lax.dynamic_slice(rvec[0:CONV_W], (0, chip * (D_MODEL // N_CHIPS)), (CONV_W, D_MODEL // N_CHIPS)),
        "conv_b": rvec[4], "b_rg": rvec[5], "b_ig": rvec[6], "lru_lambda": rvec[7], "g_mix": rvec[8],
        "g_mlp": rvec[9], "g_ple": rvec[10], "q_gain": rvec[11, :HEAD_DIM], "k_gain": rvec[12, :HEAD_DIM],
        "sinks": rvec[13, :N_HEADS], "w_rg": red[nv:nv + nw], "w_ig": red[nv + nw:nv + 2 * nw],
    }
    sg = {k: sg[k].reshape(w[k].shape) for k in _SMALL}
    d_s, m_s, v_s = _adamw_small([w[k] for k in _SMALL], [sg[k] for k in _SMALL], [m[k] for k in _SMALL],
                                 [v[k] for k in _SMALL])
    grad, delta, new_m, new_v = dict(sg), dict(zip(_SMALL, d_s)), dict(zip(_SMALL, m_s)), dict(zip(_SMALL, v_s))

    for name in ("w_ple_proj", "w_up", "w_down", "w_rnn_proj", "w_attn_proj", "w_out", "w_ple_gate", "w_in"):
        shape = w[name].shape
        outs, ph = _adamw(w[name][0], reduced[name], m[name][0], v[name][0], "adamw_" + name, 128,
                          phases=[_ph_half_swap(r) for r in red_in] if name == "w_ple_proj" else ())
        if name == "w_ple_proj":
            reduced["w_in"] = ph[0][0]
        grad[name], delta[name], new_m[name], new_v[name] = (a.reshape(shape) for a in outs)

    return (loss, grad_x.reshape(x.shape), *[grad[k] for k in _WEIGHTS], *[delta[k] for k in _WEIGHTS],
            *[new_m[k] for k in _WEIGHTS], *[new_v[k] for k in _WEIGHTS])
```

```python
import functools
import math

import numpy as np
import jax
import jax.numpy as jnp
from jax import lax
from jax.experimental import pallas as pl
from jax.experimental.pallas import tpu as pltpu

F32 = jnp.float32
BF16 = jnp.bfloat16

D_MODEL = 1024
N_HEADS = 16
N_KV = 4
HEAD_DIM = 64
KV_W = N_KV * HEAD_DIM
D_FF = 4096
PLE_DIM = 256
WINDOW = 128
CONV_W = 4
LRU_C = 8.0
NORM_EPS = 1e-6
ROPE_THETA = 10000.0
N_CHIPS = 4
IN_TOTAL = 5632
IN_BLK = IN_TOTAL // N_CHIPS
IN_SEGS = (0, 1024, 2048, 3072, 3328, 3584, 4608, 5632)

ADAM_LR = 0.001
ADAM_B1 = 0.9
ADAM_B2 = 0.999
ADAM_EPS = 1e-08
ADAM_WD = 0.01
ADAM_STEP = 10

LANES = 128
VMEM_LIMIT = 56 * 1024 * 1024
MESH_ID = pl.DeviceIdType.MESH


def _dot(a, b):
    return jnp.dot(a, b, preferred_element_type=F32)


def _dot_nt(a, b):
    return lax.dot_general(a, b, (((1,), (1,)), ((), ())), preferred_element_type=F32)


def _dot_tn(a, b):
    return lax.dot_general(a, b, (((0,), (0,)), ((), ())), preferred_element_type=F32)


def _split_dot(x, ind):
    hi = x.astype(BF16)
    lo = (x - hi.astype(F32)).astype(BF16)
    return _dot(hi, ind) + _dot(lo, ind)


def _sigmoid(x):
    return 1.0 / (1.0 + jnp.exp(-x))


_GELU_C = math.sqrt(2.0 / math.pi)


def _gelu_and_grad(g):
    inner = _GELU_C * (g + 0.044715 * g * g * g)
    t = jnp.tanh(inner)
    gelu = 0.5 * g * (1.0 + t)
    dgelu = 0.5 * (1.0 + t) + 0.5 * g * (1.0 - t * t) * _GELU_C * (1.0 + 3.0 * 0.044715 * g * g)
    return gelu, dgelu


def _const(shape):
    nd = len(shape)
    return pl.BlockSpec(shape, lambda *_: (0,) * nd)


def _params(n_grid, vmem=VMEM_LIMIT):
    return pltpu.CompilerParams(dimension_semantics=("arbitrary",) * n_grid, vmem_limit_bytes=vmem)


def _rms_fwd(x, g):
    r = lax.rsqrt(jnp.mean(x * x, axis=-1, keepdims=True) + NORM_EPS)
    return (x * r) * g, r


def _rms_bwd(dy, x, r, g):
    dn = dy * g
    dx = r * dn - x * (r * r * r * jnp.mean(dn * x, axis=-1, keepdims=True))
    dg = jnp.sum(dy * (x * r), axis=0, keepdims=True)
    return dx, dg


def _seg_pieces(blk_lo, blk_hi):
    out = []
    for s in range(7):
        lo, hi = max(blk_lo, IN_SEGS[s]), min(blk_hi, IN_SEGS[s + 1])
        if lo < hi:
            out.append((s, lo - IN_SEGS[s], hi - IN_SEGS[s], lo - blk_lo))
    return out


def _mesh_pos():
    x, y, c = lax.axis_index("x"), lax.axis_index("y"), lax.axis_index("c")
    other_chips = [(1 - x, y), (x, 1 - y), (1 - x, 1 - y)]
    return x, y, c, other_chips


def _peer_slot(k, x, y):
    dx = jnp.bitwise_xor(k // 2, x)
    dy = jnp.bitwise_xor(k % 2, y)
    return jnp.maximum(dx + 2 * dy - 1, 0)


def _half_rows(c, R):
    return pl.ds(pl.multiple_of(c * R, R), R), pl.ds(pl.multiple_of((1 - c) * R, R), R)


def _remote(src, dst, sems, to):
    return pltpu.make_async_remote_copy(src_ref=src, dst_ref=dst, send_sem=sems[0], recv_sem=sems[1],
                                        device_id=to, device_id_type=MESH_ID)


class _Phase:
    def __init__(self, ins, inout, outs, n_remote, n_local, build):
        self.ins, self.inout, self.outs = list(ins), list(inout), list(outs)
        self.n_remote, self.n_local, self.build = n_remote, n_local, build


def _ph_gather_send(wb):
    R2, C = wb.shape
    R = R2 // 2

    def build(ins, outs, rsem, lsem):
        (w_ref,), (g_ref,) = ins, outs
        x, y, c, chips = _mesh_pos()
        me = 2 * x + y
        mine, _ = _half_rows(c, R)
        loc = [pltpu.make_async_copy(w_ref, g_ref.at[me], lsem(0))]
        outg = [_remote(w_ref.at[mine], g_ref.at[me, mine], rsem(j), (cx, cy, c)) for j, (cx, cy) in enumerate(chips)]
        inc = [functools.partial(_remote, w_ref.at[mine], g_ref.at[2 * cx + cy, mine], rsem(j), (x, y, c))
               for j, (cx, cy) in enumerate(chips)]
        return loc, outg, inc

    return _Phase([wb], [], [jax.ShapeDtypeStruct((N_CHIPS, R2, C), wb.dtype)], 3, 1, build)


def _ph_gather_pass(gath):
    _, R2, C = gath.shape
    R = R2 // 2

    def build(ins, outs, rsem, lsem):
        (g_ref,) = outs
        x, y, c, chips = _mesh_pos()
        mine, theirs = _half_rows(c, R)
        outg, inc = [], []
        for j, (cx, cy) in enumerate(chips):
            blk = g_ref.at[2 * cx + cy, mine]
            outg.append(_remote(blk, blk, rsem(j), (x, y, 1 - c)))
            got = g_ref.at[2 * cx + cy, theirs]
            inc.append(functools.partial(_remote, got, got, rsem(j), (x, y, c)))
        return [], outg, inc

    return _Phase([], [gath], [], 3, 0, build)


def _ph_pair_send(partial):
    _, R2, C = partial.shape
    R = R2 // 2

    def build(ins, outs, rsem, lsem):
        (p_ref,), (s_ref,) = ins, outs
        x, y, c, _ = _mesh_pos()
        _, theirs = _half_rows(c, R)
        src = p_ref.at[:, theirs, :]
        return ([], [_remote(src, s_ref, rsem(0), (x, y, 1 - c))],
                [functools.partial(_remote, src, s_ref, rsem(0), (x, y, c))])

    return _Phase([partial], [], [jax.ShapeDtypeStruct((N_CHIPS, R, C), F32)], 1, 0, build)


def _ph_chip_send(sendb):
    def build(ins, outs, rsem, lsem):
        (s_ref,), (r_ref,) = ins, outs
        x, y, c, chips = _mesh_pos()
        outg = [_remote(s_ref.at[j], r_ref.at[j], rsem(j), (cx, cy, c)) for j, (cx, cy) in enumerate(chips)]
        inc = [functools.partial(_remote, s_ref.at[j], r_ref.at[j], rsem(j), (x, y, c)) for j in range(3)]
        return [], outg, inc

    return _Phase([sendb], [], [jax.ShapeDtypeStruct(sendb.shape, sendb.dtype)], 3, 0, build)


def _ph_half_swap(red):
    R2, C = red.shape
    R = R2 // 2

    def build(ins, outs, rsem, lsem):
        (r_ref,) = outs
        x, y, c, _ = _mesh_pos()
        mine, theirs = _half_rows(c, R)
        return ([], [_remote(r_ref.at[mine], r_ref.at[mine], rsem(0), (x, y, 1 - c))],
                [functools.partial(_remote, r_ref.at[theirs], r_ref.at[theirs], rsem(0), (x, y, c))])

    return _Phase([], [red], [], 1, 0, build)


def _call(body, *, name, grid, in_specs, out_specs, out_shape, scratch_shapes=(), phases=()):
    single = not isinstance(out_specs, (list, tuple))
    out_specs = [out_specs] if single else list(out_specs)
    out_shape = [out_shape] if single else list(out_shape)
    n_in, n_out, n_scr = len(in_specs), len(out_specs), len(scratch_shapes)
    if not phases:
        call = pl.pallas_call(body, name=name, grid=grid, in_specs=in_specs, out_specs=out_specs,
                              out_shape=out_shape, scratch_shapes=list(scratch_shapes),
                              compiler_params=_params(len(grid)))
        return lambda *operands: (list(call(*operands)), [])

    ex_in, ex_out, aliases, spans = [], [], {}, []
    for ph in phases:
        i0, o0 = len(ex_in), len(ex_out)
        ex_in += ph.ins
        for a in ph.inout:
            aliases[n_in + len(ex_in)] = n_out + len(ex_out)
            ex_in.append(a)
            ex_out.append(jax.ShapeDtypeStruct(a.shape, a.dtype))
        ex_out += ph.outs
        spans.append((i0, len(ph.ins), o0, len(ex_out) - o0))
    n_remote = sum(ph.n_remote for ph in phases)
    n_local = max(sum(ph.n_local for ph in phases), 1)

    def wrapped(*refs):
        base_in, xin = refs[:n_in], refs[n_in:n_in + len(ex_in)]
        o0 = n_in + len(ex_in)
        base_out, xout = refs[o0:o0 + n_out], refs[o0 + n_out:o0 + n_out + len(ex_out)]
        scr = refs[o0 + n_out + len(ex_out):]
        send_sems, recv_sems, loc_sems = scr[n_scr:]
        first = functools.reduce(jnp.logical_and, [pl.program_id(i) == 0 for i in range(len(grid))])
        last = functools.reduce(jnp.logical_and, [pl.program_id(i) == grid[i] - 1 for i in range(len(grid))])

        def copies():
            out, r0, l0 = [], 0, 0
            for ph, (i0, ni, p0, no) in zip(phases, spans):
                rsem = lambda k, r0=r0: (send_sems.at[r0 + k], recv_sems.at[r0 + k])
                lsem = lambda k, l0=l0: loc_sems.at[l0 + k]
                out.append(ph.build(xin[i0:i0 + ni], xout[p0:p0 + no], rsem, lsem))
                r0, l0 = r0 + ph.n_remote, l0 + ph.n_local
            return out

        @pl.when(first)
        def _():
            for loc, outg, _ in copies():
                for cp in loc + outg:
                    cp.start()

        body(*base_in, *base_out, *scr[:n_scr])

        @pl.when(last)
        def _():
            for loc, outg, inc in copies():
                for make in inc:
                    make().wait_recv()
                for cp in outg:
                    cp.wait_send()
                for cp in loc:
                    cp.wait()

    hbm = pl.BlockSpec(memory_space=pl.ANY)
    call = pl.pallas_call(
        wrapped, name=name, grid=grid, in_specs=list(in_specs) + [hbm] * len(ex_in),
        out_specs=out_specs + [hbm] * len(ex_out), out_shape=out_shape + ex_out,
        scratch_shapes=list(scratch_shapes) + [pltpu.SemaphoreType.DMA((n_remote,)), pltpu.SemaphoreType.DMA((n_remote,)),
                                              pltpu.SemaphoreType.DMA((n_local,))],
        input_output_aliases=aliases, compiler_params=_params(len(grid)))

    def run(*operands):
        res = call(*operands, *ex_in)
        extra = res[n_out:]
        return list(res[:n_out]), [list(extra[p0:p0 + no]) for (_, _, p0, no) in spans]

    return run


def _inproj_fwd(x, g_mix, w_in, tm, phases=()):
    T = x.shape[0]
    widths = [IN_SEGS[i + 1] - IN_SEGS[i] for i in range(7)]

    def body(x_ref, g_ref, w_ref, h_ref, *z_refs):
        h, _ = _rms_fwd(x_ref[...], g_ref[...])
        hb = h.astype(BF16)
        h_ref[...] = hb
        for j in range(N_CHIPS):
            zj = _dot(hb, w_ref[j])
            for s, lo, hi, off in _seg_pieces(j * IN_BLK, (j + 1) * IN_BLK):
                z_refs[s][:, lo:hi] = zj[:, off:off + hi - lo]

    return _call(
        body, phases=phases, name="inproj_fwd", grid=(T // tm,),
        in_specs=[pl.BlockSpec((tm, D_MODEL), lambda i: (i, 0)), _const((1, D_MODEL)),
                  _const((N_CHIPS, D_MODEL, IN_BLK))],
        out_specs=[pl.BlockSpec((tm, D_MODEL), lambda i: (i, 0))]
        + [pl.BlockSpec((tm, w), lambda i: (i, 0)) for w in widths],
        out_shape=[jax.ShapeDtypeStruct((T, D_MODEL), BF16)]
        + [jax.ShapeDtypeStruct((T, w), F32) for w in widths],
    )(x, g_mix, w_in)


def _inproj_bwd(dz_parts, w_in, x, g_mix, dx1, tm, phases=()):
    T = x.shape[0]
    widths = [IN_SEGS[i + 1] - IN_SEGS[i] for i in range(7)]

    def body(*refs):
        p_refs = refs[:7]
        w_ref, x_ref, g_ref, dx1_ref, gx_ref, dg_ref, dz_ref = refs[7:]

        @pl.when(pl.program_id(0) == 0)
        def _():
            dg_ref[...] = jnp.zeros_like(dg_ref)

        for s in range(7):
            dz_ref[:, IN_SEGS[s]:IN_SEGS[s + 1]] = p_refs[s][...]
        dh = jnp.zeros((tm, D_MODEL), F32)
        for j in range(N_CHIPS):
            dh = dh + _dot_nt(dz_ref[:, j * IN_BLK:(j + 1) * IN_BLK], w_ref[j])
        xv = x_ref[...]
        g = g_ref[...]
        _, r = _rms_fwd(xv, g)
        dx, dg = _rms_bwd(dh, xv, r, g)
        gx_ref[...] = dx1_ref[...] + dx
        dg_ref[...] += dg

    row = lambda w: pl.BlockSpec((tm, w), lambda i: (i, 0))
    return _call(
        body, phases=phases, name="inproj_bwd", grid=(T // tm,),
        in_specs=[row(w) for w in widths]
        + [_const((N_CHIPS, D_MODEL, IN_BLK)), row(D_MODEL), _const((1, D_MODEL)), row(D_MODEL)],
        out_specs=[row(D_MODEL), _const((1, D_MODEL))],
        out_shape=[jax.ShapeDtypeStruct((T, D_MODEL), F32), jax.ShapeDtypeStruct((1, D_MODEL), F32)],
        scratch_shapes=[pltpu.VMEM((tm, IN_TOTAL), BF16)],
    )(*dz_parts, w_in, x, g_mix, dx1)


def _wgrad_in(h0, dz_parts, tm):
    T = h0.shape[0]
    widths = [IN_SEGS[i + 1] - IN_SEGS[i] for i in range(7)]

    def body(*refs):
        h_ref, p_refs, o_ref, acc_ref, sem = refs[0], refs[1:8], refs[8], refs[9], refs[10]
        t = pl.program_id(0)

        @pl.when(t == 0)
        def _():
            acc_ref[...] = jnp.zeros_like(acc_ref)

        hv = h_ref[...]
        for j in range(N_CHIPS):
            for s, lo, hi, off in _seg_pieces(j * IN_BLK, (j + 1) * IN_BLK):
                acc_ref[j, :, off:off + hi - lo] += _dot_tn(hv, p_refs[s][:, lo:hi])

        @pl.when(t == T // tm - 1)
        def _():
            cp = pltpu.make_async_copy(acc_ref, o_ref, sem)
            cp.start()
            cp.wait()

    row = lambda w: pl.BlockSpec((tm, w), lambda i: (i, 0))
    return pl.pallas_call(
        body, name="wgrad_in", grid=(T // tm,), in_specs=[row(D_MODEL)] + [row(w) for w in widths],
        out_specs=pl.BlockSpec(memory_space=pl.ANY),
        out_shape=jax.ShapeDtypeStruct((N_CHIPS, D_MODEL, IN_BLK), F32),
        scratch_shapes=[pltpu.VMEM((N_CHIPS, D_MODEL, IN_BLK), F32), pltpu.SemaphoreType.DMA],
        compiler_params=_params(1),
    )(h0, *dz_parts)


def _wgrad(a, g, name, blocked, cn, tm, phases=()):
    T, K = a.shape
    N = g.shape[1]
    nb = N // cn

    def body(a_ref, g_ref, o_ref):
        @pl.when(pl.program_id(1) == 0)
        def _():
            o_ref[...] = jnp.zeros_like(o_ref)

        o_ref[...] += _dot_tn(a_ref[...].astype(BF16), g_ref[...].astype(BF16))

    if blocked:
        out_spec = pl.BlockSpec((None, K, cn), lambda j, t: (j, 0, 0))
        out_shape = jax.ShapeDtypeStruct((nb, K, cn), F32)
    else:
        out_spec = pl.BlockSpec((K, cn), lambda j, t: (0, j))
        out_shape = jax.ShapeDtypeStruct((K, N), F32)
    outs, extra = _call(
        body, phases=phases, name=name, grid=(nb, T // tm),
        in_specs=[pl.BlockSpec((tm, K), lambda j, t: (t, 0)), pl.BlockSpec((tm, cn), lambda j, t: (t, j))],
        out_specs=out_spec, out_shape=out_shape,
    )(a, g)
    return outs[0], extra


def _shift_down(x, prev8, sft, row, row8, tm):
    xs = pltpu.roll(x, sft, 0)
    top = jnp.where(row8 < sft, pltpu.roll(prev8, sft, 0), xs[0:8])
    return jnp.concatenate([top, xs[8:]], axis=0)


def _shift_up(x, next8, sft, row8, tm):
    xs = pltpu.roll(x, tm - sft, 0)
    bot = jnp.where(row8 >= 8 - sft, pltpu.roll(next8, 8 - sft, 0), xs[tm - 8:tm])
    return jnp.concatenate([xs[0:tm - 8], bot], axis=0)


def _conv_fwd(x, prev8, cw_ref, cb, row, row8, tm):
    xc = cb + cw_ref[CONV_W - 1:CONV_W, :] * x
    for sft in range(1, CONV_W):
        j = CONV_W - 1 - sft
        xc = xc + cw_ref[j:j + 1, :] * _shift_down(x, prev8, sft, row, row8, tm)
    return xc


def _blockdiag_dot(xb, w_ref, transpose):
    outs = []
    for b in range(D_MODEL // LANES):
        xs = xb[:, b * LANES:(b + 1) * LANES]
        outs.append(_dot_nt(xs, w_ref[b]) if transpose else _dot(xs, w_ref[b]))
    return jnp.concatenate(outs, axis=1)


def _softplus_neg(lam):
    e = jnp.exp(-jnp.abs(lam))
    u = 1.0 + e
    log1p_e = jnp.where(u == 1.0, e, jnp.log(u) * (e / (u - 1.0)))
    sp = jnp.maximum(-lam, 0.0) + log1p_e
    return sp, -_sigmoid(-lam)


def _lru_gates(xc, wrg_ref, brg, wig_ref, big, sp):
    xcb = xc.astype(BF16)
    r = _sigmoid(_blockdiag_dot(xcb, wrg_ref, False) + brg)
    i = _sigmoid(_blockdiag_dot(xcb, wig_ref, False) + big)
    log_a = (-LRU_C) * r * sp
    a = jnp.exp(log_a)
    t = jnp.tanh(log_a)
    one_m_a2 = (-2.0) * t / (1.0 - t)
    mult = jnp.sqrt(one_m_a2)
    return xcb, r, i, a, mult


def _scan_down(a, b, row, tm):
    d = 1
    while d < tm:
        if d < 8:
            keep = row >= d
            a_s = jnp.where(keep, pltpu.roll(a, d, 0), 1.0)
            b_s = jnp.where(keep, pltpu.roll(b, d, 0), 0.0)
            b = a * b_s + b
            a = a * a_s
        else:
            b = jnp.concatenate([b[:d], a[d:] * b[:-d] + b[d:]], axis=0)
            a = jnp.concatenate([a[:d], a[d:] * a[:-d]], axis=0)
        d *= 2
    return a, b


def _scan_up(c, b, row, tm):
    d = 1
    while d < tm:
        if d < 8:
            keep = row < tm - d
            c_s = jnp.where(keep, pltpu.roll(c, tm - d, 0), 1.0)
            b_s = jnp.where(keep, pltpu.roll(b, tm - d, 0), 0.0)
            b = c * b_s + b
            c = c * c_s
        else:
            b = jnp.concatenate([c[:-d] * b[d:] + b[:-d], b[-d:]], axis=0)
            c = jnp.concatenate([c[:-d] * c[d:], c[-d:]], axis=0)
        d *= 2
    return c, b


def _rnn_fwd(xr, gr, conv_w, conv_b, wrg2, b_rg, wig2, b_ig, lam, n_seq, S, tm, phases=()):
    T = xr.shape[0]
    nt = S // tm
    W = D_MODEL

    def body(xr_ref, gr_ref, cw_ref, cb_ref, wrg_ref, brg_ref, wig_ref, big_ref, lam_ref,
             xc_ref, h_ref, r_ref, i_ref, a_ref, mult_ref, ya_ref, px_ref, ph_ref):
        @pl.when(pl.program_id(1) == 0)
        def _():
            px_ref[...] = jnp.zeros_like(px_ref)
            ph_ref[...] = jnp.zeros_like(ph_ref)

        row = lax.broadcasted_iota(jnp.int32, (tm, W), 0)
        row8 = lax.broadcasted_iota(jnp.int32, (8, W), 0)
        x = xr_ref[...]
        xc = _conv_fwd(x, px_ref[...], cw_ref, cb_ref[...], row, row8, tm)
        sp, _ = _softplus_neg(lam_ref[...])
        _, r, i, a, mult = _lru_gates(xc, wrg_ref, brg_ref[...], wig_ref, big_ref[...], sp)
        r_ref[...], i_ref[...], a_ref[...], mult_ref[...] = r, i, a, mult
        bterm = mult * (i * xc)
        acum, hloc = _scan_down(a, bterm, row, tm)
        h = hloc + acum * ph_ref[7:8, :]
        h_ref[...] = h
        xc_ref[...] = xc
        gelu, _ = _gelu_and_grad(gr_ref[...])
        ya_ref[...] = (h * gelu).astype(BF16)
        px_ref[...] = xr_ref[tm - 8:tm, :]
        ph_ref[...] = h_ref[tm - 8:tm, :]

    tile = pl.BlockSpec((tm, W), lambda s, t: (s * nt + t, 0))
    return _call(
        body, phases=phases, name="rnn_fwd", grid=(n_seq, nt),
        in_specs=[tile, tile, _const((CONV_W, W)), _const((1, W)), _const((8, LANES, LANES)), _const((1, W)),
                  _const((8, LANES, LANES)), _const((1, W)), _const((1, W))],
        out_specs=[tile] * 7,
        out_shape=[jax.ShapeDtypeStruct((T, W), F32)] * 6 + [jax.ShapeDtypeStruct((T, W), BF16)],
        scratch_shapes=[pltpu.VMEM((8, W), F32), pltpu.VMEM((8, W), F32)],
    )(xr, gr, conv_w, conv_b, wrg2, b_rg, wig2, b_ig, lam)


def _rnn_bwd(dya, xr, gr, xc, h, gates, conv_w, wrg2, wig2, lam, n_seq, S, tm, phases=()):
    T = xr.shape[0]
    nt = S // tm
    W = D_MODEL
    nb8 = tm // 8

    def body(dya_ref, xr_ref, gr_ref, xc_ref, h_ref, r_ref, i_ref, a_ref, mult_ref, xprev_ref, hprev_ref, cw_ref,
             wrg_ref, wig_ref, lam_ref, dxr_ref, dgr_ref, vec_ref, dwrg_ref, dwig_ref, cg_ref, ndxc_ref, tmp_ref):
        s, ti = pl.program_id(0), pl.program_id(1)

        @pl.when((s == 0) & (ti == 0))
        def _():
            vec_ref[...] = jnp.zeros_like(vec_ref)
            dwrg_ref[...] = jnp.zeros_like(dwrg_ref)
            dwig_ref[...] = jnp.zeros_like(dwig_ref)

        @pl.when(ti == 0)
        def _():
            cg_ref[...] = jnp.zeros_like(cg_ref)
            ndxc_ref[...] = jnp.zeros_like(ndxc_ref)

        first = ti == nt - 1
        row = lax.broadcasted_iota(jnp.int32, (tm, W), 0)
        row8 = lax.broadcasted_iota(jnp.int32, (8, W), 0)
        x = xr_ref[...]
        xc = xc_ref[...]
        hv = h_ref[...]
        xprev = jnp.where(first, 0.0, xprev_ref[...])
        hprev = jnp.where(first, 0.0, hprev_ref[...])
        sp, dsp_dlam = _softplus_neg(lam_ref[...])
        xcb = xc.astype(BF16)
        r, i, a, mult = r_ref[...], i_ref[...], a_ref[...], mult_ref[...]

        gelu, dgelu = _gelu_and_grad(gr_ref[...])
        dya_v = dya_ref[...]
        dgr_ref[...] = (dya_v * hv * dgelu).astype(BF16)
        dh = dya_v * gelu
        c = jnp.where(row < tm - 1, pltpu.roll(a, tm - 1, 0), 1.0)
        ccum, gloc = _scan_up(c, dh, row, tm)
        G = gloc + ccum * cg_ref[0:1, :]
        tmp_ref[...] = a * G
        cg_ref[...] = tmp_ref[0:8, :]

        h_m1 = _shift_down(hv, hprev, 1, row, row8, tm)
        ixc = i * xc
        dixc = G * mult
        dlog_a = (G * h_m1) * a - (G * ixc) * (a * a / mult)
        dr = dlog_a * ((-LRU_C) * sp)
        di = dixc * xc
        drg = dr * r * (1.0 - r)
        dig = di * i * (1.0 - i)
        vec_ref[7:8, :] += jnp.sum(dlog_a * ((-LRU_C) * r), axis=0, keepdims=True) * dsp_dlam
        vec_ref[5:6, :] += jnp.sum(drg, axis=0, keepdims=True)
        vec_ref[6:7, :] += jnp.sum(dig, axis=0, keepdims=True)
        drgb = drg.astype(BF16)
        digb = dig.astype(BF16)
        dxc = dixc * i + _blockdiag_dot(drgb, wrg_ref, True) + _blockdiag_dot(digb, wig_ref, True)
        for b in range(W // LANES):
            sl = slice(b * LANES, (b + 1) * LANES)
            dwrg_ref[b] += _dot_tn(xcb[:, sl], drgb[:, sl])
            dwig_ref[b] += _dot_tn(xcb[:, sl], digb[:, sl])

        vec_ref[4:5, :] += jnp.sum(dxc, axis=0, keepdims=True)
        vec_ref[3:4, :] += jnp.sum(dxc * x, axis=0, keepdims=True)
        dxr = cw_ref[CONV_W - 1:CONV_W, :] * dxc
        nxt = ndxc_ref[...]
        for sft in range(1, CONV_W):
            j = CONV_W - 1 - sft
            vec_ref[j:j + 1, :] += jnp.sum(dxc * _shift_down(x, xprev, sft, row, row8, tm), axis=0, keepdims=True)
            dxr = dxr + cw_ref[j:j + 1, :] * _shift_up(dxc, nxt, sft, row8, tm)
        dxr_ref[...] = dxr.astype(BF16)
        tmp_ref[...] = dxc
        ndxc_ref[...] = tmp_ref[0:8, :]

    rev = lambda s, t: (s * nt + nt - 1 - t, 0)
    tile = pl.BlockSpec((tm, W), rev)
    prev8 = pl.BlockSpec((8, W), lambda s, t: (jnp.maximum((s * nt + nt - 1 - t) * nb8 - 1, 0), 0))
    return _call(
        body, phases=phases, name="rnn_bwd", grid=(n_seq, nt),
        in_specs=[tile] * 9 + [prev8, prev8, _const((CONV_W, W)), _const((8, LANES, LANES)),
                               _const((8, LANES, LANES)), _const((1, W))],
        out_specs=[tile, tile, _const((16, W)), _const((8, LANES, LANES)), _const((8, LANES, LANES))],
        out_shape=[jax.ShapeDtypeStruct((T, W), BF16), jax.ShapeDtypeStruct((T, W), BF16),
                   jax.ShapeDtypeStruct((16, W), F32), jax.ShapeDtypeStruct((8, LANES, LANES), F32),
                   jax.ShapeDtypeStruct((8, LANES, LANES), F32)],
        scratch_shapes=[pltpu.VMEM((8, W), F32), pltpu.VMEM((8, W), F32), pltpu.VMEM((tm, W), F32)],
    )(dya, xr, gr, xc, h, *gates, xr, h, conv_w, wrg2, wig2, lam)


def _head_swap(t, lane):
    w = t.shape[1]
    return jnp.where(lane % HEAD_DIM < HEAD_DIM // 2, pltpu.roll(t, w - HEAD_DIM // 2, 1),
                     pltpu.roll(t, HEAD_DIM // 2, 1))


def _qk_prep(t, gain, cosf, sins, ind, indt, lane):
    ms = _split_dot(t * t, ind) * (1.0 / HEAD_DIM)
    rstd = _split_dot(lax.rsqrt(ms + NORM_EPS), indt)
    tn = (t * rstd) * gain
    return tn * cosf + _head_swap(tn, lane) * sins, rstd


def _qk_prep_bwd(dy, t, rstd, gain, cosf, sins, ind, indt, lane):
    dtn = dy * cosf + _head_swap(dy * sins, lane)
    dgain = jnp.sum(dtn * (t * rstd), axis=0, keepdims=True)
    dn = dtn * gain
    m = _split_dot(_split_dot(dn * t, ind), indt) * (1.0 / HEAD_DIM)
    return rstd * dn - t * (rstd * rstd * rstd * m), dgain


def _attn_mask_t(blk_idx):
    ci = lax.broadcasted_iota(jnp.int32, (2 * WINDOW, WINDOW), 0)
    qi = lax.broadcasted_iota(jnp.int32, (2 * WINDOW, WINDOW), 1)
    diff = WINDOW + qi - ci
    return (diff >= 0) & (diff < WINDOW) & ((ci >= WINDOW) | (blk_idx > 0))


def _stack_heads(t, kvh, lo):
    parts = []
    for i in (2 * kvh, 2 * kvh + 1):
        tp = t[:, i * LANES:(i + 1) * LANES]
        parts += [jnp.where(lo, tp, 0.0), jnp.where(lo, 0.0, tp)]
    return jnp.concatenate(parts, axis=0).astype(BF16)


def _unstack_heads(ts, lo):
    w = WINDOW
    return jnp.where(lo, ts[0:w], ts[w:2 * w]), jnp.where(lo, ts[2 * w:3 * w], ts[3 * w:4 * w])


def _dup_head(t, kvh, lo2):
    m = kvh // 2
    t2 = t[:, m * LANES:(m + 1) * LANES]
    t2r = pltpu.roll(t2, HEAD_DIM, 1)
    return (jnp.where(lo2, t2, t2r) if kvh % 2 == 0 else jnp.where(lo2, t2r, t2)).astype(BF16)


def _fold_head(ts, kvh, lo2):
    tot = ts + pltpu.roll(ts, HEAD_DIM, 1)
    own = lo2 if kvh % 2 == 0 else ~lo2
    return jnp.where(own, tot, 0.0)


KEY_CHUNKS = tuple(slice(i * 64, (i + 1) * 64) for i in range(2 * WINDOW // 64))


def _fold8(x, op):
    return op(x.reshape(x.shape[0] // 8, 8, x.shape[1]), axis=0)


def _softmax_stats(s_ref, b, cols, sink):
    m8 = None
    for c in KEY_CHUNKS:
        t = _fold8(s_ref[b, c, cols], jnp.max)
        m8 = t if m8 is None else jnp.maximum(m8, t)
    mx = jnp.maximum(jnp.max(m8, axis=0, keepdims=True), sink)
    d8 = None
    for c in KEY_CHUNKS:
        t = _fold8(jnp.exp(s_ref[b, c, cols] - mx), jnp.sum)
        d8 = t if d8 is None else d8 + t
    es = jnp.exp(sink - mx)
    inv = 1.0 / (jnp.sum(d8, axis=0, keepdims=True) + es)
    return mx, inv, es * inv


def _attn_fwd(q, k, v, qg, kg, sinks, cosf, sins, ind_q, ind_qt, ind_k, ind_kt, n_seq, S, phases=()):
    T = q.shape[0]
    nblk = S // WINDOW
    W = D_MODEL

    def body(sink_ref, q_ref, k_ref, v_ref, qg_ref, kg_ref, cos_ref, sin_ref, iq_ref, iqt_ref, ik_ref, ikt_ref,
             o_ref, kc_ref, vc_ref, s_ref, p_ref, qs_ref, kd_ref, vd_ref):
        n = pl.program_id(1)

        @pl.when(n == 0)
        def _():
            kc_ref[...] = jnp.zeros_like(kc_ref)
            vc_ref[...] = jnp.zeros_like(vc_ref)

        lane = lax.broadcasted_iota(jnp.int32, (WINDOW, W), 1)
        lo = lane[:, :LANES] < HEAD_DIM
        lo2 = lax.broadcasted_iota(jnp.int32, (2 * WINDOW, LANES), 1) < HEAD_DIM
        cosf, sinv = jnp.tile(cos_ref[...], (1, W // LANES)), jnp.tile(sin_ref[...], (1, W // LANES))
        qr, _ = _qk_prep(q_ref[...], qg_ref[...], cosf, sinv, iq_ref[...], iqt_ref[...], lane)
        kr, _ = _qk_prep(k_ref[...], kg_ref[...], cosf[:, :KV_W], sinv[:, :KV_W], ik_ref[...], ikt_ref[...],
                         lane[:, :KV_W])
        kc_ref[WINDOW:2 * WINDOW, :] = kr
        vc_ref[WINDOW:2 * WINDOW, :] = v_ref[...]
        kc, vc = kc_ref[...], vc_ref[...]
        mask = jnp.tile(_attn_mask_t(n), (1, 4))
        qr = qr * HEAD_DIM ** -0.5
        for kvh in range(N_KV):
            qs_ref[kvh] = _stack_heads(qr, kvh, lo)
            kd_ref[kvh] = _dup_head(kc, kvh, lo2)
            vd_ref[kvh] = _dup_head(vc, kvh, lo2)

        def scores(kvh):
            s_ref[kvh % 2] = jnp.where(mask, _dot_nt(kd_ref[kvh], qs_ref[kvh]), -1e30)

        def softmax(kvh):
            b = kvh % 2
            for r in range(4):
                cols = slice(r * WINDOW, (r + 1) * WINDOW)
                mx, inv, _ = _softmax_stats(s_ref, b, cols, sink_ref[4 * kvh + r])
                for c in KEY_CHUNKS:
                    p_ref[b, c, cols] = (jnp.exp(s_ref[b, c, cols] - mx) * inv).astype(BF16)

        def output(kvh):
            o0, o1 = _unstack_heads(_dot_tn(p_ref[kvh % 2], vd_ref[kvh]), lo)
            o_ref[:, (2 * kvh) * LANES:(2 * kvh + 1) * LANES] = o0.astype(BF16)
            o_ref[:, (2 * kvh + 1) * LANES:(2 * kvh + 2) * LANES] = o1.astype(BF16)

        scores(0)
        for kvh in range(N_KV):
            if kvh + 1 < N_KV:
                scores(kvh + 1)
            softmax(kvh)
            output(kvh)
        kc_ref[0:WINDOW, :] = kr
        vc_ref[0:WINDOW, :] = v_ref[...]

    blk = lambda w: pl.BlockSpec((WINDOW, w), lambda s, n: (s * nblk + n, 0))
    pos = pl.BlockSpec((WINDOW, LANES), lambda s, n: (n, 0))
    outs, extra = _call(
        body, phases=phases, name="attn_fwd", grid=(n_seq, nblk),
        in_specs=[pl.BlockSpec(memory_space=pltpu.SMEM), blk(W), blk(KV_W), blk(KV_W), _const((1, W)),
                  _const((1, KV_W)), pos, pos, _const((W, LANES)), _const((LANES, W)), _const((KV_W, LANES)),
                  _const((LANES, KV_W))],
        out_specs=blk(W), out_shape=jax.ShapeDtypeStruct((T, W), BF16),
        scratch_shapes=[pltpu.VMEM((2 * WINDOW, KV_W), F32), pltpu.VMEM((2 * WINDOW, KV_W), F32),
                        pltpu.VMEM((2, 2 * WINDOW, 4 * WINDOW), F32), pltpu.VMEM((2, 2 * WINDOW, 4 * WINDOW), BF16),
                        pltpu.VMEM((N_KV, 4 * WINDOW, LANES), BF16), pltpu.VMEM((N_KV, 2 * WINDOW, LANES), BF16),
                        pltpu.VMEM((N_KV, 2 * WINDOW, LANES), BF16)],
    )(sinks, q, k, v, qg, kg, cosf, sins, ind_q, ind_qt, ind_k, ind_kt)
    return outs[0], extra


def _attn_bwd(do, q, k, v, qg, kg, sinks, cosf, sins, ind_q, ind_qt, ind_k, ind_kt, n_seq, S, phases=()):
    T = q.shape[0]
    nblk = S // WINDOW
    W = D_MODEL

    def body(sink_ref, do_ref, q_ref, k_ref, v_ref, qg_ref, kg_ref, cos_ref, sin_ref, iq_ref, iqt_ref, ik_ref,
             ikt_ref, dq_ref, dkc_ref, dkp_ref, dvc_ref, dvp_ref, dqg_ref, dsk_ref, kc_ref, vc_ref, dqr_ref,
             dk_ref, dv_ref, s_ref, dp_ref, p_ref, ds_ref, qs_ref, dos_ref, kd_ref, vd_ref):
        s_id, n = pl.program_id(0), pl.program_id(1)

        @pl.when((s_id == 0) & (n == 0))
        def _():
            dqg_ref[...] = jnp.zeros_like(dqg_ref)
            dsk_ref[...] = jnp.zeros_like(dsk_ref)

        @pl.when(n == 0)
        def _():
            kc_ref[...] = jnp.zeros_like(kc_ref)
            vc_ref[...] = jnp.zeros_like(vc_ref)

        lane = lax.broadcasted_iota(jnp.int32, (WINDOW, W), 1)
        lane_k = lane[:, :KV_W]
        lane128 = lane[:, :LANES]
        cosf, sinv = jnp.tile(cos_ref[...], (1, W // LANES)), jnp.tile(sin_ref[...], (1, W // LANES))
        qv = q_ref[...]
        qr, q_rstd = _qk_prep(qv, qg_ref[...], cosf, sinv, iq_ref[...], iqt_ref[...], lane)
        kr, _ = _qk_prep(k_ref[...], kg_ref[...], cosf[:, :KV_W], sinv[:, :KV_W], ik_ref[...], ikt_ref[...], lane_k)
        kc_ref[WINDOW:2 * WINDOW, :] = kr
        vc_ref[WINDOW:2 * WINDOW, :] = v_ref[...]
        kc, vc = kc_ref[...], vc_ref[...]
        dov = do_ref[...]
        mask = jnp.tile(_attn_mask_t(n), (1, 4))
        lo = lane128 < HEAD_DIM
        lo2 = lax.broadcasted_iota(jnp.int32, (2 * WINDOW, LANES), 1) < HEAD_DIM
        scale = HEAD_DIM ** -0.5
        qr = qr * scale
        dk_ref[...] = jnp.zeros_like(dk_ref)
        dv_ref[...] = jnp.zeros_like(dv_ref)
        for kvh in range(N_KV):
            qs_ref[kvh] = _stack_heads(qr, kvh, lo)
            dos_ref[kvh] = _stack_heads(dov, kvh, lo)
            kd_ref[kvh] = _dup_head(kc, kvh, lo2)
            vd_ref[kvh] = _dup_head(vc, kvh, lo2)

        def scores(kvh):
            b = kvh % 2
            s_ref[b] = jnp.where(mask, _dot_nt(kd_ref[kvh], qs_ref[kvh]), -1e30)
            dp_ref[b] = _dot_nt(vd_ref[kvh], dos_ref[kvh])

        def softmax(kvh):
            b = kvh % 2
            for r in range(4):
                cols = slice(r * WINDOW, (r + 1) * WINDOW)
                head = 4 * kvh + r
                mx, inv, ps = _softmax_stats(s_ref, b, cols, sink_ref[head])
                g8 = None
                for c in KEY_CHUNKS:
                    t = _fold8(jnp.exp(s_ref[b, c, cols] - mx) * dp_ref[b, c, cols], jnp.sum)
                    g8 = t if g8 is None else g8 + t
                dd = jnp.sum(g8, axis=0, keepdims=True) * inv
                for c in KEY_CHUNKS:
                    p = jnp.exp(s_ref[b, c, cols] - mx) * inv
                    p_ref[b, c, cols] = p.astype(BF16)
                    ds_ref[b, c, cols] = (p * (dp_ref[b, c, cols] - dd)).astype(BF16)
                dsk_ref[head:head + 1, :] -= ps * dd

        def grads(kvh):
            m, b = kvh // 2, kvh % 2
            dq0, dq1 = _unstack_heads(_dot_tn(ds_ref[b], kd_ref[kvh]) * scale, lo)
            dqr_ref[:, (2 * kvh) * LANES:(2 * kvh + 1) * LANES] = dq0
            dqr_ref[:, (2 * kvh + 1) * LANES:(2 * kvh + 2) * LANES] = dq1
            dk_ref[:, m * LANES:(m + 1) * LANES] += _fold_head(_dot(ds_ref[b], qs_ref[kvh]), kvh, lo2)
            dv_ref[:, m * LANES:(m + 1) * LANES] += _fold_head(_dot(p_ref[b], dos_ref[kvh]), kvh, lo2)

        scores(0)
        for kvh in range(N_KV):
            if kvh + 1 < N_KV:
                scores(kvh + 1)
            softmax(kvh)
            grads(kvh)
        dq, dqg = _qk_prep_bwd(dqr_ref[...], qv, q_rstd, qg_ref[...], cosf, sinv, iq_ref[...], iqt_ref[...], lane)
        dq_ref[...] = dq.astype(BF16)
        dqg_ref[...] += dqg
        dkp_ref[...] = dk_ref[0:WINDOW, :]
        dkc_ref[...] = dk_ref[WINDOW:2 * WINDOW, :]
        dvp_ref[...] = dv_ref[0:WINDOW, :]
        dvc_ref[...] = dv_ref[WINDOW:2 * WINDOW, :]
        kc_ref[0:WINDOW, :] = kr
        vc_ref[0:WINDOW, :] = v_ref[...]

    blk = lambda w: pl.BlockSpec((WINDOW, w), lambda s, n: (s * nblk + n, 0))
    pos = pl.BlockSpec((WINDOW, LANES), lambda s, n: (n, 0))
    kv_out = jax.ShapeDtypeStruct((T, KV_W), F32)
    stage = lambda dt: pltpu.VMEM((2, 2 * WINDOW, 4 * WINDOW), dt)
    return _call(
        body, phases=phases, name="attn_bwd", grid=(n_seq, nblk),
        in_specs=[pl.BlockSpec(memory_space=pltpu.SMEM), blk(W), blk(W), blk(KV_W), blk(KV_W), _const((1, W)),
                  _const((1, KV_W)), pos, pos, _const((W, LANES)), _const((LANES, W)), _const((KV_W, LANES)),
                  _const((LANES, KV_W))],
        out_specs=[blk(W), blk(KV_W), blk(KV_W), blk(KV_W), blk(KV_W), _const((1, W)), _const((N_HEADS, LANES))],
        out_shape=[jax.ShapeDtypeStruct((T, W), BF16), kv_out, kv_out, kv_out, kv_out,
                   jax.ShapeDtypeStruct((1, W), F32), jax.ShapeDtypeStruct((N_HEADS, LANES), F32)],
        scratch_shapes=[pltpu.VMEM((2 * WINDOW, KV_W), F32), pltpu.VMEM((2 * WINDOW, KV_W), F32),
                        pltpu.VMEM((WINDOW, W), F32), pltpu.VMEM((2 * WINDOW, KV_W), F32),
                        pltpu.VMEM((2 * WINDOW, KV_W), F32), stage(F32), stage(F32), stage(BF16), stage(BF16),
                        pltpu.VMEM((N_KV, 4 * WINDOW, LANES), BF16), pltpu.VMEM((N_KV, 4 * WINDOW, LANES), BF16),
                        pltpu.VMEM((N_KV, 2 * WINDOW, LANES), BF16), pltpu.VMEM((N_KV, 2 * WINDOW, LANES), BF16)],
    )(sinks, do, q, k, v, qg, kg, cosf, sins, ind_q, ind_qt, ind_k, ind_kt)


def _kv_bwd(dkc, dkp, dvc, dvp, k, kg, cosf, sins, ind_k, ind_kt, n_seq, S, phases=()):
    T = k.shape[0]
    nblk = S // WINDOW

    def body(dkc_ref, dkp_ref, dvc_ref, dvp_ref, k_ref, kg_ref, cos_ref, sin_ref, ik_ref, ikt_ref,
             dk_ref, dv_ref, dkg_ref):
        s_id, n = pl.program_id(0), pl.program_id(1)

        @pl.when((s_id == 0) & (n == 0))
        def _():
            dkg_ref[...] = jnp.zeros_like(dkg_ref)

        has_next = n < nblk - 1
        lane = lax.broadcasted_iota(jnp.int32, (WINDOW, KV_W), 1)
        dkr = dkc_ref[...] + jnp.where(has_next, dkp_ref[...], 0.0)
        dv_ref[...] = (dvc_ref[...] + jnp.where(has_next, dvp_ref[...], 0.0)).astype(BF16)
        cosf, sinv = jnp.tile(cos_ref[...], (1, KV_W // LANES)), jnp.tile(sin_ref[...], (1, KV_W // LANES))
        kv = k_ref[...]
        _, rstd = _qk_prep(kv, kg_ref[...], cosf, sinv, ik_ref[...], ikt_ref[...], lane)
        dk, dkg = _qk_prep_bwd(dkr, kv, rstd, kg_ref[...], cosf, sinv, ik_ref[...], ikt_ref[...], lane)
        dk_ref[...] = dk.astype(BF16)
        dkg_ref[...] += dkg

    cur = pl.BlockSpec((WINDOW, KV_W), lambda s, n: (s * nblk + n, 0))
    nxt = pl.BlockSpec((WINDOW, KV_W), lambda s, n: (s * nblk + jnp.minimum(n + 1, nblk - 1), 0))
    pos = pl.BlockSpec((WINDOW, LANES), lambda s, n: (n, 0))
    return _call(
        body, phases=phases, name="kv_bwd", grid=(n_seq, nblk),
        in_specs=[cur, nxt, cur, nxt, cur, _const((1, KV_W)), pos, pos, _const((KV_W, LANES)),
                  _const((LANES, KV_W))],
        out_specs=[cur, cur, _const((1, KV_W))],
        out_shape=[jax.ShapeDtypeStruct((T, KV_W), BF16), jax.ShapeDtypeStruct((T, KV_W), BF16),
                   jax.ShapeDtypeStruct((1, KV_W), F32)],
    )(dkc, dkp, dvc, dvp, k, kg, cosf, sins, ind_k, ind_kt)


def _merge_fwd(x, ya, o, ga, gb, w_rnn, w_attn, w_out, tm, phases=()):
    T = x.shape[0]
    W = D_MODEL

    def body(x_ref, ya_ref, o_ref, ga_ref, gb_ref, wr_ref, wa_ref, wo_ref, x1_ref, mg_ref, yao_ref, ybo_ref):
        y_a = _dot(ya_ref[...], wr_ref[...])
        y_b = _dot(o_ref[...], wa_ref[...])
        yao_ref[...] = y_a
        ybo_ref[...] = y_b
        mg = (_sigmoid(ga_ref[...]) * y_a + _sigmoid(gb_ref[...]) * y_b).astype(BF16)
        mg_ref[...] = mg
        x1_ref[...] = x_ref[...] + _dot(mg, wo_ref[...])

    row = pl.BlockSpec((tm, W), lambda i: (i, 0))
    sq = _const((W, W))
    return _call(
        body, phases=phases, name="merge_fwd", grid=(T // tm,),
        in_specs=[row, row, row, row, row, sq, sq, sq], out_specs=[row, row, row, row],
        out_shape=[jax.ShapeDtypeStruct((T, W), F32), jax.ShapeDtypeStruct((T, W), BF16),
                   jax.ShapeDtypeStruct((T, W), F32), jax.ShapeDtypeStruct((T, W), F32)],
    )(x, ya, o, ga, gb, w_rnn, w_attn, w_out)


def _merge_bwd(dx1, ga, gb, y_a, y_b, w_rnn, w_attn, w_out, tm, phases=()):
    T = dx1.shape[0]
    W = D_MODEL

    def body(dx1_ref, ga_ref, gb_ref, ya_ref, yb_ref, wr_ref, wa_ref, wo_ref,
             dga_ref, dgb_ref, dya_ref, dyb_ref, dyain_ref, do_ref):
        dm = _dot_nt(dx1_ref[...].astype(BF16), wo_ref[...])
        sa = _sigmoid(ga_ref[...])
        sb = _sigmoid(gb_ref[...])
        dga_ref[...] = (dm * ya_ref[...] * (sa * (1.0 - sa))).astype(BF16)
        dgb_ref[...] = (dm * yb_ref[...] * (sb * (1.0 - sb))).astype(BF16)
        dya = (dm * sa).astype(BF16)
        dyb = (dm * sb).astype(BF16)
        dya_ref[...] = dya
        dyb_ref[...] = dyb
        dyain_ref[...] = _dot_nt(dya, wr_ref[...])
        do_ref[...] = _dot_nt(dyb, wa_ref[...])

    row = pl.BlockSpec((tm, W), lambda i: (i, 0))
    sq = _const((W, W))
    b16 = jax.ShapeDtypeStruct((T, W), BF16)
    f32 = jax.ShapeDtypeStruct((T, W), F32)
    return _call(
        body, phases=phases, name="merge_bwd", grid=(T // tm,),
        in_specs=[row, row, row, row, row, sq, sq, sq], out_specs=[row] * 6,
        out_shape=[b16, b16, b16, b16, f32, f32],
    )(dx1, ga, gb, y_a, y_b, w_rnn, w_attn, w_out)


def _mlp_fwd(x1, g_mlp, w_up, w_down, tm, phases=()):
    T = x1.shape[0]
    W = D_MODEL

    def body(x_ref, g_ref, wu_ref, wd_ref, x2_ref, hm_ref, u_ref, act_ref):
        xv = x_ref[...]
        hm, _ = _rms_fwd(xv, g_ref[...])
        hmb = hm.astype(BF16)
        hm_ref[...] = hmb
        for j in range(N_CHIPS):
            u = _dot(hmb, wu_ref[j])
            u_ref[:, j * W:(j + 1) * W] = u
            ru = jnp.maximum(u, 0.0)
            act_ref[:, j * W:(j + 1) * W] = (ru * ru).astype(BF16)
        x2_ref[...] = xv + _dot(act_ref[...], wd_ref[...])

    row = lambda w: pl.BlockSpec((tm, w), lambda i: (i, 0))
    return _call(
        body, phases=phases, name="mlp_fwd", grid=(T // tm,),
        in_specs=[row(W), _const((1, W)), _const((N_CHIPS, W, W)), _const((D_FF, W))],
        out_specs=[row(W), row(W), row(D_FF), row(D_FF)],
        out_shape=[jax.ShapeDtypeStruct((T, W), F32), jax.ShapeDtypeStruct((T, W), BF16),
                   jax.ShapeDtypeStruct((T, D_FF), F32), jax.ShapeDtypeStruct((T, D_FF), BF16)],
    )(x1, g_mlp, w_up, w_down)


def _mlp_bwd(dx2, u, x1, g_mlp, w_up, w_down, tm, phases=()):
    T = x1.shape[0]
    W = D_MODEL

    def body(dx2_ref, u_ref, x_ref, g_ref, wu_ref, wd_ref, dx1_ref, du_ref, dg_ref):
        @pl.when(pl.program_id(0) == 0)
        def _():
            dg_ref[...] = jnp.zeros_like(dg_ref)

        dx2 = dx2_ref[...]
        dact = _dot_nt(dx2.astype(BF16), wd_ref[...])
        du_ref[...] = (dact * (2.0 * jnp.maximum(u_ref[...], 0.0))).astype(BF16)
        dhm = jnp.zeros((tm, W), F32)
        for j in range(N_CHIPS):
            dhm = dhm + _dot_nt(du_ref[:, j * W:(j + 1) * W], wu_ref[j])
        xv = x_ref[...]
        g = g_ref[...]
        _, r = _rms_fwd(xv, g)
        dx, dg = _rms_bwd(dhm, xv, r, g)
        dx1_ref[...] = dx2 + dx
        dg_ref[...] += dg

    row = lambda w: pl.BlockSpec((tm, w), lambda i: (i, 0))
    return _call(
        body, phases=phases, name="mlp_bwd", grid=(T // tm,),
        in_specs=[row(W), row(D_FF), row(W), _const((1, W)), _const((N_CHIPS, W, W)), _const((D_FF, W))],
        out_specs=[row(W), row(D_FF), _const((1, W))],
        out_shape=[jax.ShapeDtypeStruct((T, W), F32), jax.ShapeDtypeStruct((T, D_FF), BF16),
                   jax.ShapeDtypeStruct((1, W), F32)],
    )(dx2, u, x1, g_mlp, w_up, w_down)


def _ple_loss(x2, p, target, g_ple, w_gate, w_proj, tm, phases=()):
    T = x2.shape[0]
    W = D_MODEL
    cw = W // N_CHIPS

    def body(x_ref, p_ref, t_ref, g_ref, wg_ref, wp_ref, loss_ref, dx2_ref, pb_ref, de_ref, hp_ref, dtg_ref, dg_ref):
        @pl.when(pl.program_id(0) == 0)
        def _():
            dg_ref[...] = jnp.zeros_like(dg_ref)
            loss_ref[...] = jnp.zeros_like(loss_ref)

        xv = x_ref[...]
        g = g_ref[...]
        pb = p_ref[...].astype(BF16)
        pb_ref[...] = pb
        e = jnp.concatenate([_dot(pb, wp_ref[j]) for j in range(N_CHIPS)], axis=1)
        hp, r = _rms_fwd(xv, g)
        hpb = hp.astype(BF16)
        hp_ref[...] = hpb
        sg = _sigmoid(_dot(hpb, wg_ref[...]))
        diff = (xv + e * sg) - t_ref[...]
        loss_ref[...] += jnp.sum(diff * diff) * (0.5 / W)
        dx3 = diff * (1.0 / W)
        de_ref[...] = (dx3 * sg).astype(BF16)
        dtg = (dx3 * e * (sg * (1.0 - sg))).astype(BF16)
        dtg_ref[...] = dtg
        dx, dg = _rms_bwd(_dot_nt(dtg, wg_ref[...]), xv, r, g)
        dx2_ref[...] = dx3 + dx
        dg_ref[...] += dg

    row = lambda w: pl.BlockSpec((tm, w), lambda i: (i, 0))
    b16 = lambda w: jax.ShapeDtypeStruct((T, w), BF16)
    return _call(
        body, phases=phases, name="ple_loss", grid=(T // tm,),
        in_specs=[row(W), row(PLE_DIM), row(W), _const((1, W)), _const((W, W)), _const((N_CHIPS, PLE_DIM, cw))],
        out_specs=[_const((8, LANES)), row(W), row(PLE_DIM), row(W), row(W), row(W), _const((1, W))],
        out_shape=[jax.ShapeDtypeStruct((8, LANES), F32), jax.ShapeDtypeStruct((T, W), F32), b16(PLE_DIM),
                   b16(W), b16(W), b16(W), jax.ShapeDtypeStruct((1, W), F32)],
    )(x2, p, target, g_ple, w_gate, w_proj)


def _adamw(w, g, m, v, name, tr, phases=()):
    R, C = w.shape
    c1 = 1.0 / (1.0 - ADAM_B1 ** ADAM_STEP)
    c2 = 1.0 / (1.0 - ADAM_B2 ** ADAM_STEP)

    def body(w_ref, g_ref, m_ref, v_ref, go_ref, d_ref, nm_ref, nv_ref):
        gv = g_ref[...]
        go_ref[...] = gv
        nm = ADAM_B1 * m_ref[...] + (1.0 - ADAM_B1) * gv
        nv = ADAM_B2 * v_ref[...] + (1.0 - ADAM_B2) * (gv * gv)
        nm_ref[...] = nm
        nv_ref[...] = nv
        d_ref[...] = (-ADAM_LR) * ((nm * c1) / (jnp.sqrt(nv * c2) + ADAM_EPS) + ADAM_WD * w_ref[...])

    row = pl.BlockSpec((tr, C), lambda i: (i, 0))
    sds = jax.ShapeDtypeStruct((R, C), F32)
    return _call(
        body, phases=phases, name=name, grid=(R // tr,), in_specs=[row] * 4, out_specs=[row] * 4,
        out_shape=[sds] * 4,
    )(w, g, m, v)


def _indicator(width):
    ind = np.zeros((width, LANES), np.float32)
    ind[np.arange(width), np.arange(width) // HEAD_DIM] = 1.0
    return jnp.asarray(ind, BF16), jnp.asarray(ind.T, BF16)


def _rope_tables(S):
    inv = ROPE_THETA ** (-jnp.arange(0, HEAD_DIM, 2, dtype=F32) / HEAD_DIM)
    ang = jnp.arange(S, dtype=F32)[:, None] * inv[None, :]
    cos, sin = jnp.cos(ang), jnp.sin(ang)
    cosf = jnp.tile(jnp.concatenate([cos, cos], axis=1), (1, LANES // HEAD_DIM))
    sins = jnp.tile(jnp.concatenate([-sin, sin], axis=1), (1, LANES // HEAD_DIM))
    return cosf, sins


def _pair_blockdiag(w):
    w4 = w.reshape(8, 2, HEAD_DIM, HEAD_DIM)
    eye = jnp.eye(2, dtype=w.dtype)
    return jnp.einsum("bpij,pq->bpiqj", w4, eye).reshape(8, LANES, LANES)


def _pair_blockdiag_extract(g):
    g5 = g.reshape(8, 2, HEAD_DIM, 2, HEAD_DIM)
    return jnp.stack([g5[:, 0, :, 0, :], g5[:, 1, :, 1, :]], axis=1).reshape(16, HEAD_DIM, HEAD_DIM)


def _pair_sum(parts, sibs, name):
    n = len(parts)
    dims = [(p.shape[1] // 2, p.shape[2]) for p in parts]

    def body(*refs):
        p_r, s_r, send_r, own_r, mine_r, sem = (refs[0:n], refs[n:2 * n], refs[2 * n:3 * n], refs[3 * n:4 * n],
                                                refs[4 * n:5 * n], refs[5 * n])
        x, y, c, chips = _mesh_pos()
        me = 2 * x + y
        loads = []
        for i, (R, _) in enumerate(dims):
            mine, _ = _half_rows(c, R)
            cp = pltpu.make_async_copy(p_r[i].at[:, mine, :], mine_r[i], sem.at[i])
            cp.start()
            loads.append(cp)
        for i in range(n):
            loads[i].wait()
            for j, (cx, cy) in enumerate(chips):
                k = 2 * cx + cy
                send_r[i][j] = (mine_r[i][k] + s_r[i][k]).astype(BF16)
            own_r[i][...] = mine_r[i][me] + s_r[i][me]

    vm = pl.BlockSpec(memory_space=pltpu.VMEM)
    out = pl.pallas_call(
        body, name=name, in_specs=[pl.BlockSpec(memory_space=pl.ANY)] * n + [vm] * n, out_specs=[vm] * (2 * n),
        out_shape=[jax.ShapeDtypeStruct((3, R, C), BF16) for R, C in dims]
        + [jax.ShapeDtypeStruct((R, C), F32) for R, C in dims],
        scratch_shapes=[pltpu.VMEM((N_CHIPS, R, C), F32) for R, C in dims] + [pltpu.SemaphoreType.DMA((n,))],
        compiler_params=pltpu.CompilerParams(vmem_limit_bytes=VMEM_LIMIT),
    )(*parts, *sibs)
    return out[:n], out[n:]


def _chip_sum(owns, recvs, name):
    n = len(owns)
    dims = [o.shape for o in owns]

    def body(*refs):
        own_r, recv_r, red_r, stage_r, sem = refs[0:n], refs[n:2 * n], refs[2 * n:3 * n], refs[3 * n:4 * n], refs[4 * n]
        x, y, c, _ = _mesh_pos()
        me = 2 * x + y
        stores = []
        for i, (R, _) in enumerate(dims):
            for k_me in range(N_CHIPS):

                @pl.when(me == k_me)
                def _():
                    acc = None
                    for k in range(N_CHIPS):
                        slot = ((k // 2) ^ (k_me // 2)) + 2 * ((k % 2) ^ (k_me % 2)) - 1
                        term = own_r[i][...] if k == k_me else recv_r[i][slot].astype(F32)
                        acc = term if acc is None else acc + term
                    stage_r[i][...] = acc

            mine, _ = _half_rows(c, R)
            cp = pltpu.make_async_copy(stage_r[i], red_r[i].at[mine, :], sem.at[i])
            cp.start()
            stores.append(cp)
        for cp in stores:
            cp.wait()

    vm = pl.BlockSpec(memory_space=pltpu.VMEM)
    return pl.pallas_call(
        body, name=name, in_specs=[vm] * (2 * n), out_specs=[pl.BlockSpec(memory_space=pl.ANY)] * n,
        out_shape=[jax.ShapeDtypeStruct((2 * R, C), F32) for R, C in dims],
        scratch_shapes=[pltpu.VMEM((R, C), F32) for R, C in dims] + [pltpu.SemaphoreType.DMA((n,))],
        compiler_params=pltpu.CompilerParams(vmem_limit_bytes=VMEM_LIMIT),
    )(*owns, *recvs)


def _gather_bf16(shard, name):
    R2, C = shard.shape
    R = R2 // 2
    H = R // 2

    def body(s_ref, o_ref, send_sems, recv_sems):
        x, y, c, _ = _mesh_pos()
        me, chip_x, chip_y, chip_d = 2 * x + y, 2 * (1 - x) + y, 2 * x + (1 - y), 2 * (1 - x) + (1 - y)
        to_x, to_y, me_dev, sibling = (1 - x, y, c), (x, 1 - y, c), (x, y, c), (x, y, 1 - c)

        def rows(core, off, n):
            return pl.ds(pl.multiple_of(core * R + off, H), n)

        def copy(k, chip, rws, to):
            blk = o_ref.at[chip, rws]
            return _remote(blk, blk, (send_sems.at[k], recv_sems.at[k]), to)

        piece, half_a, half_b = rows(c, 0, R), rows(c, 0, H), rows(c, H, H)
        o_ref[me] = s_ref[...].astype(BF16)
        sends = [copy(0, me, piece, to_x), copy(1, me, piece, to_y)]
        for cp in sends:
            cp.start()
        arrivals = [(0, chip_x, piece, (2, half_a, to_y)), (1, chip_y, piece, (3, half_b, to_x)),
                    (2, chip_d, half_a, None), (3, chip_d, half_b, None)]
        for k, chip, rws, onward in arrivals:
            copy(k, chip, rws, me_dev).wait_recv()
            if onward is not None:
                sends.append(copy(onward[0], chip, onward[1], onward[2]))
                sends[-1].start()
            sends.append(copy(4 + k, chip, rws, sibling))
            sends[-1].start()
        for k, chip, rws in [(4, chip_x, rows(1 - c, 0, R)), (5, chip_y, rows(1 - c, 0, R)),
                             (6, chip_d, rows(1 - c, 0, H)), (7, chip_d, rows(1 - c, H, H))]:
            copy(k, chip, rws, me_dev).wait_recv()
        for cp in sends:
            cp.wait_send()

    return pl.pallas_call(
        body, name=name, out_shape=jax.ShapeDtypeStruct((N_CHIPS, R2, C), BF16),
        in_specs=[pl.BlockSpec(memory_space=pltpu.VMEM)], out_specs=pl.BlockSpec(memory_space=pltpu.VMEM),
        scratch_shapes=[pltpu.SemaphoreType.DMA((8,)), pltpu.SemaphoreType.DMA((8,))],
        compiler_params=pltpu.CompilerParams(vmem_limit_bytes=VMEM_LIMIT),
    )(shard)


def _pair_exchange_sum(partial, name):
    _, R2, C = partial.shape
    R = R2 // 2

    def body(p_ref, send_ref, own_ref, mine_ref, sib_ref, loc_sems, send_sems, recv_sems):
        x, y, c, chips = _mesh_pos()
        me = 2 * x + y
        mine, theirs = _half_rows(c, R)
        order = [2 * cx + cy for cx, cy in chips] + [me]
        locs, pairs = [], []
        for i, k in enumerate(order):
            loc = pltpu.make_async_copy(p_ref.at[k, mine, :], mine_ref.at[i], loc_sems.at[i])
            pair = _remote(p_ref.at[k, theirs, :], sib_ref.at[i], (send_sems.at[i], recv_sems.at[i]), (x, y, 1 - c))
            loc.start()
            pair.start()
            locs.append(loc)
            pairs.append(pair)
        for i in range(N_CHIPS):
            locs[i].wait()
            pairs[i].wait_recv()
            total = mine_ref[i] + sib_ref[i]
            if i < 3:
                send_ref[i] = total.astype(BF16)
            else:
                own_ref[...] = total
        for pair in pairs:
            pair.wait_send()

    vm = pl.BlockSpec(memory_space=pltpu.VMEM)
    return pl.pallas_call(
        body, name=name, in_specs=[pl.BlockSpec(memory_space=pl.ANY)], out_specs=[vm, vm],
        out_shape=[jax.ShapeDtypeStruct((3, R, C), BF16), jax.ShapeDtypeStruct((R, C), F32)],
        scratch_shapes=[pltpu.VMEM((N_CHIPS, R, C), F32), pltpu.VMEM((N_CHIPS, R, C), F32),
                        pltpu.SemaphoreType.DMA((N_CHIPS,)), pltpu.SemaphoreType.DMA((N_CHIPS,)),
                        pltpu.SemaphoreType.DMA((N_CHIPS,))],
        compiler_params=pltpu.CompilerParams(vmem_limit_bytes=VMEM_LIMIT),
    )(partial)


def _allreduce_small(buf, name):
    rows, width = buf.shape
    h = rows // 2

    def body(b_ref, o_ref, sib_ref, pair_ref, in_ref, pair_sems, send_sems, recv_sems, fin_sems):
        x, y, c, chips = _mesh_pos()
        me = 2 * x + y
        mine, theirs = _half_rows(c, h)
        sibling = (x, y, 1 - c)
        pair = _remote(b_ref.at[theirs], sib_ref, (pair_sems.at[0], pair_sems.at[1]), sibling)
        pair.start()
        pair.wait()
        pair_ref[...] = b_ref[mine, :] + sib_ref[...]
        sends = []
        for j, (cx, cy) in enumerate(chips):
            cp = _remote(pair_ref, in_ref.at[j], (send_sems.at[j], recv_sems.at[j]), (cx, cy, c))
            cp.start()
            sends.append(cp)
        for cp in sends:
            cp.wait_recv()
        acc = None
        for k in range(N_CHIPS):
            term = jnp.where(me == k, pair_ref[...], in_ref[_peer_slot(k, x, y)])
            acc = term if acc is None else acc + term
        o_ref[mine, :] = acc
        fin = _remote(o_ref.at[mine], o_ref.at[mine], (fin_sems.at[0], fin_sems.at[1]), sibling)
        fin.start()
        fin.wait_send()
        _remote(o_ref.at[theirs], o_ref.at[theirs], (fin_sems.at[0], fin_sems.at[1]), sibling).wait_recv()
        for cp in sends:
            cp.wait_send()

    return pl.pallas_call(
        body, name=name, out_shape=jax.ShapeDtypeStruct((rows, width), F32),
        in_specs=[pl.BlockSpec(memory_space=pltpu.VMEM)], out_specs=pl.BlockSpec(memory_space=pltpu.VMEM),
        scratch_shapes=[pltpu.VMEM((h, width), F32), pltpu.VMEM((h, width), F32), pltpu.VMEM((3, h, width), F32),
                        pltpu.SemaphoreType.DMA((2,)), pltpu.SemaphoreType.DMA((3,)), pltpu.SemaphoreType.DMA((3,)),
                        pltpu.SemaphoreType.DMA((2,))],
        compiler_params=pltpu.CompilerParams(vmem_limit_bytes=VMEM_LIMIT),
    )(buf)


def _adamw_small(ws, gs, ms, vs):
    n = len(ws)
    c1 = 1.0 / (1.0 - ADAM_B1 ** ADAM_STEP)
    c2 = 1.0 / (1.0 - ADAM_B2 ** ADAM_STEP)

    def body(*refs):
        w_r, g_r, m_r, v_r = refs[0:n], refs[n:2 * n], refs[2 * n:3 * n], refs[3 * n:4 * n]
        d_r, nm_r, nv_r = refs[4 * n:5 * n], refs[5 * n:6 * n], refs[6 * n:7 * n]
        for i in range(n):
            gv = g_r[i][...]
            nm = ADAM_B1 * m_r[i][...] + (1.0 - ADAM_B1) * gv
            nv = ADAM_B2 * v_r[i][...] + (1.0 - ADAM_B2) * (gv * gv)
            nm_r[i][...] = nm
            nv_r[i][...] = nv
            d_r[i][...] = (-ADAM_LR) * ((nm * c1) / (jnp.sqrt(nv * c2) + ADAM_EPS) + ADAM_WD * w_r[i][...])

    vm = pl.BlockSpec(memory_space=pltpu.VMEM)
    sds = [jax.ShapeDtypeStruct(w.shape, F32) for w in ws]
    out = pl.pallas_call(body, name="adamw_small", in_specs=[vm] * (4 * n), out_specs=[vm] * (3 * n),
                         out_shape=sds * 3)(*ws, *gs, *ms, *vs)
    return out[0:n], out[n:2 * n], out[2 * n:3 * n]


_BIG = ("w_in", "w_rnn_proj", "w_attn_proj", "w_out", "w_up", "w_down", "w_ple_gate", "w_ple_proj")
_SMALL = ("g_mix", "conv_w", "conv_b", "w_rg", "b_rg", "w_ig", "b_ig", "lru_lambda", "q_gain", "k_gain", "sinks",
          "g_mlp", "g_ple")
_WEIGHTS = ("g_mix", "w_in", "conv_w", "conv_b", "w_rg", "b_rg", "w_ig", "b_ig", "lru_lambda", "w_rnn_proj",
            "q_gain", "k_gain", "sinks", "w_attn_proj", "w_out", "g_mlp", "w_up", "w_down", "g_ple", "w_ple_gate",
            "w_ple_proj")


def _pad_row(v):
    v = v.reshape(1, -1)
    return jnp.pad(v, ((0, 0), (0, D_MODEL - v.shape[1])))


def kernel(x, p, g_mix, w_in, conv_w, conv_b, w_rg, b_rg, w_ig, b_ig, lru_lambda, w_rnn_proj, q_gain, k_gain, sinks, w_attn_proj, w_out, g_mlp, w_up, w_down, g_ple, w_ple_gate, w_ple_proj, loss_target, m_g_mix, m_w_in, m_conv_w, m_conv_b, m_w_rg, m_b_rg, m_w_ig, m_b_ig, m_lru_lambda, m_w_rnn_proj, m_q_gain, m_k_gain, m_sinks, m_w_attn_proj, m_w_out, m_g_mlp, m_w_up, m_w_down, m_g_ple, m_w_ple_gate, m_w_ple_proj, v_g_mix, v_w_in, v_conv_w, v_conv_b, v_w_rg, v_b_rg, v_w_ig, v_b_ig, v_lru_lambda, v_w_rnn_proj, v_q_gain, v_k_gain, v_sinks, v_w_attn_proj, v_w_out, v_g_mlp, v_w_up, v_w_down, v_g_ple, v_w_ple_gate, v_w_ple_proj):
    w = dict(g_mix=g_mix, w_in=w_in, conv_w=conv_w, conv_b=conv_b, w_rg=w_rg, b_rg=b_rg, w_ig=w_ig, b_ig=b_ig,
             lru_lambda=lru_lambda, w_rnn_proj=w_rnn_proj, q_gain=q_gain, k_gain=k_gain, sinks=sinks,
             w_attn_proj=w_attn_proj, w_out=w_out, g_mlp=g_mlp, w_up=w_up, w_down=w_down, g_ple=g_ple,
             w_ple_gate=w_ple_gate, w_ple_proj=w_ple_proj)
    m = dict(g_mix=m_g_mix, w_in=m_w_in, conv_w=m_conv_w, conv_b=m_conv_b, w_rg=m_w_rg, b_rg=m_b_rg, w_ig=m_w_ig,
             b_ig=m_b_ig, lru_lambda=m_lru_lambda, w_rnn_proj=m_w_rnn_proj, q_gain=m_q_gain, k_gain=m_k_gain,
             sinks=m_sinks, w_attn_proj=m_w_attn_proj, w_out=m_w_out, g_mlp=m_g_mlp, w_up=m_w_up, w_down=m_w_down,
             g_ple=m_g_ple, w_ple_gate=m_w_ple_gate, w_ple_proj=m_w_ple_proj)
    v = dict(g_mix=v_g_mix, w_in=v_w_in, conv_w=v_conv_w, conv_b=v_conv_b, w_rg=v_w_rg, b_rg=v_b_rg, w_ig=v_w_ig,
             b_ig=v_b_ig, lru_lambda=v_lru_lambda, w_rnn_proj=v_w_rnn_proj, q_gain=v_q_gain, k_gain=v_k_gain,
             sinks=v_sinks, w_attn_proj=v_w_attn_proj, w_out=v_w_out, g_mlp=v_g_mlp, w_up=v_w_up, w_down=v_w_down,
             g_ple=v_g_ple, w_ple_gate=v_w_ple_gate, w_ple_proj=v_w_ple_proj)
    n_seq, S, _ = x.shape
    T = n_seq * S
    chip = 2 * lax.axis_index("x") + lax.axis_index("y")

    tm, tm_rnn = 512, 256
    xf, pf, tf = x.reshape(T, D_MODEL), p.reshape(T, PLE_DIM), loss_target.reshape(T, D_MODEL)
    first = lambda outs: [o[0] for o in outs]

    w_in_g = _gather_bf16(w["w_in"][0], "gather_w_in")
    wb = {name: w[name][0].astype(BF16) for name in _BIG if name != "w_in"}
    grp_mix, grp_mlp, grp_ple = ("w_rnn_proj", "w_attn_proj", "w_out"), ("w_up", "w_down"), ("w_ple_gate", "w_ple_proj")

    wb["conv_w"] = jnp.pad(conv_w[0], ((0, 16 - CONV_W), (0, 0)))

    cosf, sins = _rope_tables(S)
    ind_q, ind_qt = _indicator(D_MODEL)
    ind_k, ind_kt = _indicator(KV_W)
    wrg2 = _pair_blockdiag(w_rg[0]).astype(BF16)
    wig2 = _pair_blockdiag(w_ig[0]).astype(BF16)
    qg = jnp.tile(q_gain, (1, N_HEADS))
    kg = jnp.tile(k_gain, (1, N_KV))
    sk = sinks.reshape(N_HEADS)
    attn_c = (qg, kg, sk, cosf, sins, ind_q, ind_qt, ind_k, ind_kt, n_seq, S)

    (h0, xr, gr, zq, zk, zv, ga, gb), ph = _inproj_fwd(xf, g_mix, w_in_g, tm,
                                                     phases=[_ph_gather_send(wb[n]) for n in grp_mix + ("conv_w",)])
    g_small = first(ph)
    o, ph = _attn_fwd(zq, zk, zv, *attn_c,
                      phases=[_ph_gather_pass(g) for g in g_small]
                      + [_ph_gather_send(wb[n]) for n in ("w_up",) + grp_ple])
    g_small, (wu, wpg, wpp) = first(ph[:4]), first(ph[4:])
    cw_full = g_small[3][:, :CONV_W, :].transpose(1, 0, 2).reshape(CONV_W, D_MODEL)
    rnn_w = (cw_full, conv_b, wrg2, b_rg, wig2, b_ig, lru_lambda)
    (xc, h, *gates, ya), ph = _rnn_fwd(xr, gr, *rnn_w, n_seq, S, tm_rnn,
                               phases=[_ph_gather_pass(g) for g in (wu, wpg, wpp)]
                               + [_ph_gather_send(wb["w_down"])])
    (wu, wpg, wpp), wd = first(ph[:3]), ph[3][0]
    wr, wa, wo = (g.reshape(D_MODEL, D_MODEL) for g in g_small[:3])
    wpg = wpg.reshape(D_MODEL, D_MODEL)
    (x1, merged, y_a, y_b), ph = _merge_fwd(xf, ya, o, ga, gb, wr, wa, wo, tm, phases=[_ph_gather_pass(wd)])
    wd = ph[0][0].reshape(D_FF, D_MODEL)
    (x2, hm, u, act), _ = _mlp_fwd(x1, g_mlp, wu, wd, tm // 2)
    (loss_t, dx2, pb, de, hp, dtg, dg_ple), _ = _ple_loss(x2, pf, tf, g_ple, wpg, wpp, tm)

    chipmajor = lambda g: g.reshape(N_CHIPS, g.shape[-2] // N_CHIPS, g.shape[-1]) if g.ndim == 2 else g
    tmw = min(2 * tm, T)
    dw_pp = _wgrad(pb, de, "wgrad_ple_proj", False, D_MODEL, tmw)[0]
    part_ple = [chipmajor(_wgrad(hp, dtg, "wgrad_ple_gate", False, D_MODEL, tmw)[0]),
                dw_pp.reshape(PLE_DIM, N_CHIPS, D_MODEL // N_CHIPS).transpose(1, 0, 2)]
    (dx1, du, dg_mlp), ph = _mlp_bwd(dx2, u, x1, g_mlp, wu, wd, tm // 2, phases=[_ph_pair_send(g) for g in part_ple])
    send_ple, own_ple = _pair_sum(part_ple, first(ph), "pair_sum_ple")
    dw_down, ph = _wgrad(act, dx2, "wgrad_down", False, D_MODEL // 2, tmw, phases=[_ph_chip_send(s) for s in send_ple])
    red_ple = _chip_sum(own_ple, first(ph), "chip_sum_ple")
    part_mlp = [_wgrad(hm, du, "wgrad_up", True, D_MODEL, tmw)[0], chipmajor(dw_down)]
    (dga, dgb, dya, dyb, dyain, do), _ = _merge_bwd(dx1, ga, gb, y_a, y_b, wr, wa, wo, tm)
    dw_rnn, ph_up = _wgrad(ya, dya, "wgrad_rnn_proj", False, D_MODEL, tmw, phases=[_ph_pair_send(part_mlp[0])])
    dw_attn, ph_down = _wgrad(o, dyb, "wgrad_attn_proj", False, D_MODEL, tmw, phases=[_ph_pair_send(part_mlp[1])])
    dw_out, ph = _wgrad(merged, dx1, "wgrad_out", False, D_MODEL, tmw, phases=[_ph_half_swap(r) for r in red_ple])
    red_ple = first(ph)
    send_mlp, own_mlp = _pair_sum(part_mlp, [ph_up[0][0], ph_down[0][0]], "pair_sum_mlp")
    part_mix = [chipmajor(dw_rnn), chipmajor(dw_attn), chipmajor(dw_out)]
    (dxr, dgr, vec, dwrg2, dwig2), ph = _rnn_bwd(
        dyain, xr, gr, xc, h, gates, cw_full, wrg2, wig2, lru_lambda, n_seq, S, tm_rnn,
        phases=[_ph_chip_send(s) for s in send_mlp] + [_ph_pair_send(g) for g in part_mix])
    red_mlp = _chip_sum(own_mlp, first(ph[:2]), "chip_sum_mlp")
    send_mix, own_mix = _pair_sum(part_mix, first(ph[2:]), "pair_sum_mix")
    (dq, dkc, dkp, dvc, dvp, dqg, dsk), ph = _attn_bwd(
        do, zq, zk, zv, *attn_c, phases=[_ph_half_swap(r) for r in red_mlp] + [_ph_chip_send(s) for s in send_mix])
    red_mlp = first(ph[:2])
    red_mix = _chip_sum(own_mix, first(ph[2:]), "chip_sum_mix")
    (dk, dv, dkg), _ = _kv_bwd(dkc, dkp, dvc, dvp, zk, kg, cosf, sins, ind_k, ind_kt, n_seq, S)
    dz_parts = [dxr, dgr, dq, dk, dv, dga, dgb]
    send_in, own_in = _pair_exchange_sum(_wgrad_in(h0, dz_parts, tm), "pair_sum_in")
    (grad_x, dg_mix), ph = _inproj_bwd(dz_parts, w_in_g, xf, g_mix, dx1, tm,
                                       phases=[_ph_half_swap(r) for r in red_mix] + [_ph_chip_send(send_in)])
    red_mix = first(ph[:3])
    red_in = _chip_sum([own_in], first(ph[3:]), "chip_sum_in")
    reduced = dict(zip(grp_ple + grp_mlp + grp_mix, red_ple + red_mlp + red_mix))
    grads = {
        "g_mix": dg_mix[0], "g_mlp": dg_mlp[0], "g_ple": dg_ple[0],
        "conv_w": vec[0:CONV_W], "conv_b": vec[4], "b_rg": vec[5], "b_ig": vec[6], "lru_lambda": vec[7],
        "w_rg": _pair_blockdiag_extract(dwrg2), "w_ig": _pair_blockdiag_extract(dwig2),
        "q_gain": dqg.reshape(N_HEADS, HEAD_DIM).sum(0), "k_gain": dkg.reshape(N_KV, HEAD_DIM).sum(0),
        "sinks": dsk.sum(1),
    }

    rows = [grads["conv_w"], _pad_row(grads["conv_b"]), _pad_row(grads["b_rg"]), _pad_row(grads["b_ig"]),
            _pad_row(grads["lru_lambda"]), _pad_row(grads["g_mix"]), _pad_row(grads["g_mlp"]),
            _pad_row(grads["g_ple"]), _pad_row(grads["q_gain"]), _pad_row(grads["k_gain"]), _pad_row(grads["sinks"]),
            _pad_row(loss_t[0:1, 0:1]), jnp.zeros((1, D_MODEL), F32)]
    vecs = jnp.concatenate(rows, axis=0)
    packed = jnp.concatenate([vecs.reshape(-1, LANES), grads["w_rg"].reshape(-1, LANES),
                              grads["w_ig"].reshape(-1, LANES)], axis=0)
    red = _allreduce_small(packed, "allreduce_small")
    nv = vecs.size // LANES
    rvec = red[0:nv].reshape(16, D_MODEL)
    loss = rvec[14, 0]
    nw = grads["w_rg"].size // LANES
    sg = {
        "conv_w": lax.dynamic_slice(rvec[0:CONV_W], (0, chip * (D_MODEL // N_CHIPS)), (CONV_W, D_MODEL // N_CHIPS)),
        "conv_b": rvec[4], "b_rg": rvec[5], "b_ig": rvec[6], "lru_lambda": rvec[7], "g_mix": rvec[8],
        "g_mlp": rvec[9], "g_ple": rvec[10], "q_gain": rvec[11, :HEAD_DIM], "k_gain": rvec[12, :HEAD_DIM],
        "sinks": rvec[13, :N_HEADS], "w_rg": red[nv:nv + nw], "w_ig": red[nv + nw:nv + 2 * nw],
    }
    sg = {k: sg[k].reshape(w[k].shape) for k in _SMALL}
    d_s, m_s, v_s = _adamw_small([w[k] for k in _SMALL], [sg[k] for k in _SMALL], [m[k] for k in _SMALL],
                                 [v[k] for k in _SMALL])
    grad, delta, new_m, new_v = dict(sg), dict(zip(_SMALL, d_s)), dict(zip(_SMALL, m_s)), dict(zip(_SMALL, v_s))

    for name in ("w_ple_proj", "w_up", "w_down", "w_rnn_proj", "w_attn_proj", "w_out", "w_ple_gate", "w_in"):
        shape = w[name].shape
        outs, ph = _adamw(w[name][0], reduced[name], m[name][0], v[name][0], "adamw_" + name, 128,
                          phases=[_ph_half_swap(r) for r in red_in] if name == "w_ple_proj" else ())
        if name == "w_ple_proj":
            reduced["w_in"] = ph[0][0]
        grad[name], delta[name], new_m[name], new_v[name] = (a.reshape(shape) for a in outs)

    return (loss, grad_x.reshape(x.shape), *[grad[k] for k in _WEIGHTS], *[delta[k] for k in _WEIGHTS],
            *[new_m[k] for k in _WEIGHTS], *[new_v[k] for k in _WEIGHTS])
```

```python
import functools
import math

import numpy as np
import jax
import jax.numpy as jnp
from jax import lax
from jax.experimental import pallas as pl
from jax.experimental.pallas import tpu as pltpu

F32 = jnp.float32
BF16 = jnp.bfloat16

D_MODEL = 1024
N_HEADS = 16
N_KV = 4
HEAD_DIM = 64
KV_W = N_KV * HEAD_DIM
D_FF = 4096
PLE_DIM = 256
WINDOW = 128
CONV_W = 4
LRU_C = 8.0
NORM_EPS = 1e-6
ROPE_THETA = 10000.0
N_CHIPS = 4
IN_TOTAL = 5632
IN_BLK = IN_TOTAL // N_CHIPS
IN_SEGS = (0, 1024, 2048, 3072, 3328, 3584, 4608, 5632)

ADAM_LR = 0.001
ADAM_B1 = 0.9
ADAM_B2 = 0.999
ADAM_EPS = 1e-08
ADAM_WD = 0.01
ADAM_STEP = 10

LANES = 128
V7X_VMEM_BYTES = 64 * 1024 * 1024
VMEM_LIMIT = V7X_VMEM_BYTES - 8 * 1024 * 1024
MESH_ID = pl.DeviceIdType.MESH
TM, TM_RNN, ADAMW_ROWS = 512, 256, 256


def _dot(a, b):
    return jnp.dot(a, b, preferred_element_type=F32)


def _dot_nt(a, b):
    return lax.dot_general(a, b, (((1,), (1,)), ((), ())), preferred_element_type=F32)


def _dot_tn(a, b):
    return lax.dot_general(a, b, (((0,), (0,)), ((), ())), preferred_element_type=F32)


def _split_dot(x, ind):
    hi = x.astype(BF16)
    lo = (x - hi.astype(F32)).astype(BF16)
    return _dot(hi, ind) + _dot(lo, ind)


def _sigmoid(x):
    return 1.0 / (1.0 + jnp.exp(-x))


_GELU_C = math.sqrt(2.0 / math.pi)


def _gelu_and_grad(g):
    inner = _GELU_C * (g + 0.044715 * g * g * g)
    t = jnp.tanh(inner)
    gelu = 0.5 * g * (1.0 + t)
    dgelu = 0.5 * (1.0 + t) + 0.5 * g * (1.0 - t * t) * _GELU_C * (1.0 + 3.0 * 0.044715 * g * g)
    return gelu, dgelu


def _const(shape):
    nd = len(shape)
    return pl.BlockSpec(shape, lambda *_: (0,) * nd)


def _params(n_grid, vmem=VMEM_LIMIT):
    return pltpu.CompilerParams(dimension_semantics=("arbitrary",) * n_grid, vmem_limit_bytes=vmem)


def _rms_fwd(x, g):
    r = lax.rsqrt(jnp.mean(x * x, axis=-1, keepdims=True) + NORM_EPS)
    return (x * r) * g, r


def _rms_bwd(dy, x, r, g):
    dn = dy * g
    dx = r * dn - x * (r * r * r * jnp.mean(dn * x, axis=-1, keepdims=True))
    dg = jnp.sum(dy * (x * r), axis=0, keepdims=True)
    return dx, dg


def _seg_pieces(blk_lo, blk_hi):
    out = []
    for s in range(7):
        lo, hi = max(blk_lo, IN_SEGS[s]), min(blk_hi, IN_SEGS[s + 1])
        if lo < hi:
            out.append((s, lo - IN_SEGS[s], hi - IN_SEGS[s], lo - blk_lo))
    return out


def _mesh_pos():
    x, y, c = lax.axis_index("x"), lax.axis_index("y"), lax.axis_index("c")
    other_chips = [(1 - x, y), (x, 1 - y), (1 - x, 1 - y)]
    return x, y, c, other_chips


def _peer_slot(k, x, y):
    dx = jnp.bitwise_xor(k // 2, x)
    dy = jnp.bitwise_xor(k % 2, y)
    return jnp.maximum(dx + 2 * dy - 1, 0)


def _half_rows(c, R):
    return pl.ds(pl.multiple_of(c * R, R), R), pl.ds(pl.multiple_of((1 - c) * R, R), R)


def _remote(src, dst, sems, to):
    return pltpu.make_async_remote_copy(src_ref=src, dst_ref=dst, send_sem=sems[0], recv_sem=sems[1],
                                        device_id=to, device_id_type=MESH_ID)


class _Phase:
    def __init__(self, ins, inout, outs, n_remote, n_local, build):
        self.ins, self.inout, self.outs = list(ins), list(inout), list(outs)
        self.n_remote, self.n_local, self.build = n_remote, n_local, build


def _ph_gather_send(wb):
    R2, C = wb.shape
    R = R2 // 2

    def build(ins, outs, rsem, lsem):
        (w_ref,), (g_ref,) = ins, outs
        x, y, c, chips = _mesh_pos()
        me = 2 * x + y
        mine, _ = _half_rows(c, R)
        loc = [pltpu.make_async_copy(w_ref, g_ref.at[me], lsem(0))]
        outg = [_remote(w_ref.at[mine], g_ref.at[me, mine], rsem(j), (cx, cy, c)) for j, (cx, cy) in enumerate(chips)]
        inc = [functools.partial(_remote, w_ref.at[mine], g_ref.at[2 * cx + cy, mine], rsem(j), (x, y, c))
               for j, (cx, cy) in enumerate(chips)]
        return loc, outg, inc

    return _Phase([wb], [], [jax.ShapeDtypeStruct((N_CHIPS, R2, C), wb.dtype)], 3, 1, build)


def _ph_gather_pass(gath):
    _, R2, C = gath.shape
    R = R2 // 2

    def build(ins, outs, rsem, lsem):
        (g_ref,) = outs
        x, y, c, chips = _mesh_pos()
        mine, theirs = _half_rows(c, R)
        outg, inc = [], []
        for j, (cx, cy) in enumerate(chips):
            blk = g_ref.at[2 * cx + cy, mine]
            outg.append(_remote(blk, blk, rsem(j), (x, y, 1 - c)))
            got = g_ref.at[2 * cx + cy, theirs]
            inc.append(functools.partial(_remote, got, got, rsem(j), (x, y, c)))
        return [], outg, inc

    return _Phase([], [gath], [], 3, 0, build)


def _ph_pair_send(partial):
    _, R2, C = partial.shape
    R = R2 // 2

    def build(ins, outs, rsem, lsem):
        (p_ref,), (s_ref,) = ins, outs
        x, y, c, _ = _mesh_pos()
        _, theirs = _half_rows(c, R)
        src = p_ref.at[:, theirs, :]
        return ([], [_remote(src, s_ref, rsem(0), (x, y, 1 - c))],
                [functools.partial(_remote, src, s_ref, rsem(0), (x, y, c))])

    return _Phase([partial], [], [jax.ShapeDtypeStruct((N_CHIPS, R, C), F32)], 1, 0, build)


def _ph_chip_send(sendb):
    def build(ins, outs, rsem, lsem):
        (s_ref,), (r_ref,) = ins, outs
        x, y, c, chips = _mesh_pos()
        outg = [_remote(s_ref.at[j], r_ref.at[j], rsem(j), (cx, cy, c)) for j, (cx, cy) in enumerate(chips)]
        inc = [functools.partial(_remote, s_ref.at[j], r_ref.at[j], rsem(j), (x, y, c)) for j in range(3)]
        return [], outg, inc

    return _Phase([sendb], [], [jax.ShapeDtypeStruct(sendb.shape, sendb.dtype)], 3, 0, build)


def _ph_half_swap(red):
    R2, C = red.shape
    R = R2 // 2

    def build(ins, outs, rsem, lsem):
        (r_ref,) = outs
        x, y, c, _ = _mesh_pos()
        mine, theirs = _half_rows(c, R)
        return ([], [_remote(r_ref.at[mine], r_ref.at[mine], rsem(0), (x, y, 1 - c))],
                [functools.partial(_remote, r_ref.at[theirs], r_ref.at[theirs], rsem(0), (x, y, c))])

    return _Phase([], [red], [], 1, 0, build)


def _call(body, *, name, grid, in_specs, out_specs, out_shape, scratch_shapes=(), phases=()):
    single = not isinstance(out_specs, (list, tuple))
    out_specs = [out_specs] if single else list(out_specs)
    out_shape = [out_shape] if single else list(out_shape)
    n_in, n_out, n_scr = len(in_specs), len(out_specs), len(scratch_shapes)
    if not phases:
        call = pl.pallas_call(body, name=name, grid=grid, in_specs=in_specs, out_specs=out_specs,
                              out_shape=out_shape, scratch_shapes=list(scratch_shapes),
                              compiler_params=_params(len(grid)))
        return lambda *operands: (list(call(*operands)), [])

    ex_in, ex_out, aliases, spans = [], [], {}, []
    for ph in phases:
        i0, o0 = len(ex_in), len(ex_out)
        ex_in += ph.ins
        for a in ph.inout:
            aliases[n_in + len(ex_in)] = n_out + len(ex_out)
            ex_in.append(a)
            ex_out.append(jax.ShapeDtypeStruct(a.shape, a.dtype))
        ex_out += ph.outs
        spans.append((i0, len(ph.ins), o0, len(ex_out) - o0))
    n_remote = sum(ph.n_remote for ph in phases)
    n_local = max(sum(ph.n_local for ph in phases), 1)

    def wrapped(*refs):
        base_in, xin = refs[:n_in], refs[n_in:n_in + len(ex_in)]
        o0 = n_in + len(ex_in)
        base_out, xout = refs[o0:o0 + n_out], refs[o0 + n_out:o0 + n_out + len(ex_out)]
        scr = refs[o0 + n_out + len(ex_out):]
        send_sems, recv_sems, loc_sems = scr[n_scr:]
        first = functools.reduce(jnp.logical_and, [pl.program_id(i) == 0 for i in range(len(grid))])
        last = functools.reduce(jnp.logical_and, [pl.program_id(i) == grid[i] - 1 for i in range(len(grid))])

        def copies():
            out, r0, l0 = [], 0, 0
            for ph, (i0, ni, p0, no) in zip(phases, spans):
                rsem = lambda k, r0=r0: (send_sems.at[r0 + k], recv_sems.at[r0 + k])
                lsem = lambda k, l0=l0: loc_sems.at[l0 + k]
                out.append(ph.build(xin[i0:i0 + ni], xout[p0:p0 + no], rsem, lsem))
                r0, l0 = r0 + ph.n_remote, l0 + ph.n_local
            return out

        @pl.when(first)
        def _():
            for loc, outg, _ in copies():
                for cp in loc + outg:
                    cp.start()

        body(*base_in, *base_out, *scr[:n_scr])

        @pl.when(last)
        def _():
            for loc, outg, inc in copies():
                for make in inc:
                    make().wait_recv()
                for cp in outg:
                    cp.wait_send()
                for cp in loc:
                    cp.wait()

    hbm = pl.BlockSpec(memory_space=pl.ANY)
    call = pl.pallas_call(
        wrapped, name=name, grid=grid, in_specs=list(in_specs) + [hbm] * len(ex_in),
        out_specs=out_specs + [hbm] * len(ex_out), out_shape=out_shape + ex_out,
        scratch_shapes=list(scratch_shapes) + [pltpu.SemaphoreType.DMA((n_remote,)), pltpu.SemaphoreType.DMA((n_remote,)),
                                              pltpu.SemaphoreType.DMA((n_local,))],
        input_output_aliases=aliases, compiler_params=_params(len(grid)))

    def run(*operands):
        res = call(*operands, *ex_in)
        extra = res[n_out:]
        return list(res[:n_out]), [list(extra[p0:p0 + no]) for (_, _, p0, no) in spans]

    return run


def _inproj_fwd(x, g_mix, w_in, tm, phases=()):
    T = x.shape[0]
    widths = [IN_SEGS[i + 1] - IN_SEGS[i] for i in range(7)]

    def body(x_ref, g_ref, w_ref, h_ref, *z_refs):
        h, _ = _rms_fwd(x_ref[...], g_ref[...])
        hb = h.astype(BF16)
        h_ref[...] = hb
        for j in range(N_CHIPS):
            zj = _dot(hb, w_ref[j])
            for s, lo, hi, off in _seg_pieces(j * IN_BLK, (j + 1) * IN_BLK):
                z_refs[s][:, lo:hi] = zj[:, off:off + hi - lo]

    return _call(
        body, phases=phases, name="inproj_fwd", grid=(T // tm,),
        in_specs=[pl.BlockSpec((tm, D_MODEL), lambda i: (i, 0)), _const((1, D_MODEL)),
                  _const((N_CHIPS, D_MODEL, IN_BLK))],
        out_specs=[pl.BlockSpec((tm, D_MODEL), lambda i: (i, 0))]
        + [pl.BlockSpec((tm, w), lambda i: (i, 0)) for w in widths],
        out_shape=[jax.ShapeDtypeStruct((T, D_MODEL), BF16)]
        + [jax.ShapeDtypeStruct((T, w), F32) for w in widths],
    )(x, g_mix, w_in)


def _inproj_bwd(dz_parts, w_in, x, g_mix, dx1, tm, phases=()):
    T = x.shape[0]
    widths = [IN_SEGS[i + 1] - IN_SEGS[i] for i in range(7)]

    def body(*refs):
        p_refs = refs[:7]
        w_ref, x_ref, g_ref, dx1_ref, gx_ref, dg_ref, dz_ref = refs[7:]

        @pl.when(pl.program_id(0) == 0)
        def _():
            dg_ref[...] = jnp.zeros_like(dg_ref)

        for s in range(7):
            dz_ref[:, IN_SEGS[s]:IN_SEGS[s + 1]] = p_refs[s][...]
        dh = jnp.zeros((tm, D_MODEL), F32)
        for j in range(N_CHIPS):
            dh = dh + _dot_nt(dz_ref[:, j * IN_BLK:(j + 1) * IN_BLK], w_ref[j])
        xv = x_ref[...]
        g = g_ref[...]
        _, r = _rms_fwd(xv, g)
        dx, dg = _rms_bwd(dh, xv, r, g)
        gx_ref[...] = dx1_ref[...] + dx
        dg_ref[...] += dg

    row = lambda w: pl.BlockSpec((tm, w), lambda i: (i, 0))
    return _call(
        body, phases=phases, name="inproj_bwd", grid=(T // tm,),
        in_specs=[row(w) for w in widths]
        + [_const((N_CHIPS, D_MODEL, IN_BLK)), row(D_MODEL), _const((1, D_MODEL)), row(D_MODEL)],
        out_specs=[row(D_MODEL), _const((1, D_MODEL))],
        out_shape=[jax.ShapeDtypeStruct((T, D_MODEL), F32), jax.ShapeDtypeStruct((1, D_MODEL), F32)],
        scratch_shapes=[pltpu.VMEM((tm, IN_TOTAL), BF16)],
    )(*dz_parts, w_in, x, g_mix, dx1)


def _wgrad_in(h0, dz_parts, tm):
    T = h0.shape[0]
    widths = [IN_SEGS[i + 1] - IN_SEGS[i] for i in range(7)]

    def body(*refs):
        h_ref, p_refs, o_ref, acc_ref, sems = refs[0], refs[1:8], refs[8], refs[9], refs[10]
        t = pl.program_id(0)
        last = T // tm - 1

        @pl.when(t == 0)
        def _():
            acc_ref[...] = jnp.zeros_like(acc_ref)

        def accumulate(j):
            for s, lo, hi, off in _seg_pieces(j * IN_BLK, (j + 1) * IN_BLK):
                acc_ref[j, :, off:off + hi - lo] += _dot_tn(h_ref[...], p_refs[s][:, lo:hi])

        @pl.when(t < last)
        def _():
            for j in range(N_CHIPS):
                accumulate(j)

        @pl.when(t == last)
        def _():
            copies = [pltpu.make_async_copy(acc_ref.at[j], o_ref.at[j], sems.at[j]) for j in range(N_CHIPS)]
            for j in range(N_CHIPS):
                accumulate(j)
                copies[j].start()
            for cp in copies:
                cp.wait()

    row = lambda w: pl.BlockSpec((tm, w), lambda i: (i, 0))
    return pl.pallas_call(
        body, name="wgrad_in", grid=(T // tm,), in_specs=[row(D_MODEL)] + [row(w) for w in widths],
        out_specs=pl.BlockSpec(memory_space=pl.ANY),
        out_shape=jax.ShapeDtypeStruct((N_CHIPS, D_MODEL, IN_BLK), F32),
        scratch_shapes=[pltpu.VMEM((N_CHIPS, D_MODEL, IN_BLK), F32), pltpu.SemaphoreType.DMA((N_CHIPS,))],
        compiler_params=_params(1),
    )(h0, *dz_parts)


def _wgrad(a, g, name, blocked, cn, tm, phases=()):
    T, K = a.shape
    N = g.shape[1]
    nb = N // cn

    def body(a_ref, g_ref, o_ref):
        @pl.when(pl.program_id(1) == 0)
        def _():
            o_ref[...] = jnp.zeros_like(o_ref)

        o_ref[...] += _dot_tn(a_ref[...].astype(BF16), g_ref[...].astype(BF16))

    if blocked:
        out_spec = pl.BlockSpec((None, K, cn), lambda j, t: (j, 0, 0))
        out_shape = jax.ShapeDtypeStruct((nb, K, cn), F32)
    else:
        out_spec = pl.BlockSpec((K, cn), lambda j, t: (0, j))
        out_shape = jax.ShapeDtypeStruct((K, N), F32)
    outs, extra = _call(
        body, phases=phases, name=name, grid=(nb, T // tm),
        in_specs=[pl.BlockSpec((tm, K), lambda j, t: (t, 0)), pl.BlockSpec((tm, cn), lambda j, t: (t, j))],
        out_specs=out_spec, out_shape=out_shape,
    )(a, g)
    return outs[0], extra


def _shift_down(x, prev8, sft, row, row8, tm):
    xs = pltpu.roll(x, sft, 0)
    top = jnp.where(row8 < sft, pltpu.roll(prev8, sft, 0), xs[0:8])
    return jnp.concatenate([top, xs[8:]], axis=0)


def _shift_up(x, next8, sft, row8, tm):
    xs = pltpu.roll(x, tm - sft, 0)
    bot = jnp.where(row8 >= 8 - sft, pltpu.roll(next8, 8 - sft, 0), xs[tm - 8:tm])
    return jnp.concatenate([xs[0:tm - 8], bot], axis=0)


def _conv_fwd(x, prev8, cw_ref, cb, row, row8, tm):
    xc = cb + cw_ref[CONV_W - 1:CONV_W, :] * x
    for sft in range(1, CONV_W):
        j = CONV_W - 1 - sft
        xc = xc + cw_ref[j:j + 1, :] * _shift_down(x, prev8, sft, row, row8, tm)
    return xc


def _blockdiag_dot(xb, w_ref, transpose):
    outs = []
    for b in range(D_MODEL // LANES):
        xs = xb[:, b * LANES:(b + 1) * LANES]
        outs.append(_dot_nt(xs, w_ref[b]) if transpose else _dot(xs, w_ref[b]))
    return jnp.concatenate(outs, axis=1)


def _softplus_neg(lam):
    e = jnp.exp(-jnp.abs(lam))
    u = 1.0 + e
    log1p_e = jnp.where(u == 1.0, e, jnp.log(u) * (e / (u - 1.0)))
    sp = jnp.maximum(-lam, 0.0) + log1p_e
    return sp, -_sigmoid(-lam)


def _lru_gates(xc, wrg_ref, brg, wig_ref, big, sp):
    xcb = xc.astype(BF16)
    r = _sigmoid(_blockdiag_dot(xcb, wrg_ref, False) + brg)
    i = _sigmoid(_blockdiag_dot(xcb, wig_ref, False) + big)
    log_a = (-LRU_C) * r * sp
    a = jnp.exp(log_a)
    t = jnp.tanh(log_a)
    one_m_a2 = (-2.0) * t / (1.0 - t)
    mult = jnp.sqrt(one_m_a2)
    return xcb, r, i, a, mult


def _scan_down(a, b, row, tm):
    d = 1
    while d < tm:
        if d < 8:
            keep = row >= d
            a_s = jnp.where(keep, pltpu.roll(a, d, 0), 1.0)
            b_s = jnp.where(keep, pltpu.roll(b, d, 0), 0.0)
            b = a * b_s + b
            a = a * a_s
        else:
            b = jnp.concatenate([b[:d], a[d:] * b[:-d] + b[d:]], axis=0)
            a = jnp.concatenate([a[:d], a[d:] * a[:-d]], axis=0)
        d *= 2
    return a, b


def _scan_up(c, b, row, tm):
    d = 1
    while d < tm:
        if d < 8:
            keep = row < tm - d
            c_s = jnp.where(keep, pltpu.roll(c, tm - d, 0), 1.0)
            b_s = jnp.where(keep, pltpu.roll(b, tm - d, 0), 0.0)
            b = c * b_s + b
            c = c * c_s
        else:
            b = jnp.concatenate([c[:-d] * b[d:] + b[:-d], b[-d:]], axis=0)
            c = jnp.concatenate([c[:-d] * c[d:], c[-d:]], axis=0)
        d *= 2
    return c, b


def _rnn_fwd(xr, gr, conv_w, conv_b, wrg2, b_rg, wig2, b_ig, lam, n_seq, S, tm, phases=()):
    T = xr.shape[0]
    nt = S // tm
    W = D_MODEL

    def body(xr_ref, gr_ref, cw_ref, cb_ref, wrg_ref, brg_ref, wig_ref, big_ref, lam_ref,
             xc_ref, h_ref, r_ref, i_ref, a_ref, mult_ref, ya_ref, px_ref, ph_ref):
        @pl.when(pl.program_id(1) == 0)
        def _():
            px_ref[...] = jnp.zeros_like(px_ref)
            ph_ref[...] = jnp.zeros_like(ph_ref)

        row = lax.broadcasted_iota(jnp.int32, (tm, W), 0)
        row8 = lax.broadcasted_iota(jnp.int32, (8, W), 0)
        x = xr_ref[...]
        xc = _conv_fwd(x, px_ref[...], cw_ref, cb_ref[...], row, row8, tm)
        sp, _ = _softplus_neg(lam_ref[...])
        _, r, i, a, mult = _lru_gates(xc, wrg_ref, brg_ref[...], wig_ref, big_ref[...], sp)
        r_ref[...], i_ref[...], a_ref[...], mult_ref[...] = r, i, a, mult
        bterm = mult * (i * xc)
        acum, hloc = _scan_down(a, bterm, row, tm)
        h = hloc + acum * ph_ref[7:8, :]
        h_ref[...] = h
        xc_ref[...] = xc
        gelu, _ = _gelu_and_grad(gr_ref[...])
        ya_ref[...] = (h * gelu).astype(BF16)
        px_ref[...] = xr_ref[tm - 8:tm, :]
        ph_ref[...] = h_ref[tm - 8:tm, :]

    tile = pl.BlockSpec((tm, W), lambda s, t: (s * nt + t, 0))
    return _call(
        body, phases=phases, name="rnn_fwd", grid=(n_seq, nt),
        in_specs=[tile, tile, _const((CONV_W, W)), _const((1, W)), _const((8, LANES, LANES)), _const((1, W)),
                  _const((8, LANES, LANES)), _const((1, W)), _const((1, W))],
        out_specs=[tile] * 7,
        out_shape=[jax.ShapeDtypeStruct((T, W), F32)] * 6 + [jax.ShapeDtypeStruct((T, W), BF16)],
        scratch_shapes=[pltpu.VMEM((8, W), F32), pltpu.VMEM((8, W), F32)],
    )(xr, gr, conv_w, conv_b, wrg2, b_rg, wig2, b_ig, lam)


def _rnn_bwd(dya, xr, gr, xc, h, gates, conv_w, wrg2, wig2, lam, n_seq, S, tm, phases=()):
    T = xr.shape[0]
    nt = S // tm
    W = D_MODEL
    nb8 = tm // 8

    def body(dya_ref, xr_ref, gr_ref, xc_ref, h_ref, r_ref, i_ref, a_ref, mult_ref, xprev_ref, hprev_ref, cw_ref,
             wrg_ref, wig_ref, lam_ref, dxr_ref, dgr_ref, vec_ref, dwrg_ref, dwig_ref, cg_ref, ndxc_ref, tmp_ref):
        s, ti = pl.program_id(0), pl.program_id(1)

        @pl.when((s == 0) & (ti == 0))
        def _():
            vec_ref[...] = jnp.zeros_like(vec_ref)
            dwrg_ref[...] = jnp.zeros_like(dwrg_ref)
            dwig_ref[...] = jnp.zeros_like(dwig_ref)

        @pl.when(ti == 0)
        def _():
            cg_ref[...] = jnp.zeros_like(cg_ref)
            ndxc_ref[...] = jnp.zeros_like(ndxc_ref)

        first = ti == nt - 1
        row = lax.broadcasted_iota(jnp.int32, (tm, W), 0)
        row8 = lax.broadcasted_iota(jnp.int32, (8, W), 0)
        x = xr_ref[...]
        xc = xc_ref[...]
        hv = h_ref[...]
        xprev = jnp.where(first, 0.0, xprev_ref[...])
        hprev = jnp.where(first, 0.0, hprev_ref[...])
        sp, dsp_dlam = _softplus_neg(lam_ref[...])
        xcb = xc.astype(BF16)
        r, i, a, mult = r_ref[...], i_ref[...], a_ref[...], mult_ref[...]

        gelu, dgelu = _gelu_and_grad(gr_ref[...])
        dya_v = dya_ref[...]
        dgr_ref[...] = (dya_v * hv * dgelu).astype(BF16)
        dh = dya_v * gelu
        c = jnp.where(row < tm - 1, pltpu.roll(a, tm - 1, 0), 1.0)
        ccum, gloc = _scan_up(c, dh, row, tm)
        G = gloc + ccum * cg_ref[0:1, :]
        tmp_ref[...] = a * G
        cg_ref[...] = tmp_ref[0:8, :]

        h_m1 = _shift_down(hv, hprev, 1, row, row8, tm)
        ixc = i * xc
        dixc = G * mult
        dlog_a = (G * h_m1) * a - (G * ixc) * (a * a / mult)
        dr = dlog_a * ((-LRU_C) * sp)
        di = dixc * xc
        drg = dr * r * (1.0 - r)
        dig = di * i * (1.0 - i)
        vec_ref[7:8, :] += jnp.sum(dlog_a * ((-LRU_C) * r), axis=0, keepdims=True) * dsp_dlam
        vec_ref[5:6, :] += jnp.sum(drg, axis=0, keepdims=True)
        vec_ref[6:7, :] += jnp.sum(dig, axis=0, keepdims=True)
        drgb = drg.astype(BF16)
        digb = dig.astype(BF16)
        dxc = dixc * i + _blockdiag_dot(drgb, wrg_ref, True) + _blockdiag_dot(digb, wig_ref, True)
        for b in range(W // LANES):
            sl = slice(b * LANES, (b + 1) * LANES)
            dwrg_ref[b] += _dot_tn(xcb[:, sl], drgb[:, sl])
            dwig_ref[b] += _dot_tn(xcb[:, sl], digb[:, sl])

        vec_ref[4:5, :] += jnp.sum(dxc, axis=0, keepdims=True)
        vec_ref[3:4, :] += jnp.sum(dxc * x, axis=0, keepdims=True)
        dxr = cw_ref[CONV_W - 1:CONV_W, :] * dxc
        nxt = ndxc_ref[...]
        for sft in range(1, CONV_W):
            j = CONV_W - 1 - sft
            vec_ref[j:j + 1, :] += jnp.sum(dxc * _shift_down(x, xprev, sft, row, row8, tm), axis=0, keepdims=True)
            dxr = dxr + cw_ref[j:j + 1, :] * _shift_up(dxc, nxt, sft, row8, tm)
        dxr_ref[...] = dxr.astype(BF16)
        tmp_ref[...] = dxc
        ndxc_ref[...] = tmp_ref[0:8, :]

    rev = lambda s, t: (s * nt + nt - 1 - t, 0)
    tile = pl.BlockSpec((tm, W), rev)
    prev8 = pl.BlockSpec((8, W), lambda s, t: (jnp.maximum((s * nt + nt - 1 - t) * nb8 - 1, 0), 0))
    return _call(
        body, phases=phases, name="rnn_bwd", grid=(n_seq, nt),
        in_specs=[tile] * 9 + [prev8, prev8, _const((CONV_W, W)), _const((8, LANES, LANES)),
                               _const((8, LANES, LANES)), _const((1, W))],
        out_specs=[tile, tile, _const((16, W)), _const((8, LANES, LANES)), _const((8, LANES, LANES))],
        out_shape=[jax.ShapeDtypeStruct((T, W), BF16), jax.ShapeDtypeStruct((T, W), BF16),
                   jax.ShapeDtypeStruct((16, W), F32), jax.ShapeDtypeStruct((8, LANES, LANES), F32),
                   jax.ShapeDtypeStruct((8, LANES, LANES), F32)],
        scratch_shapes=[pltpu.VMEM((8, W), F32), pltpu.VMEM((8, W), F32), pltpu.VMEM((tm, W), F32)],
    )(dya, xr, gr, xc, h, *gates, xr, h, conv_w, wrg2, wig2, lam)


def _head_swap(t, lane):
    w = t.shape[1]
    return jnp.where(lane % HEAD_DIM < HEAD_DIM // 2, pltpu.roll(t, w - HEAD_DIM // 2, 1),
                     pltpu.roll(t, HEAD_DIM // 2, 1))


def _qk_prep(t, gain, cosf, sins, ind, indt, lane):
    ms = _split_dot(t * t, ind) * (1.0 / HEAD_DIM)
    rstd = _split_dot(lax.rsqrt(ms + NORM_EPS), indt)
    tn = (t * rstd) * gain
    return tn * cosf + _head_swap(tn, lane) * sins, rstd


def _qk_prep_bwd(dy, t, rstd, gain, cosf, sins, ind, indt, lane):
    dtn = dy * cosf + _head_swap(dy * sins, lane)
    dgain = jnp.sum(dtn * (t * rstd), axis=0, keepdims=True)
    dn = dtn * gain
    m = _split_dot(_split_dot(dn * t, ind), indt) * (1.0 / HEAD_DIM)
    return rstd * dn - t * (rstd * rstd * rstd * m), dgain


def _attn_mask_t(blk_idx):
    ci = lax.broadcasted_iota(jnp.int32, (2 * WINDOW, WINDOW), 0)
    qi = lax.broadcasted_iota(jnp.int32, (2 * WINDOW, WINDOW), 1)
    diff = WINDOW + qi - ci
    return (diff >= 0) & (diff < WINDOW) & ((ci >= WINDOW) | (blk_idx > 0))


def _stack_heads(t, kvh, lo):
    parts = []
    for i in (2 * kvh, 2 * kvh + 1):
        tp = t[:, i * LANES:(i + 1) * LANES]
        parts += [jnp.where(lo, tp, 0.0), jnp.where(lo, 0.0, tp)]
    return jnp.concatenate(parts, axis=0).astype(BF16)


def _unstack_heads(ts, lo):
    w = WINDOW
    return jnp.where(lo, ts[0:w], ts[w:2 * w]), jnp.where(lo, ts[2 * w:3 * w], ts[3 * w:4 * w])


def _dup_head(t, kvh, lo2):
    m = kvh // 2
    t2 = t[:, m * LANES:(m + 1) * LANES]
    t2r = pltpu.roll(t2, HEAD_DIM, 1)
    return (jnp.where(lo2, t2, t2r) if kvh % 2 == 0 else jnp.where(lo2, t2r, t2)).astype(BF16)


def _fold_head(ts, kvh, lo2):
    tot = ts + pltpu.roll(ts, HEAD_DIM, 1)
    own = lo2 if kvh % 2 == 0 else ~lo2
    return jnp.where(own, tot, 0.0)


KEY_CHUNKS = tuple(slice(i * 64, (i + 1) * 64) for i in range(2 * WINDOW // 64))


def _fold8(x, op):
    return op(x.reshape(x.shape[0] // 8, 8, x.shape[1]), axis=0)


def _softmax_stats(s_ref, b, cols, sink):
    m8 = None
    for c in KEY_CHUNKS:
        t = _fold8(s_ref[b, c, cols], jnp.max)
        m8 = t if m8 is None else jnp.maximum(m8, t)
    mx = jnp.maximum(jnp.max(m8, axis=0, keepdims=True), sink)
    d8 = None
    for c in KEY_CHUNKS:
        t = _fold8(jnp.exp(s_ref[b, c, cols] - mx), jnp.sum)
        d8 = t if d8 is None else d8 + t
    es = jnp.exp(sink - mx)
    inv = 1.0 / (jnp.sum(d8, axis=0, keepdims=True) + es)
    return mx, inv, es * inv


def _attn_fwd(q, k, v, qg, kg, sinks, cosf, sins, ind_q, ind_qt, ind_k, ind_kt, n_seq, S, phases=()):
    T = q.shape[0]
    nblk = S // WINDOW
    W = D_MODEL

    def body(sink_ref, q_ref, k_ref, v_ref, qg_ref, kg_ref, cos_ref, sin_ref, iq_ref, iqt_ref, ik_ref, ikt_ref,
             o_ref, kc_ref, vc_ref, s_ref, p_ref, qs_ref, kd_ref, vd_ref):
        n = pl.program_id(1)

        @pl.when(n == 0)
        def _():
            kc_ref[...] = jnp.zeros_like(kc_ref)
            vc_ref[...] = jnp.zeros_like(vc_ref)

        lane = lax.broadcasted_iota(jnp.int32, (WINDOW, W), 1)
        lo = lane[:, :LANES] < HEAD_DIM
        lo2 = lax.broadcasted_iota(jnp.int32, (2 * WINDOW, LANES), 1) < HEAD_DIM
        cosf, sinv = jnp.tile(cos_ref[...], (1, W // LANES)), jnp.tile(sin_ref[...], (1, W // LANES))
        qr, _ = _qk_prep(q_ref[...], qg_ref[...], cosf, sinv, iq_ref[...], iqt_ref[...], lane)
        kr, _ = _qk_prep(k_ref[...], kg_ref[...], cosf[:, :KV_W], sinv[:, :KV_W], ik_ref[...], ikt_ref[...],
                         lane[:, :KV_W])
        kc_ref[WINDOW:2 * WINDOW, :] = kr
        vc_ref[WINDOW:2 * WINDOW, :] = v_ref[...]
        kc, vc = kc_ref[...], vc_ref[...]
        mask = jnp.tile(_attn_mask_t(n), (1, 4))
        qr = qr * HEAD_DIM ** -0.5
        for kvh in range(N_KV):
            qs_ref[kvh] = _stack_heads(qr, kvh, lo)
            kd_ref[kvh] = _dup_head(kc, kvh, lo2)
            vd_ref[kvh] = _dup_head(vc, kvh, lo2)

        def scores(kvh):
            s_ref[kvh % 2] = jnp.where(mask, _dot_nt(kd_ref[kvh], qs_ref[kvh]), -1e30)

        def softmax(kvh):
            b = kvh % 2
            for r in range(4):
                cols = slice(r * WINDOW, (r + 1) * WINDOW)
                mx, inv, _ = _softmax_stats(s_ref, b, cols, sink_ref[4 * kvh + r])
                for c in KEY_CHUNKS:
                    p_ref[b, c, cols] = (jnp.exp(s_ref[b, c, cols] - mx) * inv).astype(BF16)

        def output(kvh):
            o0, o1 = _unstack_heads(_dot_tn(p_ref[kvh % 2], vd_ref[kvh]), lo)
            o_ref[:, (2 * kvh) * LANES:(2 * kvh + 1) * LANES] = o0.astype(BF16)
            o_ref[:, (2 * kvh + 1) * LANES:(2 * kvh + 2) * LANES] = o1.astype(BF16)

        scores(0)
        for kvh in range(N_KV):
            if kvh + 1 < N_KV:
                scores(kvh + 1)
            softmax(kvh)
            output(kvh)
        kc_ref[0:WINDOW, :] = kr
        vc_ref[0:WINDOW, :] = v_ref[...]

    blk = lambda w: pl.BlockSpec((WINDOW, w), lambda s, n: (s * nblk + n, 0))
    pos = pl.BlockSpec((WINDOW, LANES), lambda s, n: (n, 0))
    outs, extra = _call(
        body, phases=phases, name="attn_fwd", grid=(n_seq, nblk),
        in_specs=[pl.BlockSpec(memory_space=pltpu.SMEM), blk(W), blk(KV_W), blk(KV_W), _const((1, W)),
                  _const((1, KV_W)), pos, pos, _const((W, LANES)), _const((LANES, W)), _const((KV_W, LANES)),
                  _const((LANES, KV_W))],
        out_specs=blk(W), out_shape=jax.ShapeDtypeStruct((T, W), BF16),
        scratch_shapes=[pltpu.VMEM((2 * WINDOW, KV_W), F32), pltpu.VMEM((2 * WINDOW, KV_W), F32),
                        pltpu.VMEM((2, 2 * WINDOW, 4 * WINDOW), F32), pltpu.VMEM((2, 2 * WINDOW, 4 * WINDOW), BF16),
                        pltpu.VMEM((N_KV, 4 * WINDOW, LANES), BF16), pltpu.VMEM((N_KV, 2 * WINDOW, LANES), BF16),
                        pltpu.VMEM((N_KV, 2 * WINDOW, LANES), BF16)],
    )(sinks, q, k, v, qg, kg, cosf, sins, ind_q, ind_qt, ind_k, ind_kt)
    return outs[0], extra


def _attn_bwd(do, q, k, v, qg, kg, sinks, cosf, sins, ind_q, ind_qt, ind_k, ind_kt, n_seq, S, phases=()):
    T = q.shape[0]
    nblk = S // WINDOW
    W = D_MODEL

    def body(sink_ref, do_ref, q_ref, k_ref, v_ref, qg_ref, kg_ref, cos_ref, sin_ref, iq_ref, iqt_ref, ik_ref,
             ikt_ref, dq_ref, dkc_ref, dkp_ref, dvc_ref, dvp_ref, dqg_ref, dsk_ref, kc_ref, vc_ref, dqr_ref,
             dk_ref, dv_ref, s_ref, dp_ref, p_ref, ds_ref, qs_ref, dos_ref, kd_ref, vd_ref):
        s_id, n = pl.program_id(0), pl.program_id(1)

        @pl.when((s_id == 0) & (n == 0))
        def _():
            dqg_ref[...] = jnp.zeros_like(dqg_ref)
            dsk_ref[...] = jnp.zeros_like(dsk_ref)

        @pl.when(n == 0)
        def _():
            kc_ref[...] = jnp.zeros_like(kc_ref)
            vc_ref[...] = jnp.zeros_like(vc_ref)

        lane = lax.broadcasted_iota(jnp.int32, (WINDOW, W), 1)
        lane_k = lane[:, :KV_W]
        lane128 = lane[:, :LANES]
        cosf, sinv = jnp.tile(cos_ref[...], (1, W // LANES)), jnp.tile(sin_ref[...], (1, W // LANES))
        qv = q_ref[...]
        qr, q_rstd = _qk_prep(qv, qg_ref[...], cosf, sinv, iq_ref[...], iqt_ref[...], lane)
        kr, _ = _qk_prep(k_ref[...], kg_ref[...], cosf[:, :KV_W], sinv[:, :KV_W], ik_ref[...], ikt_ref[...], lane_k)
        kc_ref[WINDOW:2 * WINDOW, :] = kr
        vc_ref[WINDOW:2 * WINDOW, :] = v_ref[...]
        kc, vc = kc_ref[...], vc_ref[...]
        dov = do_ref[...]
        mask = jnp.tile(_attn_mask_t(n), (1, 4))
        lo = lane128 < HEAD_DIM
        lo2 = lax.broadcasted_iota(jnp.int32, (2 * WINDOW, LANES), 1) < HEAD_DIM
        scale = HEAD_DIM ** -0.5
        qr = qr * scale
        dk_ref[...] = jnp.zeros_like(dk_ref)
        dv_ref[...] = jnp.zeros_like(dv_ref)
        for kvh in range(N_KV):
            qs_ref[kvh] = _stack_heads(qr, kvh, lo)
            dos_ref[kvh] = _stack_heads(dov, kvh, lo)
            kd_ref[kvh] = _dup_head(kc, kvh, lo2)
            vd_ref[kvh] = _dup_head(vc, kvh, lo2)

        def scores(kvh):
            b = kvh % 2
            s_ref[b] = jnp.where(mask, _dot_nt(kd_ref[kvh], qs_ref[kvh]), -1e30)
            dp_ref[b] = _dot_nt(vd_ref[kvh], dos_ref[kvh])

        def softmax(kvh):
            b = kvh % 2
            for r in range(4):
                cols = slice(r * WINDOW, (r + 1) * WINDOW)
                head = 4 * kvh + r
                mx, inv, ps = _softmax_stats(s_ref, b, cols, sink_ref[head])
                g8 = None
                for c in KEY_CHUNKS:
                    t = _fold8(jnp.exp(s_ref[b, c, cols] - mx) * dp_ref[b, c, cols], jnp.sum)
                    g8 = t if g8 is None else g8 + t
                dd = jnp.sum(g8, axis=0, keepdims=True) * inv
                for c in KEY_CHUNKS:
                    p = jnp.exp(s_ref[b, c, cols] - mx) * inv
                    p_ref[b, c, cols] = p.astype(BF16)
                    ds_ref[b, c, cols] = (p * (dp_ref[b, c, cols] - dd)).astype(BF16)
                dsk_ref[head:head + 1, :] -= ps * dd

        def grads(kvh):
            m, b = kvh // 2, kvh % 2
            dq0, dq1 = _unstack_heads(_dot_tn(ds_ref[b], kd_ref[kvh]) * scale, lo)
            dqr_ref[:, (2 * kvh) * LANES:(2 * kvh + 1) * LANES] = dq0
            dqr_ref[:, (2 * kvh + 1) * LANES:(2 * kvh + 2) * LANES] = dq1
            dk_ref[:, m * LANES:(m + 1) * LANES] += _fold_head(_dot(ds_ref[b], qs_ref[kvh]), kvh, lo2)
            dv_ref[:, m * LANES:(m + 1) * LANES] += _fold_head(_dot(p_ref[b], dos_ref[kvh]), kvh, lo2)

        scores(0)
        for kvh in range(N_KV):
            if kvh + 1 < N_KV:
                scores(kvh + 1)
            softmax(kvh)
            grads(kvh)
        dq, dqg = _qk_prep_bwd(dqr_ref[...], qv, q_rstd, qg_ref[...], cosf, sinv, iq_ref[...], iqt_ref[...], lane)
        dq_ref[...] = dq.astype(BF16)
        dqg_ref[...] += dqg
        dkp_ref[...] = dk_ref[0:WINDOW, :]
        dkc_ref[...] = dk_ref[WINDOW:2 * WINDOW, :]
        dvp_ref[...] = dv_ref[0:WINDOW, :]
        dvc_ref[...] = dv_ref[WINDOW:2 * WINDOW, :]
        kc_ref[0:WINDOW, :] = kr
        vc_ref[0:WINDOW, :] = v_ref[...]

    blk = lambda w: pl.BlockSpec((WINDOW, w), lambda s, n: (s * nblk + n, 0))
    pos = pl.BlockSpec((WINDOW, LANES), lambda s, n: (n, 0))
    kv_out = jax.ShapeDtypeStruct((T, KV_W), F32)
    stage = lambda dt: pltpu.VMEM((2, 2 * WINDOW, 4 * WINDOW), dt)
    return _call(
        body, phases=phases, name="attn_bwd", grid=(n_seq, nblk),
        in_specs=[pl.BlockSpec(memory_space=pltpu.SMEM), blk(W), blk(W), blk(KV_W), blk(KV_W), _const((1, W)),
                  _const((1, KV_W)), pos, pos, _const((W, LANES)), _const((LANES, W)), _const((KV_W, LANES)),
                  _const((LANES, KV_W))],
        out_specs=[blk(W), blk(KV_W), blk(KV_W), blk(KV_W), blk(KV_W), _const((1, W)), _const((N_HEADS, LANES))],
        out_shape=[jax.ShapeDtypeStruct((T, W), BF16), kv_out, kv_out, kv_out, kv_out,
                   jax.ShapeDtypeStruct((1, W), F32), jax.ShapeDtypeStruct((N_HEADS, LANES), F32)],
        scratch_shapes=[pltpu.VMEM((2 * WINDOW, KV_W), F32), pltpu.VMEM((2 * WINDOW, KV_W), F32),
                        pltpu.VMEM((WINDOW, W), F32), pltpu.VMEM((2 * WINDOW, KV_W), F32),
                        pltpu.VMEM((2 * WINDOW, KV_W), F32), stage(F32), stage(F32), stage(BF16), stage(BF16),
                        pltpu.VMEM((N_KV, 4 * WINDOW, LANES), BF16), pltpu.VMEM((N_KV, 4 * WINDOW, LANES), BF16),
                        pltpu.VMEM((N_KV, 2 * WINDOW, LANES), BF16), pltpu.VMEM((N_KV, 2 * WINDOW, LANES), BF16)],
    )(sinks, do, q, k, v, qg, kg, cosf, sins, ind_q, ind_qt, ind_k, ind_kt)


def _kv_bwd(dkc, dkp, dvc, dvp, k, kg, cosf, sins, ind_k, ind_kt, n_seq, S, phases=()):
    T = k.shape[0]
    nblk = S // WINDOW

    def body(dkc_ref, dkp_ref, dvc_ref, dvp_ref, k_ref, kg_ref, cos_ref, sin_ref, ik_ref, ikt_ref,
             dk_ref, dv_ref, dkg_ref):
        s_id, n = pl.program_id(0), pl.program_id(1)

        @pl.when((s_id == 0) & (n == 0))
        def _():
            dkg_ref[...] = jnp.zeros_like(dkg_ref)

        has_next = n < nblk - 1
        lane = lax.broadcasted_iota(jnp.int32, (WINDOW, KV_W), 1)
        dkr = dkc_ref[...] + jnp.where(has_next, dkp_ref[...], 0.0)
        dv_ref[...] = (dvc_ref[...] + jnp.where(has_next, dvp_ref[...], 0.0)).astype(BF16)
        cosf, sinv = jnp.tile(cos_ref[...], (1, KV_W // LANES)), jnp.tile(sin_ref[...], (1, KV_W // LANES))
        kv = k_ref[...]
        _, rstd = _qk_prep(kv, kg_ref[...], cosf, sinv, ik_ref[...], ikt_ref[...], lane)
        dk, dkg = _qk_prep_bwd(dkr, kv, rstd, kg_ref[...], cosf, sinv, ik_ref[...], ikt_ref[...], lane)
        dk_ref[...] = dk.astype(BF16)
        dkg_ref[...] += dkg

    cur = pl.BlockSpec((WINDOW, KV_W), lambda s, n: (s * nblk + n, 0))
    nxt = pl.BlockSpec((WINDOW, KV_W), lambda s, n: (s * nblk + jnp.minimum(n + 1, nblk - 1), 0))
    pos = pl.BlockSpec((WINDOW, LANES), lambda s, n: (n, 0))
    return _call(
        body, phases=phases, name="kv_bwd", grid=(n_seq, nblk),
        in_specs=[cur, nxt, cur, nxt, cur, _const((1, KV_W)), pos, pos, _const((KV_W, LANES)),
                  _const((LANES, KV_W))],
        out_specs=[cur, cur, _const((1, KV_W))],
        out_shape=[jax.ShapeDtypeStruct((T, KV_W), BF16), jax.ShapeDtypeStruct((T, KV_W), BF16),
                   jax.ShapeDtypeStruct((1, KV_W), F32)],
    )(dkc, dkp, dvc, dvp, k, kg, cosf, sins, ind_k, ind_kt)


def _merge_fwd(x, ya, o, ga, gb, w_rnn, w_attn, w_out, tm, phases=()):
    T = x.shape[0]
    W = D_MODEL

    def body(x_ref, ya_ref, o_ref, ga_ref, gb_ref, wr_ref, wa_ref, wo_ref, x1_ref, mg_ref, yao_ref, ybo_ref):
        y_a = _dot(ya_ref[...], wr_ref[...])
        y_b = _dot(o_ref[...], wa_ref[...])
        yao_ref[...] = y_a
        ybo_ref[...] = y_b
        mg = (_sigmoid(ga_ref[...]) * y_a + _sigmoid(gb_ref[...]) * y_b).astype(BF16)
        mg_ref[...] = mg
        x1_ref[...] = x_ref[...] + _dot(mg, wo_ref[...])

    row = pl.BlockSpec((tm, W), lambda i: (i, 0))
    sq = _const((W, W))
    return _call(
        body, phases=phases, name="merge_fwd", grid=(T // tm,),
        in_specs=[row, row, row, row, row, sq, sq, sq], out_specs=[row, row, row, row],
        out_shape=[jax.ShapeDtypeStruct((T, W), F32), jax.ShapeDtypeStruct((T, W), BF16),
                   jax.ShapeDtypeStruct((T, W), F32), jax.ShapeDtypeStruct((T, W), F32)],
    )(x, ya, o, ga, gb, w_rnn, w_attn, w_out)


def _merge_bwd(dx1, ga, gb, y_a, y_b, w_rnn, w_attn, w_out, tm, phases=()):
    T = dx1.shape[0]
    W = D_MODEL

    def body(dx1_ref, ga_ref, gb_ref, ya_ref, yb_ref, wr_ref, wa_ref, wo_ref,
             dga_ref, dgb_ref, dya_ref, dyb_ref, dyain_ref, do_ref):
        dm = _dot_nt(dx1_ref[...].astype(BF16), wo_ref[...])
        sa = _sigmoid(ga_ref[...])
        sb = _sigmoid(gb_ref[...])
        dga_ref[...] = (dm * ya_ref[...] * (sa * (1.0 - sa))).astype(BF16)
        dgb_ref[...] = (dm * yb_ref[...] * (sb * (1.0 - sb))).astype(BF16)
        dya = (dm * sa).astype(BF16)
        dyb = (dm * sb).astype(BF16)
        dya_ref[...] = dya
        dyb_ref[...] = dyb
        dyain_ref[...] = _dot_nt(dya, wr_ref[...])
        do_ref[...] = _dot_nt(dyb, wa_ref[...])

    row = pl.BlockSpec((tm, W), lambda i: (i, 0))
    sq = _const((W, W))
    b16 = jax.ShapeDtypeStruct((T, W), BF16)
    f32 = jax.ShapeDtypeStruct((T, W), F32)
    return _call(
        body, phases=phases, name="merge_bwd", grid=(T // tm,),
        in_specs=[row, row, row, row, row, sq, sq, sq], out_specs=[row] * 6,
        out_shape=[b16, b16, b16, b16, f32, f32],
    )(dx1, ga, gb, y_a, y_b, w_rnn, w_attn, w_out)


def _mlp_fwd(x1, g_mlp, w_up, w_down, tm, phases=()):
    T = x1.shape[0]
    W = D_MODEL

    def body(x_ref, g_ref, wu_ref, wd_ref, x2_ref, hm_ref, u_ref, act_ref):
        xv = x_ref[...]
        hm, _ = _rms_fwd(xv, g_ref[...])
        hmb = hm.astype(BF16)
        hm_ref[...] = hmb
        for j in range(N_CHIPS):
            u = _dot(hmb, wu_ref[j])
            u_ref[:, j * W:(j + 1) * W] = u
            ru = jnp.maximum(u, 0.0)
            act_ref[:, j * W:(j + 1) * W] = (ru * ru).astype(BF16)
        x2_ref[...] = xv + _dot(act_ref[...], wd_ref[...])

    row = lambda w: pl.BlockSpec((tm, w), lambda i: (i, 0))
    return _call(
        body, phases=phases, name="mlp_fwd", grid=(T // tm,),
        in_specs=[row(W), _const((1, W)), _const((N_CHIPS, W, W)), _const((D_FF, W))],
        out_specs=[row(W), row(W), row(D_FF), row(D_FF)],
        out_shape=[jax.ShapeDtypeStruct((T, W), F32), jax.ShapeDtypeStruct((T, W), BF16),
                   jax.ShapeDtypeStruct((T, D_FF), F32), jax.ShapeDtypeStruct((T, D_FF), BF16)],
    )(x1, g_mlp, w_up, w_down)


def _mlp_bwd(dx2, u, x1, g_mlp, w_up, w_down, tm, phases=()):
    T = x1.shape[0]
    W = D_MODEL

    def body(dx2_ref, u_ref, x_ref, g_ref, wu_ref, wd_ref, dx1_ref, du_ref, dg_ref):
        @pl.when(pl.program_id(0) == 0)
        def _():
            dg_ref[...] = jnp.zeros_like(dg_ref)

        dx2 = dx2_ref[...]
        dact = _dot_nt(dx2.astype(BF16), wd_ref[...])
        du_ref[...] = (dact * (2.0 * jnp.maximum(u_ref[...], 0.0))).astype(BF16)
        dhm = jnp.zeros((tm, W), F32)
        for j in range(N_CHIPS):
            dhm = dhm + _dot_nt(du_ref[:, j * W:(j + 1) * W], wu_ref[j])
        xv = x_ref[...]
        g = g_ref[...]
        _, r = _rms_fwd(xv, g)
        dx, dg = _rms_bwd(dhm, xv, r, g)
        dx1_ref[...] = dx2 + dx
        dg_ref[...] += dg

    row = lambda w: pl.BlockSpec((tm, w), lambda i: (i, 0))
    return _call(
        body, phases=phases, name="mlp_bwd", grid=(T // tm,),
        in_specs=[row(W), row(D_FF), row(W), _const((1, W)), _const((N_CHIPS, W, W)), _const((D_FF, W))],
        out_specs=[row(W), row(D_FF), _const((1, W))],
        out_shape=[jax.ShapeDtypeStruct((T, W), F32), jax.ShapeDtypeStruct((T, D_FF), BF16),
                   jax.ShapeDtypeStruct((1, W), F32)],
    )(dx2, u, x1, g_mlp, w_up, w_down)


def _ple_loss(x2, p, target, g_ple, w_gate, w_proj, tm, phases=()):
    T = x2.shape[0]
    W = D_MODEL
    cw = W // N_CHIPS

    def body(x_ref, p_ref, t_ref, g_ref, wg_ref, wp_ref, loss_ref, dx2_ref, pb_ref, de_ref, hp_ref, dtg_ref, dg_ref):
        @pl.when(pl.program_id(0) == 0)
        def _():
            dg_ref[...] = jnp.zeros_like(dg_ref)
            loss_ref[...] = jnp.zeros_like(loss_ref)

        xv = x_ref[...]
        g = g_ref[...]
        pb = p_ref[...].astype(BF16)
        pb_ref[...] = pb
        e = jnp.concatenate([_dot(pb, wp_ref[j]) for j in range(N_CHIPS)], axis=1)
        hp, r = _rms_fwd(xv, g)
        hpb = hp.astype(BF16)
        hp_ref[...] = hpb
        sg = _sigmoid(_dot(hpb, wg_ref[...]))
        diff = (xv + e * sg) - t_ref[...]
        loss_ref[...] += jnp.sum(diff * diff) * (0.5 / W)
        dx3 = diff * (1.0 / W)
        de_ref[...] = (dx3 * sg).astype(BF16)
        dtg = (dx3 * e * (sg * (1.0 - sg))).astype(BF16)
        dtg_ref[...] = dtg
        dx, dg = _rms_bwd(_dot_nt(dtg, wg_ref[...]), xv, r, g)
        dx2_ref[...] = dx3 + dx
        dg_ref[...] += dg

    row = lambda w: pl.BlockSpec((tm, w), lambda i: (i, 0))
    b16 = lambda w: jax.ShapeDtypeStruct((T, w), BF16)
    return _call(
        body, phases=phases, name="ple_loss", grid=(T // tm,),
        in_specs=[row(W), row(PLE_DIM), row(W), _const((1, W)), _const((W, W)), _const((N_CHIPS, PLE_DIM, cw))],
        out_specs=[_const((8, LANES)), row(W), row(PLE_DIM), row(W), row(W), row(W), _const((1, W))],
        out_shape=[jax.ShapeDtypeStruct((8, LANES), F32), jax.ShapeDtypeStruct((T, W), F32), b16(PLE_DIM),
                   b16(W), b16(W), b16(W), jax.ShapeDtypeStruct((1, W), F32)],
    )(x2, p, target, g_ple, w_gate, w_proj)


def _adamw(w, g, m, v, name, tr, phases=()):
    R, C = w.shape
    c1 = 1.0 / (1.0 - ADAM_B1 ** ADAM_STEP)
    c2 = 1.0 / (1.0 - ADAM_B2 ** ADAM_STEP)

    def body(w_ref, g_ref, m_ref, v_ref, go_ref, d_ref, nm_ref, nv_ref):
        gv = g_ref[...]
        go_ref[...] = gv
        nm = ADAM_B1 * m_ref[...] + (1.0 - ADAM_B1) * gv
        nv = ADAM_B2 * v_ref[...] + (1.0 - ADAM_B2) * (gv * gv)
        nm_ref[...] = nm
        nv_ref[...] = nv
        d_ref[...] = (-ADAM_LR) * ((nm * c1) / (jnp.sqrt(nv * c2) + ADAM_EPS) + ADAM_WD * w_ref[...])

    row = pl.BlockSpec((tr, C), lambda i: (i, 0))
    sds = jax.ShapeDtypeStruct((R, C), F32)
    return _call(
        body, phases=phases, name=name, grid=(R // tr,), in_specs=[row] * 4, out_specs=[row] * 4,
        out_shape=[sds] * 4,
    )(w, g, m, v)


def _indicator(width):
    ind = np.zeros((width, LANES), np.float32)
    ind[np.arange(width), np.arange(width) // HEAD_DIM] = 1.0
    return jnp.asarray(ind, BF16), jnp.asarray(ind.T, BF16)


def _rope_tables(S):
    inv = ROPE_THETA ** (-jnp.arange(0, HEAD_DIM, 2, dtype=F32) / HEAD_DIM)
    ang = jnp.arange(S, dtype=F32)[:, None] * inv[None, :]
    cos, sin = jnp.cos(ang), jnp.sin(ang)
    cosf = jnp.tile(jnp.concatenate([cos, cos], axis=1), (1, LANES // HEAD_DIM))
    sins = jnp.tile(jnp.concatenate([-sin, sin], axis=1), (1, LANES // HEAD_DIM))
    return cosf, sins


def _pair_blockdiag(w):
    w4 = w.reshape(8, 2, HEAD_DIM, HEAD_DIM)
    eye = jnp.eye(2, dtype=w.dtype)
    return jnp.einsum("bpij,pq->bpiqj", w4, eye).reshape(8, LANES, LANES)


def _pair_blockdiag_extract(g):
    g5 = g.reshape(8, 2, HEAD_DIM, 2, HEAD_DIM)
    return jnp.stack([g5[:, 0, :, 0, :], g5[:, 1, :, 1, :]], axis=1).reshape(16, HEAD_DIM, HEAD_DIM)


def _pair_sum(parts, sibs, name):
    n = len(parts)
    dims = [(p.shape[1] // 2, p.shape[2]) for p in parts]

    def body(*refs):
        p_r, s_r, send_r, own_r, mine_r, sem = (refs[0:n], refs[n:2 * n], refs[2 * n:3 * n], refs[3 * n:4 * n],
                                                refs[4 * n:5 * n], refs[5 * n])
        x, y, c, chips = _mesh_pos()
        me = 2 * x + y
        loads = []
        for i, (R, _) in enumerate(dims):
            mine, _ = _half_rows(c, R)
            cp = pltpu.make_async_copy(p_r[i].at[:, mine, :], mine_r[i], sem.at[i])
            cp.start()
            loads.append(cp)
        for i in range(n):
            loads[i].wait()
            for j, (cx, cy) in enumerate(chips):
                k = 2 * cx + cy
                send_r[i][j] = (mine_r[i][k] + s_r[i][k]).astype(BF16)
            own_r[i][...] = mine_r[i][me] + s_r[i][me]

    vm = pl.BlockSpec(memory_space=pltpu.VMEM)
    out = pl.pallas_call(
        body, name=name, in_specs=[pl.BlockSpec(memory_space=pl.ANY)] * n + [vm] * n, out_specs=[vm] * (2 * n),
        out_shape=[jax.ShapeDtypeStruct((3, R, C), BF16) for R, C in dims]
        + [jax.ShapeDtypeStruct((R, C), F32) for R, C in dims],
        scratch_shapes=[pltpu.VMEM((N_CHIPS, R, C), F32) for R, C in dims] + [pltpu.SemaphoreType.DMA((n,))],
        compiler_params=pltpu.CompilerParams(vmem_limit_bytes=VMEM_LIMIT),
    )(*parts, *sibs)
    return out[:n], out[n:]


def _chip_sum(owns, recvs, name):
    n = len(owns)
    dims = [o.shape for o in owns]

    def body(*refs):
        own_r, recv_r, red_r, stage_r, sem = refs[0:n], refs[n:2 * n], refs[2 * n:3 * n], refs[3 * n:4 * n], refs[4 * n]
        x, y, c, _ = _mesh_pos()
        me = 2 * x + y
        stores = []
        for i, (R, _) in enumerate(dims):
            for k_me in range(N_CHIPS):

                @pl.when(me == k_me)
                def _():
                    acc = None
                    for k in range(N_CHIPS):
                        slot = ((k // 2) ^ (k_me // 2)) + 2 * ((k % 2) ^ (k_me % 2)) - 1
                        term = own_r[i][...] if k == k_me else recv_r[i][slot].astype(F32)
                        acc = term if acc is None else acc + term
                    stage_r[i][...] = acc

            mine, _ = _half_rows(c, R)
            cp = pltpu.make_async_copy(stage_r[i], red_r[i].at[mine, :], sem.at[i])
            cp.start()
            stores.append(cp)
        for cp in stores:
            cp.wait()

    vm = pl.BlockSpec(memory_space=pltpu.VMEM)
    return pl.pallas_call(
        body, name=name, in_specs=[vm] * (2 * n), out_specs=[pl.BlockSpec(memory_space=pl.ANY)] * n,
        out_shape=[jax.ShapeDtypeStruct((2 * R, C), F32) for R, C in dims],
        scratch_shapes=[pltpu.VMEM((R, C), F32) for R, C in dims] + [pltpu.SemaphoreType.DMA((n,))],
        compiler_params=pltpu.CompilerParams(vmem_limit_bytes=VMEM_LIMIT),
    )(*owns, *recvs)


def _gather_bf16(shard, name):
    R2, C = shard.shape
    R = R2 // 2
    H = R // 2

    def body(s_ref, o_ref, send_sems, recv_sems):
        x, y, c, _ = _mesh_pos()
        me, chip_x, chip_y, chip_d = 2 * x + y, 2 * (1 - x) + y, 2 * x + (1 - y), 2 * (1 - x) + (1 - y)
        to_x, to_y, me_dev, sibling = (1 - x, y, c), (x, 1 - y, c), (x, y, c), (x, y, 1 - c)

        def rows(core, off, n):
            return pl.ds(pl.multiple_of(core * R + off, H), n)

        def copy(k, chip, rws, to):
            blk = o_ref.at[chip, rws]
            return _remote(blk, blk, (send_sems.at[k], recv_sems.at[k]), to)

        piece, half_a, half_b = rows(c, 0, R), rows(c, 0, H), rows(c, H, H)
        o_ref[me] = s_ref[...].astype(BF16)
        sends = [copy(0, me, piece, to_x), copy(1, me, piece, to_y)]
        for cp in sends:
            cp.start()
        arrivals = [(0, chip_x, piece, (2, half_a, to_y)), (1, chip_y, piece, (3, half_b, to_x)),
                    (2, chip_d, half_a, None), (3, chip_d, half_b, None)]
        for k, chip, rws, onward in arrivals:
            copy(k, chip, rws, me_dev).wait_recv()
            if onward is not None:
                sends.append(copy(onward[0], chip, onward[1], onward[2]))
                sends[-1].start()
            sends.append(copy(4 + k, chip, rws, sibling))
            sends[-1].start()
        for k, chip, rws in [(4, chip_x, rows(1 - c, 0, R)), (5, chip_y, rows(1 - c, 0, R)),
                             (6, chip_d, rows(1 - c, 0, H)), (7, chip_d, rows(1 - c, H, H))]:
            copy(k, chip, rws, me_dev).wait_recv()
        for cp in sends:
            cp.wait_send()

    return pl.pallas_call(
        body, name=name, out_shape=jax.ShapeDtypeStruct((N_CHIPS, R2, C), BF16),
        in_specs=[pl.BlockSpec(memory_space=pltpu.VMEM)], out_specs=pl.BlockSpec(memory_space=pltpu.VMEM),
        scratch_shapes=[pltpu.SemaphoreType.DMA((8,)), pltpu.SemaphoreType.DMA((8,))],
        compiler_params=pltpu.CompilerParams(vmem_limit_bytes=VMEM_LIMIT),
    )(shard)


def _pair_exchange_sum(partial, name):
    _, R2, C = partial.shape
    R = R2 // 2

    def body(p_ref, send_ref, own_ref, mine_ref, sib_ref, loc_sems, send_sems, recv_sems):
        x, y, c, chips = _mesh_pos()
        me = 2 * x + y
        mine, theirs = _half_rows(c, R)
        order = [2 * cx + cy for cx, cy in chips] + [me]
        locs, pairs = [], []
        for i, k in enumerate(order):
            loc = pltpu.make_async_copy(p_ref.at[k, mine, :], mine_ref.at[i], loc_sems.at[i])
            pair = _remote(p_ref.at[k, theirs, :], sib_ref.at[i], (send_sems.at[i], recv_sems.at[i]), (x, y, 1 - c))
            loc.start()
            pair.start()
            locs.append(loc)
            pairs.append(pair)
        for i in range(N_CHIPS):
            locs[i].wait()
            pairs[i].wait_recv()
            total = mine_ref[i] + sib_ref[i]
            if i < 3:
                send_ref[i] = total.astype(BF16)
            else:
                own_ref[...] = total
        for pair in pairs:
            pair.wait_send()

    vm = pl.BlockSpec(memory_space=pltpu.VMEM)
    return pl.pallas_call(
        body, name=name, in_specs=[pl.BlockSpec(memory_space=pl.ANY)], out_specs=[vm, vm],
        out_shape=[jax.ShapeDtypeStruct((3, R, C), BF16), jax.ShapeDtypeStruct((R, C), F32)],
        scratch_shapes=[pltpu.VMEM((N_CHIPS, R, C), F32), pltpu.VMEM((N_CHIPS, R, C), F32),
                        pltpu.SemaphoreType.DMA((N_CHIPS,)), pltpu.SemaphoreType.DMA((N_CHIPS,)),
                        pltpu.SemaphoreType.DMA((N_CHIPS,))],
        compiler_params=pltpu.CompilerParams(vmem_limit_bytes=VMEM_LIMIT),
    )(partial)


def _allreduce_small(buf, name):
    rows, width = buf.shape
    h = rows // 2

    def body(b_ref, o_ref, sib_ref, pair_ref, in_ref, pair_sems, send_sems, recv_sems, fin_sems):
        x, y, c, chips = _mesh_pos()
        me = 2 * x + y
        mine, theirs = _half_rows(c, h)
        sibling = (x, y, 1 - c)
        pair = _remote(b_ref.at[theirs], sib_ref, (pair_sems.at[0], pair_sems.at[1]), sibling)
        pair.start()
        pair.wait()
        pair_ref[...] = b_ref[mine, :] + sib_ref[...]
        sends = []
        for j, (cx, cy) in enumerate(chips):
            cp = _remote(pair_ref, in_ref.at[j], (send_sems.at[j], recv_sems.at[j]), (cx, cy, c))
            cp.start()
            sends.append(cp)
        for cp in sends:
            cp.wait_recv()
        acc = None
        for k in range(N_CHIPS):
            term = jnp.where(me == k, pair_ref[...], in_ref[_peer_slot(k, x, y)])
            acc = term if acc is None else acc + term
        o_ref[mine, :] = acc
        fin = _remote(o_ref.at[mine], o_ref.at[mine], (fin_sems.at[0], fin_sems.at[1]), sibling)
        fin.start()
        fin.wait_send()
        _remote(o_ref.at[theirs], o_ref.at[theirs], (fin_sems.at[0], fin_sems.at[1]), sibling).wait_recv()
        for cp in sends:
            cp.wait_send()

    return pl.pallas_call(
        body, name=name, out_shape=jax.ShapeDtypeStruct((rows, width), F32),
        in_specs=[pl.BlockSpec(memory_space=pltpu.VMEM)], out_specs=pl.BlockSpec(memory_space=pltpu.VMEM),
        scratch_shapes=[pltpu.VMEM((h, width), F32), pltpu.VMEM((h, width), F32), pltpu.VMEM((3, h, width), F32),
                        pltpu.SemaphoreType.DMA((2,)), pltpu.SemaphoreType.DMA((3,)), pltpu.SemaphoreType.DMA((3,)),
                        pltpu.SemaphoreType.DMA((2,))],
        compiler_params=pltpu.CompilerParams(vmem_limit_bytes=VMEM_LIMIT),
    )(buf)


def _adamw_small(ws, gs, ms, vs):
    n = len(ws)
    c1 = 1.0 / (1.0 - ADAM_B1 ** ADAM_STEP)
    c2 = 1.0 / (1.0 - ADAM_B2 ** ADAM_STEP)

    def body(*refs):
        w_r, g_r, m_r, v_r = refs[0:n], refs[n:2 * n], refs[2 * n:3 * n], refs[3 * n:4 * n]
        d_r, nm_r, nv_r = refs[4 * n:5 * n], refs[5 * n:6 * n], refs[6 * n:7 * n]
        for i in range(n):
            gv = g_r[i][...]
            nm = ADAM_B1 * m_r[i][...] + (1.0 - ADAM_B1) * gv
            nv = ADAM_B2 * v_r[i][...] + (1.0 - ADAM_B2) * (gv * gv)
            nm_r[i][...] = nm
            nv_r[i][...] = nv
            d_r[i][...] = (-ADAM_LR) * ((nm * c1) / (jnp.sqrt(nv * c2) + ADAM_EPS) + ADAM_WD * w_r[i][...])

    vm = pl.BlockSpec(memory_space=pltpu.VMEM)
    sds = [jax.ShapeDtypeStruct(w.shape, F32) for w in ws]
    out = pl.pallas_call(body, name="adamw_small", in_specs=[vm] * (4 * n), out_specs=[vm] * (3 * n),
                         out_shape=sds * 3)(*ws, *gs, *ms, *vs)
    return out[0:n], out[n:2 * n], out[2 * n:3 * n]


_BIG = ("w_in", "w_rnn_proj", "w_attn_proj", "w_out", "w_up", "w_down", "w_ple_gate", "w_ple_proj")
_SMALL = ("g_mix", "conv_w", "conv_b", "w_rg", "b_rg", "w_ig", "b_ig", "lru_lambda", "q_gain", "k_gain", "sinks",
          "g_mlp", "g_ple")
_WEIGHTS = ("g_mix", "w_in", "conv_w", "conv_b", "w_rg", "b_rg", "w_ig", "b_ig", "lru_lambda", "w_rnn_proj",
            "q_gain", "k_gain", "sinks", "w_attn_proj", "w_out", "g_mlp", "w_up", "w_down", "g_ple", "w_ple_gate",
            "w_ple_proj")


def _pad_row(v):
    v = v.reshape(1, -1)
    return jnp.pad(v, ((0, 0), (0, D_MODEL - v.shape[1])))


def kernel(x, p, g_mix, w_in, conv_w, conv_b, w_rg, b_rg, w_ig, b_ig, lru_lambda, w_rnn_proj, q_gain, k_gain, sinks, w_attn_proj, w_out, g_mlp, w_up, w_down, g_ple, w_ple_gate, w_ple_proj, loss_target, m_g_mix, m_w_in, m_conv_w, m_conv_b, m_w_rg, m_b_rg, m_w_ig, m_b_ig, m_lru_lambda, m_w_rnn_proj, m_q_gain, m_k_gain, m_sinks, m_w_attn_proj, m_w_out, m_g_mlp, m_w_up, m_w_down, m_g_ple, m_w_ple_gate, m_w_ple_proj, v_g_mix, v_w_in, v_conv_w, v_conv_b, v_w_rg, v_b_rg, v_w_ig, v_b_ig, v_lru_lambda, v_w_rnn_proj, v_q_gain, v_k_gain, v_sinks, v_w_attn_proj, v_w_out, v_g_mlp, v_w_up, v_w_down, v_g_ple, v_w_ple_gate, v_w_ple_proj):
    w = dict(g_mix=g_mix, w_in=w_in, conv_w=conv_w, conv_b=conv_b, w_rg=w_rg, b_rg=b_rg, w_ig=w_ig, b_ig=b_ig,
             lru_lambda=lru_lambda, w_rnn_proj=w_rnn_proj, q_gain=q_gain, k_gain=k_gain, sinks=sinks,
             w_attn_proj=w_attn_proj, w_out=w_out, g_mlp=g_mlp, w_up=w_up, w_down=w_down, g_ple=g_ple,
             w_ple_gate=w_ple_gate, w_ple_proj=w_ple_proj)
    m = dict(g_mix=m_g_mix, w_in=m_w_in, conv_w=m_conv_w, conv_b=m_conv_b, w_rg=m_w_rg, b_rg=m_b_rg, w_ig=m_w_ig,
             b_ig=m_b_ig, lru_lambda=m_lru_lambda, w_rnn_proj=m_w_rnn_proj, q_gain=m_q_gain, k_gain=m_k_gain,
             sinks=m_sinks, w_attn_proj=m_w_attn_proj, w_out=m_w_out, g_mlp=m_g_mlp, w_up=m_w_up, w_down=m_w_down,
             g_ple=m_g_ple, w_ple_gate=m_w_ple_gate, w_ple_proj=m_w_ple_proj)
    v = dict(g_mix=v_g_mix, w_in=v_w_in, conv_w=v_conv_w, conv_b=v_conv_b, w_rg=v_w_rg, b_rg=v_b_rg, w_ig=v_w_ig,
             b_ig=v_b_ig, lru_lambda=v_lru_lambda, w_rnn_proj=v_w_rnn_proj, q_gain=v_q_gain, k_gain=v_k_gain,
             sinks=v_sinks, w_attn_proj=v_w_attn_proj, w_out=v_w_out, g_mlp=v_g_mlp, w_up=v_w_up, w_down=v_w_down,
             g_ple=v_g_ple, w_ple_gate=v_w_ple_gate, w_ple_proj=v_w_ple_proj)
    n_seq, S, _ = x.shape
    T = n_seq * S
    chip = 2 * lax.axis_index("x") + lax.axis_index("y")

    tm, tm_rnn = TM, TM_RNN
    xf, pf, tf = x.reshape(T, D_MODEL), p.reshape(T, PLE_DIM), loss_target.reshape(T, D_MODEL)
    first = lambda outs: [o[0] for o in outs]

    w_in_g = _gather_bf16(w["w_in"][0], "gather_w_in")
    wb = {name: w[name][0].astype(BF16) for name in _BIG if name != "w_in"}
    grp_mix, grp_mlp, grp_ple = ("w_rnn_proj", "w_attn_proj", "w_out"), ("w_up", "w_down"), ("w_ple_gate", "w_ple_proj")

    wb["conv_w"] = jnp.pad(conv_w[0], ((0, 16 - CONV_W), (0, 0)))

    cosf, sins = _rope_tables(S)
    ind_q, ind_qt = _indicator(D_MODEL)
    ind_k, ind_kt = _indicator(KV_W)
    wrg2 = _pair_blockdiag(w_rg[0]).astype(BF16)
    wig2 = _pair_blockdiag(w_ig[0]).astype(BF16)
    qg = jnp.tile(q_gain, (1, N_HEADS))
    kg = jnp.tile(k_gain, (1, N_KV))
    sk = sinks.reshape(N_HEADS)
    attn_c = (qg, kg, sk, cosf, sins, ind_q, ind_qt, ind_k, ind_kt, n_seq, S)

    (h0, xr, gr, zq, zk, zv, ga, gb), ph = _inproj_fwd(xf, g_mix, w_in_g, tm,
                                                     phases=[_ph_gather_send(wb[n]) for n in grp_mix + ("conv_w",)])
    g_small = first(ph)
    o, ph = _attn_fwd(zq, zk, zv, *attn_c,
                      phases=[_ph_gather_pass(g) for g in g_small]
                      + [_ph_gather_send(wb[n]) for n in ("w_up",) + grp_ple])
    g_small, (wu, wpg, wpp) = first(ph[:4]), first(ph[4:])
    cw_full = g_small[3][:, :CONV_W, :].transpose(1, 0, 2).reshape(CONV_W, D_MODEL)
    rnn_w = (cw_full, conv_b, wrg2, b_rg, wig2, b_ig, lru_lambda)
    (xc, h, *gates, ya), ph = _rnn_fwd(xr, gr, *rnn_w, n_seq, S, tm_rnn,
                               phases=[_ph_gather_pass(g) for g in (wu, wpg, wpp)]
                               + [_ph_gather_send(wb["w_down"])])
    (wu, wpg, wpp), wd = first(ph[:3]), ph[3][0]
    wr, wa, wo = (g.reshape(D_MODEL, D_MODEL) for g in g_small[:3])
    wpg = wpg.reshape(D_MODEL, D_MODEL)
    (x1, merged, y_a, y_b), ph = _merge_fwd(xf, ya, o, ga, gb, wr, wa, wo, tm, phases=[_ph_gather_pass(wd)])
    wd = ph[0][0].reshape(D_FF, D_MODEL)
    (x2, hm, u, act), _ = _mlp_fwd(x1, g_mlp, wu, wd, tm // 2)
    (loss_t, dx2, pb, de, hp, dtg, dg_ple), _ = _ple_loss(x2, pf, tf, g_ple, wpg, wpp, tm)

    chipmajor = lambda g: g.reshape(N_CHIPS, g.shape[-2] // N_CHIPS, g.shape[-1]) if g.ndim == 2 else g
    tmw = min(2 * tm, T)
    dw_pp = _wgrad(pb, de, "wgrad_ple_proj", False, D_MODEL, tmw)[0]
    part_ple = [chipmajor(_wgrad(hp, dtg, "wgrad_ple_gate", False, D_MODEL, tmw)[0]),
                dw_pp.reshape(PLE_DIM, N_CHIPS, D_MODEL // N_CHIPS).transpose(1, 0, 2)]
    (dx1, du, dg_mlp), ph = _mlp_bwd(dx2, u, x1, g_mlp, wu, wd, tm // 2, phases=[_ph_pair_send(g) for g in part_ple])
    send_ple, own_ple = _pair_sum(part_ple, first(ph), "pair_sum_ple")
    dw_down, ph = _wgrad(act, dx2, "wgrad_down", False, D_MODEL // 2, tmw, phases=[_ph_chip_send(s) for s in send_ple])
    red_ple = _chip_sum(own_ple, first(ph), "chip_sum_ple")
    part_mlp = [_wgrad(hm, du, "wgrad_up", True, D_MODEL, tmw)[0], chipmajor(dw_down)]
    (dga, dgb, dya, dyb, dyain, do), _ = _merge_bwd(dx1, ga, gb, y_a, y_b, wr, wa, wo, tm)
    dw_rnn, ph_up = _wgrad(ya, dya, "wgrad_rnn_proj", False, D_MODEL, tmw, phases=[_ph_pair_send(part_mlp[0])])
    dw_attn, ph_down = _wgrad(o, dyb, "wgrad_attn_proj", False, D_MODEL, tmw, phases=[_ph_pair_send(part_mlp[1])])
    dw_out, ph = _wgrad(merged, dx1, "wgrad_out", False, D_MODEL, tmw, phases=[_ph_half_swap(r) for r in red_ple])
    red_ple = first(ph)
    send_mlp, own_mlp = _pair_sum(part_mlp, [ph_up[0][0], ph_down[0][0]], "pair_sum_mlp")
    part_mix = [chipmajor(dw_rnn), chipmajor(dw_attn), chipmajor(dw_out)]
    (dxr, dgr, vec, dwrg2, dwig2), ph = _rnn_bwd(
        dyain, xr, gr, xc, h, gates, cw_full, wrg2, wig2, lru_lambda, n_seq, S, tm_rnn,
        phases=[_ph_chip_send(s) for s in send_mlp] + [_ph_pair_send(g) for g in part_mix])
    red_mlp = _chip_sum(own_mlp, first(ph[:2]), "chip_sum_mlp")
    send_mix, own_mix = _pair_sum(part_mix, first(ph[2:]), "pair_sum_mix")
    (dq, dkc, dkp, dvc, dvp, dqg, dsk), ph = _attn_bwd(
        do, zq, zk, zv, *attn_c, phases=[_ph_half_swap(r) for r in red_mlp] + [_ph_chip_send(s) for s in send_mix])
    red_mlp = first(ph[:2])
    red_mix = _chip_sum(own_mix, first(ph[2:]), "chip_sum_mix")
    (dk, dv, dkg), _ = _kv_bwd(dkc, dkp, dvc, dvp, zk, kg, cosf, sins, ind_k, ind_kt, n_seq, S)
    dz_parts = [dxr, dgr, dq, dk, dv, dga, dgb]
    send_in, own_in = _pair_exchange_sum(_wgrad_in(h0, dz_parts, tm), "pair_sum_in")
    (grad_x, dg_mix), ph = _inproj_bwd(dz_parts, w_in_g, xf, g_mix, dx1, tm,
                                       phases=[_ph_half_swap(r) for r in red_mix] + [_ph_chip_send(send_in)])
    red_mix = first(ph[:3])
    red_in = _chip_sum([own_in], first(ph[3:]), "chip_sum_in")
    reduced = dict(zip(grp_ple + grp_mlp + grp_mix, red_ple + red_mlp + red_mix))
    grads = {
        "g_mix": dg_mix[0], "g_mlp": dg_mlp[0], "g_ple": dg_ple[0],
        "conv_w": vec[0:CONV_W], "conv_b": vec[4], "b_rg": vec[5], "b_ig": vec[6], "lru_lambda": vec[7],
        "w_rg": _pair_blockdiag_extract(dwrg2), "w_ig": _pair_blockdiag_extract(dwig2),
        "q_gain": dqg.reshape(N_HEADS, HEAD_DIM).sum(0), "k_gain": dkg.reshape(N_KV, HEAD_DIM).sum(0),
        "sinks": dsk.sum(1),
    }

    rows = [grads["conv_w"], _pad_row(grads["conv_b"]), _pad_row(grads["b_rg"]), _pad_row(grads["b_ig"]),
            _pad_row(grads["lru_lambda"]), _pad_row(grads["g_mix"]), _pad_row(grads["g_mlp"]),
            _pad_row(grads["g_ple"]), _pad_row(grads["q_gain"]), _pad_row(grads["k_gain"]), _pad_row(grads["sinks"]),
            _pad_row(loss_t[0:1, 0:1]), jnp.zeros((1, D_MODEL), F32)]
    vecs = jnp.concatenate(rows, axis=0)
    packed = jnp.concatenate([vecs.reshape(-1, LANES), grads["w_rg"].reshape(-1, LANES),
                              grads["w_ig"].reshape(-1, LANES)], axis=0)
    red = _allreduce_small(packed, "allreduce_small")
    nv = vecs.size // LANES
    rvec = red[0:nv].reshape(16, D_MODEL)
    loss = rvec[14, 0]
    nw = grads["w_rg"].size // LANES
    sg = {
        "conv_w": lax.dynamic_slice(rvec[0:CONV_W], (0, chip * (D_MODEL // N_CHIPS)), (CONV_W, D_MODEL // N_CHIPS)),
        "conv_b": rvec[4], "b_rg": rvec[5], "b_ig": rvec[6], "lru_lambda": rvec[7], "g_mix": rvec[8],
        "g_mlp": rvec[9], "g_ple": rvec[10], "q_gain": rvec[11, :HEAD_DIM], "k_gain": rvec[12, :HEAD_DIM],
        "sinks": rvec[13, :N_HEADS], "w_rg": red[nv:nv + nw], "w_ig": red[nv + nw:nv + 2 * nw],
    }
    sg = {k: sg[k].reshape(w[k].shape) for k in _SMALL}
    d_s, m_s, v_s = _adamw_small([w[k] for k in _SMALL], [sg[k] for k in _SMALL], [m[k] for k in _SMALL],
                                 [v[k] for k in _SMALL])
    grad, delta, new_m, new_v = dict(sg), dict(zip(_SMALL, d_s)), dict(zip(_SMALL, m_s)), dict(zip(_SMALL, v_s))

    for name in ("w_ple_proj", "w_up", "w_down", "w_rnn_proj", "w_attn_proj", "w_out", "w_ple_gate", "w_in"):
        shape = w[name].shape
        outs, ph = _adamw(w[name][0], reduced[name], m[name][0], v[name][0], "adamw_" + name, ADAMW_ROWS,
                          phases=[_ph_half_swap(r) for r in red_in] if name == "w_ple_proj" else ())
        if name == "w_ple_proj":
            reduced["w_in"] = ph[0][0]
        grad[name], delta[name], new_m[name], new_v[name] = (a.reshape(shape) for a in outs)

    return (loss, grad_x.reshape(x.shape), *[grad[k] for k in _WEIGHTS], *[delta[k] for k in _WEIGHTS],
            *[new_m[k] for k in _WEIGHTS], *[new_v[k] for k in _WEIGHTS])
```

```python
import functools
import math

import numpy as np
import jax
import jax.numpy as jnp
from jax import lax
from jax.experimental import pallas as pl
from jax.experimental.pallas import tpu as pltpu

F32 = jnp.float32
BF16 = jnp.bfloat16

D_MODEL = 1024
N_HEADS = 16
N_KV = 4
HEAD_DIM = 64
KV_W = N_KV * HEAD_DIM
D_FF = 4096
PLE_DIM = 256
WINDOW = 128
CONV_W = 4
LRU_C = 8.0
NORM_EPS = 1e-6
ROPE_THETA = 10000.0
N_CHIPS = 4
IN_TOTAL = 5632
IN_BLK = IN_TOTAL // N_CHIPS
IN_SEGS = (0, 1024, 2048, 3072, 3328, 3584, 4608, 5632)

ADAM_LR = 0.001
ADAM_B1 = 0.9
ADAM_B2 = 0.999
ADAM_EPS = 1e-08
ADAM_WD = 0.01
ADAM_STEP = 10

LANES = 128
V7X_VMEM_BYTES = 64 * 1024 * 1024
VMEM_LIMIT = V7X_VMEM_BYTES - 8 * 1024 * 1024
MESH_ID = pl.DeviceIdType.MESH
TM, TM_RNN, ADAMW_ROWS = 512, 256, 256


def _dot(a, b):
    return jnp.dot(a, b, preferred_element_type=F32)


def _dot_nt(a, b):
    return lax.dot_general(a, b, (((1,), (1,)), ((), ())), preferred_element_type=F32)


def _dot_tn(a, b):
    return lax.dot_general(a, b, (((0,), (0,)), ((), ())), preferred_element_type=F32)


def _split_dot(x, ind):
    hi = x.astype(BF16)
    lo = (x - hi.astype(F32)).astype(BF16)
    return _dot(hi, ind) + _dot(lo, ind)


def _sigmoid(x):
    return 1.0 / (1.0 + jnp.exp(-x))


_GELU_C = math.sqrt(2.0 / math.pi)


def _gelu_and_grad(g):
    inner = _GELU_C * (g + 0.044715 * g * g * g)
    t = jnp.tanh(inner)
    gelu = 0.5 * g * (1.0 + t)
    dgelu = 0.5 * (1.0 + t) + 0.5 * g * (1.0 - t * t) * _GELU_C * (1.0 + 3.0 * 0.044715 * g * g)
    return gelu, dgelu


def _const(shape):
    nd = len(shape)
    return pl.BlockSpec(shape, lambda *_: (0,) * nd)


def _params(n_grid, vmem=VMEM_LIMIT):
    return pltpu.CompilerParams(dimension_semantics=("arbitrary",) * n_grid, vmem_limit_bytes=vmem)


def _rms_fwd(x, g):
    r = lax.rsqrt(jnp.mean(x * x, axis=-1, keepdims=True) + NORM_EPS)
    return (x * r) * g, r


def _rms_bwd(dy, x, r, g):
    dn = dy * g
    dx = r * dn - x * (r * r * r * jnp.mean(dn * x, axis=-1, keepdims=True))
    dg = jnp.sum(dy * (x * r), axis=0, keepdims=True)
    return dx, dg


def _seg_pieces(blk_lo, blk_hi):
    out = []
    for s in range(7):
        lo, hi = max(blk_lo, IN_SEGS[s]), min(blk_hi, IN_SEGS[s + 1])
        if lo < hi:
            out.append((s, lo - IN_SEGS[s], hi - IN_SEGS[s], lo - blk_lo))
    return out


def _mesh_pos():
    x, y, c = lax.axis_index("x"), lax.axis_index("y"), lax.axis_index("c")
    other_chips = [(1 - x, y), (x, 1 - y), (1 - x, 1 - y)]
    return x, y, c, other_chips


def _peer_slot(k, x, y):
    dx = jnp.bitwise_xor(k // 2, x)
    dy = jnp.bitwise_xor(k % 2, y)
    return jnp.maximum(dx + 2 * dy - 1, 0)


def _half_rows(c, R):
    return pl.ds(pl.multiple_of(c * R, R), R), pl.ds(pl.multiple_of((1 - c) * R, R), R)


def _remote(src, dst, sems, to):
    return pltpu.make_async_remote_copy(src_ref=src, dst_ref=dst, send_sem=sems[0], recv_sem=sems[1],
                                        device_id=to, device_id_type=MESH_ID)


class _Phase:
    def __init__(self, ins, inout, outs, n_remote, n_local, build):
        self.ins, self.inout, self.outs = list(ins), list(inout), list(outs)
        self.n_remote, self.n_local, self.build = n_remote, n_local, build


def _ph_gather_send(wb):
    R2, C = wb.shape
    R = R2 // 2

    def build(ins, outs, rsem, lsem):
        (w_ref,), (g_ref,) = ins, outs
        x, y, c, chips = _mesh_pos()
        me = 2 * x + y
        mine, _ = _half_rows(c, R)
        loc = [pltpu.make_async_copy(w_ref, g_ref.at[me], lsem(0))]
        outg = [_remote(w_ref.at[mine], g_ref.at[me, mine], rsem(j), (cx, cy, c)) for j, (cx, cy) in enumerate(chips)]
        inc = [functools.partial(_remote, w_ref.at[mine], g_ref.at[2 * cx + cy, mine], rsem(j), (x, y, c))
               for j, (cx, cy) in enumerate(chips)]
        return loc, outg, inc

    return _Phase([wb], [], [jax.ShapeDtypeStruct((N_CHIPS, R2, C), wb.dtype)], 3, 1, build)


def _ph_gather_pass(gath):
    _, R2, C = gath.shape
    R = R2 // 2

    def build(ins, outs, rsem, lsem):
        (g_ref,) = outs
        x, y, c, chips = _mesh_pos()
        mine, theirs = _half_rows(c, R)
        outg, inc = [], []
        for j, (cx, cy) in enumerate(chips):
            blk = g_ref.at[2 * cx + cy, mine]
            outg.append(_remote(blk, blk, rsem(j), (x, y, 1 - c)))
            got = g_ref.at[2 * cx + cy, theirs]
            inc.append(functools.partial(_remote, got, got, rsem(j), (x, y, c)))
        return [], outg, inc

    return _Phase([], [gath], [], 3, 0, build)


def _ph_pair_send(partial):
    _, R2, C = partial.shape
    R = R2 // 2

    def build(ins, outs, rsem, lsem):
        (p_ref,), (s_ref,) = ins, outs
        x, y, c, _ = _mesh_pos()
        _, theirs = _half_rows(c, R)
        src = p_ref.at[:, theirs, :]
        return ([], [_remote(src, s_ref, rsem(0), (x, y, 1 - c))],
                [functools.partial(_remote, src, s_ref, rsem(0), (x, y, c))])

    return _Phase([partial], [], [jax.ShapeDtypeStruct((N_CHIPS, R, C), F32)], 1, 0, build)


def _ph_chip_send(sendb):
    def build(ins, outs, rsem, lsem):
        (s_ref,), (r_ref,) = ins, outs
        x, y, c, chips = _mesh_pos()
        outg = [_remote(s_ref.at[j], r_ref.at[j], rsem(j), (cx, cy, c)) for j, (cx, cy) in enumerate(chips)]
        inc = [functools.partial(_remote, s_ref.at[j], r_ref.at[j], rsem(j), (x, y, c)) for j in range(3)]
        return [], outg, inc

    return _Phase([sendb], [], [jax.ShapeDtypeStruct(sendb.shape, sendb.dtype)], 3, 0, build)


def _ph_half_swap(red):
    R2, C = red.shape
    R = R2 // 2

    def build(ins, outs, rsem, lsem):
        (r_ref,) = outs
        x, y, c, _ = _mesh_pos()
        mine, theirs = _half_rows(c, R)
        return ([], [_remote(r_ref.at[mine], r_ref.at[mine], rsem(0), (x, y, 1 - c))],
                [functools.partial(_remote, r_ref.at[theirs], r_ref.at[theirs], rsem(0), (x, y, c))])

    return _Phase([], [red], [], 1, 0, build)


def _call(body, *, name, grid, in_specs, out_specs, out_shape, scratch_shapes=(), phases=()):
    single = not isinstance(out_specs, (list, tuple))
    out_specs = [out_specs] if single else list(out_specs)
    out_shape = [out_shape] if single else list(out_shape)
    n_in, n_out, n_scr = len(in_specs), len(out_specs), len(scratch_shapes)
    if not phases:
        call = pl.pallas_call(body, name=name, grid=grid, in_specs=in_specs, out_specs=out_specs,
                              out_shape=out_shape, scratch_shapes=list(scratch_shapes),
                              compiler_params=_params(len(grid)))
        return lambda *operands: (list(call(*operands)), [])

    ex_in, ex_out, aliases, spans = [], [], {}, []
    for ph in phases:
        i0, o0 = len(ex_in), len(ex_out)
        ex_in += ph.ins
        for a in ph.inout:
            aliases[n_in + len(ex_in)] = n_out + len(ex_out)
            ex_in.append(a)
            ex_out.append(jax.ShapeDtypeStruct(a.shape, a.dtype))
        ex_out += ph.outs
        spans.append((i0, len(ph.ins), o0, len(ex_out) - o0))
    n_remote = sum(ph.n_remote for ph in phases)
    n_local = max(sum(ph.n_local for ph in phases), 1)

    def wrapped(*refs):
        base_in, xin = refs[:n_in], refs[n_in:n_in + len(ex_in)]
        o0 = n_in + len(ex_in)
        base_out, xout = refs[o0:o0 + n_out], refs[o0 + n_out:o0 + n_out + len(ex_out)]
        scr = refs[o0 + n_out + len(ex_out):]
        send_sems, recv_sems, loc_sems = scr[n_scr:]
        first = functools.reduce(jnp.logical_and, [pl.program_id(i) == 0 for i in range(len(grid))])
        last = functools.reduce(jnp.logical_and, [pl.program_id(i) == grid[i] - 1 for i in range(len(grid))])

        def copies():
            out, r0, l0 = [], 0, 0
            for ph, (i0, ni, p0, no) in zip(phases, spans):
                rsem = lambda k, r0=r0: (send_sems.at[r0 + k], recv_sems.at[r0 + k])
                lsem = lambda k, l0=l0: loc_sems.at[l0 + k]
                out.append(ph.build(xin[i0:i0 + ni], xout[p0:p0 + no], rsem, lsem))
                r0, l0 = r0 + ph.n_remote, l0 + ph.n_local
            return out

        @pl.when(first)
        def _():
            for loc, outg, _ in copies():
                for cp in loc + outg:
                    cp.start()

        body(*base_in, *base_out, *scr[:n_scr])

        @pl.when(last)
        def _():
            for loc, outg, inc in copies():
                for make in inc:
                    make().wait_recv()
                for cp in outg:
                    cp.wait_send()
                for cp in loc:
                    cp.wait()

    hbm = pl.BlockSpec(memory_space=pl.ANY)
    call = pl.pallas_call(
        wrapped, name=name, grid=grid, in_specs=list(in_specs) + [hbm] * len(ex_in),
        out_specs=out_specs + [hbm] * len(ex_out), out_shape=out_shape + ex_out,
        scratch_shapes=list(scratch_shapes) + [pltpu.SemaphoreType.DMA((n_remote,)), pltpu.SemaphoreType.DMA((n_remote,)),
                                              pltpu.SemaphoreType.DMA((n_local,))],
        input_output_aliases=aliases, compiler_params=_params(len(grid)))

    def run(*operands):
        res = call(*operands, *ex_in)
        extra = res[n_out:]
        return list(res[:n_out]), [list(extra[p0:p0 + no]) for (_, _, p0, no) in spans]

    return run


def _inproj_fwd(x, g_mix, w_in, tm, phases=()):
    T = x.shape[0]
    widths = [IN_SEGS[i + 1] - IN_SEGS[i] for i in range(7)]

    def body(x_ref, g_ref, w_ref, h_ref, *z_refs):
        h, _ = _rms_fwd(x_ref[...], g_ref[...])
        hb = h.astype(BF16)
        h_ref[...] = hb
        for j in range(N_CHIPS):
            zj = _dot(hb, w_ref[j])
            for s, lo, hi, off in _seg_pieces(j * IN_BLK, (j + 1) * IN_BLK):
                z_refs[s][:, lo:hi] = zj[:, off:off + hi - lo]

    return _call(
        body, phases=phases, name="inproj_fwd", grid=(T // tm,),
        in_specs=[pl.BlockSpec((tm, D_MODEL), lambda i: (i, 0)), _const((1, D_MODEL)),
                  _const((N_CHIPS, D_MODEL, IN_BLK))],
        out_specs=[pl.BlockSpec((tm, D_MODEL), lambda i: (i, 0))]
        + [pl.BlockSpec((tm, w), lambda i: (i, 0)) for w in widths],
        out_shape=[jax.ShapeDtypeStruct((T, D_MODEL), BF16)]
        + [jax.ShapeDtypeStruct((T, w), F32) for w in widths],
    )(x, g_mix, w_in)


def _inproj_bwd(dz_parts, w_in, x, g_mix, dx1, tm, phases=()):
    T = x.shape[0]
    widths = [IN_SEGS[i + 1] - IN_SEGS[i] for i in range(7)]

    def body(*refs):
        p_refs = refs[:7]
        w_ref, x_ref, g_ref, dx1_ref, gx_ref, dg_ref, dz_ref = refs[7:]

        @pl.when(pl.program_id(0) == 0)
        def _():
            dg_ref[...] = jnp.zeros_like(dg_ref)

        for s in range(7):
            dz_ref[:, IN_SEGS[s]:IN_SEGS[s + 1]] = p_refs[s][...]
        dh = jnp.zeros((tm, D_MODEL), F32)
        for j in range(N_CHIPS):
            dh = dh + _dot_nt(dz_ref[:, j * IN_BLK:(j + 1) * IN_BLK], w_ref[j])
        xv = x_ref[...]
        g = g_ref[...]
        _, r = _rms_fwd(xv, g)
        dx, dg = _rms_bwd(dh, xv, r, g)
        gx_ref[...] = dx1_ref[...] + dx
        dg_ref[...] += dg

    row = lambda w: pl.BlockSpec((tm, w), lambda i: (i, 0))
    return _call(
        body, phases=phases, name="inproj_bwd", grid=(T // tm,),
        in_specs=[row(w) for w in widths]
        + [_const((N_CHIPS, D_MODEL, IN_BLK)), row(D_MODEL), _const((1, D_MODEL)), row(D_MODEL)],
        out_specs=[row(D_MODEL), _const((1, D_MODEL))],
        out_shape=[jax.ShapeDtypeStruct((T, D_MODEL), F32), jax.ShapeDtypeStruct((1, D_MODEL), F32)],
        scratch_shapes=[pltpu.VMEM((tm, IN_TOTAL), BF16)],
    )(*dz_parts, w_in, x, g_mix, dx1)


def _wgrad_in(h0, dz_parts, tm):
    T = h0.shape[0]
    widths = [IN_SEGS[i + 1] - IN_SEGS[i] for i in range(7)]

    def body(*refs):
        h_ref, p_refs, o_ref, acc_ref, sems = refs[0], refs[1:8], refs[8], refs[9], refs[10]
        t = pl.program_id(0)
        last = T // tm - 1

        @pl.when(t == 0)
        def _():
            acc_ref[...] = jnp.zeros_like(acc_ref)

        def accumulate(j):
            for s, lo, hi, off in _seg_pieces(j * IN_BLK, (j + 1) * IN_BLK):
                acc_ref[j, :, off:off + hi - lo] += _dot_tn(h_ref[...], p_refs[s][:, lo:hi])

        @pl.when(t < last)
        def _():
            for j in range(N_CHIPS):
                accumulate(j)

        @pl.when(t == last)
        def _():
            copies = [pltpu.make_async_copy(acc_ref.at[j], o_ref.at[j], sems.at[j]) for j in range(N_CHIPS)]
            for j in range(N_CHIPS):
                accumulate(j)
                copies[j].start()
            for cp in copies:
                cp.wait()

    row = lambda w: pl.BlockSpec((tm, w), lambda i: (i, 0))
    return pl.pallas_call(
        body, name="wgrad_in", grid=(T // tm,), in_specs=[row(D_MODEL)] + [row(w) for w in widths],
        out_specs=pl.BlockSpec(memory_space=pl.ANY),
        out_shape=jax.ShapeDtypeStruct((N_CHIPS, D_MODEL, IN_BLK), F32),
        scratch_shapes=[pltpu.VMEM((N_CHIPS, D_MODEL, IN_BLK), F32), pltpu.SemaphoreType.DMA((N_CHIPS,))],
        compiler_params=_params(1),
    )(h0, *dz_parts)


def _wgrad(a, g, name, blocked, cn, tm, phases=()):
    T, K = a.shape
    N = g.shape[1]
    nb = N // cn

    def body(a_ref, g_ref, o_ref):
        @pl.when(pl.program_id(1) == 0)
        def _():
            o_ref[...] = jnp.zeros_like(o_ref)

        o_ref[...] += _dot_tn(a_ref[...].astype(BF16), g_ref[...].astype(BF16))

    if blocked:
        out_spec = pl.BlockSpec((None, K, cn), lambda j, t: (j, 0, 0))
        out_shape = jax.ShapeDtypeStruct((nb, K, cn), F32)
    else:
        out_spec = pl.BlockSpec((K, cn), lambda j, t: (0, j))
        out_shape = jax.ShapeDtypeStruct((K, N), F32)
    outs, extra = _call(
        body, phases=phases, name=name, grid=(nb, T // tm),
        in_specs=[pl.BlockSpec((tm, K), lambda j, t: (t, 0)), pl.BlockSpec((tm, cn), lambda j, t: (t, j))],
        out_specs=out_spec, out_shape=out_shape,
    )(a, g)
    return outs[0], extra


def _shift_down(x, prev8, sft, row, row8, tm):
    xs = pltpu.roll(x, sft, 0)
    top = jnp.where(row8 < sft, pltpu.roll(prev8, sft, 0), xs[0:8])
    return jnp.concatenate([top, xs[8:]], axis=0)


def _shift_up(x, next8, sft, row8, tm):
    xs = pltpu.roll(x, tm - sft, 0)
    bot = jnp.where(row8 >= 8 - sft, pltpu.roll(next8, 8 - sft, 0), xs[tm - 8:tm])
    return jnp.concatenate([xs[0:tm - 8], bot], axis=0)


def _conv_fwd(x, prev8, cw_ref, cb, row, row8, tm):
    xc = cb + cw_ref[CONV_W - 1:CONV_W, :] * x
    for sft in range(1, CONV_W):
        j = CONV_W - 1 - sft
        xc = xc + cw_ref[j:j + 1, :] * _shift_down(x, prev8, sft, row, row8, tm)
    return xc


def _blockdiag_dot(xb, w_ref, transpose):
    outs = []
    for b in range(D_MODEL // LANES):
        xs = xb[:, b * LANES:(b + 1) * LANES]
        outs.append(_dot_nt(xs, w_ref[b]) if transpose else _dot(xs, w_ref[b]))
    return jnp.concatenate(outs, axis=1)


def _softplus_neg(lam):
    e = jnp.exp(-jnp.abs(lam))
    u = 1.0 + e
    log1p_e = jnp.where(u == 1.0, e, jnp.log(u) * (e / (u - 1.0)))
    sp = jnp.maximum(-lam, 0.0) + log1p_e
    return sp, -_sigmoid(-lam)


def _lru_gates(xc, wrg_ref, brg, wig_ref, big, sp):
    xcb = xc.astype(BF16)
    r = _sigmoid(_blockdiag_dot(xcb, wrg_ref, False) + brg)
    i = _sigmoid(_blockdiag_dot(xcb, wig_ref, False) + big)
    log_a = (-LRU_C) * r * sp
    a = jnp.exp(log_a)
    t = jnp.tanh(log_a)
    one_m_a2 = (-2.0) * t / (1.0 - t)
    mult = jnp.sqrt(one_m_a2)
    return xcb, r, i, a, mult


def _scan_down(a, b, row, tm):
    d = 1
    while d < tm:
        if d < 8:
            keep = row >= d
            a_s = jnp.where(keep, pltpu.roll(a, d, 0), 1.0)
            b_s = jnp.where(keep, pltpu.roll(b, d, 0), 0.0)
            b = a * b_s + b
            a = a * a_s
        else:
            b = jnp.concatenate([b[:d], a[d:] * b[:-d] + b[d:]], axis=0)
            a = jnp.concatenate([a[:d], a[d:] * a[:-d]], axis=0)
        d *= 2
    return a, b


def _scan_up(c, b, row, tm):
    d = 1
    while d < tm:
        if d < 8:
            keep = row < tm - d
            c_s = jnp.where(keep, pltpu.roll(c, tm - d, 0), 1.0)
            b_s = jnp.where(keep, pltpu.roll(b, tm - d, 0), 0.0)
            b = c * b_s + b
            c = c * c_s
        else:
            b = jnp.concatenate([c[:-d] * b[d:] + b[:-d], b[-d:]], axis=0)
            c = jnp.concatenate([c[:-d] * c[d:], c[-d:]], axis=0)
        d *= 2
    return c, b


def _rnn_fwd(xr, gr, conv_w, conv_b, wrg2, b_rg, wig2, b_ig, lam, n_seq, S, tm, phases=()):
    T = xr.shape[0]
    nt = S // tm
    W = D_MODEL

    def body(xr_ref, gr_ref, cw_ref, cb_ref, wrg_ref, brg_ref, wig_ref, big_ref, lam_ref,
             xc_ref, h_ref, r_ref, i_ref, a_ref, mult_ref, ya_ref, px_ref, ph_ref):
        @pl.when(pl.program_id(1) == 0)
        def _():
            px_ref[...] = jnp.zeros_like(px_ref)
            ph_ref[...] = jnp.zeros_like(ph_ref)

        row = lax.broadcasted_iota(jnp.int32, (tm, W), 0)
        row8 = lax.broadcasted_iota(jnp.int32, (8, W), 0)
        x = xr_ref[...]
        xc = _conv_fwd(x, px_ref[...], cw_ref, cb_ref[...], row, row8, tm)
        sp, _ = _softplus_neg(lam_ref[...])
        _, r, i, a, mult = _lru_gates(xc, wrg_ref, brg_ref[...], wig_ref, big_ref[...], sp)
        r_ref[...], i_ref[...], a_ref[...], mult_ref[...] = r, i, a, mult
        bterm = mult * (i * xc)
        acum, hloc = _scan_down(a, bterm, row, tm)
        h = hloc + acum * ph_ref[7:8, :]
        h_ref[...] = h
        xc_ref[...] = xc
        gelu, _ = _gelu_and_grad(gr_ref[...])
        ya_ref[...] = (h * gelu).astype(BF16)
        px_ref[...] = xr_ref[tm - 8:tm, :]
        ph_ref[...] = h_ref[tm - 8:tm, :]

    tile = pl.BlockSpec((tm, W), lambda s, t: (s * nt + t, 0))
    return _call(
        body, phases=phases, name="rnn_fwd", grid=(n_seq, nt),
        in_specs=[tile, tile, _const((CONV_W, W)), _const((1, W)), _const((8, LANES, LANES)), _const((1, W)),
                  _const((8, LANES, LANES)), _const((1, W)), _const((1, W))],
        out_specs=[tile] * 7,
        out_shape=[jax.ShapeDtypeStruct((T, W), F32)] * 6 + [jax.ShapeDtypeStruct((T, W), BF16)],
        scratch_shapes=[pltpu.VMEM((8, W), F32), pltpu.VMEM((8, W), F32)],
    )(xr, gr, conv_w, conv_b, wrg2, b_rg, wig2, b_ig, lam)


def _rnn_bwd(dya, xr, gr, xc, h, gates, conv_w, wrg2, wig2, lam, n_seq, S, tm, phases=()):
    T = xr.shape[0]
    nt = S // tm
    W = D_MODEL
    nb8 = tm // 8

    def body(dya_ref, xr_ref, gr_ref, xc_ref, h_ref, r_ref, i_ref, a_ref, mult_ref, xprev_ref, hprev_ref, cw_ref,
             wrg_ref, wig_ref, lam_ref, dxr_ref, dgr_ref, vec_ref, dwrg_ref, dwig_ref, cg_ref, ndxc_ref, tmp_ref):
        s, ti = pl.program_id(0), pl.program_id(1)

        @pl.when((s == 0) & (ti == 0))
        def _():
            vec_ref[...] = jnp.zeros_like(vec_ref)
            dwrg_ref[...] = jnp.zeros_like(dwrg_ref)
            dwig_ref[...] = jnp.zeros_like(dwig_ref)

        @pl.when(ti == 0)
        def _():
            cg_ref[...] = jnp.zeros_like(cg_ref)
            ndxc_ref[...] = jnp.zeros_like(ndxc_ref)

        first = ti == nt - 1
        row = lax.broadcasted_iota(jnp.int32, (tm, W), 0)
        row8 = lax.broadcasted_iota(jnp.int32, (8, W), 0)
        x = xr_ref[...]
        xc = xc_ref[...]
        hv = h_ref[...]
        xprev = jnp.where(first, 0.0, xprev_ref[...])
        hprev = jnp.where(first, 0.0, hprev_ref[...])
        sp, dsp_dlam = _softplus_neg(lam_ref[...])
        xcb = xc.astype(BF16)
        r, i, a, mult = r_ref[...], i_ref[...], a_ref[...], mult_ref[...]

        gelu, dgelu = _gelu_and_grad(gr_ref[...])
        dya_v = dya_ref[...]
        dgr_ref[...] = (dya_v * hv * dgelu).astype(BF16)
        dh = dya_v * gelu
        c = jnp.where(row < tm - 1, pltpu.roll(a, tm - 1, 0), 1.0)
        ccum, gloc = _scan_up(c, dh, row, tm)
        G = gloc + ccum * cg_ref[0:1, :]
        tmp_ref[...] = a * G
        cg_ref[...] = tmp_ref[0:8, :]

        h_m1 = _shift_down(hv, hprev, 1, row, row8, tm)
        ixc = i * xc
        dixc = G * mult
        dlog_a = (G * h_m1) * a - (G * ixc) * (a * a / mult)
        dr = dlog_a * ((-LRU_C) * sp)
        di = dixc * xc
        drg = dr * r * (1.0 - r)
        dig = di * i * (1.0 - i)
        vec_ref[7:8, :] += jnp.sum(dlog_a * ((-LRU_C) * r), axis=0, keepdims=True) * dsp_dlam
        vec_ref[5:6, :] += jnp.sum(drg, axis=0, keepdims=True)
        vec_ref[6:7, :] += jnp.sum(dig, axis=0, keepdims=True)
        drgb = drg.astype(BF16)
        digb = dig.astype(BF16)
        dxc = dixc * i + _blockdiag_dot(drgb, wrg_ref, True) + _blockdiag_dot(digb, wig_ref, True)
        for b in range(W // LANES):
            sl = slice(b * LANES, (b + 1) * LANES)
            dwrg_ref[b] += _dot_tn(xcb[:, sl], drgb[:, sl])
            dwig_ref[b] += _dot_tn(xcb[:, sl], digb[:, sl])

        vec_ref[4:5, :] += jnp.sum(dxc, axis=0, keepdims=True)
        vec_ref[3:4, :] += jnp.sum(dxc * x, axis=0, keepdims=True)
        dxr = cw_ref[CONV_W - 1:CONV_W, :] * dxc
        nxt = ndxc_ref[...]
        for sft in range(1, CONV_W):
            j = CONV_W - 1 - sft
            vec_ref[j:j + 1, :] += jnp.sum(dxc * _shift_down(x, xprev, sft, row, row8, tm), axis=0, keepdims=True)
            dxr = dxr + cw_ref[j:j + 1, :] * _shift_up(dxc, nxt, sft, row8, tm)
        dxr_ref[...] = dxr.astype(BF16)
        tmp_ref[...] = dxc
        ndxc_ref[...] = tmp_ref[0:8, :]

    rev = lambda s, t: (s * nt + nt - 1 - t, 0)
    tile = pl.BlockSpec((tm, W), rev)
    prev8 = pl.BlockSpec((8, W), lambda s, t: (jnp.maximum((s * nt + nt - 1 - t) * nb8 - 1, 0), 0))
    return _call(
        body, phases=phases, name="rnn_bwd", grid=(n_seq, nt),
        in_specs=[tile] * 9 + [prev8, prev8, _const((CONV_W, W)), _const((8, LANES, LANES)),
                               _const((8, LANES, LANES)), _const((1, W))],
        out_specs=[tile, tile, _const((16, W)), _const((8, LANES, LANES)), _const((8, LANES, LANES))],
        out_shape=[jax.ShapeDtypeStruct((T, W), BF16), jax.ShapeDtypeStruct((T, W), BF16),
                   jax.ShapeDtypeStruct((16, W), F32), jax.ShapeDtypeStruct((8, LANES, LANES), F32),
                   jax.ShapeDtypeStruct((8, LANES, LANES), F32)],
        scratch_shapes=[pltpu.VMEM((8, W), F32), pltpu.VMEM((8, W), F32), pltpu.VMEM((tm, W), F32)],
    )(dya, xr, gr, xc, h, *gates, xr, h, conv_w, wrg2, wig2, lam)


def _head_swap(t, lane):
    w = t.shape[1]
    return jnp.where(lane % HEAD_DIM < HEAD_DIM // 2, pltpu.roll(t, w - HEAD_DIM // 2, 1),
                     pltpu.roll(t, HEAD_DIM // 2, 1))


def _qk_prep(t, gain, cosf, sins, ind, indt, lane):
    ms = _split_dot(t * t, ind) * (1.0 / HEAD_DIM)
    rstd = _split_dot(lax.rsqrt(ms + NORM_EPS), indt)
    tn = (t * rstd) * gain
    return tn * cosf + _head_swap(tn, lane) * sins, rstd


def _qk_prep_bwd(dy, t, rstd, gain, cosf, sins, ind, indt, lane):
    dtn = dy * cosf + _head_swap(dy * sins, lane)
    dgain = jnp.sum(dtn * (t * rstd), axis=0, keepdims=True)
    dn = dtn * gain
    m = _split_dot(_split_dot(dn * t, ind), indt) * (1.0 / HEAD_DIM)
    return rstd * dn - t * (rstd * rstd * rstd * m), dgain


def _attn_mask_t(blk_idx):
    ci = lax.broadcasted_iota(jnp.int32, (2 * WINDOW, WINDOW), 0)
    qi = lax.broadcasted_iota(jnp.int32, (2 * WINDOW, WINDOW), 1)
    diff = WINDOW + qi - ci
    return (diff >= 0) & (diff < WINDOW) & ((ci >= WINDOW) | (blk_idx > 0))


def _stack_heads(t, kvh, lo):
    parts = []
    for i in (2 * kvh, 2 * kvh + 1):
        tp = t[:, i * LANES:(i + 1) * LANES]
        parts += [jnp.where(lo, tp, 0.0), jnp.where(lo, 0.0, tp)]
    return jnp.concatenate(parts, axis=0).astype(BF16)


def _unstack_heads(ts, lo):
    w = WINDOW
    return jnp.where(lo, ts[0:w], ts[w:2 * w]), jnp.where(lo, ts[2 * w:3 * w], ts[3 * w:4 * w])


def _dup_head(t, kvh, lo2):
    m = kvh // 2
    t2 = t[:, m * LANES:(m + 1) * LANES]
    t2r = pltpu.roll(t2, HEAD_DIM, 1)
    return (jnp.where(lo2, t2, t2r) if kvh % 2 == 0 else jnp.where(lo2, t2r, t2)).astype(BF16)


def _fold_head(ts, kvh, lo2):
    tot = ts + pltpu.roll(ts, HEAD_DIM, 1)
    own = lo2 if kvh % 2 == 0 else ~lo2
    return jnp.where(own, tot, 0.0)


KEY_CHUNKS = tuple(slice(i * 64, (i + 1) * 64) for i in range(2 * WINDOW // 64))


def _fold8(x, op):
    return op(x.reshape(x.shape[0] // 8, 8, x.shape[1]), axis=0)


def _softmax_stats(s_ref, b, cols, sink):
    m8 = None
    for c in KEY_CHUNKS:
        t = _fold8(s_ref[b, c, cols], jnp.max)
        m8 = t if m8 is None else jnp.maximum(m8, t)
    mx = jnp.maximum(jnp.max(m8, axis=0, keepdims=True), sink)
    d8 = None
    for c in KEY_CHUNKS:
        t = _fold8(jnp.exp(s_ref[b, c, cols] - mx), jnp.sum)
        d8 = t if d8 is None else d8 + t
    es = jnp.exp(sink - mx)
    inv = 1.0 / (jnp.sum(d8, axis=0, keepdims=True) + es)
    return mx, inv, es * inv


def _attn_fwd(q, k, v, qg, kg, sinks, cosf, sins, ind_q, ind_qt, ind_k, ind_kt, n_seq, S, phases=()):
    T = q.shape[0]
    nblk = S // WINDOW
    W = D_MODEL

    def body(sink_ref, q_ref, k_ref, v_ref, qg_ref, kg_ref, cos_ref, sin_ref, iq_ref, iqt_ref, ik_ref, ikt_ref,
             o_ref, kc_ref, vc_ref, s_ref, p_ref, qs_ref, kd_ref, vd_ref):
        n = pl.program_id(1)

        @pl.when(n == 0)
        def _():
            kc_ref[...] = jnp.zeros_like(kc_ref)
            vc_ref[...] = jnp.zeros_like(vc_ref)

        lane = lax.broadcasted_iota(jnp.int32, (WINDOW, W), 1)
        lo = lane[:, :LANES] < HEAD_DIM
        lo2 = lax.broadcasted_iota(jnp.int32, (2 * WINDOW, LANES), 1) < HEAD_DIM
        cosf, sinv = jnp.tile(cos_ref[...], (1, W // LANES)), jnp.tile(sin_ref[...], (1, W // LANES))
        qr, _ = _qk_prep(q_ref[...], qg_ref[...], cosf, sinv, iq_ref[...], iqt_ref[...], lane)
        kr, _ = _qk_prep(k_ref[...], kg_ref[...], cosf[:, :KV_W], sinv[:, :KV_W], ik_ref[...], ikt_ref[...],
                         lane[:, :KV_W])
        kc_ref[WINDOW:2 * WINDOW, :] = kr
        vc_ref[WINDOW:2 * WINDOW, :] = v_ref[...]
        kc, vc = kc_ref[...], vc_ref[...]
        mask = jnp.tile(_attn_mask_t(n), (1, 4))
        qr = qr * HEAD_DIM ** -0.5
        for kvh in range(N_KV):
            qs_ref[kvh] = _stack_heads(qr, kvh, lo)
            kd_ref[kvh] = _dup_head(kc, kvh, lo2)
            vd_ref[kvh] = _dup_head(vc, kvh, lo2)

        def scores(kvh):
            s_ref[kvh % 2] = jnp.where(mask, _dot_nt(kd_ref[kvh], qs_ref[kvh]), -1e30)

        def softmax(kvh):
            b = kvh % 2
            for r in range(4):
                cols = slice(r * WINDOW, (r + 1) * WINDOW)
                mx, inv, _ = _softmax_stats(s_ref, b, cols, sink_ref[4 * kvh + r])
                for c in KEY_CHUNKS:
                    p_ref[b, c, cols] = (jnp.exp(s_ref[b, c, cols] - mx) * inv).astype(BF16)

        def output(kvh):
            o0, o1 = _unstack_heads(_dot_tn(p_ref[kvh % 2], vd_ref[kvh]), lo)
            o_ref[:, (2 * kvh) * LANES:(2 * kvh + 1) * LANES] = o0.astype(BF16)
            o_ref[:, (2 * kvh + 1) * LANES:(2 * kvh + 2) * LANES] = o1.astype(BF16)

        scores(0)
        for kvh in range(N_KV):
            if kvh + 1 < N_KV:
                scores(kvh + 1)
            softmax(kvh)
            output(kvh)
        kc_ref[0:WINDOW, :] = kr
        vc_ref[0:WINDOW, :] = v_ref[...]

    blk = lambda w: pl.BlockSpec((WINDOW, w), lambda s, n: (s * nblk + n, 0))
    pos = pl.BlockSpec((WINDOW, LANES), lambda s, n: (n, 0))
    outs, extra = _call(
        body, phases=phases, name="attn_fwd", grid=(n_seq, nblk),
        in_specs=[pl.BlockSpec(memory_space=pltpu.SMEM), blk(W), blk(KV_W), blk(KV_W), _const((1, W)),
                  _const((1, KV_W)), pos, pos, _const((W, LANES)), _const((LANES, W)), _const((KV_W, LANES)),
                  _const((LANES, KV_W))],
        out_specs=blk(W), out_shape=jax.ShapeDtypeStruct((T, W), BF16),
        scratch_shapes=[pltpu.VMEM((2 * WINDOW, KV_W), F32), pltpu.VMEM((2 * WINDOW, KV_W), F32),
                        pltpu.VMEM((2, 2 * WINDOW, 4 * WINDOW), F32), pltpu.VMEM((2, 2 * WINDOW, 4 * WINDOW), BF16),
                        pltpu.VMEM((N_KV, 4 * WINDOW, LANES), BF16), pltpu.VMEM((N_KV, 2 * WINDOW, LANES), BF16),
                        pltpu.VMEM((N_KV, 2 * WINDOW, LANES), BF16)],
    )(sinks, q, k, v, qg, kg, cosf, sins, ind_q, ind_qt, ind_k, ind_kt)
    return outs[0], extra


def _attn_bwd(do, q, k, v, qg, kg, sinks, cosf, sins, ind_q, ind_qt, ind_k, ind_kt, n_seq, S, phases=()):
    T = q.shape[0]
    nblk = S // WINDOW
    W = D_MODEL

    def body(sink_ref, do_ref, q_ref, k_ref, v_ref, qg_ref, kg_ref, cos_ref, sin_ref, iq_ref, iqt_ref, ik_ref,
             ikt_ref, dq_ref, dkc_ref, dkp_ref, dvc_ref, dvp_ref, dqg_ref, dsk_ref, kc_ref, vc_ref, dqr_ref,
             dk_ref, dv_ref, s_ref, dp_ref, p_ref, ds_ref, qs_ref, dos_ref, kd_ref, vd_ref):
        s_id, n = pl.program_id(0), pl.program_id(1)

        @pl.when((s_id == 0) & (n == 0))
        def _():
            dqg_ref[...] = jnp.zeros_like(dqg_ref)
            dsk_ref[...] = jnp.zeros_like(dsk_ref)

        @pl.when(n == 0)
        def _():
            kc_ref[...] = jnp.zeros_like(kc_ref)
            vc_ref[...] = jnp.zeros_like(vc_ref)

        lane = lax.broadcasted_iota(jnp.int32, (WINDOW, W), 1)
        lane_k = lane[:, :KV_W]
        lane128 = lane[:, :LANES]
        cosf, sinv = jnp.tile(cos_ref[...], (1, W // LANES)), jnp.tile(sin_ref[...], (1, W // LANES))
        qv = q_ref[...]
        qr, q_rstd = _qk_prep(qv, qg_ref[...], cosf, sinv, iq_ref[...], iqt_ref[...], lane)
        kr, _ = _qk_prep(k_ref[...], kg_ref[...], cosf[:, :KV_W], sinv[:, :KV_W], ik_ref[...], ikt_ref[...], lane_k)
        kc_ref[WINDOW:2 * WINDOW, :] = kr
        vc_ref[WINDOW:2 * WINDOW, :] = v_ref[...]
        kc, vc = kc_ref[...], vc_ref[...]
        dov = do_ref[...]
        mask = jnp.tile(_attn_mask_t(n), (1, 4))
        lo = lane128 < HEAD_DIM
        lo2 = lax.broadcasted_iota(jnp.int32, (2 * WINDOW, LANES), 1) < HEAD_DIM
        scale = HEAD_DIM ** -0.5
        qr = qr * scale
        dk_ref[...] = jnp.zeros_like(dk_ref)
        dv_ref[...] = jnp.zeros_like(dv_ref)
        for kvh in range(N_KV):
            qs_ref[kvh] = _stack_heads(qr, kvh, lo)
            dos_ref[kvh] = _stack_heads(dov, kvh, lo)
            kd_ref[kvh] = _dup_head(kc, kvh, lo2)
            vd_ref[kvh] = _dup_head(vc, kvh, lo2)

        def scores(kvh):
            b = kvh % 2
            s_ref[b] = jnp.where(mask, _dot_nt(kd_ref[kvh], qs_ref[kvh]), -1e30)
            dp_ref[b] = _dot_nt(vd_ref[kvh], dos_ref[kvh])

        def softmax(kvh):
            b = kvh % 2
            for r in range(4):
                cols = slice(r * WINDOW, (r + 1) * WINDOW)
                head = 4 * kvh + r
                mx, inv, ps = _softmax_stats(s_ref, b, cols, sink_ref[head])
                g8 = None
                for c in KEY_CHUNKS:
                    t = _fold8(jnp.exp(s_ref[b, c, cols] - mx) * dp_ref[b, c, cols], jnp.sum)
                    g8 = t if g8 is None else g8 + t
                dd = jnp.sum(g8, axis=0, keepdims=True) * inv
                for c in KEY_CHUNKS:
                    p = jnp.exp(s_ref[b, c, cols] - mx) * inv
                    p_ref[b, c, cols] = p.astype(BF16)
                    ds_ref[b, c, cols] = (p * (dp_ref[b, c, cols] - dd)).astype(BF16)
                dsk_ref[head:head + 1, :] -= ps * dd

        def grads(kvh):
            m, b = kvh // 2, kvh % 2
            dq0, dq1 = _unstack_heads(_dot_tn(ds_ref[b], kd_ref[kvh]) * scale, lo)
            dqr_ref[:, (2 * kvh) * LANES:(2 * kvh + 1) * LANES] = dq0
            dqr_ref[:, (2 * kvh + 1) * LANES:(2 * kvh + 2) * LANES] = dq1
            dk_ref[:, m * LANES:(m + 1) * LANES] += _fold_head(_dot(ds_ref[b], qs_ref[kvh]), kvh, lo2)
            dv_ref[:, m * LANES:(m + 1) * LANES] += _fold_head(_dot(p_ref[b], dos_ref[kvh]), kvh, lo2)

        scores(0)
        for kvh in range(N_KV):
            if kvh + 1 < N_KV:
                scores(kvh + 1)
            softmax(kvh)
            grads(kvh)
        dq, dqg = _qk_prep_bwd(dqr_ref[...], qv, q_rstd, qg_ref[...], cosf, sinv, iq_ref[...], iqt_ref[...], lane)
        dq_ref[...] = dq.astype(BF16)
        dqg_ref[...] += dqg
        dkp_ref[...] = dk_ref[0:WINDOW, :]
        dkc_ref[...] = dk_ref[WINDOW:2 * WINDOW, :]
        dvp_ref[...] = dv_ref[0:WINDOW, :]
        dvc_ref[...] = dv_ref[WINDOW:2 * WINDOW, :]
        kc_ref[0:WINDOW, :] = kr
        vc_ref[0:WINDOW, :] = v_ref[...]

    blk = lambda w: pl.BlockSpec((WINDOW, w), lambda s, n: (s * nblk + n, 0))
    pos = pl.BlockSpec((WINDOW, LANES), lambda s, n: (n, 0))
    kv_out = jax.ShapeDtypeStruct((T, KV_W), F32)
    stage = lambda dt: pltpu.VMEM((2, 2 * WINDOW, 4 * WINDOW), dt)
    return _call(
        body, phases=phases, name="attn_bwd", grid=(n_seq, nblk),
        in_specs=[pl.BlockSpec(memory_space=pltpu.SMEM), blk(W), blk(W), blk(KV_W), blk(KV_W), _const((1, W)),
                  _const((1, KV_W)), pos, pos, _const((W, LANES)), _const((LANES, W)), _const((KV_W, LANES)),
                  _const((LANES, KV_W))],
        out_specs=[blk(W), blk(KV_W), blk(KV_W), blk(KV_W), blk(KV_W), _const((1, W)), _const((N_HEADS, LANES))],
        out_shape=[jax.ShapeDtypeStruct((T, W), BF16), kv_out, kv_out, kv_out, kv_out,
                   jax.ShapeDtypeStruct((1, W), F32), jax.ShapeDtypeStruct((N_HEADS, LANES), F32)],
        scratch_shapes=[pltpu.VMEM((2 * WINDOW, KV_W), F32), pltpu.VMEM((2 * WINDOW, KV_W), F32),
                        pltpu.VMEM((WINDOW, W), F32), pltpu.VMEM((2 * WINDOW, KV_W), F32),
                        pltpu.VMEM((2 * WINDOW, KV_W), F32), stage(F32), stage(F32), stage(BF16), stage(BF16),
                        pltpu.VMEM((N_KV, 4 * WINDOW, LANES), BF16), pltpu.VMEM((N_KV, 4 * WINDOW, LANES), BF16),
                        pltpu.VMEM((N_KV, 2 * WINDOW, LANES), BF16), pltpu.VMEM((N_KV, 2 * WINDOW, LANES), BF16)],
    )(sinks, do, q, k, v, qg, kg, cosf, sins, ind_q, ind_qt, ind_k, ind_kt)


def _kv_bwd(dkc, dkp, dvc, dvp, k, kg, cosf, sins, ind_k, ind_kt, n_seq, S, phases=()):
    T = k.shape[0]
    nblk = S // WINDOW

    def body(dkc_ref, dkp_ref, dvc_ref, dvp_ref, k_ref, kg_ref, cos_ref, sin_ref, ik_ref, ikt_ref,
             dk_ref, dv_ref, dkg_ref):
        s_id, n = pl.program_id(0), pl.program_id(1)

        @pl.when((s_id == 0) & (n == 0))
        def _():
            dkg_ref[...] = jnp.zeros_like(dkg_ref)

        has_next = n < nblk - 1
        lane = lax.broadcasted_iota(jnp.int32, (WINDOW, KV_W), 1)
        dkr = dkc_ref[...] + jnp.where(has_next, dkp_ref[...], 0.0)
        dv_ref[...] = (dvc_ref[...] + jnp.where(has_next, dvp_ref[...], 0.0)).astype(BF16)
        cosf, sinv = jnp.tile(cos_ref[...], (1, KV_W // LANES)), jnp.tile(sin_ref[...], (1, KV_W // LANES))
        kv = k_ref[...]
        _, rstd = _qk_prep(kv, kg_ref[...], cosf, sinv, ik_ref[...], ikt_ref[...], lane)
        dk, dkg = _qk_prep_bwd(dkr, kv, rstd, kg_ref[...], cosf, sinv, ik_ref[...], ikt_ref[...], lane)
        dk_ref[...] = dk.astype(BF16)
        dkg_ref[...] += dkg

    cur = pl.BlockSpec((WINDOW, KV_W), lambda s, n: (s * nblk + n, 0))
    nxt = pl.BlockSpec((WINDOW, KV_W), lambda s, n: (s * nblk + jnp.minimum(n + 1, nblk - 1), 0))
    pos = pl.BlockSpec((WINDOW, LANES), lambda s, n: (n, 0))
    return _call(
        body, phases=phases, name="kv_bwd", grid=(n_seq, nblk),
        in_specs=[cur, nxt, cur, nxt, cur, _const((1, KV_W)), pos, pos, _const((KV_W, LANES)),
                  _const((LANES, KV_W))],
        out_specs=[cur, cur, _const((1, KV_W))],
        out_shape=[jax.ShapeDtypeStruct((T, KV_W), BF16), jax.ShapeDtypeStruct((T, KV_W), BF16),
                   jax.ShapeDtypeStruct((1, KV_W), F32)],
    )(dkc, dkp, dvc, dvp, k, kg, cosf, sins, ind_k, ind_kt)


def _merge_fwd(x, ya, o, ga, gb, w_rnn, w_attn, w_out, tm, phases=()):
    T = x.shape[0]
    W = D_MODEL

    def body(x_ref, ya_ref, o_ref, ga_ref, gb_ref, wr_ref, wa_ref, wo_ref, x1_ref, mg_ref):
        y_a = _dot(ya_ref[...], wr_ref[...])
        y_b = _dot(o_ref[...], wa_ref[...])
        mg = (_sigmoid(ga_ref[...]) * y_a + _sigmoid(gb_ref[...]) * y_b).astype(BF16)
        mg_ref[...] = mg
        x1_ref[...] = x_ref[...] + _dot(mg, wo_ref[...])

    row = pl.BlockSpec((tm, W), lambda i: (i, 0))
    sq = _const((W, W))
    return _call(
        body, phases=phases, name="merge_fwd", grid=(T // tm,),
        in_specs=[row, row, row, row, row, sq, sq, sq], out_specs=[row, row],
        out_shape=[jax.ShapeDtypeStruct((T, W), F32), jax.ShapeDtypeStruct((T, W), BF16)],
    )(x, ya, o, ga, gb, w_rnn, w_attn, w_out)


def _merge_bwd(dx1, ga, gb, ya, o, w_rnn, w_attn, w_out, tm, phases=()):
    T = dx1.shape[0]
    W = D_MODEL

    def body(dx1_ref, ga_ref, gb_ref, ya_ref, o_ref, wr_ref, wa_ref, wo_ref,
             dga_ref, dgb_ref, dya_ref, dyb_ref, dyain_ref, do_ref):
        dm = _dot_nt(dx1_ref[...].astype(BF16), wo_ref[...])
        sa = _sigmoid(ga_ref[...])
        sb = _sigmoid(gb_ref[...])
        dga_ref[...] = (dm * _dot(ya_ref[...], wr_ref[...]) * (sa * (1.0 - sa))).astype(BF16)
        dgb_ref[...] = (dm * _dot(o_ref[...], wa_ref[...]) * (sb * (1.0 - sb))).astype(BF16)
        dya = (dm * sa).astype(BF16)
        dyb = (dm * sb).astype(BF16)
        dya_ref[...] = dya
        dyb_ref[...] = dyb
        dyain_ref[...] = _dot_nt(dya, wr_ref[...])
        do_ref[...] = _dot_nt(dyb, wa_ref[...])

    row = pl.BlockSpec((tm, W), lambda i: (i, 0))
    sq = _const((W, W))
    b16 = jax.ShapeDtypeStruct((T, W), BF16)
    f32 = jax.ShapeDtypeStruct((T, W), F32)
    return _call(
        body, phases=phases, name="merge_bwd", grid=(T // tm,),
        in_specs=[row, row, row, row, row, sq, sq, sq], out_specs=[row] * 6,
        out_shape=[b16, b16, b16, b16, f32, f32],
    )(dx1, ga, gb, ya, o, w_rnn, w_attn, w_out)


def _mlp_fwd(x1, g_mlp, w_up, w_down, tm, phases=()):
    T = x1.shape[0]
    W = D_MODEL

    def body(x_ref, g_ref, wu_ref, wd_ref, x2_ref, hm_ref, u_ref, act_ref):
        xv = x_ref[...]
        hm, _ = _rms_fwd(xv, g_ref[...])
        hmb = hm.astype(BF16)
        hm_ref[...] = hmb
        for j in range(N_CHIPS):
            u = _dot(hmb, wu_ref[j])
            u_ref[:, j * W:(j + 1) * W] = u
            ru = jnp.maximum(u, 0.0)
            act_ref[:, j * W:(j + 1) * W] = (ru * ru).astype(BF16)
        x2_ref[...] = xv + _dot(act_ref[...], wd_ref[...])

    row = lambda w: pl.BlockSpec((tm, w), lambda i: (i, 0))
    return _call(
        body, phases=phases, name="mlp_fwd", grid=(T // tm,),
        in_specs=[row(W), _const((1, W)), _const((N_CHIPS, W, W)), _const((D_FF, W))],
        out_specs=[row(W), row(W), row(D_FF), row(D_FF)],
        out_shape=[jax.ShapeDtypeStruct((T, W), F32), jax.ShapeDtypeStruct((T, W), BF16),
                   jax.ShapeDtypeStruct((T, D_FF), F32), jax.ShapeDtypeStruct((T, D_FF), BF16)],
    )(x1, g_mlp, w_up, w_down)


def _mlp_bwd(dx2, u, x1, g_mlp, w_up, w_down, tm, phases=()):
    T = x1.shape[0]
    W = D_MODEL

    def body(dx2_ref, u_ref, x_ref, g_ref, wu_ref, wd_ref, dx1_ref, du_ref, dg_ref):
        @pl.when(pl.program_id(0) == 0)
        def _():
            dg_ref[...] = jnp.zeros_like(dg_ref)

        dx2 = dx2_ref[...]
        dact = _dot_nt(dx2.astype(BF16), wd_ref[...])
        du_ref[...] = (dact * (2.0 * jnp.maximum(u_ref[...], 0.0))).astype(BF16)
        dhm = jnp.zeros((tm, W), F32)
        for j in range(N_CHIPS):
            dhm = dhm + _dot_nt(du_ref[:, j * W:(j + 1) * W], wu_ref[j])
        xv = x_ref[...]
        g = g_ref[...]
        _, r = _rms_fwd(xv, g)
        dx, dg = _rms_bwd(dhm, xv, r, g)
        dx1_ref[...] = dx2 + dx
        dg_ref[...] += dg

    row = lambda w: pl.BlockSpec((tm, w), lambda i: (i, 0))
    return _call(
        body, phases=phases, name="mlp_bwd", grid=(T // tm,),
        in_specs=[row(W), row(D_FF), row(W), _const((1, W)), _const((N_CHIPS, W, W)), _const((D_FF, W))],
        out_specs=[row(W), row(D_FF), _const((1, W))],
        out_shape=[jax.ShapeDtypeStruct((T, W), F32), jax.ShapeDtypeStruct((T, D_FF), BF16),
                   jax.ShapeDtypeStruct((1, W), F32)],
    )(dx2, u, x1, g_mlp, w_up, w_down)


def _ple_loss(x2, p, target, g_ple, w_gate, w_proj, tm, phases=()):
    T = x2.shape[0]
    W = D_MODEL
    cw = W // N_CHIPS

    def body(x_ref, p_ref, t_ref, g_ref, wg_ref, wp_ref, loss_ref, dx2_ref, pb_ref, de_ref, hp_ref, dtg_ref, dg_ref):
        @pl.when(pl.program_id(0) == 0)
        def _():
            dg_ref[...] = jnp.zeros_like(dg_ref)
            loss_ref[...] = jnp.zeros_like(loss_ref)

        xv = x_ref[...]
        g = g_ref[...]
        pb = p_ref[...].astype(BF16)
        pb_ref[...] = pb
        e = jnp.concatenate([_dot(pb, wp_ref[j]) for j in range(N_CHIPS)], axis=1)
        hp, r = _rms_fwd(xv, g)
        hpb = hp.astype(BF16)
        hp_ref[...] = hpb
        sg = _sigmoid(_dot(hpb, wg_ref[...]))
        diff = (xv + e * sg) - t_ref[...]
        loss_ref[...] += jnp.sum(diff * diff) * (0.5 / W)
        dx3 = diff * (1.0 / W)
        de_ref[...] = (dx3 * sg).astype(BF16)
        dtg = (dx3 * e * (sg * (1.0 - sg))).astype(BF16)
        dtg_ref[...] = dtg
        dx, dg = _rms_bwd(_dot_nt(dtg, wg_ref[...]), xv, r, g)
        dx2_ref[...] = dx3 + dx
        dg_ref[...] += dg

    row = lambda w: pl.BlockSpec((tm, w), lambda i: (i, 0))
    b16 = lambda w: jax.ShapeDtypeStruct((T, w), BF16)
    return _call(
        body, phases=phases, name="ple_loss", grid=(T // tm,),
        in_specs=[row(W), row(PLE_DIM), row(W), _const((1, W)), _const((W, W)), _const((N_CHIPS, PLE_DIM, cw))],
        out_specs=[_const((8, LANES)), row(W), row(PLE_DIM), row(W), row(W), row(W), _const((1, W))],
        out_shape=[jax.ShapeDtypeStruct((8, LANES), F32), jax.ShapeDtypeStruct((T, W), F32), b16(PLE_DIM),
                   b16(W), b16(W), b16(W), jax.ShapeDtypeStruct((1, W), F32)],
    )(x2, p, target, g_ple, w_gate, w_proj)


def _adamw(w, g, m, v, name, tr, phases=()):
    R, C = w.shape
    c1 = 1.0 / (1.0 - ADAM_B1 ** ADAM_STEP)
    c2 = 1.0 / (1.0 - ADAM_B2 ** ADAM_STEP)

    def body(w_ref, g_ref, m_ref, v_ref, go_ref, d_ref, nm_ref, nv_ref):
        gv = g_ref[...]
        go_ref[...] = gv
        nm = ADAM_B1 * m_ref[...] + (1.0 - ADAM_B1) * gv
        nv = ADAM_B2 * v_ref[...] + (1.0 - ADAM_B2) * (gv * gv)
        nm_ref[...] = nm
        nv_ref[...] = nv
        d_ref[...] = (-ADAM_LR) * ((nm * c1) / (jnp.sqrt(nv * c2) + ADAM_EPS) + ADAM_WD * w_ref[...])

    row = pl.BlockSpec((tr, C), lambda i: (i, 0))
    sds = jax.ShapeDtypeStruct((R, C), F32)
    return _call(
        body, phases=phases, name=name, grid=(R // tr,), in_specs=[row] * 4, out_specs=[row] * 4,
        out_shape=[sds] * 4,
    )(w, g, m, v)


def _indicator(width):
    ind = np.zeros((width, LANES), np.float32)
    ind[np.arange(width), np.arange(width) // HEAD_DIM] = 1.0
    return jnp.asarray(ind, BF16), jnp.asarray(ind.T, BF16)


def _rope_tables(S):
    inv = ROPE_THETA ** (-jnp.arange(0, HEAD_DIM, 2, dtype=F32) / HEAD_DIM)
    ang = jnp.arange(S, dtype=F32)[:, None] * inv[None, :]
    cos, sin = jnp.cos(ang), jnp.sin(ang)
    cosf = jnp.tile(jnp.concatenate([cos, cos], axis=1), (1, LANES // HEAD_DIM))
    sins = jnp.tile(jnp.concatenate([-sin, sin], axis=1), (1, LANES // HEAD_DIM))
    return cosf, sins


def _pair_blockdiag(w):
    w4 = w.reshape(8, 2, HEAD_DIM, HEAD_DIM)
    eye = jnp.eye(2, dtype=w.dtype)
    return jnp.einsum("bpij,pq->bpiqj", w4, eye).reshape(8, LANES, LANES)


def _pair_blockdiag_extract(g):
    g5 = g.reshape(8, 2, HEAD_DIM, 2, HEAD_DIM)
    return jnp.stack([g5[:, 0, :, 0, :], g5[:, 1, :, 1, :]], axis=1).reshape(16, HEAD_DIM, HEAD_DIM)


def _pair_sum(parts, sibs, name):
    n = len(parts)
    dims = [(p.shape[1] // 2, p.shape[2]) for p in parts]

    def body(*refs):
        p_r, s_r, send_r, own_r, mine_r, sem = (refs[0:n], refs[n:2 * n], refs[2 * n:3 * n], refs[3 * n:4 * n],
                                                refs[4 * n:5 * n], refs[5 * n])
        x, y, c, chips = _mesh_pos()
        me = 2 * x + y
        loads = []
        for i, (R, _) in enumerate(dims):
            mine, _ = _half_rows(c, R)
            cp = pltpu.make_async_copy(p_r[i].at[:, mine, :], mine_r[i], sem.at[i])
            cp.start()
            loads.append(cp)
        for i in range(n):
            loads[i].wait()
            for j, (cx, cy) in enumerate(chips):
                k = 2 * cx + cy
                send_r[i][j] = (mine_r[i][k] + s_r[i][k]).astype(BF16)
            own_r[i][...] = mine_r[i][me] + s_r[i][me]

    vm = pl.BlockSpec(memory_space=pltpu.VMEM)
    out = pl.pallas_call(
        body, name=name, in_specs=[pl.BlockSpec(memory_space=pl.ANY)] * n + [vm] * n, out_specs=[vm] * (2 * n),
        out_shape=[jax.ShapeDtypeStruct((3, R, C), BF16) for R, C in dims]
        + [jax.ShapeDtypeStruct((R, C), F32) for R, C in dims],
        scratch_shapes=[pltpu.VMEM((N_CHIPS, R, C), F32) for R, C in dims] + [pltpu.SemaphoreType.DMA((n,))],
        compiler_params=pltpu.CompilerParams(vmem_limit_bytes=VMEM_LIMIT),
    )(*parts, *sibs)
    return out[:n], out[n:]


def _chip_sum(owns, recvs, name):
    n = len(owns)
    dims = [o.shape for o in owns]

    def body(*refs):
        own_r, recv_r, red_r, stage_r, sem = refs[0:n], refs[n:2 * n], refs[2 * n:3 * n], refs[3 * n:4 * n], refs[4 * n]
        x, y, c, _ = _mesh_pos()
        me = 2 * x + y
        stores = []
        for i, (R, _) in enumerate(dims):
            for k_me in range(N_CHIPS):

                @pl.when(me == k_me)
                def _():
                    acc = None
                    for k in range(N_CHIPS):
                        slot = ((k // 2) ^ (k_me // 2)) + 2 * ((k % 2) ^ (k_me % 2)) - 1
                        term = own_r[i][...] if k == k_me else recv_r[i][slot].astype(F32)
                        acc = term if acc is None else acc + term
                    stage_r[i][...] = acc

            mine, _ = _half_rows(c, R)
            cp = pltpu.make_async_copy(stage_r[i], red_r[i].at[mine, :], sem.at[i])
            cp.start()
            stores.append(cp)
        for cp in stores:
            cp.wait()

    vm = pl.BlockSpec(memory_space=pltpu.VMEM)
    return pl.pallas_call(
        body, name=name, in_specs=[vm] * (2 * n), out_specs=[pl.BlockSpec(memory_space=pl.ANY)] * n,
        out_shape=[jax.ShapeDtypeStruct((2 * R, C), F32) for R, C in dims],
        scratch_shapes=[pltpu.VMEM((R, C), F32) for R, C in dims] + [pltpu.SemaphoreType.DMA((n,))],
        compiler_params=pltpu.CompilerParams(vmem_limit_bytes=VMEM_LIMIT),
    )(*owns, *recvs)


def _gather_bf16(shard, name):
    R2, C = shard.shape
    R = R2 // 2
    H = R // 2

    def body(s_ref, o_ref, send_sems, recv_sems):
        x, y, c, _ = _mesh_pos()
        me, chip_x, chip_y, chip_d = 2 * x + y, 2 * (1 - x) + y, 2 * x + (1 - y), 2 * (1 - x) + (1 - y)
        to_x, to_y, me_dev, sibling = (1 - x, y, c), (x, 1 - y, c), (x, y, c), (x, y, 1 - c)

        def rows(core, off, n):
            return pl.ds(pl.multiple_of(core * R + off, H), n)

        def copy(k, chip, rws, to):
            blk = o_ref.at[chip, rws]
            return _remote(blk, blk, (send_sems.at[k], recv_sems.at[k]), to)

        piece, half_a, half_b = rows(c, 0, R), rows(c, 0, H), rows(c, H, H)
        o_ref[me] = s_ref[...].astype(BF16)
        sends = [copy(0, me, piece, to_x), copy(1, me, piece, to_y)]
        for cp in sends:
            cp.start()
        arrivals = [(0, chip_x, piece, (2, half_a, to_y)), (1, chip_y, piece, (3, half_b, to_x)),
                    (2, chip_d, half_a, None), (3, chip_d, half_b, None)]
        for k, chip, rws, onward in arrivals:
            copy(k, chip, rws, me_dev).wait_recv()
            if onward is not None:
                sends.append(copy(onward[0], chip, onward[1], onward[2]))
                sends[-1].start()
            sends.append(copy(4 + k, chip, rws, sibling))
            sends[-1].start()
        for k, chip, rws in [(4, chip_x, rows(1 - c, 0, R)), (5, chip_y, rows(1 - c, 0, R)),
                             (6, chip_d, rows(1 - c, 0, H)), (7, chip_d, rows(1 - c, H, H))]:
            copy(k, chip, rws, me_dev).wait_recv()
        for cp in sends:
            cp.wait_send()

    return pl.pallas_call(
        body, name=name, out_shape=jax.ShapeDtypeStruct((N_CHIPS, R2, C), BF16),
        in_specs=[pl.BlockSpec(memory_space=pltpu.VMEM)], out_specs=pl.BlockSpec(memory_space=pltpu.VMEM),
        scratch_shapes=[pltpu.SemaphoreType.DMA((8,)), pltpu.SemaphoreType.DMA((8,))],
        compiler_params=pltpu.CompilerParams(vmem_limit_bytes=VMEM_LIMIT),
    )(shard)


def _pair_exchange_sum(partial, name):
    _, R2, C = partial.shape
    R = R2 // 2

    def body(p_ref, send_ref, own_ref, mine_ref, sib_ref, loc_sems, send_sems, recv_sems):
        x, y, c, chips = _mesh_pos()
        me = 2 * x + y
        mine, theirs = _half_rows(c, R)
        order = [2 * cx + cy for cx, cy in chips] + [me]
        locs, pairs = [], []
        for i, k in enumerate(order):
            loc = pltpu.make_async_copy(p_ref.at[k, mine, :], mine_ref.at[i], loc_sems.at[i])
            pair = _remote(p_ref.at[k, theirs, :], sib_ref.at[i], (send_sems.at[i], recv_sems.at[i]), (x, y, 1 - c))
            loc.start()
            pair.start()
            locs.append(loc)
            pairs.append(pair)
        for i in range(N_CHIPS):
            locs[i].wait()
            pairs[i].wait_recv()
            total = mine_ref[i] + sib_ref[i]
            if i < 3:
                send_ref[i] = total.astype(BF16)
            else:
                own_ref[...] = total
        for pair in pairs:
            pair.wait_send()

    vm = pl.BlockSpec(memory_space=pltpu.VMEM)
    return pl.pallas_call(
        body, name=name, in_specs=[pl.BlockSpec(memory_space=pl.ANY)], out_specs=[vm, vm],
        out_shape=[jax.ShapeDtypeStruct((3, R, C), BF16), jax.ShapeDtypeStruct((R, C), F32)],
        scratch_shapes=[pltpu.VMEM((N_CHIPS, R, C), F32), pltpu.VMEM((N_CHIPS, R, C), F32),
                        pltpu.SemaphoreType.DMA((N_CHIPS,)), pltpu.SemaphoreType.DMA((N_CHIPS,)),
                        pltpu.SemaphoreType.DMA((N_CHIPS,))],
        compiler_params=pltpu.CompilerParams(vmem_limit_bytes=VMEM_LIMIT),
    )(partial)


def _allreduce_small(buf, name):
    rows, width = buf.shape
    h = rows // 2

    def body(b_ref, o_ref, sib_ref, pair_ref, in_ref, pair_sems, send_sems, recv_sems, fin_sems):
        x, y, c, chips = _mesh_pos()
        me = 2 * x + y
        mine, theirs = _half_rows(c, h)
        sibling = (x, y, 1 - c)
        pair = _remote(b_ref.at[theirs], sib_ref, (pair_sems.at[0], pair_sems.at[1]), sibling)
        pair.start()
        pair.wait()
        pair_ref[...] = b_ref[mine, :] + sib_ref[...]
        sends = []
        for j, (cx, cy) in enumerate(chips):
            cp = _remote(pair_ref, in_ref.at[j], (send_sems.at[j], recv_sems.at[j]), (cx, cy, c))
            cp.start()
            sends.append(cp)
        for cp in sends:
            cp.wait_recv()
        acc = None
        for k in range(N_CHIPS):
            term = jnp.where(me == k, pair_ref[...], in_ref[_peer_slot(k, x, y)])
            acc = term if acc is None else acc + term
        o_ref[mine, :] = acc
        fin = _remote(o_ref.at[mine], o_ref.at[mine], (fin_sems.at[0], fin_sems.at[1]), sibling)
        fin.start()
        fin.wait_send()
        _remote(o_ref.at[theirs], o_ref.at[theirs], (fin_sems.at[0], fin_sems.at[1]), sibling).wait_recv()
        for cp in sends:
            cp.wait_send()

    return pl.pallas_call(
        body, name=name, out_shape=jax.ShapeDtypeStruct((rows, width), F32),
        in_specs=[pl.BlockSpec(memory_space=pltpu.VMEM)], out_specs=pl.BlockSpec(memory_space=pltpu.VMEM),
        scratch_shapes=[pltpu.VMEM((h, width), F32), pltpu.VMEM((h, width), F32), pltpu.VMEM((3, h, width), F32),
                        pltpu.SemaphoreType.DMA((2,)), pltpu.SemaphoreType.DMA((3,)), pltpu.SemaphoreType.DMA((3,)),
                        pltpu.SemaphoreType.DMA((2,))],
        compiler_params=pltpu.CompilerParams(vmem_limit_bytes=VMEM_LIMIT),
    )(buf)


def _adamw_small(ws, gs, ms, vs):
    n = len(ws)
    c1 = 1.0 / (1.0 - ADAM_B1 ** ADAM_STEP)
    c2 = 1.0 / (1.0 - ADAM_B2 ** ADAM_STEP)

    def body(*refs):
        w_r, g_r, m_r, v_r = refs[0:n], refs[n:2 * n], refs[2 * n:3 * n], refs[3 * n:4 * n]
        d_r, nm_r, nv_r = refs[4 * n:5 * n], refs[5 * n:6 * n], refs[6 * n:7 * n]
        for i in range(n):
            gv = g_r[i][...]
            nm = ADAM_B1 * m_r[i][...] + (1.0 - ADAM_B1) * gv
            nv = ADAM_B2 * v_r[i][...] + (1.0 - ADAM_B2) * (gv * gv)
            nm_r[i][...] = nm
            nv_r[i][...] = nv
            d_r[i][...] = (-ADAM_LR) * ((nm * c1) / (jnp.sqrt(nv * c2) + ADAM_EPS) + ADAM_WD * w_r[i][...])

    vm = pl.BlockSpec(memory_space=pltpu.VMEM)
    sds = [jax.ShapeDtypeStruct(w.shape, F32) for w in ws]
    out = pl.pallas_call(body, name="adamw_small", in_specs=[vm] * (4 * n), out_specs=[vm] * (3 * n),
                         out_shape=sds * 3)(*ws, *gs, *ms, *vs)
    return out[0:n], out[n:2 * n], out[2 * n:3 * n]


_BIG = ("w_in", "w_rnn_proj", "w_attn_proj", "w_out", "w_up", "w_down", "w_ple_gate", "w_ple_proj")
_SMALL = ("g_mix", "conv_w", "conv_b", "w_rg", "b_rg", "w_ig", "b_ig", "lru_lambda", "q_gain", "k_gain", "sinks",
          "g_mlp", "g_ple")
_WEIGHTS = ("g_mix", "w_in", "conv_w", "conv_b", "w_rg", "b_rg", "w_ig", "b_ig", "lru_lambda", "w_rnn_proj",
            "q_gain", "k_gain", "sinks", "w_attn_proj", "w_out", "g_mlp", "w_up", "w_down", "g_ple", "w_ple_gate",
            "w_ple_proj")


def _pad_row(v):
    v = v.reshape(1, -1)
    return jnp.pad(v, ((0, 0), (0, D_MODEL - v.shape[1])))


def kernel(x, p, g_mix, w_in, conv_w, conv_b, w_rg, b_rg, w_ig, b_ig, lru_lambda, w_rnn_proj, q_gain, k_gain, sinks, w_attn_proj, w_out, g_mlp, w_up, w_down, g_ple, w_ple_gate, w_ple_proj, loss_target, m_g_mix, m_w_in, m_conv_w, m_conv_b, m_w_rg, m_b_rg, m_w_ig, m_b_ig, m_lru_lambda, m_w_rnn_proj, m_q_gain, m_k_gain, m_sinks, m_w_attn_proj, m_w_out, m_g_mlp, m_w_up, m_w_down, m_g_ple, m_w_ple_gate, m_w_ple_proj, v_g_mix, v_w_in, v_conv_w, v_conv_b, v_w_rg, v_b_rg, v_w_ig, v_b_ig, v_lru_lambda, v_w_rnn_proj, v_q_gain, v_k_gain, v_sinks, v_w_attn_proj, v_w_out, v_g_mlp, v_w_up, v_w_down, v_g_ple, v_w_ple_gate, v_w_ple_proj):
    w = dict(g_mix=g_mix, w_in=w_in, conv_w=conv_w, conv_b=conv_b, w_rg=w_rg, b_rg=b_rg, w_ig=w_ig, b_ig=b_ig,
             lru_lambda=lru_lambda, w_rnn_proj=w_rnn_proj, q_gain=q_gain, k_gain=k_gain, sinks=sinks,
             w_attn_proj=w_attn_proj, w_out=w_out, g_mlp=g_mlp, w_up=w_up, w_down=w_down, g_ple=g_ple,
             w_ple_gate=w_ple_gate, w_ple_proj=w_ple_proj)
    m = dict(g_mix=m_g_mix, w_in=m_w_in, conv_w=m_conv_w, conv_b=m_conv_b, w_rg=m_w_rg, b_rg=m_b_rg, w_ig=m_w_ig,
             b_ig=m_b_ig, lru_lambda=m_lru_lambda, w_rnn_proj=m_w_rnn_proj, q_gain=m_q_gain, k_gain=m_k_gain,
             sinks=m_sinks, w_attn_proj=m_w_attn_proj, w_out=m_w_out, g_mlp=m_g_mlp, w_up=m_w_up, w_down=m_w_down,
             g_ple=m_g_ple, w_ple_gate=m_w_ple_gate, w_ple_proj=m_w_ple_proj)
    v = dict(g_mix=v_g_mix, w_in=v_w_in, conv_w=v_conv_w, conv_b=v_conv_b, w_rg=v_w_rg, b_rg=v_b_rg, w_ig=v_w_ig,
             b_ig=v_b_ig, lru_lambda=v_lru_lambda, w_rnn_proj=v_w_rnn_proj, q_gain=v_q_gain, k_gain=v_k_gain,
             sinks=v_sinks, w_attn_proj=v_w_attn_proj, w_out=v_w_out, g_mlp=v_g_mlp, w_up=v_w_up, w_down=v_w_down,
             g_ple=v_g_ple, w_ple_gate=v_w_ple_gate, w_ple_proj=v_w_ple_proj)
    n_seq, S, _ = x.shape
    T = n_seq * S
    chip = 2 * lax.axis_index("x") + lax.axis_index("y")

    tm, tm_rnn = TM, TM_RNN
    xf, pf, tf = x.reshape(T, D_MODEL), p.reshape(T, PLE_DIM), loss_target.reshape(T, D_MODEL)
    first = lambda outs: [o[0] for o in outs]

    w_in_g = _gather_bf16(w["w_in"][0], "gather_w_in")
    wb = {name: w[name][0].astype(BF16) for name in _BIG if name != "w_in"}
    grp_mix, grp_mlp, grp_ple = ("w_rnn_proj", "w_attn_proj", "w_out"), ("w_up", "w_down"), ("w_ple_gate", "w_ple_proj")

    wb["conv_w"] = jnp.pad(conv_w[0], ((0, 16 - CONV_W), (0, 0)))

    cosf, sins = _rope_tables(S)
    ind_q, ind_qt = _indicator(D_MODEL)
    ind_k, ind_kt = _indicator(KV_W)
    wrg2 = _pair_blockdiag(w_rg[0]).astype(BF16)
    wig2 = _pair_blockdiag(w_ig[0]).astype(BF16)
    qg = jnp.tile(q_gain, (1, N_HEADS))
    kg = jnp.tile(k_gain, (1, N_KV))
    sk = sinks.reshape(N_HEADS)
    attn_c = (qg, kg, sk, cosf, sins, ind_q, ind_qt, ind_k, ind_kt, n_seq, S)

    (h0, xr, gr, zq, zk, zv, ga, gb), ph = _inproj_fwd(xf, g_mix, w_in_g, tm,
                                                     phases=[_ph_gather_send(wb[n]) for n in grp_mix + ("conv_w",)])
    g_small = first(ph)
    o, ph = _attn_fwd(zq, zk, zv, *attn_c,
                      phases=[_ph_gather_pass(g) for g in g_small]
                      + [_ph_gather_send(wb[n]) for n in ("w_up",) + grp_ple])
    g_small, (wu, wpg, wpp) = first(ph[:4]), first(ph[4:])
    cw_full = g_small[3][:, :CONV_W, :].transpose(1, 0, 2).reshape(CONV_W, D_MODEL)
    rnn_w = (cw_full, conv_b, wrg2, b_rg, wig2, b_ig, lru_lambda)
    (xc, h, *gates, ya), ph = _rnn_fwd(xr, gr, *rnn_w, n_seq, S, tm_rnn,
                               phases=[_ph_gather_pass(g) for g in (wu, wpg, wpp)]
                               + [_ph_gather_send(wb["w_down"])])
    (wu, wpg, wpp), wd = first(ph[:3]), ph[3][0]
    wr, wa, wo = (g.reshape(D_MODEL, D_MODEL) for g in g_small[:3])
    wpg = wpg.reshape(D_MODEL, D_MODEL)
    (x1, merged), ph = _merge_fwd(xf, ya, o, ga, gb, wr, wa, wo, tm, phases=[_ph_gather_pass(wd)])
    wd = ph[0][0].reshape(D_FF, D_MODEL)
    (x2, hm, u, act), _ = _mlp_fwd(x1, g_mlp, wu, wd, tm // 2)
    (loss_t, dx2, pb, de, hp, dtg, dg_ple), _ = _ple_loss(x2, pf, tf, g_ple, wpg, wpp, tm)

    chipmajor = lambda g: g.reshape(N_CHIPS, g.shape[-2] // N_CHIPS, g.shape[-1]) if g.ndim == 2 else g
    tmw = min(2 * tm, T)
    dw_pp = _wgrad(pb, de, "wgrad_ple_proj", False, D_MODEL, tmw)[0]
    part_ple = [chipmajor(_wgrad(hp, dtg, "wgrad_ple_gate", False, D_MODEL, tmw)[0]),
                dw_pp.reshape(PLE_DIM, N_CHIPS, D_MODEL // N_CHIPS).transpose(1, 0, 2)]
    (dx1, du, dg_mlp), ph = _mlp_bwd(dx2, u, x1, g_mlp, wu, wd, tm // 2, phases=[_ph_pair_send(g) for g in part_ple])
    send_ple, own_ple = _pair_sum(part_ple, first(ph), "pair_sum_ple")
    dw_down, ph = _wgrad(act, dx2, "wgrad_down", False, D_MODEL // 2, tmw, phases=[_ph_chip_send(s) for s in send_ple])
    red_ple = _chip_sum(own_ple, first(ph), "chip_sum_ple")
    part_mlp = [_wgrad(hm, du, "wgrad_up", True, D_MODEL, tmw)[0], chipmajor(dw_down)]
    (dga, dgb, dya, dyb, dyain, do), _ = _merge_bwd(dx1, ga, gb, ya, o, wr, wa, wo, tm)
    dw_rnn, ph_up = _wgrad(ya, dya, "wgrad_rnn_proj", False, D_MODEL, tmw, phases=[_ph_pair_send(part_mlp[0])])
    dw_attn, ph_down = _wgrad(o, dyb, "wgrad_attn_proj", False, D_MODEL, tmw, phases=[_ph_pair_send(part_mlp[1])])
    dw_out, ph = _wgrad(merged, dx1, "wgrad_out", False, D_MODEL, tmw, phases=[_ph_half_swap(r) for r in red_ple])
    red_ple = first(ph)
    send_mlp, own_mlp = _pair_sum(part_mlp, [ph_up[0][0], ph_down[0][0]], "pair_sum_mlp")
    part_mix = [chipmajor(dw_rnn), chipmajor(dw_attn), chipmajor(dw_out)]
    (dxr, dgr, vec, dwrg2, dwig2), ph = _rnn_bwd(
        dyain, xr, gr, xc, h, gates, cw_full, wrg2, wig2, lru_lambda, n_seq, S, tm_rnn,
        phases=[_ph_chip_send(s) for s in send_mlp] + [_ph_pair_send(g) for g in part_mix])
    red_mlp = _chip_sum(own_mlp, first(ph[:2]), "chip_sum_mlp")
    send_mix, own_mix = _pair_sum(part_mix, first(ph[2:]), "pair_sum_mix")
    (dq, dkc, dkp, dvc, dvp, dqg, dsk), ph = _attn_bwd(
        do, zq, zk, zv, *attn_c, phases=[_ph_half_swap(r) for r in red_mlp] + [_ph_chip_send(s) for s in send_mix])
    red_mlp = first(ph[:2])
    red_mix = _chip_sum(own_mix, first(ph[2:]), "chip_sum_mix")
    (dk, dv, dkg), _ = _kv_bwd(dkc, dkp, dvc, dvp, zk, kg, cosf, sins, ind_k, ind_kt, n_seq, S)
    dz_parts = [dxr, dgr, dq, dk, dv, dga, dgb]
    send_in, own_in = _pair_exchange_sum(_wgrad_in(h0, dz_parts, tm), "pair_sum_in")
    (grad_x, dg_mix), ph = _inproj_bwd(dz_parts, w_in_g, xf, g_mix, dx1, tm,
                                       phases=[_ph_half_swap(r) for r in red_mix] + [_ph_chip_send(send_in)])
    red_mix = first(ph[:3])
    red_in = _chip_sum([own_in], first(ph[3:]), "chip_sum_in")
    reduced = dict(zip(grp_ple + grp_mlp + grp_mix, red_ple + red_mlp + red_mix))
    grads = {
        "g_mix": dg_mix[0], "g_mlp": dg_mlp[0], "g_ple": dg_ple[0],
        "conv_w": vec[0:CONV_W], "conv_b": vec[4], "b_rg": vec[5], "b_ig": vec[6], "lru_lambda": vec[7],
        "w_rg": _pair_blockdiag_extract(dwrg2), "w_ig": _pair_blockdiag_extract(dwig2),
        "q_gain": dqg.reshape(N_HEADS, HEAD_DIM).sum(0), "k_gain": dkg.reshape(N_KV, HEAD_DIM).sum(0),
        "sinks": dsk.sum(1),
    }

    rows = [grads["conv_w"], _pad_row(grads["conv_b"]), _pad_row(grads["b_rg"]), _pad_row(grads["b_ig"]),
            _pad_row(grads["lru_lambda"]), _pad_row(grads["g_mix"]), _pad_row(grads["g_mlp"]),
            _pad_row(grads["g_ple"]), _pad_row(grads["q_gain"]), _pad_row(grads["k_gain"]), _pad_row(grads["sinks"]),
            _pad_row(loss_t[0:1, 0:1]), jnp.zeros((1, D_MODEL), F32)]
    vecs = jnp.concatenate(rows, axis=0)
    packed = jnp.concatenate([vecs.reshape(-1, LANES), grads["w_rg"].reshape(-1, LANES),
                              grads["w_ig"].reshape(-1, LANES)], axis=0)
    red = _allreduce_small(packed, "allreduce_small")
    nv = vecs.size // LANES
    rvec = red[0:nv].reshape(16, D_MODEL)
    loss = rvec[14, 0]
    nw = grads["w_rg"].size // LANES
    sg = {
        "conv_w": lax.dynamic_slice(rvec[0:CONV_W], (0, chip * (D_MODEL // N_CHIPS)), (CONV_W, D_MODEL // N_CHIPS)),
        "conv_b": rvec[4], "b_rg": rvec[5], "b_ig": rvec[6], "lru_lambda": rvec[7], "g_mix": rvec[8],
        "g_mlp": rvec[9], "g_ple": rvec[10], "q_gain": rvec[11, :HEAD_DIM], "k_gain": rvec[12, :HEAD_DIM],
        "sinks": rvec[13, :N_HEADS], "w_rg": red[nv:nv + nw], "w_ig": red[nv + nw:nv + 2 * nw],
    }
    sg = {k: sg[k].reshape(w[k].shape) for k in _SMALL}
    d_s, m_s, v_s = _adamw_small([w[k] for k in _SMALL], [sg[k] for k in _SMALL], [m[k] for k in _SMALL],
                                 [v[k] for k in _SMALL])
    grad, delta, new_m, new_v = dict(sg), dict(zip(_SMALL, d_s)), dict(zip(_SMALL, m_s)), dict(zip(_SMALL, v_s))

    for name in ("w_ple_proj", "w_up", "w_down", "w_rnn_proj", "w_attn_proj", "w_out", "w_ple_gate", "w_in"):
        shape = w[name].shape
        outs, ph = _adamw(w[name][0], reduced[name], m[name][0], v[name][0], "adamw_" + name, min(ADAMW_ROWS, shape[1] // 2),
                          phases=[_ph_half_swap(r) for r in red_in] if name == "w_ple_proj" else ())
        if name == "w_ple_proj":
            reduced["w_in"] = ph[0][0]
        grad[name], delta[name], new_m[name], new_v[name] = (a.reshape(shape) for a in outs)

    return (loss, grad_x.reshape(x.shape), *[grad[k] for k in _WEIGHTS], *[delta[k] for k in _WEIGHTS],
            *[new_m[k] for k in _WEIGHTS], *[new_v[k] for k in _WEIGHTS])
```

```python
import functools
import math

import numpy as np
import jax
import jax.numpy as jnp
from jax import lax
from jax.experimental import pallas as pl
from jax.experimental.pallas import tpu as pltpu

F32 = jnp.float32
BF16 = jnp.bfloat16

D_MODEL = 1024
N_HEADS = 16
N_KV = 4
HEAD_DIM = 64
KV_W = N_KV * HEAD_DIM
D_FF = 4096
PLE_DIM = 256
WINDOW = 128
CONV_W = 4
LRU_C = 8.0
NORM_EPS = 1e-6
ROPE_THETA = 10000.0
N_CHIPS = 4
IN_TOTAL = 5632
IN_BLK = IN_TOTAL // N_CHIPS
IN_SEGS = (0, 1024, 2048, 3072, 3328, 3584, 4608, 5632)

ADAM_LR = 0.001
ADAM_B1 = 0.9
ADAM_B2 = 0.999
ADAM_EPS = 1e-08
ADAM_WD = 0.01
ADAM_STEP = 10

LANES = 128
V7X_VMEM_BYTES = 64 * 1024 * 1024
VMEM_LIMIT = V7X_VMEM_BYTES - 8 * 1024 * 1024
MESH_ID = pl.DeviceIdType.MESH
TM, TM_RNN, ADAMW_ROWS = 512, 256, 256
ATTN_BLOCKS_PER_STEP = 2


def _dot(a, b):
    return jnp.dot(a, b, preferred_element_type=F32)


def _dot_nt(a, b):
    return lax.dot_general(a, b, (((1,), (1,)), ((), ())), preferred_element_type=F32)


def _dot_tn(a, b):
    return lax.dot_general(a, b, (((0,), (0,)), ((), ())), preferred_element_type=F32)


def _split_dot(x, ind):
    hi = x.astype(BF16)
    lo = (x - hi.astype(F32)).astype(BF16)
    return _dot(hi, ind) + _dot(lo, ind)


def _sigmoid(x):
    return 1.0 / (1.0 + jnp.exp(-x))


_GELU_C = math.sqrt(2.0 / math.pi)


def _gelu_and_grad(g):
    inner = _GELU_C * (g + 0.044715 * g * g * g)
    t = jnp.tanh(inner)
    gelu = 0.5 * g * (1.0 + t)
    dgelu = 0.5 * (1.0 + t) + 0.5 * g * (1.0 - t * t) * _GELU_C * (1.0 + 3.0 * 0.044715 * g * g)
    return gelu, dgelu


def _const(shape):
    nd = len(shape)
    return pl.BlockSpec(shape, lambda *_: (0,) * nd)


def _params(n_grid, vmem=VMEM_LIMIT):
    return pltpu.CompilerParams(dimension_semantics=("arbitrary",) * n_grid, vmem_limit_bytes=vmem)


def _rms_fwd(x, g):
    r = lax.rsqrt(jnp.mean(x * x, axis=-1, keepdims=True) + NORM_EPS)
    return (x * r) * g, r


def _rms_bwd(dy, x, r, g):
    dn = dy * g
    dx = r * dn - x * (r * r * r * jnp.mean(dn * x, axis=-1, keepdims=True))
    dg = jnp.sum(dy * (x * r), axis=0, keepdims=True)
    return dx, dg


def _seg_pieces(blk_lo, blk_hi):
    out = []
    for s in range(7):
        lo, hi = max(blk_lo, IN_SEGS[s]), min(blk_hi, IN_SEGS[s + 1])
        if lo < hi:
            out.append((s, lo - IN_SEGS[s], hi - IN_SEGS[s], lo - blk_lo))
    return out


def _mesh_pos():
    x, y, c = lax.axis_index("x"), lax.axis_index("y"), lax.axis_index("c")
    other_chips = [(1 - x, y), (x, 1 - y), (1 - x, 1 - y)]
    return x, y, c, other_chips


def _peer_slot(k, x, y):
    dx = jnp.bitwise_xor(k // 2, x)
    dy = jnp.bitwise_xor(k % 2, y)
    return jnp.maximum(dx + 2 * dy - 1, 0)


def _half_rows(c, R):
    return pl.ds(pl.multiple_of(c * R, R), R), pl.ds(pl.multiple_of((1 - c) * R, R), R)


def _remote(src, dst, sems, to):
    return pltpu.make_async_remote_copy(src_ref=src, dst_ref=dst, send_sem=sems[0], recv_sem=sems[1],
                                        device_id=to, device_id_type=MESH_ID)


class _Phase:
    def __init__(self, ins, inout, outs, n_remote, n_local, build):
        self.ins, self.inout, self.outs = list(ins), list(inout), list(outs)
        self.n_remote, self.n_local, self.build = n_remote, n_local, build


def _ph_gather_send(wb):
    R2, C = wb.shape
    R = R2 // 2

    def build(ins, outs, rsem, lsem):
        (w_ref,), (g_ref,) = ins, outs
        x, y, c, chips = _mesh_pos()
        me = 2 * x + y
        mine, _ = _half_rows(c, R)
        loc = [pltpu.make_async_copy(w_ref, g_ref.at[me], lsem(0))]
        outg = [_remote(w_ref.at[mine], g_ref.at[me, mine], rsem(j), (cx, cy, c)) for j, (cx, cy) in enumerate(chips)]
        inc = [functools.partial(_remote, w_ref.at[mine], g_ref.at[2 * cx + cy, mine], rsem(j), (x, y, c))
               for j, (cx, cy) in enumerate(chips)]
        return loc, outg, inc

    return _Phase([wb], [], [jax.ShapeDtypeStruct((N_CHIPS, R2, C), wb.dtype)], 3, 1, build)


def _ph_gather_pass(gath):
    _, R2, C = gath.shape
    R = R2 // 2

    def build(ins, outs, rsem, lsem):
        (g_ref,) = outs
        x, y, c, chips = _mesh_pos()
        mine, theirs = _half_rows(c, R)
        outg, inc = [], []
        for j, (cx, cy) in enumerate(chips):
            blk = g_ref.at[2 * cx + cy, mine]
            outg.append(_remote(blk, blk, rsem(j), (x, y, 1 - c)))
            got = g_ref.at[2 * cx + cy, theirs]
            inc.append(functools.partial(_remote, got, got, rsem(j), (x, y, c)))
        return [], outg, inc

    return _Phase([], [gath], [], 3, 0, build)


def _ph_pair_send(partial):
    _, R2, C = partial.shape
    R = R2 // 2

    def build(ins, outs, rsem, lsem):
        (p_ref,), (s_ref,) = ins, outs
        x, y, c, _ = _mesh_pos()
        _, theirs = _half_rows(c, R)
        src = p_ref.at[:, theirs, :]
        return ([], [_remote(src, s_ref, rsem(0), (x, y, 1 - c))],
                [functools.partial(_remote, src, s_ref, rsem(0), (x, y, c))])

    return _Phase([partial], [], [jax.ShapeDtypeStruct((N_CHIPS, R, C), F32)], 1, 0, build)


def _ph_chip_send(sendb):
    def build(ins, outs, rsem, lsem):
        (s_ref,), (r_ref,) = ins, outs
        x, y, c, chips = _mesh_pos()
        outg = [_remote(s_ref.at[j], r_ref.at[j], rsem(j), (cx, cy, c)) for j, (cx, cy) in enumerate(chips)]
        inc = [functools.partial(_remote, s_ref.at[j], r_ref.at[j], rsem(j), (x, y, c)) for j in range(3)]
        return [], outg, inc

    return _Phase([sendb], [], [jax.ShapeDtypeStruct(sendb.shape, sendb.dtype)], 3, 0, build)


def _ph_half_swap(red):
    R2, C = red.shape
    R = R2 // 2

    def build(ins, outs, rsem, lsem):
        (r_ref,) = outs
        x, y, c, _ = _mesh_pos()
        mine, theirs = _half_rows(c, R)
        return ([], [_remote(r_ref.at[mine], r_ref.at[mine], rsem(0), (x, y, 1 - c))],
                [functools.partial(_remote, r_ref.at[theirs], r_ref.at[theirs], rsem(0), (x, y, c))])

    return _Phase([], [red], [], 1, 0, build)


def _call(body, *, name, grid, in_specs, out_specs, out_shape, scratch_shapes=(), phases=()):
    single = not isinstance(out_specs, (list, tuple))
    out_specs = [out_specs] if single else list(out_specs)
    out_shape = [out_shape] if single else list(out_shape)
    n_in, n_out, n_scr = len(in_specs), len(out_specs), len(scratch_shapes)
    if not phases:
        call = pl.pallas_call(body, name=name, grid=grid, in_specs=in_specs, out_specs=out_specs,
                              out_shape=out_shape, scratch_shapes=list(scratch_shapes),
                              compiler_params=_params(len(grid)))
        return lambda *operands: (list(call(*operands)), [])

    ex_in, ex_out, aliases, spans = [], [], {}, []
    for ph in phases:
        i0, o0 = len(ex_in), len(ex_out)
        ex_in += ph.ins
        for a in ph.inout:
            aliases[n_in + len(ex_in)] = n_out + len(ex_out)
            ex_in.append(a)
            ex_out.append(jax.ShapeDtypeStruct(a.shape, a.dtype))
        ex_out += ph.outs
        spans.append((i0, len(ph.ins), o0, len(ex_out) - o0))
    n_remote = sum(ph.n_remote for ph in phases)
    n_local = max(sum(ph.n_local for ph in phases), 1)

    def wrapped(*refs):
        base_in, xin = refs[:n_in], refs[n_in:n_in + len(ex_in)]
        o0 = n_in + len(ex_in)
        base_out, xout = refs[o0:o0 + n_out], refs[o0 + n_out:o0 + n_out + len(ex_out)]
        scr = refs[o0 + n_out + len(ex_out):]
        send_sems, recv_sems, loc_sems = scr[n_scr:]
        first = functools.reduce(jnp.logical_and, [pl.program_id(i) == 0 for i in range(len(grid))])
        last = functools.reduce(jnp.logical_and, [pl.program_id(i) == grid[i] - 1 for i in range(len(grid))])

        def copies():
            out, r0, l0 = [], 0, 0
            for ph, (i0, ni, p0, no) in zip(phases, spans):
                rsem = lambda k, r0=r0: (send_sems.at[r0 + k], recv_sems.at[r0 + k])
                lsem = lambda k, l0=l0: loc_sems.at[l0 + k]
                out.append(ph.build(xin[i0:i0 + ni], xout[p0:p0 + no], rsem, lsem))
                r0, l0 = r0 + ph.n_remote, l0 + ph.n_local
            return out

        @pl.when(first)
        def _():
            for loc, outg, _ in copies():
                for cp in loc + outg:
                    cp.start()

        body(*base_in, *base_out, *scr[:n_scr])

        @pl.when(last)
        def _():
            for loc, outg, inc in copies():
                for make in inc:
                    make().wait_recv()
                for cp in outg:
                    cp.wait_send()
                for cp in loc:
                    cp.wait()

    hbm = pl.BlockSpec(memory_space=pl.ANY)
    call = pl.pallas_call(
        wrapped, name=name, grid=grid, in_specs=list(in_specs) + [hbm] * len(ex_in),
        out_specs=out_specs + [hbm] * len(ex_out), out_shape=out_shape + ex_out,
        scratch_shapes=list(scratch_shapes) + [pltpu.SemaphoreType.DMA((n_remote,)), pltpu.SemaphoreType.DMA((n_remote,)),
                                              pltpu.SemaphoreType.DMA((n_local,))],
        input_output_aliases=aliases, compiler_params=_params(len(grid)))

    def run(*operands):
        res = call(*operands, *ex_in)
        extra = res[n_out:]
        return list(res[:n_out]), [list(extra[p0:p0 + no]) for (_, _, p0, no) in spans]

    return run


def _inproj_fwd(x, g_mix, w_in, tm, phases=()):
    T = x.shape[0]
    widths = [IN_SEGS[i + 1] - IN_SEGS[i] for i in range(7)]

    def body(x_ref, g_ref, w_ref, h_ref, *z_refs):
        h, _ = _rms_fwd(x_ref[...], g_ref[...])
        hb = h.astype(BF16)
        h_ref[...] = hb
        for j in range(N_CHIPS):
            zj = _dot(hb, w_ref[j])
            for s, lo, hi, off in _seg_pieces(j * IN_BLK, (j + 1) * IN_BLK):
                z_refs[s][:, lo:hi] = zj[:, off:off + hi - lo]

    return _call(
        body, phases=phases, name="inproj_fwd", grid=(T // tm,),
        in_specs=[pl.BlockSpec((tm, D_MODEL), lambda i: (i, 0)), _const((1, D_MODEL)),
                  _const((N_CHIPS, D_MODEL, IN_BLK))],
        out_specs=[pl.BlockSpec((tm, D_MODEL), lambda i: (i, 0))]
        + [pl.BlockSpec((tm, w), lambda i: (i, 0)) for w in widths],
        out_shape=[jax.ShapeDtypeStruct((T, D_MODEL), BF16)]
        + [jax.ShapeDtypeStruct((T, w), F32) for w in widths],
    )(x, g_mix, w_in)


def _inproj_bwd(dz_parts, w_in, x, g_mix, dx1, tm, phases=()):
    T = x.shape[0]
    widths = [IN_SEGS[i + 1] - IN_SEGS[i] for i in range(7)]

    def body(*refs):
        p_refs = refs[:7]
        w_ref, x_ref, g_ref, dx1_ref, gx_ref, dg_ref, dz_ref = refs[7:]

        @pl.when(pl.program_id(0) == 0)
        def _():
            dg_ref[...] = jnp.zeros_like(dg_ref)

        for s in range(7):
            dz_ref[:, IN_SEGS[s]:IN_SEGS[s + 1]] = p_refs[s][...]
        dh = jnp.zeros((tm, D_MODEL), F32)
        for j in range(N_CHIPS):
            dh = dh + _dot_nt(dz_ref[:, j * IN_BLK:(j + 1) * IN_BLK], w_ref[j])
        xv = x_ref[...]
        g = g_ref[...]
        _, r = _rms_fwd(xv, g)
        dx, dg = _rms_bwd(dh, xv, r, g)
        gx_ref[...] = dx1_ref[...] + dx
        dg_ref[...] += dg

    row = lambda w: pl.BlockSpec((tm, w), lambda i: (i, 0))
    return _call(
        body, phases=phases, name="inproj_bwd", grid=(T // tm,),
        in_specs=[row(w) for w in widths]
        + [_const((N_CHIPS, D_MODEL, IN_BLK)), row(D_MODEL), _const((1, D_MODEL)), row(D_MODEL)],
        out_specs=[row(D_MODEL), _const((1, D_MODEL))],
        out_shape=[jax.ShapeDtypeStruct((T, D_MODEL), F32), jax.ShapeDtypeStruct((1, D_MODEL), F32)],
        scratch_shapes=[pltpu.VMEM((tm, IN_TOTAL), BF16)],
    )(*dz_parts, w_in, x, g_mix, dx1)


def _wgrad_in(h0, dz_parts, tm):
    T = h0.shape[0]
    widths = [IN_SEGS[i + 1] - IN_SEGS[i] for i in range(7)]

    def body(*refs):
        h_ref, p_refs, o_ref, acc_ref, sems = refs[0], refs[1:8], refs[8], refs[9], refs[10]
        t = pl.program_id(0)
        last = T // tm - 1

        @pl.when(t == 0)
        def _():
            acc_ref[...] = jnp.zeros_like(acc_ref)

        def accumulate(j):
            for s, lo, hi, off in _seg_pieces(j * IN_BLK, (j + 1) * IN_BLK):
                acc_ref[j, :, off:off + hi - lo] += _dot_tn(h_ref[...], p_refs[s][:, lo:hi])

        @pl.when(t < last)
        def _():
            for j in range(N_CHIPS):
                accumulate(j)

        @pl.when(t == last)
        def _():
            copies = [pltpu.make_async_copy(acc_ref.at[j], o_ref.at[j], sems.at[j]) for j in range(N_CHIPS)]
            for j in range(N_CHIPS):
                accumulate(j)
                copies[j].start()
            for cp in copies:
                cp.wait()

    row = lambda w: pl.BlockSpec((tm, w), lambda i: (i, 0))
    return pl.pallas_call(
        body, name="wgrad_in", grid=(T // tm,), in_specs=[row(D_MODEL)] + [row(w) for w in widths],
        out_specs=pl.BlockSpec(memory_space=pl.ANY),
        out_shape=jax.ShapeDtypeStruct((N_CHIPS, D_MODEL, IN_BLK), F32),
        scratch_shapes=[pltpu.VMEM((N_CHIPS, D_MODEL, IN_BLK), F32), pltpu.SemaphoreType.DMA((N_CHIPS,))],
        compiler_params=_params(1),
    )(h0, *dz_parts)


def _wgrad(a, g, name, blocked, cn, tm, phases=()):
    T, K = a.shape
    N = g.shape[1]
    nb = N // cn

    def body(a_ref, g_ref, o_ref):
        @pl.when(pl.program_id(1) == 0)
        def _():
            o_ref[...] = jnp.zeros_like(o_ref)

        o_ref[...] += _dot_tn(a_ref[...].astype(BF16), g_ref[...].astype(BF16))

    if blocked:
        out_spec = pl.BlockSpec((None, K, cn), lambda j, t: (j, 0, 0))
        out_shape = jax.ShapeDtypeStruct((nb, K, cn), F32)
    else:
        out_spec = pl.BlockSpec((K, cn), lambda j, t: (0, j))
        out_shape = jax.ShapeDtypeStruct((K, N), F32)
    outs, extra = _call(
        body, phases=phases, name=name, grid=(nb, T // tm),
        in_specs=[pl.BlockSpec((tm, K), lambda j, t: (t, 0)), pl.BlockSpec((tm, cn), lambda j, t: (t, j))],
        out_specs=out_spec, out_shape=out_shape,
    )(a, g)
    return outs[0], extra


def _shift_down(x, prev8, sft, row, row8, tm):
    xs = pltpu.roll(x, sft, 0)
    top = jnp.where(row8 < sft, pltpu.roll(prev8, sft, 0), xs[0:8])
    return jnp.concatenate([top, xs[8:]], axis=0)


def _shift_up(x, next8, sft, row8, tm):
    xs = pltpu.roll(x, tm - sft, 0)
    bot = jnp.where(row8 >= 8 - sft, pltpu.roll(next8, 8 - sft, 0), xs[tm - 8:tm])
    return jnp.concatenate([xs[0:tm - 8], bot], axis=0)


def _conv_fwd(x, prev8, cw_ref, cb, row, row8, tm):
    xc = cb + cw_ref[CONV_W - 1:CONV_W, :] * x
    for sft in range(1, CONV_W):
        j = CONV_W - 1 - sft
        xc = xc + cw_ref[j:j + 1, :] * _shift_down(x, prev8, sft, row, row8, tm)
    return xc


def _blockdiag_dot(xb, w_ref, transpose):
    outs = []
    for b in range(D_MODEL // LANES):
        xs = xb[:, b * LANES:(b + 1) * LANES]
        outs.append(_dot_nt(xs, w_ref[b]) if transpose else _dot(xs, w_ref[b]))
    return jnp.concatenate(outs, axis=1)


def _softplus_neg(lam):
    e = jnp.exp(-jnp.abs(lam))
    u = 1.0 + e
    log1p_e = jnp.where(u == 1.0, e, jnp.log(u) * (e / (u - 1.0)))
    sp = jnp.maximum(-lam, 0.0) + log1p_e
    return sp, -_sigmoid(-lam)


def _lru_gates(xc, wrg_ref, brg, wig_ref, big, sp):
    xcb = xc.astype(BF16)
    r = _sigmoid(_blockdiag_dot(xcb, wrg_ref, False) + brg)
    i = _sigmoid(_blockdiag_dot(xcb, wig_ref, False) + big)
    log_a = (-LRU_C) * r * sp
    a = jnp.exp(log_a)
    t = jnp.tanh(log_a)
    one_m_a2 = (-2.0) * t / (1.0 - t)
    mult = jnp.sqrt(one_m_a2)
    return xcb, r, i, a, mult


def _scan_down(a, b, row, tm):
    d = 1
    while d < tm:
        if d < 8:
            keep = row >= d
            a_s = jnp.where(keep, pltpu.roll(a, d, 0), 1.0)
            b_s = jnp.where(keep, pltpu.roll(b, d, 0), 0.0)
            b = a * b_s + b
            a = a * a_s
        else:
            b = jnp.concatenate([b[:d], a[d:] * b[:-d] + b[d:]], axis=0)
            a = jnp.concatenate([a[:d], a[d:] * a[:-d]], axis=0)
        d *= 2
    return a, b


def _scan_up(c, b, row, tm):
    d = 1
    while d < tm:
        if d < 8:
            keep = row < tm - d
            c_s = jnp.where(keep, pltpu.roll(c, tm - d, 0), 1.0)
            b_s = jnp.where(keep, pltpu.roll(b, tm - d, 0), 0.0)
            b = c * b_s + b
            c = c * c_s
        else:
            b = jnp.concatenate([c[:-d] * b[d:] + b[:-d], b[-d:]], axis=0)
            c = jnp.concatenate([c[:-d] * c[d:], c[-d:]], axis=0)
        d *= 2
    return c, b


def _rnn_fwd(xr, gr, conv_w, conv_b, wrg2, b_rg, wig2, b_ig, lam, n_seq, S, tm, phases=()):
    T = xr.shape[0]
    nt = S // tm
    W = D_MODEL

    def body(xr_ref, gr_ref, cw_ref, cb_ref, wrg_ref, brg_ref, wig_ref, big_ref, lam_ref,
             xc_ref, h_ref, r_ref, i_ref, a_ref, mult_ref, ya_ref, px_ref, ph_ref):
        @pl.when(pl.program_id(1) == 0)
        def _():
            px_ref[...] = jnp.zeros_like(px_ref)
            ph_ref[...] = jnp.zeros_like(ph_ref)

        row = lax.broadcasted_iota(jnp.int32, (tm, W), 0)
        row8 = lax.broadcasted_iota(jnp.int32, (8, W), 0)
        x = xr_ref[...]
        xc = _conv_fwd(x, px_ref[...], cw_ref, cb_ref[...], row, row8, tm)
        sp, _ = _softplus_neg(lam_ref[...])
        _, r, i, a, mult = _lru_gates(xc, wrg_ref, brg_ref[...], wig_ref, big_ref[...], sp)
        r_ref[...], i_ref[...], a_ref[...], mult_ref[...] = r, i, a, mult
        bterm = mult * (i * xc)
        acum, hloc = _scan_down(a, bterm, row, tm)
        h = hloc + acum * ph_ref[7:8, :]
        h_ref[...] = h
        xc_ref[...] = xc
        gelu, _ = _gelu_and_grad(gr_ref[...])
        ya_ref[...] = (h * gelu).astype(BF16)
        px_ref[...] = xr_ref[tm - 8:tm, :]
        ph_ref[...] = h_ref[tm - 8:tm, :]

    tile = pl.BlockSpec((tm, W), lambda s, t: (s * nt + t, 0))
    return _call(
        body, phases=phases, name="rnn_fwd", grid=(n_seq, nt),
        in_specs=[tile, tile, _const((CONV_W, W)), _const((1, W)), _const((8, LANES, LANES)), _const((1, W)),
                  _const((8, LANES, LANES)), _const((1, W)), _const((1, W))],
        out_specs=[tile] * 7,
        out_shape=[jax.ShapeDtypeStruct((T, W), F32)] * 6 + [jax.ShapeDtypeStruct((T, W), BF16)],
        scratch_shapes=[pltpu.VMEM((8, W), F32), pltpu.VMEM((8, W), F32)],
    )(xr, gr, conv_w, conv_b, wrg2, b_rg, wig2, b_ig, lam)


def _rnn_bwd(dya, xr, gr, xc, h, gates, conv_w, wrg2, wig2, lam, n_seq, S, tm, phases=()):
    T = xr.shape[0]
    nt = S // tm
    W = D_MODEL
    nb8 = tm // 8

    def body(dya_ref, xr_ref, gr_ref, xc_ref, h_ref, r_ref, i_ref, a_ref, mult_ref, xprev_ref, hprev_ref, cw_ref,
             wrg_ref, wig_ref, lam_ref, dxr_ref, dgr_ref, vec_ref, dwrg_ref, dwig_ref, cg_ref, ndxc_ref, tmp_ref):
        s, ti = pl.program_id(0), pl.program_id(1)

        @pl.when((s == 0) & (ti == 0))
        def _():
            vec_ref[...] = jnp.zeros_like(vec_ref)
            dwrg_ref[...] = jnp.zeros_like(dwrg_ref)
            dwig_ref[...] = jnp.zeros_like(dwig_ref)

        @pl.when(ti == 0)
        def _():
            cg_ref[...] = jnp.zeros_like(cg_ref)
            ndxc_ref[...] = jnp.zeros_like(ndxc_ref)

        first = ti == nt - 1
        row = lax.broadcasted_iota(jnp.int32, (tm, W), 0)
        row8 = lax.broadcasted_iota(jnp.int32, (8, W), 0)
        x = xr_ref[...]
        xc = xc_ref[...]
        hv = h_ref[...]
        xprev = jnp.where(first, 0.0, xprev_ref[...])
        hprev = jnp.where(first, 0.0, hprev_ref[...])
        sp, dsp_dlam = _softplus_neg(lam_ref[...])
        xcb = xc.astype(BF16)
        r, i, a, mult = r_ref[...], i_ref[...], a_ref[...], mult_ref[...]

        gelu, dgelu = _gelu_and_grad(gr_ref[...])
        dya_v = dya_ref[...]
        dgr_ref[...] = (dya_v * hv * dgelu).astype(BF16)
        dh = dya_v * gelu
        c = jnp.where(row < tm - 1, pltpu.roll(a, tm - 1, 0), 1.0)
        ccum, gloc = _scan_up(c, dh, row, tm)
        G = gloc + ccum * cg_ref[0:1, :]
        tmp_ref[...] = a * G
        cg_ref[...] = tmp_ref[0:8, :]

        h_m1 = _shift_down(hv, hprev, 1, row, row8, tm)
        ixc = i * xc
        dixc = G * mult
        dlog_a = (G * h_m1) * a - (G * ixc) * (a * a / mult)
        dr = dlog_a * ((-LRU_C) * sp)
        di = dixc * xc
        drg = dr * r * (1.0 - r)
        dig = di * i * (1.0 - i)
        vec_ref[7:8, :] += jnp.sum(dlog_a * ((-LRU_C) * r), axis=0, keepdims=True) * dsp_dlam
        vec_ref[5:6, :] += jnp.sum(drg, axis=0, keepdims=True)
        vec_ref[6:7, :] += jnp.sum(dig, axis=0, keepdims=True)
        drgb = drg.astype(BF16)
        digb = dig.astype(BF16)
        dxc = dixc * i + _blockdiag_dot(drgb, wrg_ref, True) + _blockdiag_dot(digb, wig_ref, True)
        for b in range(W // LANES):
            sl = slice(b * LANES, (b + 1) * LANES)
            dwrg_ref[b] += _dot_tn(xcb[:, sl], drgb[:, sl])
            dwig_ref[b] += _dot_tn(xcb[:, sl], digb[:, sl])

        vec_ref[4:5, :] += jnp.sum(dxc, axis=0, keepdims=True)
        vec_ref[3:4, :] += jnp.sum(dxc * x, axis=0, keepdims=True)
        dxr = cw_ref[CONV_W - 1:CONV_W, :] * dxc
        nxt = ndxc_ref[...]
        for sft in range(1, CONV_W):
            j = CONV_W - 1 - sft
            vec_ref[j:j + 1, :] += jnp.sum(dxc * _shift_down(x, xprev, sft, row, row8, tm), axis=0, keepdims=True)
            dxr = dxr + cw_ref[j:j + 1, :] * _shift_up(dxc, nxt, sft, row8, tm)
        dxr_ref[...] = dxr.astype(BF16)
        tmp_ref[...] = dxc
        ndxc_ref[...] = tmp_ref[0:8, :]

    rev = lambda s, t: (s * nt + nt - 1 - t, 0)
    tile = pl.BlockSpec((tm, W), rev)
    prev8 = pl.BlockSpec((8, W), lambda s, t: (jnp.maximum((s * nt + nt - 1 - t) * nb8 - 1, 0), 0))
    return _call(
        body, phases=phases, name="rnn_bwd", grid=(n_seq, nt),
        in_specs=[tile] * 9 + [prev8, prev8, _const((CONV_W, W)), _const((8, LANES, LANES)),
                               _const((8, LANES, LANES)), _const((1, W))],
        out_specs=[tile, tile, _const((16, W)), _const((8, LANES, LANES)), _const((8, LANES, LANES))],
        out_shape=[jax.ShapeDtypeStruct((T, W), BF16), jax.ShapeDtypeStruct((T, W), BF16),
                   jax.ShapeDtypeStruct((16, W), F32), jax.ShapeDtypeStruct((8, LANES, LANES), F32),
                   jax.ShapeDtypeStruct((8, LANES, LANES), F32)],
        scratch_shapes=[pltpu.VMEM((8, W), F32), pltpu.VMEM((8, W), F32), pltpu.VMEM((tm, W), F32)],
    )(dya, xr, gr, xc, h, *gates, xr, h, conv_w, wrg2, wig2, lam)


def _head_swap(t, lane):
    w = t.shape[1]
    return jnp.where(lane % HEAD_DIM < HEAD_DIM // 2, pltpu.roll(t, w - HEAD_DIM // 2, 1),
                     pltpu.roll(t, HEAD_DIM // 2, 1))


def _qk_prep(t, gain, cosf, sins, ind, indt, lane):
    ms = _split_dot(t * t, ind) * (1.0 / HEAD_DIM)
    rstd = _split_dot(lax.rsqrt(ms + NORM_EPS), indt)
    tn = (t * rstd) * gain
    return tn * cosf + _head_swap(tn, lane) * sins, rstd


def _qk_prep_bwd(dy, t, rstd, gain, cosf, sins, ind, indt, lane):
    dtn = dy * cosf + _head_swap(dy * sins, lane)
    dgain = jnp.sum(dtn * (t * rstd), axis=0, keepdims=True)
    dn = dtn * gain
    m = _split_dot(_split_dot(dn * t, ind), indt) * (1.0 / HEAD_DIM)
    return rstd * dn - t * (rstd * rstd * rstd * m), dgain


def _attn_mask_t(blk_idx):
    ci = lax.broadcasted_iota(jnp.int32, (2 * WINDOW, WINDOW), 0)
    qi = lax.broadcasted_iota(jnp.int32, (2 * WINDOW, WINDOW), 1)
    diff = WINDOW + qi - ci
    return (diff >= 0) & (diff < WINDOW) & ((ci >= WINDOW) | (blk_idx > 0))


def _stack_heads(t, kvh, lo):
    parts = []
    for i in (2 * kvh, 2 * kvh + 1):
        tp = t[:, i * LANES:(i + 1) * LANES]
        parts += [jnp.where(lo, tp, 0.0), jnp.where(lo, 0.0, tp)]
    return jnp.concatenate(parts, axis=0).astype(BF16)


def _unstack_heads(ts, lo):
    w = WINDOW
    return jnp.where(lo, ts[0:w], ts[w:2 * w]), jnp.where(lo, ts[2 * w:3 * w], ts[3 * w:4 * w])


def _dup_head(t, kvh, lo2):
    m = kvh // 2
    t2 = t[:, m * LANES:(m + 1) * LANES]
    t2r = pltpu.roll(t2, HEAD_DIM, 1)
    return (jnp.where(lo2, t2, t2r) if kvh % 2 == 0 else jnp.where(lo2, t2r, t2)).astype(BF16)


def _fold_head(ts, kvh, lo2):
    tot = ts + pltpu.roll(ts, HEAD_DIM, 1)
    own = lo2 if kvh % 2 == 0 else ~lo2
    return jnp.where(own, tot, 0.0)


KEY_CHUNKS = tuple(slice(i * 64, (i + 1) * 64) for i in range(2 * WINDOW // 64))


def _fold8(x, op):
    return op(x.reshape(x.shape[0] // 8, 8, x.shape[1]), axis=0)


def _softmax_stats(s_ref, b, cols, sink):
    m8 = None
    for c in KEY_CHUNKS:
        t = _fold8(s_ref[b, c, cols], jnp.max)
        m8 = t if m8 is None else jnp.maximum(m8, t)
    mx = jnp.maximum(jnp.max(m8, axis=0, keepdims=True), sink)
    d8 = None
    for c in KEY_CHUNKS:
        t = _fold8(jnp.exp(s_ref[b, c, cols] - mx), jnp.sum)
        d8 = t if d8 is None else d8 + t
    es = jnp.exp(sink - mx)
    inv = 1.0 / (jnp.sum(d8, axis=0, keepdims=True) + es)
    return mx, inv, es * inv


def _attn_fwd(q, k, v, qg, kg, sinks, cosf, sins, ind_q, ind_qt, ind_k, ind_kt, n_seq, S, phases=()):
    T = q.shape[0]
    nblk = S // WINDOW
    W = D_MODEL

    def body(sink_ref, q_ref, k_ref, v_ref, qg_ref, kg_ref, cos_ref, sin_ref, iq_ref, iqt_ref, ik_ref, ikt_ref,
             o_ref, kc_ref, vc_ref, s_ref, p_ref, qs_ref, kd_ref, vd_ref):
        @pl.when(pl.program_id(1) == 0)
        def _():
            kc_ref[...] = jnp.zeros_like(kc_ref)
            vc_ref[...] = jnp.zeros_like(vc_ref)

        lane = lax.broadcasted_iota(jnp.int32, (WINDOW, W), 1)
        lo = lane[:, :LANES] < HEAD_DIM
        lo2 = lax.broadcasted_iota(jnp.int32, (2 * WINDOW, LANES), 1) < HEAD_DIM

        def one_block(h):
            n = pl.program_id(1) * bps + h
            rows = slice(h * WINDOW, (h + 1) * WINDOW)
            cosf, sinv = jnp.tile(cos_ref[rows, :], (1, W // LANES)), jnp.tile(sin_ref[rows, :], (1, W // LANES))
            qr, _ = _qk_prep(q_ref[rows, :], qg_ref[...], cosf, sinv, iq_ref[...], iqt_ref[...], lane)
            kr, _ = _qk_prep(k_ref[rows, :], kg_ref[...], cosf[:, :KV_W], sinv[:, :KV_W], ik_ref[...], ikt_ref[...],
                             lane[:, :KV_W])
            kc_ref[WINDOW:2 * WINDOW, :] = kr
            vc_ref[WINDOW:2 * WINDOW, :] = v_ref[rows, :]
            kc, vc = kc_ref[...], vc_ref[...]
            mask = jnp.tile(_attn_mask_t(n), (1, 4))
            qr = qr * HEAD_DIM ** -0.5
            for kvh in range(N_KV):
                qs_ref[h, kvh] = _stack_heads(qr, kvh, lo)
                kd_ref[h, kvh] = _dup_head(kc, kvh, lo2)
                vd_ref[h, kvh] = _dup_head(vc, kvh, lo2)

            def scores(kvh):
                s_ref[h, kvh % 2] = jnp.where(mask, _dot_nt(kd_ref[h, kvh], qs_ref[h, kvh]), -1e30)

            def softmax(kvh):
                sb, pb = s_ref.at[h], p_ref.at[h]
                b = kvh % 2
                for r in range(4):
                    cols = slice(r * WINDOW, (r + 1) * WINDOW)
                    mx, inv, _ = _softmax_stats(sb, b, cols, sink_ref[4 * kvh + r])
                    for c in KEY_CHUNKS:
                        pb[b, c, cols] = (jnp.exp(sb[b, c, cols] - mx) * inv).astype(BF16)

            def output(kvh):
                o0, o1 = _unstack_heads(_dot_tn(p_ref[h, kvh % 2], vd_ref[h, kvh]), lo)
                o_ref[rows, (2 * kvh) * LANES:(2 * kvh + 1) * LANES] = o0.astype(BF16)
                o_ref[rows, (2 * kvh + 1) * LANES:(2 * kvh + 2) * LANES] = o1.astype(BF16)

            scores(0)
            for kvh in range(N_KV):
                if kvh + 1 < N_KV:
                    scores(kvh + 1)
                softmax(kvh)
                output(kvh)
            kc_ref[0:WINDOW, :] = kr
            vc_ref[0:WINDOW, :] = v_ref[rows, :]

        for h in range(bps):
            one_block(h)

    bps = ATTN_BLOCKS_PER_STEP
    rows_step = bps * WINDOW
    blk = lambda w: pl.BlockSpec((rows_step, w), lambda s, n: (s * (nblk // bps) + n, 0))
    pos = pl.BlockSpec((rows_step, LANES), lambda s, n: (n, 0))
    outs, extra = _call(
        body, phases=phases, name="attn_fwd", grid=(n_seq, nblk // bps),
        in_specs=[pl.BlockSpec(memory_space=pltpu.SMEM), blk(W), blk(KV_W), blk(KV_W), _const((1, W)),
                  _const((1, KV_W)), pos, pos, _const((W, LANES)), _const((LANES, W)), _const((KV_W, LANES)),
                  _const((LANES, KV_W))],
        out_specs=blk(W), out_shape=jax.ShapeDtypeStruct((T, W), BF16),
        scratch_shapes=[pltpu.VMEM((2 * WINDOW, KV_W), F32), pltpu.VMEM((2 * WINDOW, KV_W), F32),
                        pltpu.VMEM((bps, 2, 2 * WINDOW, 4 * WINDOW), F32),
                        pltpu.VMEM((bps, 2, 2 * WINDOW, 4 * WINDOW), BF16),
                        pltpu.VMEM((bps, N_KV, 4 * WINDOW, LANES), BF16),
                        pltpu.VMEM((bps, N_KV, 2 * WINDOW, LANES), BF16),
                        pltpu.VMEM((bps, N_KV, 2 * WINDOW, LANES), BF16)],
    )(sinks, q, k, v, qg, kg, cosf, sins, ind_q, ind_qt, ind_k, ind_kt)
    return outs[0], extra


def _attn_bwd(do, q, k, v, qg, kg, sinks, cosf, sins, ind_q, ind_qt, ind_k, ind_kt, n_seq, S, phases=()):
    T = q.shape[0]
    nblk = S // WINDOW
    W = D_MODEL

    def body(sink_ref, do_ref, q_ref, k_ref, v_ref, qg_ref, kg_ref, cos_ref, sin_ref, iq_ref, iqt_ref, ik_ref,
             ikt_ref, dq_ref, dkc_ref, dkp_ref, dvc_ref, dvp_ref, dqg_ref, dsk_ref, kc_ref, vc_ref, dqr_ref,
             dk_ref, dv_ref, s_ref, dp_ref, p_ref, ds_ref, qs_ref, dos_ref, kd_ref, vd_ref):
        s_id, n_step = pl.program_id(0), pl.program_id(1)

        @pl.when((s_id == 0) & (n_step == 0))
        def _():
            dqg_ref[...] = jnp.zeros_like(dqg_ref)
            dsk_ref[...] = jnp.zeros_like(dsk_ref)

        @pl.when(n_step == 0)
        def _():
            kc_ref[...] = jnp.zeros_like(kc_ref)
            vc_ref[...] = jnp.zeros_like(vc_ref)

        lane = lax.broadcasted_iota(jnp.int32, (WINDOW, W), 1)
        lane_k = lane[:, :KV_W]
        lo = lane[:, :LANES] < HEAD_DIM
        lo2 = lax.broadcasted_iota(jnp.int32, (2 * WINDOW, LANES), 1) < HEAD_DIM
        scale = HEAD_DIM ** -0.5

        def one_block(h):
            n = n_step * bps + h
            rows = slice(h * WINDOW, (h + 1) * WINDOW)
            cosf, sinv = jnp.tile(cos_ref[rows, :], (1, W // LANES)), jnp.tile(sin_ref[rows, :], (1, W // LANES))
            qv = q_ref[rows, :]
            qr, q_rstd = _qk_prep(qv, qg_ref[...], cosf, sinv, iq_ref[...], iqt_ref[...], lane)
            kr, _ = _qk_prep(k_ref[rows, :], kg_ref[...], cosf[:, :KV_W], sinv[:, :KV_W], ik_ref[...], ikt_ref[...],
                             lane_k)
            kc_ref[WINDOW:2 * WINDOW, :] = kr
            vc_ref[WINDOW:2 * WINDOW, :] = v_ref[rows, :]
            kc, vc = kc_ref[...], vc_ref[...]
            dov = do_ref[rows, :]
            mask = jnp.tile(_attn_mask_t(n), (1, 4))
            qr = qr * scale
            dk_ref[h] = jnp.zeros((2 * WINDOW, KV_W), F32)
            dv_ref[h] = jnp.zeros((2 * WINDOW, KV_W), F32)
            for kvh in range(N_KV):
                qs_ref[h, kvh] = _stack_heads(qr, kvh, lo)
                dos_ref[h, kvh] = _stack_heads(dov, kvh, lo)
                kd_ref[h, kvh] = _dup_head(kc, kvh, lo2)
                vd_ref[h, kvh] = _dup_head(vc, kvh, lo2)
            sb, dpb, pb, dsb = s_ref.at[h], dp_ref.at[h], p_ref.at[h], ds_ref.at[h]

            def scores(kvh):
                b = kvh % 2
                sb[b] = jnp.where(mask, _dot_nt(kd_ref[h, kvh], qs_ref[h, kvh]), -1e30)
                dpb[b] = _dot_nt(vd_ref[h, kvh], dos_ref[h, kvh])

            def softmax(kvh):
                b = kvh % 2
                for r in range(4):
                    cols = slice(r * WINDOW, (r + 1) * WINDOW)
                    head = 4 * kvh + r
                    mx, inv, ps = _softmax_stats(sb, b, cols, sink_ref[head])
                    g8 = None
                    for c in KEY_CHUNKS:
                        t = _fold8(jnp.exp(sb[b, c, cols] - mx) * dpb[b, c, cols], jnp.sum)
                        g8 = t if g8 is None else g8 + t
                    dd = jnp.sum(g8, axis=0, keepdims=True) * inv
                    for c in KEY_CHUNKS:
                        p = jnp.exp(sb[b, c, cols] - mx) * inv
                        pb[b, c, cols] = p.astype(BF16)
                        dsb[b, c, cols] = (p * (dpb[b, c, cols] - dd)).astype(BF16)
                    dsk_ref[head:head + 1, :] -= ps * dd

            def grads(kvh):
                m, b = kvh // 2, kvh % 2
                dq0, dq1 = _unstack_heads(_dot_tn(dsb[b], kd_ref[h, kvh]) * scale, lo)
                dqr_ref[h, :, (2 * kvh) * LANES:(2 * kvh + 1) * LANES] = dq0
                dqr_ref[h, :, (2 * kvh + 1) * LANES:(2 * kvh + 2) * LANES] = dq1
                dk_ref[h, :, m * LANES:(m + 1) * LANES] += _fold_head(_dot(dsb[b], qs_ref[h, kvh]), kvh, lo2)
                dv_ref[h, :, m * LANES:(m + 1) * LANES] += _fold_head(_dot(pb[b], dos_ref[h, kvh]), kvh, lo2)

            scores(0)
            for kvh in range(N_KV):
                if kvh + 1 < N_KV:
                    scores(kvh + 1)
                softmax(kvh)
                grads(kvh)
            dq, dqg = _qk_prep_bwd(dqr_ref[h], qv, q_rstd, qg_ref[...], cosf, sinv, iq_ref[...], iqt_ref[...], lane)
            dq_ref[rows, :] = dq.astype(BF16)
            dqg_ref[...] += dqg
            dkp_ref[rows, :] = dk_ref[h, 0:WINDOW, :]
            dkc_ref[rows, :] = dk_ref[h, WINDOW:2 * WINDOW, :]
            dvp_ref[rows, :] = dv_ref[h, 0:WINDOW, :]
            dvc_ref[rows, :] = dv_ref[h, WINDOW:2 * WINDOW, :]
            kc_ref[0:WINDOW, :] = kr
            vc_ref[0:WINDOW, :] = v_ref[rows, :]

        for h in range(bps):
            one_block(h)

    bps = ATTN_BLOCKS_PER_STEP
    rows_step = bps * WINDOW
    blk = lambda w: pl.BlockSpec((rows_step, w), lambda s, n: (s * (nblk // bps) + n, 0))
    pos = pl.BlockSpec((rows_step, LANES), lambda s, n: (n, 0))
    kv_out = jax.ShapeDtypeStruct((T, KV_W), F32)
    stage = lambda dt: pltpu.VMEM((bps, 2, 2 * WINDOW, 4 * WINDOW), dt)
    return _call(
        body, phases=phases, name="attn_bwd", grid=(n_seq, nblk // bps),
        in_specs=[pl.BlockSpec(memory_space=pltpu.SMEM), blk(W), blk(W), blk(KV_W), blk(KV_W), _const((1, W)),
                  _const((1, KV_W)), pos, pos, _const((W, LANES)), _const((LANES, W)), _const((KV_W, LANES)),
                  _const((LANES, KV_W))],
        out_specs=[blk(W), blk(KV_W), blk(KV_W), blk(KV_W), blk(KV_W), _const((1, W)), _const((N_HEADS, LANES))],
        out_shape=[jax.ShapeDtypeStruct((T, W), BF16), kv_out, kv_out, kv_out, kv_out,
                   jax.ShapeDtypeStruct((1, W), F32), jax.ShapeDtypeStruct((N_HEADS, LANES), F32)],
        scratch_shapes=[pltpu.VMEM((2 * WINDOW, KV_W), F32), pltpu.VMEM((2 * WINDOW, KV_W), F32),
                        pltpu.VMEM((bps, WINDOW, W), F32), pltpu.VMEM((bps, 2 * WINDOW, KV_W), F32),
                        pltpu.VMEM((bps, 2 * WINDOW, KV_W), F32), stage(F32), stage(F32), stage(BF16), stage(BF16),
                        pltpu.VMEM((bps, N_KV, 4 * WINDOW, LANES), BF16),
                        pltpu.VMEM((bps, N_KV, 4 * WINDOW, LANES), BF16),
                        pltpu.VMEM((bps, N_KV, 2 * WINDOW, LANES), BF16),
                        pltpu.VMEM((bps, N_KV, 2 * WINDOW, LANES), BF16)],
    )(sinks, do, q, k, v, qg, kg, cosf, sins, ind_q, ind_qt, ind_k, ind_kt)


def _kv_bwd(dkc, dkp, dvc, dvp, k, kg, cosf, sins, ind_k, ind_kt, n_seq, S, phases=()):
    T = k.shape[0]
    nblk = S // WINDOW

    def body(dkc_ref, dkp_ref, dvc_ref, dvp_ref, k_ref, kg_ref, cos_ref, sin_ref, ik_ref, ikt_ref,
             dk_ref, dv_ref, dkg_ref):
        s_id, n = pl.program_id(0), pl.program_id(1)

        @pl.when((s_id == 0) & (n == 0))
        def _():
            dkg_ref[...] = jnp.zeros_like(dkg_ref)

        has_next = n < nblk - 1
        lane = lax.broadcasted_iota(jnp.int32, (WINDOW, KV_W), 1)
        dkr = dkc_ref[...] + jnp.where(has_next, dkp_ref[...], 0.0)
        dv_ref[...] = (dvc_ref[...] + jnp.where(has_next, dvp_ref[...], 0.0)).astype(BF16)
        cosf, sinv = jnp.tile(cos_ref[...], (1, KV_W // LANES)), jnp.tile(sin_ref[...], (1, KV_W // LANES))
        kv = k_ref[...]
        _, rstd = _qk_prep(kv, kg_ref[...], cosf, sinv, ik_ref[...], ikt_ref[...], lane)
        dk, dkg = _qk_prep_bwd(dkr, kv, rstd, kg_ref[...], cosf, sinv, ik_ref[...], ikt_ref[...], lane)
        dk_ref[...] = dk.astype(BF16)
        dkg_ref[...] += dkg

    cur = pl.BlockSpec((WINDOW, KV_W), lambda s, n: (s * nblk + n, 0))
    nxt = pl.BlockSpec((WINDOW, KV_W), lambda s, n: (s * nblk + jnp.minimum(n + 1, nblk - 1), 0))
    pos = pl.BlockSpec((WINDOW, LANES), lambda s, n: (n, 0))
    return _call(
        body, phases=phases, name="kv_bwd", grid=(n_seq, nblk),
        in_specs=[cur, nxt, cur, nxt, cur, _const((1, KV_W)), pos, pos, _const((KV_W, LANES)),
                  _const((LANES, KV_W))],
        out_specs=[cur, cur, _const((1, KV_W))],
        out_shape=[jax.ShapeDtypeStruct((T, KV_W), BF16), jax.ShapeDtypeStruct((T, KV_W), BF16),
                   jax.ShapeDtypeStruct((1, KV_W), F32)],
    )(dkc, dkp, dvc, dvp, k, kg, cosf, sins, ind_k, ind_kt)


def _merge_fwd(x, ya, o, ga, gb, w_rnn, w_attn, w_out, tm, phases=()):
    T = x.shape[0]
    W = D_MODEL

    def body(x_ref, ya_ref, o_ref, ga_ref, gb_ref, wr_ref, wa_ref, wo_ref, x1_ref, mg_ref):
        y_a = _dot(ya_ref[...], wr_ref[...])
        y_b = _dot(o_ref[...], wa_ref[...])
        mg = (_sigmoid(ga_ref[...]) * y_a + _sigmoid(gb_ref[...]) * y_b).astype(BF16)
        mg_ref[...] = mg
        x1_ref[...] = x_ref[...] + _dot(mg, wo_ref[...])

    row = pl.BlockSpec((tm, W), lambda i: (i, 0))
    sq = _const((W, W))
    return _call(
        body, phases=phases, name="merge_fwd", grid=(T // tm,),
        in_specs=[row, row, row, row, row, sq, sq, sq], out_specs=[row, row],
        out_shape=[jax.ShapeDtypeStruct((T, W), F32), jax.ShapeDtypeStruct((T, W), BF16)],
    )(x, ya, o, ga, gb, w_rnn, w_attn, w_out)


def _merge_bwd(dx1, ga, gb, ya, o, w_rnn, w_attn, w_out, tm, phases=()):
    T = dx1.shape[0]
    W = D_MODEL

    def body(dx1_ref, ga_ref, gb_ref, ya_ref, o_ref, wr_ref, wa_ref, wo_ref,
             dga_ref, dgb_ref, dya_ref, dyb_ref, dyain_ref, do_ref):
        dm = _dot_nt(dx1_ref[...].astype(BF16), wo_ref[...])
        sa = _sigmoid(ga_ref[...])
        sb = _sigmoid(gb_ref[...])
        dga_ref[...] = (dm * _dot(ya_ref[...], wr_ref[...]) * (sa * (1.0 - sa))).astype(BF16)
        dgb_ref[...] = (dm * _dot(o_ref[...], wa_ref[...]) * (sb * (1.0 - sb))).astype(BF16)
        dya = (dm * sa).astype(BF16)
        dyb = (dm * sb).astype(BF16)
        dya_ref[...] = dya
        dyb_ref[...] = dyb
        dyain_ref[...] = _dot_nt(dya, wr_ref[...])
        do_ref[...] = _dot_nt(dyb, wa_ref[...])

    row = pl.BlockSpec((tm, W), lambda i: (i, 0))
    sq = _const((W, W))
    b16 = jax.ShapeDtypeStruct((T, W), BF16)
    f32 = jax.ShapeDtypeStruct((T, W), F32)
    return _call(
        body, phases=phases, name="merge_bwd", grid=(T // tm,),
        in_specs=[row, row, row, row, row, sq, sq, sq], out_specs=[row] * 6,
        out_shape=[b16, b16, b16, b16, f32, f32],
    )(dx1, ga, gb, ya, o, w_rnn, w_attn, w_out)


def _mlp_fwd(x1, g_mlp, w_up, w_down, tm, phases=()):
    T = x1.shape[0]
    W = D_MODEL

    def body(x_ref, g_ref, wu_ref, wd_ref, x2_ref, hm_ref, u_ref, act_ref):
        xv = x_ref[...]
        hm, _ = _rms_fwd(xv, g_ref[...])
        hmb = hm.astype(BF16)
        hm_ref[...] = hmb
        for j in range(N_CHIPS):
            u = _dot(hmb, wu_ref[j])
            u_ref[:, j * W:(j + 1) * W] = u
            ru = jnp.maximum(u, 0.0)
            act_ref[:, j * W:(j + 1) * W] = (ru * ru).astype(BF16)
        x2_ref[...] = xv + _dot(act_ref[...], wd_ref[...])

    row = lambda w: pl.BlockSpec((tm, w), lambda i: (i, 0))
    return _call(
        body, phases=phases, name="mlp_fwd", grid=(T // tm,),
        in_specs=[row(W), _const((1, W)), _const((N_CHIPS, W, W)), _const((D_FF, W))],
        out_specs=[row(W), row(W), row(D_FF), row(D_FF)],
        out_shape=[jax.ShapeDtypeStruct((T, W), F32), jax.ShapeDtypeStruct((T, W), BF16),
                   jax.ShapeDtypeStruct((T, D_FF), F32), jax.ShapeDtypeStruct((T, D_FF), BF16)],
    )(x1, g_mlp, w_up, w_down)


def _mlp_bwd(dx2, u, x1, g_mlp, w_up, w_down, tm, phases=()):
    T = x1.shape[0]
    W = D_MODEL

    def body(dx2_ref, u_ref, x_ref, g_ref, wu_ref, wd_ref, dx1_ref, du_ref, dg_ref):
        @pl.when(pl.program_id(0) == 0)
        def _():
            dg_ref[...] = jnp.zeros_like(dg_ref)

        dx2 = dx2_ref[...]
        dact = _dot_nt(dx2.astype(BF16), wd_ref[...])
        du_ref[...] = (dact * (2.0 * jnp.maximum(u_ref[...], 0.0))).astype(BF16)
        dhm = jnp.zeros((tm, W), F32)
        for j in range(N_CHIPS):
            dhm = dhm + _dot_nt(du_ref[:, j * W:(j + 1) * W], wu_ref[j])
        xv = x_ref[...]
        g = g_ref[...]
        _, r = _rms_fwd(xv, g)
        dx, dg = _rms_bwd(dhm, xv, r, g)
        dx1_ref[...] = dx2 + dx
        dg_ref[...] += dg

    row = lambda w: pl.BlockSpec((tm, w), lambda i: (i, 0))
    return _call(
        body, phases=phases, name="mlp_bwd", grid=(T // tm,),
        in_specs=[row(W), row(D_FF), row(W), _const((1, W)), _const((N_CHIPS, W, W)), _const((D_FF, W))],
        out_specs=[row(W), row(D_FF), _const((1, W))],
        out_shape=[jax.ShapeDtypeStruct((T, W), F32), jax.ShapeDtypeStruct((T, D_FF), BF16),
                   jax.ShapeDtypeStruct((1, W), F32)],
    )(dx2, u, x1, g_mlp, w_up, w_down)


def _ple_loss(x2, p, target, g_ple, w_gate, w_proj, tm, phases=()):
    T = x2.shape[0]
    W = D_MODEL
    cw = W // N_CHIPS

    def body(x_ref, p_ref, t_ref, g_ref, wg_ref, wp_ref, loss_ref, dx2_ref, pb_ref, de_ref, hp_ref, dtg_ref, dg_ref):
        @pl.when(pl.program_id(0) == 0)
        def _():
            dg_ref[...] = jnp.zeros_like(dg_ref)
            loss_ref[...] = jnp.zeros_like(loss_ref)

        xv = x_ref[...]
        g = g_ref[...]
        pb = p_ref[...].astype(BF16)
        pb_ref[...] = pb
        e = jnp.concatenate([_dot(pb, wp_ref[j]) for j in range(N_CHIPS)], axis=1)
        hp, r = _rms_fwd(xv, g)
        hpb = hp.astype(BF16)
        hp_ref[...] = hpb
        sg = _sigmoid(_dot(hpb, wg_ref[...]))
        diff = (xv + e * sg) - t_ref[...]
        loss_ref[...] += jnp.sum(diff * diff) * (0.5 / W)
        dx3 = diff * (1.0 / W)
        de_ref[...] = (dx3 * sg).astype(BF16)
        dtg = (dx3 * e * (sg * (1.0 - sg))).astype(BF16)
        dtg_ref[...] = dtg
        dx, dg = _rms_bwd(_dot_nt(dtg, wg_ref[...]), xv, r, g)
        dx2_ref[...] = dx3 + dx
        dg_ref[...] += dg

    row = lambda w: pl.BlockSpec((tm, w), lambda i: (i, 0))
    b16 = lambda w: jax.ShapeDtypeStruct((T, w), BF16)
    return _call(
        body, phases=phases, name="ple_loss", grid=(T // tm,),
        in_specs=[row(W), row(PLE_DIM), row(W), _const((1, W)), _const((W, W)), _const((N_CHIPS, PLE_DIM, cw))],
        out_specs=[_const((8, LANES)), row(W), row(PLE_DIM), row(W), row(W), row(W), _const((1, W))],
        out_shape=[jax.ShapeDtypeStruct((8, LANES), F32), jax.ShapeDtypeStruct((T, W), F32), b16(PLE_DIM),
                   b16(W), b16(W), b16(W), jax.ShapeDtypeStruct((1, W), F32)],
    )(x2, p, target, g_ple, w_gate, w_proj)


def _adamw(w, g, m, v, name, tr, phases=()):
    R, C = w.shape
    c1 = 1.0 / (1.0 - ADAM_B1 ** ADAM_STEP)
    c2 = 1.0 / (1.0 - ADAM_B2 ** ADAM_STEP)

    def body(w_ref, g_ref, m_ref, v_ref, go_ref, d_ref, nm_ref, nv_ref):
        gv = g_ref[...]
        go_ref[...] = gv
        nm = ADAM_B1 * m_ref[...] + (1.0 - ADAM_B1) * gv
        nv = ADAM_B2 * v_ref[...] + (1.0 - ADAM_B2) * (gv * gv)
        nm_ref[...] = nm
        nv_ref[...] = nv
        d_ref[...] = (-ADAM_LR) * ((nm * c1) / (jnp.sqrt(nv * c2) + ADAM_EPS) + ADAM_WD * w_ref[...])

    row = pl.BlockSpec((tr, C), lambda i: (i, 0))
    sds = jax.ShapeDtypeStruct((R, C), F32)
    return _call(
        body, phases=phases, name=name, grid=(R // tr,), in_specs=[row] * 4, out_specs=[row] * 4,
        out_shape=[sds] * 4,
    )(w, g, m, v)


def _indicator(width):
    ind = np.zeros((width, LANES), np.float32)
    ind[np.arange(width), np.arange(width) // HEAD_DIM] = 1.0
    return jnp.asarray(ind, BF16), jnp.asarray(ind.T, BF16)


def _rope_tables(S):
    inv = ROPE_THETA ** (-jnp.arange(0, HEAD_DIM, 2, dtype=F32) / HEAD_DIM)
    ang = jnp.arange(S, dtype=F32)[:, None] * inv[None, :]
    cos, sin = jnp.cos(ang), jnp.sin(ang)
    cosf = jnp.tile(jnp.concatenate([cos, cos], axis=1), (1, LANES // HEAD_DIM))
    sins = jnp.tile(jnp.concatenate([-sin, sin], axis=1), (1, LANES // HEAD_DIM))
    return cosf, sins


def _pair_blockdiag(w):
    w4 = w.reshape(8, 2, HEAD_DIM, HEAD_DIM)
    eye = jnp.eye(2, dtype=w.dtype)
    return jnp.einsum("bpij,pq->bpiqj", w4, eye).reshape(8, LANES, LANES)


def _pair_blockdiag_extract(g):
    g5 = g.reshape(8, 2, HEAD_DIM, 2, HEAD_DIM)
    return jnp.stack([g5[:, 0, :, 0, :], g5[:, 1, :, 1, :]], axis=1).reshape(16, HEAD_DIM, HEAD_DIM)


def _pair_sum(parts, sibs, name):
    n = len(parts)
    dims = [(p.shape[1] // 2, p.shape[2]) for p in parts]

    def body(*refs):
        p_r, s_r, send_r, own_r, mine_r, sem = (refs[0:n], refs[n:2 * n], refs[2 * n:3 * n], refs[3 * n:4 * n],
                                                refs[4 * n:5 * n], refs[5 * n])
        x, y, c, chips = _mesh_pos()
        me = 2 * x + y
        loads = []
        for i, (R, _) in enumerate(dims):
            mine, _ = _half_rows(c, R)
            cp = pltpu.make_async_copy(p_r[i].at[:, mine, :], mine_r[i], sem.at[i])
            cp.start()
            loads.append(cp)
        for i in range(n):
            loads[i].wait()
            for j, (cx, cy) in enumerate(chips):
                k = 2 * cx + cy
                send_r[i][j] = (mine_r[i][k] + s_r[i][k]).astype(BF16)
            own_r[i][...] = mine_r[i][me] + s_r[i][me]

    vm = pl.BlockSpec(memory_space=pltpu.VMEM)
    out = pl.pallas_call(
        body, name=name, in_specs=[pl.BlockSpec(memory_space=pl.ANY)] * n + [vm] * n, out_specs=[vm] * (2 * n),
        out_shape=[jax.ShapeDtypeStruct((3, R, C), BF16) for R, C in dims]
        + [jax.ShapeDtypeStruct((R, C), F32) for R, C in dims],
        scratch_shapes=[pltpu.VMEM((N_CHIPS, R, C), F32) for R, C in dims] + [pltpu.SemaphoreType.DMA((n,))],
        compiler_params=pltpu.CompilerParams(vmem_limit_bytes=VMEM_LIMIT),
    )(*parts, *sibs)
    return out[:n], out[n:]


def _chip_sum(owns, recvs, name):
    n = len(owns)
    dims = [o.shape for o in owns]

    def body(*refs):
        own_r, recv_r, red_r, stage_r, sem = refs[0:n], refs[n:2 * n], refs[2 * n:3 * n], refs[3 * n:4 * n], refs[4 * n]
        x, y, c, _ = _mesh_pos()
        me = 2 * x + y
        stores = []
        for i, (R, _) in enumerate(dims):
            for k_me in range(N_CHIPS):

                @pl.when(me == k_me)
                def _():
                    acc = None
                    for k in range(N_CHIPS):
                        slot = ((k // 2) ^ (k_me // 2)) + 2 * ((k % 2) ^ (k_me % 2)) - 1
                        term = own_r[i][...] if k == k_me else recv_r[i][slot].astype(F32)
                        acc = term if acc is None else acc + term
                    stage_r[i][...] = acc

            mine, _ = _half_rows(c, R)
            cp = pltpu.make_async_copy(stage_r[i], red_r[i].at[mine, :], sem.at[i])
            cp.start()
            stores.append(cp)
        for cp in stores:
            cp.wait()

    vm = pl.BlockSpec(memory_space=pltpu.VMEM)
    return pl.pallas_call(
        body, name=name, in_specs=[vm] * (2 * n), out_specs=[pl.BlockSpec(memory_space=pl.ANY)] * n,
        out_shape=[jax.ShapeDtypeStruct((2 * R, C), F32) for R, C in dims],
        scratch_shapes=[pltpu.VMEM((R, C), F32) for R, C in dims] + [pltpu.SemaphoreType.DMA((n,))],
        compiler_params=pltpu.CompilerParams(vmem_limit_bytes=VMEM_LIMIT),
    )(*owns, *recvs)


def _gather_bf16(shard, name):
    R2, C = shard.shape
    R = R2 // 2
    H = R // 2

    def body(s_ref, o_ref, send_sems, recv_sems):
        x, y, c, _ = _mesh_pos()
        me, chip_x, chip_y, chip_d = 2 * x + y, 2 * (1 - x) + y, 2 * x + (1 - y), 2 * (1 - x) + (1 - y)
        to_x, to_y, me_dev, sibling = (1 - x, y, c), (x, 1 - y, c), (x, y, c), (x, y, 1 - c)

        def rows(core, off, n):
            return pl.ds(pl.multiple_of(core * R + off, H), n)

        def copy(k, chip, rws, to):
            blk = o_ref.at[chip, rws]
            return _remote(blk, blk, (send_sems.at[k], recv_sems.at[k]), to)

        piece, half_a, half_b = rows(c, 0, R), rows(c, 0, H), rows(c, H, H)
        o_ref[me] = s_ref[...].astype(BF16)
        sends = [copy(0, me, piece, to_x), copy(1, me, piece, to_y)]
        for cp in sends:
            cp.start()
        arrivals = [(0, chip_x, piece, (2, half_a, to_y)), (1, chip_y, piece, (3, half_b, to_x)),
                    (2, chip_d, half_a, None), (3, chip_d, half_b, None)]
        for k, chip, rws, onward in arrivals:
            copy(k, chip, rws, me_dev).wait_recv()
            if onward is not None:
                sends.append(copy(onward[0], chip, onward[1], onward[2]))
                sends[-1].start()
            sends.append(copy(4 + k, chip, rws, sibling))
            sends[-1].start()
        for k, chip, rws in [(4, chip_x, rows(1 - c, 0, R)), (5, chip_y, rows(1 - c, 0, R)),
                             (6, chip_d, rows(1 - c, 0, H)), (7, chip_d, rows(1 - c, H, H))]:
            copy(k, chip, rws, me_dev).wait_recv()
        for cp in sends:
            cp.wait_send()

    return pl.pallas_call(
        body, name=name, out_shape=jax.ShapeDtypeStruct((N_CHIPS, R2, C), BF16),
        in_specs=[pl.BlockSpec(memory_space=pltpu.VMEM)], out_specs=pl.BlockSpec(memory_space=pltpu.VMEM),
        scratch_shapes=[pltpu.SemaphoreType.DMA((8,)), pltpu.SemaphoreType.DMA((8,))],
        compiler_params=pltpu.CompilerParams(vmem_limit_bytes=VMEM_LIMIT),
    )(shard)


def _pair_exchange_sum(partial, name):
    _, R2, C = partial.shape
    R = R2 // 2

    def body(p_ref, send_ref, own_ref, mine_ref, sib_ref, loc_sems, send_sems, recv_sems):
        x, y, c, chips = _mesh_pos()
        me = 2 * x + y
        mine, theirs = _half_rows(c, R)
        order = [2 * cx + cy for cx, cy in chips] + [me]
        locs, pairs = [], []
        for i, k in enumerate(order):
            loc = pltpu.make_async_copy(p_ref.at[k, mine, :], mine_ref.at[i], loc_sems.at[i])
            pair = _remote(p_ref.at[k, theirs, :], sib_ref.at[i], (send_sems.at[i], recv_sems.at[i]), (x, y, 1 - c))
            loc.start()
            pair.start()
            locs.append(loc)
            pairs.append(pair)
        for i in range(N_CHIPS):
            locs[i].wait()
            pairs[i].wait_recv()
            total = mine_ref[i] + sib_ref[i]
            if i < 3:
                send_ref[i] = total.astype(BF16)
            else:
                own_ref[...] = total
        for pair in pairs:
            pair.wait_send()

    vm = pl.BlockSpec(memory_space=pltpu.VMEM)
    return pl.pallas_call(
        body, name=name, in_specs=[pl.BlockSpec(memory_space=pl.ANY)], out_specs=[vm, vm],
        out_shape=[jax.ShapeDtypeStruct((3, R, C), BF16), jax.ShapeDtypeStruct((R, C), F32)],
        scratch_shapes=[pltpu.VMEM((N_CHIPS, R, C), F32), pltpu.VMEM((N_CHIPS, R, C), F32),
                        pltpu.SemaphoreType.DMA((N_CHIPS,)), pltpu.SemaphoreType.DMA((N_CHIPS,)),
                        pltpu.SemaphoreType.DMA((N_CHIPS,))],
        compiler_params=pltpu.CompilerParams(vmem_limit_bytes=VMEM_LIMIT),
    )(partial)


def _allreduce_small(buf, name):
    rows, width = buf.shape
    h = rows // 2

    def body(b_ref, o_ref, sib_ref, pair_ref, in_ref, pair_sems, send_sems, recv_sems, fin_sems):
        x, y, c, chips = _mesh_pos()
        me = 2 * x + y
        mine, theirs = _half_rows(c, h)
        sibling = (x, y, 1 - c)
        pair = _remote(b_ref.at[theirs], sib_ref, (pair_sems.at[0], pair_sems.at[1]), sibling)
        pair.start()
        pair.wait()
        pair_ref[...] = b_ref[mine, :] + sib_ref[...]
        sends = []
        for j, (cx, cy) in enumerate(chips):
            cp = _remote(pair_ref, in_ref.at[j], (send_sems.at[j], recv_sems.at[j]), (cx, cy, c))
            cp.start()
            sends.append(cp)
        for cp in sends:
            cp.wait_recv()
        acc = None
        for k in range(N_CHIPS):
            term = jnp.where(me == k, pair_ref[...], in_ref[_peer_slot(k, x, y)])
            acc = term if acc is None else acc + term
        o_ref[mine, :] = acc
        fin = _remote(o_ref.at[mine], o_ref.at[mine], (fin_sems.at[0], fin_sems.at[1]), sibling)
        fin.start()
        fin.wait_send()
        _remote(o_ref.at[theirs], o_ref.at[theirs], (fin_sems.at[0], fin_sems.at[1]), sibling).wait_recv()
        for cp in sends:
            cp.wait_send()

    return pl.pallas_call(
        body, name=name, out_shape=jax.ShapeDtypeStruct((rows, width), F32),
        in_specs=[pl.BlockSpec(memory_space=pltpu.VMEM)], out_specs=pl.BlockSpec(memory_space=pltpu.VMEM),
        scratch_shapes=[pltpu.VMEM((h, width), F32), pltpu.VMEM((h, width), F32), pltpu.VMEM((3, h, width), F32),
                        pltpu.SemaphoreType.DMA((2,)), pltpu.SemaphoreType.DMA((3,)), pltpu.SemaphoreType.DMA((3,)),
                        pltpu.SemaphoreType.DMA((2,))],
        compiler_params=pltpu.CompilerParams(vmem_limit_bytes=VMEM_LIMIT),
    )(buf)


def _adamw_small(ws, gs, ms, vs):
    n = len(ws)
    c1 = 1.0 / (1.0 - ADAM_B1 ** ADAM_STEP)
    c2 = 1.0 / (1.0 - ADAM_B2 ** ADAM_STEP)

    def body(*refs):
        w_r, g_r, m_r, v_r = refs[0:n], refs[n:2 * n], refs[2 * n:3 * n], refs[3 * n:4 * n]
        d_r, nm_r, nv_r = refs[4 * n:5 * n], refs[5 * n:6 * n], refs[6 * n:7 * n]
        for i in range(n):
            gv = g_r[i][...]
            nm = ADAM_B1 * m_r[i][...] + (1.0 - ADAM_B1) * gv
            nv = ADAM_B2 * v_r[i][...] + (1.0 - ADAM_B2) * (gv * gv)
            nm_r[i][...] = nm
            nv_r[i][...] = nv
            d_r[i][...] = (-ADAM_LR) * ((nm * c1) / (jnp.sqrt(nv * c2) + ADAM_EPS) + ADAM_WD * w_r[i][...])

    vm = pl.BlockSpec(memory_space=pltpu.VMEM)
    sds = [jax.ShapeDtypeStruct(w.shape, F32) for w in ws]
    out = pl.pallas_call(body, name="adamw_small", in_specs=[vm] * (4 * n), out_specs=[vm] * (3 * n),
                         out_shape=sds * 3)(*ws, *gs, *ms, *vs)
    return out[0:n], out[n:2 * n], out[2 * n:3 * n]


_BIG = ("w_in", "w_rnn_proj", "w_attn_proj", "w_out", "w_up", "w_down", "w_ple_gate", "w_ple_proj")
_SMALL = ("g_mix", "conv_w", "conv_b", "w_rg", "b_rg", "w_ig", "b_ig", "lru_lambda", "q_gain", "k_gain", "sinks",
          "g_mlp", "g_ple")
_WEIGHTS = ("g_mix", "w_in", "conv_w", "conv_b", "w_rg", "b_rg", "w_ig", "b_ig", "lru_lambda", "w_rnn_proj",
            "q_gain", "k_gain", "sinks", "w_attn_proj", "w_out", "g_mlp", "w_up", "w_down", "g_ple", "w_ple_gate",
            "w_ple_proj")


def _pad_row(v):
    v = v.reshape(1, -1)
    return jnp.pad(v, ((0, 0), (0, D_MODEL - v.shape[1])))


def kernel(x, p, g_mix, w_in, conv_w, conv_b, w_rg, b_rg, w_ig, b_ig, lru_lambda, w_rnn_proj, q_gain, k_gain, sinks, w_attn_proj, w_out, g_mlp, w_up, w_down, g_ple, w_ple_gate, w_ple_proj, loss_target, m_g_mix, m_w_in, m_conv_w, m_conv_b, m_w_rg, m_b_rg, m_w_ig, m_b_ig, m_lru_lambda, m_w_rnn_proj, m_q_gain, m_k_gain, m_sinks, m_w_attn_proj, m_w_out, m_g_mlp, m_w_up, m_w_down, m_g_ple, m_w_ple_gate, m_w_ple_proj, v_g_mix, v_w_in, v_conv_w, v_conv_b, v_w_rg, v_b_rg, v_w_ig, v_b_ig, v_lru_lambda, v_w_rnn_proj, v_q_gain, v_k_gain, v_sinks, v_w_attn_proj, v_w_out, v_g_mlp, v_w_up, v_w_down, v_g_ple, v_w_ple_gate, v_w_ple_proj):
    w = dict(g_mix=g_mix, w_in=w_in, conv_w=conv_w, conv_b=conv_b, w_rg=w_rg, b_rg=b_rg, w_ig=w_ig, b_ig=b_ig,
             lru_lambda=lru_lambda, w_rnn_proj=w_rnn_proj, q_gain=q_gain, k_gain=k_gain, sinks=sinks,
             w_attn_proj=w_attn_proj, w_out=w_out, g_mlp=g_mlp, w_up=w_up, w_down=w_down, g_ple=g_ple,
             w_ple_gate=w_ple_gate, w_ple_proj=w_ple_proj)
    m = dict(g_mix=m_g_mix, w_in=m_w_in, conv_w=m_conv_w, conv_b=m_conv_b, w_rg=m_w_rg, b_rg=m_b_rg, w_ig=m_w_ig,
             b_ig=m_b_ig, lru_lambda=m_lru_lambda, w_rnn_proj=m_w_rnn_proj, q_gain=m_q_gain, k_gain=m_k_gain,
             sinks=m_sinks, w_attn_proj=m_w_attn_proj, w_out=m_w_out, g_mlp=m_g_mlp, w_up=m_w_up, w_down=m_w_down,
             g_ple=m_g_ple, w_ple_gate=m_w_ple_gate, w_ple_proj=m_w_ple_proj)
    v = dict(g_mix=v_g_mix, w_in=v_w_in, conv_w=v_conv_w, conv_b=v_conv_b, w_rg=v_w_rg, b_rg=v_b_rg, w_ig=v_w_ig,
             b_ig=v_b_ig, lru_lambda=v_lru_lambda, w_rnn_proj=v_w_rnn_proj, q_gain=v_q_gain, k_gain=v_k_gain,
             sinks=v_sinks, w_attn_proj=v_w_attn_proj, w_out=v_w_out, g_mlp=v_g_mlp, w_up=v_w_up, w_down=v_w_down,
             g_ple=v_g_ple, w_ple_gate=v_w_ple_gate, w_ple_proj=v_w_ple_proj)
    n_seq, S, _ = x.shape
    T = n_seq * S
    chip = 2 * lax.axis_index("x") + lax.axis_index("y")

    tm, tm_rnn = TM, TM_RNN
    xf, pf, tf = x.reshape(T, D_MODEL), p.reshape(T, PLE_DIM), loss_target.reshape(T, D_MODEL)
    first = lambda outs: [o[0] for o in outs]

    w_in_g = _gather_bf16(w["w_in"][0], "gather_w_in")
    wb = {name: w[name][0].astype(BF16) for name in _BIG if name != "w_in"}
    grp_mix, grp_mlp, grp_ple = ("w_rnn_proj", "w_attn_proj", "w_out"), ("w_up", "w_down"), ("w_ple_gate", "w_ple_proj")

    wb["conv_w"] = jnp.pad(conv_w[0], ((0, 16 - CONV_W), (0, 0)))

    cosf, sins = _rope_tables(S)
    ind_q, ind_qt = _indicator(D_MODEL)
    ind_k, ind_kt = _indicator(KV_W)
    wrg2 = _pair_blockdiag(w_rg[0]).astype(BF16)
    wig2 = _pair_blockdiag(w_ig[0]).astype(BF16)
    qg = jnp.tile(q_gain, (1, N_HEADS))
    kg = jnp.tile(k_gain, (1, N_KV))
    sk = sinks.reshape(N_HEADS)
    attn_c = (qg, kg, sk, cosf, sins, ind_q, ind_qt, ind_k, ind_kt, n_seq, S)

    (h0, xr, gr, zq, zk, zv, ga, gb), ph = _inproj_fwd(xf, g_mix, w_in_g, tm,
                                                     phases=[_ph_gather_send(wb[n]) for n in grp_mix + ("conv_w",)])
    g_small = first(ph)
    o, ph = _attn_fwd(zq, zk, zv, *attn_c,
                      phases=[_ph_gather_pass(g) for g in g_small]
                      + [_ph_gather_send(wb[n]) for n in ("w_up",) + grp_ple])
    g_small, (wu, wpg, wpp) = first(ph[:4]), first(ph[4:])
    cw_full = g_small[3][:, :CONV_W, :].transpose(1, 0, 2).reshape(CONV_W, D_MODEL)
    rnn_w = (cw_full, conv_b, wrg2, b_rg, wig2, b_ig, lru_lambda)
    (xc, h, *gates, ya), ph = _rnn_fwd(xr, gr, *rnn_w, n_seq, S, tm_rnn,
                               phases=[_ph_gather_pass(g) for g in (wu, wpg, wpp)]
                               + [_ph_gather_send(wb["w_down"])])
    (wu, wpg, wpp), wd = first(ph[:3]), ph[3][0]
    wr, wa, wo = (g.reshape(D_MODEL, D_MODEL) for g in g_small[:3])
    wpg = wpg.reshape(D_MODEL, D_MODEL)
    (x1, merged), ph = _merge_fwd(xf, ya, o, ga, gb, wr, wa, wo, tm, phases=[_ph_gather_pass(wd)])
    wd = ph[0][0].reshape(D_FF, D_MODEL)
    (x2, hm, u, act), _ = _mlp_fwd(x1, g_mlp, wu, wd, tm // 2)
    (loss_t, dx2, pb, de, hp, dtg, dg_ple), _ = _ple_loss(x2, pf, tf, g_ple, wpg, wpp, tm)

    chipmajor = lambda g: g.reshape(N_CHIPS, g.shape[-2] // N_CHIPS, g.shape[-1]) if g.ndim == 2 else g
    tmw = min(2 * tm, T)
    dw_pp = _wgrad(pb, de, "wgrad_ple_proj", False, D_MODEL, tmw)[0]
    part_ple = [chipmajor(_wgrad(hp, dtg, "wgrad_ple_gate", False, D_MODEL, tmw)[0]),
                dw_pp.reshape(PLE_DIM, N_CHIPS, D_MODEL // N_CHIPS).transpose(1, 0, 2)]
    (dx1, du, dg_mlp), ph = _mlp_bwd(dx2, u, x1, g_mlp, wu, wd, tm // 2, phases=[_ph_pair_send(g) for g in part_ple])
    send_ple, own_ple = _pair_sum(part_ple, first(ph), "pair_sum_ple")
    dw_down, ph = _wgrad(act, dx2, "wgrad_down", False, D_MODEL // 2, tmw, phases=[_ph_chip_send(s) for s in send_ple])
    red_ple = _chip_sum(own_ple, first(ph), "chip_sum_ple")
    part_mlp = [_wgrad(hm, du, "wgrad_up", True, D_MODEL, tmw)[0], chipmajor(dw_down)]
    (dga, dgb, dya, dyb, dyain, do), _ = _merge_bwd(dx1, ga, gb, ya, o, wr, wa, wo, tm)
    dw_rnn, ph_up = _wgrad(ya, dya, "wgrad_rnn_proj", False, D_MODEL, tmw, phases=[_ph_pair_send(part_mlp[0])])
    dw_attn, ph_down = _wgrad(o, dyb, "wgrad_attn_proj", False, D_MODEL, tmw, phases=[_ph_pair_send(part_mlp[1])])
    dw_out, ph = _wgrad(merged, dx1, "wgrad_out", False, D_MODEL, tmw, phases=[_ph_half_swap(r) for r in red_ple])
    red_ple = first(ph)
    send_mlp, own_mlp = _pair_sum(part_mlp, [ph_up[0][0], ph_down[0][0]], "pair_sum_mlp")
    part_mix = [chipmajor(dw_rnn), chipmajor(dw_attn), chipmajor(dw_out)]
    (dxr, dgr, vec, dwrg2, dwig2), ph = _rnn_bwd(
        dyain, xr, gr, xc, h, gates, cw_full, wrg2, wig2, lru_lambda, n_seq, S, tm_rnn,
        phases=[_ph_chip_send(s) for s in send_mlp] + [_ph_pair_send(g) for g in part_mix])
    red_mlp = _chip_sum(own_mlp, first(ph[:2]), "chip_sum_mlp")
    send_mix, own_mix = _pair_sum(part_mix, first(ph[2:]), "pair_sum_mix")
    (dq, dkc, dkp, dvc, dvp, dqg, dsk), ph = _attn_bwd(
        do, zq, zk, zv, *attn_c, phases=[_ph_half_swap(r) for r in red_mlp] + [_ph_chip_send(s) for s in send_mix])
    red_mlp = first(ph[:2])
    red_mix = _chip_sum(own_mix, first(ph[2:]), "chip_sum_mix")
    (dk, dv, dkg), _ = _kv_bwd(dkc, dkp, dvc, dvp, zk, kg, cosf, sins, ind_k, ind_kt, n_seq, S)
    dz_parts = [dxr, dgr, dq, dk, dv, dga, dgb]
    send_in, own_in = _pair_exchange_sum(_wgrad_in(h0, dz_parts, tm), "pair_sum_in")
    (grad_x, dg_mix), ph = _inproj_bwd(dz_parts, w_in_g, xf, g_mix, dx1, tm,
                                       phases=[_ph_half_swap(r) for r in red_mix] + [_ph_chip_send(send_in)])
    red_mix = first(ph[:3])
    red_in = _chip_sum([own_in], first(ph[3:]), "chip_sum_in")
    reduced = dict(zip(grp_ple + grp_mlp + grp_mix, red_ple + red_mlp + red_mix))
    grads = {
        "g_mix": dg_mix[0], "g_mlp": dg_mlp[0], "g_ple": dg_ple[0],
        "conv_w": vec[0:CONV_W], "conv_b": vec[4], "b_rg": vec[5], "b_ig": vec[6], "lru_lambda": vec[7],
        "w_rg": _pair_blockdiag_extract(dwrg2), "w_ig": _pair_blockdiag_extract(dwig2),
        "q_gain": dqg.reshape(N_HEADS, HEAD_DIM).sum(0), "k_gain": dkg.reshape(N_KV, HEAD_DIM).sum(0),
        "sinks": dsk.sum(1),
    }

    rows = [grads["conv_w"], _pad_row(grads["conv_b"]), _pad_row(grads["b_rg"]), _pad_row(grads["b_ig"]),
            _pad_row(grads["lru_lambda"]), _pad_row(grads["g_mix"]), _pad_row(grads["g_mlp"]),
            _pad_row(grads["g_ple"]), _pad_row(grads["q_gain"]), _pad_row(grads["k_gain"]), _pad_row(grads["sinks"]),
            _pad_row(loss_t[0:1, 0:1]), jnp.zeros((1, D_MODEL), F32)]
    vecs = jnp.concatenate(rows, axis=0)
    packed = jnp.concatenate([vecs.reshape(-1, LANES), grads["w_rg"].reshape(-1, LANES),
                              grads["w_ig"].reshape(-1, LANES)], axis=0)
    red = _allreduce_small(packed, "allreduce_small")
    nv = vecs.size // LANES
    rvec = red[0:nv].reshape(16, D_MODEL)
    loss = rvec[14, 0]
    nw = grads["w_rg"].size // LANES
    sg = {
        "conv_w": lax.dynamic_slice(rvec[0:CONV_W], (0, chip * (D_MODEL // N_CHIPS)), (CONV_W, D_MODEL // N_CHIPS)),
        "conv_b": rvec[4], "b_rg": rvec[5], "b_ig": rvec[6], "lru_lambda": rvec[7], "g_mix": rvec[8],
        "g_mlp": rvec[9], "g_ple": rvec[10], "q_gain": rvec[11, :HEAD_DIM], "k_gain": rvec[12, :HEAD_DIM],
        "sinks": rvec[13, :N_HEADS], "w_rg": red[nv:nv + nw], "w_ig": red[nv + nw:nv + 2 * nw],
    }
    sg = {k: sg[k].reshape(w[k].shape) for k in _SMALL}
    d_s, m_s, v_s = _adamw_small([w[k] for k in _SMALL], [sg[k] for k in _SMALL], [m[k] for k in _SMALL],
                                 [v[k] for k in _SMALL])
    grad, delta, new_m, new_v = dict(sg), dict(zip(_SMALL, d_s)), dict(zip(_SMALL, m_s)), dict(zip(_SMALL, v_s))

    for name in ("w_ple_proj", "w_up", "w_down", "w_rnn_proj", "w_attn_proj", "w_out", "w_ple_gate", "w_in"):
        shape = w[name].shape
        outs, ph = _adamw(w[name][0], reduced[name], m[name][0], v[name][0], "adamw_" + name, min(ADAMW_ROWS, shape[1] // 2),
                          phases=[_ph_half_swap(r) for r in red_in] if name == "w_ple_proj" else ())
        if name == "w_ple_proj":
            reduced["w_in"] = ph[0][0]
        grad[name], delta[name], new_m[name], new_v[name] = (a.reshape(shape) for a in outs)

    return (loss, grad_x.reshape(x.shape), *[grad[k] for k in _WEIGHTS], *[delta[k] for k in _WEIGHTS],
            *[new_m[k] for k in _WEIGHTS], *[new_v[k] for k in _WEIGHTS])
```

```python
import functools
import math

import numpy as np
import jax
import jax.numpy as jnp
from jax import lax
from jax.experimental import pallas as pl
from jax.experimental.pallas import tpu as pltpu

F32 = jnp.float32
BF16 = jnp.bfloat16

D_MODEL = 1024
N_HEADS = 16
N_KV = 4
HEAD_DIM = 64
KV_W = N_KV * HEAD_DIM
D_FF = 4096
PLE_DIM = 256
WINDOW = 128
CONV_W = 4
LRU_C = 8.0
NORM_EPS = 1e-6
ROPE_THETA = 10000.0
N_CHIPS = 4
IN_TOTAL = 5632
IN_BLK = IN_TOTAL // N_CHIPS
IN_SEGS = (0, 1024, 2048, 3072, 3328, 3584, 4608, 5632)

ADAM_LR = 0.001
ADAM_B1 = 0.9
ADAM_B2 = 0.999
ADAM_EPS = 1e-08
ADAM_WD = 0.01
ADAM_STEP = 10

LANES = 128
V7X_VMEM_BYTES = 64 * 1024 * 1024
VMEM_LIMIT = V7X_VMEM_BYTES - 8 * 1024 * 1024
MESH_ID = pl.DeviceIdType.MESH
TM, TM_RNN, ADAMW_ROWS = 512, 256, 256
ATTN_BLOCKS_PER_STEP = 2
KV_BLOCKS_PER_STEP = 4


def _dot(a, b):
    return jnp.dot(a, b, preferred_element_type=F32)


def _dot_nt(a, b):
    return lax.dot_general(a, b, (((1,), (1,)), ((), ())), preferred_element_type=F32)


def _dot_tn(a, b):
    return lax.dot_general(a, b, (((0,), (0,)), ((), ())), preferred_element_type=F32)


def _split_dot(x, ind):
    hi = x.astype(BF16)
    lo = (x - hi.astype(F32)).astype(BF16)
    return _dot(hi, ind) + _dot(lo, ind)


def _sigmoid(x):
    return 1.0 / (1.0 + jnp.exp(-x))


_GELU_C = math.sqrt(2.0 / math.pi)


def _gelu_and_grad(g):
    inner = _GELU_C * (g + 0.044715 * g * g * g)
    t = jnp.tanh(inner)
    gelu = 0.5 * g * (1.0 + t)
    dgelu = 0.5 * (1.0 + t) + 0.5 * g * (1.0 - t * t) * _GELU_C * (1.0 + 3.0 * 0.044715 * g * g)
    return gelu, dgelu


def _const(shape):
    nd = len(shape)
    return pl.BlockSpec(shape, lambda *_: (0,) * nd)


def _params(n_grid, vmem=VMEM_LIMIT):
    return pltpu.CompilerParams(dimension_semantics=("arbitrary",) * n_grid, vmem_limit_bytes=vmem)


def _rms_fwd(x, g):
    r = lax.rsqrt(jnp.mean(x * x, axis=-1, keepdims=True) + NORM_EPS)
    return (x * r) * g, r


def _rms_bwd(dy, x, r, g):
    dn = dy * g
    dx = r * dn - x * (r * r * r * jnp.mean(dn * x, axis=-1, keepdims=True))
    dg = jnp.sum(dy * (x * r), axis=0, keepdims=True)
    return dx, dg


def _seg_pieces(blk_lo, blk_hi):
    out = []
    for s in range(7):
        lo, hi = max(blk_lo, IN_SEGS[s]), min(blk_hi, IN_SEGS[s + 1])
        if lo < hi:
            out.append((s, lo - IN_SEGS[s], hi - IN_SEGS[s], lo - blk_lo))
    return out


def _mesh_pos():
    x, y, c = lax.axis_index("x"), lax.axis_index("y"), lax.axis_index("c")
    other_chips = [(1 - x, y), (x, 1 - y), (1 - x, 1 - y)]
    return x, y, c, other_chips


def _peer_slot(k, x, y):
    dx = jnp.bitwise_xor(k // 2, x)
    dy = jnp.bitwise_xor(k % 2, y)
    return jnp.maximum(dx + 2 * dy - 1, 0)


def _half_rows(c, R):
    return pl.ds(pl.multiple_of(c * R, R), R), pl.ds(pl.multiple_of((1 - c) * R, R), R)


def _remote(src, dst, sems, to):
    return pltpu.make_async_remote_copy(src_ref=src, dst_ref=dst, send_sem=sems[0], recv_sem=sems[1],
                                        device_id=to, device_id_type=MESH_ID)


class _Phase:
    def __init__(self, ins, inout, outs, n_remote, n_local, build):
        self.ins, self.inout, self.outs = list(ins), list(inout), list(outs)
        self.n_remote, self.n_local, self.build = n_remote, n_local, build


def _ph_gather_send(wb):
    R2, C = wb.shape
    R = R2 // 2

    def build(ins, outs, rsem, lsem):
        (w_ref,), (g_ref,) = ins, outs
        x, y, c, chips = _mesh_pos()
        me = 2 * x + y
        mine, _ = _half_rows(c, R)
        loc = [pltpu.make_async_copy(w_ref, g_ref.at[me], lsem(0))]
        outg = [_remote(w_ref.at[mine], g_ref.at[me, mine], rsem(j), (cx, cy, c)) for j, (cx, cy) in enumerate(chips)]
        inc = [functools.partial(_remote, w_ref.at[mine], g_ref.at[2 * cx + cy, mine], rsem(j), (x, y, c))
               for j, (cx, cy) in enumerate(chips)]
        return loc, outg, inc

    return _Phase([wb], [], [jax.ShapeDtypeStruct((N_CHIPS, R2, C), wb.dtype)], 3, 1, build)


def _ph_gather_pass(gath):
    _, R2, C = gath.shape
    R = R2 // 2

    def build(ins, outs, rsem, lsem):
        (g_ref,) = outs
        x, y, c, chips = _mesh_pos()
        mine, theirs = _half_rows(c, R)
        outg, inc = [], []
        for j, (cx, cy) in enumerate(chips):
            blk = g_ref.at[2 * cx + cy, mine]
            outg.append(_remote(blk, blk, rsem(j), (x, y, 1 - c)))
            got = g_ref.at[2 * cx + cy, theirs]
            inc.append(functools.partial(_remote, got, got, rsem(j), (x, y, c)))
        return [], outg, inc

    return _Phase([], [gath], [], 3, 0, build)


def _ph_pair_send(partial):
    _, R2, C = partial.shape
    R = R2 // 2

    def build(ins, outs, rsem, lsem):
        (p_ref,), (s_ref,) = ins, outs
        x, y, c, _ = _mesh_pos()
        _, theirs = _half_rows(c, R)
        src = p_ref.at[:, theirs, :]
        return ([], [_remote(src, s_ref, rsem(0), (x, y, 1 - c))],
                [functools.partial(_remote, src, s_ref, rsem(0), (x, y, c))])

    return _Phase([partial], [], [jax.ShapeDtypeStruct((N_CHIPS, R, C), F32)], 1, 0, build)


def _ph_chip_send(sendb):
    def build(ins, outs, rsem, lsem):
        (s_ref,), (r_ref,) = ins, outs
        x, y, c, chips = _mesh_pos()
        outg = [_remote(s_ref.at[j], r_ref.at[j], rsem(j), (cx, cy, c)) for j, (cx, cy) in enumerate(chips)]
        inc = [functools.partial(_remote, s_ref.at[j], r_ref.at[j], rsem(j), (x, y, c)) for j in range(3)]
        return [], outg, inc

    return _Phase([sendb], [], [jax.ShapeDtypeStruct(sendb.shape, sendb.dtype)], 3, 0, build)


def _ph_half_swap(red):
    R2, C = red.shape
    R = R2 // 2

    def build(ins, outs, rsem, lsem):
        (r_ref,) = outs
        x, y, c, _ = _mesh_pos()
        mine, theirs = _half_rows(c, R)
        return ([], [_remote(r_ref.at[mine], r_ref.at[mine], rsem(0), (x, y, 1 - c))],
                [functools.partial(_remote, r_ref.at[theirs], r_ref.at[theirs], rsem(0), (x, y, c))])

    return _Phase([], [red], [], 1, 0, build)


def _call(body, *, name, grid, in_specs, out_specs, out_shape, scratch_shapes=(), phases=()):
    single = not isinstance(out_specs, (list, tuple))
    out_specs = [out_specs] if single else list(out_specs)
    out_shape = [out_shape] if single else list(out_shape)
    n_in, n_out, n_scr = len(in_specs), len(out_specs), len(scratch_shapes)
    if not phases:
        call = pl.pallas_call(body, name=name, grid=grid, in_specs=in_specs, out_specs=out_specs,
                              out_shape=out_shape, scratch_shapes=list(scratch_shapes),
                              compiler_params=_params(len(grid)))
        return lambda *operands: (list(call(*operands)), [])

    ex_in, ex_out, aliases, spans = [], [], {}, []
    for ph in phases:
        i0, o0 = len(ex_in), len(ex_out)
        ex_in += ph.ins
        for a in ph.inout:
            aliases[n_in + len(ex_in)] = n_out + len(ex_out)
            ex_in.append(a)
            ex_out.append(jax.ShapeDtypeStruct(a.shape, a.dtype))
        ex_out += ph.outs
        spans.append((i0, len(ph.ins), o0, len(ex_out) - o0))
    n_remote = sum(ph.n_remote for ph in phases)
    n_local = max(sum(ph.n_local for ph in phases), 1)

    def wrapped(*refs):
        base_in, xin = refs[:n_in], refs[n_in:n_in + len(ex_in)]
        o0 = n_in + len(ex_in)
        base_out, xout = refs[o0:o0 + n_out], refs[o0 + n_out:o0 + n_out + len(ex_out)]
        scr = refs[o0 + n_out + len(ex_out):]
        send_sems, recv_sems, loc_sems = scr[n_scr:]
        first = functools.reduce(jnp.logical_and, [pl.program_id(i) == 0 for i in range(len(grid))])
        last = functools.reduce(jnp.logical_and, [pl.program_id(i) == grid[i] - 1 for i in range(len(grid))])

        def copies():
            out, r0, l0 = [], 0, 0
            for ph, (i0, ni, p0, no) in zip(phases, spans):
                rsem = lambda k, r0=r0: (send_sems.at[r0 + k], recv_sems.at[r0 + k])
                lsem = lambda k, l0=l0: loc_sems.at[l0 + k]
                out.append(ph.build(xin[i0:i0 + ni], xout[p0:p0 + no], rsem, lsem))
                r0, l0 = r0 + ph.n_remote, l0 + ph.n_local
            return out

        @pl.when(first)
        def _():
            for loc, outg, _ in copies():
                for cp in loc + outg:
                    cp.start()

        body(*base_in, *base_out, *scr[:n_scr])

        @pl.when(last)
        def _():
            for loc, outg, inc in copies():
                for make in inc:
                    make().wait_recv()
                for cp in outg:
                    cp.wait_send()
                for cp in loc:
                    cp.wait()

    hbm = pl.BlockSpec(memory_space=pl.ANY)
    call = pl.pallas_call(
        wrapped, name=name, grid=grid, in_specs=list(in_specs) + [hbm] * len(ex_in),
        out_specs=out_specs + [hbm] * len(ex_out), out_shape=out_shape + ex_out,
        scratch_shapes=list(scratch_shapes) + [pltpu.SemaphoreType.DMA((n_remote,)), pltpu.SemaphoreType.DMA((n_remote,)),
                                              pltpu.SemaphoreType.DMA((n_local,))],
        input_output_aliases=aliases, compiler_params=_params(len(grid)))

    def run(*operands):
        res = call(*operands, *ex_in)
        extra = res[n_out:]
        return list(res[:n_out]), [list(extra[p0:p0 + no]) for (_, _, p0, no) in spans]

    return run


def _inproj_fwd(x, g_mix, w_in, tm, phases=()):
    T = x.shape[0]
    widths = [IN_SEGS[i + 1] - IN_SEGS[i] for i in range(7)]

    def body(x_ref, g_ref, w_ref, h_ref, *z_refs):
        h, _ = _rms_fwd(x_ref[...], g_ref[...])
        hb = h.astype(BF16)
        h_ref[...] = hb
        for j in range(N_CHIPS):
            zj = _dot(hb, w_ref[j])
            for s, lo, hi, off in _seg_pieces(j * IN_BLK, (j + 1) * IN_BLK):
                z_refs[s][:, lo:hi] = zj[:, off:off + hi - lo]

    return _call(
        body, phases=phases, name="inproj_fwd", grid=(T // tm,),
        in_specs=[pl.BlockSpec((tm, D_MODEL), lambda i: (i, 0)), _const((1, D_MODEL)),
                  _const((N_CHIPS, D_MODEL, IN_BLK))],
        out_specs=[pl.BlockSpec((tm, D_MODEL), lambda i: (i, 0))]
        + [pl.BlockSpec((tm, w), lambda i: (i, 0)) for w in widths],
        out_shape=[jax.ShapeDtypeStruct((T, D_MODEL), BF16)]
        + [jax.ShapeDtypeStruct((T, w), F32) for w in widths],
    )(x, g_mix, w_in)


def _inproj_bwd(dz_parts, w_in, x, g_mix, dx1, tm, phases=()):
    T = x.shape[0]
    widths = [IN_SEGS[i + 1] - IN_SEGS[i] for i in range(7)]

    def body(*refs):
        p_refs = refs[:7]
        w_ref, x_ref, g_ref, dx1_ref, gx_ref, dg_ref, dz_ref = refs[7:]

        @pl.when(pl.program_id(0) == 0)
        def _():
            dg_ref[...] = jnp.zeros_like(dg_ref)

        for s in range(7):
            dz_ref[:, IN_SEGS[s]:IN_SEGS[s + 1]] = p_refs[s][...]
        dh = jnp.zeros((tm, D_MODEL), F32)
        for j in range(N_CHIPS):
            dh = dh + _dot_nt(dz_ref[:, j * IN_BLK:(j + 1) * IN_BLK], w_ref[j])
        xv = x_ref[...]
        g = g_ref[...]
        _, r = _rms_fwd(xv, g)
        dx, dg = _rms_bwd(dh, xv, r, g)
        gx_ref[...] = dx1_ref[...] + dx
        dg_ref[...] += dg

    row = lambda w: pl.BlockSpec((tm, w), lambda i: (i, 0))
    return _call(
        body, phases=phases, name="inproj_bwd", grid=(T // tm,),
        in_specs=[row(w) for w in widths]
        + [_const((N_CHIPS, D_MODEL, IN_BLK)), row(D_MODEL), _const((1, D_MODEL)), row(D_MODEL)],
        out_specs=[row(D_MODEL), _const((1, D_MODEL))],
        out_shape=[jax.ShapeDtypeStruct((T, D_MODEL), F32), jax.ShapeDtypeStruct((1, D_MODEL), F32)],
        scratch_shapes=[pltpu.VMEM((tm, IN_TOTAL), BF16)],
    )(*dz_parts, w_in, x, g_mix, dx1)


def _wgrad_in(h0, dz_parts, tm):
    T = h0.shape[0]
    widths = [IN_SEGS[i + 1] - IN_SEGS[i] for i in range(7)]

    def body(*refs):
        h_ref, p_refs, o_ref, acc_ref, sems = refs[0], refs[1:8], refs[8], refs[9], refs[10]
        t = pl.program_id(0)
        last = T // tm - 1

        @pl.when(t == 0)
        def _():
            acc_ref[...] = jnp.zeros_like(acc_ref)

        def accumulate(j):
            for s, lo, hi, off in _seg_pieces(j * IN_BLK, (j + 1) * IN_BLK):
                acc_ref[j, :, off:off + hi - lo] += _dot_tn(h_ref[...], p_refs[s][:, lo:hi])

        @pl.when(t < last)
        def _():
            for j in range(N_CHIPS):
                accumulate(j)

        @pl.when(t == last)
        def _():
            copies = [pltpu.make_async_copy(acc_ref.at[j], o_ref.at[j], sems.at[j]) for j in range(N_CHIPS)]
            for j in range(N_CHIPS):
                accumulate(j)
                copies[j].start()
            for cp in copies:
                cp.wait()

    row = lambda w: pl.BlockSpec((tm, w), lambda i: (i, 0))
    return pl.pallas_call(
        body, name="wgrad_in", grid=(T // tm,), in_specs=[row(D_MODEL)] + [row(w) for w in widths],
        out_specs=pl.BlockSpec(memory_space=pl.ANY),
        out_shape=jax.ShapeDtypeStruct((N_CHIPS, D_MODEL, IN_BLK), F32),
        scratch_shapes=[pltpu.VMEM((N_CHIPS, D_MODEL, IN_BLK), F32), pltpu.SemaphoreType.DMA((N_CHIPS,))],
        compiler_params=_params(1),
    )(h0, *dz_parts)


def _wgrad(a, g, name, blocked, cn, tm, phases=()):
    T, K = a.shape
    N = g.shape[1]
    nb = N // cn

    def body(a_ref, g_ref, o_ref):
        @pl.when(pl.program_id(1) == 0)
        def _():
            o_ref[...] = jnp.zeros_like(o_ref)

        o_ref[...] += _dot_tn(a_ref[...].astype(BF16), g_ref[...].astype(BF16))

    if blocked:
        out_spec = pl.BlockSpec((None, K, cn), lambda j, t: (j, 0, 0))
        out_shape = jax.ShapeDtypeStruct((nb, K, cn), F32)
    else:
        out_spec = pl.BlockSpec((K, cn), lambda j, t: (0, j))
        out_shape = jax.ShapeDtypeStruct((K, N), F32)
    outs, extra = _call(
        body, phases=phases, name=name, grid=(nb, T // tm),
        in_specs=[pl.BlockSpec((tm, K), lambda j, t: (t, 0)), pl.BlockSpec((tm, cn), lambda j, t: (t, j))],
        out_specs=out_spec, out_shape=out_shape,
    )(a, g)
    return outs[0], extra


def _shift_down(x, prev8, sft, row, row8, tm):
    xs = pltpu.roll(x, sft, 0)
    top = jnp.where(row8 < sft, pltpu.roll(prev8, sft, 0), xs[0:8])
    return jnp.concatenate([top, xs[8:]], axis=0)


def _shift_up(x, next8, sft, row8, tm):
    xs = pltpu.roll(x, tm - sft, 0)
    bot = jnp.where(row8 >= 8 - sft, pltpu.roll(next8, 8 - sft, 0), xs[tm - 8:tm])
    return jnp.concatenate([xs[0:tm - 8], bot], axis=0)


def _conv_fwd(x, prev8, cw_ref, cb, row, row8, tm):
    xc = cb + cw_ref[CONV_W - 1:CONV_W, :] * x
    for sft in range(1, CONV_W):
        j = CONV_W - 1 - sft
        xc = xc + cw_ref[j:j + 1, :] * _shift_down(x, prev8, sft, row, row8, tm)
    return xc


def _blockdiag_dot(xb, w_ref, transpose):
    outs = []
    for b in range(D_MODEL // LANES):
        xs = xb[:, b * LANES:(b + 1) * LANES]
        outs.append(_dot_nt(xs, w_ref[b]) if transpose else _dot(xs, w_ref[b]))
    return jnp.concatenate(outs, axis=1)


def _softplus_neg(lam):
    e = jnp.exp(-jnp.abs(lam))
    u = 1.0 + e
    log1p_e = jnp.where(u == 1.0, e, jnp.log(u) * (e / (u - 1.0)))
    sp = jnp.maximum(-lam, 0.0) + log1p_e
    return sp, -_sigmoid(-lam)


def _lru_gates(xc, wrg_ref, brg, wig_ref, big, sp):
    xcb = xc.astype(BF16)
    r = _sigmoid(_blockdiag_dot(xcb, wrg_ref, False) + brg)
    i = _sigmoid(_blockdiag_dot(xcb, wig_ref, False) + big)
    log_a = (-LRU_C) * r * sp
    a = jnp.exp(log_a)
    t = jnp.tanh(log_a)
    one_m_a2 = (-2.0) * t / (1.0 - t)
    mult = jnp.sqrt(one_m_a2)
    return xcb, r, i, a, mult


def _scan_down(a, b, row, tm):
    d = 1
    while d < tm:
        if d < 8:
            keep = row >= d
            a_s = jnp.where(keep, pltpu.roll(a, d, 0), 1.0)
            b_s = jnp.where(keep, pltpu.roll(b, d, 0), 0.0)
            b = a * b_s + b
            a = a * a_s
        else:
            b = jnp.concatenate([b[:d], a[d:] * b[:-d] + b[d:]], axis=0)
            a = jnp.concatenate([a[:d], a[d:] * a[:-d]], axis=0)
        d *= 2
    return a, b


def _scan_up(c, b, row, tm):
    d = 1
    while d < tm:
        if d < 8:
            keep = row < tm - d
            c_s = jnp.where(keep, pltpu.roll(c, tm - d, 0), 1.0)
            b_s = jnp.where(keep, pltpu.roll(b, tm - d, 0), 0.0)
            b = c * b_s + b
            c = c * c_s
        else:
            b = jnp.concatenate([c[:-d] * b[d:] + b[:-d], b[-d:]], axis=0)
            c = jnp.concatenate([c[:-d] * c[d:], c[-d:]], axis=0)
        d *= 2
    return c, b


def _rnn_fwd(xr, gr, conv_w, conv_b, wrg2, b_rg, wig2, b_ig, lam, n_seq, S, tm, phases=()):
    T = xr.shape[0]
    nt = S // tm
    W = D_MODEL

    def body(xr_ref, gr_ref, cw_ref, cb_ref, wrg_ref, brg_ref, wig_ref, big_ref, lam_ref,
             xc_ref, h_ref, r_ref, i_ref, a_ref, mult_ref, ya_ref, px_ref, ph_ref):
        @pl.when(pl.program_id(1) == 0)
        def _():
            px_ref[...] = jnp.zeros_like(px_ref)
            ph_ref[...] = jnp.zeros_like(ph_ref)

        row = lax.broadcasted_iota(jnp.int32, (tm, W), 0)
        row8 = lax.broadcasted_iota(jnp.int32, (8, W), 0)
        x = xr_ref[...]
        xc = _conv_fwd(x, px_ref[...], cw_ref, cb_ref[...], row, row8, tm)
        sp, _ = _softplus_neg(lam_ref[...])
        _, r, i, a, mult = _lru_gates(xc, wrg_ref, brg_ref[...], wig_ref, big_ref[...], sp)
        r_ref[...], i_ref[...], a_ref[...], mult_ref[...] = r, i, a, mult
        bterm = mult * (i * xc)
        acum, hloc = _scan_down(a, bterm, row, tm)
        h = hloc + acum * ph_ref[7:8, :]
        h_ref[...] = h
        xc_ref[...] = xc
        gelu, _ = _gelu_and_grad(gr_ref[...])
        ya_ref[...] = (h * gelu).astype(BF16)
        px_ref[...] = xr_ref[tm - 8:tm, :]
        ph_ref[...] = h_ref[tm - 8:tm, :]

    tile = pl.BlockSpec((tm, W), lambda s, t: (s * nt + t, 0))
    return _call(
        body, phases=phases, name="rnn_fwd", grid=(n_seq, nt),
        in_specs=[tile, tile, _const((CONV_W, W)), _const((1, W)), _const((8, LANES, LANES)), _const((1, W)),
                  _const((8, LANES, LANES)), _const((1, W)), _const((1, W))],
        out_specs=[tile] * 7,
        out_shape=[jax.ShapeDtypeStruct((T, W), F32)] * 6 + [jax.ShapeDtypeStruct((T, W), BF16)],
        scratch_shapes=[pltpu.VMEM((8, W), F32), pltpu.VMEM((8, W), F32)],
    )(xr, gr, conv_w, conv_b, wrg2, b_rg, wig2, b_ig, lam)


def _rnn_bwd(dya, xr, gr, xc, h, gates, conv_w, wrg2, wig2, lam, n_seq, S, tm, phases=()):
    T = xr.shape[0]
    nt = S // tm
    W = D_MODEL
    nb8 = tm // 8

    def body(dya_ref, xr_ref, gr_ref, xc_ref, h_ref, r_ref, i_ref, a_ref, mult_ref, xprev_ref, hprev_ref, cw_ref,
             wrg_ref, wig_ref, lam_ref, dxr_ref, dgr_ref, vec_ref, dwrg_ref, dwig_ref, cg_ref, ndxc_ref, tmp_ref):
        s, ti = pl.program_id(0), pl.program_id(1)

        @pl.when((s == 0) & (ti == 0))
        def _():
            vec_ref[...] = jnp.zeros_like(vec_ref)
            dwrg_ref[...] = jnp.zeros_like(dwrg_ref)
            dwig_ref[...] = jnp.zeros_like(dwig_ref)

        @pl.when(ti == 0)
        def _():
            cg_ref[...] = jnp.zeros_like(cg_ref)
            ndxc_ref[...] = jnp.zeros_like(ndxc_ref)

        first = ti == nt - 1
        row = lax.broadcasted_iota(jnp.int32, (tm, W), 0)
        row8 = lax.broadcasted_iota(jnp.int32, (8, W), 0)
        x = xr_ref[...]
        xc = xc_ref[...]
        hv = h_ref[...]
        xprev = jnp.where(first, 0.0, xprev_ref[...])
        hprev = jnp.where(first, 0.0, hprev_ref[...])
        sp, dsp_dlam = _softplus_neg(lam_ref[...])
        xcb = xc.astype(BF16)
        r, i, a, mult = r_ref[...], i_ref[...], a_ref[...], mult_ref[...]

        gelu, dgelu = _gelu_and_grad(gr_ref[...])
        dya_v = dya_ref[...]
        dgr_ref[...] = (dya_v * hv * dgelu).astype(BF16)
        dh = dya_v * gelu
        c = jnp.where(row < tm - 1, pltpu.roll(a, tm - 1, 0), 1.0)
        ccum, gloc = _scan_up(c, dh, row, tm)
        G = gloc + ccum * cg_ref[0:1, :]
        tmp_ref[...] = a * G
        cg_ref[...] = tmp_ref[0:8, :]

        h_m1 = _shift_down(hv, hprev, 1, row, row8, tm)
        ixc = i * xc
        dixc = G * mult
        dlog_a = (G * h_m1) * a - (G * ixc) * (a * a / mult)
        dr = dlog_a * ((-LRU_C) * sp)
        di = dixc * xc
        drg = dr * r * (1.0 - r)
        dig = di * i * (1.0 - i)
        vec_ref[7:8, :] += jnp.sum(dlog_a * ((-LRU_C) * r), axis=0, keepdims=True) * dsp_dlam
        vec_ref[5:6, :] += jnp.sum(drg, axis=0, keepdims=True)
        vec_ref[6:7, :] += jnp.sum(dig, axis=0, keepdims=True)
        drgb = drg.astype(BF16)
        digb = dig.astype(BF16)
        dxc = dixc * i + _blockdiag_dot(drgb, wrg_ref, True) + _blockdiag_dot(digb, wig_ref, True)
        for b in range(W // LANES):
            sl = slice(b * LANES, (b + 1) * LANES)
            dwrg_ref[b] += _dot_tn(xcb[:, sl], drgb[:, sl])
            dwig_ref[b] += _dot_tn(xcb[:, sl], digb[:, sl])

        vec_ref[4:5, :] += jnp.sum(dxc, axis=0, keepdims=True)
        vec_ref[3:4, :] += jnp.sum(dxc * x, axis=0, keepdims=True)
        dxr = cw_ref[CONV_W - 1:CONV_W, :] * dxc
        nxt = ndxc_ref[...]
        for sft in range(1, CONV_W):
            j = CONV_W - 1 - sft
            vec_ref[j:j + 1, :] += jnp.sum(dxc * _shift_down(x, xprev, sft, row, row8, tm), axis=0, keepdims=True)
            dxr = dxr + cw_ref[j:j + 1, :] * _shift_up(dxc, nxt, sft, row8, tm)
        dxr_ref[...] = dxr.astype(BF16)
        tmp_ref[...] = dxc
        ndxc_ref[...] = tmp_ref[0:8, :]

    rev = lambda s, t: (s * nt + nt - 1 - t, 0)
    tile = pl.BlockSpec((tm, W), rev)
    prev8 = pl.BlockSpec((8, W), lambda s, t: (jnp.maximum((s * nt + nt - 1 - t) * nb8 - 1, 0), 0))
    return _call(
        body, phases=phases, name="rnn_bwd", grid=(n_seq, nt),
        in_specs=[tile] * 9 + [prev8, prev8, _const((CONV_W, W)), _const((8, LANES, LANES)),
                               _const((8, LANES, LANES)), _const((1, W))],
        out_specs=[tile, tile, _const((16, W)), _const((8, LANES, LANES)), _const((8, LANES, LANES))],
        out_shape=[jax.ShapeDtypeStruct((T, W), BF16), jax.ShapeDtypeStruct((T, W), BF16),
                   jax.ShapeDtypeStruct((16, W), F32), jax.ShapeDtypeStruct((8, LANES, LANES), F32),
                   jax.ShapeDtypeStruct((8, LANES, LANES), F32)],
        scratch_shapes=[pltpu.VMEM((8, W), F32), pltpu.VMEM((8, W), F32), pltpu.VMEM((tm, W), F32)],
    )(dya, xr, gr, xc, h, *gates, xr, h, conv_w, wrg2, wig2, lam)


def _head_swap(t, lane):
    w = t.shape[1]
    return jnp.where(lane % HEAD_DIM < HEAD_DIM // 2, pltpu.roll(t, w - HEAD_DIM // 2, 1),
                     pltpu.roll(t, HEAD_DIM // 2, 1))


def _qk_prep(t, gain, cosf, sins, ind, indt, lane):
    ms = _split_dot(t * t, ind) * (1.0 / HEAD_DIM)
    rstd = _split_dot(lax.rsqrt(ms + NORM_EPS), indt)
    tn = (t * rstd) * gain
    return tn * cosf + _head_swap(tn, lane) * sins, rstd


def _qk_prep_bwd(dy, t, rstd, gain, cosf, sins, ind, indt, lane):
    dtn = dy * cosf + _head_swap(dy * sins, lane)
    dgain = jnp.sum(dtn * (t * rstd), axis=0, keepdims=True)
    dn = dtn * gain
    m = _split_dot(_split_dot(dn * t, ind), indt) * (1.0 / HEAD_DIM)
    return rstd * dn - t * (rstd * rstd * rstd * m), dgain


def _attn_mask_t(blk_idx):
    ci = lax.broadcasted_iota(jnp.int32, (2 * WINDOW, WINDOW), 0)
    qi = lax.broadcasted_iota(jnp.int32, (2 * WINDOW, WINDOW), 1)
    diff = WINDOW + qi - ci
    return (diff >= 0) & (diff < WINDOW) & ((ci >= WINDOW) | (blk_idx > 0))


def _stack_heads(t, kvh, lo):
    parts = []
    for i in (2 * kvh, 2 * kvh + 1):
        tp = t[:, i * LANES:(i + 1) * LANES]
        parts += [jnp.where(lo, tp, 0.0), jnp.where(lo, 0.0, tp)]
    return jnp.concatenate(parts, axis=0).astype(BF16)


def _unstack_heads(ts, lo):
    w = WINDOW
    return jnp.where(lo, ts[0:w], ts[w:2 * w]), jnp.where(lo, ts[2 * w:3 * w], ts[3 * w:4 * w])


def _dup_head(t, kvh, lo2):
    m = kvh // 2
    t2 = t[:, m * LANES:(m + 1) * LANES]
    t2r = pltpu.roll(t2, HEAD_DIM, 1)
    return (jnp.where(lo2, t2, t2r) if kvh % 2 == 0 else jnp.where(lo2, t2r, t2)).astype(BF16)


def _fold_head(ts, kvh, lo2):
    tot = ts + pltpu.roll(ts, HEAD_DIM, 1)
    own = lo2 if kvh % 2 == 0 else ~lo2
    return jnp.where(own, tot, 0.0)


KEY_CHUNKS = tuple(slice(i * 64, (i + 1) * 64) for i in range(2 * WINDOW // 64))


def _fold8(x, op):
    return op(x.reshape(x.shape[0] // 8, 8, x.shape[1]), axis=0)


def _softmax_stats(s_ref, b, cols, sink):
    m8 = None
    for c in KEY_CHUNKS:
        t = _fold8(s_ref[b, c, cols], jnp.max)
        m8 = t if m8 is None else jnp.maximum(m8, t)
    mx = jnp.maximum(jnp.max(m8, axis=0, keepdims=True), sink)
    d8 = None
    for c in KEY_CHUNKS:
        t = _fold8(jnp.exp(s_ref[b, c, cols] - mx), jnp.sum)
        d8 = t if d8 is None else d8 + t
    es = jnp.exp(sink - mx)
    inv = 1.0 / (jnp.sum(d8, axis=0, keepdims=True) + es)
    return mx, inv, es * inv


def _attn_fwd(q, k, v, qg, kg, sinks, cosf, sins, ind_q, ind_qt, ind_k, ind_kt, n_seq, S, phases=()):
    T = q.shape[0]
    nblk = S // WINDOW
    W = D_MODEL

    def body(sink_ref, q_ref, k_ref, v_ref, qg_ref, kg_ref, cos_ref, sin_ref, iq_ref, iqt_ref, ik_ref, ikt_ref,
             o_ref, kc_ref, vc_ref, s_ref, p_ref, qs_ref, kd_ref, vd_ref):
        @pl.when(pl.program_id(1) == 0)
        def _():
            kc_ref[...] = jnp.zeros_like(kc_ref)
            vc_ref[...] = jnp.zeros_like(vc_ref)

        lane = lax.broadcasted_iota(jnp.int32, (WINDOW, W), 1)
        lo = lane[:, :LANES] < HEAD_DIM
        lo2 = lax.broadcasted_iota(jnp.int32, (2 * WINDOW, LANES), 1) < HEAD_DIM

        def one_block(h):
            n = pl.program_id(1) * bps + h
            rows = slice(h * WINDOW, (h + 1) * WINDOW)
            cosf, sinv = jnp.tile(cos_ref[rows, :], (1, W // LANES)), jnp.tile(sin_ref[rows, :], (1, W // LANES))
            qr, _ = _qk_prep(q_ref[rows, :], qg_ref[...], cosf, sinv, iq_ref[...], iqt_ref[...], lane)
            kr, _ = _qk_prep(k_ref[rows, :], kg_ref[...], cosf[:, :KV_W], sinv[:, :KV_W], ik_ref[...], ikt_ref[...],
                             lane[:, :KV_W])
            kc_ref[WINDOW:2 * WINDOW, :] = kr
            vc_ref[WINDOW:2 * WINDOW, :] = v_ref[rows, :]
            kc, vc = kc_ref[...], vc_ref[...]
            mask = jnp.tile(_attn_mask_t(n), (1, 4))
            qr = qr * HEAD_DIM ** -0.5
            for kvh in range(N_KV):
                qs_ref[h, kvh] = _stack_heads(qr, kvh, lo)
                kd_ref[h, kvh] = _dup_head(kc, kvh, lo2)
                vd_ref[h, kvh] = _dup_head(vc, kvh, lo2)

            def scores(kvh):
                s_ref[h, kvh % 2] = jnp.where(mask, _dot_nt(kd_ref[h, kvh], qs_ref[h, kvh]), -1e30)

            def softmax(kvh):
                sb, pb = s_ref.at[h], p_ref.at[h]
                b = kvh % 2
                for r in range(4):
                    cols = slice(r * WINDOW, (r + 1) * WINDOW)
                    mx, inv, _ = _softmax_stats(sb, b, cols, sink_ref[4 * kvh + r])
                    for c in KEY_CHUNKS:
                        pb[b, c, cols] = (jnp.exp(sb[b, c, cols] - mx) * inv).astype(BF16)

            def output(kvh):
                o0, o1 = _unstack_heads(_dot_tn(p_ref[h, kvh % 2], vd_ref[h, kvh]), lo)
                o_ref[rows, (2 * kvh) * LANES:(2 * kvh + 1) * LANES] = o0.astype(BF16)
                o_ref[rows, (2 * kvh + 1) * LANES:(2 * kvh + 2) * LANES] = o1.astype(BF16)

            scores(0)
            for kvh in range(N_KV):
                if kvh + 1 < N_KV:
                    scores(kvh + 1)
                softmax(kvh)
                output(kvh)
            kc_ref[0:WINDOW, :] = kr
            vc_ref[0:WINDOW, :] = v_ref[rows, :]

        for h in range(bps):
            one_block(h)

    bps = ATTN_BLOCKS_PER_STEP
    rows_step = bps * WINDOW
    blk = lambda w: pl.BlockSpec((rows_step, w), lambda s, n: (s * (nblk // bps) + n, 0))
    pos = pl.BlockSpec((rows_step, LANES), lambda s, n: (n, 0))
    outs, extra = _call(
        body, phases=phases, name="attn_fwd", grid=(n_seq, nblk // bps),
        in_specs=[pl.BlockSpec(memory_space=pltpu.SMEM), blk(W), blk(KV_W), blk(KV_W), _const((1, W)),
                  _const((1, KV_W)), pos, pos, _const((W, LANES)), _const((LANES, W)), _const((KV_W, LANES)),
                  _const((LANES, KV_W))],
        out_specs=blk(W), out_shape=jax.ShapeDtypeStruct((T, W), BF16),
        scratch_shapes=[pltpu.VMEM((2 * WINDOW, KV_W), F32), pltpu.VMEM((2 * WINDOW, KV_W), F32),
                        pltpu.VMEM((bps, 2, 2 * WINDOW, 4 * WINDOW), F32),
                        pltpu.VMEM((bps, 2, 2 * WINDOW, 4 * WINDOW), BF16),
                        pltpu.VMEM((bps, N_KV, 4 * WINDOW, LANES), BF16),
                        pltpu.VMEM((bps, N_KV, 2 * WINDOW, LANES), BF16),
                        pltpu.VMEM((bps, N_KV, 2 * WINDOW, LANES), BF16)],
    )(sinks, q, k, v, qg, kg, cosf, sins, ind_q, ind_qt, ind_k, ind_kt)
    return outs[0], extra


def _attn_bwd(do, q, k, v, qg, kg, sinks, cosf, sins, ind_q, ind_qt, ind_k, ind_kt, n_seq, S, phases=()):
    T = q.shape[0]
    nblk = S // WINDOW
    W = D_MODEL

    def body(sink_ref, do_ref, q_ref, k_ref, v_ref, qg_ref, kg_ref, cos_ref, sin_ref, iq_ref, iqt_ref, ik_ref,
             ikt_ref, dq_ref, dkc_ref, dkp_ref, dvc_ref, dvp_ref, dqg_ref, dsk_ref, kc_ref, vc_ref, dqr_ref,
             dk_ref, dv_ref, s_ref, dp_ref, p_ref, ds_ref, qs_ref, dos_ref, kd_ref, vd_ref):
        s_id, n_step = pl.program_id(0), pl.program_id(1)

        @pl.when((s_id == 0) & (n_step == 0))
        def _():
            dqg_ref[...] = jnp.zeros_like(dqg_ref)
            dsk_ref[...] = jnp.zeros_like(dsk_ref)

        @pl.when(n_step == 0)
        def _():
            kc_ref[...] = jnp.zeros_like(kc_ref)
            vc_ref[...] = jnp.zeros_like(vc_ref)

        lane = lax.broadcasted_iota(jnp.int32, (WINDOW, W), 1)
        lane_k = lane[:, :KV_W]
        lo = lane[:, :LANES] < HEAD_DIM
        lo2 = lax.broadcasted_iota(jnp.int32, (2 * WINDOW, LANES), 1) < HEAD_DIM
        scale = HEAD_DIM ** -0.5

        def one_block(h):
            n = n_step * bps + h
            rows = slice(h * WINDOW, (h + 1) * WINDOW)
            cosf, sinv = jnp.tile(cos_ref[rows, :], (1, W // LANES)), jnp.tile(sin_ref[rows, :], (1, W // LANES))
            qv = q_ref[rows, :]
            qr, q_rstd = _qk_prep(qv, qg_ref[...], cosf, sinv, iq_ref[...], iqt_ref[...], lane)
            kr, _ = _qk_prep(k_ref[rows, :], kg_ref[...], cosf[:, :KV_W], sinv[:, :KV_W], ik_ref[...], ikt_ref[...],
                             lane_k)
            kc_ref[WINDOW:2 * WINDOW, :] = kr
            vc_ref[WINDOW:2 * WINDOW, :] = v_ref[rows, :]
            kc, vc = kc_ref[...], vc_ref[...]
            dov = do_ref[rows, :]
            mask = jnp.tile(_attn_mask_t(n), (1, 4))
            qr = qr * scale
            dk_ref[h] = jnp.zeros((2 * WINDOW, KV_W), F32)
            dv_ref[h] = jnp.zeros((2 * WINDOW, KV_W), F32)
            for kvh in range(N_KV):
                qs_ref[h, kvh] = _stack_heads(qr, kvh, lo)
                dos_ref[h, kvh] = _stack_heads(dov, kvh, lo)
                kd_ref[h, kvh] = _dup_head(kc, kvh, lo2)
                vd_ref[h, kvh] = _dup_head(vc, kvh, lo2)
            sb, dpb, pb, dsb = s_ref.at[h], dp_ref.at[h], p_ref.at[h], ds_ref.at[h]

            def scores(kvh):
                b = kvh % 2
                sb[b] = jnp.where(mask, _dot_nt(kd_ref[h, kvh], qs_ref[h, kvh]), -1e30)
                dpb[b] = _dot_nt(vd_ref[h, kvh], dos_ref[h, kvh])

            def softmax(kvh):
                b = kvh % 2
                for r in range(4):
                    cols = slice(r * WINDOW, (r + 1) * WINDOW)
                    head = 4 * kvh + r
                    mx, inv, ps = _softmax_stats(sb, b, cols, sink_ref[head])
                    g8 = None
                    for c in KEY_CHUNKS:
                        t = _fold8(jnp.exp(sb[b, c, cols] - mx) * dpb[b, c, cols], jnp.sum)
                        g8 = t if g8 is None else g8 + t
                    dd = jnp.sum(g8, axis=0, keepdims=True) * inv
                    for c in KEY_CHUNKS:
                        p = jnp.exp(sb[b, c, cols] - mx) * inv
                        pb[b, c, cols] = p.astype(BF16)
                        dsb[b, c, cols] = (p * (dpb[b, c, cols] - dd)).astype(BF16)
                    dsk_ref[head:head + 1, :] -= ps * dd

            def grads(kvh):
                m, b = kvh // 2, kvh % 2
                dq0, dq1 = _unstack_heads(_dot_tn(dsb[b], kd_ref[h, kvh]) * scale, lo)
                dqr_ref[h, :, (2 * kvh) * LANES:(2 * kvh + 1) * LANES] = dq0
                dqr_ref[h, :, (2 * kvh + 1) * LANES:(2 * kvh + 2) * LANES] = dq1
                dk_ref[h, :, m * LANES:(m + 1) * LANES] += _fold_head(_dot(dsb[b], qs_ref[h, kvh]), kvh, lo2)
                dv_ref[h, :, m * LANES:(m + 1) * LANES] += _fold_head(_dot(pb[b], dos_ref[h, kvh]), kvh, lo2)

            scores(0)
            for kvh in range(N_KV):
                if kvh + 1 < N_KV:
                    scores(kvh + 1)
                softmax(kvh)
                grads(kvh)
            dq, dqg = _qk_prep_bwd(dqr_ref[h], qv, q_rstd, qg_ref[...], cosf, sinv, iq_ref[...], iqt_ref[...], lane)
            dq_ref[rows, :] = dq.astype(BF16)
            dqg_ref[...] += dqg
            dkp_ref[rows, :] = dk_ref[h, 0:WINDOW, :]
            dkc_ref[rows, :] = dk_ref[h, WINDOW:2 * WINDOW, :]
            dvp_ref[rows, :] = dv_ref[h, 0:WINDOW, :]
            dvc_ref[rows, :] = dv_ref[h, WINDOW:2 * WINDOW, :]
            kc_ref[0:WINDOW, :] = kr
            vc_ref[0:WINDOW, :] = v_ref[rows, :]

        for h in range(bps):
            one_block(h)

    bps = ATTN_BLOCKS_PER_STEP
    rows_step = bps * WINDOW
    blk = lambda w: pl.BlockSpec((rows_step, w), lambda s, n: (s * (nblk // bps) + n, 0))
    pos = pl.BlockSpec((rows_step, LANES), lambda s, n: (n, 0))
    kv_out = jax.ShapeDtypeStruct((T, KV_W), F32)
    stage = lambda dt: pltpu.VMEM((bps, 2, 2 * WINDOW, 4 * WINDOW), dt)
    return _call(
        body, phases=phases, name="attn_bwd", grid=(n_seq, nblk // bps),
        in_specs=[pl.BlockSpec(memory_space=pltpu.SMEM), blk(W), blk(W), blk(KV_W), blk(KV_W), _const((1, W)),
                  _const((1, KV_W)), pos, pos, _const((W, LANES)), _const((LANES, W)), _const((KV_W, LANES)),
                  _const((LANES, KV_W))],
        out_specs=[blk(W), blk(KV_W), blk(KV_W), blk(KV_W), blk(KV_W), _const((1, W)), _const((N_HEADS, LANES))],
        out_shape=[jax.ShapeDtypeStruct((T, W), BF16), kv_out, kv_out, kv_out, kv_out,
                   jax.ShapeDtypeStruct((1, W), F32), jax.ShapeDtypeStruct((N_HEADS, LANES), F32)],
        scratch_shapes=[pltpu.VMEM((2 * WINDOW, KV_W), F32), pltpu.VMEM((2 * WINDOW, KV_W), F32),
                        pltpu.VMEM((bps, WINDOW, W), F32), pltpu.VMEM((bps, 2 * WINDOW, KV_W), F32),
                        pltpu.VMEM((bps, 2 * WINDOW, KV_W), F32), stage(F32), stage(F32), stage(BF16), stage(BF16),
                        pltpu.VMEM((bps, N_KV, 4 * WINDOW, LANES), BF16),
                        pltpu.VMEM((bps, N_KV, 4 * WINDOW, LANES), BF16),
                        pltpu.VMEM((bps, N_KV, 2 * WINDOW, LANES), BF16),
                        pltpu.VMEM((bps, N_KV, 2 * WINDOW, LANES), BF16)],
    )(sinks, do, q, k, v, qg, kg, cosf, sins, ind_q, ind_qt, ind_k, ind_kt)


def _kv_bwd(dkc, dkp, dvc, dvp, k, kg, cosf, sins, ind_k, ind_kt, n_seq, S, phases=()):
    T = k.shape[0]
    nblk = S // WINDOW
    nb = min(KV_BLOCKS_PER_STEP, nblk)
    rows, nt = nb * WINDOW, nblk // nb

    def body(dkc_ref, dkp_ref, dkn_ref, dvc_ref, dvp_ref, dvn_ref, k_ref, kg_ref, cos_ref, sin_ref, ik_ref, ikt_ref,
             dk_ref, dv_ref, dkg_ref):
        s_id, n = pl.program_id(0), pl.program_id(1)

        @pl.when((s_id == 0) & (n == 0))
        def _():
            dkg_ref[...] = jnp.zeros_like(dkg_ref)

        blk = n * nb + lax.broadcasted_iota(jnp.int32, (rows, KV_W), 0) // WINDOW
        has_next = blk < nblk - 1

        def from_next(part_ref, next_ref):
            moved = jnp.concatenate([part_ref[WINDOW:rows, :], next_ref[...]], axis=0) if nb > 1 else next_ref[...]
            return jnp.where(has_next, moved, 0.0)

        lane = lax.broadcasted_iota(jnp.int32, (rows, KV_W), 1)
        dkr = dkc_ref[...] + from_next(dkp_ref, dkn_ref)
        dv_ref[...] = (dvc_ref[...] + from_next(dvp_ref, dvn_ref)).astype(BF16)
        cosf, sinv = jnp.tile(cos_ref[...], (1, KV_W // LANES)), jnp.tile(sin_ref[...], (1, KV_W // LANES))
        kv = k_ref[...]
        _, rstd = _qk_prep(kv, kg_ref[...], cosf, sinv, ik_ref[...], ikt_ref[...], lane)
        dk, dkg = _qk_prep_bwd(dkr, kv, rstd, kg_ref[...], cosf, sinv, ik_ref[...], ikt_ref[...], lane)
        dk_ref[...] = dk.astype(BF16)
        dkg_ref[...] += dkg

    cur = pl.BlockSpec((rows, KV_W), lambda s, n: (s * nt + n, 0))
    nxt = pl.BlockSpec((WINDOW, KV_W), lambda s, n: (s * nblk + jnp.minimum((n + 1) * nb, nblk - 1), 0))
    pos = pl.BlockSpec((rows, LANES), lambda s, n: (n, 0))
    return _call(
        body, phases=phases, name="kv_bwd", grid=(n_seq, nt),
        in_specs=[cur, cur, nxt, cur, cur, nxt, cur, _const((1, KV_W)), pos, pos, _const((KV_W, LANES)),
                  _const((LANES, KV_W))],
        out_specs=[cur, cur, _const((1, KV_W))],
        out_shape=[jax.ShapeDtypeStruct((T, KV_W), BF16), jax.ShapeDtypeStruct((T, KV_W), BF16),
                   jax.ShapeDtypeStruct((1, KV_W), F32)],
    )(dkc, dkp, dkp, dvc, dvp, dvp, k, kg, cosf, sins, ind_k, ind_kt)


def _merge_fwd(x, ya, o, ga, gb, w_rnn, w_attn, w_out, tm, phases=()):
    T = x.shape[0]
    W = D_MODEL

    def body(x_ref, ya_ref, o_ref, ga_ref, gb_ref, wr_ref, wa_ref, wo_ref, x1_ref, mg_ref):
        y_a = _dot(ya_ref[...], wr_ref[...])
        y_b = _dot(o_ref[...], wa_ref[...])
        mg = (_sigmoid(ga_ref[...]) * y_a + _sigmoid(gb_ref[...]) * y_b).astype(BF16)
        mg_ref[...] = mg
        x1_ref[...] = x_ref[...] + _dot(mg, wo_ref[...])

    row = pl.BlockSpec((tm, W), lambda i: (i, 0))
    sq = _const((W, W))
    return _call(
        body, phases=phases, name="merge_fwd", grid=(T // tm,),
        in_specs=[row, row, row, row, row, sq, sq, sq], out_specs=[row, row],
        out_shape=[jax.ShapeDtypeStruct((T, W), F32), jax.ShapeDtypeStruct((T, W), BF16)],
    )(x, ya, o, ga, gb, w_rnn, w_attn, w_out)


def _merge_bwd(dx1, ga, gb, ya, o, w_rnn, w_attn, w_out, tm, phases=()):
    T = dx1.shape[0]
    W = D_MODEL

    def body(dx1_ref, ga_ref, gb_ref, ya_ref, o_ref, wr_ref, wa_ref, wo_ref,
             dga_ref, dgb_ref, dya_ref, dyb_ref, dyain_ref, do_ref):
        dm = _dot_nt(dx1_ref[...].astype(BF16), wo_ref[...])
        sa = _sigmoid(ga_ref[...])
        sb = _sigmoid(gb_ref[...])
        dga_ref[...] = (dm * _dot(ya_ref[...], wr_ref[...]) * (sa * (1.0 - sa))).astype(BF16)
        dgb_ref[...] = (dm * _dot(o_ref[...], wa_ref[...]) * (sb * (1.0 - sb))).astype(BF16)
        dya = (dm * sa).astype(BF16)
        dyb = (dm * sb).astype(BF16)
        dya_ref[...] = dya
        dyb_ref[...] = dyb
        dyain_ref[...] = _dot_nt(dya, wr_ref[...])
        do_ref[...] = _dot_nt(dyb, wa_ref[...])

    row = pl.BlockSpec((tm, W), lambda i: (i, 0))
    sq = _const((W, W))
    b16 = jax.ShapeDtypeStruct((T, W), BF16)
    f32 = jax.ShapeDtypeStruct((T, W), F32)
    return _call(
        body, phases=phases, name="merge_bwd", grid=(T // tm,),
        in_specs=[row, row, row, row, row, sq, sq, sq], out_specs=[row] * 6,
        out_shape=[b16, b16, b16, b16, f32, f32],
    )(dx1, ga, gb, ya, o, w_rnn, w_attn, w_out)


def _mlp_fwd(x1, g_mlp, w_up, w_down, tm, phases=()):
    T = x1.shape[0]
    W = D_MODEL

    def body(x_ref, g_ref, wu_ref, wd_ref, x2_ref, hm_ref, u_ref, act_ref):
        xv = x_ref[...]
        hm, _ = _rms_fwd(xv, g_ref[...])
        hmb = hm.astype(BF16)
        hm_ref[...] = hmb
        for j in range(N_CHIPS):
            u = _dot(hmb, wu_ref[j])
            u_ref[:, j * W:(j + 1) * W] = u
            ru = jnp.maximum(u, 0.0)
            act_ref[:, j * W:(j + 1) * W] = (ru * ru).astype(BF16)
        x2_ref[...] = xv + _dot(act_ref[...], wd_ref[...])

    row = lambda w: pl.BlockSpec((tm, w), lambda i: (i, 0))
    return _call(
        body, phases=phases, name="mlp_fwd", grid=(T // tm,),
        in_specs=[row(W), _const((1, W)), _const((N_CHIPS, W, W)), _const((D_FF, W))],
        out_specs=[row(W), row(W), row(D_FF), row(D_FF)],
        out_shape=[jax.ShapeDtypeStruct((T, W), F32), jax.ShapeDtypeStruct((T, W), BF16),
                   jax.ShapeDtypeStruct((T, D_FF), F32), jax.ShapeDtypeStruct((T, D_FF), BF16)],
    )(x1, g_mlp, w_up, w_down)


def _mlp_bwd(dx2, u, x1, g_mlp, w_up, w_down, tm, phases=()):
    T = x1.shape[0]
    W = D_MODEL

    def body(dx2_ref, u_ref, x_ref, g_ref, wu_ref, wd_ref, dx1_ref, du_ref, dg_ref):
        @pl.when(pl.program_id(0) == 0)
        def _():
            dg_ref[...] = jnp.zeros_like(dg_ref)

        dx2 = dx2_ref[...]
        dact = _dot_nt(dx2.astype(BF16), wd_ref[...])
        du_ref[...] = (dact * (2.0 * jnp.maximum(u_ref[...], 0.0))).astype(BF16)
        dhm = jnp.zeros((tm, W), F32)
        for j in range(N_CHIPS):
            dhm = dhm + _dot_nt(du_ref[:, j * W:(j + 1) * W], wu_ref[j])
        xv = x_ref[...]
        g = g_ref[...]
        _, r = _rms_fwd(xv, g)
        dx, dg = _rms_bwd(dhm, xv, r, g)
        dx1_ref[...] = dx2 + dx
        dg_ref[...] += dg

    row = lambda w: pl.BlockSpec((tm, w), lambda i: (i, 0))
    return _call(
        body, phases=phases, name="mlp_bwd", grid=(T // tm,),
        in_specs=[row(W), row(D_FF), row(W), _const((1, W)), _const((N_CHIPS, W, W)), _const((D_FF, W))],
        out_specs=[row(W), row(D_FF), _const((1, W))],
        out_shape=[jax.ShapeDtypeStruct((T, W), F32), jax.ShapeDtypeStruct((T, D_FF), BF16),
                   jax.ShapeDtypeStruct((1, W), F32)],
    )(dx2, u, x1, g_mlp, w_up, w_down)


def _ple_loss(x2, p, target, g_ple, w_gate, w_proj, tm, phases=()):
    T = x2.shape[0]
    W = D_MODEL
    cw = W // N_CHIPS

    def body(x_ref, p_ref, t_ref, g_ref, wg_ref, wp_ref, loss_ref, dx2_ref, pb_ref, de_ref, hp_ref, dtg_ref, dg_ref):
        @pl.when(pl.program_id(0) == 0)
        def _():
            dg_ref[...] = jnp.zeros_like(dg_ref)
            loss_ref[...] = jnp.zeros_like(loss_ref)

        xv = x_ref[...]
        g = g_ref[...]
        pb = p_ref[...].astype(BF16)
        pb_ref[...] = pb
        e = jnp.concatenate([_dot(pb, wp_ref[j]) for j in range(N_CHIPS)], axis=1)
        hp, r = _rms_fwd(xv, g)
        hpb = hp.astype(BF16)
        hp_ref[...] = hpb
        sg = _sigmoid(_dot(hpb, wg_ref[...]))
        diff = (xv + e * sg) - t_ref[...]
        loss_ref[...] += jnp.sum(diff * diff) * (0.5 / W)
        dx3 = diff * (1.0 / W)
        de_ref[...] = (dx3 * sg).astype(BF16)
        dtg = (dx3 * e * (sg * (1.0 - sg))).astype(BF16)
        dtg_ref[...] = dtg
        dx, dg = _rms_bwd(_dot_nt(dtg, wg_ref[...]), xv, r, g)
        dx2_ref[...] = dx3 + dx
        dg_ref[...] += dg

    row = lambda w: pl.BlockSpec((tm, w), lambda i: (i, 0))
    b16 = lambda w: jax.ShapeDtypeStruct((T, w), BF16)
    return _call(
        body, phases=phases, name="ple_loss", grid=(T // tm,),
        in_specs=[row(W), row(PLE_DIM), row(W), _const((1, W)), _const((W, W)), _const((N_CHIPS, PLE_DIM, cw))],
        out_specs=[_const((8, LANES)), row(W), row(PLE_DIM), row(W), row(W), row(W), _const((1, W))],
        out_shape=[jax.ShapeDtypeStruct((8, LANES), F32), jax.ShapeDtypeStruct((T, W), F32), b16(PLE_DIM),
                   b16(W), b16(W), b16(W), jax.ShapeDtypeStruct((1, W), F32)],
    )(x2, p, target, g_ple, w_gate, w_proj)


def _adamw(w, g, m, v, name, tr, phases=()):
    R, C = w.shape
    c1 = 1.0 / (1.0 - ADAM_B1 ** ADAM_STEP)
    c2 = 1.0 / (1.0 - ADAM_B2 ** ADAM_STEP)

    def body(w_ref, g_ref, m_ref, v_ref, go_ref, d_ref, nm_ref, nv_ref):
        gv = g_ref[...]
        go_ref[...] = gv
        nm = ADAM_B1 * m_ref[...] + (1.0 - ADAM_B1) * gv
        nv = ADAM_B2 * v_ref[...] + (1.0 - ADAM_B2) * (gv * gv)
        nm_ref[...] = nm
        nv_ref[...] = nv
        d_ref[...] = (-ADAM_LR) * ((nm * c1) / (jnp.sqrt(nv * c2) + ADAM_EPS) + ADAM_WD * w_ref[...])

    row = pl.BlockSpec((tr, C), lambda i: (i, 0))
    sds = jax.ShapeDtypeStruct((R, C), F32)
    return _call(
        body, phases=phases, name=name, grid=(R // tr,), in_specs=[row] * 4, out_specs=[row] * 4,
        out_shape=[sds] * 4,
    )(w, g, m, v)


def _indicator(width):
    ind = np.zeros((width, LANES), np.float32)
    ind[np.arange(width), np.arange(width) // HEAD_DIM] = 1.0
    return jnp.asarray(ind, BF16), jnp.asarray(ind.T, BF16)


def _rope_tables(S):
    inv = ROPE_THETA ** (-jnp.arange(0, HEAD_DIM, 2, dtype=F32) / HEAD_DIM)
    ang = jnp.arange(S, dtype=F32)[:, None] * inv[None, :]
    cos, sin = jnp.cos(ang), jnp.sin(ang)
    cosf = jnp.tile(jnp.concatenate([cos, cos], axis=1), (1, LANES // HEAD_DIM))
    sins = jnp.tile(jnp.concatenate([-sin, sin], axis=1), (1, LANES // HEAD_DIM))
    return cosf, sins


def _pair_blockdiag(w):
    w4 = w.reshape(8, 2, HEAD_DIM, HEAD_DIM)
    eye = jnp.eye(2, dtype=w.dtype)
    return jnp.einsum("bpij,pq->bpiqj", w4, eye).reshape(8, LANES, LANES)


def _pair_blockdiag_extract(g):
    g5 = g.reshape(8, 2, HEAD_DIM, 2, HEAD_DIM)
    return jnp.stack([g5[:, 0, :, 0, :], g5[:, 1, :, 1, :]], axis=1).reshape(16, HEAD_DIM, HEAD_DIM)


def _pair_sum(parts, sibs, name):
    n = len(parts)
    dims = [(p.shape[1] // 2, p.shape[2]) for p in parts]

    def body(*refs):
        p_r, s_r, send_r, own_r, mine_r, sem = (refs[0:n], refs[n:2 * n], refs[2 * n:3 * n], refs[3 * n:4 * n],
                                                refs[4 * n:5 * n], refs[5 * n])
        x, y, c, chips = _mesh_pos()
        me = 2 * x + y
        loads = []
        for i, (R, _) in enumerate(dims):
            mine, _ = _half_rows(c, R)
            cp = pltpu.make_async_copy(p_r[i].at[:, mine, :], mine_r[i], sem.at[i])
            cp.start()
            loads.append(cp)
        for i in range(n):
            loads[i].wait()
            for j, (cx, cy) in enumerate(chips):
                k = 2 * cx + cy
                send_r[i][j] = (mine_r[i][k] + s_r[i][k]).astype(BF16)
            own_r[i][...] = mine_r[i][me] + s_r[i][me]

    vm = pl.BlockSpec(memory_space=pltpu.VMEM)
    out = pl.pallas_call(
        body, name=name, in_specs=[pl.BlockSpec(memory_space=pl.ANY)] * n + [vm] * n, out_specs=[vm] * (2 * n),
        out_shape=[jax.ShapeDtypeStruct((3, R, C), BF16) for R, C in dims]
        + [jax.ShapeDtypeStruct((R, C), F32) for R, C in dims],
        scratch_shapes=[pltpu.VMEM((N_CHIPS, R, C), F32) for R, C in dims] + [pltpu.SemaphoreType.DMA((n,))],
        compiler_params=pltpu.CompilerParams(vmem_limit_bytes=VMEM_LIMIT),
    )(*parts, *sibs)
    return out[:n], out[n:]


def _chip_sum(owns, recvs, name):
    n = len(owns)
    dims = [o.shape for o in owns]

    def body(*refs):
        own_r, recv_r, red_r, stage_r, sem = refs[0:n], refs[n:2 * n], refs[2 * n:3 * n], refs[3 * n:4 * n], refs[4 * n]
        x, y, c, _ = _mesh_pos()
        me = 2 * x + y
        stores = []
        for i, (R, _) in enumerate(dims):
            for k_me in range(N_CHIPS):

                @pl.when(me == k_me)
                def _():
                    acc = None
                    for k in range(N_CHIPS):
                        slot = ((k // 2) ^ (k_me // 2)) + 2 * ((k % 2) ^ (k_me % 2)) - 1
                        term = own_r[i][...] if k == k_me else recv_r[i][slot].astype(F32)
                        acc = term if acc is None else acc + term
                    stage_r[i][...] = acc

            mine, _ = _half_rows(c, R)
            cp = pltpu.make_async_copy(stage_r[i], red_r[i].at[mine, :], sem.at[i])
            cp.start()
            stores.append(cp)
        for cp in stores:
            cp.wait()

    vm = pl.BlockSpec(memory_space=pltpu.VMEM)
    return pl.pallas_call(
        body, name=name, in_specs=[vm] * (2 * n), out_specs=[pl.BlockSpec(memory_space=pl.ANY)] * n,
        out_shape=[jax.ShapeDtypeStruct((2 * R, C), F32) for R, C in dims],
        scratch_shapes=[pltpu.VMEM((R, C), F32) for R, C in dims] + [pltpu.SemaphoreType.DMA((n,))],
        compiler_params=pltpu.CompilerParams(vmem_limit_bytes=VMEM_LIMIT),
    )(*owns, *recvs)


def _gather_bf16(shard, name):
    R2, C = shard.shape
    R = R2 // 2
    H = R // 2

    def body(s_ref, o_ref, send_sems, recv_sems):
        x, y, c, _ = _mesh_pos()
        me, chip_x, chip_y, chip_d = 2 * x + y, 2 * (1 - x) + y, 2 * x + (1 - y), 2 * (1 - x) + (1 - y)
        to_x, to_y, me_dev, sibling = (1 - x, y, c), (x, 1 - y, c), (x, y, c), (x, y, 1 - c)

        def rows(core, off, n):
            return pl.ds(pl.multiple_of(core * R + off, H), n)

        def copy(k, chip, rws, to):
            blk = o_ref.at[chip, rws]
            return _remote(blk, blk, (send_sems.at[k], recv_sems.at[k]), to)

        piece, half_a, half_b = rows(c, 0, R), rows(c, 0, H), rows(c, H, H)
        o_ref[me] = s_ref[...].astype(BF16)
        sends = [copy(0, me, piece, to_x), copy(1, me, piece, to_y)]
        for cp in sends:
            cp.start()
        arrivals = [(0, chip_x, piece, (2, half_a, to_y)), (1, chip_y, piece, (3, half_b, to_x)),
                    (2, chip_d, half_a, None), (3, chip_d, half_b, None)]
        for k, chip, rws, onward in arrivals:
            copy(k, chip, rws, me_dev).wait_recv()
            if onward is not None:
                sends.append(copy(onward[0], chip, onward[1], onward[2]))
                sends[-1].start()
            sends.append(copy(4 + k, chip, rws, sibling))
            sends[-1].start()
        for k, chip, rws in [(4, chip_x, rows(1 - c, 0, R)), (5, chip_y, rows(1 - c, 0, R)),
                             (6, chip_d, rows(1 - c, 0, H)), (7, chip_d, rows(1 - c, H, H))]:
            copy(k, chip, rws, me_dev).wait_recv()
        for cp in sends:
            cp.wait_send()

    return pl.pallas_call(
        body, name=name, out_shape=jax.ShapeDtypeStruct((N_CHIPS, R2, C), BF16),
        in_specs=[pl.BlockSpec(memory_space=pltpu.VMEM)], out_specs=pl.BlockSpec(memory_space=pltpu.VMEM),
        scratch_shapes=[pltpu.SemaphoreType.DMA((8,)), pltpu.SemaphoreType.DMA((8,))],
        compiler_params=pltpu.CompilerParams(vmem_limit_bytes=VMEM_LIMIT),
    )(shard)


def _pair_exchange_sum(partial, name):
    _, R2, C = partial.shape
    R = R2 // 2

    def body(p_ref, send_ref, own_ref, mine_ref, sib_ref, loc_sems, send_sems, recv_sems):
        x, y, c, chips = _mesh_pos()
        me = 2 * x + y
        mine, theirs = _half_rows(c, R)
        order = [2 * cx + cy for cx, cy in chips] + [me]
        locs, pairs = [], []
        for i, k in enumerate(order):
            loc = pltpu.make_async_copy(p_ref.at[k, mine, :], mine_ref.at[i], loc_sems.at[i])
            pair = _remote(p_ref.at[k, theirs, :], sib_ref.at[i], (send_sems.at[i], recv_sems.at[i]), (x, y, 1 - c))
            loc.start()
            pair.start()
            locs.append(loc)
            pairs.append(pair)
        for i in range(N_CHIPS):
            locs[i].wait()
            pairs[i].wait_recv()
            total = mine_ref[i] + sib_ref[i]
            if i < 3:
                send_ref[i] = total.astype(BF16)
            else:
                own_ref[...] = total
        for pair in pairs:
            pair.wait_send()

    vm = pl.BlockSpec(memory_space=pltpu.VMEM)
    return pl.pallas_call(
        body, name=name, in_specs=[pl.BlockSpec(memory_space=pl.ANY)], out_specs=[vm, vm],
        out_shape=[jax.ShapeDtypeStruct((3, R, C), BF16), jax.ShapeDtypeStruct((R, C), F32)],
        scratch_shapes=[pltpu.VMEM((N_CHIPS, R, C), F32), pltpu.VMEM((N_CHIPS, R, C), F32),
                        pltpu.SemaphoreType.DMA((N_CHIPS,)), pltpu.SemaphoreType.DMA((N_CHIPS,)),
                        pltpu.SemaphoreType.DMA((N_CHIPS,))],
        compiler_params=pltpu.CompilerParams(vmem_limit_bytes=VMEM_LIMIT),
    )(partial)


def _allreduce_small(buf, name):
    rows, width = buf.shape
    h = rows // 2

    def body(b_ref, o_ref, sib_ref, pair_ref, in_ref, pair_sems, send_sems, recv_sems, fin_sems):
        x, y, c, chips = _mesh_pos()
        me = 2 * x + y
        mine, theirs = _half_rows(c, h)
        sibling = (x, y, 1 - c)
        pair = _remote(b_ref.at[theirs], sib_ref, (pair_sems.at[0], pair_sems.at[1]), sibling)
        pair.start()
        pair.wait()
        pair_ref[...] = b_ref[mine, :] + sib_ref[...]
        sends = []
        for j, (cx, cy) in enumerate(chips):
            cp = _remote(pair_ref, in_ref.at[j], (send_sems.at[j], recv_sems.at[j]), (cx, cy, c))
            cp.start()
            sends.append(cp)
        for cp in sends:
            cp.wait_recv()
        acc = None
        for k in range(N_CHIPS):
            term = jnp.where(me == k, pair_ref[...], in_ref[_peer_slot(k, x, y)])
            acc = term if acc is None else acc + term
        o_ref[mine, :] = acc
        fin = _remote(o_ref.at[mine], o_ref.at[mine], (fin_sems.at[0], fin_sems.at[1]), sibling)
        fin.start()
        fin.wait_send()
        _remote(o_ref.at[theirs], o_ref.at[theirs], (fin_sems.at[0], fin_sems.at[1]), sibling).wait_recv()
        for cp in sends:
            cp.wait_send()

    return pl.pallas_call(
        body, name=name, out_shape=jax.ShapeDtypeStruct((rows, width), F32),
        in_specs=[pl.BlockSpec(memory_space=pltpu.VMEM)], out_specs=pl.BlockSpec(memory_space=pltpu.VMEM),
        scratch_shapes=[pltpu.VMEM((h, width), F32), pltpu.VMEM((h, width), F32), pltpu.VMEM((3, h, width), F32),
                        pltpu.SemaphoreType.DMA((2,)), pltpu.SemaphoreType.DMA((3,)), pltpu.SemaphoreType.DMA((3,)),
                        pltpu.SemaphoreType.DMA((2,))],
        compiler_params=pltpu.CompilerParams(vmem_limit_bytes=VMEM_LIMIT),
    )(buf)


def _adamw_small(ws, gs, ms, vs):
    n = len(ws)
    c1 = 1.0 / (1.0 - ADAM_B1 ** ADAM_STEP)
    c2 = 1.0 / (1.0 - ADAM_B2 ** ADAM_STEP)

    def body(*refs):
        w_r, g_r, m_r, v_r = refs[0:n], refs[n:2 * n], refs[2 * n:3 * n], refs[3 * n:4 * n]
        d_r, nm_r, nv_r = refs[4 * n:5 * n], refs[5 * n:6 * n], refs[6 * n:7 * n]
        for i in range(n):
            gv = g_r[i][...]
            nm = ADAM_B1 * m_r[i][...] + (1.0 - ADAM_B1) * gv
            nv = ADAM_B2 * v_r[i][...] + (1.0 - ADAM_B2) * (gv * gv)
            nm_r[i][...] = nm
            nv_r[i][...] = nv
            d_r[i][...] = (-ADAM_LR) * ((nm * c1) / (jnp.sqrt(nv * c2) + ADAM_EPS) + ADAM_WD * w_r[i][...])

    vm = pl.BlockSpec(memory_space=pltpu.VMEM)
    sds = [jax.ShapeDtypeStruct(w.shape, F32) for w in ws]
    out = pl.pallas_call(body, name="adamw_small", in_specs=[vm] * (4 * n), out_specs=[vm] * (3 * n),
                         out_shape=sds * 3)(*ws, *gs, *ms, *vs)
    return out[0:n], out[n:2 * n], out[2 * n:3 * n]


_BIG = ("w_in", "w_rnn_proj", "w_attn_proj", "w_out", "w_up", "w_down", "w_ple_gate", "w_ple_proj")
_SMALL = ("g_mix", "conv_w", "conv_b", "w_rg", "b_rg", "w_ig", "b_ig", "lru_lambda", "q_gain", "k_gain", "sinks",
          "g_mlp", "g_ple")
_WEIGHTS = ("g_mix", "w_in", "conv_w", "conv_b", "w_rg", "b_rg", "w_ig", "b_ig", "lru_lambda", "w_rnn_proj",
            "q_gain", "k_gain", "sinks", "w_attn_proj", "w_out", "g_mlp", "w_up", "w_down", "g_ple", "w_ple_gate",
            "w_ple_proj")


def _pad_row(v):
    v = v.reshape(1, -1)
    return jnp.pad(v, ((0, 0), (0, D_MODEL - v.shape[1])))


def kernel(x, p, g_mix, w_in, conv_w, conv_b, w_rg, b_rg, w_ig, b_ig, lru_lambda, w_rnn_proj, q_gain, k_gain, sinks, w_attn_proj, w_out, g_mlp, w_up, w_down, g_ple, w_ple_gate, w_ple_proj, loss_target, m_g_mix, m_w_in, m_conv_w, m_conv_b, m_w_rg, m_b_rg, m_w_ig, m_b_ig, m_lru_lambda, m_w_rnn_proj, m_q_gain, m_k_gain, m_sinks, m_w_attn_proj, m_w_out, m_g_mlp, m_w_up, m_w_down, m_g_ple, m_w_ple_gate, m_w_ple_proj, v_g_mix, v_w_in, v_conv_w, v_conv_b, v_w_rg, v_b_rg, v_w_ig, v_b_ig, v_lru_lambda, v_w_rnn_proj, v_q_gain, v_k_gain, v_sinks, v_w_attn_proj, v_w_out, v_g_mlp, v_w_up, v_w_down, v_g_ple, v_w_ple_gate, v_w_ple_proj):
    w = dict(g_mix=g_mix, w_in=w_in, conv_w=conv_w, conv_b=conv_b, w_rg=w_rg, b_rg=b_rg, w_ig=w_ig, b_ig=b_ig,
             lru_lambda=lru_lambda, w_rnn_proj=w_rnn_proj, q_gain=q_gain, k_gain=k_gain, sinks=sinks,
             w_attn_proj=w_attn_proj, w_out=w_out, g_mlp=g_mlp, w_up=w_up, w_down=w_down, g_ple=g_ple,
             w_ple_gate=w_ple_gate, w_ple_proj=w_ple_proj)
    m = dict(g_mix=m_g_mix, w_in=m_w_in, conv_w=m_conv_w, conv_b=m_conv_b, w_rg=m_w_rg, b_rg=m_b_rg, w_ig=m_w_ig,
             b_ig=m_b_ig, lru_lambda=m_lru_lambda, w_rnn_proj=m_w_rnn_proj, q_gain=m_q_gain, k_gain=m_k_gain,
             sinks=m_sinks, w_attn_proj=m_w_attn_proj, w_out=m_w_out, g_mlp=m_g_mlp, w_up=m_w_up, w_down=m_w_down,
             g_ple=m_g_ple, w_ple_gate=m_w_ple_gate, w_ple_proj=m_w_ple_proj)
    v = dict(g_mix=v_g_mix, w_in=v_w_in, conv_w=v_conv_w, conv_b=v_conv_b, w_rg=v_w_rg, b_rg=v_b_rg, w_ig=v_w_ig,
             b_ig=v_b_ig, lru_lambda=v_lru_lambda, w_rnn_proj=v_w_rnn_proj, q_gain=v_q_gain, k_gain=v_k_gain,
             sinks=v_sinks, w_attn_proj=v_w_attn_proj, w_out=v_w_out, g_mlp=v_g_mlp, w_up=v_w_up, w_down=v_w_down,
             g_ple=v_g_ple, w_ple_gate=v_w_ple_gate, w_ple_proj=v_w_ple_proj)
    n_seq, S, _ = x.shape
    T = n_seq * S
    chip = 2 * lax.axis_index("x") + lax.axis_index("y")

    tm, tm_rnn = TM, TM_RNN
    xf, pf, tf = x.reshape(T, D_MODEL), p.reshape(T, PLE_DIM), loss_target.reshape(T, D_MODEL)
    first = lambda outs: [o[0] for o in outs]

    w_in_g = _gather_bf16(w["w_in"][0], "gather_w_in")
    wb = {name: w[name][0].astype(BF16) for name in _BIG if name != "w_in"}
    grp_mix, grp_mlp, grp_ple = ("w_rnn_proj", "w_attn_proj", "w_out"), ("w_up", "w_down"), ("w_ple_gate", "w_ple_proj")

    wb["conv_w"] = jnp.pad(conv_w[0], ((0, 16 - CONV_W), (0, 0)))

    cosf, sins = _rope_tables(S)
    ind_q, ind_qt = _indicator(D_MODEL)
    ind_k, ind_kt = _indicator(KV_W)
    wrg2 = _pair_blockdiag(w_rg[0]).astype(BF16)
    wig2 = _pair_blockdiag(w_ig[0]).astype(BF16)
    qg = jnp.tile(q_gain, (1, N_HEADS))
    kg = jnp.tile(k_gain, (1, N_KV))
    sk = sinks.reshape(N_HEADS)
    attn_c = (qg, kg, sk, cosf, sins, ind_q, ind_qt, ind_k, ind_kt, n_seq, S)

    (h0, xr, gr, zq, zk, zv, ga, gb), ph = _inproj_fwd(xf, g_mix, w_in_g, tm,
                                                     phases=[_ph_gather_send(wb[n]) for n in grp_mix + ("conv_w",)])
    g_small = first(ph)
    o, ph = _attn_fwd(zq, zk, zv, *attn_c,
                      phases=[_ph_gather_pass(g) for g in g_small]
                      + [_ph_gather_send(wb[n]) for n in ("w_up",) + grp_ple])
    g_small, (wu, wpg, wpp) = first(ph[:4]), first(ph[4:])
    cw_full = g_small[3][:, :CONV_W, :].transpose(1, 0, 2).reshape(CONV_W, D_MODEL)
    rnn_w = (cw_full, conv_b, wrg2, b_rg, wig2, b_ig, lru_lambda)
    (xc, h, *gates, ya), ph = _rnn_fwd(xr, gr, *rnn_w, n_seq, S, tm_rnn,
                               phases=[_ph_gather_pass(g) for g in (wu, wpg, wpp)]
                               + [_ph_gather_send(wb["w_down"])])
    (wu, wpg, wpp), wd = first(ph[:3]), ph[3][0]
    wr, wa, wo = (g.reshape(D_MODEL, D_MODEL) for g in g_small[:3])
    wpg = wpg.reshape(D_MODEL, D_MODEL)
    (x1, merged), ph = _merge_fwd(xf, ya, o, ga, gb, wr, wa, wo, tm, phases=[_ph_gather_pass(wd)])
    wd = ph[0][0].reshape(D_FF, D_MODEL)
    (x2, hm, u, act), _ = _mlp_fwd(x1, g_mlp, wu, wd, tm // 2)
    (loss_t, dx2, pb, de, hp, dtg, dg_ple), _ = _ple_loss(x2, pf, tf, g_ple, wpg, wpp, tm)

    chipmajor = lambda g: g.reshape(N_CHIPS, g.shape[-2] // N_CHIPS, g.shape[-1]) if g.ndim == 2 else g
    tmw = min(2 * tm, T)
    dw_pp = _wgrad(pb, de, "wgrad_ple_proj", False, D_MODEL, tmw)[0]
    part_ple = [chipmajor(_wgrad(hp, dtg, "wgrad_ple_gate", False, D_MODEL, tmw)[0]),
                dw_pp.reshape(PLE_DIM, N_CHIPS, D_MODEL // N_CHIPS).transpose(1, 0, 2)]
    (dx1, du, dg_mlp), ph = _mlp_bwd(dx2, u, x1, g_mlp, wu, wd, tm // 2, phases=[_ph_pair_send(g) for g in part_ple])
    send_ple, own_ple = _pair_sum(part_ple, first(ph), "pair_sum_ple")
    dw_down, ph = _wgrad(act, dx2, "wgrad_down", False, D_MODEL // 2, tmw, phases=[_ph_chip_send(s) for s in send_ple])
    red_ple = _chip_sum(own_ple, first(ph), "chip_sum_ple")
    part_mlp = [_wgrad(hm, du, "wgrad_up", True, D_MODEL, tmw)[0], chipmajor(dw_down)]
    (dga, dgb, dya, dyb, dyain, do), _ = _merge_bwd(dx1, ga, gb, ya, o, wr, wa, wo, tm)
    dw_rnn, ph_up = _wgrad(ya, dya, "wgrad_rnn_proj", False, D_MODEL, tmw, phases=[_ph_pair_send(part_mlp[0])])
    dw_attn, ph_down = _wgrad(o, dyb, "wgrad_attn_proj", False, D_MODEL, tmw, phases=[_ph_pair_send(part_mlp[1])])
    dw_out, ph = _wgrad(merged, dx1, "wgrad_out", False, D_MODEL, tmw, phases=[_ph_half_swap(r) for r in red_ple])
    red_ple = first(ph)
    send_mlp, own_mlp = _pair_sum(part_mlp, [ph_up[0][0], ph_down[0][0]], "pair_sum_mlp")
    part_mix = [chipmajor(dw_rnn), chipmajor(dw_attn), chipmajor(dw_out)]
    (dxr, dgr, vec, dwrg2, dwig2), ph = _rnn_bwd(
        dyain, xr, gr, xc, h, gates, cw_full, wrg2, wig2, lru_lambda, n_seq, S, tm_rnn,
        phases=[_ph_chip_send(s) for s in send_mlp] + [_ph_pair_send(g) for g in part_mix])
    red_mlp = _chip_sum(own_mlp, first(ph[:2]), "chip_sum_mlp")
    send_mix, own_mix = _pair_sum(part_mix, first(ph[2:]), "pair_sum_mix")
    (dq, dkc, dkp, dvc, dvp, dqg, dsk), ph = _attn_bwd(
        do, zq, zk, zv, *attn_c, phases=[_ph_half_swap(r) for r in red_mlp] + [_ph_chip_send(s) for s in send_mix])
    red_mlp = first(ph[:2])
    red_mix = _chip_sum(own_mix, first(ph[2:]), "chip_sum_mix")
    (dk, dv, dkg), _ = _kv_bwd(dkc, dkp, dvc, dvp, zk, kg, cosf, sins, ind_k, ind_kt, n_seq, S)
    dz_parts = [dxr, dgr, dq, dk, dv, dga, dgb]
    send_in, own_in = _pair_exchange_sum(_wgrad_in(h0, dz_parts, tm), "pair_sum_in")
    (grad_x, dg_mix), ph = _inproj_bwd(dz_parts, w_in_g, xf, g_mix, dx1, tm,
                                       phases=[_ph_half_swap(r) for r in red_mix] + [_ph_chip_send(send_in)])
    red_mix = first(ph[:3])
    red_in = _chip_sum([own_in], first(ph[3:]), "chip_sum_in")
    reduced = dict(zip(grp_ple + grp_mlp + grp_mix, red_ple + red_mlp + red_mix))
    grads = {
        "g_mix": dg_mix[0], "g_mlp": dg_mlp[0], "g_ple": dg_ple[0],
        "conv_w": vec[0:CONV_W], "conv_b": vec[4], "b_rg": vec[5], "b_ig": vec[6], "lru_lambda": vec[7],
        "w_rg": _pair_blockdiag_extract(dwrg2), "w_ig": _pair_blockdiag_extract(dwig2),
        "q_gain": dqg.reshape(N_HEADS, HEAD_DIM).sum(0), "k_gain": dkg.reshape(N_KV, HEAD_DIM).sum(0),
        "sinks": dsk.sum(1),
    }

    rows = [grads["conv_w"], _pad_row(grads["conv_b"]), _pad_row(grads["b_rg"]), _pad_row(grads["b_ig"]),
            _pad_row(grads["lru_lambda"]), _pad_row(grads["g_mix"]), _pad_row(grads["g_mlp"]),
            _pad_row(grads["g_ple"]), _pad_row(grads["q_gain"]), _pad_row(grads["k_gain"]), _pad_row(grads["sinks"]),
            _pad_row(loss_t[0:1, 0:1]), jnp.zeros((1, D_MODEL), F32)]
    vecs = jnp.concatenate(rows, axis=0)
    packed = jnp.concatenate([vecs.reshape(-1, LANES), grads["w_rg"].reshape(-1, LANES),
                              grads["w_ig"].reshape(-1, LANES)], axis=0)
    red = _allreduce_small(packed, "allreduce_small")
    nv = vecs.size // LANES
    rvec = red[0:nv].reshape(16, D_MODEL)
    loss = rvec[14, 0]
    nw = grads["w_rg"].size // LANES
    sg = {
        "conv_w": lax.dynamic_slice(rvec[0:CONV_W], (0, chip * (D_MODEL // N_CHIPS)), (CONV_W, D_MODEL // N_CHIPS)),
        "conv_b": rvec[4], "b_rg": rvec[5], "b_ig": rvec[6], "lru_lambda": rvec[7], "g_mix": rvec[8],
        "g_mlp": rvec[9], "g_ple": rvec[10], "q_gain": rvec[11, :HEAD_DIM], "k_gain": rvec[12, :HEAD_DIM],
        "sinks": rvec[13, :N_HEADS], "w_rg": red[nv:nv + nw], "w_ig": red[nv + nw:nv + 2 * nw],
    }
    sg = {k: sg[k].reshape(w[k].shape) for k in _SMALL}
    d_s, m_s, v_s = _adamw_small([w[k] for k in _SMALL], [sg[k] for k in _SMALL], [m[k] for k in _SMALL],
                                 [v[k] for k in _SMALL])
    grad, delta, new_m, new_v = dict(sg), dict(zip(_SMALL, d_s)), dict(zip(_SMALL, m_s)), dict(zip(_SMALL, v_s))

    for name in ("w_ple_proj", "w_up", "w_down", "w_rnn_proj", "w_attn_proj", "w_out", "w_ple_gate", "w_in"):
        shape = w[name].shape
        outs, ph = _adamw(w[name][0], reduced[name], m[name][0], v[name][0], "adamw_" + name, min(ADAMW_ROWS, shape[1] // 2),
                          phases=[_ph_half_swap(r) for r in red_in] if name == "w_ple_proj" else ())
        if name == "w_ple_proj":
            reduced["w_in"] = ph[0][0]
        grad[name], delta[name], new_m[name], new_v[name] = (a.reshape(shape) for a in outs)

    return (loss, grad_x.reshape(x.shape), *[grad[k] for k in _WEIGHTS], *[delta[k] for k in _WEIGHTS],
            *[new_m[k] for k in _WEIGHTS], *[new_v[k] for k in _WEIGHTS])
```

```python
import functools
import math

import numpy as np
import jax
import jax.numpy as jnp
from jax import lax
from jax.experimental import pallas as pl
from jax.experimental.pallas import tpu as pltpu

F32 = jnp.float32
BF16 = jnp.bfloat16

D_MODEL = 1024
N_HEADS = 16
N_KV = 4
HEAD_DIM = 64
KV_W = N_KV * HEAD_DIM
D_FF = 4096
PLE_DIM = 256
WINDOW = 128
CONV_W = 4
LRU_C = 8.0
NORM_EPS = 1e-6
ROPE_THETA = 10000.0
N_CHIPS = 4
IN_TOTAL = 5632
IN_BLK = IN_TOTAL // N_CHIPS
IN_SEGS = (0, 1024, 2048, 3072, 3328, 3584, 4608, 5632)

ADAM_LR = 0.001
ADAM_B1 = 0.9
ADAM_B2 = 0.999
ADAM_EPS = 1e-08
ADAM_WD = 0.01
ADAM_STEP = 10

LANES = 128
V7X_VMEM_BYTES = 64 * 1024 * 1024
VMEM_LIMIT = V7X_VMEM_BYTES - 8 * 1024 * 1024
MESH_ID = pl.DeviceIdType.MESH
TM, TM_RNN, ADAMW_ROWS = 512, 256, 256
ATTN_BLOCKS_PER_STEP = 2
KV_BLOCKS_PER_STEP = 4


def _dot(a, b):
    return jnp.dot(a, b, preferred_element_type=F32)


def _dot_nt(a, b):
    return lax.dot_general(a, b, (((1,), (1,)), ((), ())), preferred_element_type=F32)


def _dot_tn(a, b):
    return lax.dot_general(a, b, (((0,), (0,)), ((), ())), preferred_element_type=F32)


def _split_dot(x, ind):
    hi = x.astype(BF16)
    lo = (x - hi.astype(F32)).astype(BF16)
    return _dot(hi, ind) + _dot(lo, ind)


def _sigmoid(x):
    return 1.0 / (1.0 + jnp.exp(-x))


_GELU_C = math.sqrt(2.0 / math.pi)


def _gelu_and_grad(g):
    inner = _GELU_C * (g + 0.044715 * g * g * g)
    t = jnp.tanh(inner)
    gelu = 0.5 * g * (1.0 + t)
    dgelu = 0.5 * (1.0 + t) + 0.5 * g * (1.0 - t * t) * _GELU_C * (1.0 + 3.0 * 0.044715 * g * g)
    return gelu, dgelu


def _const(shape):
    nd = len(shape)
    return pl.BlockSpec(shape, lambda *_: (0,) * nd)


def _params(n_grid, vmem=VMEM_LIMIT):
    return pltpu.CompilerParams(dimension_semantics=("arbitrary",) * n_grid, vmem_limit_bytes=vmem)


def _rms_fwd(x, g):
    r = lax.rsqrt(jnp.mean(x * x, axis=-1, keepdims=True) + NORM_EPS)
    return (x * r) * g, r


def _rms_bwd(dy, x, r, g):
    dn = dy * g
    dx = r * dn - x * (r * r * r * jnp.mean(dn * x, axis=-1, keepdims=True))
    dg = jnp.sum(dy * (x * r), axis=0, keepdims=True)
    return dx, dg


def _seg_pieces(blk_lo, blk_hi):
    out = []
    for s in range(7):
        lo, hi = max(blk_lo, IN_SEGS[s]), min(blk_hi, IN_SEGS[s + 1])
        if lo < hi:
            out.append((s, lo - IN_SEGS[s], hi - IN_SEGS[s], lo - blk_lo))
    return out


def _mesh_pos():
    x, y, c = lax.axis_index("x"), lax.axis_index("y"), lax.axis_index("c")
    other_chips = [(1 - x, y), (x, 1 - y), (1 - x, 1 - y)]
    return x, y, c, other_chips


def _peer_slot(k, x, y):
    dx = jnp.bitwise_xor(k // 2, x)
    dy = jnp.bitwise_xor(k % 2, y)
    return jnp.maximum(dx + 2 * dy - 1, 0)


def _half_rows(c, R):
    return pl.ds(pl.multiple_of(c * R, R), R), pl.ds(pl.multiple_of((1 - c) * R, R), R)


def _remote(src, dst, sems, to):
    return pltpu.make_async_remote_copy(src_ref=src, dst_ref=dst, send_sem=sems[0], recv_sem=sems[1],
                                        device_id=to, device_id_type=MESH_ID)


class _Phase:
    def __init__(self, ins, inout, outs, n_remote, n_local, build):
        self.ins, self.inout, self.outs = list(ins), list(inout), list(outs)
        self.n_remote, self.n_local, self.build = n_remote, n_local, build


def _ph_gather_send(wb):
    R2, C = wb.shape
    R = R2 // 2

    def build(ins, outs, rsem, lsem):
        (w_ref,), (g_ref,) = ins, outs
        x, y, c, chips = _mesh_pos()
        me = 2 * x + y
        mine, _ = _half_rows(c, R)
        loc = [pltpu.make_async_copy(w_ref, g_ref.at[me], lsem(0))]
        outg = [_remote(w_ref.at[mine], g_ref.at[me, mine], rsem(j), (cx, cy, c)) for j, (cx, cy) in enumerate(chips)]
        inc = [functools.partial(_remote, w_ref.at[mine], g_ref.at[2 * cx + cy, mine], rsem(j), (x, y, c))
               for j, (cx, cy) in enumerate(chips)]
        return loc, outg, inc

    return _Phase([wb], [], [jax.ShapeDtypeStruct((N_CHIPS, R2, C), wb.dtype)], 3, 1, build)


def _ph_gather_pass(gath):
    _, R2, C = gath.shape
    R = R2 // 2

    def build(ins, outs, rsem, lsem):
        (g_ref,) = outs
        x, y, c, chips = _mesh_pos()
        mine, theirs = _half_rows(c, R)
        outg, inc = [], []
        for j, (cx, cy) in enumerate(chips):
            blk = g_ref.at[2 * cx + cy, mine]
            outg.append(_remote(blk, blk, rsem(j), (x, y, 1 - c)))
            got = g_ref.at[2 * cx + cy, theirs]
            inc.append(functools.partial(_remote, got, got, rsem(j), (x, y, c)))
        return [], outg, inc

    return _Phase([], [gath], [], 3, 0, build)


def _tree_rows(R):
    H = R // 2
    rows = lambda core, off, n: pl.ds(pl.multiple_of(core * R + off, H), n)
    return (lambda core: rows(core, 0, R)), (lambda core: rows(core, 0, H)), (lambda core: rows(core, H, H))


def _ph_tree_send(wb):
    R2, C = wb.shape
    piece, _, _ = _tree_rows(R2 // 2)

    def build(ins, outs, rsem, lsem):
        (w_ref,), (g_ref,) = ins, outs
        x, y, c, _ = _mesh_pos()
        me, peers = 2 * x + y, [((1 - x, y, c), 2 * (1 - x) + y), ((x, 1 - y, c), 2 * x + (1 - y))]
        loc = [pltpu.make_async_copy(w_ref, g_ref.at[me], lsem(0))]
        outg = [_remote(w_ref.at[piece(c)], g_ref.at[me, piece(c)], rsem(k), dev) for k, (dev, _) in enumerate(peers)]
        inc = [functools.partial(_remote, w_ref.at[piece(c)], g_ref.at[chip, piece(c)], rsem(k), (x, y, c))
               for k, (_, chip) in enumerate(peers)]
        return loc, outg, inc

    return _Phase([wb], [], [jax.ShapeDtypeStruct((N_CHIPS, R2, C), wb.dtype)], 2, 1, build)


def _ph_tree_relay(gath):
    _, R2, C = gath.shape
    piece, half_a, half_b = _tree_rows(R2 // 2)

    def build(ins, outs, rsem, lsem):
        (g_ref,) = outs
        x, y, c, _ = _mesh_pos()
        chip_x, chip_y, chip_d = 2 * (1 - x) + y, 2 * x + (1 - y), 2 * (1 - x) + (1 - y)
        me_dev, sibling = (x, y, c), (x, y, 1 - c)
        blk = lambda chip, rows: g_ref.at[chip, rows]
        sends = [(blk(chip_x, half_a(c)), (x, 1 - y, c)), (blk(chip_y, half_b(c)), (1 - x, y, c)),
                 (blk(chip_x, piece(c)), sibling), (blk(chip_y, piece(c)), sibling)]
        recvs = [blk(chip_d, half_a(c)), blk(chip_d, half_b(c)), blk(chip_x, piece(1 - c)), blk(chip_y, piece(1 - c))]
        outg = [_remote(b, b, rsem(k), dev) for k, (b, dev) in enumerate(sends)]
        inc = [functools.partial(_remote, b, b, rsem(k), me_dev) for k, b in enumerate(recvs)]
        return [], outg, inc

    return _Phase([], [gath], [], 4, 0, build)


def _ph_tree_finish(gath):
    _, R2, C = gath.shape
    _, half_a, half_b = _tree_rows(R2 // 2)

    def build(ins, outs, rsem, lsem):
        (g_ref,) = outs
        x, y, c, _ = _mesh_pos()
        chip_d = 2 * (1 - x) + (1 - y)
        outg = [_remote(g_ref.at[chip_d, h(c)], g_ref.at[chip_d, h(c)], rsem(k), (x, y, 1 - c))
                for k, h in enumerate((half_a, half_b))]
        inc = [functools.partial(_remote, g_ref.at[chip_d, h(1 - c)], g_ref.at[chip_d, h(1 - c)], rsem(k), (x, y, c))
               for k, h in enumerate((half_a, half_b))]
        return [], outg, inc

    return _Phase([], [gath], [], 2, 0, build)


def _ph_pair_send(partial):
    _, R2, C = partial.shape
    R = R2 // 2

    def build(ins, outs, rsem, lsem):
        (p_ref,), (s_ref,) = ins, outs
        x, y, c, _ = _mesh_pos()
        _, theirs = _half_rows(c, R)
        src = p_ref.at[:, theirs, :]
        return ([], [_remote(src, s_ref, rsem(0), (x, y, 1 - c))],
                [functools.partial(_remote, src, s_ref, rsem(0), (x, y, c))])

    return _Phase([partial], [], [jax.ShapeDtypeStruct((N_CHIPS, R, C), F32)], 1, 0, build)


def _ph_chip_send(sendb):
    def build(ins, outs, rsem, lsem):
        (s_ref,), (r_ref,) = ins, outs
        x, y, c, chips = _mesh_pos()
        outg = [_remote(s_ref.at[j], r_ref.at[j], rsem(j), (cx, cy, c)) for j, (cx, cy) in enumerate(chips)]
        inc = [functools.partial(_remote, s_ref.at[j], r_ref.at[j], rsem(j), (x, y, c)) for j in range(3)]
        return [], outg, inc

    return _Phase([sendb], [], [jax.ShapeDtypeStruct(sendb.shape, sendb.dtype)], 3, 0, build)


def _ph_half_swap(red):
    R2, C = red.shape
    R = R2 // 2

    def build(ins, outs, rsem, lsem):
        (r_ref,) = outs
        x, y, c, _ = _mesh_pos()
        mine, theirs = _half_rows(c, R)
        return ([], [_remote(r_ref.at[mine], r_ref.at[mine], rsem(0), (x, y, 1 - c))],
                [functools.partial(_remote, r_ref.at[theirs], r_ref.at[theirs], rsem(0), (x, y, c))])

    return _Phase([], [red], [], 1, 0, build)


def _call(body, *, name, grid, in_specs, out_specs, out_shape, scratch_shapes=(), phases=()):
    single = not isinstance(out_specs, (list, tuple))
    out_specs = [out_specs] if single else list(out_specs)
    out_shape = [out_shape] if single else list(out_shape)
    n_in, n_out, n_scr = len(in_specs), len(out_specs), len(scratch_shapes)
    if not phases:
        call = pl.pallas_call(body, name=name, grid=grid, in_specs=in_specs, out_specs=out_specs,
                              out_shape=out_shape, scratch_shapes=list(scratch_shapes),
                              compiler_params=_params(len(grid)))
        return lambda *operands: (list(call(*operands)), [])

    ex_in, ex_out, aliases, spans = [], [], {}, []
    for ph in phases:
        i0, o0 = len(ex_in), len(ex_out)
        ex_in += ph.ins
        for a in ph.inout:
            aliases[n_in + len(ex_in)] = n_out + len(ex_out)
            ex_in.append(a)
            ex_out.append(jax.ShapeDtypeStruct(a.shape, a.dtype))
        ex_out += ph.outs
        spans.append((i0, len(ph.ins), o0, len(ex_out) - o0))
    n_remote = sum(ph.n_remote for ph in phases)
    n_local = max(sum(ph.n_local for ph in phases), 1)

    def wrapped(*refs):
        base_in, xin = refs[:n_in], refs[n_in:n_in + len(ex_in)]
        o0 = n_in + len(ex_in)
        base_out, xout = refs[o0:o0 + n_out], refs[o0 + n_out:o0 + n_out + len(ex_out)]
        scr = refs[o0 + n_out + len(ex_out):]
        send_sems, recv_sems, loc_sems = scr[n_scr:]
        first = functools.reduce(jnp.logical_and, [pl.program_id(i) == 0 for i in range(len(grid))])
        last = functools.reduce(jnp.logical_and, [pl.program_id(i) == grid[i] - 1 for i in range(len(grid))])

        def copies():
            out, r0, l0 = [], 0, 0
            for ph, (i0, ni, p0, no) in zip(phases, spans):
                rsem = lambda k, r0=r0: (send_sems.at[r0 + k], recv_sems.at[r0 + k])
                lsem = lambda k, l0=l0: loc_sems.at[l0 + k]
                out.append(ph.build(xin[i0:i0 + ni], xout[p0:p0 + no], rsem, lsem))
                r0, l0 = r0 + ph.n_remote, l0 + ph.n_local
            return out

        @pl.when(first)
        def _():
            for loc, outg, _ in copies():
                for cp in loc + outg:
                    cp.start()

        body(*base_in, *base_out, *scr[:n_scr])

        @pl.when(last)
        def _():
            for loc, outg, inc in copies():
                for make in inc:
                    make().wait_recv()
                for cp in outg:
                    cp.wait_send()
                for cp in loc:
                    cp.wait()

    hbm = pl.BlockSpec(memory_space=pl.ANY)
    call = pl.pallas_call(
        wrapped, name=name, grid=grid, in_specs=list(in_specs) + [hbm] * len(ex_in),
        out_specs=out_specs + [hbm] * len(ex_out), out_shape=out_shape + ex_out,
        scratch_shapes=list(scratch_shapes) + [pltpu.SemaphoreType.DMA((n_remote,)), pltpu.SemaphoreType.DMA((n_remote,)),
                                              pltpu.SemaphoreType.DMA((n_local,))],
        input_output_aliases=aliases, compiler_params=_params(len(grid)))

    def run(*operands):
        res = call(*operands, *ex_in)
        extra = res[n_out:]
        return list(res[:n_out]), [list(extra[p0:p0 + no]) for (_, _, p0, no) in spans]

    return run


def _inproj_fwd(x, g_mix, w_in, tm, phases=()):
    T = x.shape[0]
    widths = [IN_SEGS[i + 1] - IN_SEGS[i] for i in range(7)]

    def body(x_ref, g_ref, w_ref, h_ref, *z_refs):
        h, _ = _rms_fwd(x_ref[...], g_ref[...])
        hb = h.astype(BF16)
        h_ref[...] = hb
        for j in range(N_CHIPS):
            zj = _dot(hb, w_ref[j])
            for s, lo, hi, off in _seg_pieces(j * IN_BLK, (j + 1) * IN_BLK):
                z_refs[s][:, lo:hi] = zj[:, off:off + hi - lo]

    return _call(
        body, phases=phases, name="inproj_fwd", grid=(T // tm,),
        in_specs=[pl.BlockSpec((tm, D_MODEL), lambda i: (i, 0)), _const((1, D_MODEL)),
                  _const((N_CHIPS, D_MODEL, IN_BLK))],
        out_specs=[pl.BlockSpec((tm, D_MODEL), lambda i: (i, 0))]
        + [pl.BlockSpec((tm, w), lambda i: (i, 0)) for w in widths],
        out_shape=[jax.ShapeDtypeStruct((T, D_MODEL), BF16)]
        + [jax.ShapeDtypeStruct((T, w), F32) for w in widths],
    )(x, g_mix, w_in)


def _inproj_bwd(dz_parts, w_in, x, g_mix, dx1, tm, phases=()):
    T = x.shape[0]
    widths = [IN_SEGS[i + 1] - IN_SEGS[i] for i in range(7)]

    def body(*refs):
        p_refs = refs[:7]
        w_ref, x_ref, g_ref, dx1_ref, gx_ref, dg_ref, dz_ref = refs[7:]

        @pl.when(pl.program_id(0) == 0)
        def _():
            dg_ref[...] = jnp.zeros_like(dg_ref)

        for s in range(7):
            dz_ref[:, IN_SEGS[s]:IN_SEGS[s + 1]] = p_refs[s][...]
        dh = jnp.zeros((tm, D_MODEL), F32)
        for j in range(N_CHIPS):
            dh = dh + _dot_nt(dz_ref[:, j * IN_BLK:(j + 1) * IN_BLK], w_ref[j])
        xv = x_ref[...]
        g = g_ref[...]
        _, r = _rms_fwd(xv, g)
        dx, dg = _rms_bwd(dh, xv, r, g)
        gx_ref[...] = dx1_ref[...] + dx
        dg_ref[...] += dg

    row = lambda w: pl.BlockSpec((tm, w), lambda i: (i, 0))
    return _call(
        body, phases=phases, name="inproj_bwd", grid=(T // tm,),
        in_specs=[row(w) for w in widths]
        + [_const((N_CHIPS, D_MODEL, IN_BLK)), row(D_MODEL), _const((1, D_MODEL)), row(D_MODEL)],
        out_specs=[row(D_MODEL), _const((1, D_MODEL))],
        out_shape=[jax.ShapeDtypeStruct((T, D_MODEL), F32), jax.ShapeDtypeStruct((1, D_MODEL), F32)],
        scratch_shapes=[pltpu.VMEM((tm, IN_TOTAL), BF16)],
    )(*dz_parts, w_in, x, g_mix, dx1)


def _wgrad_in(h0, dz_parts, tm):
    T = h0.shape[0]
    widths = [IN_SEGS[i + 1] - IN_SEGS[i] for i in range(7)]

    def body(*refs):
        h_ref, p_refs, o_ref, acc_ref, sems = refs[0], refs[1:8], refs[8], refs[9], refs[10]
        t = pl.program_id(0)
        last = T // tm - 1

        @pl.when(t == 0)
        def _():
            acc_ref[...] = jnp.zeros_like(acc_ref)

        def accumulate(j):
            for s, lo, hi, off in _seg_pieces(j * IN_BLK, (j + 1) * IN_BLK):
                acc_ref[j, :, off:off + hi - lo] += _dot_tn(h_ref[...], p_refs[s][:, lo:hi])

        @pl.when(t < last)
        def _():
            for j in range(N_CHIPS):
                accumulate(j)

        @pl.when(t == last)
        def _():
            copies = [pltpu.make_async_copy(acc_ref.at[j], o_ref.at[j], sems.at[j]) for j in range(N_CHIPS)]
            for j in range(N_CHIPS):
                accumulate(j)
                copies[j].start()
            for cp in copies:
                cp.wait()

    row = lambda w: pl.BlockSpec((tm, w), lambda i: (i, 0))
    return pl.pallas_call(
        body, name="wgrad_in", grid=(T // tm,), in_specs=[row(D_MODEL)] + [row(w) for w in widths],
        out_specs=pl.BlockSpec(memory_space=pl.ANY),
        out_shape=jax.ShapeDtypeStruct((N_CHIPS, D_MODEL, IN_BLK), F32),
        scratch_shapes=[pltpu.VMEM((N_CHIPS, D_MODEL, IN_BLK), F32), pltpu.SemaphoreType.DMA((N_CHIPS,))],
        compiler_params=_params(1),
    )(h0, *dz_parts)


def _wgrad(a, g, name, blocked, cn, tm, phases=()):
    T, K = a.shape
    N = g.shape[1]
    nb = N // cn

    def body(a_ref, g_ref, o_ref):
        @pl.when(pl.program_id(1) == 0)
        def _():
            o_ref[...] = jnp.zeros_like(o_ref)

        o_ref[...] += _dot_tn(a_ref[...].astype(BF16), g_ref[...].astype(BF16))

    if blocked:
        out_spec = pl.BlockSpec((None, K, cn), lambda j, t: (j, 0, 0))
        out_shape = jax.ShapeDtypeStruct((nb, K, cn), F32)
    else:
        out_spec = pl.BlockSpec((K, cn), lambda j, t: (0, j))
        out_shape = jax.ShapeDtypeStruct((K, N), F32)
    outs, extra = _call(
        body, phases=phases, name=name, grid=(nb, T // tm),
        in_specs=[pl.BlockSpec((tm, K), lambda j, t: (t, 0)), pl.BlockSpec((tm, cn), lambda j, t: (t, j))],
        out_specs=out_spec, out_shape=out_shape,
    )(a, g)
    return outs[0], extra


def _shift_down(x, prev8, sft, row, row8, tm):
    xs = pltpu.roll(x, sft, 0)
    top = jnp.where(row8 < sft, pltpu.roll(prev8, sft, 0), xs[0:8])
    return jnp.concatenate([top, xs[8:]], axis=0)


def _shift_up(x, next8, sft, row8, tm):
    xs = pltpu.roll(x, tm - sft, 0)
    bot = jnp.where(row8 >= 8 - sft, pltpu.roll(next8, 8 - sft, 0), xs[tm - 8:tm])
    return jnp.concatenate([xs[0:tm - 8], bot], axis=0)


def _conv_fwd(x, prev8, cw_ref, cb, row, row8, tm):
    xc = cb + cw_ref[CONV_W - 1:CONV_W, :] * x
    for sft in range(1, CONV_W):
        j = CONV_W - 1 - sft
        xc = xc + cw_ref[j:j + 1, :] * _shift_down(x, prev8, sft, row, row8, tm)
    return xc


def _blockdiag_dot(xb, w_ref, transpose):
    outs = []
    for b in range(D_MODEL // LANES):
        xs = xb[:, b * LANES:(b + 1) * LANES]
        outs.append(_dot_nt(xs, w_ref[b]) if transpose else _dot(xs, w_ref[b]))
    return jnp.concatenate(outs, axis=1)


def _softplus_neg(lam):
    e = jnp.exp(-jnp.abs(lam))
    u = 1.0 + e
    log1p_e = jnp.where(u == 1.0, e, jnp.log(u) * (e / (u - 1.0)))
    sp = jnp.maximum(-lam, 0.0) + log1p_e
    return sp, -_sigmoid(-lam)


def _lru_gates(xc, wrg_ref, brg, wig_ref, big, sp):
    xcb = xc.astype(BF16)
    r = _sigmoid(_blockdiag_dot(xcb, wrg_ref, False) + brg)
    i = _sigmoid(_blockdiag_dot(xcb, wig_ref, False) + big)
    log_a = (-LRU_C) * r * sp
    a = jnp.exp(log_a)
    t = jnp.tanh(log_a)
    one_m_a2 = (-2.0) * t / (1.0 - t)
    mult = jnp.sqrt(one_m_a2)
    return xcb, r, i, a, mult


def _scan_down(a, b, row, tm):
    d = 1
    while d < tm:
        if d < 8:
            keep = row >= d
            a_s = jnp.where(keep, pltpu.roll(a, d, 0), 1.0)
            b_s = jnp.where(keep, pltpu.roll(b, d, 0), 0.0)
            b = a * b_s + b
            a = a * a_s
        else:
            b = jnp.concatenate([b[:d], a[d:] * b[:-d] + b[d:]], axis=0)
            a = jnp.concatenate([a[:d], a[d:] * a[:-d]], axis=0)
        d *= 2
    return a, b


def _scan_up(c, b, row, tm):
    d = 1
    while d < tm:
        if d < 8:
            keep = row < tm - d
            c_s = jnp.where(keep, pltpu.roll(c, tm - d, 0), 1.0)
            b_s = jnp.where(keep, pltpu.roll(b, tm - d, 0), 0.0)
            b = c * b_s + b
            c = c * c_s
        else:
            b = jnp.concatenate([c[:-d] * b[d:] + b[:-d], b[-d:]], axis=0)
            c = jnp.concatenate([c[:-d] * c[d:], c[-d:]], axis=0)
        d *= 2
    return c, b


def _rnn_fwd(xr, gr, conv_w, conv_b, wrg2, b_rg, wig2, b_ig, lam, n_seq, S, tm, phases=()):
    T = xr.shape[0]
    nt = S // tm
    W = D_MODEL

    def body(xr_ref, gr_ref, cw_ref, cb_ref, wrg_ref, brg_ref, wig_ref, big_ref, lam_ref,
             xc_ref, h_ref, r_ref, i_ref, a_ref, mult_ref, ya_ref, px_ref, ph_ref):
        @pl.when(pl.program_id(1) == 0)
        def _():
            px_ref[...] = jnp.zeros_like(px_ref)
            ph_ref[...] = jnp.zeros_like(ph_ref)

        row = lax.broadcasted_iota(jnp.int32, (tm, W), 0)
        row8 = lax.broadcasted_iota(jnp.int32, (8, W), 0)
        x = xr_ref[...]
        xc = _conv_fwd(x, px_ref[...], cw_ref, cb_ref[...], row, row8, tm)
        sp, _ = _softplus_neg(lam_ref[...])
        _, r, i, a, mult = _lru_gates(xc, wrg_ref, brg_ref[...], wig_ref, big_ref[...], sp)
        r_ref[...], i_ref[...], a_ref[...], mult_ref[...] = r, i, a, mult
        bterm = mult * (i * xc)
        acum, hloc = _scan_down(a, bterm, row, tm)
        h = hloc + acum * ph_ref[7:8, :]
        h_ref[...] = h
        xc_ref[...] = xc
        gelu, _ = _gelu_and_grad(gr_ref[...])
        ya_ref[...] = (h * gelu).astype(BF16)
        px_ref[...] = xr_ref[tm - 8:tm, :]
        ph_ref[...] = h_ref[tm - 8:tm, :]

    tile = pl.BlockSpec((tm, W), lambda s, t: (s * nt + t, 0))
    return _call(
        body, phases=phases, name="rnn_fwd", grid=(n_seq, nt),
        in_specs=[tile, tile, _const((CONV_W, W)), _const((1, W)), _const((8, LANES, LANES)), _const((1, W)),
                  _const((8, LANES, LANES)), _const((1, W)), _const((1, W))],
        out_specs=[tile] * 7,
        out_shape=[jax.ShapeDtypeStruct((T, W), F32)] * 6 + [jax.ShapeDtypeStruct((T, W), BF16)],
        scratch_shapes=[pltpu.VMEM((8, W), F32), pltpu.VMEM((8, W), F32)],
    )(xr, gr, conv_w, conv_b, wrg2, b_rg, wig2, b_ig, lam)


def _rnn_bwd(dya, xr, gr, xc, h, gates, conv_w, wrg2, wig2, lam, n_seq, S, tm, phases=()):
    T = xr.shape[0]
    nt = S // tm
    W = D_MODEL
    nb8 = tm // 8

    def body(dya_ref, xr_ref, gr_ref, xc_ref, h_ref, r_ref, i_ref, a_ref, mult_ref, xprev_ref, hprev_ref, cw_ref,
             wrg_ref, wig_ref, lam_ref, dxr_ref, dgr_ref, vec_ref, dwrg_ref, dwig_ref, cg_ref, ndxc_ref, tmp_ref):
        s, ti = pl.program_id(0), pl.program_id(1)

        @pl.when((s == 0) & (ti == 0))
        def _():
            vec_ref[...] = jnp.zeros_like(vec_ref)
            dwrg_ref[...] = jnp.zeros_like(dwrg_ref)
            dwig_ref[...] = jnp.zeros_like(dwig_ref)

        @pl.when(ti == 0)
        def _():
            cg_ref[...] = jnp.zeros_like(cg_ref)
            ndxc_ref[...] = jnp.zeros_like(ndxc_ref)

        first = ti == nt - 1
        row = lax.broadcasted_iota(jnp.int32, (tm, W), 0)
        row8 = lax.broadcasted_iota(jnp.int32, (8, W), 0)
        x = xr_ref[...]
        xc = xc_ref[...]
        hv = h_ref[...]
        xprev = jnp.where(first, 0.0, xprev_ref[...])
        hprev = jnp.where(first, 0.0, hprev_ref[...])
        sp, dsp_dlam = _softplus_neg(lam_ref[...])
        xcb = xc.astype(BF16)
        r, i, a, mult = r_ref[...], i_ref[...], a_ref[...], mult_ref[...]

        gelu, dgelu = _gelu_and_grad(gr_ref[...])
        dya_v = dya_ref[...]
        dgr_ref[...] = (dya_v * hv * dgelu).astype(BF16)
        dh = dya_v * gelu
        c = jnp.where(row < tm - 1, pltpu.roll(a, tm - 1, 0), 1.0)
        ccum, gloc = _scan_up(c, dh, row, tm)
        G = gloc + ccum * cg_ref[0:1, :]
        tmp_ref[...] = a * G
        cg_ref[...] = tmp_ref[0:8, :]

        h_m1 = _shift_down(hv, hprev, 1, row, row8, tm)
        ixc = i * xc
        dixc = G * mult
        dlog_a = (G * h_m1) * a - (G * ixc) * (a * a / mult)
        dr = dlog_a * ((-LRU_C) * sp)
        di = dixc * xc
        drg = dr * r * (1.0 - r)
        dig = di * i * (1.0 - i)
        vec_ref[7:8, :] += jnp.sum(dlog_a * ((-LRU_C) * r), axis=0, keepdims=True) * dsp_dlam
        vec_ref[5:6, :] += jnp.sum(drg, axis=0, keepdims=True)
        vec_ref[6:7, :] += jnp.sum(dig, axis=0, keepdims=True)
        drgb = drg.astype(BF16)
        digb = dig.astype(BF16)
        dxc = dixc * i + _blockdiag_dot(drgb, wrg_ref, True) + _blockdiag_dot(digb, wig_ref, True)
        for b in range(W // LANES):
            sl = slice(b * LANES, (b + 1) * LANES)
            dwrg_ref[b] += _dot_tn(xcb[:, sl], drgb[:, sl])
            dwig_ref[b] += _dot_tn(xcb[:, sl], digb[:, sl])

        vec_ref[4:5, :] += jnp.sum(dxc, axis=0, keepdims=True)
        vec_ref[3:4, :] += jnp.sum(dxc * x, axis=0, keepdims=True)
        dxr = cw_ref[CONV_W - 1:CONV_W, :] * dxc
        nxt = ndxc_ref[...]
        for sft in range(1, CONV_W):
            j = CONV_W - 1 - sft
            vec_ref[j:j + 1, :] += jnp.sum(dxc * _shift_down(x, xprev, sft, row, row8, tm), axis=0, keepdims=True)
            dxr = dxr + cw_ref[j:j + 1, :] * _shift_up(dxc, nxt, sft, row8, tm)
        dxr_ref[...] = dxr.astype(BF16)
        tmp_ref[...] = dxc
        ndxc_ref[...] = tmp_ref[0:8, :]

    rev = lambda s, t: (s * nt + nt - 1 - t, 0)
    tile = pl.BlockSpec((tm, W), rev)
    prev8 = pl.BlockSpec((8, W), lambda s, t: (jnp.maximum((s * nt + nt - 1 - t) * nb8 - 1, 0), 0))
    return _call(
        body, phases=phases, name="rnn_bwd", grid=(n_seq, nt),
        in_specs=[tile] * 9 + [prev8, prev8, _const((CONV_W, W)), _const((8, LANES, LANES)),
                               _const((8, LANES, LANES)), _const((1, W))],
        out_specs=[tile, tile, _const((16, W)), _const((8, LANES, LANES)), _const((8, LANES, LANES))],
        out_shape=[jax.ShapeDtypeStruct((T, W), BF16), jax.ShapeDtypeStruct((T, W), BF16),
                   jax.ShapeDtypeStruct((16, W), F32), jax.ShapeDtypeStruct((8, LANES, LANES), F32),
                   jax.ShapeDtypeStruct((8, LANES, LANES), F32)],
        scratch_shapes=[pltpu.VMEM((8, W), F32), pltpu.VMEM((8, W), F32), pltpu.VMEM((tm, W), F32)],
    )(dya, xr, gr, xc, h, *gates, xr, h, conv_w, wrg2, wig2, lam)


def _head_swap(t, lane):
    w = t.shape[1]
    return jnp.where(lane % HEAD_DIM < HEAD_DIM // 2, pltpu.roll(t, w - HEAD_DIM // 2, 1),
                     pltpu.roll(t, HEAD_DIM // 2, 1))


def _qk_prep(t, gain, cosf, sins, ind, indt, lane):
    ms = _split_dot(t * t, ind) * (1.0 / HEAD_DIM)
    rstd = _split_dot(lax.rsqrt(ms + NORM_EPS), indt)
    tn = (t * rstd) * gain
    return tn * cosf + _head_swap(tn, lane) * sins, rstd


def _qk_prep_bwd(dy, t, rstd, gain, cosf, sins, ind, indt, lane):
    dtn = dy * cosf + _head_swap(dy * sins, lane)
    dgain = jnp.sum(dtn * (t * rstd), axis=0, keepdims=True)
    dn = dtn * gain
    m = _split_dot(_split_dot(dn * t, ind), indt) * (1.0 / HEAD_DIM)
    return rstd * dn - t * (rstd * rstd * rstd * m), dgain


def _attn_mask_t(blk_idx):
    ci = lax.broadcasted_iota(jnp.int32, (2 * WINDOW, WINDOW), 0)
    qi = lax.broadcasted_iota(jnp.int32, (2 * WINDOW, WINDOW), 1)
    diff = WINDOW + qi - ci
    return (diff >= 0) & (diff < WINDOW) & ((ci >= WINDOW) | (blk_idx > 0))


def _stack_heads(t, kvh, lo):
    parts = []
    for i in (2 * kvh, 2 * kvh + 1):
        tp = t[:, i * LANES:(i + 1) * LANES]
        parts += [jnp.where(lo, tp, 0.0), jnp.where(lo, 0.0, tp)]
    return jnp.concatenate(parts, axis=0).astype(BF16)


def _unstack_heads(ts, lo):
    w = WINDOW
    return jnp.where(lo, ts[0:w], ts[w:2 * w]), jnp.where(lo, ts[2 * w:3 * w], ts[3 * w:4 * w])


def _dup_head(t, kvh, lo2):
    m = kvh // 2
    t2 = t[:, m * LANES:(m + 1) * LANES]
    t2r = pltpu.roll(t2, HEAD_DIM, 1)
    return (jnp.where(lo2, t2, t2r) if kvh % 2 == 0 else jnp.where(lo2, t2r, t2)).astype(BF16)


def _fold_head(ts, kvh, lo2):
    tot = ts + pltpu.roll(ts, HEAD_DIM, 1)
    own = lo2 if kvh % 2 == 0 else ~lo2
    return jnp.where(own, tot, 0.0)


KEY_CHUNKS = tuple(slice(i * 64, (i + 1) * 64) for i in range(2 * WINDOW // 64))


def _fold8(x, op):
    return op(x.reshape(x.shape[0] // 8, 8, x.shape[1]), axis=0)


def _softmax_stats(s_ref, b, cols, sink):
    m8 = None
    for c in KEY_CHUNKS:
        t = _fold8(s_ref[b, c, cols], jnp.max)
        m8 = t if m8 is None else jnp.maximum(m8, t)
    mx = jnp.maximum(jnp.max(m8, axis=0, keepdims=True), sink)
    d8 = None
    for c in KEY_CHUNKS:
        t = _fold8(jnp.exp(s_ref[b, c, cols] - mx), jnp.sum)
        d8 = t if d8 is None else d8 + t
    es = jnp.exp(sink - mx)
    inv = 1.0 / (jnp.sum(d8, axis=0, keepdims=True) + es)
    return mx, inv, es * inv


def _attn_fwd(q, k, v, qg, kg, sinks, cosf, sins, ind_q, ind_qt, ind_k, ind_kt, n_seq, S, phases=()):
    T = q.shape[0]
    nblk = S // WINDOW
    W = D_MODEL

    def body(sink_ref, q_ref, k_ref, v_ref, qg_ref, kg_ref, cos_ref, sin_ref, iq_ref, iqt_ref, ik_ref, ikt_ref,
             o_ref, kc_ref, vc_ref, s_ref, p_ref, qs_ref, kd_ref, vd_ref):
        @pl.when(pl.program_id(1) == 0)
        def _():
            kc_ref[...] = jnp.zeros_like(kc_ref)
            vc_ref[...] = jnp.zeros_like(vc_ref)

        lane = lax.broadcasted_iota(jnp.int32, (WINDOW, W), 1)
        lo = lane[:, :LANES] < HEAD_DIM
        lo2 = lax.broadcasted_iota(jnp.int32, (2 * WINDOW, LANES), 1) < HEAD_DIM

        def one_block(h):
            n = pl.program_id(1) * bps + h
            rows = slice(h * WINDOW, (h + 1) * WINDOW)
            cosf, sinv = jnp.tile(cos_ref[rows, :], (1, W // LANES)), jnp.tile(sin_ref[rows, :], (1, W // LANES))
            qr, _ = _qk_prep(q_ref[rows, :], qg_ref[...], cosf, sinv, iq_ref[...], iqt_ref[...], lane)
            kr, _ = _qk_prep(k_ref[rows, :], kg_ref[...], cosf[:, :KV_W], sinv[:, :KV_W], ik_ref[...], ikt_ref[...],
                             lane[:, :KV_W])
            kc_ref[WINDOW:2 * WINDOW, :] = kr
            vc_ref[WINDOW:2 * WINDOW, :] = v_ref[rows, :]
            kc, vc = kc_ref[...], vc_ref[...]
            mask = jnp.tile(_attn_mask_t(n), (1, 4))
            qr = qr * HEAD_DIM ** -0.5
            for kvh in range(N_KV):
                qs_ref[h, kvh] = _stack_heads(qr, kvh, lo)
                kd_ref[h, kvh] = _dup_head(kc, kvh, lo2)
                vd_ref[h, kvh] = _dup_head(vc, kvh, lo2)

            def scores(kvh):
                s_ref[h, kvh % 2] = jnp.where(mask, _dot_nt(kd_ref[h, kvh], qs_ref[h, kvh]), -1e30)

            def softmax(kvh):
                sb, pb = s_ref.at[h], p_ref.at[h]
                b = kvh % 2
                for r in range(4):
                    cols = slice(r * WINDOW, (r + 1) * WINDOW)
                    mx, inv, _ = _softmax_stats(sb, b, cols, sink_ref[4 * kvh + r])
                    for c in KEY_CHUNKS:
                        pb[b, c, cols] = (jnp.exp(sb[b, c, cols] - mx) * inv).astype(BF16)

            def output(kvh):
                o0, o1 = _unstack_heads(_dot_tn(p_ref[h, kvh % 2], vd_ref[h, kvh]), lo)
                o_ref[rows, (2 * kvh) * LANES:(2 * kvh + 1) * LANES] = o0.astype(BF16)
                o_ref[rows, (2 * kvh + 1) * LANES:(2 * kvh + 2) * LANES] = o1.astype(BF16)

            scores(0)
            for kvh in range(N_KV):
                if kvh + 1 < N_KV:
                    scores(kvh + 1)
                softmax(kvh)
                output(kvh)
            kc_ref[0:WINDOW, :] = kr
            vc_ref[0:WINDOW, :] = v_ref[rows, :]

        for h in range(bps):
            one_block(h)

    bps = ATTN_BLOCKS_PER_STEP
    rows_step = bps * WINDOW
    blk = lambda w: pl.BlockSpec((rows_step, w), lambda s, n: (s * (nblk // bps) + n, 0))
    pos = pl.BlockSpec((rows_step, LANES), lambda s, n: (n, 0))
    outs, extra = _call(
        body, phases=phases, name="attn_fwd", grid=(n_seq, nblk // bps),
        in_specs=[pl.BlockSpec(memory_space=pltpu.SMEM), blk(W), blk(KV_W), blk(KV_W), _const((1, W)),
                  _const((1, KV_W)), pos, pos, _const((W, LANES)), _const((LANES, W)), _const((KV_W, LANES)),
                  _const((LANES, KV_W))],
        out_specs=blk(W), out_shape=jax.ShapeDtypeStruct((T, W), BF16),
        scratch_shapes=[pltpu.VMEM((2 * WINDOW, KV_W), F32), pltpu.VMEM((2 * WINDOW, KV_W), F32),
                        pltpu.VMEM((bps, 2, 2 * WINDOW, 4 * WINDOW), F32),
                        pltpu.VMEM((bps, 2, 2 * WINDOW, 4 * WINDOW), BF16),
                        pltpu.VMEM((bps, N_KV, 4 * WINDOW, LANES), BF16),
                        pltpu.VMEM((bps, N_KV, 2 * WINDOW, LANES), BF16),
                        pltpu.VMEM((bps, N_KV, 2 * WINDOW, LANES), BF16)],
    )(sinks, q, k, v, qg, kg, cosf, sins, ind_q, ind_qt, ind_k, ind_kt)
    return outs[0], extra


def _attn_bwd(do, q, k, v, qg, kg, sinks, cosf, sins, ind_q, ind_qt, ind_k, ind_kt, n_seq, S, phases=()):
    T = q.shape[0]
    nblk = S // WINDOW
    W = D_MODEL

    def body(sink_ref, do_ref, q_ref, k_ref, v_ref, qg_ref, kg_ref, cos_ref, sin_ref, iq_ref, iqt_ref, ik_ref,
             ikt_ref, dq_ref, dkc_ref, dkp_ref, dvc_ref, dvp_ref, dqg_ref, dsk_ref, kc_ref, vc_ref, dqr_ref,
             dk_ref, dv_ref, s_ref, dp_ref, p_ref, ds_ref, qs_ref, dos_ref, kd_ref, vd_ref):
        s_id, n_step = pl.program_id(0), pl.program_id(1)

        @pl.when((s_id == 0) & (n_step == 0))
        def _():
            dqg_ref[...] = jnp.zeros_like(dqg_ref)
            dsk_ref[...] = jnp.zeros_like(dsk_ref)

        @pl.when(n_step == 0)
        def _():
            kc_ref[...] = jnp.zeros_like(kc_ref)
            vc_ref[...] = jnp.zeros_like(vc_ref)

        lane = lax.broadcasted_iota(jnp.int32, (WINDOW, W), 1)
        lane_k = lane[:, :KV_W]
        lo = lane[:, :LANES] < HEAD_DIM
        lo2 = lax.broadcasted_iota(jnp.int32, (2 * WINDOW, LANES), 1) < HEAD_DIM
        scale = HEAD_DIM ** -0.5

        def one_block(h):
            n = n_step * bps + h
            rows = slice(h * WINDOW, (h + 1) * WINDOW)
            cosf, sinv = jnp.tile(cos_ref[rows, :], (1, W // LANES)), jnp.tile(sin_ref[rows, :], (1, W // LANES))
            qv = q_ref[rows, :]
            qr, q_rstd = _qk_prep(qv, qg_ref[...], cosf, sinv, iq_ref[...], iqt_ref[...], lane)
            kr, _ = _qk_prep(k_ref[rows, :], kg_ref[...], cosf[:, :KV_W], sinv[:, :KV_W], ik_ref[...], ikt_ref[...],
                             lane_k)
            kc_ref[WINDOW:2 * WINDOW, :] = kr
            vc_ref[WINDOW:2 * WINDOW, :] = v_ref[rows, :]
            kc, vc = kc_ref[...], vc_ref[...]
            dov = do_ref[rows, :]
            mask = jnp.tile(_attn_mask_t(n), (1, 4))
            qr = qr * scale
            dk_ref[h] = jnp.zeros((2 * WINDOW, KV_W), F32)
            dv_ref[h] = jnp.zeros((2 * WINDOW, KV_W), F32)
            for kvh in range(N_KV):
                qs_ref[h, kvh] = _stack_heads(qr, kvh, lo)
                dos_ref[h, kvh] = _stack_heads(dov, kvh, lo)
                kd_ref[h, kvh] = _dup_head(kc, kvh, lo2)
                vd_ref[h, kvh] = _dup_head(vc, kvh, lo2)
            sb, dpb, pb, dsb = s_ref.at[h], dp_ref.at[h], p_ref.at[h], ds_ref.at[h]

            def scores(kvh):
                b = kvh % 2
                sb[b] = jnp.where(mask, _dot_nt(kd_ref[h, kvh], qs_ref[h, kvh]), -1e30)
                dpb[b] = _dot_nt(vd_ref[h, kvh], dos_ref[h, kvh])

            def softmax(kvh):
                b = kvh % 2
                for r in range(4):
                    cols = slice(r * WINDOW, (r + 1) * WINDOW)
                    head = 4 * kvh + r
                    mx, inv, ps = _softmax_stats(sb, b, cols, sink_ref[head])
                    g8 = None
                    for c in KEY_CHUNKS:
                        t = _fold8(jnp.exp(sb[b, c, cols] - mx) * dpb[b, c, cols], jnp.sum)
                        g8 = t if g8 is None else g8 + t
                    dd = jnp.sum(g8, axis=0, keepdims=True) * inv
                    for c in KEY_CHUNKS:
                        p = jnp.exp(sb[b, c, cols] - mx) * inv
                        pb[b, c, cols] = p.astype(BF16)
                        dsb[b, c, cols] = (p * (dpb[b, c, cols] - dd)).astype(BF16)
                    dsk_ref[head:head + 1, :] -= ps * dd

            def grads(kvh):
                m, b = kvh // 2, kvh % 2
                dq0, dq1 = _unstack_heads(_dot_tn(dsb[b], kd_ref[h, kvh]) * scale, lo)
                dqr_ref[h, :, (2 * kvh) * LANES:(2 * kvh + 1) * LANES] = dq0
                dqr_ref[h, :, (2 * kvh + 1) * LANES:(2 * kvh + 2) * LANES] = dq1
                dk_ref[h, :, m * LANES:(m + 1) * LANES] += _fold_head(_dot(dsb[b], qs_ref[h, kvh]), kvh, lo2)
                dv_ref[h, :, m * LANES:(m + 1) * LANES] += _fold_head(_dot(pb[b], dos_ref[h, kvh]), kvh, lo2)

            scores(0)
            for kvh in range(N_KV):
                if kvh + 1 < N_KV:
                    scores(kvh + 1)
                softmax(kvh)
                grads(kvh)
            dq, dqg = _qk_prep_bwd(dqr_ref[h], qv, q_rstd, qg_ref[...], cosf, sinv, iq_ref[...], iqt_ref[...], lane)
            dq_ref[rows, :] = dq.astype(BF16)
            dqg_ref[...] += dqg
            dkp_ref[rows, :] = dk_ref[h, 0:WINDOW, :]
            dkc_ref[rows, :] = dk_ref[h, WINDOW:2 * WINDOW, :]
            dvp_ref[rows, :] = dv_ref[h, 0:WINDOW, :]
            dvc_ref[rows, :] = dv_ref[h, WINDOW:2 * WINDOW, :]
            kc_ref[0:WINDOW, :] = kr
            vc_ref[0:WINDOW, :] = v_ref[rows, :]

        for h in range(bps):
            one_block(h)

    bps = ATTN_BLOCKS_PER_STEP
    rows_step = bps * WINDOW
    blk = lambda w: pl.BlockSpec((rows_step, w), lambda s, n: (s * (nblk // bps) + n, 0))
    pos = pl.BlockSpec((rows_step, LANES), lambda s, n: (n, 0))
    kv_out = jax.ShapeDtypeStruct((T, KV_W), F32)
    stage = lambda dt: pltpu.VMEM((bps, 2, 2 * WINDOW, 4 * WINDOW), dt)
    return _call(
        body, phases=phases, name="attn_bwd", grid=(n_seq, nblk // bps),
        in_specs=[pl.BlockSpec(memory_space=pltpu.SMEM), blk(W), blk(W), blk(KV_W), blk(KV_W), _const((1, W)),
                  _const((1, KV_W)), pos, pos, _const((W, LANES)), _const((LANES, W)), _const((KV_W, LANES)),
                  _const((LANES, KV_W))],
        out_specs=[blk(W), blk(KV_W), blk(KV_W), blk(KV_W), blk(KV_W), _const((1, W)), _const((N_HEADS, LANES))],
        out_shape=[jax.ShapeDtypeStruct((T, W), BF16), kv_out, kv_out, kv_out, kv_out,
                   jax.ShapeDtypeStruct((1, W), F32), jax.ShapeDtypeStruct((N_HEADS, LANES), F32)],
        scratch_shapes=[pltpu.VMEM((2 * WINDOW, KV_W), F32), pltpu.VMEM((2 * WINDOW, KV_W), F32),
                        pltpu.VMEM((bps, WINDOW, W), F32), pltpu.VMEM((bps, 2 * WINDOW, KV_W), F32),
                        pltpu.VMEM((bps, 2 * WINDOW, KV_W), F32), stage(F32), stage(F32), stage(BF16), stage(BF16),
                        pltpu.VMEM((bps, N_KV, 4 * WINDOW, LANES), BF16),
                        pltpu.VMEM((bps, N_KV, 4 * WINDOW, LANES), BF16),
                        pltpu.VMEM((bps, N_KV, 2 * WINDOW, LANES), BF16),
                        pltpu.VMEM((bps, N_KV, 2 * WINDOW, LANES), BF16)],
    )(sinks, do, q, k, v, qg, kg, cosf, sins, ind_q, ind_qt, ind_k, ind_kt)


def _kv_bwd(dkc, dkp, dvc, dvp, k, kg, cosf, sins, ind_k, ind_kt, n_seq, S, phases=()):
    T = k.shape[0]
    nblk = S // WINDOW
    nb = min(KV_BLOCKS_PER_STEP, nblk)
    rows, nt = nb * WINDOW, nblk // nb

    def body(dkc_ref, dkp_ref, dkn_ref, dvc_ref, dvp_ref, dvn_ref, k_ref, kg_ref, cos_ref, sin_ref, ik_ref, ikt_ref,
             dk_ref, dv_ref, dkg_ref):
        s_id, n = pl.program_id(0), pl.program_id(1)

        @pl.when((s_id == 0) & (n == 0))
        def _():
            dkg_ref[...] = jnp.zeros_like(dkg_ref)

        blk = n * nb + lax.broadcasted_iota(jnp.int32, (rows, KV_W), 0) // WINDOW
        has_next = blk < nblk - 1

        def from_next(part_ref, next_ref):
            moved = jnp.concatenate([part_ref[WINDOW:rows, :], next_ref[...]], axis=0) if nb > 1 else next_ref[...]
            return jnp.where(has_next, moved, 0.0)

        lane = lax.broadcasted_iota(jnp.int32, (rows, KV_W), 1)
        dkr = dkc_ref[...] + from_next(dkp_ref, dkn_ref)
        dv_ref[...] = (dvc_ref[...] + from_next(dvp_ref, dvn_ref)).astype(BF16)
        cosf, sinv = jnp.tile(cos_ref[...], (1, KV_W // LANES)), jnp.tile(sin_ref[...], (1, KV_W // LANES))
        kv = k_ref[...]
        _, rstd = _qk_prep(kv, kg_ref[...], cosf, sinv, ik_ref[...], ikt_ref[...], lane)
        dk, dkg = _qk_prep_bwd(dkr, kv, rstd, kg_ref[...], cosf, sinv, ik_ref[...], ikt_ref[...], lane)
        dk_ref[...] = dk.astype(BF16)
        dkg_ref[...] += dkg

    cur = pl.BlockSpec((rows, KV_W), lambda s, n: (s * nt + n, 0))
    nxt = pl.BlockSpec((WINDOW, KV_W), lambda s, n: (s * nblk + jnp.minimum((n + 1) * nb, nblk - 1), 0))
    pos = pl.BlockSpec((rows, LANES), lambda s, n: (n, 0))
    return _call(
        body, phases=phases, name="kv_bwd", grid=(n_seq, nt),
        in_specs=[cur, cur, nxt, cur, cur, nxt, cur, _const((1, KV_W)), pos, pos, _const((KV_W, LANES)),
                  _const((LANES, KV_W))],
        out_specs=[cur, cur, _const((1, KV_W))],
        out_shape=[jax.ShapeDtypeStruct((T, KV_W), BF16), jax.ShapeDtypeStruct((T, KV_W), BF16),
                   jax.ShapeDtypeStruct((1, KV_W), F32)],
    )(dkc, dkp, dkp, dvc, dvp, dvp, k, kg, cosf, sins, ind_k, ind_kt)


def _merge_fwd(x, ya, o, ga, gb, w_rnn, w_attn, w_out, tm, phases=()):
    T = x.shape[0]
    W = D_MODEL

    def body(x_ref, ya_ref, o_ref, ga_ref, gb_ref, wr_ref, wa_ref, wo_ref, x1_ref, mg_ref):
        y_a = _dot(ya_ref[...], wr_ref[...])
        y_b = _dot(o_ref[...], wa_ref[...])
        mg = (_sigmoid(ga_ref[...]) * y_a + _sigmoid(gb_ref[...]) * y_b).astype(BF16)
        mg_ref[...] = mg
        x1_ref[...] = x_ref[...] + _dot(mg, wo_ref[...])

    row = pl.BlockSpec((tm, W), lambda i: (i, 0))
    sq = _const((W, W))
    return _call(
        body, phases=phases, name="merge_fwd", grid=(T // tm,),
        in_specs=[row, row, row, row, row, sq, sq, sq], out_specs=[row, row],
        out_shape=[jax.ShapeDtypeStruct((T, W), F32), jax.ShapeDtypeStruct((T, W), BF16)],
    )(x, ya, o, ga, gb, w_rnn, w_attn, w_out)


def _merge_bwd(dx1, ga, gb, ya, o, w_rnn, w_attn, w_out, tm, phases=()):
    T = dx1.shape[0]
    W = D_MODEL

    def body(dx1_ref, ga_ref, gb_ref, ya_ref, o_ref, wr_ref, wa_ref, wo_ref,
             dga_ref, dgb_ref, dya_ref, dyb_ref, dyain_ref, do_ref):
        dm = _dot_nt(dx1_ref[...].astype(BF16), wo_ref[...])
        sa = _sigmoid(ga_ref[...])
        sb = _sigmoid(gb_ref[...])
        dga_ref[...] = (dm * _dot(ya_ref[...], wr_ref[...]) * (sa * (1.0 - sa))).astype(BF16)
        dgb_ref[...] = (dm * _dot(o_ref[...], wa_ref[...]) * (sb * (1.0 - sb))).astype(BF16)
        dya = (dm * sa).astype(BF16)
        dyb = (dm * sb).astype(BF16)
        dya_ref[...] = dya
        dyb_ref[...] = dyb
        dyain_ref[...] = _dot_nt(dya, wr_ref[...])
        do_ref[...] = _dot_nt(dyb, wa_ref[...])

    row = pl.BlockSpec((tm, W), lambda i: (i, 0))
    sq = _const((W, W))
    b16 = jax.ShapeDtypeStruct((T, W), BF16)
    f32 = jax.ShapeDtypeStruct((T, W), F32)
    return _call(
        body, phases=phases, name="merge_bwd", grid=(T // tm,),
        in_specs=[row, row, row, row, row, sq, sq, sq], out_specs=[row] * 6,
        out_shape=[b16, b16, b16, b16, f32, f32],
    )(dx1, ga, gb, ya, o, w_rnn, w_attn, w_out)


def _mlp_fwd(x1, g_mlp, w_up, w_down, tm, phases=()):
    T = x1.shape[0]
    W = D_MODEL

    def body(x_ref, g_ref, wu_ref, wd_ref, x2_ref, hm_ref, u_ref, act_ref):
        xv = x_ref[...]
        hm, _ = _rms_fwd(xv, g_ref[...])
        hmb = hm.astype(BF16)
        hm_ref[...] = hmb
        for j in range(N_CHIPS):
            u = _dot(hmb, wu_ref[j])
            u_ref[:, j * W:(j + 1) * W] = u
            ru = jnp.maximum(u, 0.0)
            act_ref[:, j * W:(j + 1) * W] = (ru * ru).astype(BF16)
        x2_ref[...] = xv + _dot(act_ref[...], wd_ref[...])

    row = lambda w: pl.BlockSpec((tm, w), lambda i: (i, 0))
    return _call(
        body, phases=phases, name="mlp_fwd", grid=(T // tm,),
        in_specs=[row(W), _const((1, W)), _const((N_CHIPS, W, W)), _const((D_FF, W))],
        out_specs=[row(W), row(W), row(D_FF), row(D_FF)],
        out_shape=[jax.ShapeDtypeStruct((T, W), F32), jax.ShapeDtypeStruct((T, W), BF16),
                   jax.ShapeDtypeStruct((T, D_FF), F32), jax.ShapeDtypeStruct((T, D_FF), BF16)],
    )(x1, g_mlp, w_up, w_down)


def _mlp_bwd(dx2, u, x1, g_mlp, w_up, w_down, tm, phases=()):
    T = x1.shape[0]
    W = D_MODEL

    def body(dx2_ref, u_ref, x_ref, g_ref, wu_ref, wd_ref, dx1_ref, du_ref, dg_ref):
        @pl.when(pl.program_id(0) == 0)
        def _():
            dg_ref[...] = jnp.zeros_like(dg_ref)

        dx2 = dx2_ref[...]
        dact = _dot_nt(dx2.astype(BF16), wd_ref[...])
        du_ref[...] = (dact * (2.0 * jnp.maximum(u_ref[...], 0.0))).astype(BF16)
        dhm = jnp.zeros((tm, W), F32)
        for j in range(N_CHIPS):
            dhm = dhm + _dot_nt(du_ref[:, j * W:(j + 1) * W], wu_ref[j])
        xv = x_ref[...]
        g = g_ref[...]
        _, r = _rms_fwd(xv, g)
        dx, dg = _rms_bwd(dhm, xv, r, g)
        dx1_ref[...] = dx2 + dx
        dg_ref[...] += dg

    row = lambda w: pl.BlockSpec((tm, w), lambda i: (i, 0))
    return _call(
        body, phases=phases, name="mlp_bwd", grid=(T // tm,),
        in_specs=[row(W), row(D_FF), row(W), _const((1, W)), _const((N_CHIPS, W, W)), _const((D_FF, W))],
        out_specs=[row(W), row(D_FF), _const((1, W))],
        out_shape=[jax.ShapeDtypeStruct((T, W), F32), jax.ShapeDtypeStruct((T, D_FF), BF16),
                   jax.ShapeDtypeStruct((1, W), F32)],
    )(dx2, u, x1, g_mlp, w_up, w_down)


def _ple_loss(x2, p, target, g_ple, w_gate, w_proj, tm, phases=()):
    T = x2.shape[0]
    W = D_MODEL
    cw = W // N_CHIPS

    def body(x_ref, p_ref, t_ref, g_ref, wg_ref, wp_ref, loss_ref, dx2_ref, pb_ref, de_ref, hp_ref, dtg_ref, dg_ref):
        @pl.when(pl.program_id(0) == 0)
        def _():
            dg_ref[...] = jnp.zeros_like(dg_ref)
            loss_ref[...] = jnp.zeros_like(loss_ref)

        xv = x_ref[...]
        g = g_ref[...]
        pb = p_ref[...].astype(BF16)
        pb_ref[...] = pb
        e = jnp.concatenate([_dot(pb, wp_ref[j]) for j in range(N_CHIPS)], axis=1)
        hp, r = _rms_fwd(xv, g)
        hpb = hp.astype(BF16)
        hp_ref[...] = hpb
        sg = _sigmoid(_dot(hpb, wg_ref[...]))
        diff = (xv + e * sg) - t_ref[...]
        loss_ref[...] += jnp.sum(diff * diff) * (0.5 / W)
        dx3 = diff * (1.0 / W)
        de_ref[...] = (dx3 * sg).astype(BF16)
        dtg = (dx3 * e * (sg * (1.0 - sg))).astype(BF16)
        dtg_ref[...] = dtg
        dx, dg = _rms_bwd(_dot_nt(dtg, wg_ref[...]), xv, r, g)
        dx2_ref[...] = dx3 + dx
        dg_ref[...] += dg

    row = lambda w: pl.BlockSpec((tm, w), lambda i: (i, 0))
    b16 = lambda w: jax.ShapeDtypeStruct((T, w), BF16)
    return _call(
        body, phases=phases, name="ple_loss", grid=(T // tm,),
        in_specs=[row(W), row(PLE_DIM), row(W), _const((1, W)), _const((W, W)), _const((N_CHIPS, PLE_DIM, cw))],
        out_specs=[_const((8, LANES)), row(W), row(PLE_DIM), row(W), row(W), row(W), _const((1, W))],
        out_shape=[jax.ShapeDtypeStruct((8, LANES), F32), jax.ShapeDtypeStruct((T, W), F32), b16(PLE_DIM),
                   b16(W), b16(W), b16(W), jax.ShapeDtypeStruct((1, W), F32)],
    )(x2, p, target, g_ple, w_gate, w_proj)


def _adamw(w, g, m, v, name, tr, phases=()):
    R, C = w.shape
    c1 = 1.0 / (1.0 - ADAM_B1 ** ADAM_STEP)
    c2 = 1.0 / (1.0 - ADAM_B2 ** ADAM_STEP)

    def body(w_ref, g_ref, m_ref, v_ref, go_ref, d_ref, nm_ref, nv_ref):
        gv = g_ref[...]
        go_ref[...] = gv
        nm = ADAM_B1 * m_ref[...] + (1.0 - ADAM_B1) * gv
        nv = ADAM_B2 * v_ref[...] + (1.0 - ADAM_B2) * (gv * gv)
        nm_ref[...] = nm
        nv_ref[...] = nv
        d_ref[...] = (-ADAM_LR) * ((nm * c1) / (jnp.sqrt(nv * c2) + ADAM_EPS) + ADAM_WD * w_ref[...])

    row = pl.BlockSpec((tr, C), lambda i: (i, 0))
    sds = jax.ShapeDtypeStruct((R, C), F32)
    return _call(
        body, phases=phases, name=name, grid=(R // tr,), in_specs=[row] * 4, out_specs=[row] * 4,
        out_shape=[sds] * 4,
    )(w, g, m, v)


def _indicator(width):
    ind = np.zeros((width, LANES), np.float32)
    ind[np.arange(width), np.arange(width) // HEAD_DIM] = 1.0
    return jnp.asarray(ind, BF16), jnp.asarray(ind.T, BF16)


def _rope_tables(S):
    inv = ROPE_THETA ** (-jnp.arange(0, HEAD_DIM, 2, dtype=F32) / HEAD_DIM)
    ang = jnp.arange(S, dtype=F32)[:, None] * inv[None, :]
    cos, sin = jnp.cos(ang), jnp.sin(ang)
    cosf = jnp.tile(jnp.concatenate([cos, cos], axis=1), (1, LANES // HEAD_DIM))
    sins = jnp.tile(jnp.concatenate([-sin, sin], axis=1), (1, LANES // HEAD_DIM))
    return cosf, sins


def _pair_blockdiag(w):
    w4 = w.reshape(8, 2, HEAD_DIM, HEAD_DIM)
    eye = jnp.eye(2, dtype=w.dtype)
    return jnp.einsum("bpij,pq->bpiqj", w4, eye).reshape(8, LANES, LANES)


def _pair_blockdiag_extract(g):
    g5 = g.reshape(8, 2, HEAD_DIM, 2, HEAD_DIM)
    return jnp.stack([g5[:, 0, :, 0, :], g5[:, 1, :, 1, :]], axis=1).reshape(16, HEAD_DIM, HEAD_DIM)


def _pair_sum(parts, sibs, name):
    n = len(parts)
    dims = [(p.shape[1] // 2, p.shape[2]) for p in parts]

    def body(*refs):
        p_r, s_r, send_r, own_r, mine_r, sem = (refs[0:n], refs[n:2 * n], refs[2 * n:3 * n], refs[3 * n:4 * n],
                                                refs[4 * n:5 * n], refs[5 * n])
        x, y, c, chips = _mesh_pos()
        me = 2 * x + y
        loads = []
        for i, (R, _) in enumerate(dims):
            mine, _ = _half_rows(c, R)
            cp = pltpu.make_async_copy(p_r[i].at[:, mine, :], mine_r[i], sem.at[i])
            cp.start()
            loads.append(cp)
        for i in range(n):
            loads[i].wait()
            for j, (cx, cy) in enumerate(chips):
                k = 2 * cx + cy
                send_r[i][j] = (mine_r[i][k] + s_r[i][k]).astype(BF16)
            own_r[i][...] = mine_r[i][me] + s_r[i][me]

    vm = pl.BlockSpec(memory_space=pltpu.VMEM)
    out = pl.pallas_call(
        body, name=name, in_specs=[pl.BlockSpec(memory_space=pl.ANY)] * n + [vm] * n, out_specs=[vm] * (2 * n),
        out_shape=[jax.ShapeDtypeStruct((3, R, C), BF16) for R, C in dims]
        + [jax.ShapeDtypeStruct((R, C), F32) for R, C in dims],
        scratch_shapes=[pltpu.VMEM((N_CHIPS, R, C), F32) for R, C in dims] + [pltpu.SemaphoreType.DMA((n,))],
        compiler_params=pltpu.CompilerParams(vmem_limit_bytes=VMEM_LIMIT),
    )(*parts, *sibs)
    return out[:n], out[n:]


def _chip_sum(owns, recvs, name):
    n = len(owns)
    dims = [o.shape for o in owns]

    def body(*refs):
        own_r, recv_r, red_r, stage_r, sem = refs[0:n], refs[n:2 * n], refs[2 * n:3 * n], refs[3 * n:4 * n], refs[4 * n]
        x, y, c, _ = _mesh_pos()
        me = 2 * x + y
        stores = []
        for i, (R, _) in enumerate(dims):
            for k_me in range(N_CHIPS):

                @pl.when(me == k_me)
                def _():
                    acc = None
                    for k in range(N_CHIPS):
                        slot = ((k // 2) ^ (k_me // 2)) + 2 * ((k % 2) ^ (k_me % 2)) - 1
                        term = own_r[i][...] if k == k_me else recv_r[i][slot].astype(F32)
                        acc = term if acc is None else acc + term
                    stage_r[i][...] = acc

            mine, _ = _half_rows(c, R)
            cp = pltpu.make_async_copy(stage_r[i], red_r[i].at[mine, :], sem.at[i])
            cp.start()
            stores.append(cp)
        for cp in stores:
            cp.wait()

    vm = pl.BlockSpec(memory_space=pltpu.VMEM)
    return pl.pallas_call(
        body, name=name, in_specs=[vm] * (2 * n), out_specs=[pl.BlockSpec(memory_space=pl.ANY)] * n,
        out_shape=[jax.ShapeDtypeStruct((2 * R, C), F32) for R, C in dims],
        scratch_shapes=[pltpu.VMEM((R, C), F32) for R, C in dims] + [pltpu.SemaphoreType.DMA((n,))],
        compiler_params=pltpu.CompilerParams(vmem_limit_bytes=VMEM_LIMIT),
    )(*owns, *recvs)


def _gather_bf16(shard, name):
    R2, C = shard.shape
    R = R2 // 2
    H = R // 2

    def body(s_ref, o_ref, send_sems, recv_sems):
        x, y, c, _ = _mesh_pos()
        me, chip_x, chip_y, chip_d = 2 * x + y, 2 * (1 - x) + y, 2 * x + (1 - y), 2 * (1 - x) + (1 - y)
        to_x, to_y, me_dev, sibling = (1 - x, y, c), (x, 1 - y, c), (x, y, c), (x, y, 1 - c)

        def rows(core, off, n):
            return pl.ds(pl.multiple_of(core * R + off, H), n)

        def copy(k, chip, rws, to):
            blk = o_ref.at[chip, rws]
            return _remote(blk, blk, (send_sems.at[k], recv_sems.at[k]), to)

        piece, half_a, half_b = rows(c, 0, R), rows(c, 0, H), rows(c, H, H)
        o_ref[me] = s_ref[...].astype(BF16)
        sends = [copy(0, me, piece, to_x), copy(1, me, piece, to_y)]
        for cp in sends:
            cp.start()
        arrivals = [(0, chip_x, piece, (2, half_a, to_y)), (1, chip_y, piece, (3, half_b, to_x)),
                    (2, chip_d, half_a, None), (3, chip_d, half_b, None)]
        for k, chip, rws, onward in arrivals:
            copy(k, chip, rws, me_dev).wait_recv()
            if onward is not None:
                sends.append(copy(onward[0], chip, onward[1], onward[2]))
                sends[-1].start()
            sends.append(copy(4 + k, chip, rws, sibling))
            sends[-1].start()
        for k, chip, rws in [(4, chip_x, rows(1 - c, 0, R)), (5, chip_y, rows(1 - c, 0, R)),
                             (6, chip_d, rows(1 - c, 0, H)), (7, chip_d, rows(1 - c, H, H))]:
            copy(k, chip, rws, me_dev).wait_recv()
        for cp in sends:
            cp.wait_send()

    return pl.pallas_call(
        body, name=name, out_shape=jax.ShapeDtypeStruct((N_CHIPS, R2, C), BF16),
        in_specs=[pl.BlockSpec(memory_space=pltpu.VMEM)], out_specs=pl.BlockSpec(memory_space=pltpu.VMEM),
        scratch_shapes=[pltpu.SemaphoreType.DMA((8,)), pltpu.SemaphoreType.DMA((8,))],
        compiler_params=pltpu.CompilerParams(vmem_limit_bytes=VMEM_LIMIT),
    )(shard)


def _pair_exchange_sum(partial, name):
    _, R2, C = partial.shape
    R = R2 // 2

    def body(p_ref, send_ref, own_ref, mine_ref, sib_ref, loc_sems, send_sems, recv_sems):
        x, y, c, chips = _mesh_pos()
        me = 2 * x + y
        mine, theirs = _half_rows(c, R)
        order = [2 * cx + cy for cx, cy in chips] + [me]
        locs, pairs = [], []
        for i, k in enumerate(order):
            loc = pltpu.make_async_copy(p_ref.at[k, mine, :], mine_ref.at[i], loc_sems.at[i])
            pair = _remote(p_ref.at[k, theirs, :], sib_ref.at[i], (send_sems.at[i], recv_sems.at[i]), (x, y, 1 - c))
            loc.start()
            pair.start()
            locs.append(loc)
            pairs.append(pair)
        for i in range(N_CHIPS):
            locs[i].wait()
            pairs[i].wait_recv()
            total = mine_ref[i] + sib_ref[i]
            if i < 3:
                send_ref[i] = total.astype(BF16)
            else:
                own_ref[...] = total
        for pair in pairs:
            pair.wait_send()

    vm = pl.BlockSpec(memory_space=pltpu.VMEM)
    return pl.pallas_call(
        body, name=name, in_specs=[pl.BlockSpec(memory_space=pl.ANY)], out_specs=[vm, vm],
        out_shape=[jax.ShapeDtypeStruct((3, R, C), BF16), jax.ShapeDtypeStruct((R, C), F32)],
        scratch_shapes=[pltpu.VMEM((N_CHIPS, R, C), F32), pltpu.VMEM((N_CHIPS, R, C), F32),
                        pltpu.SemaphoreType.DMA((N_CHIPS,)), pltpu.SemaphoreType.DMA((N_CHIPS,)),
                        pltpu.SemaphoreType.DMA((N_CHIPS,))],
        compiler_params=pltpu.CompilerParams(vmem_limit_bytes=VMEM_LIMIT),
    )(partial)


def _allreduce_small(buf, name):
    rows, width = buf.shape
    h = rows // 2

    def body(b_ref, o_ref, sib_ref, pair_ref, in_ref, pair_sems, send_sems, recv_sems, fin_sems):
        x, y, c, chips = _mesh_pos()
        me = 2 * x + y
        mine, theirs = _half_rows(c, h)
        sibling = (x, y, 1 - c)
        pair = _remote(b_ref.at[theirs], sib_ref, (pair_sems.at[0], pair_sems.at[1]), sibling)
        pair.start()
        pair.wait()
        pair_ref[...] = b_ref[mine, :] + sib_ref[...]
        sends = []
        for j, (cx, cy) in enumerate(chips):
            cp = _remote(pair_ref, in_ref.at[j], (send_sems.at[j], recv_sems.at[j]), (cx, cy, c))
            cp.start()
            sends.append(cp)
        for cp in sends:
            cp.wait_recv()
        acc = None
        for k in range(N_CHIPS):
            term = jnp.where(me == k, pair_ref[...], in_ref[_peer_slot(k, x, y)])
            acc = term if acc is None else acc + term
        o_ref[mine, :] = acc
        fin = _remote(o_ref.at[mine], o_ref.at[mine], (fin_sems.at[0], fin_sems.at[1]), sibling)
        fin.start()
        fin.wait_send()
        _remote(o_ref.at[theirs], o_ref.at[theirs], (fin_sems.at[0], fin_sems.at[1]), sibling).wait_recv()
        for cp in sends:
            cp.wait_send()

    return pl.pallas_call(
        body, name=name, out_shape=jax.ShapeDtypeStruct((rows, width), F32),
        in_specs=[pl.BlockSpec(memory_space=pltpu.VMEM)], out_specs=pl.BlockSpec(memory_space=pltpu.VMEM),
        scratch_shapes=[pltpu.VMEM((h, width), F32), pltpu.VMEM((h, width), F32), pltpu.VMEM((3, h, width), F32),
                        pltpu.SemaphoreType.DMA((2,)), pltpu.SemaphoreType.DMA((3,)), pltpu.SemaphoreType.DMA((3,)),
                        pltpu.SemaphoreType.DMA((2,))],
        compiler_params=pltpu.CompilerParams(vmem_limit_bytes=VMEM_LIMIT),
    )(buf)


def _adamw_small(ws, gs, ms, vs):
    n = len(ws)
    c1 = 1.0 / (1.0 - ADAM_B1 ** ADAM_STEP)
    c2 = 1.0 / (1.0 - ADAM_B2 ** ADAM_STEP)

    def body(*refs):
        w_r, g_r, m_r, v_r = refs[0:n], refs[n:2 * n], refs[2 * n:3 * n], refs[3 * n:4 * n]
        d_r, nm_r, nv_r = refs[4 * n:5 * n], refs[5 * n:6 * n], refs[6 * n:7 * n]
        for i in range(n):
            gv = g_r[i][...]
            nm = ADAM_B1 * m_r[i][...] + (1.0 - ADAM_B1) * gv
            nv = ADAM_B2 * v_r[i][...] + (1.0 - ADAM_B2) * (gv * gv)
            nm_r[i][...] = nm
            nv_r[i][...] = nv
            d_r[i][...] = (-ADAM_LR) * ((nm * c1) / (jnp.sqrt(nv * c2) + ADAM_EPS) + ADAM_WD * w_r[i][...])

    vm = pl.BlockSpec(memory_space=pltpu.VMEM)
    sds = [jax.ShapeDtypeStruct(w.shape, F32) for w in ws]
    out = pl.pallas_call(body, name="adamw_small", in_specs=[vm] * (4 * n), out_specs=[vm] * (3 * n),
                         out_shape=sds * 3)(*ws, *gs, *ms, *vs)
    return out[0:n], out[n:2 * n], out[2 * n:3 * n]


_BIG = ("w_in", "w_rnn_proj", "w_attn_proj", "w_out", "w_up", "w_down", "w_ple_gate", "w_ple_proj")
_SMALL = ("g_mix", "conv_w", "conv_b", "w_rg", "b_rg", "w_ig", "b_ig", "lru_lambda", "q_gain", "k_gain", "sinks",
          "g_mlp", "g_ple")
_WEIGHTS = ("g_mix", "w_in", "conv_w", "conv_b", "w_rg", "b_rg", "w_ig", "b_ig", "lru_lambda", "w_rnn_proj",
            "q_gain", "k_gain", "sinks", "w_attn_proj", "w_out", "g_mlp", "w_up", "w_down", "g_ple", "w_ple_gate",
            "w_ple_proj")


def _pad_row(v):
    v = v.reshape(1, -1)
    return jnp.pad(v, ((0, 0), (0, D_MODEL - v.shape[1])))


def kernel(x, p, g_mix, w_in, conv_w, conv_b, w_rg, b_rg, w_ig, b_ig, lru_lambda, w_rnn_proj, q_gain, k_gain, sinks, w_attn_proj, w_out, g_mlp, w_up, w_down, g_ple, w_ple_gate, w_ple_proj, loss_target, m_g_mix, m_w_in, m_conv_w, m_conv_b, m_w_rg, m_b_rg, m_w_ig, m_b_ig, m_lru_lambda, m_w_rnn_proj, m_q_gain, m_k_gain, m_sinks, m_w_attn_proj, m_w_out, m_g_mlp, m_w_up, m_w_down, m_g_ple, m_w_ple_gate, m_w_ple_proj, v_g_mix, v_w_in, v_conv_w, v_conv_b, v_w_rg, v_b_rg, v_w_ig, v_b_ig, v_lru_lambda, v_w_rnn_proj, v_q_gain, v_k_gain, v_sinks, v_w_attn_proj, v_w_out, v_g_mlp, v_w_up, v_w_down, v_g_ple, v_w_ple_gate, v_w_ple_proj):
    w = dict(g_mix=g_mix, w_in=w_in, conv_w=conv_w, conv_b=conv_b, w_rg=w_rg, b_rg=b_rg, w_ig=w_ig, b_ig=b_ig,
             lru_lambda=lru_lambda, w_rnn_proj=w_rnn_proj, q_gain=q_gain, k_gain=k_gain, sinks=sinks,
             w_attn_proj=w_attn_proj, w_out=w_out, g_mlp=g_mlp, w_up=w_up, w_down=w_down, g_ple=g_ple,
             w_ple_gate=w_ple_gate, w_ple_proj=w_ple_proj)
    m = dict(g_mix=m_g_mix, w_in=m_w_in, conv_w=m_conv_w, conv_b=m_conv_b, w_rg=m_w_rg, b_rg=m_b_rg, w_ig=m_w_ig,
             b_ig=m_b_ig, lru_lambda=m_lru_lambda, w_rnn_proj=m_w_rnn_proj, q_gain=m_q_gain, k_gain=m_k_gain,
             sinks=m_sinks, w_attn_proj=m_w_attn_proj, w_out=m_w_out, g_mlp=m_g_mlp, w_up=m_w_up, w_down=m_w_down,
             g_ple=m_g_ple, w_ple_gate=m_w_ple_gate, w_ple_proj=m_w_ple_proj)
    v = dict(g_mix=v_g_mix, w_in=v_w_in, conv_w=v_conv_w, conv_b=v_conv_b, w_rg=v_w_rg, b_rg=v_b_rg, w_ig=v_w_ig,
             b_ig=v_b_ig, lru_lambda=v_lru_lambda, w_rnn_proj=v_w_rnn_proj, q_gain=v_q_gain, k_gain=v_k_gain,
             sinks=v_sinks, w_attn_proj=v_w_attn_proj, w_out=v_w_out, g_mlp=v_g_mlp, w_up=v_w_up, w_down=v_w_down,
             g_ple=v_g_ple, w_ple_gate=v_w_ple_gate, w_ple_proj=v_w_ple_proj)
    n_seq, S, _ = x.shape
    T = n_seq * S
    chip = 2 * lax.axis_index("x") + lax.axis_index("y")

    tm, tm_rnn = TM, TM_RNN
    xf, pf, tf = x.reshape(T, D_MODEL), p.reshape(T, PLE_DIM), loss_target.reshape(T, D_MODEL)
    first = lambda outs: [o[0] for o in outs]

    w_in_g = _gather_bf16(w["w_in"][0], "gather_w_in")
    wb = {name: w[name][0].astype(BF16) for name in _BIG if name != "w_in"}
    grp_mix, grp_mlp, grp_ple = ("w_rnn_proj", "w_attn_proj", "w_out"), ("w_up", "w_down"), ("w_ple_gate", "w_ple_proj")

    wb["conv_w"] = jnp.pad(conv_w[0], ((0, 16 - CONV_W), (0, 0)))

    cosf, sins = _rope_tables(S)
    ind_q, ind_qt = _indicator(D_MODEL)
    ind_k, ind_kt = _indicator(KV_W)
    wrg2 = _pair_blockdiag(w_rg[0]).astype(BF16)
    wig2 = _pair_blockdiag(w_ig[0]).astype(BF16)
    qg = jnp.tile(q_gain, (1, N_HEADS))
    kg = jnp.tile(k_gain, (1, N_KV))
    sk = sinks.reshape(N_HEADS)
    attn_c = (qg, kg, sk, cosf, sins, ind_q, ind_qt, ind_k, ind_kt, n_seq, S)

    early = grp_mix + ("w_up",)
    (h0, xr, gr, zq, zk, zv, ga, gb), ph = _inproj_fwd(
        xf, g_mix, w_in_g, tm, phases=[_ph_tree_send(wb[n]) for n in early] + [_ph_gather_send(wb["conv_w"])])
    g_early, g_conv = first(ph[:4]), ph[4][0]
    o, ph = _attn_fwd(zq, zk, zv, *attn_c,
                      phases=[_ph_tree_relay(g) for g in g_early] + [_ph_gather_pass(g_conv)]
                      + [_ph_tree_send(wb["w_down"])])
    g_early, g_conv, wd = first(ph[:4]), ph[4][0], ph[5][0]
    cw_full = g_conv[:, :CONV_W, :].transpose(1, 0, 2).reshape(CONV_W, D_MODEL)
    rnn_w = (cw_full, conv_b, wrg2, b_rg, wig2, b_ig, lru_lambda)
    (xc, h, *gates, ya), ph = _rnn_fwd(xr, gr, *rnn_w, n_seq, S, tm_rnn,
                               phases=[_ph_tree_finish(g) for g in g_early] + [_ph_tree_relay(wd)]
                               + [_ph_tree_send(wb[n]) for n in grp_ple])
    g_early, wd, gw_ple = first(ph[:4]), ph[4][0], first(ph[5:])
    wr, wa, wo = (g.reshape(D_MODEL, D_MODEL) for g in g_early[:3])
    wu = g_early[3]
    (x1, merged), ph = _merge_fwd(xf, ya, o, ga, gb, wr, wa, wo, tm,
                                  phases=[_ph_tree_finish(wd)] + [_ph_tree_relay(g) for g in gw_ple])
    wd, gw_ple = ph[0][0].reshape(D_FF, D_MODEL), first(ph[1:])
    (x2, hm, u, act), ph = _mlp_fwd(x1, g_mlp, wu, wd, tm // 2, phases=[_ph_tree_finish(g) for g in gw_ple])
    wpg, wpp = first(ph)
    wpg = wpg.reshape(D_MODEL, D_MODEL)
    (loss_t, dx2, pb, de, hp, dtg, dg_ple), _ = _ple_loss(x2, pf, tf, g_ple, wpg, wpp, tm)

    chipmajor = lambda g: g.reshape(N_CHIPS, g.shape[-2] // N_CHIPS, g.shape[-1]) if g.ndim == 2 else g
    tmw = min(2 * tm, T)
    dw_pp = _wgrad(pb, de, "wgrad_ple_proj", False, D_MODEL, tmw)[0]
    part_ple = [chipmajor(_wgrad(hp, dtg, "wgrad_ple_gate", False, D_MODEL, tmw)[0]),
                dw_pp.reshape(PLE_DIM, N_CHIPS, D_MODEL // N_CHIPS).transpose(1, 0, 2)]
    (dx1, du, dg_mlp), ph = _mlp_bwd(dx2, u, x1, g_mlp, wu, wd, tm // 2, phases=[_ph_pair_send(g) for g in part_ple])
    send_ple, own_ple = _pair_sum(part_ple, first(ph), "pair_sum_ple")
    dw_down, ph = _wgrad(act, dx2, "wgrad_down", False, D_MODEL // 2, tmw, phases=[_ph_chip_send(s) for s in send_ple])
    red_ple = _chip_sum(own_ple, first(ph), "chip_sum_ple")
    part_mlp = [_wgrad(hm, du, "wgrad_up", True, D_MODEL, tmw)[0], chipmajor(dw_down)]
    (dga, dgb, dya, dyb, dyain, do), _ = _merge_bwd(dx1, ga, gb, ya, o, wr, wa, wo, tm)
    dw_rnn, ph_up = _wgrad(ya, dya, "wgrad_rnn_proj", False, D_MODEL, tmw, phases=[_ph_pair_send(part_mlp[0])])
    dw_attn, ph_down = _wgrad(o, dyb, "wgrad_attn_proj", False, D_MODEL, tmw, phases=[_ph_pair_send(part_mlp[1])])
    dw_out, ph = _wgrad(merged, dx1, "wgrad_out", False, D_MODEL, tmw, phases=[_ph_half_swap(r) for r in red_ple])
    red_ple = first(ph)
    send_mlp, own_mlp = _pair_sum(part_mlp, [ph_up[0][0], ph_down[0][0]], "pair_sum_mlp")
    part_mix = [chipmajor(dw_rnn), chipmajor(dw_attn), chipmajor(dw_out)]
    (dxr, dgr, vec, dwrg2, dwig2), ph = _rnn_bwd(
        dyain, xr, gr, xc, h, gates, cw_full, wrg2, wig2, lru_lambda, n_seq, S, tm_rnn,
        phases=[_ph_chip_send(s) for s in send_mlp] + [_ph_pair_send(g) for g in part_mix])
    red_mlp = _chip_sum(own_mlp, first(ph[:2]), "chip_sum_mlp")
    send_mix, own_mix = _pair_sum(part_mix, first(ph[2:]), "pair_sum_mix")
    (dq, dkc, dkp, dvc, dvp, dqg, dsk), ph = _attn_bwd(
        do, zq, zk, zv, *attn_c, phases=[_ph_half_swap(r) for r in red_mlp] + [_ph_chip_send(s) for s in send_mix])
    red_mlp = first(ph[:2])
    red_mix = _chip_sum(own_mix, first(ph[2:]), "chip_sum_mix")
    (dk, dv, dkg), _ = _kv_bwd(dkc, dkp, dvc, dvp, zk, kg, cosf, sins, ind_k, ind_kt, n_seq, S)
    dz_parts = [dxr, dgr, dq, dk, dv, dga, dgb]
    send_in, own_in = _pair_exchange_sum(_wgrad_in(h0, dz_parts, tm), "pair_sum_in")
    (grad_x, dg_mix), ph = _inproj_bwd(dz_parts, w_in_g, xf, g_mix, dx1, tm,
                                       phases=[_ph_half_swap(r) for r in red_mix] + [_ph_chip_send(send_in)])
    red_mix = first(ph[:3])
    red_in = _chip_sum([own_in], first(ph[3:]), "chip_sum_in")
    reduced = dict(zip(grp_ple + grp_mlp + grp_mix, red_ple + red_mlp + red_mix))
    grads = {
        "g_mix": dg_mix[0], "g_mlp": dg_mlp[0], "g_ple": dg_ple[0],
        "conv_w": vec[0:CONV_W], "conv_b": vec[4], "b_rg": vec[5], "b_ig": vec[6], "lru_lambda": vec[7],
        "w_rg": _pair_blockdiag_extract(dwrg2), "w_ig": _pair_blockdiag_extract(dwig2),
        "q_gain": dqg.reshape(N_HEADS, HEAD_DIM).sum(0), "k_gain": dkg.reshape(N_KV, HEAD_DIM).sum(0),
        "sinks": dsk.sum(1),
    }

    rows = [grads["conv_w"], _pad_row(grads["conv_b"]), _pad_row(grads["b_rg"]), _pad_row(grads["b_ig"]),
            _pad_row(grads["lru_lambda"]), _pad_row(grads["g_mix"]), _pad_row(grads["g_mlp"]),
            _pad_row(grads["g_ple"]), _pad_row(grads["q_gain"]), _pad_row(grads["k_gain"]), _pad_row(grads["sinks"]),
            _pad_row(loss_t[0:1, 0:1]), jnp.zeros((1, D_MODEL), F32)]
    vecs = jnp.concatenate(rows, axis=0)
    packed = jnp.concatenate([vecs.reshape(-1, LANES), grads["w_rg"].reshape(-1, LANES),
                              grads["w_ig"].reshape(-1, LANES)], axis=0)
    red = _allreduce_small(packed, "allreduce_small")
    nv = vecs.size // LANES
    rvec = red[0:nv].reshape(16, D_MODEL)
    loss = rvec[14, 0]
    nw = grads["w_rg"].size // LANES
    sg = {
        "conv_w": lax.dynamic_slice(rvec[0:CONV_W], (0, chip * (D_MODEL // N_CHIPS)), (CONV_W, D_MODEL // N_CHIPS)),
        "conv_b": rvec[4], "b_rg": rvec[5], "b_ig": rvec[6], "lru_lambda": rvec[7], "g_mix": rvec[8],
        "g_mlp": rvec[9], "g_ple": rvec[10], "q_gain": rvec[11, :HEAD_DIM], "k_gain": rvec[12, :HEAD_DIM],
        "sinks": rvec[13, :N_HEADS], "w_rg": red[nv:nv + nw], "w_ig": red[nv + nw:nv + 2 * nw],
    }
    sg = {k: sg[k].reshape(w[k].shape) for k in _SMALL}
    d_s, m_s, v_s = _adamw_small([w[k] for k in _SMALL], [sg[k] for k in _SMALL], [m[k] for k in _SMALL],
                                 [v[k] for k in _SMALL])
    grad, delta, new_m, new_v = dict(sg), dict(zip(_SMALL, d_s)), dict(zip(_SMALL, m_s)), dict(zip(_SMALL, v_s))

    for name in ("w_ple_proj", "w_up", "w_down", "w_rnn_proj", "w_attn_proj", "w_out", "w_ple_gate", "w_in"):
        shape = w[name].shape
        outs, ph = _adamw(w[name][0], reduced[name], m[name][0], v[name][0], "adamw_" + name, min(ADAMW_ROWS, shape[1] // 2),
                          phases=[_ph_half_swap(r) for r in red_in] if name == "w_ple_proj" else ())
        if name == "w_ple_proj":
            reduced["w_in"] = ph[0][0]
        grad[name], delta[name], new_m[name], new_v[name] = (a.reshape(shape) for a in outs)

    return (loss, grad_x.reshape(x.shape), *[grad[k] for k in _WEIGHTS], *[delta[k] for k in _WEIGHTS],
            *[new_m[k] for k in _WEIGHTS], *[new_v[k] for k in _WEIGHTS])
```

```python
import functools
import math

import numpy as np
import jax
import jax.numpy as jnp
from jax import lax
from jax.experimental import pallas as pl
from jax.experimental.pallas import tpu as pltpu

F32 = jnp.float32
BF16 = jnp.bfloat16

D_MODEL = 1024
N_HEADS = 16
N_KV = 4
HEAD_DIM = 64
KV_W = N_KV * HEAD_DIM
D_FF = 4096
PLE_DIM = 256
WINDOW = 128
CONV_W = 4
LRU_C = 8.0
NORM_EPS = 1e-6
ROPE_THETA = 10000.0
N_CHIPS = 4
IN_TOTAL = 5632
IN_BLK = IN_TOTAL // N_CHIPS
IN_SEGS = (0, 1024, 2048, 3072, 3328, 3584, 4608, 5632)

ADAM_LR = 0.001
ADAM_B1 = 0.9
ADAM_B2 = 0.999
ADAM_EPS = 1e-08
ADAM_WD = 0.01
ADAM_STEP = 10

LANES = 128
V7X_VMEM_BYTES = 64 * 1024 * 1024
VMEM_LIMIT = V7X_VMEM_BYTES - 8 * 1024 * 1024
MESH_ID = pl.DeviceIdType.MESH
TM, TM_RNN, ADAMW_ROWS = 512, 256, 256
ATTN_BLOCKS_PER_STEP = 2
KV_BLOCKS_PER_STEP = 8


def _dot(a, b):
    return jnp.dot(a, b, preferred_element_type=F32)


def _dot_nt(a, b):
    return lax.dot_general(a, b, (((1,), (1,)), ((), ())), preferred_element_type=F32)


def _dot_tn(a, b):
    return lax.dot_general(a, b, (((0,), (0,)), ((), ())), preferred_element_type=F32)


def _split_dot(x, ind):
    hi = x.astype(BF16)
    lo = (x - hi.astype(F32)).astype(BF16)
    return _dot(hi, ind) + _dot(lo, ind)


def _sigmoid(x):
    return 1.0 / (1.0 + jnp.exp(-x))


_GELU_C = math.sqrt(2.0 / math.pi)


def _gelu_and_grad(g):
    inner = _GELU_C * (g + 0.044715 * g * g * g)
    t = jnp.tanh(inner)
    gelu = 0.5 * g * (1.0 + t)
    dgelu = 0.5 * (1.0 + t) + 0.5 * g * (1.0 - t * t) * _GELU_C * (1.0 + 3.0 * 0.044715 * g * g)
    return gelu, dgelu


def _const(shape):
    nd = len(shape)
    return pl.BlockSpec(shape, lambda *_: (0,) * nd)


def _params(n_grid, vmem=VMEM_LIMIT):
    return pltpu.CompilerParams(dimension_semantics=("arbitrary",) * n_grid, vmem_limit_bytes=vmem)


def _rms_fwd(x, g):
    r = lax.rsqrt(jnp.mean(x * x, axis=-1, keepdims=True) + NORM_EPS)
    return (x * r) * g, r


def _rms_bwd(dy, x, r, g):
    dn = dy * g
    dx = r * dn - x * (r * r * r * jnp.mean(dn * x, axis=-1, keepdims=True))
    dg = jnp.sum(dy * (x * r), axis=0, keepdims=True)
    return dx, dg


def _seg_pieces(blk_lo, blk_hi):
    out = []
    for s in range(7):
        lo, hi = max(blk_lo, IN_SEGS[s]), min(blk_hi, IN_SEGS[s + 1])
        if lo < hi:
            out.append((s, lo - IN_SEGS[s], hi - IN_SEGS[s], lo - blk_lo))
    return out


def _mesh_pos():
    x, y, c = lax.axis_index("x"), lax.axis_index("y"), lax.axis_index("c")
    other_chips = [(1 - x, y), (x, 1 - y), (1 - x, 1 - y)]
    return x, y, c, other_chips


def _peer_slot(k, x, y):
    dx = jnp.bitwise_xor(k // 2, x)
    dy = jnp.bitwise_xor(k % 2, y)
    return jnp.maximum(dx + 2 * dy - 1, 0)


def _half_rows(c, R):
    return pl.ds(pl.multiple_of(c * R, R), R), pl.ds(pl.multiple_of((1 - c) * R, R), R)


def _remote(src, dst, sems, to):
    return pltpu.make_async_remote_copy(src_ref=src, dst_ref=dst, send_sem=sems[0], recv_sem=sems[1],
                                        device_id=to, device_id_type=MESH_ID)


class _Phase:
    def __init__(self, ins, inout, outs, n_remote, n_local, build):
        self.ins, self.inout, self.outs = list(ins), list(inout), list(outs)
        self.n_remote, self.n_local, self.build = n_remote, n_local, build


def _ph_gather_send(wb):
    R2, C = wb.shape
    R = R2 // 2

    def build(ins, outs, rsem, lsem):
        (w_ref,), (g_ref,) = ins, outs
        x, y, c, chips = _mesh_pos()
        me = 2 * x + y
        mine, _ = _half_rows(c, R)
        loc = [pltpu.make_async_copy(w_ref, g_ref.at[me], lsem(0))]
        outg = [_remote(w_ref.at[mine], g_ref.at[me, mine], rsem(j), (cx, cy, c)) for j, (cx, cy) in enumerate(chips)]
        inc = [functools.partial(_remote, w_ref.at[mine], g_ref.at[2 * cx + cy, mine], rsem(j), (x, y, c))
               for j, (cx, cy) in enumerate(chips)]
        return loc, outg, inc

    return _Phase([wb], [], [jax.ShapeDtypeStruct((N_CHIPS, R2, C), wb.dtype)], 3, 1, build)


def _ph_gather_pass(gath):
    _, R2, C = gath.shape
    R = R2 // 2

    def build(ins, outs, rsem, lsem):
        (g_ref,) = outs
        x, y, c, chips = _mesh_pos()
        mine, theirs = _half_rows(c, R)
        outg, inc = [], []
        for j, (cx, cy) in enumerate(chips):
            blk = g_ref.at[2 * cx + cy, mine]
            outg.append(_remote(blk, blk, rsem(j), (x, y, 1 - c)))
            got = g_ref.at[2 * cx + cy, theirs]
            inc.append(functools.partial(_remote, got, got, rsem(j), (x, y, c)))
        return [], outg, inc

    return _Phase([], [gath], [], 3, 0, build)


def _ph_pair_send(partial):
    _, R2, C = partial.shape
    R = R2 // 2

    def build(ins, outs, rsem, lsem):
        (p_ref,), (s_ref,) = ins, outs
        x, y, c, _ = _mesh_pos()
        _, theirs = _half_rows(c, R)
        src = p_ref.at[:, theirs, :]
        return ([], [_remote(src, s_ref, rsem(0), (x, y, 1 - c))],
                [functools.partial(_remote, src, s_ref, rsem(0), (x, y, c))])

    return _Phase([partial], [], [jax.ShapeDtypeStruct((N_CHIPS, R, C), F32)], 1, 0, build)


def _ph_chip_send(sendb):
    def build(ins, outs, rsem, lsem):
        (s_ref,), (r_ref,) = ins, outs
        x, y, c, chips = _mesh_pos()
        outg = [_remote(s_ref.at[j], r_ref.at[j], rsem(j), (cx, cy, c)) for j, (cx, cy) in enumerate(chips)]
        inc = [functools.partial(_remote, s_ref.at[j], r_ref.at[j], rsem(j), (x, y, c)) for j in range(3)]
        return [], outg, inc

    return _Phase([sendb], [], [jax.ShapeDtypeStruct(sendb.shape, sendb.dtype)], 3, 0, build)


def _ph_half_swap(red):
    R2, C = red.shape
    R = R2 // 2

    def build(ins, outs, rsem, lsem):
        (r_ref,) = outs
        x, y, c, _ = _mesh_pos()
        mine, theirs = _half_rows(c, R)
        return ([], [_remote(r_ref.at[mine], r_ref.at[mine], rsem(0), (x, y, 1 - c))],
                [functools.partial(_remote, r_ref.at[theirs], r_ref.at[theirs], rsem(0), (x, y, c))])

    return _Phase([], [red], [], 1, 0, build)


def _call(body, *, name, grid, in_specs, out_specs, out_shape, scratch_shapes=(), phases=()):
    single = not isinstance(out_specs, (list, tuple))
    out_specs = [out_specs] if single else list(out_specs)
    out_shape = [out_shape] if single else list(out_shape)
    n_in, n_out, n_scr = len(in_specs), len(out_specs), len(scratch_shapes)
    if not phases:
        call = pl.pallas_call(body, name=name, grid=grid, in_specs=in_specs, out_specs=out_specs,
                              out_shape=out_shape, scratch_shapes=list(scratch_shapes),
                              compiler_params=_params(len(grid)))
        return lambda *operands: (list(call(*operands)), [])

    ex_in, ex_out, aliases, spans = [], [], {}, []
    for ph in phases:
        i0, o0 = len(ex_in), len(ex_out)
        ex_in += ph.ins
        for a in ph.inout:
            aliases[n_in + len(ex_in)] = n_out + len(ex_out)
            ex_in.append(a)
            ex_out.append(jax.ShapeDtypeStruct(a.shape, a.dtype))
        ex_out += ph.outs
        spans.append((i0, len(ph.ins), o0, len(ex_out) - o0))
    n_remote = sum(ph.n_remote for ph in phases)
    n_local = max(sum(ph.n_local for ph in phases), 1)

    def wrapped(*refs):
        base_in, xin = refs[:n_in], refs[n_in:n_in + len(ex_in)]
        o0 = n_in + len(ex_in)
        base_out, xout = refs[o0:o0 + n_out], refs[o0 + n_out:o0 + n_out + len(ex_out)]
        scr = refs[o0 + n_out + len(ex_out):]
        send_sems, recv_sems, loc_sems = scr[n_scr:]
        first = functools.reduce(jnp.logical_and, [pl.program_id(i) == 0 for i in range(len(grid))])
        last = functools.reduce(jnp.logical_and, [pl.program_id(i) == grid[i] - 1 for i in range(len(grid))])

        def copies():
            out, r0, l0 = [], 0, 0
            for ph, (i0, ni, p0, no) in zip(phases, spans):
                rsem = lambda k, r0=r0: (send_sems.at[r0 + k], recv_sems.at[r0 + k])
                lsem = lambda k, l0=l0: loc_sems.at[l0 + k]
                out.append(ph.build(xin[i0:i0 + ni], xout[p0:p0 + no], rsem, lsem))
                r0, l0 = r0 + ph.n_remote, l0 + ph.n_local
            return out

        @pl.when(first)
        def _():
            for loc, outg, _ in copies():
                for cp in loc + outg:
                    cp.start()

        body(*base_in, *base_out, *scr[:n_scr])

        @pl.when(last)
        def _():
            for loc, outg, inc in copies():
                for make in inc:
                    make().wait_recv()
                for cp in outg:
                    cp.wait_send()
                for cp in loc:
                    cp.wait()

    hbm = pl.BlockSpec(memory_space=pl.ANY)
    call = pl.pallas_call(
        wrapped, name=name, grid=grid, in_specs=list(in_specs) + [hbm] * len(ex_in),
        out_specs=out_specs + [hbm] * len(ex_out), out_shape=out_shape + ex_out,
        scratch_shapes=list(scratch_shapes) + [pltpu.SemaphoreType.DMA((n_remote,)), pltpu.SemaphoreType.DMA((n_remote,)),
                                              pltpu.SemaphoreType.DMA((n_local,))],
        input_output_aliases=aliases, compiler_params=_params(len(grid)))

    def run(*operands):
        res = call(*operands, *ex_in)
        extra = res[n_out:]
        return list(res[:n_out]), [list(extra[p0:p0 + no]) for (_, _, p0, no) in spans]

    return run


def _inproj_fwd(x, g_mix, w_in, tm, phases=()):
    T = x.shape[0]
    widths = [IN_SEGS[i + 1] - IN_SEGS[i] for i in range(7)]

    def body(x_ref, g_ref, w_ref, h_ref, *z_refs):
        h, _ = _rms_fwd(x_ref[...], g_ref[...])
        hb = h.astype(BF16)
        h_ref[...] = hb
        for j in range(N_CHIPS):
            zj = _dot(hb, w_ref[j])
            for s, lo, hi, off in _seg_pieces(j * IN_BLK, (j + 1) * IN_BLK):
                z_refs[s][:, lo:hi] = zj[:, off:off + hi - lo]

    return _call(
        body, phases=phases, name="inproj_fwd", grid=(T // tm,),
        in_specs=[pl.BlockSpec((tm, D_MODEL), lambda i: (i, 0)), _const((1, D_MODEL)),
                  _const((N_CHIPS, D_MODEL, IN_BLK))],
        out_specs=[pl.BlockSpec((tm, D_MODEL), lambda i: (i, 0))]
        + [pl.BlockSpec((tm, w), lambda i: (i, 0)) for w in widths],
        out_shape=[jax.ShapeDtypeStruct((T, D_MODEL), BF16)]
        + [jax.ShapeDtypeStruct((T, w), F32) for w in widths],
    )(x, g_mix, w_in)


def _inproj_bwd(dz_parts, w_in, x, g_mix, dx1, tm, phases=()):
    T = x.shape[0]
    widths = [IN_SEGS[i + 1] - IN_SEGS[i] for i in range(7)]

    def body(*refs):
        p_refs = refs[:7]
        w_ref, x_ref, g_ref, dx1_ref, gx_ref, dg_ref, dz_ref = refs[7:]

        @pl.when(pl.program_id(0) == 0)
        def _():
            dg_ref[...] = jnp.zeros_like(dg_ref)

        for s in range(7):
            dz_ref[:, IN_SEGS[s]:IN_SEGS[s + 1]] = p_refs[s][...]
        dh = jnp.zeros((tm, D_MODEL), F32)
        for j in range(N_CHIPS):
            dh = dh + _dot_nt(dz_ref[:, j * IN_BLK:(j + 1) * IN_BLK], w_ref[j])
        xv = x_ref[...]
        g = g_ref[...]
        _, r = _rms_fwd(xv, g)
        dx, dg = _rms_bwd(dh, xv, r, g)
        gx_ref[...] = dx1_ref[...] + dx
        dg_ref[...] += dg

    row = lambda w: pl.BlockSpec((tm, w), lambda i: (i, 0))
    return _call(
        body, phases=phases, name="inproj_bwd", grid=(T // tm,),
        in_specs=[row(w) for w in widths]
        + [_const((N_CHIPS, D_MODEL, IN_BLK)), row(D_MODEL), _const((1, D_MODEL)), row(D_MODEL)],
        out_specs=[row(D_MODEL), _const((1, D_MODEL))],
        out_shape=[jax.ShapeDtypeStruct((T, D_MODEL), F32), jax.ShapeDtypeStruct((1, D_MODEL), F32)],
        scratch_shapes=[pltpu.VMEM((tm, IN_TOTAL), BF16)],
    )(*dz_parts, w_in, x, g_mix, dx1)


def _wgrad_in(h0, dz_parts, tm):
    T = h0.shape[0]
    widths = [IN_SEGS[i + 1] - IN_SEGS[i] for i in range(7)]

    def body(*refs):
        h_ref, p_refs, o_ref, acc_ref, sems = refs[0], refs[1:8], refs[8], refs[9], refs[10]
        t = pl.program_id(0)
        last = T // tm - 1

        @pl.when(t == 0)
        def _():
            acc_ref[...] = jnp.zeros_like(acc_ref)

        def accumulate(j):
            for s, lo, hi, off in _seg_pieces(j * IN_BLK, (j + 1) * IN_BLK):
                acc_ref[j, :, off:off + hi - lo] += _dot_tn(h_ref[...], p_refs[s][:, lo:hi])

        @pl.when(t < last)
        def _():
            for j in range(N_CHIPS):
                accumulate(j)

        @pl.when(t == last)
        def _():
            copies = [pltpu.make_async_copy(acc_ref.at[j], o_ref.at[j], sems.at[j]) for j in range(N_CHIPS)]
            for j in range(N_CHIPS):
                accumulate(j)
                copies[j].start()
            for cp in copies:
                cp.wait()

    row = lambda w: pl.BlockSpec((tm, w), lambda i: (i, 0))
    return pl.pallas_call(
        body, name="wgrad_in", grid=(T // tm,), in_specs=[row(D_MODEL)] + [row(w) for w in widths],
        out_specs=pl.BlockSpec(memory_space=pl.ANY),
        out_shape=jax.ShapeDtypeStruct((N_CHIPS, D_MODEL, IN_BLK), F32),
        scratch_shapes=[pltpu.VMEM((N_CHIPS, D_MODEL, IN_BLK), F32), pltpu.SemaphoreType.DMA((N_CHIPS,))],
        compiler_params=_params(1),
    )(h0, *dz_parts)


def _wgrad(a, g, name, blocked, cn, tm, phases=()):
    T, K = a.shape
    N = g.shape[1]
    nb = N // cn

    def body(a_ref, g_ref, o_ref):
        @pl.when(pl.program_id(1) == 0)
        def _():
            o_ref[...] = jnp.zeros_like(o_ref)

        o_ref[...] += _dot_tn(a_ref[...].astype(BF16), g_ref[...].astype(BF16))

    if blocked:
        out_spec = pl.BlockSpec((None, K, cn), lambda j, t: (j, 0, 0))
        out_shape = jax.ShapeDtypeStruct((nb, K, cn), F32)
    else:
        out_spec = pl.BlockSpec((K, cn), lambda j, t: (0, j))
        out_shape = jax.ShapeDtypeStruct((K, N), F32)
    outs, extra = _call(
        body, phases=phases, name=name, grid=(nb, T // tm),
        in_specs=[pl.BlockSpec((tm, K), lambda j, t: (t, 0)), pl.BlockSpec((tm, cn), lambda j, t: (t, j))],
        out_specs=out_spec, out_shape=out_shape,
    )(a, g)
    return outs[0], extra


def _shift_down(x, prev8, sft, row, row8, tm):
    xs = pltpu.roll(x, sft, 0)
    top = jnp.where(row8 < sft, pltpu.roll(prev8, sft, 0), xs[0:8])
    return jnp.concatenate([top, xs[8:]], axis=0)


def _shift_up(x, next8, sft, row8, tm):
    xs = pltpu.roll(x, tm - sft, 0)
    bot = jnp.where(row8 >= 8 - sft, pltpu.roll(next8, 8 - sft, 0), xs[tm - 8:tm])
    return jnp.concatenate([xs[0:tm - 8], bot], axis=0)


def _conv_fwd(x, prev8, cw_ref, cb, row, row8, tm):
    xc = cb + cw_ref[CONV_W - 1:CONV_W, :] * x
    for sft in range(1, CONV_W):
        j = CONV_W - 1 - sft
        xc = xc + cw_ref[j:j + 1, :] * _shift_down(x, prev8, sft, row, row8, tm)
    return xc


def _blockdiag_dot(xb, w_ref, transpose):
    outs = []
    for b in range(D_MODEL // LANES):
        xs = xb[:, b * LANES:(b + 1) * LANES]
        outs.append(_dot_nt(xs, w_ref[b]) if transpose else _dot(xs, w_ref[b]))
    return jnp.concatenate(outs, axis=1)


def _softplus_neg(lam):
    e = jnp.exp(-jnp.abs(lam))
    u = 1.0 + e
    log1p_e = jnp.where(u == 1.0, e, jnp.log(u) * (e / (u - 1.0)))
    sp = jnp.maximum(-lam, 0.0) + log1p_e
    return sp, -_sigmoid(-lam)


def _lru_gates(xc, wrg_ref, brg, wig_ref, big, sp):
    xcb = xc.astype(BF16)
    r = _sigmoid(_blockdiag_dot(xcb, wrg_ref, False) + brg)
    i = _sigmoid(_blockdiag_dot(xcb, wig_ref, False) + big)
    log_a = (-LRU_C) * r * sp
    a = jnp.exp(log_a)
    t = jnp.tanh(log_a)
    one_m_a2 = (-2.0) * t / (1.0 - t)
    mult = jnp.sqrt(one_m_a2)
    return xcb, r, i, a, mult


def _scan_down(a, b, row, tm):
    d = 1
    while d < tm:
        if d < 8:
            keep = row >= d
            a_s = jnp.where(keep, pltpu.roll(a, d, 0), 1.0)
            b_s = jnp.where(keep, pltpu.roll(b, d, 0), 0.0)
            b = a * b_s + b
            a = a * a_s
        else:
            b = jnp.concatenate([b[:d], a[d:] * b[:-d] + b[d:]], axis=0)
            a = jnp.concatenate([a[:d], a[d:] * a[:-d]], axis=0)
        d *= 2
    return a, b


def _scan_up(c, b, row, tm):
    d = 1
    while d < tm:
        if d < 8:
            keep = row < tm - d
            c_s = jnp.where(keep, pltpu.roll(c, tm - d, 0), 1.0)
            b_s = jnp.where(keep, pltpu.roll(b, tm - d, 0), 0.0)
            b = c * b_s + b
            c = c * c_s
        else:
            b = jnp.concatenate([c[:-d] * b[d:] + b[:-d], b[-d:]], axis=0)
            c = jnp.concatenate([c[:-d] * c[d:], c[-d:]], axis=0)
        d *= 2
    return c, b


def _rnn_fwd(xr, gr, conv_w, conv_b, wrg2, b_rg, wig2, b_ig, lam, n_seq, S, tm, phases=()):
    T = xr.shape[0]
    nt = S // tm
    W = D_MODEL

    def body(xr_ref, gr_ref, cw_ref, cb_ref, wrg_ref, brg_ref, wig_ref, big_ref, lam_ref,
             xc_ref, h_ref, r_ref, i_ref, a_ref, mult_ref, ya_ref, px_ref, ph_ref):
        @pl.when(pl.program_id(1) == 0)
        def _():
            px_ref[...] = jnp.zeros_like(px_ref)
            ph_ref[...] = jnp.zeros_like(ph_ref)

        row = lax.broadcasted_iota(jnp.int32, (tm, W), 0)
        row8 = lax.broadcasted_iota(jnp.int32, (8, W), 0)
        x = xr_ref[...]
        xc = _conv_fwd(x, px_ref[...], cw_ref, cb_ref[...], row, row8, tm)
        sp, _ = _softplus_neg(lam_ref[...])
        _, r, i, a, mult = _lru_gates(xc, wrg_ref, brg_ref[...], wig_ref, big_ref[...], sp)
        r_ref[...], i_ref[...], a_ref[...], mult_ref[...] = r, i, a, mult
        bterm = mult * (i * xc)
        acum, hloc = _scan_down(a, bterm, row, tm)
        h = hloc + acum * ph_ref[7:8, :]
        h_ref[...] = h
        xc_ref[...] = xc
        gelu, _ = _gelu_and_grad(gr_ref[...])
        ya_ref[...] = (h * gelu).astype(BF16)
        px_ref[...] = xr_ref[tm - 8:tm, :]
        ph_ref[...] = h_ref[tm - 8:tm, :]

    tile = pl.BlockSpec((tm, W), lambda s, t: (s * nt + t, 0))
    return _call(
        body, phases=phases, name="rnn_fwd", grid=(n_seq, nt),
        in_specs=[tile, tile, _const((CONV_W, W)), _const((1, W)), _const((8, LANES, LANES)), _const((1, W)),
                  _const((8, LANES, LANES)), _const((1, W)), _const((1, W))],
        out_specs=[tile] * 7,
        out_shape=[jax.ShapeDtypeStruct((T, W), F32)] * 6 + [jax.ShapeDtypeStruct((T, W), BF16)],
        scratch_shapes=[pltpu.VMEM((8, W), F32), pltpu.VMEM((8, W), F32)],
    )(xr, gr, conv_w, conv_b, wrg2, b_rg, wig2, b_ig, lam)


def _rnn_bwd(dya, xr, gr, xc, h, gates, conv_w, wrg2, wig2, lam, n_seq, S, tm, phases=()):
    T = xr.shape[0]
    nt = S // tm
    W = D_MODEL
    nb8 = tm // 8

    def body(dya_ref, xr_ref, gr_ref, xc_ref, h_ref, r_ref, i_ref, a_ref, mult_ref, xprev_ref, hprev_ref, cw_ref,
             wrg_ref, wig_ref, lam_ref, dxr_ref, dgr_ref, vec_ref, dwrg_ref, dwig_ref, cg_ref, ndxc_ref, tmp_ref):
        s, ti = pl.program_id(0), pl.program_id(1)

        @pl.when((s == 0) & (ti == 0))
        def _():
            vec_ref[...] = jnp.zeros_like(vec_ref)
            dwrg_ref[...] = jnp.zeros_like(dwrg_ref)
            dwig_ref[...] = jnp.zeros_like(dwig_ref)

        @pl.when(ti == 0)
        def _():
            cg_ref[...] = jnp.zeros_like(cg_ref)
            ndxc_ref[...] = jnp.zeros_like(ndxc_ref)

        first = ti == nt - 1
        row = lax.broadcasted_iota(jnp.int32, (tm, W), 0)
        row8 = lax.broadcasted_iota(jnp.int32, (8, W), 0)
        x = xr_ref[...]
        xc = xc_ref[...]
        hv = h_ref[...]
        xprev = jnp.where(first, 0.0, xprev_ref[...])
        hprev = jnp.where(first, 0.0, hprev_ref[...])
        sp, dsp_dlam = _softplus_neg(lam_ref[...])
        xcb = xc.astype(BF16)
        r, i, a, mult = r_ref[...], i_ref[...], a_ref[...], mult_ref[...]

        gelu, dgelu = _gelu_and_grad(gr_ref[...])
        dya_v = dya_ref[...]
        dgr_ref[...] = (dya_v * hv * dgelu).astype(BF16)
        dh = dya_v * gelu
        c = jnp.where(row < tm - 1, pltpu.roll(a, tm - 1, 0), 1.0)
        ccum, gloc = _scan_up(c, dh, row, tm)
        G = gloc + ccum * cg_ref[0:1, :]
        tmp_ref[...] = a * G
        cg_ref[...] = tmp_ref[0:8, :]

        h_m1 = _shift_down(hv, hprev, 1, row, row8, tm)
        ixc = i * xc
        dixc = G * mult
        dlog_a = (G * h_m1) * a - (G * ixc) * (a * a / mult)
        dr = dlog_a * ((-LRU_C) * sp)
        di = dixc * xc
        drg = dr * r * (1.0 - r)
        dig = di * i * (1.0 - i)
        vec_ref[7:8, :] += jnp.sum(dlog_a * ((-LRU_C) * r), axis=0, keepdims=True) * dsp_dlam
        vec_ref[5:6, :] += jnp.sum(drg, axis=0, keepdims=True)
        vec_ref[6:7, :] += jnp.sum(dig, axis=0, keepdims=True)
        drgb = drg.astype(BF16)
        digb = dig.astype(BF16)
        dxc = dixc * i + _blockdiag_dot(drgb, wrg_ref, True) + _blockdiag_dot(digb, wig_ref, True)
        for b in range(W // LANES):
            sl = slice(b * LANES, (b + 1) * LANES)
            dwrg_ref[b] += _dot_tn(xcb[:, sl], drgb[:, sl])
            dwig_ref[b] += _dot_tn(xcb[:, sl], digb[:, sl])

        vec_ref[4:5, :] += jnp.sum(dxc, axis=0, keepdims=True)
        vec_ref[3:4, :] += jnp.sum(dxc * x, axis=0, keepdims=True)
        dxr = cw_ref[CONV_W - 1:CONV_W, :] * dxc
        nxt = ndxc_ref[...]
        for sft in range(1, CONV_W):
            j = CONV_W - 1 - sft
            vec_ref[j:j + 1, :] += jnp.sum(dxc * _shift_down(x, xprev, sft, row, row8, tm), axis=0, keepdims=True)
            dxr = dxr + cw_ref[j:j + 1, :] * _shift_up(dxc, nxt, sft, row8, tm)
        dxr_ref[...] = dxr.astype(BF16)
        tmp_ref[...] = dxc
        ndxc_ref[...] = tmp_ref[0:8, :]

    rev = lambda s, t: (s * nt + nt - 1 - t, 0)
    tile = pl.BlockSpec((tm, W), rev)
    prev8 = pl.BlockSpec((8, W), lambda s, t: (jnp.maximum((s * nt + nt - 1 - t) * nb8 - 1, 0), 0))
    return _call(
        body, phases=phases, name="rnn_bwd", grid=(n_seq, nt),
        in_specs=[tile] * 9 + [prev8, prev8, _const((CONV_W, W)), _const((8, LANES, LANES)),
                               _const((8, LANES, LANES)), _const((1, W))],
        out_specs=[tile, tile, _const((16, W)), _const((8, LANES, LANES)), _const((8, LANES, LANES))],
        out_shape=[jax.ShapeDtypeStruct((T, W), BF16), jax.ShapeDtypeStruct((T, W), BF16),
                   jax.ShapeDtypeStruct((16, W), F32), jax.ShapeDtypeStruct((8, LANES, LANES), F32),
                   jax.ShapeDtypeStruct((8, LANES, LANES), F32)],
        scratch_shapes=[pltpu.VMEM((8, W), F32), pltpu.VMEM((8, W), F32), pltpu.VMEM((tm, W), F32)],
    )(dya, xr, gr, xc, h, *gates, xr, h, conv_w, wrg2, wig2, lam)


def _head_swap(t, lane):
    w = t.shape[1]
    return jnp.where(lane % HEAD_DIM < HEAD_DIM // 2, pltpu.roll(t, w - HEAD_DIM // 2, 1),
                     pltpu.roll(t, HEAD_DIM // 2, 1))


def _qk_prep(t, gain, cosf, sins, ind, indt, lane):
    ms = _split_dot(t * t, ind) * (1.0 / HEAD_DIM)
    rstd = _split_dot(lax.rsqrt(ms + NORM_EPS), indt)
    tn = (t * rstd) * gain
    return tn * cosf + _head_swap(tn, lane) * sins, rstd


def _qk_prep_bwd(dy, t, rstd, gain, cosf, sins, ind, indt, lane):
    dtn = dy * cosf + _head_swap(dy * sins, lane)
    dgain = jnp.sum(dtn * (t * rstd), axis=0, keepdims=True)
    dn = dtn * gain
    m = _split_dot(_split_dot(dn * t, ind), indt) * (1.0 / HEAD_DIM)
    return rstd * dn - t * (rstd * rstd * rstd * m), dgain


def _attn_mask_t(blk_idx):
    ci = lax.broadcasted_iota(jnp.int32, (2 * WINDOW, WINDOW), 0)
    qi = lax.broadcasted_iota(jnp.int32, (2 * WINDOW, WINDOW), 1)
    diff = WINDOW + qi - ci
    return (diff >= 0) & (diff < WINDOW) & ((ci >= WINDOW) | (blk_idx > 0))


def _stack_heads(t, kvh, lo):
    parts = []
    for i in (2 * kvh, 2 * kvh + 1):
        tp = t[:, i * LANES:(i + 1) * LANES]
        parts += [jnp.where(lo, tp, 0.0), jnp.where(lo, 0.0, tp)]
    return jnp.concatenate(parts, axis=0).astype(BF16)


def _unstack_heads(ts, lo):
    w = WINDOW
    return jnp.where(lo, ts[0:w], ts[w:2 * w]), jnp.where(lo, ts[2 * w:3 * w], ts[3 * w:4 * w])


def _dup_head(t, kvh, lo2):
    m = kvh // 2
    t2 = t[:, m * LANES:(m + 1) * LANES]
    t2r = pltpu.roll(t2, HEAD_DIM, 1)
    return (jnp.where(lo2, t2, t2r) if kvh % 2 == 0 else jnp.where(lo2, t2r, t2)).astype(BF16)


def _fold_head(ts, kvh, lo2):
    tot = ts + pltpu.roll(ts, HEAD_DIM, 1)
    own = lo2 if kvh % 2 == 0 else ~lo2
    return jnp.where(own, tot, 0.0)


KEY_CHUNKS = tuple(slice(i * 64, (i + 1) * 64) for i in range(2 * WINDOW // 64))


def _fold8(x, op):
    return op(x.reshape(x.shape[0] // 8, 8, x.shape[1]), axis=0)


def _softmax_stats(s_ref, b, cols, sink):
    m8 = None
    for c in KEY_CHUNKS:
        t = _fold8(s_ref[b, c, cols], jnp.max)
        m8 = t if m8 is None else jnp.maximum(m8, t)
    mx = jnp.maximum(jnp.max(m8, axis=0, keepdims=True), sink)
    d8 = None
    for c in KEY_CHUNKS:
        t = _fold8(jnp.exp(s_ref[b, c, cols] - mx), jnp.sum)
        d8 = t if d8 is None else d8 + t
    es = jnp.exp(sink - mx)
    inv = 1.0 / (jnp.sum(d8, axis=0, keepdims=True) + es)
    return mx, inv, es * inv


def _attn_fwd(q, k, v, qg, kg, sinks, cosf, sins, ind_q, ind_qt, ind_k, ind_kt, n_seq, S, phases=()):
    T = q.shape[0]
    nblk = S // WINDOW
    W = D_MODEL

    def body(sink_ref, q_ref, k_ref, v_ref, qg_ref, kg_ref, cos_ref, sin_ref, iq_ref, iqt_ref, ik_ref, ikt_ref,
             o_ref, kc_ref, vc_ref, s_ref, p_ref, qs_ref, kd_ref, vd_ref):
        @pl.when(pl.program_id(1) == 0)
        def _():
            kc_ref[...] = jnp.zeros_like(kc_ref)
            vc_ref[...] = jnp.zeros_like(vc_ref)

        lane = lax.broadcasted_iota(jnp.int32, (WINDOW, W), 1)
        lo = lane[:, :LANES] < HEAD_DIM
        lo2 = lax.broadcasted_iota(jnp.int32, (2 * WINDOW, LANES), 1) < HEAD_DIM

        def one_block(h):
            n = pl.program_id(1) * bps + h
            rows = slice(h * WINDOW, (h + 1) * WINDOW)
            cosf, sinv = jnp.tile(cos_ref[rows, :], (1, W // LANES)), jnp.tile(sin_ref[rows, :], (1, W // LANES))
            qr, _ = _qk_prep(q_ref[rows, :], qg_ref[...], cosf, sinv, iq_ref[...], iqt_ref[...], lane)
            kr, _ = _qk_prep(k_ref[rows, :], kg_ref[...], cosf[:, :KV_W], sinv[:, :KV_W], ik_ref[...], ikt_ref[...],
                             lane[:, :KV_W])
            kc_ref[WINDOW:2 * WINDOW, :] = kr
            vc_ref[WINDOW:2 * WINDOW, :] = v_ref[rows, :]
            kc, vc = kc_ref[...], vc_ref[...]
            mask = jnp.tile(_attn_mask_t(n), (1, 4))
            qr = qr * HEAD_DIM ** -0.5
            for kvh in range(N_KV):
                qs_ref[h, kvh] = _stack_heads(qr, kvh, lo)
                kd_ref[h, kvh] = _dup_head(kc, kvh, lo2)
                vd_ref[h, kvh] = _dup_head(vc, kvh, lo2)

            def scores(kvh):
                s_ref[h, kvh % 2] = jnp.where(mask, _dot_nt(kd_ref[h, kvh], qs_ref[h, kvh]), -1e30)

            def softmax(kvh):
                sb, pb = s_ref.at[h], p_ref.at[h]
                b = kvh % 2
                for r in range(4):
                    cols = slice(r * WINDOW, (r + 1) * WINDOW)
                    mx, inv, _ = _softmax_stats(sb, b, cols, sink_ref[4 * kvh + r])
                    for c in KEY_CHUNKS:
                        pb[b, c, cols] = (jnp.exp(sb[b, c, cols] - mx) * inv).astype(BF16)

            def output(kvh):
                o0, o1 = _unstack_heads(_dot_tn(p_ref[h, kvh % 2], vd_ref[h, kvh]), lo)
                o_ref[rows, (2 * kvh) * LANES:(2 * kvh + 1) * LANES] = o0.astype(BF16)
                o_ref[rows, (2 * kvh + 1) * LANES:(2 * kvh + 2) * LANES] = o1.astype(BF16)

            scores(0)
            for kvh in range(N_KV):
                if kvh + 1 < N_KV:
                    scores(kvh + 1)
                softmax(kvh)
                output(kvh)
            kc_ref[0:WINDOW, :] = kr
            vc_ref[0:WINDOW, :] = v_ref[rows, :]

        for h in range(bps):
            one_block(h)

    bps = ATTN_BLOCKS_PER_STEP
    rows_step = bps * WINDOW
    blk = lambda w: pl.BlockSpec((rows_step, w), lambda s, n: (s * (nblk // bps) + n, 0))
    pos = pl.BlockSpec((rows_step, LANES), lambda s, n: (n, 0))
    outs, extra = _call(
        body, phases=phases, name="attn_fwd", grid=(n_seq, nblk // bps),
        in_specs=[pl.BlockSpec(memory_space=pltpu.SMEM), blk(W), blk(KV_W), blk(KV_W), _const((1, W)),
                  _const((1, KV_W)), pos, pos, _const((W, LANES)), _const((LANES, W)), _const((KV_W, LANES)),
                  _const((LANES, KV_W))],
        out_specs=blk(W), out_shape=jax.ShapeDtypeStruct((T, W), BF16),
        scratch_shapes=[pltpu.VMEM((2 * WINDOW, KV_W), F32), pltpu.VMEM((2 * WINDOW, KV_W), F32),
                        pltpu.VMEM((bps, 2, 2 * WINDOW, 4 * WINDOW), F32),
                        pltpu.VMEM((bps, 2, 2 * WINDOW, 4 * WINDOW), BF16),
                        pltpu.VMEM((bps, N_KV, 4 * WINDOW, LANES), BF16),
                        pltpu.VMEM((bps, N_KV, 2 * WINDOW, LANES), BF16),
                        pltpu.VMEM((bps, N_KV, 2 * WINDOW, LANES), BF16)],
    )(sinks, q, k, v, qg, kg, cosf, sins, ind_q, ind_qt, ind_k, ind_kt)
    return outs[0], extra


def _attn_bwd(do, q, k, v, qg, kg, sinks, cosf, sins, ind_q, ind_qt, ind_k, ind_kt, n_seq, S, phases=()):
    T = q.shape[0]
    nblk = S // WINDOW
    W = D_MODEL

    def body(sink_ref, do_ref, q_ref, k_ref, v_ref, qg_ref, kg_ref, cos_ref, sin_ref, iq_ref, iqt_ref, ik_ref,
             ikt_ref, dq_ref, dkc_ref, dkp_ref, dvc_ref, dvp_ref, dqg_ref, dsk_ref, kc_ref, vc_ref, dqr_ref,
             dk_ref, dv_ref, s_ref, dp_ref, p_ref, ds_ref, qs_ref, dos_ref, kd_ref, vd_ref):
        s_id, n_step = pl.program_id(0), pl.program_id(1)

        @pl.when((s_id == 0) & (n_step == 0))
        def _():
            dqg_ref[...] = jnp.zeros_like(dqg_ref)
            dsk_ref[...] = jnp.zeros_like(dsk_ref)

        @pl.when(n_step == 0)
        def _():
            kc_ref[...] = jnp.zeros_like(kc_ref)
            vc_ref[...] = jnp.zeros_like(vc_ref)

        lane = lax.broadcasted_iota(jnp.int32, (WINDOW, W), 1)
        lane_k = lane[:, :KV_W]
        lo = lane[:, :LANES] < HEAD_DIM
        lo2 = lax.broadcasted_iota(jnp.int32, (2 * WINDOW, LANES), 1) < HEAD_DIM
        scale = HEAD_DIM ** -0.5

        def one_block(h):
            n = n_step * bps + h
            rows = slice(h * WINDOW, (h + 1) * WINDOW)
            cosf, sinv = jnp.tile(cos_ref[rows, :], (1, W // LANES)), jnp.tile(sin_ref[rows, :], (1, W // LANES))
            qv = q_ref[rows, :]
            qr, q_rstd = _qk_prep(qv, qg_ref[...], cosf, sinv, iq_ref[...], iqt_ref[...], lane)
            kr, _ = _qk_prep(k_ref[rows, :], kg_ref[...], cosf[:, :KV_W], sinv[:, :KV_W], ik_ref[...], ikt_ref[...],
                             lane_k)
            kc_ref[WINDOW:2 * WINDOW, :] = kr
            vc_ref[WINDOW:2 * WINDOW, :] = v_ref[rows, :]
            kc, vc = kc_ref[...], vc_ref[...]
            dov = do_ref[rows, :]
            mask = jnp.tile(_attn_mask_t(n), (1, 4))
            qr = qr * scale
            dk_ref[h] = jnp.zeros((2 * WINDOW, KV_W), F32)
            dv_ref[h] = jnp.zeros((2 * WINDOW, KV_W), F32)
            for kvh in range(N_KV):
                qs_ref[h, kvh] = _stack_heads(qr, kvh, lo)
                dos_ref[h, kvh] = _stack_heads(dov, kvh, lo)
                kd_ref[h, kvh] = _dup_head(kc, kvh, lo2)
                vd_ref[h, kvh] = _dup_head(vc, kvh, lo2)
            sb, dpb, pb, dsb = s_ref.at[h], dp_ref.at[h], p_ref.at[h], ds_ref.at[h]

            def scores(kvh):
                b = kvh % 2
                sb[b] = jnp.where(mask, _dot_nt(kd_ref[h, kvh], qs_ref[h, kvh]), -1e30)
                dpb[b] = _dot_nt(vd_ref[h, kvh], dos_ref[h, kvh])

            def softmax(kvh):
                b = kvh % 2
                for r in range(4):
                    cols = slice(r * WINDOW, (r + 1) * WINDOW)
                    head = 4 * kvh + r
                    mx, inv, ps = _softmax_stats(sb, b, cols, sink_ref[head])
                    g8 = None
                    for c in KEY_CHUNKS:
                        t = _fold8(jnp.exp(sb[b, c, cols] - mx) * dpb[b, c, cols], jnp.sum)
                        g8 = t if g8 is None else g8 + t
                    dd = jnp.sum(g8, axis=0, keepdims=True) * inv
                    for c in KEY_CHUNKS:
                        p = jnp.exp(sb[b, c, cols] - mx) * inv
                        pb[b, c, cols] = p.astype(BF16)
                        dsb[b, c, cols] = (p * (dpb[b, c, cols] - dd)).astype(BF16)
                    dsk_ref[head:head + 1, :] -= ps * dd

            def grads(kvh):
                m, b = kvh // 2, kvh % 2
                dq0, dq1 = _unstack_heads(_dot_tn(dsb[b], kd_ref[h, kvh]) * scale, lo)
                dqr_ref[h, :, (2 * kvh) * LANES:(2 * kvh + 1) * LANES] = dq0
                dqr_ref[h, :, (2 * kvh + 1) * LANES:(2 * kvh + 2) * LANES] = dq1
                dk_ref[h, :, m * LANES:(m + 1) * LANES] += _fold_head(_dot(dsb[b], qs_ref[h, kvh]), kvh, lo2)
                dv_ref[h, :, m * LANES:(m + 1) * LANES] += _fold_head(_dot(pb[b], dos_ref[h, kvh]), kvh, lo2)

            scores(0)
            for kvh in range(N_KV):
                if kvh + 1 < N_KV:
                    scores(kvh + 1)
                softmax(kvh)
                grads(kvh)
            dq, dqg = _qk_prep_bwd(dqr_ref[h], qv, q_rstd, qg_ref[...], cosf, sinv, iq_ref[...], iqt_ref[...], lane)
            dq_ref[rows, :] = dq.astype(BF16)
            dqg_ref[...] += dqg
            dkp_ref[rows, :] = dk_ref[h, 0:WINDOW, :]
            dkc_ref[rows, :] = dk_ref[h, WINDOW:2 * WINDOW, :]
            dvp_ref[rows, :] = dv_ref[h, 0:WINDOW, :]
            dvc_ref[rows, :] = dv_ref[h, WINDOW:2 * WINDOW, :]
            kc_ref[0:WINDOW, :] = kr
            vc_ref[0:WINDOW, :] = v_ref[rows, :]

        for h in range(bps):
            one_block(h)

    bps = ATTN_BLOCKS_PER_STEP
    rows_step = bps * WINDOW
    blk = lambda w: pl.BlockSpec((rows_step, w), lambda s, n: (s * (nblk // bps) + n, 0))
    pos = pl.BlockSpec((rows_step, LANES), lambda s, n: (n, 0))
    kv_out = jax.ShapeDtypeStruct((T, KV_W), F32)
    stage = lambda dt: pltpu.VMEM((bps, 2, 2 * WINDOW, 4 * WINDOW), dt)
    return _call(
        body, phases=phases, name="attn_bwd", grid=(n_seq, nblk // bps),
        in_specs=[pl.BlockSpec(memory_space=pltpu.SMEM), blk(W), blk(W), blk(KV_W), blk(KV_W), _const((1, W)),
                  _const((1, KV_W)), pos, pos, _const((W, LANES)), _const((LANES, W)), _const((KV_W, LANES)),
                  _const((LANES, KV_W))],
        out_specs=[blk(W), blk(KV_W), blk(KV_W), blk(KV_W), blk(KV_W), _const((1, W)), _const((N_HEADS, LANES))],
        out_shape=[jax.ShapeDtypeStruct((T, W), BF16), kv_out, kv_out, kv_out, kv_out,
                   jax.ShapeDtypeStruct((1, W), F32), jax.ShapeDtypeStruct((N_HEADS, LANES), F32)],
        scratch_shapes=[pltpu.VMEM((2 * WINDOW, KV_W), F32), pltpu.VMEM((2 * WINDOW, KV_W), F32),
                        pltpu.VMEM((bps, WINDOW, W), F32), pltpu.VMEM((bps, 2 * WINDOW, KV_W), F32),
                        pltpu.VMEM((bps, 2 * WINDOW, KV_W), F32), stage(F32), stage(F32), stage(BF16), stage(BF16),
                        pltpu.VMEM((bps, N_KV, 4 * WINDOW, LANES), BF16),
                        pltpu.VMEM((bps, N_KV, 4 * WINDOW, LANES), BF16),
                        pltpu.VMEM((bps, N_KV, 2 * WINDOW, LANES), BF16),
                        pltpu.VMEM((bps, N_KV, 2 * WINDOW, LANES), BF16)],
    )(sinks, do, q, k, v, qg, kg, cosf, sins, ind_q, ind_qt, ind_k, ind_kt)


def _kv_bwd(dkc, dkp, dvc, dvp, k, kg, cosf, sins, ind_k, ind_kt, n_seq, S, phases=()):
    T = k.shape[0]
    nblk = S // WINDOW
    nb = min(KV_BLOCKS_PER_STEP, nblk)
    rows, nt = nb * WINDOW, nblk // nb

    def body(dkc_ref, dkp_ref, dkn_ref, dvc_ref, dvp_ref, dvn_ref, k_ref, kg_ref, cos_ref, sin_ref, ik_ref, ikt_ref,
             dk_ref, dv_ref, dkg_ref):
        s_id, n = pl.program_id(0), pl.program_id(1)

        @pl.when((s_id == 0) & (n == 0))
        def _():
            dkg_ref[...] = jnp.zeros_like(dkg_ref)

        blk = n * nb + lax.broadcasted_iota(jnp.int32, (rows, KV_W), 0) // WINDOW
        has_next = blk < nblk - 1

        def from_next(part_ref, next_ref):
            moved = jnp.concatenate([part_ref[WINDOW:rows, :], next_ref[...]], axis=0) if nb > 1 else next_ref[...]
            return jnp.where(has_next, moved, 0.0)

        lane = lax.broadcasted_iota(jnp.int32, (rows, KV_W), 1)
        dkr = dkc_ref[...] + from_next(dkp_ref, dkn_ref)
        dv_ref[...] = (dvc_ref[...] + from_next(dvp_ref, dvn_ref)).astype(BF16)
        cosf, sinv = jnp.tile(cos_ref[...], (1, KV_W // LANES)), jnp.tile(sin_ref[...], (1, KV_W // LANES))
        kv = k_ref[...]
        _, rstd = _qk_prep(kv, kg_ref[...], cosf, sinv, ik_ref[...], ikt_ref[...], lane)
        dk, dkg = _qk_prep_bwd(dkr, kv, rstd, kg_ref[...], cosf, sinv, ik_ref[...], ikt_ref[...], lane)
        dk_ref[...] = dk.astype(BF16)
        dkg_ref[...] += dkg

    cur = pl.BlockSpec((rows, KV_W), lambda s, n: (s * nt + n, 0))
    nxt = pl.BlockSpec((WINDOW, KV_W), lambda s, n: (s * nblk + jnp.minimum((n + 1) * nb, nblk - 1), 0))
    pos = pl.BlockSpec((rows, LANES), lambda s, n: (n, 0))
    return _call(
        body, phases=phases, name="kv_bwd", grid=(n_seq, nt),
        in_specs=[cur, cur, nxt, cur, cur, nxt, cur, _const((1, KV_W)), pos, pos, _const((KV_W, LANES)),
                  _const((LANES, KV_W))],
        out_specs=[cur, cur, _const((1, KV_W))],
        out_shape=[jax.ShapeDtypeStruct((T, KV_W), BF16), jax.ShapeDtypeStruct((T, KV_W), BF16),
                   jax.ShapeDtypeStruct((1, KV_W), F32)],
    )(dkc, dkp, dkp, dvc, dvp, dvp, k, kg, cosf, sins, ind_k, ind_kt)


def _merge_fwd(x, ya, o, ga, gb, w_rnn, w_attn, w_out, tm, phases=()):
    T = x.shape[0]
    W = D_MODEL

    def body(x_ref, ya_ref, o_ref, ga_ref, gb_ref, wr_ref, wa_ref, wo_ref, x1_ref, mg_ref):
        y_a = _dot(ya_ref[...], wr_ref[...])
        y_b = _dot(o_ref[...], wa_ref[...])
        mg = (_sigmoid(ga_ref[...]) * y_a + _sigmoid(gb_ref[...]) * y_b).astype(BF16)
        mg_ref[...] = mg
        x1_ref[...] = x_ref[...] + _dot(mg, wo_ref[...])

    row = pl.BlockSpec((tm, W), lambda i: (i, 0))
    sq = _const((W, W))
    return _call(
        body, phases=phases, name="merge_fwd", grid=(T // tm,),
        in_specs=[row, row, row, row, row, sq, sq, sq], out_specs=[row, row],
        out_shape=[jax.ShapeDtypeStruct((T, W), F32), jax.ShapeDtypeStruct((T, W), BF16)],
    )(x, ya, o, ga, gb, w_rnn, w_attn, w_out)


def _merge_bwd(dx1, ga, gb, ya, o, w_rnn, w_attn, w_out, tm, phases=()):
    T = dx1.shape[0]
    W = D_MODEL

    def body(dx1_ref, ga_ref, gb_ref, ya_ref, o_ref, wr_ref, wa_ref, wo_ref,
             dga_ref, dgb_ref, dya_ref, dyb_ref, dyain_ref, do_ref):
        dm = _dot_nt(dx1_ref[...].astype(BF16), wo_ref[...])
        sa = _sigmoid(ga_ref[...])
        sb = _sigmoid(gb_ref[...])
        dga_ref[...] = (dm * _dot(ya_ref[...], wr_ref[...]) * (sa * (1.0 - sa))).astype(BF16)
        dgb_ref[...] = (dm * _dot(o_ref[...], wa_ref[...]) * (sb * (1.0 - sb))).astype(BF16)
        dya = (dm * sa).astype(BF16)
        dyb = (dm * sb).astype(BF16)
        dya_ref[...] = dya
        dyb_ref[...] = dyb
        dyain_ref[...] = _dot_nt(dya, wr_ref[...])
        do_ref[...] = _dot_nt(dyb, wa_ref[...])

    row = pl.BlockSpec((tm, W), lambda i: (i, 0))
    sq = _const((W, W))
    b16 = jax.ShapeDtypeStruct((T, W), BF16)
    f32 = jax.ShapeDtypeStruct((T, W), F32)
    return _call(
        body, phases=phases, name="merge_bwd", grid=(T // tm,),
        in_specs=[row, row, row, row, row, sq, sq, sq], out_specs=[row] * 6,
        out_shape=[b16, b16, b16, b16, f32, f32],
    )(dx1, ga, gb, ya, o, w_rnn, w_attn, w_out)


def _mlp_fwd(x1, g_mlp, w_up, w_down, tm, phases=()):
    T = x1.shape[0]
    W = D_MODEL

    def body(x_ref, g_ref, wu_ref, wd_ref, x2_ref, hm_ref, u_ref, act_ref):
        xv = x_ref[...]
        hm, _ = _rms_fwd(xv, g_ref[...])
        hmb = hm.astype(BF16)
        hm_ref[...] = hmb
        for j in range(N_CHIPS):
            u = _dot(hmb, wu_ref[j])
            u_ref[:, j * W:(j + 1) * W] = u
            ru = jnp.maximum(u, 0.0)
            act_ref[:, j * W:(j + 1) * W] = (ru * ru).astype(BF16)
        x2_ref[...] = xv + _dot(act_ref[...], wd_ref[...])

    row = lambda w: pl.BlockSpec((tm, w), lambda i: (i, 0))
    return _call(
        body, phases=phases, name="mlp_fwd", grid=(T // tm,),
        in_specs=[row(W), _const((1, W)), _const((N_CHIPS, W, W)), _const((D_FF, W))],
        out_specs=[row(W), row(W), row(D_FF), row(D_FF)],
        out_shape=[jax.ShapeDtypeStruct((T, W), F32), jax.ShapeDtypeStruct((T, W), BF16),
                   jax.ShapeDtypeStruct((T, D_FF), F32), jax.ShapeDtypeStruct((T, D_FF), BF16)],
    )(x1, g_mlp, w_up, w_down)


def _mlp_bwd(dx2, u, x1, g_mlp, w_up, w_down, tm, phases=()):
    T = x1.shape[0]
    W = D_MODEL

    def body(dx2_ref, u_ref, x_ref, g_ref, wu_ref, wd_ref, dx1_ref, du_ref, dg_ref):
        @pl.when(pl.program_id(0) == 0)
        def _():
            dg_ref[...] = jnp.zeros_like(dg_ref)

        dx2 = dx2_ref[...]
        dact = _dot_nt(dx2.astype(BF16), wd_ref[...])
        du_ref[...] = (dact * (2.0 * jnp.maximum(u_ref[...], 0.0))).astype(BF16)
        dhm = jnp.zeros((tm, W), F32)
        for j in range(N_CHIPS):
            dhm = dhm + _dot_nt(du_ref[:, j * W:(j + 1) * W], wu_ref[j])
        xv = x_ref[...]
        g = g_ref[...]
        _, r = _rms_fwd(xv, g)
        dx, dg = _rms_bwd(dhm, xv, r, g)
        dx1_ref[...] = dx2 + dx
        dg_ref[...] += dg

    row = lambda w: pl.BlockSpec((tm, w), lambda i: (i, 0))
    return _call(
        body, phases=phases, name="mlp_bwd", grid=(T // tm,),
        in_specs=[row(W), row(D_FF), row(W), _const((1, W)), _const((N_CHIPS, W, W)), _const((D_FF, W))],
        out_specs=[row(W), row(D_FF), _const((1, W))],
        out_shape=[jax.ShapeDtypeStruct((T, W), F32), jax.ShapeDtypeStruct((T, D_FF), BF16),
                   jax.ShapeDtypeStruct((1, W), F32)],
    )(dx2, u, x1, g_mlp, w_up, w_down)


def _ple_loss(x2, p, target, g_ple, w_gate, w_proj, tm, phases=()):
    T = x2.shape[0]
    W = D_MODEL
    cw = W // N_CHIPS

    def body(x_ref, p_ref, t_ref, g_ref, wg_ref, wp_ref, loss_ref, dx2_ref, pb_ref, de_ref, hp_ref, dtg_ref, dg_ref):
        @pl.when(pl.program_id(0) == 0)
        def _():
            dg_ref[...] = jnp.zeros_like(dg_ref)
            loss_ref[...] = jnp.zeros_like(loss_ref)

        xv = x_ref[...]
        g = g_ref[...]
        pb = p_ref[...].astype(BF16)
        pb_ref[...] = pb
        e = jnp.concatenate([_dot(pb, wp_ref[j]) for j in range(N_CHIPS)], axis=1)
        hp, r = _rms_fwd(xv, g)
        hpb = hp.astype(BF16)
        hp_ref[...] = hpb
        sg = _sigmoid(_dot(hpb, wg_ref[...]))
        diff = (xv + e * sg) - t_ref[...]
        loss_ref[...] += jnp.sum(diff * diff) * (0.5 / W)
        dx3 = diff * (1.0 / W)
        de_ref[...] = (dx3 * sg).astype(BF16)
        dtg = (dx3 * e * (sg * (1.0 - sg))).astype(BF16)
        dtg_ref[...] = dtg
        dx, dg = _rms_bwd(_dot_nt(dtg, wg_ref[...]), xv, r, g)
        dx2_ref[...] = dx3 + dx
        dg_ref[...] += dg

    row = lambda w: pl.BlockSpec((tm, w), lambda i: (i, 0))
    b16 = lambda w: jax.ShapeDtypeStruct((T, w), BF16)
    return _call(
        body, phases=phases, name="ple_loss", grid=(T // tm,),
        in_specs=[row(W), row(PLE_DIM), row(W), _const((1, W)), _const((W, W)), _const((N_CHIPS, PLE_DIM, cw))],
        out_specs=[_const((8, LANES)), row(W), row(PLE_DIM), row(W), row(W), row(W), _const((1, W))],
        out_shape=[jax.ShapeDtypeStruct((8, LANES), F32), jax.ShapeDtypeStruct((T, W), F32), b16(PLE_DIM),
                   b16(W), b16(W), b16(W), jax.ShapeDtypeStruct((1, W), F32)],
    )(x2, p, target, g_ple, w_gate, w_proj)


def _adamw(w, g, m, v, name, tr, phases=()):
    R, C = w.shape
    c1 = 1.0 / (1.0 - ADAM_B1 ** ADAM_STEP)
    c2 = 1.0 / (1.0 - ADAM_B2 ** ADAM_STEP)

    def body(w_ref, g_ref, m_ref, v_ref, go_ref, d_ref, nm_ref, nv_ref):
        gv = g_ref[...]
        go_ref[...] = gv
        nm = ADAM_B1 * m_ref[...] + (1.0 - ADAM_B1) * gv
        nv = ADAM_B2 * v_ref[...] + (1.0 - ADAM_B2) * (gv * gv)
        nm_ref[...] = nm
        nv_ref[...] = nv
        d_ref[...] = (-ADAM_LR) * ((nm * c1) / (jnp.sqrt(nv * c2) + ADAM_EPS) + ADAM_WD * w_ref[...])

    row = pl.BlockSpec((tr, C), lambda i: (i, 0))
    sds = jax.ShapeDtypeStruct((R, C), F32)
    return _call(
        body, phases=phases, name=name, grid=(R // tr,), in_specs=[row] * 4, out_specs=[row] * 4,
        out_shape=[sds] * 4,
    )(w, g, m, v)


def _indicator(width):
    ind = np.zeros((width, LANES), np.float32)
    ind[np.arange(width), np.arange(width) // HEAD_DIM] = 1.0
    return jnp.asarray(ind, BF16), jnp.asarray(ind.T, BF16)


def _rope_tables(S):
    inv = ROPE_THETA ** (-jnp.arange(0, HEAD_DIM, 2, dtype=F32) / HEAD_DIM)
    ang = jnp.arange(S, dtype=F32)[:, None] * inv[None, :]
    cos, sin = jnp.cos(ang), jnp.sin(ang)
    cosf = jnp.tile(jnp.concatenate([cos, cos], axis=1), (1, LANES // HEAD_DIM))
    sins = jnp.tile(jnp.concatenate([-sin, sin], axis=1), (1, LANES // HEAD_DIM))
    return cosf, sins


def _pair_blockdiag(w):
    w4 = w.reshape(8, 2, HEAD_DIM, HEAD_DIM)
    eye = jnp.eye(2, dtype=w.dtype)
    return jnp.einsum("bpij,pq->bpiqj", w4, eye).reshape(8, LANES, LANES)


def _pair_blockdiag_extract(g):
    g5 = g.reshape(8, 2, HEAD_DIM, 2, HEAD_DIM)
    return jnp.stack([g5[:, 0, :, 0, :], g5[:, 1, :, 1, :]], axis=1).reshape(16, HEAD_DIM, HEAD_DIM)


def _pair_sum(parts, sibs, name):
    n = len(parts)
    dims = [(p.shape[1] // 2, p.shape[2]) for p in parts]

    def body(*refs):
        p_r, s_r, send_r, own_r, mine_r, sem = (refs[0:n], refs[n:2 * n], refs[2 * n:3 * n], refs[3 * n:4 * n],
                                                refs[4 * n:5 * n], refs[5 * n])
        x, y, c, chips = _mesh_pos()
        me = 2 * x + y
        loads = []
        for i, (R, _) in enumerate(dims):
            mine, _ = _half_rows(c, R)
            cp = pltpu.make_async_copy(p_r[i].at[:, mine, :], mine_r[i], sem.at[i])
            cp.start()
            loads.append(cp)
        for i in range(n):
            loads[i].wait()
            for j, (cx, cy) in enumerate(chips):
                k = 2 * cx + cy
                send_r[i][j] = (mine_r[i][k] + s_r[i][k]).astype(BF16)
            own_r[i][...] = mine_r[i][me] + s_r[i][me]

    vm = pl.BlockSpec(memory_space=pltpu.VMEM)
    out = pl.pallas_call(
        body, name=name, in_specs=[pl.BlockSpec(memory_space=pl.ANY)] * n + [vm] * n, out_specs=[vm] * (2 * n),
        out_shape=[jax.ShapeDtypeStruct((3, R, C), BF16) for R, C in dims]
        + [jax.ShapeDtypeStruct((R, C), F32) for R, C in dims],
        scratch_shapes=[pltpu.VMEM((N_CHIPS, R, C), F32) for R, C in dims] + [pltpu.SemaphoreType.DMA((n,))],
        compiler_params=pltpu.CompilerParams(vmem_limit_bytes=VMEM_LIMIT),
    )(*parts, *sibs)
    return out[:n], out[n:]


def _chip_sum(owns, recvs, name):
    n = len(owns)
    dims = [o.shape for o in owns]

    def body(*refs):
        own_r, recv_r, red_r, stage_r, sem = refs[0:n], refs[n:2 * n], refs[2 * n:3 * n], refs[3 * n:4 * n], refs[4 * n]
        x, y, c, _ = _mesh_pos()
        me = 2 * x + y
        stores = []
        for i, (R, _) in enumerate(dims):
            for k_me in range(N_CHIPS):

                @pl.when(me == k_me)
                def _():
                    acc = None
                    for k in range(N_CHIPS):
                        slot = ((k // 2) ^ (k_me // 2)) + 2 * ((k % 2) ^ (k_me % 2)) - 1
                        term = own_r[i][...] if k == k_me else recv_r[i][slot].astype(F32)
                        acc = term if acc is None else acc + term
                    stage_r[i][...] = acc

            mine, _ = _half_rows(c, R)
            cp = pltpu.make_async_copy(stage_r[i], red_r[i].at[mine, :], sem.at[i])
            cp.start()
            stores.append(cp)
        for cp in stores:
            cp.wait()

    vm = pl.BlockSpec(memory_space=pltpu.VMEM)
    return pl.pallas_call(
        body, name=name, in_specs=[vm] * (2 * n), out_specs=[pl.BlockSpec(memory_space=pl.ANY)] * n,
        out_shape=[jax.ShapeDtypeStruct((2 * R, C), F32) for R, C in dims],
        scratch_shapes=[pltpu.VMEM((R, C), F32) for R, C in dims] + [pltpu.SemaphoreType.DMA((n,))],
        compiler_params=pltpu.CompilerParams(vmem_limit_bytes=VMEM_LIMIT),
    )(*owns, *recvs)


def _gather_bf16(shard, name):
    R2, C = shard.shape
    R = R2 // 2
    H = R // 2

    def body(s_ref, o_ref, send_sems, recv_sems):
        x, y, c, _ = _mesh_pos()
        me, chip_x, chip_y, chip_d = 2 * x + y, 2 * (1 - x) + y, 2 * x + (1 - y), 2 * (1 - x) + (1 - y)
        to_x, to_y, me_dev, sibling = (1 - x, y, c), (x, 1 - y, c), (x, y, c), (x, y, 1 - c)

        def rows(core, off, n):
            return pl.ds(pl.multiple_of(core * R + off, H), n)

        def copy(k, chip, rws, to):
            blk = o_ref.at[chip, rws]
            return _remote(blk, blk, (send_sems.at[k], recv_sems.at[k]), to)

        piece, half_a, half_b = rows(c, 0, R), rows(c, 0, H), rows(c, H, H)
        o_ref[me] = s_ref[...].astype(BF16)
        sends = [copy(0, me, piece, to_x), copy(1, me, piece, to_y)]
        for cp in sends:
            cp.start()
        arrivals = [(0, chip_x, piece, (2, half_a, to_y)), (1, chip_y, piece, (3, half_b, to_x)),
                    (2, chip_d, half_a, None), (3, chip_d, half_b, None)]
        for k, chip, rws, onward in arrivals:
            copy(k, chip, rws, me_dev).wait_recv()
            if onward is not None:
                sends.append(copy(onward[0], chip, onward[1], onward[2]))
                sends[-1].start()
            sends.append(copy(4 + k, chip, rws, sibling))
            sends[-1].start()
        for k, chip, rws in [(4, chip_x, rows(1 - c, 0, R)), (5, chip_y, rows(1 - c, 0, R)),
                             (6, chip_d, rows(1 - c, 0, H)), (7, chip_d, rows(1 - c, H, H))]:
            copy(k, chip, rws, me_dev).wait_recv()
        for cp in sends:
            cp.wait_send()

    return pl.pallas_call(
        body, name=name, out_shape=jax.ShapeDtypeStruct((N_CHIPS, R2, C), BF16),
        in_specs=[pl.BlockSpec(memory_space=pltpu.VMEM)], out_specs=pl.BlockSpec(memory_space=pltpu.VMEM),
        scratch_shapes=[pltpu.SemaphoreType.DMA((8,)), pltpu.SemaphoreType.DMA((8,))],
        compiler_params=pltpu.CompilerParams(vmem_limit_bytes=VMEM_LIMIT),
    )(shard)


def _pair_exchange_sum(partial, name):
    _, R2, C = partial.shape
    R = R2 // 2

    def body(p_ref, send_ref, own_ref, mine_ref, sib_ref, loc_sems, send_sems, recv_sems):
        x, y, c, chips = _mesh_pos()
        me = 2 * x + y
        mine, theirs = _half_rows(c, R)
        order = [2 * cx + cy for cx, cy in chips] + [me]
        locs, pairs = [], []
        for i, k in enumerate(order):
            loc = pltpu.make_async_copy(p_ref.at[k, mine, :], mine_ref.at[i], loc_sems.at[i])
            pair = _remote(p_ref.at[k, theirs, :], sib_ref.at[i], (send_sems.at[i], recv_sems.at[i]), (x, y, 1 - c))
            loc.start()
            pair.start()
            locs.append(loc)
            pairs.append(pair)
        for i in range(N_CHIPS):
            locs[i].wait()
            pairs[i].wait_recv()
            total = mine_ref[i] + sib_ref[i]
            if i < 3:
                send_ref[i] = total.astype(BF16)
            else:
                own_ref[...] = total
        for pair in pairs:
            pair.wait_send()

    vm = pl.BlockSpec(memory_space=pltpu.VMEM)
    return pl.pallas_call(
        body, name=name, in_specs=[pl.BlockSpec(memory_space=pl.ANY)], out_specs=[vm, vm],
        out_shape=[jax.ShapeDtypeStruct((3, R, C), BF16), jax.ShapeDtypeStruct((R, C), F32)],
        scratch_shapes=[pltpu.VMEM((N_CHIPS, R, C), F32), pltpu.VMEM((N_CHIPS, R, C), F32),
                        pltpu.SemaphoreType.DMA((N_CHIPS,)), pltpu.SemaphoreType.DMA((N_CHIPS,)),
                        pltpu.SemaphoreType.DMA((N_CHIPS,))],
        compiler_params=pltpu.CompilerParams(vmem_limit_bytes=VMEM_LIMIT),
    )(partial)


def _allreduce_small(buf, name):
    rows, width = buf.shape
    h = rows // 2

    def body(b_ref, o_ref, sib_ref, pair_ref, in_ref, pair_sems, send_sems, recv_sems, fin_sems):
        x, y, c, chips = _mesh_pos()
        me = 2 * x + y
        mine, theirs = _half_rows(c, h)
        sibling = (x, y, 1 - c)
        pair = _remote(b_ref.at[theirs], sib_ref, (pair_sems.at[0], pair_sems.at[1]), sibling)
        pair.start()
        pair.wait()
        pair_ref[...] = b_ref[mine, :] + sib_ref[...]
        sends = []
        for j, (cx, cy) in enumerate(chips):
            cp = _remote(pair_ref, in_ref.at[j], (send_sems.at[j], recv_sems.at[j]), (cx, cy, c))
            cp.start()
            sends.append(cp)
        for cp in sends:
            cp.wait_recv()
        acc = None
        for k in range(N_CHIPS):
            term = jnp.where(me == k, pair_ref[...], in_ref[_peer_slot(k, x, y)])
            acc = term if acc is None else acc + term
        o_ref[mine, :] = acc
        fin = _remote(o_ref.at[mine], o_ref.at[mine], (fin_sems.at[0], fin_sems.at[1]), sibling)
        fin.start()
        fin.wait_send()
        _remote(o_ref.at[theirs], o_ref.at[theirs], (fin_sems.at[0], fin_sems.at[1]), sibling).wait_recv()
        for cp in sends:
            cp.wait_send()

    return pl.pallas_call(
        body, name=name, out_shape=jax.ShapeDtypeStruct((rows, width), F32),
        in_specs=[pl.BlockSpec(memory_space=pltpu.VMEM)], out_specs=pl.BlockSpec(memory_space=pltpu.VMEM),
        scratch_shapes=[pltpu.VMEM((h, width), F32), pltpu.VMEM((h, width), F32), pltpu.VMEM((3, h, width), F32),
                        pltpu.SemaphoreType.DMA((2,)), pltpu.SemaphoreType.DMA((3,)), pltpu.SemaphoreType.DMA((3,)),
                        pltpu.SemaphoreType.DMA((2,))],
        compiler_params=pltpu.CompilerParams(vmem_limit_bytes=VMEM_LIMIT),
    )(buf)


def _adamw_small(ws, gs, ms, vs):
    n = len(ws)
    c1 = 1.0 / (1.0 - ADAM_B1 ** ADAM_STEP)
    c2 = 1.0 / (1.0 - ADAM_B2 ** ADAM_STEP)

    def body(*refs):
        w_r, g_r, m_r, v_r = refs[0:n], refs[n:2 * n], refs[2 * n:3 * n], refs[3 * n:4 * n]
        d_r, nm_r, nv_r = refs[4 * n:5 * n], refs[5 * n:6 * n], refs[6 * n:7 * n]
        for i in range(n):
            gv = g_r[i][...]
            nm = ADAM_B1 * m_r[i][...] + (1.0 - ADAM_B1) * gv
            nv = ADAM_B2 * v_r[i][...] + (1.0 - ADAM_B2) * (gv * gv)
            nm_r[i][...] = nm
            nv_r[i][...] = nv
            d_r[i][...] = (-ADAM_LR) * ((nm * c1) / (jnp.sqrt(nv * c2) + ADAM_EPS) + ADAM_WD * w_r[i][...])

    vm = pl.BlockSpec(memory_space=pltpu.VMEM)
    sds = [jax.ShapeDtypeStruct(w.shape, F32) for w in ws]
    out = pl.pallas_call(body, name="adamw_small", in_specs=[vm] * (4 * n), out_specs=[vm] * (3 * n),
                         out_shape=sds * 3)(*ws, *gs, *ms, *vs)
    return out[0:n], out[n:2 * n], out[2 * n:3 * n]


_BIG = ("w_in", "w_rnn_proj", "w_attn_proj", "w_out", "w_up", "w_down", "w_ple_gate", "w_ple_proj")
_SMALL = ("g_mix", "conv_w", "conv_b", "w_rg", "b_rg", "w_ig", "b_ig", "lru_lambda", "q_gain", "k_gain", "sinks",
          "g_mlp", "g_ple")
_WEIGHTS = ("g_mix", "w_in", "conv_w", "conv_b", "w_rg", "b_rg", "w_ig", "b_ig", "lru_lambda", "w_rnn_proj",
            "q_gain", "k_gain", "sinks", "w_attn_proj", "w_out", "g_mlp", "w_up", "w_down", "g_ple", "w_ple_gate",
            "w_ple_proj")


def _pad_row(v):
    v = v.reshape(1, -1)
    return jnp.pad(v, ((0, 0), (0, D_MODEL - v.shape[1])))


def kernel(x, p, g_mix, w_in, conv_w, conv_b, w_rg, b_rg, w_ig, b_ig, lru_lambda, w_rnn_proj, q_gain, k_gain, sinks, w_attn_proj, w_out, g_mlp, w_up, w_down, g_ple, w_ple_gate, w_ple_proj, loss_target, m_g_mix, m_w_in, m_conv_w, m_conv_b, m_w_rg, m_b_rg, m_w_ig, m_b_ig, m_lru_lambda, m_w_rnn_proj, m_q_gain, m_k_gain, m_sinks, m_w_attn_proj, m_w_out, m_g_mlp, m_w_up, m_w_down, m_g_ple, m_w_ple_gate, m_w_ple_proj, v_g_mix, v_w_in, v_conv_w, v_conv_b, v_w_rg, v_b_rg, v_w_ig, v_b_ig, v_lru_lambda, v_w_rnn_proj, v_q_gain, v_k_gain, v_sinks, v_w_attn_proj, v_w_out, v_g_mlp, v_w_up, v_w_down, v_g_ple, v_w_ple_gate, v_w_ple_proj):
    w = dict(g_mix=g_mix, w_in=w_in, conv_w=conv_w, conv_b=conv_b, w_rg=w_rg, b_rg=b_rg, w_ig=w_ig, b_ig=b_ig,
             lru_lambda=lru_lambda, w_rnn_proj=w_rnn_proj, q_gain=q_gain, k_gain=k_gain, sinks=sinks,
             w_attn_proj=w_attn_proj, w_out=w_out, g_mlp=g_mlp, w_up=w_up, w_down=w_down, g_ple=g_ple,
             w_ple_gate=w_ple_gate, w_ple_proj=w_ple_proj)
    m = dict(g_mix=m_g_mix, w_in=m_w_in, conv_w=m_conv_w, conv_b=m_conv_b, w_rg=m_w_rg, b_rg=m_b_rg, w_ig=m_w_ig,
             b_ig=m_b_ig, lru_lambda=m_lru_lambda, w_rnn_proj=m_w_rnn_proj, q_gain=m_q_gain, k_gain=m_k_gain,
             sinks=m_sinks, w_attn_proj=m_w_attn_proj, w_out=m_w_out, g_mlp=m_g_mlp, w_up=m_w_up, w_down=m_w_down,
             g_ple=m_g_ple, w_ple_gate=m_w_ple_gate, w_ple_proj=m_w_ple_proj)
    v = dict(g_mix=v_g_mix, w_in=v_w_in, conv_w=v_conv_w, conv_b=v_conv_b, w_rg=v_w_rg, b_rg=v_b_rg, w_ig=v_w_ig,
             b_ig=v_b_ig, lru_lambda=v_lru_lambda, w_rnn_proj=v_w_rnn_proj, q_gain=v_q_gain, k_gain=v_k_gain,
             sinks=v_sinks, w_attn_proj=v_w_attn_proj, w_out=v_w_out, g_mlp=v_g_mlp, w_up=v_w_up, w_down=v_w_down,
             g_ple=v_g_ple, w_ple_gate=v_w_ple_gate, w_ple_proj=v_w_ple_proj)
    n_seq, S, _ = x.shape
    T = n_seq * S
    chip = 2 * lax.axis_index("x") + lax.axis_index("y")

    tm, tm_rnn = TM, TM_RNN
    xf, pf, tf = x.reshape(T, D_MODEL), p.reshape(T, PLE_DIM), loss_target.reshape(T, D_MODEL)
    first = lambda outs: [o[0] for o in outs]

    w_in_g = _gather_bf16(w["w_in"][0], "gather_w_in")
    wb = {name: w[name][0].astype(BF16) for name in _BIG if name != "w_in"}
    grp_mix, grp_mlp, grp_ple = ("w_rnn_proj", "w_attn_proj", "w_out"), ("w_up", "w_down"), ("w_ple_gate", "w_ple_proj")

    wb["conv_w"] = jnp.pad(conv_w[0], ((0, 16 - CONV_W), (0, 0)))

    cosf, sins = _rope_tables(S)
    ind_q, ind_qt = _indicator(D_MODEL)
    ind_k, ind_kt = _indicator(KV_W)
    wrg2 = _pair_blockdiag(w_rg[0]).astype(BF16)
    wig2 = _pair_blockdiag(w_ig[0]).astype(BF16)
    qg = jnp.tile(q_gain, (1, N_HEADS))
    kg = jnp.tile(k_gain, (1, N_KV))
    sk = sinks.reshape(N_HEADS)
    attn_c = (qg, kg, sk, cosf, sins, ind_q, ind_qt, ind_k, ind_kt, n_seq, S)

    (h0, xr, gr, zq, zk, zv, ga, gb), ph = _inproj_fwd(xf, g_mix, w_in_g, tm,
                                                     phases=[_ph_gather_send(wb[n]) for n in grp_mix + ("conv_w",)])
    g_small = first(ph)
    o, ph = _attn_fwd(zq, zk, zv, *attn_c,
                      phases=[_ph_gather_pass(g) for g in g_small]
                      + [_ph_gather_send(wb[n]) for n in ("w_up",) + grp_ple])
    g_small, (wu, wpg, wpp) = first(ph[:4]), first(ph[4:])
    cw_full = g_small[3][:, :CONV_W, :].transpose(1, 0, 2).reshape(CONV_W, D_MODEL)
    rnn_w = (cw_full, conv_b, wrg2, b_rg, wig2, b_ig, lru_lambda)
    (xc, h, *gates, ya), ph = _rnn_fwd(xr, gr, *rnn_w, n_seq, S, tm_rnn,
                               phases=[_ph_gather_pass(g) for g in (wu, wpg, wpp)]
                               + [_ph_gather_send(wb["w_down"])])
    (wu, wpg, wpp), wd = first(ph[:3]), ph[3][0]
    wr, wa, wo = (g.reshape(D_MODEL, D_MODEL) for g in g_small[:3])
    wpg = wpg.reshape(D_MODEL, D_MODEL)
    (x1, merged), ph = _merge_fwd(xf, ya, o, ga, gb, wr, wa, wo, tm, phases=[_ph_gather_pass(wd)])
    wd = ph[0][0].reshape(D_FF, D_MODEL)
    (x2, hm, u, act), _ = _mlp_fwd(x1, g_mlp, wu, wd, tm // 2)
    (loss_t, dx2, pb, de, hp, dtg, dg_ple), _ = _ple_loss(x2, pf, tf, g_ple, wpg, wpp, tm)

    chipmajor = lambda g: g.reshape(N_CHIPS, g.shape[-2] // N_CHIPS, g.shape[-1]) if g.ndim == 2 else g
    tmw = min(2 * tm, T)
    dw_pp = _wgrad(pb, de, "wgrad_ple_proj", False, D_MODEL, tmw)[0]
    part_ple = [chipmajor(_wgrad(hp, dtg, "wgrad_ple_gate", False, D_MODEL, tmw)[0]),
                dw_pp.reshape(PLE_DIM, N_CHIPS, D_MODEL // N_CHIPS).transpose(1, 0, 2)]
    (dx1, du, dg_mlp), ph = _mlp_bwd(dx2, u, x1, g_mlp, wu, wd, tm // 2, phases=[_ph_pair_send(g) for g in part_ple])
    send_ple, own_ple = _pair_sum(part_ple, first(ph), "pair_sum_ple")
    dw_down, ph = _wgrad(act, dx2, "wgrad_down", False, D_MODEL // 2, tmw, phases=[_ph_chip_send(s) for s in send_ple])
    red_ple = _chip_sum(own_ple, first(ph), "chip_sum_ple")
    part_mlp = [_wgrad(hm, du, "wgrad_up", True, D_MODEL, tmw)[0], chipmajor(dw_down)]
    (dga, dgb, dya, dyb, dyain, do), _ = _merge_bwd(dx1, ga, gb, ya, o, wr, wa, wo, tm)
    dw_rnn, ph_up = _wgrad(ya, dya, "wgrad_rnn_proj", False, D_MODEL, tmw, phases=[_ph_pair_send(part_mlp[0])])
    dw_attn, ph_down = _wgrad(o, dyb, "wgrad_attn_proj", False, D_MODEL, tmw, phases=[_ph_pair_send(part_mlp[1])])
    dw_out, ph = _wgrad(merged, dx1, "wgrad_out", False, D_MODEL, tmw, phases=[_ph_half_swap(r) for r in red_ple])
    red_ple = first(ph)
    send_mlp, own_mlp = _pair_sum(part_mlp, [ph_up[0][0], ph_down[0][0]], "pair_sum_mlp")
    part_mix = [chipmajor(dw_rnn), chipmajor(dw_attn), chipmajor(dw_out)]
    (dxr, dgr, vec, dwrg2, dwig2), ph = _rnn_bwd(
        dyain, xr, gr, xc, h, gates, cw_full, wrg2, wig2, lru_lambda, n_seq, S, tm_rnn,
        phases=[_ph_chip_send(s) for s in send_mlp] + [_ph_pair_send(g) for g in part_mix])
    red_mlp = _chip_sum(own_mlp, first(ph[:2]), "chip_sum_mlp")
    send_mix, own_mix = _pair_sum(part_mix, first(ph[2:]), "pair_sum_mix")
    (dq, dkc, dkp, dvc, dvp, dqg, dsk), ph = _attn_bwd(
        do, zq, zk, zv, *attn_c, phases=[_ph_half_swap(r) for r in red_mlp] + [_ph_chip_send(s) for s in send_mix])
    red_mlp = first(ph[:2])
    red_mix = _chip_sum(own_mix, first(ph[2:]), "chip_sum_mix")
    (dk, dv, dkg), _ = _kv_bwd(dkc, dkp, dvc, dvp, zk, kg, cosf, sins, ind_k, ind_kt, n_seq, S)
    dz_parts = [dxr, dgr, dq, dk, dv, dga, dgb]
    send_in, own_in = _pair_exchange_sum(_wgrad_in(h0, dz_parts, tm), "pair_sum_in")
    (grad_x, dg_mix), ph = _inproj_bwd(dz_parts, w_in_g, xf, g_mix, dx1, tm,
                                       phases=[_ph_half_swap(r) for r in red_mix] + [_ph_chip_send(send_in)])
    red_mix = first(ph[:3])
    red_in = _chip_sum([own_in], first(ph[3:]), "chip_sum_in")
    reduced = dict(zip(grp_ple + grp_mlp + grp_mix, red_ple + red_mlp + red_mix))
    grads = {
        "g_mix": dg_mix[0], "g_mlp": dg_mlp[0], "g_ple": dg_ple[0],
        "conv_w": vec[0:CONV_W], "conv_b": vec[4], "b_rg": vec[5], "b_ig": vec[6], "lru_lambda": vec[7],
        "w_rg": _pair_blockdiag_extract(dwrg2), "w_ig": _pair_blockdiag_extract(dwig2),
        "q_gain": dqg.reshape(N_HEADS, HEAD_DIM).sum(0), "k_gain": dkg.reshape(N_KV, HEAD_DIM).sum(0),
        "sinks": dsk.sum(1),
    }

    rows = [grads["conv_w"], _pad_row(grads["conv_b"]), _pad_row(grads["b_rg"]), _pad_row(grads["b_ig"]),
            _pad_row(grads["lru_lambda"]), _pad_row(grads["g_mix"]), _pad_row(grads["g_mlp"]),
            _pad_row(grads["g_ple"]), _pad_row(grads["q_gain"]), _pad_row(grads["k_gain"]), _pad_row(grads["sinks"]),
            _pad_row(loss_t[0:1, 0:1]), jnp.zeros((1, D_MODEL), F32)]
    vecs = jnp.concatenate(rows, axis=0)
    packed = jnp.concatenate([vecs.reshape(-1, LANES), grads["w_rg"].reshape(-1, LANES),
                              grads["w_ig"].reshape(-1, LANES)], axis=0)
    red = _allreduce_small(packed, "allreduce_small")
    nv = vecs.size // LANES
    rvec = red[0:nv].reshape(16, D_MODEL)
    loss = rvec[14, 0]
    nw = grads["w_rg"].size // LANES
    sg = {
        "conv_w": lax.dynamic_slice(rvec[0:CONV_W], (0, chip * (D_MODEL // N_CHIPS)), (CONV_W, D_MODEL // N_CHIPS)),
        "conv_b": rvec[4], "b_rg": rvec[5], "b_ig": rvec[6], "lru_lambda": rvec[7], "g_mix": rvec[8],
        "g_mlp": rvec[9], "g_ple": rvec[10], "q_gain": rvec[11, :HEAD_DIM], "k_gain": rvec[12, :HEAD_DIM],
        "sinks": rvec[13, :N_HEADS], "w_rg": red[nv:nv + nw], "w_ig": red[nv + nw:nv + 2 * nw],
    }
    sg = {k: sg[k].reshape(w[k].shape) for k in _SMALL}
    d_s, m_s, v_s = _adamw_small([w[k] for k in _SMALL], [sg[k] for k in _SMALL], [m[k] for k in _SMALL],
                                 [v[k] for k in _SMALL])
    grad, delta, new_m, new_v = dict(sg), dict(zip(_SMALL, d_s)), dict(zip(_SMALL, m_s)), dict(zip(_SMALL, v_s))

    for name in ("w_ple_proj", "w_up", "w_down", "w_rnn_proj", "w_attn_proj", "w_out", "w_ple_gate", "w_in"):
        shape = w[name].shape
        outs, ph = _adamw(w[name][0], reduced[name], m[name][0], v[name][0], "adamw_" + name, min(ADAMW_ROWS, shape[1] // 2),
                          phases=[_ph_half_swap(r) for r in red_in] if name == "w_ple_proj" else ())
        if name == "w_ple_proj":
            reduced["w_in"] = ph[0][0]
        grad[name], delta[name], new_m[name], new_v[name] = (a.reshape(shape) for a in outs)

    return (loss, grad_x.reshape(x.shape), *[grad[k] for k in _WEIGHTS], *[delta[k] for k in _WEIGHTS],
            *[new_m[k] for k in _WEIGHTS], *[new_v[k] for k in _WEIGHTS])
```

```python
import functools
import math

import numpy as np
import jax
import jax.numpy as jnp
from jax import lax
from jax.experimental import pallas as pl
from jax.experimental.pallas import tpu as pltpu

F32 = jnp.float32
BF16 = jnp.bfloat16

D_MODEL = 1024
N_HEADS = 16
N_KV = 4
HEAD_DIM = 64
KV_W = N_KV * HEAD_DIM
D_FF = 4096
PLE_DIM = 256
WINDOW = 128
CONV_W = 4
LRU_C = 8.0
NORM_EPS = 1e-6
ROPE_THETA = 10000.0
N_CHIPS = 4
IN_TOTAL = 5632
IN_BLK = IN_TOTAL // N_CHIPS
IN_SEGS = (0, 1024, 2048, 3072, 3328, 3584, 4608, 5632)

ADAM_LR = 0.001
ADAM_B1 = 0.9
ADAM_B2 = 0.999
ADAM_EPS = 1e-08
ADAM_WD = 0.01
ADAM_STEP = 10

LANES = 128
V7X_VMEM_BYTES = 64 * 1024 * 1024
VMEM_LIMIT = V7X_VMEM_BYTES - 8 * 1024 * 1024
MESH_ID = pl.DeviceIdType.MESH
TM, TM_RNN, ADAMW_ROWS = 512, 256, 256
ATTN_BLOCKS_PER_STEP = 2
KV_BLOCKS_PER_STEP = 8


def _dot(a, b):
    return jnp.dot(a, b, preferred_element_type=F32)


def _dot_nt(a, b):
    return lax.dot_general(a, b, (((1,), (1,)), ((), ())), preferred_element_type=F32)


def _dot_tn(a, b):
    return lax.dot_general(a, b, (((0,), (0,)), ((), ())), preferred_element_type=F32)


def _split_dot(x, ind):
    hi = x.astype(BF16)
    lo = (x - hi.astype(F32)).astype(BF16)
    return _dot(hi, ind) + _dot(lo, ind)


def _sigmoid(x):
    return 1.0 / (1.0 + jnp.exp(-x))


_GELU_C = math.sqrt(2.0 / math.pi)


def _gelu_and_grad(g):
    inner = _GELU_C * (g + 0.044715 * g * g * g)
    t = jnp.tanh(inner)
    gelu = 0.5 * g * (1.0 + t)
    dgelu = 0.5 * (1.0 + t) + 0.5 * g * (1.0 - t * t) * _GELU_C * (1.0 + 3.0 * 0.044715 * g * g)
    return gelu, dgelu


def _const(shape):
    nd = len(shape)
    return pl.BlockSpec(shape, lambda *_: (0,) * nd)


def _params(n_grid, vmem=VMEM_LIMIT):
    return pltpu.CompilerParams(dimension_semantics=("arbitrary",) * n_grid, vmem_limit_bytes=vmem)


def _rms_fwd(x, g):
    r = lax.rsqrt(jnp.mean(x * x, axis=-1, keepdims=True) + NORM_EPS)
    return (x * r) * g, r


def _rms_bwd(dy, x, r, g):
    dn = dy * g
    dx = r * dn - x * (r * r * r * jnp.mean(dn * x, axis=-1, keepdims=True))
    dg = jnp.sum(dy * (x * r), axis=0, keepdims=True)
    return dx, dg


def _seg_pieces(blk_lo, blk_hi):
    out = []
    for s in range(7):
        lo, hi = max(blk_lo, IN_SEGS[s]), min(blk_hi, IN_SEGS[s + 1])
        if lo < hi:
            out.append((s, lo - IN_SEGS[s], hi - IN_SEGS[s], lo - blk_lo))
    return out


def _mesh_pos():
    x, y, c = lax.axis_index("x"), lax.axis_index("y"), lax.axis_index("c")
    other_chips = [(1 - x, y), (x, 1 - y), (1 - x, 1 - y)]
    return x, y, c, other_chips


def _peer_slot(k, x, y):
    dx = jnp.bitwise_xor(k // 2, x)
    dy = jnp.bitwise_xor(k % 2, y)
    return jnp.maximum(dx + 2 * dy - 1, 0)


def _half_rows(c, R):
    return pl.ds(pl.multiple_of(c * R, R), R), pl.ds(pl.multiple_of((1 - c) * R, R), R)


def _remote(src, dst, sems, to):
    return pltpu.make_async_remote_copy(src_ref=src, dst_ref=dst, send_sem=sems[0], recv_sem=sems[1],
                                        device_id=to, device_id_type=MESH_ID)


class _Phase:
    def __init__(self, ins, inout, outs, n_remote, n_local, build):
        self.ins, self.inout, self.outs = list(ins), list(inout), list(outs)
        self.n_remote, self.n_local, self.build = n_remote, n_local, build


def _ph_gather_send(wb):
    R2, C = wb.shape
    R = R2 // 2

    def build(ins, outs, rsem, lsem):
        (w_ref,), (g_ref,) = ins, outs
        x, y, c, chips = _mesh_pos()
        me = 2 * x + y
        mine, _ = _half_rows(c, R)
        loc = [pltpu.make_async_copy(w_ref, g_ref.at[me], lsem(0))]
        outg = [_remote(w_ref.at[mine], g_ref.at[me, mine], rsem(j), (cx, cy, c)) for j, (cx, cy) in enumerate(chips)]
        inc = [functools.partial(_remote, w_ref.at[mine], g_ref.at[2 * cx + cy, mine], rsem(j), (x, y, c))
               for j, (cx, cy) in enumerate(chips)]
        return loc, outg, inc

    return _Phase([wb], [], [jax.ShapeDtypeStruct((N_CHIPS, R2, C), wb.dtype)], 3, 1, build)


def _ph_gather_pass(gath):
    _, R2, C = gath.shape
    R = R2 // 2

    def build(ins, outs, rsem, lsem):
        (g_ref,) = outs
        x, y, c, chips = _mesh_pos()
        mine, theirs = _half_rows(c, R)
        outg, inc = [], []
        for j, (cx, cy) in enumerate(chips):
            blk = g_ref.at[2 * cx + cy, mine]
            outg.append(_remote(blk, blk, rsem(j), (x, y, 1 - c)))
            got = g_ref.at[2 * cx + cy, theirs]
            inc.append(functools.partial(_remote, got, got, rsem(j), (x, y, c)))
        return [], outg, inc

    return _Phase([], [gath], [], 3, 0, build)


def _ph_pair_send(partial):
    _, R2, C = partial.shape
    R = R2 // 2

    def build(ins, outs, rsem, lsem):
        (p_ref,), (s_ref,) = ins, outs
        x, y, c, _ = _mesh_pos()
        _, theirs = _half_rows(c, R)
        src = p_ref.at[:, theirs, :]
        return ([], [_remote(src, s_ref, rsem(0), (x, y, 1 - c))],
                [functools.partial(_remote, src, s_ref, rsem(0), (x, y, c))])

    return _Phase([partial], [], [jax.ShapeDtypeStruct((N_CHIPS, R, C), F32)], 1, 0, build)


def _ph_chip_send(sendb):
    def build(ins, outs, rsem, lsem):
        (s_ref,), (r_ref,) = ins, outs
        x, y, c, chips = _mesh_pos()
        outg = [_remote(s_ref.at[j], r_ref.at[j], rsem(j), (cx, cy, c)) for j, (cx, cy) in enumerate(chips)]
        inc = [functools.partial(_remote, s_ref.at[j], r_ref.at[j], rsem(j), (x, y, c)) for j in range(3)]
        return [], outg, inc

    return _Phase([sendb], [], [jax.ShapeDtypeStruct(sendb.shape, sendb.dtype)], 3, 0, build)


def _ph_half_swap(red):
    R2, C = red.shape
    R = R2 // 2

    def build(ins, outs, rsem, lsem):
        (r_ref,) = outs
        x, y, c, _ = _mesh_pos()
        mine, theirs = _half_rows(c, R)
        return ([], [_remote(r_ref.at[mine], r_ref.at[mine], rsem(0), (x, y, 1 - c))],
                [functools.partial(_remote, r_ref.at[theirs], r_ref.at[theirs], rsem(0), (x, y, c))])

    return _Phase([], [red], [], 1, 0, build)


def _call(body, *, name, grid, in_specs, out_specs, out_shape, scratch_shapes=(), phases=()):
    single = not isinstance(out_specs, (list, tuple))
    out_specs = [out_specs] if single else list(out_specs)
    out_shape = [out_shape] if single else list(out_shape)
    n_in, n_out, n_scr = len(in_specs), len(out_specs), len(scratch_shapes)
    if not phases:
        call = pl.pallas_call(body, name=name, grid=grid, in_specs=in_specs, out_specs=out_specs,
                              out_shape=out_shape, scratch_shapes=list(scratch_shapes),
                              compiler_params=_params(len(grid)))
        return lambda *operands: (list(call(*operands)), [])

    ex_in, ex_out, aliases, spans = [], [], {}, []
    for ph in phases:
        i0, o0 = len(ex_in), len(ex_out)
        ex_in += ph.ins
        for a in ph.inout:
            aliases[n_in + len(ex_in)] = n_out + len(ex_out)
            ex_in.append(a)
            ex_out.append(jax.ShapeDtypeStruct(a.shape, a.dtype))
        ex_out += ph.outs
        spans.append((i0, len(ph.ins), o0, len(ex_out) - o0))
    n_remote = sum(ph.n_remote for ph in phases)
    n_local = max(sum(ph.n_local for ph in phases), 1)

    def wrapped(*refs):
        base_in, xin = refs[:n_in], refs[n_in:n_in + len(ex_in)]
        o0 = n_in + len(ex_in)
        base_out, xout = refs[o0:o0 + n_out], refs[o0 + n_out:o0 + n_out + len(ex_out)]
        scr = refs[o0 + n_out + len(ex_out):]
        send_sems, recv_sems, loc_sems = scr[n_scr:]
        first = functools.reduce(jnp.logical_and, [pl.program_id(i) == 0 for i in range(len(grid))])
        last = functools.reduce(jnp.logical_and, [pl.program_id(i) == grid[i] - 1 for i in range(len(grid))])

        def copies():
            out, r0, l0 = [], 0, 0
            for ph, (i0, ni, p0, no) in zip(phases, spans):
                rsem = lambda k, r0=r0: (send_sems.at[r0 + k], recv_sems.at[r0 + k])
                lsem = lambda k, l0=l0: loc_sems.at[l0 + k]
                out.append(ph.build(xin[i0:i0 + ni], xout[p0:p0 + no], rsem, lsem))
                r0, l0 = r0 + ph.n_remote, l0 + ph.n_local
            return out

        @pl.when(first)
        def _():
            for loc, outg, _ in copies():
                for cp in loc + outg:
                    cp.start()

        body(*base_in, *base_out, *scr[:n_scr])

        @pl.when(last)
        def _():
            for loc, outg, inc in copies():
                for make in inc:
                    make().wait_recv()
                for cp in outg:
                    cp.wait_send()
                for cp in loc:
                    cp.wait()

    hbm = pl.BlockSpec(memory_space=pl.ANY)
    call = pl.pallas_call(
        wrapped, name=name, grid=grid, in_specs=list(in_specs) + [hbm] * len(ex_in),
        out_specs=out_specs + [hbm] * len(ex_out), out_shape=out_shape + ex_out,
        scratch_shapes=list(scratch_shapes) + [pltpu.SemaphoreType.DMA((n_remote,)), pltpu.SemaphoreType.DMA((n_remote,)),
                                              pltpu.SemaphoreType.DMA((n_local,))],
        input_output_aliases=aliases, compiler_params=_params(len(grid)))

    def run(*operands):
        res = call(*operands, *ex_in)
        extra = res[n_out:]
        return list(res[:n_out]), [list(extra[p0:p0 + no]) for (_, _, p0, no) in spans]

    return run


def _inproj_fwd(x, g_mix, w_in, tm, phases=()):
    T = x.shape[0]
    widths = [IN_SEGS[i + 1] - IN_SEGS[i] for i in range(7)]

    def body(x_ref, g_ref, w_ref, h_ref, *z_refs):
        h, _ = _rms_fwd(x_ref[...], g_ref[...])
        hb = h.astype(BF16)
        h_ref[...] = hb
        for j in range(N_CHIPS):
            zj = _dot(hb, w_ref[j])
            for s, lo, hi, off in _seg_pieces(j * IN_BLK, (j + 1) * IN_BLK):
                z_refs[s][:, lo:hi] = zj[:, off:off + hi - lo]

    return _call(
        body, phases=phases, name="inproj_fwd", grid=(T // tm,),
        in_specs=[pl.BlockSpec((tm, D_MODEL), lambda i: (i, 0)), _const((1, D_MODEL)),
                  _const((N_CHIPS, D_MODEL, IN_BLK))],
        out_specs=[pl.BlockSpec((tm, D_MODEL), lambda i: (i, 0))]
        + [pl.BlockSpec((tm, w), lambda i: (i, 0)) for w in widths],
        out_shape=[jax.ShapeDtypeStruct((T, D_MODEL), BF16)]
        + [jax.ShapeDtypeStruct((T, w), F32) for w in widths],
    )(x, g_mix, w_in)


def _inproj_bwd(dz_parts, w_in, x, g_mix, dx1, tm, phases=()):
    T = x.shape[0]
    widths = [IN_SEGS[i + 1] - IN_SEGS[i] for i in range(7)]

    def body(*refs):
        p_refs = refs[:7]
        w_ref, x_ref, g_ref, dx1_ref, gx_ref, dg_ref, dz_ref = refs[7:]

        @pl.when(pl.program_id(0) == 0)
        def _():
            dg_ref[...] = jnp.zeros_like(dg_ref)

        for s in range(7):
            dz_ref[:, IN_SEGS[s]:IN_SEGS[s + 1]] = p_refs[s][...]
        dh = jnp.zeros((tm, D_MODEL), F32)
        for j in range(N_CHIPS):
            dh = dh + _dot_nt(dz_ref[:, j * IN_BLK:(j + 1) * IN_BLK], w_ref[j])
        xv = x_ref[...]
        g = g_ref[...]
        _, r = _rms_fwd(xv, g)
        dx, dg = _rms_bwd(dh, xv, r, g)
        gx_ref[...] = dx1_ref[...] + dx
        dg_ref[...] += dg

    row = lambda w: pl.BlockSpec((tm, w), lambda i: (i, 0))
    return _call(
        body, phases=phases, name="inproj_bwd", grid=(T // tm,),
        in_specs=[row(w) for w in widths]
        + [_const((N_CHIPS, D_MODEL, IN_BLK)), row(D_MODEL), _const((1, D_MODEL)), row(D_MODEL)],
        out_specs=[row(D_MODEL), _const((1, D_MODEL))],
        out_shape=[jax.ShapeDtypeStruct((T, D_MODEL), F32), jax.ShapeDtypeStruct((1, D_MODEL), F32)],
        scratch_shapes=[pltpu.VMEM((tm, IN_TOTAL), BF16)],
    )(*dz_parts, w_in, x, g_mix, dx1)


def _wgrad_in(h0, dz_parts, tm):
    T = h0.shape[0]
    widths = [IN_SEGS[i + 1] - IN_SEGS[i] for i in range(7)]

    def body(*refs):
        h_ref, p_refs, o_ref, ob_ref, acc_ref, stage_ref, sems = (refs[0], refs[1:8], refs[8], refs[9], refs[10],
                                                                  refs[11], refs[12])
        t = pl.program_id(0)
        last = T // tm - 1

        @pl.when(t == 0)
        def _():
            acc_ref[...] = jnp.zeros_like(acc_ref)

        def accumulate(j):
            for s, lo, hi, off in _seg_pieces(j * IN_BLK, (j + 1) * IN_BLK):
                acc_ref[j, :, off:off + hi - lo] += _dot_tn(h_ref[...], p_refs[s][:, lo:hi])

        @pl.when(t < last)
        def _():
            for j in range(N_CHIPS):
                accumulate(j)

        @pl.when(t == last)
        def _():
            copies = [pltpu.make_async_copy(acc_ref.at[j], o_ref.at[j], sems.at[j]) for j in range(N_CHIPS)]
            narrow = [pltpu.make_async_copy(stage_ref.at[j % 2], ob_ref.at[j], sems.at[N_CHIPS + j])
                      for j in range(N_CHIPS)]
            for j in range(N_CHIPS):
                accumulate(j)
                copies[j].start()
                if j >= 2:
                    narrow[j - 2].wait()
                stage_ref[j % 2] = acc_ref[j].astype(BF16)
                narrow[j].start()
            for cp in copies + narrow[N_CHIPS - 2:]:
                cp.wait()

    row = lambda w: pl.BlockSpec((tm, w), lambda i: (i, 0))
    hbm = pl.BlockSpec(memory_space=pl.ANY)
    return pl.pallas_call(
        body, name="wgrad_in", grid=(T // tm,), in_specs=[row(D_MODEL)] + [row(w) for w in widths],
        out_specs=[hbm, hbm],
        out_shape=[jax.ShapeDtypeStruct((N_CHIPS, D_MODEL, IN_BLK), F32),
                   jax.ShapeDtypeStruct((N_CHIPS, D_MODEL, IN_BLK), BF16)],
        scratch_shapes=[pltpu.VMEM((N_CHIPS, D_MODEL, IN_BLK), F32), pltpu.VMEM((2, D_MODEL, IN_BLK), BF16),
                        pltpu.SemaphoreType.DMA((2 * N_CHIPS,))],
        compiler_params=_params(1),
    )(h0, *dz_parts)


def _wgrad(a, g, name, blocked, cn, tm, phases=()):
    T, K = a.shape
    N = g.shape[1]
    nb = N // cn

    def body(a_ref, g_ref, o_ref):
        @pl.when(pl.program_id(1) == 0)
        def _():
            o_ref[...] = jnp.zeros_like(o_ref)

        o_ref[...] += _dot_tn(a_ref[...].astype(BF16), g_ref[...].astype(BF16))

    if blocked:
        out_spec = pl.BlockSpec((None, K, cn), lambda j, t: (j, 0, 0))
        out_shape = jax.ShapeDtypeStruct((nb, K, cn), F32)
    else:
        out_spec = pl.BlockSpec((K, cn), lambda j, t: (0, j))
        out_shape = jax.ShapeDtypeStruct((K, N), F32)
    outs, extra = _call(
        body, phases=phases, name=name, grid=(nb, T // tm),
        in_specs=[pl.BlockSpec((tm, K), lambda j, t: (t, 0)), pl.BlockSpec((tm, cn), lambda j, t: (t, j))],
        out_specs=out_spec, out_shape=out_shape,
    )(a, g)
    return outs[0], extra


def _shift_down(x, prev8, sft, row, row8, tm):
    xs = pltpu.roll(x, sft, 0)
    top = jnp.where(row8 < sft, pltpu.roll(prev8, sft, 0), xs[0:8])
    return jnp.concatenate([top, xs[8:]], axis=0)


def _shift_up(x, next8, sft, row8, tm):
    xs = pltpu.roll(x, tm - sft, 0)
    bot = jnp.where(row8 >= 8 - sft, pltpu.roll(next8, 8 - sft, 0), xs[tm - 8:tm])
    return jnp.concatenate([xs[0:tm - 8], bot], axis=0)


def _conv_fwd(x, prev8, cw_ref, cb, row, row8, tm):
    xc = cb + cw_ref[CONV_W - 1:CONV_W, :] * x
    for sft in range(1, CONV_W):
        j = CONV_W - 1 - sft
        xc = xc + cw_ref[j:j + 1, :] * _shift_down(x, prev8, sft, row, row8, tm)
    return xc


def _blockdiag_dot(xb, w_ref, transpose):
    outs = []
    for b in range(D_MODEL // LANES):
        xs = xb[:, b * LANES:(b + 1) * LANES]
        outs.append(_dot_nt(xs, w_ref[b]) if transpose else _dot(xs, w_ref[b]))
    return jnp.concatenate(outs, axis=1)


def _softplus_neg(lam):
    e = jnp.exp(-jnp.abs(lam))
    u = 1.0 + e
    log1p_e = jnp.where(u == 1.0, e, jnp.log(u) * (e / (u - 1.0)))
    sp = jnp.maximum(-lam, 0.0) + log1p_e
    return sp, -_sigmoid(-lam)


def _lru_gates(xc, wrg_ref, brg, wig_ref, big, sp):
    xcb = xc.astype(BF16)
    r = _sigmoid(_blockdiag_dot(xcb, wrg_ref, False) + brg)
    i = _sigmoid(_blockdiag_dot(xcb, wig_ref, False) + big)
    log_a = (-LRU_C) * r * sp
    a = jnp.exp(log_a)
    t = jnp.tanh(log_a)
    one_m_a2 = (-2.0) * t / (1.0 - t)
    mult = jnp.sqrt(one_m_a2)
    return xcb, r, i, a, mult


def _scan_down(a, b, row, tm):
    d = 1
    while d < tm:
        if d < 8:
            keep = row >= d
            a_s = jnp.where(keep, pltpu.roll(a, d, 0), 1.0)
            b_s = jnp.where(keep, pltpu.roll(b, d, 0), 0.0)
            b = a * b_s + b
            a = a * a_s
        else:
            b = jnp.concatenate([b[:d], a[d:] * b[:-d] + b[d:]], axis=0)
            a = jnp.concatenate([a[:d], a[d:] * a[:-d]], axis=0)
        d *= 2
    return a, b


def _scan_up(c, b, row, tm):
    d = 1
    while d < tm:
        if d < 8:
            keep = row < tm - d
            c_s = jnp.where(keep, pltpu.roll(c, tm - d, 0), 1.0)
            b_s = jnp.where(keep, pltpu.roll(b, tm - d, 0), 0.0)
            b = c * b_s + b
            c = c * c_s
        else:
            b = jnp.concatenate([c[:-d] * b[d:] + b[:-d], b[-d:]], axis=0)
            c = jnp.concatenate([c[:-d] * c[d:], c[-d:]], axis=0)
        d *= 2
    return c, b


def _rnn_fwd(xr, gr, conv_w, conv_b, wrg2, b_rg, wig2, b_ig, lam, n_seq, S, tm, phases=()):
    T = xr.shape[0]
    nt = S // tm
    W = D_MODEL

    def body(xr_ref, gr_ref, cw_ref, cb_ref, wrg_ref, brg_ref, wig_ref, big_ref, lam_ref,
             xc_ref, h_ref, r_ref, i_ref, a_ref, mult_ref, ya_ref, px_ref, ph_ref):
        @pl.when(pl.program_id(1) == 0)
        def _():
            px_ref[...] = jnp.zeros_like(px_ref)
            ph_ref[...] = jnp.zeros_like(ph_ref)

        row = lax.broadcasted_iota(jnp.int32, (tm, W), 0)
        row8 = lax.broadcasted_iota(jnp.int32, (8, W), 0)
        x = xr_ref[...]
        xc = _conv_fwd(x, px_ref[...], cw_ref, cb_ref[...], row, row8, tm)
        sp, _ = _softplus_neg(lam_ref[...])
        _, r, i, a, mult = _lru_gates(xc, wrg_ref, brg_ref[...], wig_ref, big_ref[...], sp)
        r_ref[...], i_ref[...], a_ref[...], mult_ref[...] = r, i, a, mult
        bterm = mult * (i * xc)
        acum, hloc = _scan_down(a, bterm, row, tm)
        h = hloc + acum * ph_ref[7:8, :]
        h_ref[...] = h
        xc_ref[...] = xc
        gelu, _ = _gelu_and_grad(gr_ref[...])
        ya_ref[...] = (h * gelu).astype(BF16)
        px_ref[...] = xr_ref[tm - 8:tm, :]
        ph_ref[...] = h_ref[tm - 8:tm, :]

    tile = pl.BlockSpec((tm, W), lambda s, t: (s * nt + t, 0))
    return _call(
        body, phases=phases, name="rnn_fwd", grid=(n_seq, nt),
        in_specs=[tile, tile, _const((CONV_W, W)), _const((1, W)), _const((8, LANES, LANES)), _const((1, W)),
                  _const((8, LANES, LANES)), _const((1, W)), _const((1, W))],
        out_specs=[tile] * 7,
        out_shape=[jax.ShapeDtypeStruct((T, W), F32)] * 6 + [jax.ShapeDtypeStruct((T, W), BF16)],
        scratch_shapes=[pltpu.VMEM((8, W), F32), pltpu.VMEM((8, W), F32)],
    )(xr, gr, conv_w, conv_b, wrg2, b_rg, wig2, b_ig, lam)


def _rnn_bwd(dya, xr, gr, xc, h, gates, conv_w, wrg2, wig2, lam, n_seq, S, tm, phases=()):
    T = xr.shape[0]
    nt = S // tm
    W = D_MODEL
    nb8 = tm // 8

    def body(dya_ref, xr_ref, gr_ref, xc_ref, h_ref, r_ref, i_ref, a_ref, mult_ref, xprev_ref, hprev_ref, cw_ref,
             wrg_ref, wig_ref, lam_ref, dxr_ref, dgr_ref, vec_ref, dwrg_ref, dwig_ref, cg_ref, ndxc_ref, tmp_ref):
        s, ti = pl.program_id(0), pl.program_id(1)

        @pl.when((s == 0) & (ti == 0))
        def _():
            vec_ref[...] = jnp.zeros_like(vec_ref)
            dwrg_ref[...] = jnp.zeros_like(dwrg_ref)
            dwig_ref[...] = jnp.zeros_like(dwig_ref)

        @pl.when(ti == 0)
        def _():
            cg_ref[...] = jnp.zeros_like(cg_ref)
            ndxc_ref[...] = jnp.zeros_like(ndxc_ref)

        first = ti == nt - 1
        row = lax.broadcasted_iota(jnp.int32, (tm, W), 0)
        row8 = lax.broadcasted_iota(jnp.int32, (8, W), 0)
        x = xr_ref[...]
        xc = xc_ref[...]
        hv = h_ref[...]
        xprev = jnp.where(first, 0.0, xprev_ref[...])
        hprev = jnp.where(first, 0.0, hprev_ref[...])
        sp, dsp_dlam = _softplus_neg(lam_ref[...])
        xcb = xc.astype(BF16)
        r, i, a, mult = r_ref[...], i_ref[...], a_ref[...], mult_ref[...]

        gelu, dgelu = _gelu_and_grad(gr_ref[...])
        dya_v = dya_ref[...]
        dgr_ref[...] = (dya_v * hv * dgelu).astype(BF16)
        dh = dya_v * gelu
        c = jnp.where(row < tm - 1, pltpu.roll(a, tm - 1, 0), 1.0)
        ccum, gloc = _scan_up(c, dh, row, tm)
        G = gloc + ccum * cg_ref[0:1, :]
        tmp_ref[...] = a * G
        cg_ref[...] = tmp_ref[0:8, :]

        h_m1 = _shift_down(hv, hprev, 1, row, row8, tm)
        ixc = i * xc
        dixc = G * mult
        dlog_a = (G * h_m1) * a - (G * ixc) * (a * a / mult)
        dr = dlog_a * ((-LRU_C) * sp)
        di = dixc * xc
        drg = dr * r * (1.0 - r)
        dig = di * i * (1.0 - i)
        vec_ref[7:8, :] += jnp.sum(dlog_a * ((-LRU_C) * r), axis=0, keepdims=True) * dsp_dlam
        vec_ref[5:6, :] += jnp.sum(drg, axis=0, keepdims=True)
        vec_ref[6:7, :] += jnp.sum(dig, axis=0, keepdims=True)
        drgb = drg.astype(BF16)
        digb = dig.astype(BF16)
        dxc = dixc * i + _blockdiag_dot(drgb, wrg_ref, True) + _blockdiag_dot(digb, wig_ref, True)
        for b in range(W // LANES):
            sl = slice(b * LANES, (b + 1) * LANES)
            dwrg_ref[b] += _dot_tn(xcb[:, sl], drgb[:, sl])
            dwig_ref[b] += _dot_tn(xcb[:, sl], digb[:, sl])

        vec_ref[4:5, :] += jnp.sum(dxc, axis=0, keepdims=True)
        vec_ref[3:4, :] += jnp.sum(dxc * x, axis=0, keepdims=True)
        dxr = cw_ref[CONV_W - 1:CONV_W, :] * dxc
        nxt = ndxc_ref[...]
        for sft in range(1, CONV_W):
            j = CONV_W - 1 - sft
            vec_ref[j:j + 1, :] += jnp.sum(dxc * _shift_down(x, xprev, sft, row, row8, tm), axis=0, keepdims=True)
            dxr = dxr + cw_ref[j:j + 1, :] * _shift_up(dxc, nxt, sft, row8, tm)
        dxr_ref[...] = dxr.astype(BF16)
        tmp_ref[...] = dxc
        ndxc_ref[...] = tmp_ref[0:8, :]

    rev = lambda s, t: (s * nt + nt - 1 - t, 0)
    tile = pl.BlockSpec((tm, W), rev)
    prev8 = pl.BlockSpec((8, W), lambda s, t: (jnp.maximum((s * nt + nt - 1 - t) * nb8 - 1, 0), 0))
    return _call(
        body, phases=phases, name="rnn_bwd", grid=(n_seq, nt),
        in_specs=[tile] * 9 + [prev8, prev8, _const((CONV_W, W)), _const((8, LANES, LANES)),
                               _const((8, LANES, LANES)), _const((1, W))],
        out_specs=[tile, tile, _const((16, W)), _const((8, LANES, LANES)), _const((8, LANES, LANES))],
        out_shape=[jax.ShapeDtypeStruct((T, W), BF16), jax.ShapeDtypeStruct((T, W), BF16),
                   jax.ShapeDtypeStruct((16, W), F32), jax.ShapeDtypeStruct((8, LANES, LANES), F32),
                   jax.ShapeDtypeStruct((8, LANES, LANES), F32)],
        scratch_shapes=[pltpu.VMEM((8, W), F32), pltpu.VMEM((8, W), F32), pltpu.VMEM((tm, W), F32)],
    )(dya, xr, gr, xc, h, *gates, xr, h, conv_w, wrg2, wig2, lam)


def _head_swap(t, lane):
    w = t.shape[1]
    return jnp.where(lane % HEAD_DIM < HEAD_DIM // 2, pltpu.roll(t, w - HEAD_DIM // 2, 1),
                     pltpu.roll(t, HEAD_DIM // 2, 1))


def _qk_prep(t, gain, cosf, sins, ind, indt, lane):
    ms = _split_dot(t * t, ind) * (1.0 / HEAD_DIM)
    rstd = _split_dot(lax.rsqrt(ms + NORM_EPS), indt)
    tn = (t * rstd) * gain
    return tn * cosf + _head_swap(tn, lane) * sins, rstd


def _qk_prep_bwd(dy, t, rstd, gain, cosf, sins, ind, indt, lane):
    dtn = dy * cosf + _head_swap(dy * sins, lane)
    dgain = jnp.sum(dtn * (t * rstd), axis=0, keepdims=True)
    dn = dtn * gain
    m = _split_dot(_split_dot(dn * t, ind), indt) * (1.0 / HEAD_DIM)
    return rstd * dn - t * (rstd * rstd * rstd * m), dgain


def _attn_mask_t(blk_idx):
    ci = lax.broadcasted_iota(jnp.int32, (2 * WINDOW, WINDOW), 0)
    qi = lax.broadcasted_iota(jnp.int32, (2 * WINDOW, WINDOW), 1)
    diff = WINDOW + qi - ci
    return (diff >= 0) & (diff < WINDOW) & ((ci >= WINDOW) | (blk_idx > 0))


def _stack_heads(t, kvh, lo):
    parts = []
    for i in (2 * kvh, 2 * kvh + 1):
        tp = t[:, i * LANES:(i + 1) * LANES]
        parts += [jnp.where(lo, tp, 0.0), jnp.where(lo, 0.0, tp)]
    return jnp.concatenate(parts, axis=0).astype(BF16)


def _unstack_heads(ts, lo):
    w = WINDOW
    return jnp.where(lo, ts[0:w], ts[w:2 * w]), jnp.where(lo, ts[2 * w:3 * w], ts[3 * w:4 * w])


def _dup_head(t, kvh, lo2):
    m = kvh // 2
    t2 = t[:, m * LANES:(m + 1) * LANES]
    t2r = pltpu.roll(t2, HEAD_DIM, 1)
    return (jnp.where(lo2, t2, t2r) if kvh % 2 == 0 else jnp.where(lo2, t2r, t2)).astype(BF16)


def _fold_head(ts, kvh, lo2):
    tot = ts + pltpu.roll(ts, HEAD_DIM, 1)
    own = lo2 if kvh % 2 == 0 else ~lo2
    return jnp.where(own, tot, 0.0)


KEY_CHUNKS = tuple(slice(i * 64, (i + 1) * 64) for i in range(2 * WINDOW // 64))


def _fold8(x, op):
    return op(x.reshape(x.shape[0] // 8, 8, x.shape[1]), axis=0)


def _softmax_stats(s_ref, b, cols, sink):
    m8 = None
    for c in KEY_CHUNKS:
        t = _fold8(s_ref[b, c, cols], jnp.max)
        m8 = t if m8 is None else jnp.maximum(m8, t)
    mx = jnp.maximum(jnp.max(m8, axis=0, keepdims=True), sink)
    d8 = None
    for c in KEY_CHUNKS:
        t = _fold8(jnp.exp(s_ref[b, c, cols] - mx), jnp.sum)
        d8 = t if d8 is None else d8 + t
    es = jnp.exp(sink - mx)
    inv = 1.0 / (jnp.sum(d8, axis=0, keepdims=True) + es)
    return mx, inv, es * inv


def _attn_fwd(q, k, v, qg, kg, sinks, cosf, sins, ind_q, ind_qt, ind_k, ind_kt, n_seq, S, phases=()):
    T = q.shape[0]
    nblk = S // WINDOW
    W = D_MODEL

    def body(sink_ref, q_ref, k_ref, v_ref, qg_ref, kg_ref, cos_ref, sin_ref, iq_ref, iqt_ref, ik_ref, ikt_ref,
             o_ref, kc_ref, vc_ref, s_ref, p_ref, qs_ref, kd_ref, vd_ref):
        @pl.when(pl.program_id(1) == 0)
        def _():
            kc_ref[...] = jnp.zeros_like(kc_ref)
            vc_ref[...] = jnp.zeros_like(vc_ref)

        lane = lax.broadcasted_iota(jnp.int32, (WINDOW, W), 1)
        lo = lane[:, :LANES] < HEAD_DIM
        lo2 = lax.broadcasted_iota(jnp.int32, (2 * WINDOW, LANES), 1) < HEAD_DIM

        def one_block(h):
            n = pl.program_id(1) * bps + h
            rows = slice(h * WINDOW, (h + 1) * WINDOW)
            cosf, sinv = jnp.tile(cos_ref[rows, :], (1, W // LANES)), jnp.tile(sin_ref[rows, :], (1, W // LANES))
            qr, _ = _qk_prep(q_ref[rows, :], qg_ref[...], cosf, sinv, iq_ref[...], iqt_ref[...], lane)
            kr, _ = _qk_prep(k_ref[rows, :], kg_ref[...], cosf[:, :KV_W], sinv[:, :KV_W], ik_ref[...], ikt_ref[...],
                             lane[:, :KV_W])
            kc_ref[WINDOW:2 * WINDOW, :] = kr
            vc_ref[WINDOW:2 * WINDOW, :] = v_ref[rows, :]
            kc, vc = kc_ref[...], vc_ref[...]
            mask = jnp.tile(_attn_mask_t(n), (1, 4))
            qr = qr * HEAD_DIM ** -0.5
            for kvh in range(N_KV):
                qs_ref[h, kvh] = _stack_heads(qr, kvh, lo)
                kd_ref[h, kvh] = _dup_head(kc, kvh, lo2)
                vd_ref[h, kvh] = _dup_head(vc, kvh, lo2)

            def scores(kvh):
                s_ref[h, kvh % 2] = jnp.where(mask, _dot_nt(kd_ref[h, kvh], qs_ref[h, kvh]), -1e30)

            def softmax(kvh):
                sb, pb = s_ref.at[h], p_ref.at[h]
                b = kvh % 2
                for r in range(4):
                    cols = slice(r * WINDOW, (r + 1) * WINDOW)
                    mx, inv, _ = _softmax_stats(sb, b, cols, sink_ref[4 * kvh + r])
                    for c in KEY_CHUNKS:
                        pb[b, c, cols] = (jnp.exp(sb[b, c, cols] - mx) * inv).astype(BF16)

            def output(kvh):
                o0, o1 = _unstack_heads(_dot_tn(p_ref[h, kvh % 2], vd_ref[h, kvh]), lo)
                o_ref[rows, (2 * kvh) * LANES:(2 * kvh + 1) * LANES] = o0.astype(BF16)
                o_ref[rows, (2 * kvh + 1) * LANES:(2 * kvh + 2) * LANES] = o1.astype(BF16)

            scores(0)
            for kvh in range(N_KV):
                if kvh + 1 < N_KV:
                    scores(kvh + 1)
                softmax(kvh)
                output(kvh)
            kc_ref[0:WINDOW, :] = kr
            vc_ref[0:WINDOW, :] = v_ref[rows, :]

        for h in range(bps):
            one_block(h)

    bps = ATTN_BLOCKS_PER_STEP
    rows_step = bps * WINDOW
    blk = lambda w: pl.BlockSpec((rows_step, w), lambda s, n: (s * (nblk // bps) + n, 0))
    pos = pl.BlockSpec((rows_step, LANES), lambda s, n: (n, 0))
    outs, extra = _call(
        body, phases=phases, name="attn_fwd", grid=(n_seq, nblk // bps),
        in_specs=[pl.BlockSpec(memory_space=pltpu.SMEM), blk(W), blk(KV_W), blk(KV_W), _const((1, W)),
                  _const((1, KV_W)), pos, pos, _const((W, LANES)), _const((LANES, W)), _const((KV_W, LANES)),
                  _const((LANES, KV_W))],
        out_specs=blk(W), out_shape=jax.ShapeDtypeStruct((T, W), BF16),
        scratch_shapes=[pltpu.VMEM((2 * WINDOW, KV_W), F32), pltpu.VMEM((2 * WINDOW, KV_W), F32),
                        pltpu.VMEM((bps, 2, 2 * WINDOW, 4 * WINDOW), F32),
                        pltpu.VMEM((bps, 2, 2 * WINDOW, 4 * WINDOW), BF16),
                        pltpu.VMEM((bps, N_KV, 4 * WINDOW, LANES), BF16),
                        pltpu.VMEM((bps, N_KV, 2 * WINDOW, LANES), BF16),
                        pltpu.VMEM((bps, N_KV, 2 * WINDOW, LANES), BF16)],
    )(sinks, q, k, v, qg, kg, cosf, sins, ind_q, ind_qt, ind_k, ind_kt)
    return outs[0], extra


def _attn_bwd(do, q, k, v, qg, kg, sinks, cosf, sins, ind_q, ind_qt, ind_k, ind_kt, n_seq, S, phases=()):
    T = q.shape[0]
    nblk = S // WINDOW
    W = D_MODEL

    def body(sink_ref, do_ref, q_ref, k_ref, v_ref, qg_ref, kg_ref, cos_ref, sin_ref, iq_ref, iqt_ref, ik_ref,
             ikt_ref, dq_ref, dkc_ref, dkp_ref, dvc_ref, dvp_ref, dqg_ref, dsk_ref, kc_ref, vc_ref, dqr_ref,
             dk_ref, dv_ref, s_ref, dp_ref, p_ref, ds_ref, qs_ref, dos_ref, kd_ref, vd_ref):
        s_id, n_step = pl.program_id(0), pl.program_id(1)

        @pl.when((s_id == 0) & (n_step == 0))
        def _():
            dqg_ref[...] = jnp.zeros_like(dqg_ref)
            dsk_ref[...] = jnp.zeros_like(dsk_ref)

        @pl.when(n_step == 0)
        def _():
            kc_ref[...] = jnp.zeros_like(kc_ref)
            vc_ref[...] = jnp.zeros_like(vc_ref)

        lane = lax.broadcasted_iota(jnp.int32, (WINDOW, W), 1)
        lane_k = lane[:, :KV_W]
        lo = lane[:, :LANES] < HEAD_DIM
        lo2 = lax.broadcasted_iota(jnp.int32, (2 * WINDOW, LANES), 1) < HEAD_DIM
        scale = HEAD_DIM ** -0.5

        def one_block(h):
            n = n_step * bps + h
            rows = slice(h * WINDOW, (h + 1) * WINDOW)
            cosf, sinv = jnp.tile(cos_ref[rows, :], (1, W // LANES)), jnp.tile(sin_ref[rows, :], (1, W // LANES))
            qv = q_ref[rows, :]
            qr, q_rstd = _qk_prep(qv, qg_ref[...], cosf, sinv, iq_ref[...], iqt_ref[...], lane)
            kr, _ = _qk_prep(k_ref[rows, :], kg_ref[...], cosf[:, :KV_W], sinv[:, :KV_W], ik_ref[...], ikt_ref[...],
                             lane_k)
            kc_ref[WINDOW:2 * WINDOW, :] = kr
            vc_ref[WINDOW:2 * WINDOW, :] = v_ref[rows, :]
            kc, vc = kc_ref[...], vc_ref[...]
            dov = do_ref[rows, :]
            mask = jnp.tile(_attn_mask_t(n), (1, 4))
            qr = qr * scale
            dk_ref[h] = jnp.zeros((2 * WINDOW, KV_W), F32)
            dv_ref[h] = jnp.zeros((2 * WINDOW, KV_W), F32)
            for kvh in range(N_KV):
                qs_ref[h, kvh] = _stack_heads(qr, kvh, lo)
                dos_ref[h, kvh] = _stack_heads(dov, kvh, lo)
                kd_ref[h, kvh] = _dup_head(kc, kvh, lo2)
                vd_ref[h, kvh] = _dup_head(vc, kvh, lo2)
            sb, dpb, pb, dsb = s_ref.at[h], dp_ref.at[h], p_ref.at[h], ds_ref.at[h]

            def scores(kvh):
                b = kvh % 2
                sb[b] = jnp.where(mask, _dot_nt(kd_ref[h, kvh], qs_ref[h, kvh]), -1e30)
                dpb[b] = _dot_nt(vd_ref[h, kvh], dos_ref[h, kvh])

            def softmax(kvh):
                b = kvh % 2
                for r in range(4):
                    cols = slice(r * WINDOW, (r + 1) * WINDOW)
                    head = 4 * kvh + r
                    mx, inv, ps = _softmax_stats(sb, b, cols, sink_ref[head])
                    g8 = None
                    for c in KEY_CHUNKS:
                        t = _fold8(jnp.exp(sb[b, c, cols] - mx) * dpb[b, c, cols], jnp.sum)
                        g8 = t if g8 is None else g8 + t
                    dd = jnp.sum(g8, axis=0, keepdims=True) * inv
                    for c in KEY_CHUNKS:
                        p = jnp.exp(sb[b, c, cols] - mx) * inv
                        pb[b, c, cols] = p.astype(BF16)
                        dsb[b, c, cols] = (p * (dpb[b, c, cols] - dd)).astype(BF16)
                    dsk_ref[head:head + 1, :] -= ps * dd

            def grads(kvh):
                m, b = kvh // 2, kvh % 2
                dq0, dq1 = _unstack_heads(_dot_tn(dsb[b], kd_ref[h, kvh]) * scale, lo)
                dqr_ref[h, :, (2 * kvh) * LANES:(2 * kvh + 1) * LANES] = dq0
                dqr_ref[h, :, (2 * kvh + 1) * LANES:(2 * kvh + 2) * LANES] = dq1
                dk_ref[h, :, m * LANES:(m + 1) * LANES] += _fold_head(_dot(dsb[b], qs_ref[h, kvh]), kvh, lo2)
                dv_ref[h, :, m * LANES:(m + 1) * LANES] += _fold_head(_dot(pb[b], dos_ref[h, kvh]), kvh, lo2)

            scores(0)
            for kvh in range(N_KV):
                if kvh + 1 < N_KV:
                    scores(kvh + 1)
                softmax(kvh)
                grads(kvh)
            dq, dqg = _qk_prep_bwd(dqr_ref[h], qv, q_rstd, qg_ref[...], cosf, sinv, iq_ref[...], iqt_ref[...], lane)
            dq_ref[rows, :] = dq.astype(BF16)
            dqg_ref[...] += dqg
            dkp_ref[rows, :] = dk_ref[h, 0:WINDOW, :]
            dkc_ref[rows, :] = dk_ref[h, WINDOW:2 * WINDOW, :]
            dvp_ref[rows, :] = dv_ref[h, 0:WINDOW, :]
            dvc_ref[rows, :] = dv_ref[h, WINDOW:2 * WINDOW, :]
            kc_ref[0:WINDOW, :] = kr
            vc_ref[0:WINDOW, :] = v_ref[rows, :]

        for h in range(bps):
            one_block(h)

    bps = ATTN_BLOCKS_PER_STEP
    rows_step = bps * WINDOW
    blk = lambda w: pl.BlockSpec((rows_step, w), lambda s, n: (s * (nblk // bps) + n, 0))
    pos = pl.BlockSpec((rows_step, LANES), lambda s, n: (n, 0))
    kv_out = jax.ShapeDtypeStruct((T, KV_W), F32)
    stage = lambda dt: pltpu.VMEM((bps, 2, 2 * WINDOW, 4 * WINDOW), dt)
    return _call(
        body, phases=phases, name="attn_bwd", grid=(n_seq, nblk // bps),
        in_specs=[pl.BlockSpec(memory_space=pltpu.SMEM), blk(W), blk(W), blk(KV_W), blk(KV_W), _const((1, W)),
                  _const((1, KV_W)), pos, pos, _const((W, LANES)), _const((LANES, W)), _const((KV_W, LANES)),
                  _const((LANES, KV_W))],
        out_specs=[blk(W), blk(KV_W), blk(KV_W), blk(KV_W), blk(KV_W), _const((1, W)), _const((N_HEADS, LANES))],
        out_shape=[jax.ShapeDtypeStruct((T, W), BF16), kv_out, kv_out, kv_out, kv_out,
                   jax.ShapeDtypeStruct((1, W), F32), jax.ShapeDtypeStruct((N_HEADS, LANES), F32)],
        scratch_shapes=[pltpu.VMEM((2 * WINDOW, KV_W), F32), pltpu.VMEM((2 * WINDOW, KV_W), F32),
                        pltpu.VMEM((bps, WINDOW, W), F32), pltpu.VMEM((bps, 2 * WINDOW, KV_W), F32),
                        pltpu.VMEM((bps, 2 * WINDOW, KV_W), F32), stage(F32), stage(F32), stage(BF16), stage(BF16),
                        pltpu.VMEM((bps, N_KV, 4 * WINDOW, LANES), BF16),
                        pltpu.VMEM((bps, N_KV, 4 * WINDOW, LANES), BF16),
                        pltpu.VMEM((bps, N_KV, 2 * WINDOW, LANES), BF16),
                        pltpu.VMEM((bps, N_KV, 2 * WINDOW, LANES), BF16)],
    )(sinks, do, q, k, v, qg, kg, cosf, sins, ind_q, ind_qt, ind_k, ind_kt)


def _kv_bwd(dkc, dkp, dvc, dvp, k, kg, cosf, sins, ind_k, ind_kt, n_seq, S, phases=()):
    T = k.shape[0]
    nblk = S // WINDOW
    nb = min(KV_BLOCKS_PER_STEP, nblk)
    rows, nt = nb * WINDOW, nblk // nb

    def body(dkc_ref, dkp_ref, dkn_ref, dvc_ref, dvp_ref, dvn_ref, k_ref, kg_ref, cos_ref, sin_ref, ik_ref, ikt_ref,
             dk_ref, dv_ref, dkg_ref):
        s_id, n = pl.program_id(0), pl.program_id(1)

        @pl.when((s_id == 0) & (n == 0))
        def _():
            dkg_ref[...] = jnp.zeros_like(dkg_ref)

        blk = n * nb + lax.broadcasted_iota(jnp.int32, (rows, KV_W), 0) // WINDOW
        has_next = blk < nblk - 1

        def from_next(part_ref, next_ref):
            moved = jnp.concatenate([part_ref[WINDOW:rows, :], next_ref[...]], axis=0) if nb > 1 else next_ref[...]
            return jnp.where(has_next, moved, 0.0)

        lane = lax.broadcasted_iota(jnp.int32, (rows, KV_W), 1)
        dkr = dkc_ref[...] + from_next(dkp_ref, dkn_ref)
        dv_ref[...] = (dvc_ref[...] + from_next(dvp_ref, dvn_ref)).astype(BF16)
        cosf, sinv = jnp.tile(cos_ref[...], (1, KV_W // LANES)), jnp.tile(sin_ref[...], (1, KV_W // LANES))
        kv = k_ref[...]
        _, rstd = _qk_prep(kv, kg_ref[...], cosf, sinv, ik_ref[...], ikt_ref[...], lane)
        dk, dkg = _qk_prep_bwd(dkr, kv, rstd, kg_ref[...], cosf, sinv, ik_ref[...], ikt_ref[...], lane)
        dk_ref[...] = dk.astype(BF16)
        dkg_ref[...] += dkg

    cur = pl.BlockSpec((rows, KV_W), lambda s, n: (s * nt + n, 0))
    nxt = pl.BlockSpec((WINDOW, KV_W), lambda s, n: (s * nblk + jnp.minimum((n + 1) * nb, nblk - 1), 0))
    pos = pl.BlockSpec((rows, LANES), lambda s, n: (n, 0))
    return _call(
        body, phases=phases, name="kv_bwd", grid=(n_seq, nt),
        in_specs=[cur, cur, nxt, cur, cur, nxt, cur, _const((1, KV_W)), pos, pos, _const((KV_W, LANES)),
                  _const((LANES, KV_W))],
        out_specs=[cur, cur, _const((1, KV_W))],
        out_shape=[jax.ShapeDtypeStruct((T, KV_W), BF16), jax.ShapeDtypeStruct((T, KV_W), BF16),
                   jax.ShapeDtypeStruct((1, KV_W), F32)],
    )(dkc, dkp, dkp, dvc, dvp, dvp, k, kg, cosf, sins, ind_k, ind_kt)


def _merge_fwd(x, ya, o, ga, gb, w_rnn, w_attn, w_out, tm, phases=()):
    T = x.shape[0]
    W = D_MODEL

    def body(x_ref, ya_ref, o_ref, ga_ref, gb_ref, wr_ref, wa_ref, wo_ref, x1_ref, mg_ref):
        y_a = _dot(ya_ref[...], wr_ref[...])
        y_b = _dot(o_ref[...], wa_ref[...])
        mg = (_sigmoid(ga_ref[...]) * y_a + _sigmoid(gb_ref[...]) * y_b).astype(BF16)
        mg_ref[...] = mg
        x1_ref[...] = x_ref[...] + _dot(mg, wo_ref[...])

    row = pl.BlockSpec((tm, W), lambda i: (i, 0))
    sq = _const((W, W))
    return _call(
        body, phases=phases, name="merge_fwd", grid=(T // tm,),
        in_specs=[row, row, row, row, row, sq, sq, sq], out_specs=[row, row],
        out_shape=[jax.ShapeDtypeStruct((T, W), F32), jax.ShapeDtypeStruct((T, W), BF16)],
    )(x, ya, o, ga, gb, w_rnn, w_attn, w_out)


def _merge_bwd(dx1, ga, gb, ya, o, w_rnn, w_attn, w_out, tm, phases=()):
    T = dx1.shape[0]
    W = D_MODEL

    def body(dx1_ref, ga_ref, gb_ref, ya_ref, o_ref, wr_ref, wa_ref, wo_ref,
             dga_ref, dgb_ref, dya_ref, dyb_ref, dyain_ref, do_ref):
        dm = _dot_nt(dx1_ref[...].astype(BF16), wo_ref[...])
        sa = _sigmoid(ga_ref[...])
        sb = _sigmoid(gb_ref[...])
        dga_ref[...] = (dm * _dot(ya_ref[...], wr_ref[...]) * (sa * (1.0 - sa))).astype(BF16)
        dgb_ref[...] = (dm * _dot(o_ref[...], wa_ref[...]) * (sb * (1.0 - sb))).astype(BF16)
        dya = (dm * sa).astype(BF16)
        dyb = (dm * sb).astype(BF16)
        dya_ref[...] = dya
        dyb_ref[...] = dyb
        dyain_ref[...] = _dot_nt(dya, wr_ref[...])
        do_ref[...] = _dot_nt(dyb, wa_ref[...])

    row = pl.BlockSpec((tm, W), lambda i: (i, 0))
    sq = _const((W, W))
    b16 = jax.ShapeDtypeStruct((T, W), BF16)
    f32 = jax.ShapeDtypeStruct((T, W), F32)
    return _call(
        body, phases=phases, name="merge_bwd", grid=(T // tm,),
        in_specs=[row, row, row, row, row, sq, sq, sq], out_specs=[row] * 6,
        out_shape=[b16, b16, b16, b16, f32, f32],
    )(dx1, ga, gb, ya, o, w_rnn, w_attn, w_out)


def _mlp_fwd(x1, g_mlp, w_up, w_down, tm, phases=()):
    T = x1.shape[0]
    W = D_MODEL

    def body(x_ref, g_ref, wu_ref, wd_ref, x2_ref, hm_ref, u_ref, act_ref):
        xv = x_ref[...]
        hm, _ = _rms_fwd(xv, g_ref[...])
        hmb = hm.astype(BF16)
        hm_ref[...] = hmb
        for j in range(N_CHIPS):
            u = _dot(hmb, wu_ref[j])
            u_ref[:, j * W:(j + 1) * W] = u
            ru = jnp.maximum(u, 0.0)
            act_ref[:, j * W:(j + 1) * W] = (ru * ru).astype(BF16)
        x2_ref[...] = xv + _dot(act_ref[...], wd_ref[...])

    row = lambda w: pl.BlockSpec((tm, w), lambda i: (i, 0))
    return _call(
        body, phases=phases, name="mlp_fwd", grid=(T // tm,),
        in_specs=[row(W), _const((1, W)), _const((N_CHIPS, W, W)), _const((D_FF, W))],
        out_specs=[row(W), row(W), row(D_FF), row(D_FF)],
        out_shape=[jax.ShapeDtypeStruct((T, W), F32), jax.ShapeDtypeStruct((T, W), BF16),
                   jax.ShapeDtypeStruct((T, D_FF), F32), jax.ShapeDtypeStruct((T, D_FF), BF16)],
    )(x1, g_mlp, w_up, w_down)


def _mlp_bwd(dx2, u, x1, g_mlp, w_up, w_down, tm, phases=()):
    T = x1.shape[0]
    W = D_MODEL

    def body(dx2_ref, u_ref, x_ref, g_ref, wu_ref, wd_ref, dx1_ref, du_ref, dg_ref):
        @pl.when(pl.program_id(0) == 0)
        def _():
            dg_ref[...] = jnp.zeros_like(dg_ref)

        dx2 = dx2_ref[...]
        dact = _dot_nt(dx2.astype(BF16), wd_ref[...])
        du_ref[...] = (dact * (2.0 * jnp.maximum(u_ref[...], 0.0))).astype(BF16)
        dhm = jnp.zeros((tm, W), F32)
        for j in range(N_CHIPS):
            dhm = dhm + _dot_nt(du_ref[:, j * W:(j + 1) * W], wu_ref[j])
        xv = x_ref[...]
        g = g_ref[...]
        _, r = _rms_fwd(xv, g)
        dx, dg = _rms_bwd(dhm, xv, r, g)
        dx1_ref[...] = dx2 + dx
        dg_ref[...] += dg

    row = lambda w: pl.BlockSpec((tm, w), lambda i: (i, 0))
    return _call(
        body, phases=phases, name="mlp_bwd", grid=(T // tm,),
        in_specs=[row(W), row(D_FF), row(W), _const((1, W)), _const((N_CHIPS, W, W)), _const((D_FF, W))],
        out_specs=[row(W), row(D_FF), _const((1, W))],
        out_shape=[jax.ShapeDtypeStruct((T, W), F32), jax.ShapeDtypeStruct((T, D_FF), BF16),
                   jax.ShapeDtypeStruct((1, W), F32)],
    )(dx2, u, x1, g_mlp, w_up, w_down)


def _ple_loss(x2, p, target, g_ple, w_gate, w_proj, tm, phases=()):
    T = x2.shape[0]
    W = D_MODEL
    cw = W // N_CHIPS

    def body(x_ref, p_ref, t_ref, g_ref, wg_ref, wp_ref, loss_ref, dx2_ref, pb_ref, de_ref, hp_ref, dtg_ref, dg_ref):
        @pl.when(pl.program_id(0) == 0)
        def _():
            dg_ref[...] = jnp.zeros_like(dg_ref)
            loss_ref[...] = jnp.zeros_like(loss_ref)

        xv = x_ref[...]
        g = g_ref[...]
        pb = p_ref[...].astype(BF16)
        pb_ref[...] = pb
        e = jnp.concatenate([_dot(pb, wp_ref[j]) for j in range(N_CHIPS)], axis=1)
        hp, r = _rms_fwd(xv, g)
        hpb = hp.astype(BF16)
        hp_ref[...] = hpb
        sg = _sigmoid(_dot(hpb, wg_ref[...]))
        diff = (xv + e * sg) - t_ref[...]
        loss_ref[...] += jnp.sum(diff * diff) * (0.5 / W)
        dx3 = diff * (1.0 / W)
        de_ref[...] = (dx3 * sg).astype(BF16)
        dtg = (dx3 * e * (sg * (1.0 - sg))).astype(BF16)
        dtg_ref[...] = dtg
        dx, dg = _rms_bwd(_dot_nt(dtg, wg_ref[...]), xv, r, g)
        dx2_ref[...] = dx3 + dx
        dg_ref[...] += dg

    row = lambda w: pl.BlockSpec((tm, w), lambda i: (i, 0))
    b16 = lambda w: jax.ShapeDtypeStruct((T, w), BF16)
    return _call(
        body, phases=phases, name="ple_loss", grid=(T // tm,),
        in_specs=[row(W), row(PLE_DIM), row(W), _const((1, W)), _const((W, W)), _const((N_CHIPS, PLE_DIM, cw))],
        out_specs=[_const((8, LANES)), row(W), row(PLE_DIM), row(W), row(W), row(W), _const((1, W))],
        out_shape=[jax.ShapeDtypeStruct((8, LANES), F32), jax.ShapeDtypeStruct((T, W), F32), b16(PLE_DIM),
                   b16(W), b16(W), b16(W), jax.ShapeDtypeStruct((1, W), F32)],
    )(x2, p, target, g_ple, w_gate, w_proj)


def _adamw(w, g, m, v, name, tr, phases=()):
    R, C = w.shape
    c1 = 1.0 / (1.0 - ADAM_B1 ** ADAM_STEP)
    c2 = 1.0 / (1.0 - ADAM_B2 ** ADAM_STEP)

    def body(w_ref, g_ref, m_ref, v_ref, go_ref, d_ref, nm_ref, nv_ref):
        gv = g_ref[...]
        go_ref[...] = gv
        nm = ADAM_B1 * m_ref[...] + (1.0 - ADAM_B1) * gv
        nv = ADAM_B2 * v_ref[...] + (1.0 - ADAM_B2) * (gv * gv)
        nm_ref[...] = nm
        nv_ref[...] = nv
        d_ref[...] = (-ADAM_LR) * ((nm * c1) / (jnp.sqrt(nv * c2) + ADAM_EPS) + ADAM_WD * w_ref[...])

    row = pl.BlockSpec((tr, C), lambda i: (i, 0))
    sds = jax.ShapeDtypeStruct((R, C), F32)
    return _call(
        body, phases=phases, name=name, grid=(R // tr,), in_specs=[row] * 4, out_specs=[row] * 4,
        out_shape=[sds] * 4,
    )(w, g, m, v)


def _indicator(width):
    ind = np.zeros((width, LANES), np.float32)
    ind[np.arange(width), np.arange(width) // HEAD_DIM] = 1.0
    return jnp.asarray(ind, BF16), jnp.asarray(ind.T, BF16)


def _rope_tables(S):
    inv = ROPE_THETA ** (-jnp.arange(0, HEAD_DIM, 2, dtype=F32) / HEAD_DIM)
    ang = jnp.arange(S, dtype=F32)[:, None] * inv[None, :]
    cos, sin = jnp.cos(ang), jnp.sin(ang)
    cosf = jnp.tile(jnp.concatenate([cos, cos], axis=1), (1, LANES // HEAD_DIM))
    sins = jnp.tile(jnp.concatenate([-sin, sin], axis=1), (1, LANES // HEAD_DIM))
    return cosf, sins


def _pair_blockdiag(w):
    w4 = w.reshape(8, 2, HEAD_DIM, HEAD_DIM)
    eye = jnp.eye(2, dtype=w.dtype)
    return jnp.einsum("bpij,pq->bpiqj", w4, eye).reshape(8, LANES, LANES)


def _pair_blockdiag_extract(g):
    g5 = g.reshape(8, 2, HEAD_DIM, 2, HEAD_DIM)
    return jnp.stack([g5[:, 0, :, 0, :], g5[:, 1, :, 1, :]], axis=1).reshape(16, HEAD_DIM, HEAD_DIM)


def _pair_sum(parts, sibs, name):
    n = len(parts)
    dims = [(p.shape[1] // 2, p.shape[2]) for p in parts]

    def body(*refs):
        p_r, s_r, send_r, own_r, mine_r, sem = (refs[0:n], refs[n:2 * n], refs[2 * n:3 * n], refs[3 * n:4 * n],
                                                refs[4 * n:5 * n], refs[5 * n])
        x, y, c, chips = _mesh_pos()
        me = 2 * x + y
        loads = []
        for i, (R, _) in enumerate(dims):
            mine, _ = _half_rows(c, R)
            cp = pltpu.make_async_copy(p_r[i].at[:, mine, :], mine_r[i], sem.at[i])
            cp.start()
            loads.append(cp)
        for i in range(n):
            loads[i].wait()
            for j, (cx, cy) in enumerate(chips):
                k = 2 * cx + cy
                send_r[i][j] = (mine_r[i][k] + s_r[i][k]).astype(BF16)
            own_r[i][...] = mine_r[i][me] + s_r[i][me]

    vm = pl.BlockSpec(memory_space=pltpu.VMEM)
    out = pl.pallas_call(
        body, name=name, in_specs=[pl.BlockSpec(memory_space=pl.ANY)] * n + [vm] * n, out_specs=[vm] * (2 * n),
        out_shape=[jax.ShapeDtypeStruct((3, R, C), BF16) for R, C in dims]
        + [jax.ShapeDtypeStruct((R, C), F32) for R, C in dims],
        scratch_shapes=[pltpu.VMEM((N_CHIPS, R, C), F32) for R, C in dims] + [pltpu.SemaphoreType.DMA((n,))],
        compiler_params=pltpu.CompilerParams(vmem_limit_bytes=VMEM_LIMIT),
    )(*parts, *sibs)
    return out[:n], out[n:]


def _chip_sum(owns, recvs, name):
    n = len(owns)
    dims = [o.shape for o in owns]

    def body(*refs):
        own_r, recv_r, red_r, stage_r, sem = refs[0:n], refs[n:2 * n], refs[2 * n:3 * n], refs[3 * n:4 * n], refs[4 * n]
        x, y, c, _ = _mesh_pos()
        me = 2 * x + y
        stores = []
        for i, (R, _) in enumerate(dims):
            for k_me in range(N_CHIPS):

                @pl.when(me == k_me)
                def _():
                    acc = None
                    for k in range(N_CHIPS):
                        slot = ((k // 2) ^ (k_me // 2)) + 2 * ((k % 2) ^ (k_me % 2)) - 1
                        term = own_r[i][...] if k == k_me else recv_r[i][slot].astype(F32)
                        acc = term if acc is None else acc + term
                    stage_r[i][...] = acc

            mine, _ = _half_rows(c, R)
            cp = pltpu.make_async_copy(stage_r[i], red_r[i].at[mine, :], sem.at[i])
            cp.start()
            stores.append(cp)
        for cp in stores:
            cp.wait()

    vm = pl.BlockSpec(memory_space=pltpu.VMEM)
    return pl.pallas_call(
        body, name=name, in_specs=[vm] * (2 * n), out_specs=[pl.BlockSpec(memory_space=pl.ANY)] * n,
        out_shape=[jax.ShapeDtypeStruct((2 * R, C), F32) for R, C in dims],
        scratch_shapes=[pltpu.VMEM((R, C), F32) for R, C in dims] + [pltpu.SemaphoreType.DMA((n,))],
        compiler_params=pltpu.CompilerParams(vmem_limit_bytes=VMEM_LIMIT),
    )(*owns, *recvs)


def _gather_bf16(shard, name):
    R2, C = shard.shape
    R = R2 // 2
    H = R // 2

    def body(s_ref, o_ref, send_sems, recv_sems):
        x, y, c, _ = _mesh_pos()
        me, chip_x, chip_y, chip_d = 2 * x + y, 2 * (1 - x) + y, 2 * x + (1 - y), 2 * (1 - x) + (1 - y)
        to_x, to_y, me_dev, sibling = (1 - x, y, c), (x, 1 - y, c), (x, y, c), (x, y, 1 - c)

        def rows(core, off, n):
            return pl.ds(pl.multiple_of(core * R + off, H), n)

        def copy(k, chip, rws, to):
            blk = o_ref.at[chip, rws]
            return _remote(blk, blk, (send_sems.at[k], recv_sems.at[k]), to)

        piece, half_a, half_b = rows(c, 0, R), rows(c, 0, H), rows(c, H, H)
        o_ref[me] = s_ref[...].astype(BF16)
        sends = [copy(0, me, piece, to_x), copy(1, me, piece, to_y)]
        for cp in sends:
            cp.start()
        arrivals = [(0, chip_x, piece, (2, half_a, to_y)), (1, chip_y, piece, (3, half_b, to_x)),
                    (2, chip_d, half_a, None), (3, chip_d, half_b, None)]
        for k, chip, rws, onward in arrivals:
            copy(k, chip, rws, me_dev).wait_recv()
            if onward is not None:
                sends.append(copy(onward[0], chip, onward[1], onward[2]))
                sends[-1].start()
            sends.append(copy(4 + k, chip, rws, sibling))
            sends[-1].start()
        for k, chip, rws in [(4, chip_x, rows(1 - c, 0, R)), (5, chip_y, rows(1 - c, 0, R)),
                             (6, chip_d, rows(1 - c, 0, H)), (7, chip_d, rows(1 - c, H, H))]:
            copy(k, chip, rws, me_dev).wait_recv()
        for cp in sends:
            cp.wait_send()

    return pl.pallas_call(
        body, name=name, out_shape=jax.ShapeDtypeStruct((N_CHIPS, R2, C), BF16),
        in_specs=[pl.BlockSpec(memory_space=pltpu.VMEM)], out_specs=pl.BlockSpec(memory_space=pltpu.VMEM),
        scratch_shapes=[pltpu.SemaphoreType.DMA((8,)), pltpu.SemaphoreType.DMA((8,))],
        compiler_params=pltpu.CompilerParams(vmem_limit_bytes=VMEM_LIMIT),
    )(shard)


def _pair_exchange_sum(partial, partial_b, name):
    _, R2, C = partial.shape
    R = R2 // 2

    def body(p_ref, pb_ref, send_ref, own_ref, mine_ref, sib_ref, loc_sems, send_sems, recv_sems):
        x, y, c, chips = _mesh_pos()
        me = 2 * x + y
        mine, theirs = _half_rows(c, R)
        order = [2 * cx + cy for cx, cy in chips] + [me]
        locs, pairs = [], []
        for i, k in enumerate(order):
            loc = pltpu.make_async_copy(p_ref.at[k, mine, :], mine_ref.at[i], loc_sems.at[i])
            pair = _remote(pb_ref.at[k, theirs, :], sib_ref.at[i], (send_sems.at[i], recv_sems.at[i]), (x, y, 1 - c))
            loc.start()
            pair.start()
            locs.append(loc)
            pairs.append(pair)
        for i in range(N_CHIPS):
            locs[i].wait()
            pairs[i].wait_recv()
            total = mine_ref[i] + sib_ref[i].astype(F32)
            if i < 3:
                send_ref[i] = total.astype(BF16)
            else:
                own_ref[...] = total
        for pair in pairs:
            pair.wait_send()

    vm = pl.BlockSpec(memory_space=pltpu.VMEM)
    return pl.pallas_call(
        body, name=name, in_specs=[pl.BlockSpec(memory_space=pl.ANY)] * 2, out_specs=[vm, vm],
        out_shape=[jax.ShapeDtypeStruct((3, R, C), BF16), jax.ShapeDtypeStruct((R, C), F32)],
        scratch_shapes=[pltpu.VMEM((N_CHIPS, R, C), F32), pltpu.VMEM((N_CHIPS, R, C), BF16),
                        pltpu.SemaphoreType.DMA((N_CHIPS,)), pltpu.SemaphoreType.DMA((N_CHIPS,)),
                        pltpu.SemaphoreType.DMA((N_CHIPS,))],
        compiler_params=pltpu.CompilerParams(vmem_limit_bytes=VMEM_LIMIT),
    )(partial, partial_b)


def _allreduce_small(buf, name):
    rows, width = buf.shape
    h = rows // 2

    def body(b_ref, o_ref, sib_ref, pair_ref, in_ref, pair_sems, send_sems, recv_sems, fin_sems):
        x, y, c, chips = _mesh_pos()
        me = 2 * x + y
        mine, theirs = _half_rows(c, h)
        sibling = (x, y, 1 - c)
        pair = _remote(b_ref.at[theirs], sib_ref, (pair_sems.at[0], pair_sems.at[1]), sibling)
        pair.start()
        pair.wait()
        pair_ref[...] = b_ref[mine, :] + sib_ref[...]
        sends = []
        for j, (cx, cy) in enumerate(chips):
            cp = _remote(pair_ref, in_ref.at[j], (send_sems.at[j], recv_sems.at[j]), (cx, cy, c))
            cp.start()
            sends.append(cp)
        for cp in sends:
            cp.wait_recv()
        acc = None
        for k in range(N_CHIPS):
            term = jnp.where(me == k, pair_ref[...], in_ref[_peer_slot(k, x, y)])
            acc = term if acc is None else acc + term
        o_ref[mine, :] = acc
        fin = _remote(o_ref.at[mine], o_ref.at[mine], (fin_sems.at[0], fin_sems.at[1]), sibling)
        fin.start()
        fin.wait_send()
        _remote(o_ref.at[theirs], o_ref.at[theirs], (fin_sems.at[0], fin_sems.at[1]), sibling).wait_recv()
        for cp in sends:
            cp.wait_send()

    return pl.pallas_call(
        body, name=name, out_shape=jax.ShapeDtypeStruct((rows, width), F32),
        in_specs=[pl.BlockSpec(memory_space=pltpu.VMEM)], out_specs=pl.BlockSpec(memory_space=pltpu.VMEM),
        scratch_shapes=[pltpu.VMEM((h, width), F32), pltpu.VMEM((h, width), F32), pltpu.VMEM((3, h, width), F32),
                        pltpu.SemaphoreType.DMA((2,)), pltpu.SemaphoreType.DMA((3,)), pltpu.SemaphoreType.DMA((3,)),
                        pltpu.SemaphoreType.DMA((2,))],
        compiler_params=pltpu.CompilerParams(vmem_limit_bytes=VMEM_LIMIT),
    )(buf)


def _adamw_small(ws, gs, ms, vs):
    n = len(ws)
    c1 = 1.0 / (1.0 - ADAM_B1 ** ADAM_STEP)
    c2 = 1.0 / (1.0 - ADAM_B2 ** ADAM_STEP)

    def body(*refs):
        w_r, g_r, m_r, v_r = refs[0:n], refs[n:2 * n], refs[2 * n:3 * n], refs[3 * n:4 * n]
        d_r, nm_r, nv_r = refs[4 * n:5 * n], refs[5 * n:6 * n], refs[6 * n:7 * n]
        for i in range(n):
            gv = g_r[i][...]
            nm = ADAM_B1 * m_r[i][...] + (1.0 - ADAM_B1) * gv
            nv = ADAM_B2 * v_r[i][...] + (1.0 - ADAM_B2) * (gv * gv)
            nm_r[i][...] = nm
            nv_r[i][...] = nv
            d_r[i][...] = (-ADAM_LR) * ((nm * c1) / (jnp.sqrt(nv * c2) + ADAM_EPS) + ADAM_WD * w_r[i][...])

    vm = pl.BlockSpec(memory_space=pltpu.VMEM)
    sds = [jax.ShapeDtypeStruct(w.shape, F32) for w in ws]
    out = pl.pallas_call(body, name="adamw_small", in_specs=[vm] * (4 * n), out_specs=[vm] * (3 * n),
                         out_shape=sds * 3)(*ws, *gs, *ms, *vs)
    return out[0:n], out[n:2 * n], out[2 * n:3 * n]


_BIG = ("w_in", "w_rnn_proj", "w_attn_proj", "w_out", "w_up", "w_down", "w_ple_gate", "w_ple_proj")
_SMALL = ("g_mix", "conv_w", "conv_b", "w_rg", "b_rg", "w_ig", "b_ig", "lru_lambda", "q_gain", "k_gain", "sinks",
          "g_mlp", "g_ple")
_WEIGHTS = ("g_mix", "w_in", "conv_w", "conv_b", "w_rg", "b_rg", "w_ig", "b_ig", "lru_lambda", "w_rnn_proj",
            "q_gain", "k_gain", "sinks", "w_attn_proj", "w_out", "g_mlp", "w_up", "w_down", "g_ple", "w_ple_gate",
            "w_ple_proj")


def _pad_row(v):
    v = v.reshape(1, -1)
    return jnp.pad(v, ((0, 0), (0, D_MODEL - v.shape[1])))


def kernel(x, p, g_mix, w_in, conv_w, conv_b, w_rg, b_rg, w_ig, b_ig, lru_lambda, w_rnn_proj, q_gain, k_gain, sinks, w_attn_proj, w_out, g_mlp, w_up, w_down, g_ple, w_ple_gate, w_ple_proj, loss_target, m_g_mix, m_w_in, m_conv_w, m_conv_b, m_w_rg, m_b_rg, m_w_ig, m_b_ig, m_lru_lambda, m_w_rnn_proj, m_q_gain, m_k_gain, m_sinks, m_w_attn_proj, m_w_out, m_g_mlp, m_w_up, m_w_down, m_g_ple, m_w_ple_gate, m_w_ple_proj, v_g_mix, v_w_in, v_conv_w, v_conv_b, v_w_rg, v_b_rg, v_w_ig, v_b_ig, v_lru_lambda, v_w_rnn_proj, v_q_gain, v_k_gain, v_sinks, v_w_attn_proj, v_w_out, v_g_mlp, v_w_up, v_w_down, v_g_ple, v_w_ple_gate, v_w_ple_proj):
    w = dict(g_mix=g_mix, w_in=w_in, conv_w=conv_w, conv_b=conv_b, w_rg=w_rg, b_rg=b_rg, w_ig=w_ig, b_ig=b_ig,
             lru_lambda=lru_lambda, w_rnn_proj=w_rnn_proj, q_gain=q_gain, k_gain=k_gain, sinks=sinks,
             w_attn_proj=w_attn_proj, w_out=w_out, g_mlp=g_mlp, w_up=w_up, w_down=w_down, g_ple=g_ple,
             w_ple_gate=w_ple_gate, w_ple_proj=w_ple_proj)
    m = dict(g_mix=m_g_mix, w_in=m_w_in, conv_w=m_conv_w, conv_b=m_conv_b, w_rg=m_w_rg, b_rg=m_b_rg, w_ig=m_w_ig,
             b_ig=m_b_ig, lru_lambda=m_lru_lambda, w_rnn_proj=m_w_rnn_proj, q_gain=m_q_gain, k_gain=m_k_gain,
             sinks=m_sinks, w_attn_proj=m_w_attn_proj, w_out=m_w_out, g_mlp=m_g_mlp, w_up=m_w_up, w_down=m_w_down,
             g_ple=m_g_ple, w_ple_gate=m_w_ple_gate, w_ple_proj=m_w_ple_proj)
    v = dict(g_mix=v_g_mix, w_in=v_w_in, conv_w=v_conv_w, conv_b=v_conv_b, w_rg=v_w_rg, b_rg=v_b_rg, w_ig=v_w_ig,
             b_ig=v_b_ig, lru_lambda=v_lru_lambda, w_rnn_proj=v_w_rnn_proj, q_gain=v_q_gain, k_gain=v_k_gain,
             sinks=v_sinks, w_attn_proj=v_w_attn_proj, w_out=v_w_out, g_mlp=v_g_mlp, w_up=v_w_up, w_down=v_w_down,
             g_ple=v_g_ple, w_ple_gate=v_w_ple_gate, w_ple_proj=v_w_ple_proj)
    n_seq, S, _ = x.shape
    T = n_seq * S
    chip = 2 * lax.axis_index("x") + lax.axis_index("y")

    tm, tm_rnn = TM, TM_RNN
    xf, pf, tf = x.reshape(T, D_MODEL), p.reshape(T, PLE_DIM), loss_target.reshape(T, D_MODEL)
    first = lambda outs: [o[0] for o in outs]

    w_in_g = _gather_bf16(w["w_in"][0], "gather_w_in")
    wb = {name: w[name][0].astype(BF16) for name in _BIG if name != "w_in"}
    grp_mix, grp_mlp, grp_ple = ("w_rnn_proj", "w_attn_proj", "w_out"), ("w_up", "w_down"), ("w_ple_gate", "w_ple_proj")

    wb["conv_w"] = jnp.pad(conv_w[0], ((0, 16 - CONV_W), (0, 0)))

    cosf, sins = _rope_tables(S)
    ind_q, ind_qt = _indicator(D_MODEL)
    ind_k, ind_kt = _indicator(KV_W)
    wrg2 = _pair_blockdiag(w_rg[0]).astype(BF16)
    wig2 = _pair_blockdiag(w_ig[0]).astype(BF16)
    qg = jnp.tile(q_gain, (1, N_HEADS))
    kg = jnp.tile(k_gain, (1, N_KV))
    sk = sinks.reshape(N_HEADS)
    attn_c = (qg, kg, sk, cosf, sins, ind_q, ind_qt, ind_k, ind_kt, n_seq, S)

    (h0, xr, gr, zq, zk, zv, ga, gb), ph = _inproj_fwd(xf, g_mix, w_in_g, tm,
                                                     phases=[_ph_gather_send(wb[n]) for n in grp_mix + ("conv_w",)])
    g_small = first(ph)
    o, ph = _attn_fwd(zq, zk, zv, *attn_c,
                      phases=[_ph_gather_pass(g) for g in g_small]
                      + [_ph_gather_send(wb[n]) for n in ("w_up",) + grp_ple])
    g_small, (wu, wpg, wpp) = first(ph[:4]), first(ph[4:])
    cw_full = g_small[3][:, :CONV_W, :].transpose(1, 0, 2).reshape(CONV_W, D_MODEL)
    rnn_w = (cw_full, conv_b, wrg2, b_rg, wig2, b_ig, lru_lambda)
    (xc, h, *gates, ya), ph = _rnn_fwd(xr, gr, *rnn_w, n_seq, S, tm_rnn,
                               phases=[_ph_gather_pass(g) for g in (wu, wpg, wpp)]
                               + [_ph_gather_send(wb["w_down"])])
    (wu, wpg, wpp), wd = first(ph[:3]), ph[3][0]
    wr, wa, wo = (g.reshape(D_MODEL, D_MODEL) for g in g_small[:3])
    wpg = wpg.reshape(D_MODEL, D_MODEL)
    (x1, merged), ph = _merge_fwd(xf, ya, o, ga, gb, wr, wa, wo, tm, phases=[_ph_gather_pass(wd)])
    wd = ph[0][0].reshape(D_FF, D_MODEL)
    (x2, hm, u, act), _ = _mlp_fwd(x1, g_mlp, wu, wd, tm // 2)
    (loss_t, dx2, pb, de, hp, dtg, dg_ple), _ = _ple_loss(x2, pf, tf, g_ple, wpg, wpp, tm)

    chipmajor = lambda g: g.reshape(N_CHIPS, g.shape[-2] // N_CHIPS, g.shape[-1]) if g.ndim == 2 else g
    tmw = min(2 * tm, T)
    dw_pp = _wgrad(pb, de, "wgrad_ple_proj", False, D_MODEL, tmw)[0]
    part_ple = [chipmajor(_wgrad(hp, dtg, "wgrad_ple_gate", False, D_MODEL, tmw)[0]),
                dw_pp.reshape(PLE_DIM, N_CHIPS, D_MODEL // N_CHIPS).transpose(1, 0, 2)]
    (dx1, du, dg_mlp), ph = _mlp_bwd(dx2, u, x1, g_mlp, wu, wd, tm // 2, phases=[_ph_pair_send(g) for g in part_ple])
    send_ple, own_ple = _pair_sum(part_ple, first(ph), "pair_sum_ple")
    dw_down, ph = _wgrad(act, dx2, "wgrad_down", False, D_MODEL // 2, tmw, phases=[_ph_chip_send(s) for s in send_ple])
    red_ple = _chip_sum(own_ple, first(ph), "chip_sum_ple")
    part_mlp = [_wgrad(hm, du, "wgrad_up", True, D_MODEL, tmw)[0], chipmajor(dw_down)]
    (dga, dgb, dya, dyb, dyain, do), _ = _merge_bwd(dx1, ga, gb, ya, o, wr, wa, wo, tm)
    dw_rnn, ph_up = _wgrad(ya, dya, "wgrad_rnn_proj", False, D_MODEL, tmw, phases=[_ph_pair_send(part_mlp[0])])
    dw_attn, ph_down = _wgrad(o, dyb, "wgrad_attn_proj", False, D_MODEL, tmw, phases=[_ph_pair_send(part_mlp[1])])
    dw_out, ph = _wgrad(merged, dx1, "wgrad_out", False, D_MODEL, tmw, phases=[_ph_half_swap(r) for r in red_ple])
    red_ple = first(ph)
    send_mlp, own_mlp = _pair_sum(part_mlp, [ph_up[0][0], ph_down[0][0]], "pair_sum_mlp")
    part_mix = [chipmajor(dw_rnn), chipmajor(dw_attn), chipmajor(dw_out)]
    (dxr, dgr, vec, dwrg2, dwig2), ph = _rnn_bwd(
        dyain, xr, gr, xc, h, gates, cw_full, wrg2, wig2, lru_lambda, n_seq, S, tm_rnn,
        phases=[_ph_chip_send(s) for s in send_mlp] + [_ph_pair_send(g) for g in part_mix])
    red_mlp = _chip_sum(own_mlp, first(ph[:2]), "chip_sum_mlp")
    send_mix, own_mix = _pair_sum(part_mix, first(ph[2:]), "pair_sum_mix")
    (dq, dkc, dkp, dvc, dvp, dqg, dsk), ph = _attn_bwd(
        do, zq, zk, zv, *attn_c, phases=[_ph_half_swap(r) for r in red_mlp] + [_ph_chip_send(s) for s in send_mix])
    red_mlp = first(ph[:2])
    red_mix = _chip_sum(own_mix, first(ph[2:]), "chip_sum_mix")
    (dk, dv, dkg), _ = _kv_bwd(dkc, dkp, dvc, dvp, zk, kg, cosf, sins, ind_k, ind_kt, n_seq, S)
    dz_parts = [dxr, dgr, dq, dk, dv, dga, dgb]
    send_in, own_in = _pair_exchange_sum(*_wgrad_in(h0, dz_parts, tm), "pair_sum_in")
    (grad_x, dg_mix), ph = _inproj_bwd(dz_parts, w_in_g, xf, g_mix, dx1, tm,
                                       phases=[_ph_half_swap(r) for r in red_mix] + [_ph_chip_send(send_in)])
    red_mix = first(ph[:3])
    red_in = _chip_sum([own_in], first(ph[3:]), "chip_sum_in")
    reduced = dict(zip(grp_ple + grp_mlp + grp_mix, red_ple + red_mlp + red_mix))
    grads = {
        "g_mix": dg_mix[0], "g_mlp": dg_mlp[0], "g_ple": dg_ple[0],
        "conv_w": vec[0:CONV_W], "conv_b": vec[4], "b_rg": vec[5], "b_ig": vec[6], "lru_lambda": vec[7],
        "w_rg": _pair_blockdiag_extract(dwrg2), "w_ig": _pair_blockdiag_extract(dwig2),
        "q_gain": dqg.reshape(N_HEADS, HEAD_DIM).sum(0), "k_gain": dkg.reshape(N_KV, HEAD_DIM).sum(0),
        "sinks": dsk.sum(1),
    }

    rows = [grads["conv_w"], _pad_row(grads["conv_b"]), _pad_row(grads["b_rg"]), _pad_row(grads["b_ig"]),
            _pad_row(grads["lru_lambda"]), _pad_row(grads["g_mix"]), _pad_row(grads["g_mlp"]),
            _pad_row(grads["g_ple"]), _pad_row(grads["q_gain"]), _pad_row(grads["k_gain"]), _pad_row(grads["sinks"]),
            _pad_row(loss_t[0:1, 0:1]), jnp.zeros((1, D_MODEL), F32)]
    vecs = jnp.concatenate(rows, axis=0)
    packed = jnp.concatenate([vecs.reshape(-1, LANES), grads["w_rg"].reshape(-1, LANES),
                              grads["w_ig"].reshape(-1, LANES)], axis=0)
    red = _allreduce_small(packed, "allreduce_small")
    nv = vecs.size // LANES
    rvec = red[0:nv].reshape(16, D_MODEL)
    loss = rvec[14, 0]
    nw = grads["w_rg"].size // LANES
    sg = {
        "conv_w": lax.dynamic_slice(rvec[0:CONV_W], (0, chip * (D_MODEL // N_CHIPS)), (CONV_W, D_MODEL // N_CHIPS)),
        "conv_b": rvec[4], "b_rg": rvec[5], "b_ig": rvec[6], "lru_lambda": rvec[7], "g_mix": rvec[8],
        "g_mlp": rvec[9], "g_ple": rvec[10], "q_gain": rvec[11, :HEAD_DIM], "k_gain": rvec[12, :HEAD_DIM],
        "sinks": rvec[13, :N_HEADS], "w_rg": red[nv:nv + nw], "w_ig": red[nv + nw:nv + 2 * nw],
    }
    sg = {k: sg[k].reshape(w[k].shape) for k in _SMALL}
    d_s, m_s, v_s = _adamw_small([w[k] for k in _SMALL], [sg[k] for k in _SMALL], [m[k] for k in _SMALL],
                                 [v[k] for k in _SMALL])
    grad, delta, new_m, new_v = dict(sg), dict(zip(_SMALL, d_s)), dict(zip(_SMALL, m_s)), dict(zip(_SMALL, v_s))

    for name in ("w_ple_proj", "w_up", "w_down", "w_rnn_proj", "w_attn_proj", "w_out", "w_ple_gate", "w_in"):
        shape = w[name].shape
        outs, ph = _adamw(w[name][0], reduced[name], m[name][0], v[name][0], "adamw_" + name, min(ADAMW_ROWS, shape[1] // 2),
                          phases=[_ph_half_swap(r) for r in red_in] if name == "w_ple_proj" else ())
        if name == "w_ple_proj":
            reduced["w_in"] = ph[0][0]
        grad[name], delta[name], new_m[name], new_v[name] = (a.reshape(shape) for a in outs)

    return (loss, grad_x.reshape(x.shape), *[grad[k] for k in _WEIGHTS], *[delta[k] for k in _WEIGHTS],
            *[new_m[k] for k in _WEIGHTS], *[new_v[k] for k in _WEIGHTS])
```

```python
import functools
import math

import numpy as np
import jax
import jax.numpy as jnp
from jax import lax
from jax.experimental import pallas as pl
from jax.experimental.pallas import tpu as pltpu

F32 = jnp.float32
BF16 = jnp.bfloat16

D_MODEL = 1024
N_HEADS = 16
N_KV = 4
HEAD_DIM = 64
KV_W = N_KV * HEAD_DIM
D_FF = 4096
PLE_DIM = 256
WINDOW = 128
CONV_W = 4
LRU_C = 8.0
NORM_EPS = 1e-6
ROPE_THETA = 10000.0
N_CHIPS = 4
IN_TOTAL = 5632
IN_BLK = IN_TOTAL // N_CHIPS
IN_SEGS = (0, 1024, 2048, 3072, 3328, 3584, 4608, 5632)

ADAM_LR = 0.001
ADAM_B1 = 0.9
ADAM_B2 = 0.999
ADAM_EPS = 1e-08
ADAM_WD = 0.01
ADAM_STEP = 10

LANES = 128
V7X_VMEM_BYTES = 64 * 1024 * 1024
VMEM_LIMIT = V7X_VMEM_BYTES - 8 * 1024 * 1024
MESH_ID = pl.DeviceIdType.MESH
TM, TM_RNN, ADAMW_ROWS = 512, 256, 256
ATTN_BLOCKS_PER_STEP = 2
KV_BLOCKS_PER_STEP = 8


def _dot(a, b):
    return jnp.dot(a, b, preferred_element_type=F32)


def _dot_nt(a, b):
    return lax.dot_general(a, b, (((1,), (1,)), ((), ())), preferred_element_type=F32)


def _dot_tn(a, b):
    return lax.dot_general(a, b, (((0,), (0,)), ((), ())), preferred_element_type=F32)


def _split_dot(x, ind):
    hi = x.astype(BF16)
    lo = (x - hi.astype(F32)).astype(BF16)
    return _dot(hi, ind) + _dot(lo, ind)


def _sigmoid(x):
    return 1.0 / (1.0 + jnp.exp(-x))


_GELU_C = math.sqrt(2.0 / math.pi)


def _gelu_and_grad(g):
    inner = _GELU_C * (g + 0.044715 * g * g * g)
    t = jnp.tanh(inner)
    gelu = 0.5 * g * (1.0 + t)
    dgelu = 0.5 * (1.0 + t) + 0.5 * g * (1.0 - t * t) * _GELU_C * (1.0 + 3.0 * 0.044715 * g * g)
    return gelu, dgelu


def _const(shape):
    nd = len(shape)
    return pl.BlockSpec(shape, lambda *_: (0,) * nd)


def _params(n_grid, vmem=VMEM_LIMIT):
    return pltpu.CompilerParams(dimension_semantics=("arbitrary",) * n_grid, vmem_limit_bytes=vmem)


def _rms_fwd(x, g):
    r = lax.rsqrt(jnp.mean(x * x, axis=-1, keepdims=True) + NORM_EPS)
    return (x * r) * g, r


def _rms_bwd(dy, x, r, g):
    dn = dy * g
    dx = r * dn - x * (r * r * r * jnp.mean(dn * x, axis=-1, keepdims=True))
    dg = jnp.sum(dy * (x * r), axis=0, keepdims=True)
    return dx, dg


def _seg_pieces(blk_lo, blk_hi):
    out = []
    for s in range(7):
        lo, hi = max(blk_lo, IN_SEGS[s]), min(blk_hi, IN_SEGS[s + 1])
        if lo < hi:
            out.append((s, lo - IN_SEGS[s], hi - IN_SEGS[s], lo - blk_lo))
    return out


def _mesh_pos():
    x, y, c = lax.axis_index("x"), lax.axis_index("y"), lax.axis_index("c")
    other_chips = [(1 - x, y), (x, 1 - y), (1 - x, 1 - y)]
    return x, y, c, other_chips


def _peer_slot(k, x, y):
    dx = jnp.bitwise_xor(k // 2, x)
    dy = jnp.bitwise_xor(k % 2, y)
    return jnp.maximum(dx + 2 * dy - 1, 0)


def _half_rows(c, R):
    return pl.ds(pl.multiple_of(c * R, R), R), pl.ds(pl.multiple_of((1 - c) * R, R), R)


def _remote(src, dst, sems, to):
    return pltpu.make_async_remote_copy(src_ref=src, dst_ref=dst, send_sem=sems[0], recv_sem=sems[1],
                                        device_id=to, device_id_type=MESH_ID)


class _Phase:
    def __init__(self, ins, inout, outs, n_remote, n_local, build):
        self.ins, self.inout, self.outs = list(ins), list(inout), list(outs)
        self.n_remote, self.n_local, self.build = n_remote, n_local, build


def _ph_gather_send(wb):
    R2, C = wb.shape
    R = R2 // 2

    def build(ins, outs, rsem, lsem):
        (w_ref,), (g_ref,) = ins, outs
        x, y, c, chips = _mesh_pos()
        me = 2 * x + y
        mine, _ = _half_rows(c, R)
        loc = [pltpu.make_async_copy(w_ref, g_ref.at[me], lsem(0))]
        outg = [_remote(w_ref.at[mine], g_ref.at[me, mine], rsem(j), (cx, cy, c)) for j, (cx, cy) in enumerate(chips)]
        inc = [functools.partial(_remote, w_ref.at[mine], g_ref.at[2 * cx + cy, mine], rsem(j), (x, y, c))
               for j, (cx, cy) in enumerate(chips)]
        return loc, outg, inc

    return _Phase([wb], [], [jax.ShapeDtypeStruct((N_CHIPS, R2, C), wb.dtype)], 3, 1, build)


def _ph_gather_pass(gath):
    _, R2, C = gath.shape
    R = R2 // 2

    def build(ins, outs, rsem, lsem):
        (g_ref,) = outs
        x, y, c, chips = _mesh_pos()
        mine, theirs = _half_rows(c, R)
        outg, inc = [], []
        for j, (cx, cy) in enumerate(chips):
            blk = g_ref.at[2 * cx + cy, mine]
            outg.append(_remote(blk, blk, rsem(j), (x, y, 1 - c)))
            got = g_ref.at[2 * cx + cy, theirs]
            inc.append(functools.partial(_remote, got, got, rsem(j), (x, y, c)))
        return [], outg, inc

    return _Phase([], [gath], [], 3, 0, build)


def _ph_pair_send(partial):
    _, R2, C = partial.shape
    R = R2 // 2

    def build(ins, outs, rsem, lsem):
        (p_ref,), (s_ref,) = ins, outs
        x, y, c, _ = _mesh_pos()
        _, theirs = _half_rows(c, R)
        src = p_ref.at[:, theirs, :]
        return ([], [_remote(src, s_ref, rsem(0), (x, y, 1 - c))],
                [functools.partial(_remote, src, s_ref, rsem(0), (x, y, c))])

    return _Phase([partial], [], [jax.ShapeDtypeStruct((N_CHIPS, R, C), partial.dtype)], 1, 0, build)


def _ph_chip_send(sendb):
    def build(ins, outs, rsem, lsem):
        (s_ref,), (r_ref,) = ins, outs
        x, y, c, chips = _mesh_pos()
        outg = [_remote(s_ref.at[j], r_ref.at[j], rsem(j), (cx, cy, c)) for j, (cx, cy) in enumerate(chips)]
        inc = [functools.partial(_remote, s_ref.at[j], r_ref.at[j], rsem(j), (x, y, c)) for j in range(3)]
        return [], outg, inc

    return _Phase([sendb], [], [jax.ShapeDtypeStruct(sendb.shape, sendb.dtype)], 3, 0, build)


def _ph_half_swap(red):
    R2, C = red.shape
    R = R2 // 2

    def build(ins, outs, rsem, lsem):
        (r_ref,) = outs
        x, y, c, _ = _mesh_pos()
        mine, theirs = _half_rows(c, R)
        return ([], [_remote(r_ref.at[mine], r_ref.at[mine], rsem(0), (x, y, 1 - c))],
                [functools.partial(_remote, r_ref.at[theirs], r_ref.at[theirs], rsem(0), (x, y, c))])

    return _Phase([], [red], [], 1, 0, build)


def _call(body, *, name, grid, in_specs, out_specs, out_shape, scratch_shapes=(), phases=()):
    single = not isinstance(out_specs, (list, tuple))
    out_specs = [out_specs] if single else list(out_specs)
    out_shape = [out_shape] if single else list(out_shape)
    n_in, n_out, n_scr = len(in_specs), len(out_specs), len(scratch_shapes)
    if not phases:
        call = pl.pallas_call(body, name=name, grid=grid, in_specs=in_specs, out_specs=out_specs,
                              out_shape=out_shape, scratch_shapes=list(scratch_shapes),
                              compiler_params=_params(len(grid)))
        return lambda *operands: (list(call(*operands)), [])

    ex_in, ex_out, aliases, spans = [], [], {}, []
    for ph in phases:
        i0, o0 = len(ex_in), len(ex_out)
        ex_in += ph.ins
        for a in ph.inout:
            aliases[n_in + len(ex_in)] = n_out + len(ex_out)
            ex_in.append(a)
            ex_out.append(jax.ShapeDtypeStruct(a.shape, a.dtype))
        ex_out += ph.outs
        spans.append((i0, len(ph.ins), o0, len(ex_out) - o0))
    n_remote = sum(ph.n_remote for ph in phases)
    n_local = max(sum(ph.n_local for ph in phases), 1)

    def wrapped(*refs):
        base_in, xin = refs[:n_in], refs[n_in:n_in + len(ex_in)]
        o0 = n_in + len(ex_in)
        base_out, xout = refs[o0:o0 + n_out], refs[o0 + n_out:o0 + n_out + len(ex_out)]
        scr = refs[o0 + n_out + len(ex_out):]
        send_sems, recv_sems, loc_sems = scr[n_scr:]
        first = functools.reduce(jnp.logical_and, [pl.program_id(i) == 0 for i in range(len(grid))])
        last = functools.reduce(jnp.logical_and, [pl.program_id(i) == grid[i] - 1 for i in range(len(grid))])

        def copies():
            out, r0, l0 = [], 0, 0
            for ph, (i0, ni, p0, no) in zip(phases, spans):
                rsem = lambda k, r0=r0: (send_sems.at[r0 + k], recv_sems.at[r0 + k])
                lsem = lambda k, l0=l0: loc_sems.at[l0 + k]
                out.append(ph.build(xin[i0:i0 + ni], xout[p0:p0 + no], rsem, lsem))
                r0, l0 = r0 + ph.n_remote, l0 + ph.n_local
            return out

        @pl.when(first)
        def _():
            for loc, outg, _ in copies():
                for cp in loc + outg:
                    cp.start()

        body(*base_in, *base_out, *scr[:n_scr])

        @pl.when(last)
        def _():
            for loc, outg, inc in copies():
                for make in inc:
                    make().wait_recv()
                for cp in outg:
                    cp.wait_send()
                for cp in loc:
                    cp.wait()

    hbm = pl.BlockSpec(memory_space=pl.ANY)
    call = pl.pallas_call(
        wrapped, name=name, grid=grid, in_specs=list(in_specs) + [hbm] * len(ex_in),
        out_specs=out_specs + [hbm] * len(ex_out), out_shape=out_shape + ex_out,
        scratch_shapes=list(scratch_shapes) + [pltpu.SemaphoreType.DMA((n_remote,)), pltpu.SemaphoreType.DMA((n_remote,)),
                                              pltpu.SemaphoreType.DMA((n_local,))],
        input_output_aliases=aliases, compiler_params=_params(len(grid)))

    def run(*operands):
        res = call(*operands, *ex_in)
        extra = res[n_out:]
        return list(res[:n_out]), [list(extra[p0:p0 + no]) for (_, _, p0, no) in spans]

    return run


def _inproj_fwd(x, g_mix, w_in, tm, phases=()):
    T = x.shape[0]
    widths = [IN_SEGS[i + 1] - IN_SEGS[i] for i in range(7)]

    def body(x_ref, g_ref, w_ref, h_ref, *z_refs):
        h, _ = _rms_fwd(x_ref[...], g_ref[...])
        hb = h.astype(BF16)
        h_ref[...] = hb
        for j in range(N_CHIPS):
            zj = _dot(hb, w_ref[j])
            for s, lo, hi, off in _seg_pieces(j * IN_BLK, (j + 1) * IN_BLK):
                z_refs[s][:, lo:hi] = zj[:, off:off + hi - lo]

    return _call(
        body, phases=phases, name="inproj_fwd", grid=(T // tm,),
        in_specs=[pl.BlockSpec((tm, D_MODEL), lambda i: (i, 0)), _const((1, D_MODEL)),
                  _const((N_CHIPS, D_MODEL, IN_BLK))],
        out_specs=[pl.BlockSpec((tm, D_MODEL), lambda i: (i, 0))]
        + [pl.BlockSpec((tm, w), lambda i: (i, 0)) for w in widths],
        out_shape=[jax.ShapeDtypeStruct((T, D_MODEL), BF16)]
        + [jax.ShapeDtypeStruct((T, w), F32) for w in widths],
    )(x, g_mix, w_in)


def _inproj_bwd(dz_parts, w_in, x, g_mix, dx1, tm, phases=()):
    T = x.shape[0]
    widths = [IN_SEGS[i + 1] - IN_SEGS[i] for i in range(7)]

    def body(*refs):
        p_refs = refs[:7]
        w_ref, x_ref, g_ref, dx1_ref, gx_ref, dg_ref, dz_ref = refs[7:]

        @pl.when(pl.program_id(0) == 0)
        def _():
            dg_ref[...] = jnp.zeros_like(dg_ref)

        for s in range(7):
            dz_ref[:, IN_SEGS[s]:IN_SEGS[s + 1]] = p_refs[s][...]
        dh = jnp.zeros((tm, D_MODEL), F32)
        for j in range(N_CHIPS):
            dh = dh + _dot_nt(dz_ref[:, j * IN_BLK:(j + 1) * IN_BLK], w_ref[j])
        xv = x_ref[...]
        g = g_ref[...]
        _, r = _rms_fwd(xv, g)
        dx, dg = _rms_bwd(dh, xv, r, g)
        gx_ref[...] = dx1_ref[...] + dx
        dg_ref[...] += dg

    row = lambda w: pl.BlockSpec((tm, w), lambda i: (i, 0))
    return _call(
        body, phases=phases, name="inproj_bwd", grid=(T // tm,),
        in_specs=[row(w) for w in widths]
        + [_const((N_CHIPS, D_MODEL, IN_BLK)), row(D_MODEL), _const((1, D_MODEL)), row(D_MODEL)],
        out_specs=[row(D_MODEL), _const((1, D_MODEL))],
        out_shape=[jax.ShapeDtypeStruct((T, D_MODEL), F32), jax.ShapeDtypeStruct((1, D_MODEL), F32)],
        scratch_shapes=[pltpu.VMEM((tm, IN_TOTAL), BF16)],
    )(*dz_parts, w_in, x, g_mix, dx1)


def _wgrad_in(h0, dz_parts, tm):
    T = h0.shape[0]
    widths = [IN_SEGS[i + 1] - IN_SEGS[i] for i in range(7)]

    def body(*refs):
        h_ref, p_refs, o_ref, ob_ref, acc_ref, stage_ref, sems = (refs[0], refs[1:8], refs[8], refs[9], refs[10],
                                                                  refs[11], refs[12])
        t = pl.program_id(0)
        last = T // tm - 1

        @pl.when(t == 0)
        def _():
            acc_ref[...] = jnp.zeros_like(acc_ref)

        def accumulate(j):
            for s, lo, hi, off in _seg_pieces(j * IN_BLK, (j + 1) * IN_BLK):
                acc_ref[j, :, off:off + hi - lo] += _dot_tn(h_ref[...], p_refs[s][:, lo:hi])

        @pl.when(t < last)
        def _():
            for j in range(N_CHIPS):
                accumulate(j)

        @pl.when(t == last)
        def _():
            copies = [pltpu.make_async_copy(acc_ref.at[j], o_ref.at[j], sems.at[j]) for j in range(N_CHIPS)]
            narrow = [pltpu.make_async_copy(stage_ref.at[j % 2], ob_ref.at[j], sems.at[N_CHIPS + j])
                      for j in range(N_CHIPS)]
            for j in range(N_CHIPS):
                accumulate(j)
                copies[j].start()
                if j >= 2:
                    narrow[j - 2].wait()
                stage_ref[j % 2] = acc_ref[j].astype(BF16)
                narrow[j].start()
            for cp in copies + narrow[N_CHIPS - 2:]:
                cp.wait()

    row = lambda w: pl.BlockSpec((tm, w), lambda i: (i, 0))
    hbm = pl.BlockSpec(memory_space=pl.ANY)
    return pl.pallas_call(
        body, name="wgrad_in", grid=(T // tm,), in_specs=[row(D_MODEL)] + [row(w) for w in widths],
        out_specs=[hbm, hbm],
        out_shape=[jax.ShapeDtypeStruct((N_CHIPS, D_MODEL, IN_BLK), F32),
                   jax.ShapeDtypeStruct((N_CHIPS, D_MODEL, IN_BLK), BF16)],
        scratch_shapes=[pltpu.VMEM((N_CHIPS, D_MODEL, IN_BLK), F32), pltpu.VMEM((2, D_MODEL, IN_BLK), BF16),
                        pltpu.SemaphoreType.DMA((2 * N_CHIPS,))],
        compiler_params=_params(1),
    )(h0, *dz_parts)


def _wgrad(a, g, name, blocked, cn, tm, phases=(), narrow=False):
    T, K = a.shape
    N = g.shape[1]
    nb = N // cn

    def body(a_ref, g_ref, o_ref, *ob_ref):
        @pl.when(pl.program_id(1) == 0)
        def _():
            o_ref[...] = jnp.zeros_like(o_ref)

        o_ref[...] += _dot_tn(a_ref[...].astype(BF16), g_ref[...].astype(BF16))
        if narrow:
            @pl.when(pl.program_id(1) == T // tm - 1)
            def _():
                ob_ref[0][...] = o_ref[...].astype(BF16)

    if blocked:
        out_spec = pl.BlockSpec((None, K, cn), lambda j, t: (j, 0, 0))
        shape = (nb, K, cn)
    else:
        out_spec = pl.BlockSpec((K, cn), lambda j, t: (0, j))
        shape = (K, N)
    n_out = 2 if narrow else 1
    outs, extra = _call(
        body, phases=phases, name=name, grid=(nb, T // tm),
        in_specs=[pl.BlockSpec((tm, K), lambda j, t: (t, 0)), pl.BlockSpec((tm, cn), lambda j, t: (t, j))],
        out_specs=[out_spec] * n_out,
        out_shape=[jax.ShapeDtypeStruct(shape, F32), jax.ShapeDtypeStruct(shape, BF16)][:n_out],
    )(a, g)
    return (*outs, extra)


def _shift_down(x, prev8, sft, row, row8, tm):
    xs = pltpu.roll(x, sft, 0)
    top = jnp.where(row8 < sft, pltpu.roll(prev8, sft, 0), xs[0:8])
    return jnp.concatenate([top, xs[8:]], axis=0)


def _shift_up(x, next8, sft, row8, tm):
    xs = pltpu.roll(x, tm - sft, 0)
    bot = jnp.where(row8 >= 8 - sft, pltpu.roll(next8, 8 - sft, 0), xs[tm - 8:tm])
    return jnp.concatenate([xs[0:tm - 8], bot], axis=0)


def _conv_fwd(x, prev8, cw_ref, cb, row, row8, tm):
    xc = cb + cw_ref[CONV_W - 1:CONV_W, :] * x
    for sft in range(1, CONV_W):
        j = CONV_W - 1 - sft
        xc = xc + cw_ref[j:j + 1, :] * _shift_down(x, prev8, sft, row, row8, tm)
    return xc


def _blockdiag_dot(xb, w_ref, transpose):
    outs = []
    for b in range(D_MODEL // LANES):
        xs = xb[:, b * LANES:(b + 1) * LANES]
        outs.append(_dot_nt(xs, w_ref[b]) if transpose else _dot(xs, w_ref[b]))
    return jnp.concatenate(outs, axis=1)


def _softplus_neg(lam):
    e = jnp.exp(-jnp.abs(lam))
    u = 1.0 + e
    log1p_e = jnp.where(u == 1.0, e, jnp.log(u) * (e / (u - 1.0)))
    sp = jnp.maximum(-lam, 0.0) + log1p_e
    return sp, -_sigmoid(-lam)


def _lru_gates(xc, wrg_ref, brg, wig_ref, big, sp):
    xcb = xc.astype(BF16)
    r = _sigmoid(_blockdiag_dot(xcb, wrg_ref, False) + brg)
    i = _sigmoid(_blockdiag_dot(xcb, wig_ref, False) + big)
    log_a = (-LRU_C) * r * sp
    a = jnp.exp(log_a)
    t = jnp.tanh(log_a)
    one_m_a2 = (-2.0) * t / (1.0 - t)
    mult = jnp.sqrt(one_m_a2)
    return xcb, r, i, a, mult


def _scan_down(a, b, row, tm):
    d = 1
    while d < tm:
        if d < 8:
            keep = row >= d
            a_s = jnp.where(keep, pltpu.roll(a, d, 0), 1.0)
            b_s = jnp.where(keep, pltpu.roll(b, d, 0), 0.0)
            b = a * b_s + b
            a = a * a_s
        else:
            b = jnp.concatenate([b[:d], a[d:] * b[:-d] + b[d:]], axis=0)
            a = jnp.concatenate([a[:d], a[d:] * a[:-d]], axis=0)
        d *= 2
    return a, b


def _scan_up(c, b, row, tm):
    d = 1
    while d < tm:
        if d < 8:
            keep = row < tm - d
            c_s = jnp.where(keep, pltpu.roll(c, tm - d, 0), 1.0)
            b_s = jnp.where(keep, pltpu.roll(b, tm - d, 0), 0.0)
            b = c * b_s + b
            c = c * c_s
        else:
            b = jnp.concatenate([c[:-d] * b[d:] + b[:-d], b[-d:]], axis=0)
            c = jnp.concatenate([c[:-d] * c[d:], c[-d:]], axis=0)
        d *= 2
    return c, b


def _rnn_fwd(xr, gr, conv_w, conv_b, wrg2, b_rg, wig2, b_ig, lam, n_seq, S, tm, phases=()):
    T = xr.shape[0]
    nt = S // tm
    W = D_MODEL

    def body(xr_ref, gr_ref, cw_ref, cb_ref, wrg_ref, brg_ref, wig_ref, big_ref, lam_ref,
             xc_ref, h_ref, r_ref, i_ref, a_ref, mult_ref, ya_ref, px_ref, ph_ref):
        @pl.when(pl.program_id(1) == 0)
        def _():
            px_ref[...] = jnp.zeros_like(px_ref)
            ph_ref[...] = jnp.zeros_like(ph_ref)

        row = lax.broadcasted_iota(jnp.int32, (tm, W), 0)
        row8 = lax.broadcasted_iota(jnp.int32, (8, W), 0)
        x = xr_ref[...]
        xc = _conv_fwd(x, px_ref[...], cw_ref, cb_ref[...], row, row8, tm)
        sp, _ = _softplus_neg(lam_ref[...])
        _, r, i, a, mult = _lru_gates(xc, wrg_ref, brg_ref[...], wig_ref, big_ref[...], sp)
        r_ref[...], i_ref[...], a_ref[...], mult_ref[...] = r, i, a, mult
        bterm = mult * (i * xc)
        acum, hloc = _scan_down(a, bterm, row, tm)
        h = hloc + acum * ph_ref[7:8, :]
        h_ref[...] = h
        xc_ref[...] = xc
        gelu, _ = _gelu_and_grad(gr_ref[...])
        ya_ref[...] = (h * gelu).astype(BF16)
        px_ref[...] = xr_ref[tm - 8:tm, :]
        ph_ref[...] = h_ref[tm - 8:tm, :]

    tile = pl.BlockSpec((tm, W), lambda s, t: (s * nt + t, 0))
    return _call(
        body, phases=phases, name="rnn_fwd", grid=(n_seq, nt),
        in_specs=[tile, tile, _const((CONV_W, W)), _const((1, W)), _const((8, LANES, LANES)), _const((1, W)),
                  _const((8, LANES, LANES)), _const((1, W)), _const((1, W))],
        out_specs=[tile] * 7,
        out_shape=[jax.ShapeDtypeStruct((T, W), F32)] * 6 + [jax.ShapeDtypeStruct((T, W), BF16)],
        scratch_shapes=[pltpu.VMEM((8, W), F32), pltpu.VMEM((8, W), F32)],
    )(xr, gr, conv_w, conv_b, wrg2, b_rg, wig2, b_ig, lam)


def _rnn_bwd(dya, xr, gr, xc, h, gates, conv_w, wrg2, wig2, lam, n_seq, S, tm, phases=()):
    T = xr.shape[0]
    nt = S // tm
    W = D_MODEL
    nb8 = tm // 8

    def body(dya_ref, xr_ref, gr_ref, xc_ref, h_ref, r_ref, i_ref, a_ref, mult_ref, xprev_ref, hprev_ref, cw_ref,
             wrg_ref, wig_ref, lam_ref, dxr_ref, dgr_ref, vec_ref, dwrg_ref, dwig_ref, cg_ref, ndxc_ref, tmp_ref):
        s, ti = pl.program_id(0), pl.program_id(1)

        @pl.when((s == 0) & (ti == 0))
        def _():
            vec_ref[...] = jnp.zeros_like(vec_ref)
            dwrg_ref[...] = jnp.zeros_like(dwrg_ref)
            dwig_ref[...] = jnp.zeros_like(dwig_ref)

        @pl.when(ti == 0)
        def _():
            cg_ref[...] = jnp.zeros_like(cg_ref)
            ndxc_ref[...] = jnp.zeros_like(ndxc_ref)

        first = ti == nt - 1
        row = lax.broadcasted_iota(jnp.int32, (tm, W), 0)
        row8 = lax.broadcasted_iota(jnp.int32, (8, W), 0)
        x = xr_ref[...]
        xc = xc_ref[...]
        hv = h_ref[...]
        xprev = jnp.where(first, 0.0, xprev_ref[...])
        hprev = jnp.where(first, 0.0, hprev_ref[...])
        sp, dsp_dlam = _softplus_neg(lam_ref[...])
        xcb = xc.astype(BF16)
        r, i, a, mult = r_ref[...], i_ref[...], a_ref[...], mult_ref[...]

        gelu, dgelu = _gelu_and_grad(gr_ref[...])
        dya_v = dya_ref[...]
        dgr_ref[...] = (dya_v * hv * dgelu).astype(BF16)
        dh = dya_v * gelu
        c = jnp.where(row < tm - 1, pltpu.roll(a, tm - 1, 0), 1.0)
        ccum, gloc = _scan_up(c, dh, row, tm)
        G = gloc + ccum * cg_ref[0:1, :]
        tmp_ref[...] = a * G
        cg_ref[...] = tmp_ref[0:8, :]

        h_m1 = _shift_down(hv, hprev, 1, row, row8, tm)
        ixc = i * xc
        dixc = G * mult
        dlog_a = (G * h_m1) * a - (G * ixc) * (a * a / mult)
        dr = dlog_a * ((-LRU_C) * sp)
        di = dixc * xc
        drg = dr * r * (1.0 - r)
        dig = di * i * (1.0 - i)
        vec_ref[7:8, :] += jnp.sum(dlog_a * ((-LRU_C) * r), axis=0, keepdims=True) * dsp_dlam
        vec_ref[5:6, :] += jnp.sum(drg, axis=0, keepdims=True)
        vec_ref[6:7, :] += jnp.sum(dig, axis=0, keepdims=True)
        drgb = drg.astype(BF16)
        digb = dig.astype(BF16)
        dxc = dixc * i + _blockdiag_dot(drgb, wrg_ref, True) + _blockdiag_dot(digb, wig_ref, True)
        for b in range(W // LANES):
            sl = slice(b * LANES, (b + 1) * LANES)
            dwrg_ref[b] += _dot_tn(xcb[:, sl], drgb[:, sl])
            dwig_ref[b] += _dot_tn(xcb[:, sl], digb[:, sl])

        vec_ref[4:5, :] += jnp.sum(dxc, axis=0, keepdims=True)
        vec_ref[3:4, :] += jnp.sum(dxc * x, axis=0, keepdims=True)
        dxr = cw_ref[CONV_W - 1:CONV_W, :] * dxc
        nxt = ndxc_ref[...]
        for sft in range(1, CONV_W):
            j = CONV_W - 1 - sft
            vec_ref[j:j + 1, :] += jnp.sum(dxc * _shift_down(x, xprev, sft, row, row8, tm), axis=0, keepdims=True)
            dxr = dxr + cw_ref[j:j + 1, :] * _shift_up(dxc, nxt, sft, row8, tm)
        dxr_ref[...] = dxr.astype(BF16)
        tmp_ref[...] = dxc
        ndxc_ref[...] = tmp_ref[0:8, :]

    rev = lambda s, t: (s * nt + nt - 1 - t, 0)
    tile = pl.BlockSpec((tm, W), rev)
    prev8 = pl.BlockSpec((8, W), lambda s, t: (jnp.maximum((s * nt + nt - 1 - t) * nb8 - 1, 0), 0))
    return _call(
        body, phases=phases, name="rnn_bwd", grid=(n_seq, nt),
        in_specs=[tile] * 9 + [prev8, prev8, _const((CONV_W, W)), _const((8, LANES, LANES)),
                               _const((8, LANES, LANES)), _const((1, W))],
        out_specs=[tile, tile, _const((16, W)), _const((8, LANES, LANES)), _const((8, LANES, LANES))],
        out_shape=[jax.ShapeDtypeStruct((T, W), BF16), jax.ShapeDtypeStruct((T, W), BF16),
                   jax.ShapeDtypeStruct((16, W), F32), jax.ShapeDtypeStruct((8, LANES, LANES), F32),
                   jax.ShapeDtypeStruct((8, LANES, LANES), F32)],
        scratch_shapes=[pltpu.VMEM((8, W), F32), pltpu.VMEM((8, W), F32), pltpu.VMEM((tm, W), F32)],
    )(dya, xr, gr, xc, h, *gates, xr, h, conv_w, wrg2, wig2, lam)


def _head_swap(t, lane):
    w = t.shape[1]
    return jnp.where(lane % HEAD_DIM < HEAD_DIM // 2, pltpu.roll(t, w - HEAD_DIM // 2, 1),
                     pltpu.roll(t, HEAD_DIM // 2, 1))


def _qk_prep(t, gain, cosf, sins, ind, indt, lane):
    ms = _split_dot(t * t, ind) * (1.0 / HEAD_DIM)
    rstd = _split_dot(lax.rsqrt(ms + NORM_EPS), indt)
    tn = (t * rstd) * gain
    return tn * cosf + _head_swap(tn, lane) * sins, rstd


def _qk_prep_bwd(dy, t, rstd, gain, cosf, sins, ind, indt, lane):
    dtn = dy * cosf + _head_swap(dy * sins, lane)
    dgain = jnp.sum(dtn * (t * rstd), axis=0, keepdims=True)
    dn = dtn * gain
    m = _split_dot(_split_dot(dn * t, ind), indt) * (1.0 / HEAD_DIM)
    return rstd * dn - t * (rstd * rstd * rstd * m), dgain


def _attn_mask_t(blk_idx):
    ci = lax.broadcasted_iota(jnp.int32, (2 * WINDOW, WINDOW), 0)
    qi = lax.broadcasted_iota(jnp.int32, (2 * WINDOW, WINDOW), 1)
    diff = WINDOW + qi - ci
    return (diff >= 0) & (diff < WINDOW) & ((ci >= WINDOW) | (blk_idx > 0))


def _stack_heads(t, kvh, lo):
    parts = []
    for i in (2 * kvh, 2 * kvh + 1):
        tp = t[:, i * LANES:(i + 1) * LANES]
        parts += [jnp.where(lo, tp, 0.0), jnp.where(lo, 0.0, tp)]
    return jnp.concatenate(parts, axis=0).astype(BF16)


def _unstack_heads(ts, lo):
    w = WINDOW
    return jnp.where(lo, ts[0:w], ts[w:2 * w]), jnp.where(lo, ts[2 * w:3 * w], ts[3 * w:4 * w])


def _dup_head(t, kvh, lo2):
    m = kvh // 2
    t2 = t[:, m * LANES:(m + 1) * LANES]
    t2r = pltpu.roll(t2, HEAD_DIM, 1)
    return (jnp.where(lo2, t2, t2r) if kvh % 2 == 0 else jnp.where(lo2, t2r, t2)).astype(BF16)


def _fold_head(ts, kvh, lo2):
    tot = ts + pltpu.roll(ts, HEAD_DIM, 1)
    own = lo2 if kvh % 2 == 0 else ~lo2
    return jnp.where(own, tot, 0.0)


KEY_CHUNKS = tuple(slice(i * 64, (i + 1) * 64) for i in range(2 * WINDOW // 64))


def _fold8(x, op):
    return op(x.reshape(x.shape[0] // 8, 8, x.shape[1]), axis=0)


def _softmax_stats(s_ref, b, cols, sink):
    m8 = None
    for c in KEY_CHUNKS:
        t = _fold8(s_ref[b, c, cols], jnp.max)
        m8 = t if m8 is None else jnp.maximum(m8, t)
    mx = jnp.maximum(jnp.max(m8, axis=0, keepdims=True), sink)
    d8 = None
    for c in KEY_CHUNKS:
        t = _fold8(jnp.exp(s_ref[b, c, cols] - mx), jnp.sum)
        d8 = t if d8 is None else d8 + t
    es = jnp.exp(sink - mx)
    inv = 1.0 / (jnp.sum(d8, axis=0, keepdims=True) + es)
    return mx, inv, es * inv


def _attn_fwd(q, k, v, qg, kg, sinks, cosf, sins, ind_q, ind_qt, ind_k, ind_kt, n_seq, S, phases=()):
    T = q.shape[0]
    nblk = S // WINDOW
    W = D_MODEL

    def body(sink_ref, q_ref, k_ref, v_ref, qg_ref, kg_ref, cos_ref, sin_ref, iq_ref, iqt_ref, ik_ref, ikt_ref,
             o_ref, kc_ref, vc_ref, s_ref, p_ref, qs_ref, kd_ref, vd_ref):
        @pl.when(pl.program_id(1) == 0)
        def _():
            kc_ref[...] = jnp.zeros_like(kc_ref)
            vc_ref[...] = jnp.zeros_like(vc_ref)

        lane = lax.broadcasted_iota(jnp.int32, (WINDOW, W), 1)
        lo = lane[:, :LANES] < HEAD_DIM
        lo2 = lax.broadcasted_iota(jnp.int32, (2 * WINDOW, LANES), 1) < HEAD_DIM

        def one_block(h):
            n = pl.program_id(1) * bps + h
            rows = slice(h * WINDOW, (h + 1) * WINDOW)
            cosf, sinv = jnp.tile(cos_ref[rows, :], (1, W // LANES)), jnp.tile(sin_ref[rows, :], (1, W // LANES))
            qr, _ = _qk_prep(q_ref[rows, :], qg_ref[...], cosf, sinv, iq_ref[...], iqt_ref[...], lane)
            kr, _ = _qk_prep(k_ref[rows, :], kg_ref[...], cosf[:, :KV_W], sinv[:, :KV_W], ik_ref[...], ikt_ref[...],
                             lane[:, :KV_W])
            kc_ref[WINDOW:2 * WINDOW, :] = kr
            vc_ref[WINDOW:2 * WINDOW, :] = v_ref[rows, :]
            kc, vc = kc_ref[...], vc_ref[...]
            mask = jnp.tile(_attn_mask_t(n), (1, 4))
            qr = qr * HEAD_DIM ** -0.5
            for kvh in range(N_KV):
                qs_ref[h, kvh] = _stack_heads(qr, kvh, lo)
                kd_ref[h, kvh] = _dup_head(kc, kvh, lo2)
                vd_ref[h, kvh] = _dup_head(vc, kvh, lo2)

            def scores(kvh):
                s_ref[h, kvh % 2] = jnp.where(mask, _dot_nt(kd_ref[h, kvh], qs_ref[h, kvh]), -1e30)

            def softmax(kvh):
                sb, pb = s_ref.at[h], p_ref.at[h]
                b = kvh % 2
                for r in range(4):
                    cols = slice(r * WINDOW, (r + 1) * WINDOW)
                    mx, inv, _ = _softmax_stats(sb, b, cols, sink_ref[4 * kvh + r])
                    for c in KEY_CHUNKS:
                        pb[b, c, cols] = (jnp.exp(sb[b, c, cols] - mx) * inv).astype(BF16)

            def output(kvh):
                o0, o1 = _unstack_heads(_dot_tn(p_ref[h, kvh % 2], vd_ref[h, kvh]), lo)
                o_ref[rows, (2 * kvh) * LANES:(2 * kvh + 1) * LANES] = o0.astype(BF16)
                o_ref[rows, (2 * kvh + 1) * LANES:(2 * kvh + 2) * LANES] = o1.astype(BF16)

            scores(0)
            for kvh in range(N_KV):
                if kvh + 1 < N_KV:
                    scores(kvh + 1)
                softmax(kvh)
                output(kvh)
            kc_ref[0:WINDOW, :] = kr
            vc_ref[0:WINDOW, :] = v_ref[rows, :]

        for h in range(bps):
            one_block(h)

    bps = ATTN_BLOCKS_PER_STEP
    rows_step = bps * WINDOW
    blk = lambda w: pl.BlockSpec((rows_step, w), lambda s, n: (s * (nblk // bps) + n, 0))
    pos = pl.BlockSpec((rows_step, LANES), lambda s, n: (n, 0))
    outs, extra = _call(
        body, phases=phases, name="attn_fwd", grid=(n_seq, nblk // bps),
        in_specs=[pl.BlockSpec(memory_space=pltpu.SMEM), blk(W), blk(KV_W), blk(KV_W), _const((1, W)),
                  _const((1, KV_W)), pos, pos, _const((W, LANES)), _const((LANES, W)), _const((KV_W, LANES)),
                  _const((LANES, KV_W))],
        out_specs=blk(W), out_shape=jax.ShapeDtypeStruct((T, W), BF16),
        scratch_shapes=[pltpu.VMEM((2 * WINDOW, KV_W), F32), pltpu.VMEM((2 * WINDOW, KV_W), F32),
                        pltpu.VMEM((bps, 2, 2 * WINDOW, 4 * WINDOW), F32),
                        pltpu.VMEM((bps, 2, 2 * WINDOW, 4 * WINDOW), BF16),
                        pltpu.VMEM((bps, N_KV, 4 * WINDOW, LANES), BF16),
                        pltpu.VMEM((bps, N_KV, 2 * WINDOW, LANES), BF16),
                        pltpu.VMEM((bps, N_KV, 2 * WINDOW, LANES), BF16)],
    )(sinks, q, k, v, qg, kg, cosf, sins, ind_q, ind_qt, ind_k, ind_kt)
    return outs[0], extra


def _attn_bwd(do, q, k, v, qg, kg, sinks, cosf, sins, ind_q, ind_qt, ind_k, ind_kt, n_seq, S, phases=()):
    T = q.shape[0]
    nblk = S // WINDOW
    W = D_MODEL

    def body(sink_ref, do_ref, q_ref, k_ref, v_ref, qg_ref, kg_ref, cos_ref, sin_ref, iq_ref, iqt_ref, ik_ref,
             ikt_ref, dq_ref, dkc_ref, dkp_ref, dvc_ref, dvp_ref, dqg_ref, dsk_ref, kc_ref, vc_ref, dqr_ref,
             dk_ref, dv_ref, s_ref, dp_ref, p_ref, ds_ref, qs_ref, dos_ref, kd_ref, vd_ref):
        s_id, n_step = pl.program_id(0), pl.program_id(1)

        @pl.when((s_id == 0) & (n_step == 0))
        def _():
            dqg_ref[...] = jnp.zeros_like(dqg_ref)
            dsk_ref[...] = jnp.zeros_like(dsk_ref)

        @pl.when(n_step == 0)
        def _():
            kc_ref[...] = jnp.zeros_like(kc_ref)
            vc_ref[...] = jnp.zeros_like(vc_ref)

        lane = lax.broadcasted_iota(jnp.int32, (WINDOW, W), 1)
        lane_k = lane[:, :KV_W]
        lo = lane[:, :LANES] < HEAD_DIM
        lo2 = lax.broadcasted_iota(jnp.int32, (2 * WINDOW, LANES), 1) < HEAD_DIM
        scale = HEAD_DIM ** -0.5

        def one_block(h):
            n = n_step * bps + h
            rows = slice(h * WINDOW, (h + 1) * WINDOW)
            cosf, sinv = jnp.tile(cos_ref[rows, :], (1, W // LANES)), jnp.tile(sin_ref[rows, :], (1, W // LANES))
            qv = q_ref[rows, :]
            qr, q_rstd = _qk_prep(qv, qg_ref[...], cosf, sinv, iq_ref[...], iqt_ref[...], lane)
            kr, _ = _qk_prep(k_ref[rows, :], kg_ref[...], cosf[:, :KV_W], sinv[:, :KV_W], ik_ref[...], ikt_ref[...],
                             lane_k)
            kc_ref[WINDOW:2 * WINDOW, :] = kr
            vc_ref[WINDOW:2 * WINDOW, :] = v_ref[rows, :]
            kc, vc = kc_ref[...], vc_ref[...]
            dov = do_ref[rows, :]
            mask = jnp.tile(_attn_mask_t(n), (1, 4))
            qr = qr * scale
            dk_ref[h] = jnp.zeros((2 * WINDOW, KV_W), F32)
            dv_ref[h] = jnp.zeros((2 * WINDOW, KV_W), F32)
            for kvh in range(N_KV):
                qs_ref[h, kvh] = _stack_heads(qr, kvh, lo)
                dos_ref[h, kvh] = _stack_heads(dov, kvh, lo)
                kd_ref[h, kvh] = _dup_head(kc, kvh, lo2)
                vd_ref[h, kvh] = _dup_head(vc, kvh, lo2)
            sb, dpb, pb, dsb = s_ref.at[h], dp_ref.at[h], p_ref.at[h], ds_ref.at[h]

            def scores(kvh):
                b = kvh % 2
                sb[b] = jnp.where(mask, _dot_nt(kd_ref[h, kvh], qs_ref[h, kvh]), -1e30)
                dpb[b] = _dot_nt(vd_ref[h, kvh], dos_ref[h, kvh])

            def softmax(kvh):
                b = kvh % 2
                for r in range(4):
                    cols = slice(r * WINDOW, (r + 1) * WINDOW)
                    head = 4 * kvh + r
                    mx, inv, ps = _softmax_stats(sb, b, cols, sink_ref[head])
                    g8 = None
                    for c in KEY_CHUNKS:
                        t = _fold8(jnp.exp(sb[b, c, cols] - mx) * dpb[b, c, cols], jnp.sum)
                        g8 = t if g8 is None else g8 + t
                    dd = jnp.sum(g8, axis=0, keepdims=True) * inv
                    for c in KEY_CHUNKS:
                        p = jnp.exp(sb[b, c, cols] - mx) * inv
                        pb[b, c, cols] = p.astype(BF16)
                        dsb[b, c, cols] = (p * (dpb[b, c, cols] - dd)).astype(BF16)
                    dsk_ref[head:head + 1, :] -= ps * dd

            def grads(kvh):
                m, b = kvh // 2, kvh % 2
                dq0, dq1 = _unstack_heads(_dot_tn(dsb[b], kd_ref[h, kvh]) * scale, lo)
                dqr_ref[h, :, (2 * kvh) * LANES:(2 * kvh + 1) * LANES] = dq0
                dqr_ref[h, :, (2 * kvh + 1) * LANES:(2 * kvh + 2) * LANES] = dq1
                dk_ref[h, :, m * LANES:(m + 1) * LANES] += _fold_head(_dot(dsb[b], qs_ref[h, kvh]), kvh, lo2)
                dv_ref[h, :, m * LANES:(m + 1) * LANES] += _fold_head(_dot(pb[b], dos_ref[h, kvh]), kvh, lo2)

            scores(0)
            for kvh in range(N_KV):
                if kvh + 1 < N_KV:
                    scores(kvh + 1)
                softmax(kvh)
                grads(kvh)
            dq, dqg = _qk_prep_bwd(dqr_ref[h], qv, q_rstd, qg_ref[...], cosf, sinv, iq_ref[...], iqt_ref[...], lane)
            dq_ref[rows, :] = dq.astype(BF16)
            dqg_ref[...] += dqg
            dkp_ref[rows, :] = dk_ref[h, 0:WINDOW, :]
            dkc_ref[rows, :] = dk_ref[h, WINDOW:2 * WINDOW, :]
            dvp_ref[rows, :] = dv_ref[h, 0:WINDOW, :]
            dvc_ref[rows, :] = dv_ref[h, WINDOW:2 * WINDOW, :]
            kc_ref[0:WINDOW, :] = kr
            vc_ref[0:WINDOW, :] = v_ref[rows, :]

        for h in range(bps):
            one_block(h)

    bps = ATTN_BLOCKS_PER_STEP
    rows_step = bps * WINDOW
    blk = lambda w: pl.BlockSpec((rows_step, w), lambda s, n: (s * (nblk // bps) + n, 0))
    pos = pl.BlockSpec((rows_step, LANES), lambda s, n: (n, 0))
    kv_out = jax.ShapeDtypeStruct((T, KV_W), F32)
    stage = lambda dt: pltpu.VMEM((bps, 2, 2 * WINDOW, 4 * WINDOW), dt)
    return _call(
        body, phases=phases, name="attn_bwd", grid=(n_seq, nblk // bps),
        in_specs=[pl.BlockSpec(memory_space=pltpu.SMEM), blk(W), blk(W), blk(KV_W), blk(KV_W), _const((1, W)),
                  _const((1, KV_W)), pos, pos, _const((W, LANES)), _const((LANES, W)), _const((KV_W, LANES)),
                  _const((LANES, KV_W))],
        out_specs=[blk(W), blk(KV_W), blk(KV_W), blk(KV_W), blk(KV_W), _const((1, W)), _const((N_HEADS, LANES))],
        out_shape=[jax.ShapeDtypeStruct((T, W), BF16), kv_out, kv_out, kv_out, kv_out,
                   jax.ShapeDtypeStruct((1, W), F32), jax.ShapeDtypeStruct((N_HEADS, LANES), F32)],
        scratch_shapes=[pltpu.VMEM((2 * WINDOW, KV_W), F32), pltpu.VMEM((2 * WINDOW, KV_W), F32),
                        pltpu.VMEM((bps, WINDOW, W), F32), pltpu.VMEM((bps, 2 * WINDOW, KV_W), F32),
                        pltpu.VMEM((bps, 2 * WINDOW, KV_W), F32), stage(F32), stage(F32), stage(BF16), stage(BF16),
                        pltpu.VMEM((bps, N_KV, 4 * WINDOW, LANES), BF16),
                        pltpu.VMEM((bps, N_KV, 4 * WINDOW, LANES), BF16),
                        pltpu.VMEM((bps, N_KV, 2 * WINDOW, LANES), BF16),
                        pltpu.VMEM((bps, N_KV, 2 * WINDOW, LANES), BF16)],
    )(sinks, do, q, k, v, qg, kg, cosf, sins, ind_q, ind_qt, ind_k, ind_kt)


def _kv_bwd(dkc, dkp, dvc, dvp, k, kg, cosf, sins, ind_k, ind_kt, n_seq, S, phases=()):
    T = k.shape[0]
    nblk = S // WINDOW
    nb = min(KV_BLOCKS_PER_STEP, nblk)
    rows, nt = nb * WINDOW, nblk // nb

    def body(dkc_ref, dkp_ref, dkn_ref, dvc_ref, dvp_ref, dvn_ref, k_ref, kg_ref, cos_ref, sin_ref, ik_ref, ikt_ref,
             dk_ref, dv_ref, dkg_ref):
        s_id, n = pl.program_id(0), pl.program_id(1)

        @pl.when((s_id == 0) & (n == 0))
        def _():
            dkg_ref[...] = jnp.zeros_like(dkg_ref)

        blk = n * nb + lax.broadcasted_iota(jnp.int32, (rows, KV_W), 0) // WINDOW
        has_next = blk < nblk - 1

        def from_next(part_ref, next_ref):
            moved = jnp.concatenate([part_ref[WINDOW:rows, :], next_ref[...]], axis=0) if nb > 1 else next_ref[...]
            return jnp.where(has_next, moved, 0.0)

        lane = lax.broadcasted_iota(jnp.int32, (rows, KV_W), 1)
        dkr = dkc_ref[...] + from_next(dkp_ref, dkn_ref)
        dv_ref[...] = (dvc_ref[...] + from_next(dvp_ref, dvn_ref)).astype(BF16)
        cosf, sinv = jnp.tile(cos_ref[...], (1, KV_W // LANES)), jnp.tile(sin_ref[...], (1, KV_W // LANES))
        kv = k_ref[...]
        _, rstd = _qk_prep(kv, kg_ref[...], cosf, sinv, ik_ref[...], ikt_ref[...], lane)
        dk, dkg = _qk_prep_bwd(dkr, kv, rstd, kg_ref[...], cosf, sinv, ik_ref[...], ikt_ref[...], lane)
        dk_ref[...] = dk.astype(BF16)
        dkg_ref[...] += dkg

    cur = pl.BlockSpec((rows, KV_W), lambda s, n: (s * nt + n, 0))
    nxt = pl.BlockSpec((WINDOW, KV_W), lambda s, n: (s * nblk + jnp.minimum((n + 1) * nb, nblk - 1), 0))
    pos = pl.BlockSpec((rows, LANES), lambda s, n: (n, 0))
    return _call(
        body, phases=phases, name="kv_bwd", grid=(n_seq, nt),
        in_specs=[cur, cur, nxt, cur, cur, nxt, cur, _const((1, KV_W)), pos, pos, _const((KV_W, LANES)),
                  _const((LANES, KV_W))],
        out_specs=[cur, cur, _const((1, KV_W))],
        out_shape=[jax.ShapeDtypeStruct((T, KV_W), BF16), jax.ShapeDtypeStruct((T, KV_W), BF16),
                   jax.ShapeDtypeStruct((1, KV_W), F32)],
    )(dkc, dkp, dkp, dvc, dvp, dvp, k, kg, cosf, sins, ind_k, ind_kt)


def _merge_fwd(x, ya, o, ga, gb, w_rnn, w_attn, w_out, tm, phases=()):
    T = x.shape[0]
    W = D_MODEL

    def body(x_ref, ya_ref, o_ref, ga_ref, gb_ref, wr_ref, wa_ref, wo_ref, x1_ref, mg_ref):
        y_a = _dot(ya_ref[...], wr_ref[...])
        y_b = _dot(o_ref[...], wa_ref[...])
        mg = (_sigmoid(ga_ref[...]) * y_a + _sigmoid(gb_ref[...]) * y_b).astype(BF16)
        mg_ref[...] = mg
        x1_ref[...] = x_ref[...] + _dot(mg, wo_ref[...])

    row = pl.BlockSpec((tm, W), lambda i: (i, 0))
    sq = _const((W, W))
    return _call(
        body, phases=phases, name="merge_fwd", grid=(T // tm,),
        in_specs=[row, row, row, row, row, sq, sq, sq], out_specs=[row, row],
        out_shape=[jax.ShapeDtypeStruct((T, W), F32), jax.ShapeDtypeStruct((T, W), BF16)],
    )(x, ya, o, ga, gb, w_rnn, w_attn, w_out)


def _merge_bwd(dx1, ga, gb, ya, o, w_rnn, w_attn, w_out, tm, phases=()):
    T = dx1.shape[0]
    W = D_MODEL

    def body(dx1_ref, ga_ref, gb_ref, ya_ref, o_ref, wr_ref, wa_ref, wo_ref,
             dga_ref, dgb_ref, dya_ref, dyb_ref, dyain_ref, do_ref):
        dm = _dot_nt(dx1_ref[...].astype(BF16), wo_ref[...])
        sa = _sigmoid(ga_ref[...])
        sb = _sigmoid(gb_ref[...])
        dga_ref[...] = (dm * _dot(ya_ref[...], wr_ref[...]) * (sa * (1.0 - sa))).astype(BF16)
        dgb_ref[...] = (dm * _dot(o_ref[...], wa_ref[...]) * (sb * (1.0 - sb))).astype(BF16)
        dya = (dm * sa).astype(BF16)
        dyb = (dm * sb).astype(BF16)
        dya_ref[...] = dya
        dyb_ref[...] = dyb
        dyain_ref[...] = _dot_nt(dya, wr_ref[...])
        do_ref[...] = _dot_nt(dyb, wa_ref[...])

    row = pl.BlockSpec((tm, W), lambda i: (i, 0))
    sq = _const((W, W))
    b16 = jax.ShapeDtypeStruct((T, W), BF16)
    f32 = jax.ShapeDtypeStruct((T, W), F32)
    return _call(
        body, phases=phases, name="merge_bwd", grid=(T // tm,),
        in_specs=[row, row, row, row, row, sq, sq, sq], out_specs=[row] * 6,
        out_shape=[b16, b16, b16, b16, f32, f32],
    )(dx1, ga, gb, ya, o, w_rnn, w_attn, w_out)


def _mlp_fwd(x1, g_mlp, w_up, w_down, tm, phases=()):
    T = x1.shape[0]
    W = D_MODEL

    def body(x_ref, g_ref, wu_ref, wd_ref, x2_ref, hm_ref, u_ref, act_ref):
        xv = x_ref[...]
        hm, _ = _rms_fwd(xv, g_ref[...])
        hmb = hm.astype(BF16)
        hm_ref[...] = hmb
        for j in range(N_CHIPS):
            u = _dot(hmb, wu_ref[j])
            u_ref[:, j * W:(j + 1) * W] = u
            ru = jnp.maximum(u, 0.0)
            act_ref[:, j * W:(j + 1) * W] = (ru * ru).astype(BF16)
        x2_ref[...] = xv + _dot(act_ref[...], wd_ref[...])

    row = lambda w: pl.BlockSpec((tm, w), lambda i: (i, 0))
    return _call(
        body, phases=phases, name="mlp_fwd", grid=(T // tm,),
        in_specs=[row(W), _const((1, W)), _const((N_CHIPS, W, W)), _const((D_FF, W))],
        out_specs=[row(W), row(W), row(D_FF), row(D_FF)],
        out_shape=[jax.ShapeDtypeStruct((T, W), F32), jax.ShapeDtypeStruct((T, W), BF16),
                   jax.ShapeDtypeStruct((T, D_FF), F32), jax.ShapeDtypeStruct((T, D_FF), BF16)],
    )(x1, g_mlp, w_up, w_down)


def _mlp_bwd(dx2, u, x1, g_mlp, w_up, w_down, tm, phases=()):
    T = x1.shape[0]
    W = D_MODEL

    def body(dx2_ref, u_ref, x_ref, g_ref, wu_ref, wd_ref, dx1_ref, du_ref, dg_ref):
        @pl.when(pl.program_id(0) == 0)
        def _():
            dg_ref[...] = jnp.zeros_like(dg_ref)

        dx2 = dx2_ref[...]
        dact = _dot_nt(dx2.astype(BF16), wd_ref[...])
        du_ref[...] = (dact * (2.0 * jnp.maximum(u_ref[...], 0.0))).astype(BF16)
        dhm = jnp.zeros((tm, W), F32)
        for j in range(N_CHIPS):
            dhm = dhm + _dot_nt(du_ref[:, j * W:(j + 1) * W], wu_ref[j])
        xv = x_ref[...]
        g = g_ref[...]
        _, r = _rms_fwd(xv, g)
        dx, dg = _rms_bwd(dhm, xv, r, g)
        dx1_ref[...] = dx2 + dx
        dg_ref[...] += dg

    row = lambda w: pl.BlockSpec((tm, w), lambda i: (i, 0))
    return _call(
        body, phases=phases, name="mlp_bwd", grid=(T // tm,),
        in_specs=[row(W), row(D_FF), row(W), _const((1, W)), _const((N_CHIPS, W, W)), _const((D_FF, W))],
        out_specs=[row(W), row(D_FF), _const((1, W))],
        out_shape=[jax.ShapeDtypeStruct((T, W), F32), jax.ShapeDtypeStruct((T, D_FF), BF16),
                   jax.ShapeDtypeStruct((1, W), F32)],
    )(dx2, u, x1, g_mlp, w_up, w_down)


def _ple_loss(x2, p, target, g_ple, w_gate, w_proj, tm, phases=()):
    T = x2.shape[0]
    W = D_MODEL
    cw = W // N_CHIPS

    def body(x_ref, p_ref, t_ref, g_ref, wg_ref, wp_ref, loss_ref, dx2_ref, pb_ref, de_ref, hp_ref, dtg_ref, dg_ref):
        @pl.when(pl.program_id(0) == 0)
        def _():
            dg_ref[...] = jnp.zeros_like(dg_ref)
            loss_ref[...] = jnp.zeros_like(loss_ref)

        xv = x_ref[...]
        g = g_ref[...]
        pb = p_ref[...].astype(BF16)
        pb_ref[...] = pb
        e = jnp.concatenate([_dot(pb, wp_ref[j]) for j in range(N_CHIPS)], axis=1)
        hp, r = _rms_fwd(xv, g)
        hpb = hp.astype(BF16)
        hp_ref[...] = hpb
        sg = _sigmoid(_dot(hpb, wg_ref[...]))
        diff = (xv + e * sg) - t_ref[...]
        loss_ref[...] += jnp.sum(diff * diff) * (0.5 / W)
        dx3 = diff * (1.0 / W)
        de_ref[...] = (dx3 * sg).astype(BF16)
        dtg = (dx3 * e * (sg * (1.0 - sg))).astype(BF16)
        dtg_ref[...] = dtg
        dx, dg = _rms_bwd(_dot_nt(dtg, wg_ref[...]), xv, r, g)
        dx2_ref[...] = dx3 + dx
        dg_ref[...] += dg

    row = lambda w: pl.BlockSpec((tm, w), lambda i: (i, 0))
    b16 = lambda w: jax.ShapeDtypeStruct((T, w), BF16)
    return _call(
        body, phases=phases, name="ple_loss", grid=(T // tm,),
        in_specs=[row(W), row(PLE_DIM), row(W), _const((1, W)), _const((W, W)), _const((N_CHIPS, PLE_DIM, cw))],
        out_specs=[_const((8, LANES)), row(W), row(PLE_DIM), row(W), row(W), row(W), _const((1, W))],
        out_shape=[jax.ShapeDtypeStruct((8, LANES), F32), jax.ShapeDtypeStruct((T, W), F32), b16(PLE_DIM),
                   b16(W), b16(W), b16(W), jax.ShapeDtypeStruct((1, W), F32)],
    )(x2, p, target, g_ple, w_gate, w_proj)


def _adamw(w, g, m, v, name, tr, phases=()):
    R, C = w.shape
    c1 = 1.0 / (1.0 - ADAM_B1 ** ADAM_STEP)
    c2 = 1.0 / (1.0 - ADAM_B2 ** ADAM_STEP)

    def body(w_ref, g_ref, m_ref, v_ref, go_ref, d_ref, nm_ref, nv_ref):
        gv = g_ref[...]
        go_ref[...] = gv
        nm = ADAM_B1 * m_ref[...] + (1.0 - ADAM_B1) * gv
        nv = ADAM_B2 * v_ref[...] + (1.0 - ADAM_B2) * (gv * gv)
        nm_ref[...] = nm
        nv_ref[...] = nv
        d_ref[...] = (-ADAM_LR) * ((nm * c1) / (jnp.sqrt(nv * c2) + ADAM_EPS) + ADAM_WD * w_ref[...])

    row = pl.BlockSpec((tr, C), lambda i: (i, 0))
    sds = jax.ShapeDtypeStruct((R, C), F32)
    return _call(
        body, phases=phases, name=name, grid=(R // tr,), in_specs=[row] * 4, out_specs=[row] * 4,
        out_shape=[sds] * 4,
    )(w, g, m, v)


def _indicator(width):
    ind = np.zeros((width, LANES), np.float32)
    ind[np.arange(width), np.arange(width) // HEAD_DIM] = 1.0
    return jnp.asarray(ind, BF16), jnp.asarray(ind.T, BF16)


def _rope_tables(S):
    inv = ROPE_THETA ** (-jnp.arange(0, HEAD_DIM, 2, dtype=F32) / HEAD_DIM)
    ang = jnp.arange(S, dtype=F32)[:, None] * inv[None, :]
    cos, sin = jnp.cos(ang), jnp.sin(ang)
    cosf = jnp.tile(jnp.concatenate([cos, cos], axis=1), (1, LANES // HEAD_DIM))
    sins = jnp.tile(jnp.concatenate([-sin, sin], axis=1), (1, LANES // HEAD_DIM))
    return cosf, sins


def _pair_blockdiag(w):
    w4 = w.reshape(8, 2, HEAD_DIM, HEAD_DIM)
    eye = jnp.eye(2, dtype=w.dtype)
    return jnp.einsum("bpij,pq->bpiqj", w4, eye).reshape(8, LANES, LANES)


def _pair_blockdiag_extract(g):
    g5 = g.reshape(8, 2, HEAD_DIM, 2, HEAD_DIM)
    return jnp.stack([g5[:, 0, :, 0, :], g5[:, 1, :, 1, :]], axis=1).reshape(16, HEAD_DIM, HEAD_DIM)


def _pair_sum(parts, sibs, name):
    n = len(parts)
    dims = [(p.shape[1] // 2, p.shape[2]) for p in parts]

    def body(*refs):
        p_r, s_r, send_r, own_r, mine_r, sem = (refs[0:n], refs[n:2 * n], refs[2 * n:3 * n], refs[3 * n:4 * n],
                                                refs[4 * n:5 * n], refs[5 * n])
        x, y, c, chips = _mesh_pos()
        me = 2 * x + y
        loads = []
        for i, (R, _) in enumerate(dims):
            mine, _ = _half_rows(c, R)
            cp = pltpu.make_async_copy(p_r[i].at[:, mine, :], mine_r[i], sem.at[i])
            cp.start()
            loads.append(cp)
        for i in range(n):
            loads[i].wait()
            for j, (cx, cy) in enumerate(chips):
                k = 2 * cx + cy
                send_r[i][j] = (mine_r[i][k] + s_r[i][k].astype(F32)).astype(BF16)
            own_r[i][...] = mine_r[i][me] + s_r[i][me].astype(F32)

    vm = pl.BlockSpec(memory_space=pltpu.VMEM)
    out = pl.pallas_call(
        body, name=name, in_specs=[pl.BlockSpec(memory_space=pl.ANY)] * n + [vm] * n, out_specs=[vm] * (2 * n),
        out_shape=[jax.ShapeDtypeStruct((3, R, C), BF16) for R, C in dims]
        + [jax.ShapeDtypeStruct((R, C), F32) for R, C in dims],
        scratch_shapes=[pltpu.VMEM((N_CHIPS, R, C), F32) for R, C in dims] + [pltpu.SemaphoreType.DMA((n,))],
        compiler_params=pltpu.CompilerParams(vmem_limit_bytes=VMEM_LIMIT),
    )(*parts, *sibs)
    return out[:n], out[n:]


def _chip_sum(owns, recvs, name):
    n = len(owns)
    dims = [o.shape for o in owns]

    def body(*refs):
        own_r, recv_r, red_r, stage_r, sem = refs[0:n], refs[n:2 * n], refs[2 * n:3 * n], refs[3 * n:4 * n], refs[4 * n]
        x, y, c, _ = _mesh_pos()
        me = 2 * x + y
        stores = []
        for i, (R, _) in enumerate(dims):
            for k_me in range(N_CHIPS):

                @pl.when(me == k_me)
                def _():
                    acc = None
                    for k in range(N_CHIPS):
                        slot = ((k // 2) ^ (k_me // 2)) + 2 * ((k % 2) ^ (k_me % 2)) - 1
                        term = own_r[i][...] if k == k_me else recv_r[i][slot].astype(F32)
                        acc = term if acc is None else acc + term
                    stage_r[i][...] = acc

            mine, _ = _half_rows(c, R)
            cp = pltpu.make_async_copy(stage_r[i], red_r[i].at[mine, :], sem.at[i])
            cp.start()
            stores.append(cp)
        for cp in stores:
            cp.wait()

    vm = pl.BlockSpec(memory_space=pltpu.VMEM)
    return pl.pallas_call(
        body, name=name, in_specs=[vm] * (2 * n), out_specs=[pl.BlockSpec(memory_space=pl.ANY)] * n,
        out_shape=[jax.ShapeDtypeStruct((2 * R, C), F32) for R, C in dims],
        scratch_shapes=[pltpu.VMEM((R, C), F32) for R, C in dims] + [pltpu.SemaphoreType.DMA((n,))],
        compiler_params=pltpu.CompilerParams(vmem_limit_bytes=VMEM_LIMIT),
    )(*owns, *recvs)


def _gather_bf16(shard, name):
    R2, C = shard.shape
    R = R2 // 2
    H = R // 2

    def body(s_ref, o_ref, send_sems, recv_sems):
        x, y, c, _ = _mesh_pos()
        me, chip_x, chip_y, chip_d = 2 * x + y, 2 * (1 - x) + y, 2 * x + (1 - y), 2 * (1 - x) + (1 - y)
        to_x, to_y, me_dev, sibling = (1 - x, y, c), (x, 1 - y, c), (x, y, c), (x, y, 1 - c)

        def rows(core, off, n):
            return pl.ds(pl.multiple_of(core * R + off, H), n)

        def copy(k, chip, rws, to):
            blk = o_ref.at[chip, rws]
            return _remote(blk, blk, (send_sems.at[k], recv_sems.at[k]), to)

        piece, half_a, half_b = rows(c, 0, R), rows(c, 0, H), rows(c, H, H)
        o_ref[me] = s_ref[...].astype(BF16)
        sends = [copy(0, me, piece, to_x), copy(1, me, piece, to_y)]
        for cp in sends:
            cp.start()
        arrivals = [(0, chip_x, piece, (2, half_a, to_y)), (1, chip_y, piece, (3, half_b, to_x)),
                    (2, chip_d, half_a, None), (3, chip_d, half_b, None)]
        for k, chip, rws, onward in arrivals:
            copy(k, chip, rws, me_dev).wait_recv()
            if onward is not None:
                sends.append(copy(onward[0], chip, onward[1], onward[2]))
                sends[-1].start()
            sends.append(copy(4 + k, chip, rws, sibling))
            sends[-1].start()
        for k, chip, rws in [(4, chip_x, rows(1 - c, 0, R)), (5, chip_y, rows(1 - c, 0, R)),
                             (6, chip_d, rows(1 - c, 0, H)), (7, chip_d, rows(1 - c, H, H))]:
            copy(k, chip, rws, me_dev).wait_recv()
        for cp in sends:
            cp.wait_send()

    return pl.pallas_call(
        body, name=name, out_shape=jax.ShapeDtypeStruct((N_CHIPS, R2, C), BF16),
        in_specs=[pl.BlockSpec(memory_space=pltpu.VMEM)], out_specs=pl.BlockSpec(memory_space=pltpu.VMEM),
        scratch_shapes=[pltpu.SemaphoreType.DMA((8,)), pltpu.SemaphoreType.DMA((8,))],
        compiler_params=pltpu.CompilerParams(vmem_limit_bytes=VMEM_LIMIT),
    )(shard)


def _pair_exchange_sum(partial, partial_b, name):
    _, R2, C = partial.shape
    R = R2 // 2

    def body(p_ref, pb_ref, send_ref, own_ref, mine_ref, sib_ref, loc_sems, send_sems, recv_sems):
        x, y, c, chips = _mesh_pos()
        me = 2 * x + y
        mine, theirs = _half_rows(c, R)
        order = [2 * cx + cy for cx, cy in chips] + [me]
        locs, pairs = [], []
        for i, k in enumerate(order):
            loc = pltpu.make_async_copy(p_ref.at[k, mine, :], mine_ref.at[i], loc_sems.at[i])
            pair = _remote(pb_ref.at[k, theirs, :], sib_ref.at[i], (send_sems.at[i], recv_sems.at[i]), (x, y, 1 - c))
            loc.start()
            pair.start()
            locs.append(loc)
            pairs.append(pair)
        for i in range(N_CHIPS):
            locs[i].wait()
            pairs[i].wait_recv()
            total = mine_ref[i] + sib_ref[i].astype(F32)
            if i < 3:
                send_ref[i] = total.astype(BF16)
            else:
                own_ref[...] = total
        for pair in pairs:
            pair.wait_send()

    vm = pl.BlockSpec(memory_space=pltpu.VMEM)
    return pl.pallas_call(
        body, name=name, in_specs=[pl.BlockSpec(memory_space=pl.ANY)] * 2, out_specs=[vm, vm],
        out_shape=[jax.ShapeDtypeStruct((3, R, C), BF16), jax.ShapeDtypeStruct((R, C), F32)],
        scratch_shapes=[pltpu.VMEM((N_CHIPS, R, C), F32), pltpu.VMEM((N_CHIPS, R, C), BF16),
                        pltpu.SemaphoreType.DMA((N_CHIPS,)), pltpu.SemaphoreType.DMA((N_CHIPS,)),
                        pltpu.SemaphoreType.DMA((N_CHIPS,))],
        compiler_params=pltpu.CompilerParams(vmem_limit_bytes=VMEM_LIMIT),
    )(partial, partial_b)


def _allreduce_small(buf, name):
    rows, width = buf.shape
    h = rows // 2

    def body(b_ref, o_ref, sib_ref, pair_ref, in_ref, pair_sems, send_sems, recv_sems, fin_sems):
        x, y, c, chips = _mesh_pos()
        me = 2 * x + y
        mine, theirs = _half_rows(c, h)
        sibling = (x, y, 1 - c)
        pair = _remote(b_ref.at[theirs], sib_ref, (pair_sems.at[0], pair_sems.at[1]), sibling)
        pair.start()
        pair.wait()
        pair_ref[...] = b_ref[mine, :] + sib_ref[...]
        sends = []
        for j, (cx, cy) in enumerate(chips):
            cp = _remote(pair_ref, in_ref.at[j], (send_sems.at[j], recv_sems.at[j]), (cx, cy, c))
            cp.start()
            sends.append(cp)
        for cp in sends:
            cp.wait_recv()
        acc = None
        for k in range(N_CHIPS):
            term = jnp.where(me == k, pair_ref[...], in_ref[_peer_slot(k, x, y)])
            acc = term if acc is None else acc + term
        o_ref[mine, :] = acc
        fin = _remote(o_ref.at[mine], o_ref.at[mine], (fin_sems.at[0], fin_sems.at[1]), sibling)
        fin.start()
        fin.wait_send()
        _remote(o_ref.at[theirs], o_ref.at[theirs], (fin_sems.at[0], fin_sems.at[1]), sibling).wait_recv()
        for cp in sends:
            cp.wait_send()

    return pl.pallas_call(
        body, name=name, out_shape=jax.ShapeDtypeStruct((rows, width), F32),
        in_specs=[pl.BlockSpec(memory_space=pltpu.VMEM)], out_specs=pl.BlockSpec(memory_space=pltpu.VMEM),
        scratch_shapes=[pltpu.VMEM((h, width), F32), pltpu.VMEM((h, width), F32), pltpu.VMEM((3, h, width), F32),
                        pltpu.SemaphoreType.DMA((2,)), pltpu.SemaphoreType.DMA((3,)), pltpu.SemaphoreType.DMA((3,)),
                        pltpu.SemaphoreType.DMA((2,))],
        compiler_params=pltpu.CompilerParams(vmem_limit_bytes=VMEM_LIMIT),
    )(buf)


def _adamw_small(ws, gs, ms, vs):
    n = len(ws)
    c1 = 1.0 / (1.0 - ADAM_B1 ** ADAM_STEP)
    c2 = 1.0 / (1.0 - ADAM_B2 ** ADAM_STEP)

    def body(*refs):
        w_r, g_r, m_r, v_r = refs[0:n], refs[n:2 * n], refs[2 * n:3 * n], refs[3 * n:4 * n]
        d_r, nm_r, nv_r = refs[4 * n:5 * n], refs[5 * n:6 * n], refs[6 * n:7 * n]
        for i in range(n):
            gv = g_r[i][...]
            nm = ADAM_B1 * m_r[i][...] + (1.0 - ADAM_B1) * gv
            nv = ADAM_B2 * v_r[i][...] + (1.0 - ADAM_B2) * (gv * gv)
            nm_r[i][...] = nm
            nv_r[i][...] = nv
            d_r[i][...] = (-ADAM_LR) * ((nm * c1) / (jnp.sqrt(nv * c2) + ADAM_EPS) + ADAM_WD * w_r[i][...])

    vm = pl.BlockSpec(memory_space=pltpu.VMEM)
    sds = [jax.ShapeDtypeStruct(w.shape, F32) for w in ws]
    out = pl.pallas_call(body, name="adamw_small", in_specs=[vm] * (4 * n), out_specs=[vm] * (3 * n),
                         out_shape=sds * 3)(*ws, *gs, *ms, *vs)
    return out[0:n], out[n:2 * n], out[2 * n:3 * n]


_BIG = ("w_in", "w_rnn_proj", "w_attn_proj", "w_out", "w_up", "w_down", "w_ple_gate", "w_ple_proj")
_SMALL = ("g_mix", "conv_w", "conv_b", "w_rg", "b_rg", "w_ig", "b_ig", "lru_lambda", "q_gain", "k_gain", "sinks",
          "g_mlp", "g_ple")
_WEIGHTS = ("g_mix", "w_in", "conv_w", "conv_b", "w_rg", "b_rg", "w_ig", "b_ig", "lru_lambda", "w_rnn_proj",
            "q_gain", "k_gain", "sinks", "w_attn_proj", "w_out", "g_mlp", "w_up", "w_down", "g_ple", "w_ple_gate",
            "w_ple_proj")


def _pad_row(v):
    v = v.reshape(1, -1)
    return jnp.pad(v, ((0, 0), (0, D_MODEL - v.shape[1])))


def kernel(x, p, g_mix, w_in, conv_w, conv_b, w_rg, b_rg, w_ig, b_ig, lru_lambda, w_rnn_proj, q_gain, k_gain, sinks, w_attn_proj, w_out, g_mlp, w_up, w_down, g_ple, w_ple_gate, w_ple_proj, loss_target, m_g_mix, m_w_in, m_conv_w, m_conv_b, m_w_rg, m_b_rg, m_w_ig, m_b_ig, m_lru_lambda, m_w_rnn_proj, m_q_gain, m_k_gain, m_sinks, m_w_attn_proj, m_w_out, m_g_mlp, m_w_up, m_w_down, m_g_ple, m_w_ple_gate, m_w_ple_proj, v_g_mix, v_w_in, v_conv_w, v_conv_b, v_w_rg, v_b_rg, v_w_ig, v_b_ig, v_lru_lambda, v_w_rnn_proj, v_q_gain, v_k_gain, v_sinks, v_w_attn_proj, v_w_out, v_g_mlp, v_w_up, v_w_down, v_g_ple, v_w_ple_gate, v_w_ple_proj):
    w = dict(g_mix=g_mix, w_in=w_in, conv_w=conv_w, conv_b=conv_b, w_rg=w_rg, b_rg=b_rg, w_ig=w_ig, b_ig=b_ig,
             lru_lambda=lru_lambda, w_rnn_proj=w_rnn_proj, q_gain=q_gain, k_gain=k_gain, sinks=sinks,
             w_attn_proj=w_attn_proj, w_out=w_out, g_mlp=g_mlp, w_up=w_up, w_down=w_down, g_ple=g_ple,
             w_ple_gate=w_ple_gate, w_ple_proj=w_ple_proj)
    m = dict(g_mix=m_g_mix, w_in=m_w_in, conv_w=m_conv_w, conv_b=m_conv_b, w_rg=m_w_rg, b_rg=m_b_rg, w_ig=m_w_ig,
             b_ig=m_b_ig, lru_lambda=m_lru_lambda, w_rnn_proj=m_w_rnn_proj, q_gain=m_q_gain, k_gain=m_k_gain,
             sinks=m_sinks, w_attn_proj=m_w_attn_proj, w_out=m_w_out, g_mlp=m_g_mlp, w_up=m_w_up, w_down=m_w_down,
             g_ple=m_g_ple, w_ple_gate=m_w_ple_gate, w_ple_proj=m_w_ple_proj)
    v = dict(g_mix=v_g_mix, w_in=v_w_in, conv_w=v_conv_w, conv_b=v_conv_b, w_rg=v_w_rg, b_rg=v_b_rg, w_ig=v_w_ig,
             b_ig=v_b_ig, lru_lambda=v_lru_lambda, w_rnn_proj=v_w_rnn_proj, q_gain=v_q_gain, k_gain=v_k_gain,
             sinks=v_sinks, w_attn_proj=v_w_attn_proj, w_out=v_w_out, g_mlp=v_g_mlp, w_up=v_w_up, w_down=v_w_down,
             g_ple=v_g_ple, w_ple_gate=v_w_ple_gate, w_ple_proj=v_w_ple_proj)
    n_seq, S, _ = x.shape
    T = n_seq * S
    chip = 2 * lax.axis_index("x") + lax.axis_index("y")

    tm, tm_rnn = TM, TM_RNN
    xf, pf, tf = x.reshape(T, D_MODEL), p.reshape(T, PLE_DIM), loss_target.reshape(T, D_MODEL)
    first = lambda outs: [o[0] for o in outs]

    w_in_g = _gather_bf16(w["w_in"][0], "gather_w_in")
    wb = {name: w[name][0].astype(BF16) for name in _BIG if name != "w_in"}
    grp_mix, grp_mlp, grp_ple = ("w_rnn_proj", "w_attn_proj", "w_out"), ("w_up", "w_down"), ("w_ple_gate", "w_ple_proj")

    wb["conv_w"] = jnp.pad(conv_w[0], ((0, 16 - CONV_W), (0, 0)))

    cosf, sins = _rope_tables(S)
    ind_q, ind_qt = _indicator(D_MODEL)
    ind_k, ind_kt = _indicator(KV_W)
    wrg2 = _pair_blockdiag(w_rg[0]).astype(BF16)
    wig2 = _pair_blockdiag(w_ig[0]).astype(BF16)
    qg = jnp.tile(q_gain, (1, N_HEADS))
    kg = jnp.tile(k_gain, (1, N_KV))
    sk = sinks.reshape(N_HEADS)
    attn_c = (qg, kg, sk, cosf, sins, ind_q, ind_qt, ind_k, ind_kt, n_seq, S)

    (h0, xr, gr, zq, zk, zv, ga, gb), ph = _inproj_fwd(xf, g_mix, w_in_g, tm,
                                                     phases=[_ph_gather_send(wb[n]) for n in grp_mix + ("conv_w",)])
    g_small = first(ph)
    o, ph = _attn_fwd(zq, zk, zv, *attn_c,
                      phases=[_ph_gather_pass(g) for g in g_small]
                      + [_ph_gather_send(wb[n]) for n in ("w_up",) + grp_ple])
    g_small, (wu, wpg, wpp) = first(ph[:4]), first(ph[4:])
    cw_full = g_small[3][:, :CONV_W, :].transpose(1, 0, 2).reshape(CONV_W, D_MODEL)
    rnn_w = (cw_full, conv_b, wrg2, b_rg, wig2, b_ig, lru_lambda)
    (xc, h, *gates, ya), ph = _rnn_fwd(xr, gr, *rnn_w, n_seq, S, tm_rnn,
                               phases=[_ph_gather_pass(g) for g in (wu, wpg, wpp)]
                               + [_ph_gather_send(wb["w_down"])])
    (wu, wpg, wpp), wd = first(ph[:3]), ph[3][0]
    wr, wa, wo = (g.reshape(D_MODEL, D_MODEL) for g in g_small[:3])
    wpg = wpg.reshape(D_MODEL, D_MODEL)
    (x1, merged), ph = _merge_fwd(xf, ya, o, ga, gb, wr, wa, wo, tm, phases=[_ph_gather_pass(wd)])
    wd = ph[0][0].reshape(D_FF, D_MODEL)
    (x2, hm, u, act), _ = _mlp_fwd(x1, g_mlp, wu, wd, tm // 2)
    (loss_t, dx2, pb, de, hp, dtg, dg_ple), _ = _ple_loss(x2, pf, tf, g_ple, wpg, wpp, tm)

    chipmajor = lambda g: g.reshape(N_CHIPS, g.shape[-2] // N_CHIPS, g.shape[-1]) if g.ndim == 2 else g
    tmw = min(2 * tm, T)
    dw_pp = _wgrad(pb, de, "wgrad_ple_proj", False, D_MODEL, tmw)[0]
    part_ple = [chipmajor(_wgrad(hp, dtg, "wgrad_ple_gate", False, D_MODEL, tmw)[0]),
                dw_pp.reshape(PLE_DIM, N_CHIPS, D_MODEL // N_CHIPS).transpose(1, 0, 2)]
    (dx1, du, dg_mlp), ph = _mlp_bwd(dx2, u, x1, g_mlp, wu, wd, tm // 2, phases=[_ph_pair_send(g) for g in part_ple])
    send_ple, own_ple = _pair_sum(part_ple, first(ph), "pair_sum_ple")
    dw_down, dw_down_b, ph = _wgrad(act, dx2, "wgrad_down", False, D_MODEL // 2, tmw, narrow=True,
                                    phases=[_ph_chip_send(s) for s in send_ple])
    red_ple = _chip_sum(own_ple, first(ph), "chip_sum_ple")
    dw_up, dw_up_b, _ = _wgrad(hm, du, "wgrad_up", True, D_MODEL, tmw, narrow=True)
    part_mlp, part_mlp_b = [dw_up, chipmajor(dw_down)], [dw_up_b, chipmajor(dw_down_b)]
    (dga, dgb, dya, dyb, dyain, do), _ = _merge_bwd(dx1, ga, gb, ya, o, wr, wa, wo, tm)
    dw_rnn, ph_up = _wgrad(ya, dya, "wgrad_rnn_proj", False, D_MODEL, tmw, phases=[_ph_pair_send(part_mlp_b[0])])
    dw_attn, ph_down = _wgrad(o, dyb, "wgrad_attn_proj", False, D_MODEL, tmw, phases=[_ph_pair_send(part_mlp_b[1])])
    dw_out, ph = _wgrad(merged, dx1, "wgrad_out", False, D_MODEL, tmw, phases=[_ph_half_swap(r) for r in red_ple])
    red_ple = first(ph)
    send_mlp, own_mlp = _pair_sum(part_mlp, [ph_up[0][0], ph_down[0][0]], "pair_sum_mlp")
    part_mix = [chipmajor(dw_rnn), chipmajor(dw_attn), chipmajor(dw_out)]
    (dxr, dgr, vec, dwrg2, dwig2), ph = _rnn_bwd(
        dyain, xr, gr, xc, h, gates, cw_full, wrg2, wig2, lru_lambda, n_seq, S, tm_rnn,
        phases=[_ph_chip_send(s) for s in send_mlp] + [_ph_pair_send(g) for g in part_mix])
    red_mlp = _chip_sum(own_mlp, first(ph[:2]), "chip_sum_mlp")
    send_mix, own_mix = _pair_sum(part_mix, first(ph[2:]), "pair_sum_mix")
    (dq, dkc, dkp, dvc, dvp, dqg, dsk), ph = _attn_bwd(
        do, zq, zk, zv, *attn_c, phases=[_ph_half_swap(r) for r in red_mlp] + [_ph_chip_send(s) for s in send_mix])
    red_mlp = first(ph[:2])
    red_mix = _chip_sum(own_mix, first(ph[2:]), "chip_sum_mix")
    (dk, dv, dkg), _ = _kv_bwd(dkc, dkp, dvc, dvp, zk, kg, cosf, sins, ind_k, ind_kt, n_seq, S)
    dz_parts = [dxr, dgr, dq, dk, dv, dga, dgb]
    send_in, own_in = _pair_exchange_sum(*_wgrad_in(h0, dz_parts, tm), "pair_sum_in")
    (grad_x, dg_mix), ph = _inproj_bwd(dz_parts, w_in_g, xf, g_mix, dx1, tm,
                                       phases=[_ph_half_swap(r) for r in red_mix] + [_ph_chip_send(send_in)])
    red_mix = first(ph[:3])
    red_in = _chip_sum([own_in], first(ph[3:]), "chip_sum_in")
    reduced = dict(zip(grp_ple + grp_mlp + grp_mix, red_ple + red_mlp + red_mix))
    grads = {
        "g_mix": dg_mix[0], "g_mlp": dg_mlp[0], "g_ple": dg_ple[0],
        "conv_w": vec[0:CONV_W], "conv_b": vec[4], "b_rg": vec[5], "b_ig": vec[6], "lru_lambda": vec[7],
        "w_rg": _pair_blockdiag_extract(dwrg2), "w_ig": _pair_blockdiag_extract(dwig2),
        "q_gain": dqg.reshape(N_HEADS, HEAD_DIM).sum(0), "k_gain": dkg.reshape(N_KV, HEAD_DIM).sum(0),
        "sinks": dsk.sum(1),
    }

    rows = [grads["conv_w"], _pad_row(grads["conv_b"]), _pad_row(grads["b_rg"]), _pad_row(grads["b_ig"]),
            _pad_row(grads["lru_lambda"]), _pad_row(grads["g_mix"]), _pad_row(grads["g_mlp"]),
            _pad_row(grads["g_ple"]), _pad_row(grads["q_gain"]), _pad_row(grads["k_gain"]), _pad_row(grads["sinks"]),
            _pad_row(loss_t[0:1, 0:1]), jnp.zeros((1, D_MODEL), F32)]
    vecs = jnp.concatenate(rows, axis=0)
    packed = jnp.concatenate([vecs.reshape(-1, LANES), grads["w_rg"].reshape(-1, LANES),
                              grads["w_ig"].reshape(-1, LANES)], axis=0)
    red = _allreduce_small(packed, "allreduce_small")
    nv = vecs.size // LANES
    rvec = red[0:nv].reshape(16, D_MODEL)
    loss = rvec[14, 0]
    nw = grads["w_rg"].size // LANES
    sg = {
        "conv_w": lax.dynamic_slice(rvec[0:CONV_W], (0, chip * (D_MODEL // N_CHIPS)), (CONV_W, D_MODEL // N_CHIPS)),
        "conv_b": rvec[4], "b_rg": rvec[5], "b_ig": rvec[6], "lru_lambda": rvec[7], "g_mix": rvec[8],
        "g_mlp": rvec[9], "g_ple": rvec[10], "q_gain": rvec[11, :HEAD_DIM], "k_gain": rvec[12, :HEAD_DIM],
        "sinks": rvec[13, :N_HEADS], "w_rg": red[nv:nv + nw], "w_ig": red[nv + nw:nv + 2 * nw],
    }
    sg = {k: sg[k].reshape(w[k].shape) for k in _SMALL}
    d_s, m_s, v_s = _adamw_small([w[k] for k in _SMALL], [sg[k] for k in _SMALL], [m[k] for k in _SMALL],
                                 [v[k] for k in _SMALL])
    grad, delta, new_m, new_v = dict(sg), dict(zip(_SMALL, d_s)), dict(zip(_SMALL, m_s)), dict(zip(_SMALL, v_s))

    for name in ("w_ple_proj", "w_up", "w_down", "w_rnn_proj", "w_attn_proj", "w_out", "w_ple_gate", "w_in"):
        shape = w[name].shape
        outs, ph = _adamw(w[name][0], reduced[name], m[name][0], v[name][0], "adamw_" + name, min(ADAMW_ROWS, shape[1] // 2),
                          phases=[_ph_half_swap(r) for r in red_in] if name == "w_ple_proj" else ())
        if name == "w_ple_proj":
            reduced["w_in"] = ph[0][0]
        grad[name], delta[name], new_m[name], new_v[name] = (a.reshape(shape) for a in outs)

    return (loss, grad_x.reshape(x.shape), *[grad[k] for k in _WEIGHTS], *[delta[k] for k in _WEIGHTS],
            *[new_m[k] for k in _WEIGHTS], *[new_v[k] for k in _WEIGHTS])
```

```python
import functools
import math

import numpy as np
import jax
import jax.numpy as jnp
from jax import lax
from jax.experimental import pallas as pl
from jax.experimental.pallas import tpu as pltpu

F32 = jnp.float32
BF16 = jnp.bfloat16

D_MODEL = 1024
N_HEADS = 16
N_KV = 4
HEAD_DIM = 64
KV_W = N_KV * HEAD_DIM
D_FF = 4096
PLE_DIM = 256
WINDOW = 128
CONV_W = 4
LRU_C = 8.0
NORM_EPS = 1e-6
ROPE_THETA = 10000.0
N_CHIPS = 4
IN_TOTAL = 5632
IN_BLK = IN_TOTAL // N_CHIPS
IN_SEGS = (0, 1024, 2048, 3072, 3328, 3584, 4608, 5632)

ADAM_LR = 0.001
ADAM_B1 = 0.9
ADAM_B2 = 0.999
ADAM_EPS = 1e-08
ADAM_WD = 0.01
ADAM_STEP = 10

LANES = 128
V7X_VMEM_BYTES = 64 * 1024 * 1024
VMEM_LIMIT = V7X_VMEM_BYTES - 8 * 1024 * 1024
MESH_ID = pl.DeviceIdType.MESH
TM, TM_RNN, ADAMW_ROWS = 512, 256, 256
ATTN_BLOCKS_PER_STEP = 1
KV_BLOCKS_PER_STEP = 8


def _dot(a, b):
    return jnp.dot(a, b, preferred_element_type=F32)


def _dot_nt(a, b):
    return lax.dot_general(a, b, (((1,), (1,)), ((), ())), preferred_element_type=F32)


def _dot_tn(a, b):
    return lax.dot_general(a, b, (((0,), (0,)), ((), ())), preferred_element_type=F32)


def _split_dot(x, ind):
    hi = x.astype(BF16)
    lo = (x - hi.astype(F32)).astype(BF16)
    return _dot(hi, ind) + _dot(lo, ind)


def _sigmoid(x):
    return 1.0 / (1.0 + jnp.exp(-x))


_GELU_C = math.sqrt(2.0 / math.pi)


def _gelu_and_grad(g):
    inner = _GELU_C * (g + 0.044715 * g * g * g)
    t = jnp.tanh(inner)
    gelu = 0.5 * g * (1.0 + t)
    dgelu = 0.5 * (1.0 + t) + 0.5 * g * (1.0 - t * t) * _GELU_C * (1.0 + 3.0 * 0.044715 * g * g)
    return gelu, dgelu


def _const(shape):
    nd = len(shape)
    return pl.BlockSpec(shape, lambda *_: (0,) * nd)


def _params(n_grid, vmem=VMEM_LIMIT):
    return pltpu.CompilerParams(dimension_semantics=("arbitrary",) * n_grid, vmem_limit_bytes=vmem)


def _rms_fwd(x, g):
    r = lax.rsqrt(jnp.mean(x * x, axis=-1, keepdims=True) + NORM_EPS)
    return (x * r) * g, r


def _rms_bwd(dy, x, r, g):
    dn = dy * g
    dx = r * dn - x * (r * r * r * jnp.mean(dn * x, axis=-1, keepdims=True))
    dg = jnp.sum(dy * (x * r), axis=0, keepdims=True)
    return dx, dg


def _seg_pieces(blk_lo, blk_hi):
    out = []
    for s in range(7):
        lo, hi = max(blk_lo, IN_SEGS[s]), min(blk_hi, IN_SEGS[s + 1])
        if lo < hi:
            out.append((s, lo - IN_SEGS[s], hi - IN_SEGS[s], lo - blk_lo))
    return out


def _mesh_pos():
    x, y, c = lax.axis_index("x"), lax.axis_index("y"), lax.axis_index("c")
    other_chips = [(1 - x, y), (x, 1 - y), (1 - x, 1 - y)]
    return x, y, c, other_chips


def _peer_slot(k, x, y):
    dx = jnp.bitwise_xor(k // 2, x)
    dy = jnp.bitwise_xor(k % 2, y)
    return jnp.maximum(dx + 2 * dy - 1, 0)


def _half_rows(c, R):
    return pl.ds(pl.multiple_of(c * R, R), R), pl.ds(pl.multiple_of((1 - c) * R, R), R)


def _remote(src, dst, sems, to):
    return pltpu.make_async_remote_copy(src_ref=src, dst_ref=dst, send_sem=sems[0], recv_sem=sems[1],
                                        device_id=to, device_id_type=MESH_ID)


class _Phase:
    def __init__(self, ins, inout, outs, n_remote, n_local, build):
        self.ins, self.inout, self.outs = list(ins), list(inout), list(outs)
        self.n_remote, self.n_local, self.build = n_remote, n_local, build


def _ph_gather_send(wb):
    R2, C = wb.shape
    R = R2 // 2

    def build(ins, outs, rsem, lsem):
        (w_ref,), (g_ref,) = ins, outs
        x, y, c, chips = _mesh_pos()
        me = 2 * x + y
        mine, _ = _half_rows(c, R)
        loc = [pltpu.make_async_copy(w_ref, g_ref.at[me], lsem(0))]
        outg = [_remote(w_ref.at[mine], g_ref.at[me, mine], rsem(j), (cx, cy, c)) for j, (cx, cy) in enumerate(chips)]
        inc = [functools.partial(_remote, w_ref.at[mine], g_ref.at[2 * cx + cy, mine], rsem(j), (x, y, c))
               for j, (cx, cy) in enumerate(chips)]
        return loc, outg, inc

    return _Phase([wb], [], [jax.ShapeDtypeStruct((N_CHIPS, R2, C), wb.dtype)], 3, 1, build)


def _ph_gather_pass(gath):
    _, R2, C = gath.shape
    R = R2 // 2

    def build(ins, outs, rsem, lsem):
        (g_ref,) = outs
        x, y, c, chips = _mesh_pos()
        mine, theirs = _half_rows(c, R)
        outg, inc = [], []
        for j, (cx, cy) in enumerate(chips):
            blk = g_ref.at[2 * cx + cy, mine]
            outg.append(_remote(blk, blk, rsem(j), (x, y, 1 - c)))
            got = g_ref.at[2 * cx + cy, theirs]
            inc.append(functools.partial(_remote, got, got, rsem(j), (x, y, c)))
        return [], outg, inc

    return _Phase([], [gath], [], 3, 0, build)


def _ph_pair_send(partial):
    _, R2, C = partial.shape
    R = R2 // 2

    def build(ins, outs, rsem, lsem):
        (p_ref,), (s_ref,) = ins, outs
        x, y, c, _ = _mesh_pos()
        _, theirs = _half_rows(c, R)
        src = p_ref.at[:, theirs, :]
        return ([], [_remote(src, s_ref, rsem(0), (x, y, 1 - c))],
                [functools.partial(_remote, src, s_ref, rsem(0), (x, y, c))])

    return _Phase([partial], [], [jax.ShapeDtypeStruct((N_CHIPS, R, C), partial.dtype)], 1, 0, build)


def _ph_chip_send(sendb):
    def build(ins, outs, rsem, lsem):
        (s_ref,), (r_ref,) = ins, outs
        x, y, c, chips = _mesh_pos()
        outg = [_remote(s_ref.at[j], r_ref.at[j], rsem(j), (cx, cy, c)) for j, (cx, cy) in enumerate(chips)]
        inc = [functools.partial(_remote, s_ref.at[j], r_ref.at[j], rsem(j), (x, y, c)) for j in range(3)]
        return [], outg, inc

    return _Phase([sendb], [], [jax.ShapeDtypeStruct(sendb.shape, sendb.dtype)], 3, 0, build)


def _ph_half_swap(red):
    R2, C = red.shape
    R = R2 // 2

    def build(ins, outs, rsem, lsem):
        (r_ref,) = outs
        x, y, c, _ = _mesh_pos()
        mine, theirs = _half_rows(c, R)
        return ([], [_remote(r_ref.at[mine], r_ref.at[mine], rsem(0), (x, y, 1 - c))],
                [functools.partial(_remote, r_ref.at[theirs], r_ref.at[theirs], rsem(0), (x, y, c))])

    return _Phase([], [red], [], 1, 0, build)


def _call(body, *, name, grid, in_specs, out_specs, out_shape, scratch_shapes=(), phases=()):
    single = not isinstance(out_specs, (list, tuple))
    out_specs = [out_specs] if single else list(out_specs)
    out_shape = [out_shape] if single else list(out_shape)
    n_in, n_out, n_scr = len(in_specs), len(out_specs), len(scratch_shapes)
    if not phases:
        call = pl.pallas_call(body, name=name, grid=grid, in_specs=in_specs, out_specs=out_specs,
                              out_shape=out_shape, scratch_shapes=list(scratch_shapes),
                              compiler_params=_params(len(grid)))
        return lambda *operands: (list(call(*operands)), [])

    ex_in, ex_out, aliases, spans = [], [], {}, []
    for ph in phases:
        i0, o0 = len(ex_in), len(ex_out)
        ex_in += ph.ins
        for a in ph.inout:
            aliases[n_in + len(ex_in)] = n_out + len(ex_out)
            ex_in.append(a)
            ex_out.append(jax.ShapeDtypeStruct(a.shape, a.dtype))
        ex_out += ph.outs
        spans.append((i0, len(ph.ins), o0, len(ex_out) - o0))
    n_remote = sum(ph.n_remote for ph in phases)
    n_local = max(sum(ph.n_local for ph in phases), 1)

    def wrapped(*refs):
        base_in, xin = refs[:n_in], refs[n_in:n_in + len(ex_in)]
        o0 = n_in + len(ex_in)
        base_out, xout = refs[o0:o0 + n_out], refs[o0 + n_out:o0 + n_out + len(ex_out)]
        scr = refs[o0 + n_out + len(ex_out):]
        send_sems, recv_sems, loc_sems = scr[n_scr:]
        first = functools.reduce(jnp.logical_and, [pl.program_id(i) == 0 for i in range(len(grid))])
        last = functools.reduce(jnp.logical_and, [pl.program_id(i) == grid[i] - 1 for i in range(len(grid))])

        def copies():
            out, r0, l0 = [], 0, 0
            for ph, (i0, ni, p0, no) in zip(phases, spans):
                rsem = lambda k, r0=r0: (send_sems.at[r0 + k], recv_sems.at[r0 + k])
                lsem = lambda k, l0=l0: loc_sems.at[l0 + k]
                out.append(ph.build(xin[i0:i0 + ni], xout[p0:p0 + no], rsem, lsem))
                r0, l0 = r0 + ph.n_remote, l0 + ph.n_local
            return out

        @pl.when(first)
        def _():
            for loc, outg, _ in copies():
                for cp in loc + outg:
                    cp.start()

        body(*base_in, *base_out, *scr[:n_scr])

        @pl.when(last)
        def _():
            for loc, outg, inc in copies():
                for make in inc:
                    make().wait_recv()
                for cp in outg:
                    cp.wait_send()
                for cp in loc:
                    cp.wait()

    hbm = pl.BlockSpec(memory_space=pl.ANY)
    call = pl.pallas_call(
        wrapped, name=name, grid=grid, in_specs=list(in_specs) + [hbm] * len(ex_in),
        out_specs=out_specs + [hbm] * len(ex_out), out_shape=out_shape + ex_out,
        scratch_shapes=list(scratch_shapes) + [pltpu.SemaphoreType.DMA((n_remote,)), pltpu.SemaphoreType.DMA((n_remote,)),
                                              pltpu.SemaphoreType.DMA((n_local,))],
        input_output_aliases=aliases, compiler_params=_params(len(grid)))

    def run(*operands):
        res = call(*operands, *ex_in)
        extra = res[n_out:]
        return list(res[:n_out]), [list(extra[p0:p0 + no]) for (_, _, p0, no) in spans]

    return run


def _inproj_fwd(x, g_mix, w_in, tm, phases=()):
    T = x.shape[0]
    widths = [IN_SEGS[i + 1] - IN_SEGS[i] for i in range(7)]

    def body(x_ref, g_ref, w_ref, h_ref, *z_refs):
        h, _ = _rms_fwd(x_ref[...], g_ref[...])
        hb = h.astype(BF16)
        h_ref[...] = hb
        for j in range(N_CHIPS):
            zj = _dot(hb, w_ref[j])
            for s, lo, hi, off in _seg_pieces(j * IN_BLK, (j + 1) * IN_BLK):
                z_refs[s][:, lo:hi] = zj[:, off:off + hi - lo]

    return _call(
        body, phases=phases, name="inproj_fwd", grid=(T // tm,),
        in_specs=[pl.BlockSpec((tm, D_MODEL), lambda i: (i, 0)), _const((1, D_MODEL)),
                  _const((N_CHIPS, D_MODEL, IN_BLK))],
        out_specs=[pl.BlockSpec((tm, D_MODEL), lambda i: (i, 0))]
        + [pl.BlockSpec((tm, w), lambda i: (i, 0)) for w in widths],
        out_shape=[jax.ShapeDtypeStruct((T, D_MODEL), BF16)]
        + [jax.ShapeDtypeStruct((T, w), F32) for w in widths],
    )(x, g_mix, w_in)


def _inproj_bwd(dz_parts, w_in, x, g_mix, dx1, tm, phases=()):
    T = x.shape[0]
    widths = [IN_SEGS[i + 1] - IN_SEGS[i] for i in range(7)]

    def body(*refs):
        p_refs = refs[:7]
        w_ref, x_ref, g_ref, dx1_ref, gx_ref, dg_ref, dz_ref = refs[7:]

        @pl.when(pl.program_id(0) == 0)
        def _():
            dg_ref[...] = jnp.zeros_like(dg_ref)

        for s in range(7):
            dz_ref[:, IN_SEGS[s]:IN_SEGS[s + 1]] = p_refs[s][...]
        dh = jnp.zeros((tm, D_MODEL), F32)
        for j in range(N_CHIPS):
            dh = dh + _dot_nt(dz_ref[:, j * IN_BLK:(j + 1) * IN_BLK], w_ref[j])
        xv = x_ref[...]
        g = g_ref[...]
        _, r = _rms_fwd(xv, g)
        dx, dg = _rms_bwd(dh, xv, r, g)
        gx_ref[...] = dx1_ref[...] + dx
        dg_ref[...] += dg

    row = lambda w: pl.BlockSpec((tm, w), lambda i: (i, 0))
    return _call(
        body, phases=phases, name="inproj_bwd", grid=(T // tm,),
        in_specs=[row(w) for w in widths]
        + [_const((N_CHIPS, D_MODEL, IN_BLK)), row(D_MODEL), _const((1, D_MODEL)), row(D_MODEL)],
        out_specs=[row(D_MODEL), _const((1, D_MODEL))],
        out_shape=[jax.ShapeDtypeStruct((T, D_MODEL), F32), jax.ShapeDtypeStruct((1, D_MODEL), F32)],
        scratch_shapes=[pltpu.VMEM((tm, IN_TOTAL), BF16)],
    )(*dz_parts, w_in, x, g_mix, dx1)


def _wgrad_in(h0, dz_parts, tm):
    T = h0.shape[0]
    widths = [IN_SEGS[i + 1] - IN_SEGS[i] for i in range(7)]

    def body(*refs):
        h_ref, p_refs, o_ref, ob_ref, acc_ref, stage_ref, sems = (refs[0], refs[1:8], refs[8], refs[9], refs[10],
                                                                  refs[11], refs[12])
        t = pl.program_id(0)
        last = T // tm - 1

        @pl.when(t == 0)
        def _():
            acc_ref[...] = jnp.zeros_like(acc_ref)

        def accumulate(j):
            for s, lo, hi, off in _seg_pieces(j * IN_BLK, (j + 1) * IN_BLK):
                acc_ref[j, :, off:off + hi - lo] += _dot_tn(h_ref[...], p_refs[s][:, lo:hi])

        @pl.when(t < last)
        def _():
            for j in range(N_CHIPS):
                accumulate(j)

        @pl.when(t == last)
        def _():
            copies = [pltpu.make_async_copy(acc_ref.at[j], o_ref.at[j], sems.at[j]) for j in range(N_CHIPS)]
            narrow = [pltpu.make_async_copy(stage_ref.at[j % 2], ob_ref.at[j], sems.at[N_CHIPS + j])
                      for j in range(N_CHIPS)]
            for j in range(N_CHIPS):
                accumulate(j)
                copies[j].start()
                if j >= 2:
                    narrow[j - 2].wait()
                stage_ref[j % 2] = acc_ref[j].astype(BF16)
                narrow[j].start()
            for cp in copies + narrow[N_CHIPS - 2:]:
                cp.wait()

    row = lambda w: pl.BlockSpec((tm, w), lambda i: (i, 0))
    hbm = pl.BlockSpec(memory_space=pl.ANY)
    return pl.pallas_call(
        body, name="wgrad_in", grid=(T // tm,), in_specs=[row(D_MODEL)] + [row(w) for w in widths],
        out_specs=[hbm, hbm],
        out_shape=[jax.ShapeDtypeStruct((N_CHIPS, D_MODEL, IN_BLK), F32),
                   jax.ShapeDtypeStruct((N_CHIPS, D_MODEL, IN_BLK), BF16)],
        scratch_shapes=[pltpu.VMEM((N_CHIPS, D_MODEL, IN_BLK), F32), pltpu.VMEM((2, D_MODEL, IN_BLK), BF16),
                        pltpu.SemaphoreType.DMA((2 * N_CHIPS,))],
        compiler_params=_params(1),
    )(h0, *dz_parts)


def _wgrad(a, g, name, blocked, cn, tm, phases=(), narrow=False):
    T, K = a.shape
    N = g.shape[1]
    nb = N // cn

    def body(a_ref, g_ref, o_ref, *ob_ref):
        @pl.when(pl.program_id(1) == 0)
        def _():
            o_ref[...] = jnp.zeros_like(o_ref)

        o_ref[...] += _dot_tn(a_ref[...].astype(BF16), g_ref[...].astype(BF16))
        if narrow:
            @pl.when(pl.program_id(1) == T // tm - 1)
            def _():
                ob_ref[0][...] = o_ref[...].astype(BF16)

    if blocked:
        out_spec = pl.BlockSpec((None, K, cn), lambda j, t: (j, 0, 0))
        shape = (nb, K, cn)
    else:
        out_spec = pl.BlockSpec((K, cn), lambda j, t: (0, j))
        shape = (K, N)
    n_out = 2 if narrow else 1
    outs, extra = _call(
        body, phases=phases, name=name, grid=(nb, T // tm),
        in_specs=[pl.BlockSpec((tm, K), lambda j, t: (t, 0)), pl.BlockSpec((tm, cn), lambda j, t: (t, j))],
        out_specs=[out_spec] * n_out,
        out_shape=[jax.ShapeDtypeStruct(shape, F32), jax.ShapeDtypeStruct(shape, BF16)][:n_out],
    )(a, g)
    return (*outs, extra)


def _shift_down(x, prev8, sft, row, row8, tm):
    xs = pltpu.roll(x, sft, 0)
    top = jnp.where(row8 < sft, pltpu.roll(prev8, sft, 0), xs[0:8])
    return jnp.concatenate([top, xs[8:]], axis=0)


def _shift_up(x, next8, sft, row8, tm):
    xs = pltpu.roll(x, tm - sft, 0)
    bot = jnp.where(row8 >= 8 - sft, pltpu.roll(next8, 8 - sft, 0), xs[tm - 8:tm])
    return jnp.concatenate([xs[0:tm - 8], bot], axis=0)


def _conv_fwd(x, prev8, cw_ref, cb, row, row8, tm):
    xc = cb + cw_ref[CONV_W - 1:CONV_W, :] * x
    for sft in range(1, CONV_W):
        j = CONV_W - 1 - sft
        xc = xc + cw_ref[j:j + 1, :] * _shift_down(x, prev8, sft, row, row8, tm)
    return xc


def _blockdiag_dot(xb, w_ref, transpose):
    outs = []
    for b in range(D_MODEL // LANES):
        xs = xb[:, b * LANES:(b + 1) * LANES]
        outs.append(_dot_nt(xs, w_ref[b]) if transpose else _dot(xs, w_ref[b]))
    return jnp.concatenate(outs, axis=1)


def _softplus_neg(lam):
    e = jnp.exp(-jnp.abs(lam))
    u = 1.0 + e
    log1p_e = jnp.where(u == 1.0, e, jnp.log(u) * (e / (u - 1.0)))
    sp = jnp.maximum(-lam, 0.0) + log1p_e
    return sp, -_sigmoid(-lam)


def _lru_gates(xc, wrg_ref, brg, wig_ref, big, sp):
    xcb = xc.astype(BF16)
    r = _sigmoid(_blockdiag_dot(xcb, wrg_ref, False) + brg)
    i = _sigmoid(_blockdiag_dot(xcb, wig_ref, False) + big)
    log_a = (-LRU_C) * r * sp
    a = jnp.exp(log_a)
    t = jnp.tanh(log_a)
    one_m_a2 = (-2.0) * t / (1.0 - t)
    mult = jnp.sqrt(one_m_a2)
    return xcb, r, i, a, mult


def _scan_down(a, b, row, tm):
    d = 1
    while d < tm:
        if d < 8:
            keep = row >= d
            a_s = jnp.where(keep, pltpu.roll(a, d, 0), 1.0)
            b_s = jnp.where(keep, pltpu.roll(b, d, 0), 0.0)
            b = a * b_s + b
            a = a * a_s
        else:
            b = jnp.concatenate([b[:d], a[d:] * b[:-d] + b[d:]], axis=0)
            a = jnp.concatenate([a[:d], a[d:] * a[:-d]], axis=0)
        d *= 2
    return a, b


def _scan_up(c, b, row, tm):
    d = 1
    while d < tm:
        if d < 8:
            keep = row < tm - d
            c_s = jnp.where(keep, pltpu.roll(c, tm - d, 0), 1.0)
            b_s = jnp.where(keep, pltpu.roll(b, tm - d, 0), 0.0)
            b = c * b_s + b
            c = c * c_s
        else:
            b = jnp.concatenate([c[:-d] * b[d:] + b[:-d], b[-d:]], axis=0)
            c = jnp.concatenate([c[:-d] * c[d:], c[-d:]], axis=0)
        d *= 2
    return c, b


def _rnn_fwd(xr, gr, conv_w, conv_b, wrg2, b_rg, wig2, b_ig, lam, n_seq, S, tm, phases=()):
    T = xr.shape[0]
    nt = S // tm
    W = D_MODEL

    def body(xr_ref, gr_ref, cw_ref, cb_ref, wrg_ref, brg_ref, wig_ref, big_ref, lam_ref,
             xc_ref, h_ref, r_ref, i_ref, a_ref, mult_ref, ya_ref, px_ref, ph_ref):
        @pl.when(pl.program_id(1) == 0)
        def _():
            px_ref[...] = jnp.zeros_like(px_ref)
            ph_ref[...] = jnp.zeros_like(ph_ref)

        row = lax.broadcasted_iota(jnp.int32, (tm, W), 0)
        row8 = lax.broadcasted_iota(jnp.int32, (8, W), 0)
        x = xr_ref[...]
        xc = _conv_fwd(x, px_ref[...], cw_ref, cb_ref[...], row, row8, tm)
        sp, _ = _softplus_neg(lam_ref[...])
        _, r, i, a, mult = _lru_gates(xc, wrg_ref, brg_ref[...], wig_ref, big_ref[...], sp)
        r_ref[...], i_ref[...], a_ref[...], mult_ref[...] = r, i, a, mult
        bterm = mult * (i * xc)
        acum, hloc = _scan_down(a, bterm, row, tm)
        h = hloc + acum * ph_ref[7:8, :]
        h_ref[...] = h
        xc_ref[...] = xc
        gelu, _ = _gelu_and_grad(gr_ref[...])
        ya_ref[...] = (h * gelu).astype(BF16)
        px_ref[...] = xr_ref[tm - 8:tm, :]
        ph_ref[...] = h_ref[tm - 8:tm, :]

    tile = pl.BlockSpec((tm, W), lambda s, t: (s * nt + t, 0))
    return _call(
        body, phases=phases, name="rnn_fwd", grid=(n_seq, nt),
        in_specs=[tile, tile, _const((CONV_W, W)), _const((1, W)), _const((8, LANES, LANES)), _const((1, W)),
                  _const((8, LANES, LANES)), _const((1, W)), _const((1, W))],
        out_specs=[tile] * 7,
        out_shape=[jax.ShapeDtypeStruct((T, W), F32)] * 6 + [jax.ShapeDtypeStruct((T, W), BF16)],
        scratch_shapes=[pltpu.VMEM((8, W), F32), pltpu.VMEM((8, W), F32)],
    )(xr, gr, conv_w, conv_b, wrg2, b_rg, wig2, b_ig, lam)


def _rnn_bwd(dya, xr, gr, xc, h, gates, conv_w, wrg2, wig2, lam, n_seq, S, tm, phases=()):
    T = xr.shape[0]
    nt = S // tm
    W = D_MODEL
    nb8 = tm // 8

    def body(dya_ref, xr_ref, gr_ref, xc_ref, h_ref, r_ref, i_ref, a_ref, mult_ref, xprev_ref, hprev_ref, cw_ref,
             wrg_ref, wig_ref, lam_ref, dxr_ref, dgr_ref, vec_ref, dwrg_ref, dwig_ref, cg_ref, ndxc_ref, tmp_ref):
        s, ti = pl.program_id(0), pl.program_id(1)

        @pl.when((s == 0) & (ti == 0))
        def _():
            vec_ref[...] = jnp.zeros_like(vec_ref)
            dwrg_ref[...] = jnp.zeros_like(dwrg_ref)
            dwig_ref[...] = jnp.zeros_like(dwig_ref)

        @pl.when(ti == 0)
        def _():
            cg_ref[...] = jnp.zeros_like(cg_ref)
            ndxc_ref[...] = jnp.zeros_like(ndxc_ref)

        first = ti == nt - 1
        row = lax.broadcasted_iota(jnp.int32, (tm, W), 0)
        row8 = lax.broadcasted_iota(jnp.int32, (8, W), 0)
        x = xr_ref[...]
        xc = xc_ref[...]
        hv = h_ref[...]
        xprev = jnp.where(first, 0.0, xprev_ref[...])
        hprev = jnp.where(first, 0.0, hprev_ref[...])
        sp, dsp_dlam = _softplus_neg(lam_ref[...])
        xcb = xc.astype(BF16)
        r, i, a, mult = r_ref[...], i_ref[...], a_ref[...], mult_ref[...]

        gelu, dgelu = _gelu_and_grad(gr_ref[...])
        dya_v = dya_ref[...]
        dgr_ref[...] = (dya_v * hv * dgelu).astype(BF16)
        dh = dya_v * gelu
        c = jnp.where(row < tm - 1, pltpu.roll(a, tm - 1, 0), 1.0)
        ccum, gloc = _scan_up(c, dh, row, tm)
        G = gloc + ccum * cg_ref[0:1, :]
        tmp_ref[...] = a * G
        cg_ref[...] = tmp_ref[0:8, :]

        h_m1 = _shift_down(hv, hprev, 1, row, row8, tm)
        ixc = i * xc
        dixc = G * mult
        dlog_a = (G * h_m1) * a - (G * ixc) * (a * a / mult)
        dr = dlog_a * ((-LRU_C) * sp)
        di = dixc * xc
        drg = dr * r * (1.0 - r)
        dig = di * i * (1.0 - i)
        vec_ref[7:8, :] += jnp.sum(dlog_a * ((-LRU_C) * r), axis=0, keepdims=True) * dsp_dlam
        vec_ref[5:6, :] += jnp.sum(drg, axis=0, keepdims=True)
        vec_ref[6:7, :] += jnp.sum(dig, axis=0, keepdims=True)
        drgb = drg.astype(BF16)
        digb = dig.astype(BF16)
        dxc = dixc * i + _blockdiag_dot(drgb, wrg_ref, True) + _blockdiag_dot(digb, wig_ref, True)
        for b in range(W // LANES):
            sl = slice(b * LANES, (b + 1) * LANES)
            dwrg_ref[b] += _dot_tn(xcb[:, sl], drgb[:, sl])
            dwig_ref[b] += _dot_tn(xcb[:, sl], digb[:, sl])

        vec_ref[4:5, :] += jnp.sum(dxc, axis=0, keepdims=True)
        vec_ref[3:4, :] += jnp.sum(dxc * x, axis=0, keepdims=True)
        dxr = cw_ref[CONV_W - 1:CONV_W, :] * dxc
        nxt = ndxc_ref[...]
        for sft in range(1, CONV_W):
            j = CONV_W - 1 - sft
            vec_ref[j:j + 1, :] += jnp.sum(dxc * _shift_down(x, xprev, sft, row, row8, tm), axis=0, keepdims=True)
            dxr = dxr + cw_ref[j:j + 1, :] * _shift_up(dxc, nxt, sft, row8, tm)
        dxr_ref[...] = dxr.astype(BF16)
        tmp_ref[...] = dxc
        ndxc_ref[...] = tmp_ref[0:8, :]

    rev = lambda s, t: (s * nt + nt - 1 - t, 0)
    tile = pl.BlockSpec((tm, W), rev)
    prev8 = pl.BlockSpec((8, W), lambda s, t: (jnp.maximum((s * nt + nt - 1 - t) * nb8 - 1, 0), 0))
    return _call(
        body, phases=phases, name="rnn_bwd", grid=(n_seq, nt),
        in_specs=[tile] * 9 + [prev8, prev8, _const((CONV_W, W)), _const((8, LANES, LANES)),
                               _const((8, LANES, LANES)), _const((1, W))],
        out_specs=[tile, tile, _const((16, W)), _const((8, LANES, LANES)), _const((8, LANES, LANES))],
        out_shape=[jax.ShapeDtypeStruct((T, W), BF16), jax.ShapeDtypeStruct((T, W), BF16),
                   jax.ShapeDtypeStruct((16, W), F32), jax.ShapeDtypeStruct((8, LANES, LANES), F32),
                   jax.ShapeDtypeStruct((8, LANES, LANES), F32)],
        scratch_shapes=[pltpu.VMEM((8, W), F32), pltpu.VMEM((8, W), F32), pltpu.VMEM((tm, W), F32)],
    )(dya, xr, gr, xc, h, *gates, xr, h, conv_w, wrg2, wig2, lam)


def _head_swap(t, lane):
    w = t.shape[1]
    return jnp.where(lane % HEAD_DIM < HEAD_DIM // 2, pltpu.roll(t, w - HEAD_DIM // 2, 1),
                     pltpu.roll(t, HEAD_DIM // 2, 1))


def _qk_prep(t, gain, cosf, sins, ind, indt, lane):
    ms = _split_dot(t * t, ind) * (1.0 / HEAD_DIM)
    rstd = _split_dot(lax.rsqrt(ms + NORM_EPS), indt)
    tn = (t * rstd) * gain
    return tn * cosf + _head_swap(tn, lane) * sins, rstd


def _qk_prep_bwd(dy, t, rstd, gain, cosf, sins, ind, indt, lane):
    dtn = dy * cosf + _head_swap(dy * sins, lane)
    dgain = jnp.sum(dtn * (t * rstd), axis=0, keepdims=True)
    dn = dtn * gain
    m = _split_dot(_split_dot(dn * t, ind), indt) * (1.0 / HEAD_DIM)
    return rstd * dn - t * (rstd * rstd * rstd * m), dgain


def _attn_mask_t(blk_idx):
    ci = lax.broadcasted_iota(jnp.int32, (2 * WINDOW, WINDOW), 0)
    qi = lax.broadcasted_iota(jnp.int32, (2 * WINDOW, WINDOW), 1)
    diff = WINDOW + qi - ci
    return (diff >= 0) & (diff < WINDOW) & ((ci >= WINDOW) | (blk_idx > 0))


def _stack_heads(t, kvh, lo):
    parts = []
    for i in (2 * kvh, 2 * kvh + 1):
        tp = t[:, i * LANES:(i + 1) * LANES]
        parts += [jnp.where(lo, tp, 0.0), jnp.where(lo, 0.0, tp)]
    return jnp.concatenate(parts, axis=0).astype(BF16)


def _unstack_heads(ts, lo):
    w = WINDOW
    return jnp.where(lo, ts[0:w], ts[w:2 * w]), jnp.where(lo, ts[2 * w:3 * w], ts[3 * w:4 * w])


def _dup_head(t, kvh, lo2):
    m = kvh // 2
    t2 = t[:, m * LANES:(m + 1) * LANES]
    t2r = pltpu.roll(t2, HEAD_DIM, 1)
    return (jnp.where(lo2, t2, t2r) if kvh % 2 == 0 else jnp.where(lo2, t2r, t2)).astype(BF16)


def _fold_head(ts, kvh, lo2):
    tot = ts + pltpu.roll(ts, HEAD_DIM, 1)
    own = lo2 if kvh % 2 == 0 else ~lo2
    return jnp.where(own, tot, 0.0)


KEY_CHUNKS = tuple(slice(i * 64, (i + 1) * 64) for i in range(2 * WINDOW // 64))


def _fold8(x, op):
    return op(x.reshape(x.shape[0] // 8, 8, x.shape[1]), axis=0)


def _softmax_stats(s_ref, b, cols, sink):
    m8 = None
    for c in KEY_CHUNKS:
        t = _fold8(s_ref[b, c, cols], jnp.max)
        m8 = t if m8 is None else jnp.maximum(m8, t)
    mx = jnp.maximum(jnp.max(m8, axis=0, keepdims=True), sink)
    d8 = None
    for c in KEY_CHUNKS:
        t = _fold8(jnp.exp(s_ref[b, c, cols] - mx), jnp.sum)
        d8 = t if d8 is None else d8 + t
    es = jnp.exp(sink - mx)
    inv = 1.0 / (jnp.sum(d8, axis=0, keepdims=True) + es)
    return mx, inv, es * inv


def _attn_fwd(q, k, v, qg, kg, sinks, cosf, sins, ind_q, ind_qt, ind_k, ind_kt, n_seq, S, phases=()):
    T = q.shape[0]
    nblk = S // WINDOW
    W = D_MODEL

    def body(sink_ref, q_ref, k_ref, v_ref, qg_ref, kg_ref, cos_ref, sin_ref, iq_ref, iqt_ref, ik_ref, ikt_ref,
             o_ref, kc_ref, vc_ref, s_ref, p_ref, qs_ref, kd_ref, vd_ref):
        @pl.when(pl.program_id(1) == 0)
        def _():
            kc_ref[...] = jnp.zeros_like(kc_ref)
            vc_ref[...] = jnp.zeros_like(vc_ref)

        lane = lax.broadcasted_iota(jnp.int32, (WINDOW, W), 1)
        lo = lane[:, :LANES] < HEAD_DIM
        lo2 = lax.broadcasted_iota(jnp.int32, (2 * WINDOW, LANES), 1) < HEAD_DIM

        def one_block(h):
            n = pl.program_id(1) * bps + h
            rows = slice(h * WINDOW, (h + 1) * WINDOW)
            cosf, sinv = jnp.tile(cos_ref[rows, :], (1, W // LANES)), jnp.tile(sin_ref[rows, :], (1, W // LANES))
            qr, _ = _qk_prep(q_ref[rows, :], qg_ref[...], cosf, sinv, iq_ref[...], iqt_ref[...], lane)
            kr, _ = _qk_prep(k_ref[rows, :], kg_ref[...], cosf[:, :KV_W], sinv[:, :KV_W], ik_ref[...], ikt_ref[...],
                             lane[:, :KV_W])
            kc_ref[WINDOW:2 * WINDOW, :] = kr
            vc_ref[WINDOW:2 * WINDOW, :] = v_ref[rows, :]
            kc, vc = kc_ref[...], vc_ref[...]
            mask = jnp.tile(_attn_mask_t(n), (1, 4))
            qr = qr * HEAD_DIM ** -0.5
            for kvh in range(N_KV):
                qs_ref[h, kvh] = _stack_heads(qr, kvh, lo)
                kd_ref[h, kvh] = _dup_head(kc, kvh, lo2)
                vd_ref[h, kvh] = _dup_head(vc, kvh, lo2)

            def scores(kvh):
                s_ref[h, kvh % 2] = jnp.where(mask, _dot_nt(kd_ref[h, kvh], qs_ref[h, kvh]), -1e30)

            def softmax(kvh):
                sb, pb = s_ref.at[h], p_ref.at[h]
                b = kvh % 2
                for r in range(4):
                    cols = slice(r * WINDOW, (r + 1) * WINDOW)
                    mx, inv, _ = _softmax_stats(sb, b, cols, sink_ref[4 * kvh + r])
                    for c in KEY_CHUNKS:
                        pb[b, c, cols] = (jnp.exp(sb[b, c, cols] - mx) * inv).astype(BF16)

            def output(kvh):
                o0, o1 = _unstack_heads(_dot_tn(p_ref[h, kvh % 2], vd_ref[h, kvh]), lo)
                o_ref[rows, (2 * kvh) * LANES:(2 * kvh + 1) * LANES] = o0.astype(BF16)
                o_ref[rows, (2 * kvh + 1) * LANES:(2 * kvh + 2) * LANES] = o1.astype(BF16)

            scores(0)
            for kvh in range(N_KV):
                if kvh + 1 < N_KV:
                    scores(kvh + 1)
                softmax(kvh)
                output(kvh)
            kc_ref[0:WINDOW, :] = kr
            vc_ref[0:WINDOW, :] = v_ref[rows, :]

        for h in range(bps):
            one_block(h)

    bps = ATTN_BLOCKS_PER_STEP
    rows_step = bps * WINDOW
    blk = lambda w: pl.BlockSpec((rows_step, w), lambda s, n: (s * (nblk // bps) + n, 0))
    pos = pl.BlockSpec((rows_step, LANES), lambda s, n: (n, 0))
    outs, extra = _call(
        body, phases=phases, name="attn_fwd", grid=(n_seq, nblk // bps),
        in_specs=[pl.BlockSpec(memory_space=pltpu.SMEM), blk(W), blk(KV_W), blk(KV_W), _const((1, W)),
                  _const((1, KV_W)), pos, pos, _const((W, LANES)), _const((LANES, W)), _const((KV_W, LANES)),
                  _const((LANES, KV_W))],
        out_specs=blk(W), out_shape=jax.ShapeDtypeStruct((T, W), BF16),
        scratch_shapes=[pltpu.VMEM((2 * WINDOW, KV_W), F32), pltpu.VMEM((2 * WINDOW, KV_W), F32),
                        pltpu.VMEM((bps, 2, 2 * WINDOW, 4 * WINDOW), F32),
                        pltpu.VMEM((bps, 2, 2 * WINDOW, 4 * WINDOW), BF16),
                        pltpu.VMEM((bps, N_KV, 4 * WINDOW, LANES), BF16),
                        pltpu.VMEM((bps, N_KV, 2 * WINDOW, LANES), BF16),
                        pltpu.VMEM((bps, N_KV, 2 * WINDOW, LANES), BF16)],
    )(sinks, q, k, v, qg, kg, cosf, sins, ind_q, ind_qt, ind_k, ind_kt)
    return outs[0], extra


def _attn_bwd(do, q, k, v, qg, kg, sinks, cosf, sins, ind_q, ind_qt, ind_k, ind_kt, n_seq, S, phases=()):
    T = q.shape[0]
    nblk = S // WINDOW
    W = D_MODEL

    def body(sink_ref, do_ref, q_ref, k_ref, v_ref, qg_ref, kg_ref, cos_ref, sin_ref, iq_ref, iqt_ref, ik_ref,
             ikt_ref, dq_ref, dkc_ref, dkp_ref, dvc_ref, dvp_ref, dqg_ref, dsk_ref, kc_ref, vc_ref, dqr_ref,
             dk_ref, dv_ref, s_ref, dp_ref, p_ref, ds_ref, qs_ref, dos_ref, kd_ref, vd_ref):
        s_id, n_step = pl.program_id(0), pl.program_id(1)

        @pl.when((s_id == 0) & (n_step == 0))
        def _():
            dqg_ref[...] = jnp.zeros_like(dqg_ref)
            dsk_ref[...] = jnp.zeros_like(dsk_ref)

        @pl.when(n_step == 0)
        def _():
            kc_ref[...] = jnp.zeros_like(kc_ref)
            vc_ref[...] = jnp.zeros_like(vc_ref)

        lane = lax.broadcasted_iota(jnp.int32, (WINDOW, W), 1)
        lane_k = lane[:, :KV_W]
        lo = lane[:, :LANES] < HEAD_DIM
        lo2 = lax.broadcasted_iota(jnp.int32, (2 * WINDOW, LANES), 1) < HEAD_DIM
        scale = HEAD_DIM ** -0.5

        def one_block(h):
            n = n_step * bps + h
            rows = slice(h * WINDOW, (h + 1) * WINDOW)
            cosf, sinv = jnp.tile(cos_ref[rows, :], (1, W // LANES)), jnp.tile(sin_ref[rows, :], (1, W // LANES))
            qv = q_ref[rows, :]
            qr, q_rstd = _qk_prep(qv, qg_ref[...], cosf, sinv, iq_ref[...], iqt_ref[...], lane)
            kr, _ = _qk_prep(k_ref[rows, :], kg_ref[...], cosf[:, :KV_W], sinv[:, :KV_W], ik_ref[...], ikt_ref[...],
                             lane_k)
            kc_ref[WINDOW:2 * WINDOW, :] = kr
            vc_ref[WINDOW:2 * WINDOW, :] = v_ref[rows, :]
            kc, vc = kc_ref[...], vc_ref[...]
            dov = do_ref[rows, :]
            mask = jnp.tile(_attn_mask_t(n), (1, 4))
            qr = qr * scale
            dk_ref[h] = jnp.zeros((2 * WINDOW, KV_W), F32)
            dv_ref[h] = jnp.zeros((2 * WINDOW, KV_W), F32)
            for kvh in range(N_KV):
                qs_ref[h, kvh] = _stack_heads(qr, kvh, lo)
                dos_ref[h, kvh] = _stack_heads(dov, kvh, lo)
                kd_ref[h, kvh] = _dup_head(kc, kvh, lo2)
                vd_ref[h, kvh] = _dup_head(vc, kvh, lo2)
            sb, dpb, pb, dsb = s_ref.at[h], dp_ref.at[h], p_ref.at[h], ds_ref.at[h]

            def scores(kvh):
                b = kvh % 2
                sb[b] = jnp.where(mask, _dot_nt(kd_ref[h, kvh], qs_ref[h, kvh]), -1e30)
                dpb[b] = _dot_nt(vd_ref[h, kvh], dos_ref[h, kvh])

            def softmax(kvh):
                b = kvh % 2
                for r in range(4):
                    cols = slice(r * WINDOW, (r + 1) * WINDOW)
                    head = 4 * kvh + r
                    mx, inv, ps = _softmax_stats(sb, b, cols, sink_ref[head])
                    g8 = None
                    for c in KEY_CHUNKS:
                        t = _fold8(jnp.exp(sb[b, c, cols] - mx) * dpb[b, c, cols], jnp.sum)
                        g8 = t if g8 is None else g8 + t
                    dd = jnp.sum(g8, axis=0, keepdims=True) * inv
                    for c in KEY_CHUNKS:
                        p = jnp.exp(sb[b, c, cols] - mx) * inv
                        pb[b, c, cols] = p.astype(BF16)
                        dsb[b, c, cols] = (p * (dpb[b, c, cols] - dd)).astype(BF16)
                    dsk_ref[head:head + 1, :] -= ps * dd

            def grads(kvh):
                m, b = kvh // 2, kvh % 2
                dq0, dq1 = _unstack_heads(_dot_tn(dsb[b], kd_ref[h, kvh]) * scale, lo)
                dqr_ref[h, :, (2 * kvh) * LANES:(2 * kvh + 1) * LANES] = dq0
                dqr_ref[h, :, (2 * kvh + 1) * LANES:(2 * kvh + 2) * LANES] = dq1
                dk_ref[h, :, m * LANES:(m + 1) * LANES] += _fold_head(_dot(dsb[b], qs_ref[h, kvh]), kvh, lo2)
                dv_ref[h, :, m * LANES:(m + 1) * LANES] += _fold_head(_dot(pb[b], dos_ref[h, kvh]), kvh, lo2)

            scores(0)
            for kvh in range(N_KV):
                if kvh + 1 < N_KV:
                    scores(kvh + 1)
                softmax(kvh)
                grads(kvh)
            dq, dqg = _qk_prep_bwd(dqr_ref[h], qv, q_rstd, qg_ref[...], cosf, sinv, iq_ref[...], iqt_ref[...], lane)
            dq_ref[rows, :] = dq.astype(BF16)
            dqg_ref[...] += dqg
            dkp_ref[rows, :] = dk_ref[h, 0:WINDOW, :]
            dkc_ref[rows, :] = dk_ref[h, WINDOW:2 * WINDOW, :]
            dvp_ref[rows, :] = dv_ref[h, 0:WINDOW, :]
            dvc_ref[rows, :] = dv_ref[h, WINDOW:2 * WINDOW, :]
            kc_ref[0:WINDOW, :] = kr
            vc_ref[0:WINDOW, :] = v_ref[rows, :]

        for h in range(bps):
            one_block(h)

    bps = ATTN_BLOCKS_PER_STEP
    rows_step = bps * WINDOW
    blk = lambda w: pl.BlockSpec((rows_step, w), lambda s, n: (s * (nblk // bps) + n, 0))
    pos = pl.BlockSpec((rows_step, LANES), lambda s, n: (n, 0))
    kv_out = jax.ShapeDtypeStruct((T, KV_W), F32)
    stage = lambda dt: pltpu.VMEM((bps, 2, 2 * WINDOW, 4 * WINDOW), dt)
    return _call(
        body, phases=phases, name="attn_bwd", grid=(n_seq, nblk // bps),
        in_specs=[pl.BlockSpec(memory_space=pltpu.SMEM), blk(W), blk(W), blk(KV_W), blk(KV_W), _const((1, W)),
                  _const((1, KV_W)), pos, pos, _const((W, LANES)), _const((LANES, W)), _const((KV_W, LANES)),
                  _const((LANES, KV_W))],
        out_specs=[blk(W), blk(KV_W), blk(KV_W), blk(KV_W), blk(KV_W), _const((1, W)), _const((N_HEADS, LANES))],
        out_shape=[jax.ShapeDtypeStruct((T, W), BF16), kv_out, kv_out, kv_out, kv_out,
                   jax.ShapeDtypeStruct((1, W), F32), jax.ShapeDtypeStruct((N_HEADS, LANES), F32)],
        scratch_shapes=[pltpu.VMEM((2 * WINDOW, KV_W), F32), pltpu.VMEM((2 * WINDOW, KV_W), F32),
                        pltpu.VMEM((bps, WINDOW, W), F32), pltpu.VMEM((bps, 2 * WINDOW, KV_W), F32),
                        pltpu.VMEM((bps, 2 * WINDOW, KV_W), F32), stage(F32), stage(F32), stage(BF16), stage(BF16),
                        pltpu.VMEM((bps, N_KV, 4 * WINDOW, LANES), BF16),
                        pltpu.VMEM((bps, N_KV, 4 * WINDOW, LANES), BF16),
                        pltpu.VMEM((bps, N_KV, 2 * WINDOW, LANES), BF16),
                        pltpu.VMEM((bps, N_KV, 2 * WINDOW, LANES), BF16)],
    )(sinks, do, q, k, v, qg, kg, cosf, sins, ind_q, ind_qt, ind_k, ind_kt)


def _kv_bwd(dkc, dkp, dvc, dvp, k, kg, cosf, sins, ind_k, ind_kt, n_seq, S, phases=()):
    T = k.shape[0]
    nblk = S // WINDOW
    nb = min(KV_BLOCKS_PER_STEP, nblk)
    rows, nt = nb * WINDOW, nblk // nb

    def body(dkc_ref, dkp_ref, dkn_ref, dvc_ref, dvp_ref, dvn_ref, k_ref, kg_ref, cos_ref, sin_ref, ik_ref, ikt_ref,
             dk_ref, dv_ref, dkg_ref):
        s_id, n = pl.program_id(0), pl.program_id(1)

        @pl.when((s_id == 0) & (n == 0))
        def _():
            dkg_ref[...] = jnp.zeros_like(dkg_ref)

        blk = n * nb + lax.broadcasted_iota(jnp.int32, (rows, KV_W), 0) // WINDOW
        has_next = blk < nblk - 1

        def from_next(part_ref, next_ref):
            moved = jnp.concatenate([part_ref[WINDOW:rows, :], next_ref[...]], axis=0) if nb > 1 else next_ref[...]
            return jnp.where(has_next, moved, 0.0)

        lane = lax.broadcasted_iota(jnp.int32, (rows, KV_W), 1)
        dkr = dkc_ref[...] + from_next(dkp_ref, dkn_ref)
        dv_ref[...] = (dvc_ref[...] + from_next(dvp_ref, dvn_ref)).astype(BF16)
        cosf, sinv = jnp.tile(cos_ref[...], (1, KV_W // LANES)), jnp.tile(sin_ref[...], (1, KV_W // LANES))
        kv = k_ref[...]
        _, rstd = _qk_prep(kv, kg_ref[...], cosf, sinv, ik_ref[...], ikt_ref[...], lane)
        dk, dkg = _qk_prep_bwd(dkr, kv, rstd, kg_ref[...], cosf, sinv, ik_ref[...], ikt_ref[...], lane)
        dk_ref[...] = dk.astype(BF16)
        dkg_ref[...] += dkg

    cur = pl.BlockSpec((rows, KV_W), lambda s, n: (s * nt + n, 0))
    nxt = pl.BlockSpec((WINDOW, KV_W), lambda s, n: (s * nblk + jnp.minimum((n + 1) * nb, nblk - 1), 0))
    pos = pl.BlockSpec((rows, LANES), lambda s, n: (n, 0))
    return _call(
        body, phases=phases, name="kv_bwd", grid=(n_seq, nt),
        in_specs=[cur, cur, nxt, cur, cur, nxt, cur, _const((1, KV_W)), pos, pos, _const((KV_W, LANES)),
                  _const((LANES, KV_W))],
        out_specs=[cur, cur, _const((1, KV_W))],
        out_shape=[jax.ShapeDtypeStruct((T, KV_W), BF16), jax.ShapeDtypeStruct((T, KV_W), BF16),
                   jax.ShapeDtypeStruct((1, KV_W), F32)],
    )(dkc, dkp, dkp, dvc, dvp, dvp, k, kg, cosf, sins, ind_k, ind_kt)


def _merge_fwd(x, ya, o, ga, gb, w_rnn, w_attn, w_out, tm, phases=()):
    T = x.shape[0]
    W = D_MODEL

    def body(x_ref, ya_ref, o_ref, ga_ref, gb_ref, wr_ref, wa_ref, wo_ref, x1_ref, mg_ref):
        y_a = _dot(ya_ref[...], wr_ref[...])
        y_b = _dot(o_ref[...], wa_ref[...])
        mg = (_sigmoid(ga_ref[...]) * y_a + _sigmoid(gb_ref[...]) * y_b).astype(BF16)
        mg_ref[...] = mg
        x1_ref[...] = x_ref[...] + _dot(mg, wo_ref[...])

    row = pl.BlockSpec((tm, W), lambda i: (i, 0))
    sq = _const((W, W))
    return _call(
        body, phases=phases, name="merge_fwd", grid=(T // tm,),
        in_specs=[row, row, row, row, row, sq, sq, sq], out_specs=[row, row],
        out_shape=[jax.ShapeDtypeStruct((T, W), F32), jax.ShapeDtypeStruct((T, W), BF16)],
    )(x, ya, o, ga, gb, w_rnn, w_attn, w_out)


def _merge_bwd(dx1, ga, gb, ya, o, w_rnn, w_attn, w_out, tm, phases=()):
    T = dx1.shape[0]
    W = D_MODEL

    def body(dx1_ref, ga_ref, gb_ref, ya_ref, o_ref, wr_ref, wa_ref, wo_ref,
             dga_ref, dgb_ref, dya_ref, dyb_ref, dyain_ref, do_ref):
        dm = _dot_nt(dx1_ref[...].astype(BF16), wo_ref[...])
        sa = _sigmoid(ga_ref[...])
        sb = _sigmoid(gb_ref[...])
        dga_ref[...] = (dm * _dot(ya_ref[...], wr_ref[...]) * (sa * (1.0 - sa))).astype(BF16)
        dgb_ref[...] = (dm * _dot(o_ref[...], wa_ref[...]) * (sb * (1.0 - sb))).astype(BF16)
        dya = (dm * sa).astype(BF16)
        dyb = (dm * sb).astype(BF16)
        dya_ref[...] = dya
        dyb_ref[...] = dyb
        dyain_ref[...] = _dot_nt(dya, wr_ref[...])
        do_ref[...] = _dot_nt(dyb, wa_ref[...])

    row = pl.BlockSpec((tm, W), lambda i: (i, 0))
    sq = _const((W, W))
    b16 = jax.ShapeDtypeStruct((T, W), BF16)
    f32 = jax.ShapeDtypeStruct((T, W), F32)
    return _call(
        body, phases=phases, name="merge_bwd", grid=(T // tm,),
        in_specs=[row, row, row, row, row, sq, sq, sq], out_specs=[row] * 6,
        out_shape=[b16, b16, b16, b16, f32, f32],
    )(dx1, ga, gb, ya, o, w_rnn, w_attn, w_out)


def _mlp_fwd(x1, g_mlp, w_up, w_down, tm, phases=()):
    T = x1.shape[0]
    W = D_MODEL

    def body(x_ref, g_ref, wu_ref, wd_ref, x2_ref, hm_ref, u_ref, act_ref):
        xv = x_ref[...]
        hm, _ = _rms_fwd(xv, g_ref[...])
        hmb = hm.astype(BF16)
        hm_ref[...] = hmb
        for j in range(N_CHIPS):
            u = _dot(hmb, wu_ref[j])
            u_ref[:, j * W:(j + 1) * W] = u
            ru = jnp.maximum(u, 0.0)
            act_ref[:, j * W:(j + 1) * W] = (ru * ru).astype(BF16)
        x2_ref[...] = xv + _dot(act_ref[...], wd_ref[...])

    row = lambda w: pl.BlockSpec((tm, w), lambda i: (i, 0))
    return _call(
        body, phases=phases, name="mlp_fwd", grid=(T // tm,),
        in_specs=[row(W), _const((1, W)), _const((N_CHIPS, W, W)), _const((D_FF, W))],
        out_specs=[row(W), row(W), row(D_FF), row(D_FF)],
        out_shape=[jax.ShapeDtypeStruct((T, W), F32), jax.ShapeDtypeStruct((T, W), BF16),
                   jax.ShapeDtypeStruct((T, D_FF), F32), jax.ShapeDtypeStruct((T, D_FF), BF16)],
    )(x1, g_mlp, w_up, w_down)


def _mlp_bwd(dx2, u, x1, g_mlp, w_up, w_down, tm, phases=()):
    T = x1.shape[0]
    W = D_MODEL

    def body(dx2_ref, u_ref, x_ref, g_ref, wu_ref, wd_ref, dx1_ref, du_ref, dg_ref):
        @pl.when(pl.program_id(0) == 0)
        def _():
            dg_ref[...] = jnp.zeros_like(dg_ref)

        dx2 = dx2_ref[...]
        dact = _dot_nt(dx2.astype(BF16), wd_ref[...])
        du_ref[...] = (dact * (2.0 * jnp.maximum(u_ref[...], 0.0))).astype(BF16)
        dhm = jnp.zeros((tm, W), F32)
        for j in range(N_CHIPS):
            dhm = dhm + _dot_nt(du_ref[:, j * W:(j + 1) * W], wu_ref[j])
        xv = x_ref[...]
        g = g_ref[...]
        _, r = _rms_fwd(xv, g)
        dx, dg = _rms_bwd(dhm, xv, r, g)
        dx1_ref[...] = dx2 + dx
        dg_ref[...] += dg

    row = lambda w: pl.BlockSpec((tm, w), lambda i: (i, 0))
    return _call(
        body, phases=phases, name="mlp_bwd", grid=(T // tm,),
        in_specs=[row(W), row(D_FF), row(W), _const((1, W)), _const((N_CHIPS, W, W)), _const((D_FF, W))],
        out_specs=[row(W), row(D_FF), _const((1, W))],
        out_shape=[jax.ShapeDtypeStruct((T, W), F32), jax.ShapeDtypeStruct((T, D_FF), BF16),
                   jax.ShapeDtypeStruct((1, W), F32)],
    )(dx2, u, x1, g_mlp, w_up, w_down)


def _ple_loss(x2, p, target, g_ple, w_gate, w_proj, tm, phases=()):
    T = x2.shape[0]
    W = D_MODEL
    cw = W // N_CHIPS

    def body(x_ref, p_ref, t_ref, g_ref, wg_ref, wp_ref, loss_ref, dx2_ref, pb_ref, de_ref, hp_ref, dtg_ref, dg_ref):
        @pl.when(pl.program_id(0) == 0)
        def _():
            dg_ref[...] = jnp.zeros_like(dg_ref)
            loss_ref[...] = jnp.zeros_like(loss_ref)

        xv = x_ref[...]
        g = g_ref[...]
        pb = p_ref[...].astype(BF16)
        pb_ref[...] = pb
        e = jnp.concatenate([_dot(pb, wp_ref[j]) for j in range(N_CHIPS)], axis=1)
        hp, r = _rms_fwd(xv, g)
        hpb = hp.astype(BF16)
        hp_ref[...] = hpb
        sg = _sigmoid(_dot(hpb, wg_ref[...]))
        diff = (xv + e * sg) - t_ref[...]
        loss_ref[...] += jnp.sum(diff * diff) * (0.5 / W)
        dx3 = diff * (1.0 / W)
        de_ref[...] = (dx3 * sg).astype(BF16)
        dtg = (dx3 * e * (sg * (1.0 - sg))).astype(BF16)
        dtg_ref[...] = dtg
        dx, dg = _rms_bwd(_dot_nt(dtg, wg_ref[...]), xv, r, g)
        dx2_ref[...] = dx3 + dx
        dg_ref[...] += dg

    row = lambda w: pl.BlockSpec((tm, w), lambda i: (i, 0))
    b16 = lambda w: jax.ShapeDtypeStruct((T, w), BF16)
    return _call(
        body, phases=phases, name="ple_loss", grid=(T // tm,),
        in_specs=[row(W), row(PLE_DIM), row(W), _const((1, W)), _const((W, W)), _const((N_CHIPS, PLE_DIM, cw))],
        out_specs=[_const((8, LANES)), row(W), row(PLE_DIM), row(W), row(W), row(W), _const((1, W))],
        out_shape=[jax.ShapeDtypeStruct((8, LANES), F32), jax.ShapeDtypeStruct((T, W), F32), b16(PLE_DIM),
                   b16(W), b16(W), b16(W), jax.ShapeDtypeStruct((1, W), F32)],
    )(x2, p, target, g_ple, w_gate, w_proj)


def _adamw(w, g, m, v, name, tr, phases=()):
    R, C = w.shape
    c1 = 1.0 / (1.0 - ADAM_B1 ** ADAM_STEP)
    c2 = 1.0 / (1.0 - ADAM_B2 ** ADAM_STEP)

    def body(w_ref, g_ref, m_ref, v_ref, go_ref, d_ref, nm_ref, nv_ref):
        gv = g_ref[...]
        go_ref[...] = gv
        nm = ADAM_B1 * m_ref[...] + (1.0 - ADAM_B1) * gv
        nv = ADAM_B2 * v_ref[...] + (1.0 - ADAM_B2) * (gv * gv)
        nm_ref[...] = nm
        nv_ref[...] = nv
        d_ref[...] = (-ADAM_LR) * ((nm * c1) / (jnp.sqrt(nv * c2) + ADAM_EPS) + ADAM_WD * w_ref[...])

    row = pl.BlockSpec((tr, C), lambda i: (i, 0))
    sds = jax.ShapeDtypeStruct((R, C), F32)
    return _call(
        body, phases=phases, name=name, grid=(R // tr,), in_specs=[row] * 4, out_specs=[row] * 4,
        out_shape=[sds] * 4,
    )(w, g, m, v)


def _indicator(width):
    ind = np.zeros((width, LANES), np.float32)
    ind[np.arange(width), np.arange(width) // HEAD_DIM] = 1.0
    return jnp.asarray(ind, BF16), jnp.asarray(ind.T, BF16)


def _rope_tables(S):
    inv = ROPE_THETA ** (-jnp.arange(0, HEAD_DIM, 2, dtype=F32) / HEAD_DIM)
    ang = jnp.arange(S, dtype=F32)[:, None] * inv[None, :]
    cos, sin = jnp.cos(ang), jnp.sin(ang)
    cosf = jnp.tile(jnp.concatenate([cos, cos], axis=1), (1, LANES // HEAD_DIM))
    sins = jnp.tile(jnp.concatenate([-sin, sin], axis=1), (1, LANES // HEAD_DIM))
    return cosf, sins


def _pair_blockdiag(w):
    w4 = w.reshape(8, 2, HEAD_DIM, HEAD_DIM)
    eye = jnp.eye(2, dtype=w.dtype)
    return jnp.einsum("bpij,pq->bpiqj", w4, eye).reshape(8, LANES, LANES)


def _pair_blockdiag_extract(g):
    g5 = g.reshape(8, 2, HEAD_DIM, 2, HEAD_DIM)
    return jnp.stack([g5[:, 0, :, 0, :], g5[:, 1, :, 1, :]], axis=1).reshape(16, HEAD_DIM, HEAD_DIM)


def _pair_sum(parts, sibs, name):
    n = len(parts)
    dims = [(p.shape[1] // 2, p.shape[2]) for p in parts]

    def body(*refs):
        p_r, s_r, send_r, own_r, mine_r, sem = (refs[0:n], refs[n:2 * n], refs[2 * n:3 * n], refs[3 * n:4 * n],
                                                refs[4 * n:5 * n], refs[5 * n])
        x, y, c, chips = _mesh_pos()
        me = 2 * x + y
        loads = []
        for i, (R, _) in enumerate(dims):
            mine, _ = _half_rows(c, R)
            cp = pltpu.make_async_copy(p_r[i].at[:, mine, :], mine_r[i], sem.at[i])
            cp.start()
            loads.append(cp)
        for i in range(n):
            loads[i].wait()
            for j, (cx, cy) in enumerate(chips):
                k = 2 * cx + cy
                send_r[i][j] = (mine_r[i][k] + s_r[i][k].astype(F32)).astype(BF16)
            own_r[i][...] = mine_r[i][me] + s_r[i][me].astype(F32)

    vm = pl.BlockSpec(memory_space=pltpu.VMEM)
    out = pl.pallas_call(
        body, name=name, in_specs=[pl.BlockSpec(memory_space=pl.ANY)] * n + [vm] * n, out_specs=[vm] * (2 * n),
        out_shape=[jax.ShapeDtypeStruct((3, R, C), BF16) for R, C in dims]
        + [jax.ShapeDtypeStruct((R, C), F32) for R, C in dims],
        scratch_shapes=[pltpu.VMEM((N_CHIPS, R, C), F32) for R, C in dims] + [pltpu.SemaphoreType.DMA((n,))],
        compiler_params=pltpu.CompilerParams(vmem_limit_bytes=VMEM_LIMIT),
    )(*parts, *sibs)
    return out[:n], out[n:]


def _chip_sum(owns, recvs, name):
    n = len(owns)
    dims = [o.shape for o in owns]

    def body(*refs):
        own_r, recv_r, red_r, stage_r, sem = refs[0:n], refs[n:2 * n], refs[2 * n:3 * n], refs[3 * n:4 * n], refs[4 * n]
        x, y, c, _ = _mesh_pos()
        me = 2 * x + y
        stores = []
        for i, (R, _) in enumerate(dims):
            for k_me in range(N_CHIPS):

                @pl.when(me == k_me)
                def _():
                    acc = None
                    for k in range(N_CHIPS):
                        slot = ((k // 2) ^ (k_me // 2)) + 2 * ((k % 2) ^ (k_me % 2)) - 1
                        term = own_r[i][...] if k == k_me else recv_r[i][slot].astype(F32)
                        acc = term if acc is None else acc + term
                    stage_r[i][...] = acc

            mine, _ = _half_rows(c, R)
            cp = pltpu.make_async_copy(stage_r[i], red_r[i].at[mine, :], sem.at[i])
            cp.start()
            stores.append(cp)
        for cp in stores:
            cp.wait()

    vm = pl.BlockSpec(memory_space=pltpu.VMEM)
    return pl.pallas_call(
        body, name=name, in_specs=[vm] * (2 * n), out_specs=[pl.BlockSpec(memory_space=pl.ANY)] * n,
        out_shape=[jax.ShapeDtypeStruct((2 * R, C), F32) for R, C in dims],
        scratch_shapes=[pltpu.VMEM((R, C), F32) for R, C in dims] + [pltpu.SemaphoreType.DMA((n,))],
        compiler_params=pltpu.CompilerParams(vmem_limit_bytes=VMEM_LIMIT),
    )(*owns, *recvs)


def _gather_bf16(shard, name):
    R2, C = shard.shape
    R = R2 // 2
    H = R // 2

    def body(s_ref, o_ref, send_sems, recv_sems):
        x, y, c, _ = _mesh_pos()
        me, chip_x, chip_y, chip_d = 2 * x + y, 2 * (1 - x) + y, 2 * x + (1 - y), 2 * (1 - x) + (1 - y)
        to_x, to_y, me_dev, sibling = (1 - x, y, c), (x, 1 - y, c), (x, y, c), (x, y, 1 - c)

        def rows(core, off, n):
            return pl.ds(pl.multiple_of(core * R + off, H), n)

        def copy(k, chip, rws, to):
            blk = o_ref.at[chip, rws]
            return _remote(blk, blk, (send_sems.at[k], recv_sems.at[k]), to)

        piece, half_a, half_b = rows(c, 0, R), rows(c, 0, H), rows(c, H, H)
        o_ref[me] = s_ref[...].astype(BF16)
        sends = [copy(0, me, piece, to_x), copy(1, me, piece, to_y)]
        for cp in sends:
            cp.start()
        arrivals = [(0, chip_x, piece, (2, half_a, to_y)), (1, chip_y, piece, (3, half_b, to_x)),
                    (2, chip_d, half_a, None), (3, chip_d, half_b, None)]
        for k, chip, rws, onward in arrivals:
            copy(k, chip, rws, me_dev).wait_recv()
            if onward is not None:
                sends.append(copy(onward[0], chip, onward[1], onward[2]))
                sends[-1].start()
            sends.append(copy(4 + k, chip, rws, sibling))
            sends[-1].start()
        for k, chip, rws in [(4, chip_x, rows(1 - c, 0, R)), (5, chip_y, rows(1 - c, 0, R)),
                             (6, chip_d, rows(1 - c, 0, H)), (7, chip_d, rows(1 - c, H, H))]:
            copy(k, chip, rws, me_dev).wait_recv()
        for cp in sends:
            cp.wait_send()

    return pl.pallas_call(
        body, name=name, out_shape=jax.ShapeDtypeStruct((N_CHIPS, R2, C), BF16),
        in_specs=[pl.BlockSpec(memory_space=pltpu.VMEM)], out_specs=pl.BlockSpec(memory_space=pltpu.VMEM),
        scratch_shapes=[pltpu.SemaphoreType.DMA((8,)), pltpu.SemaphoreType.DMA((8,))],
        compiler_params=pltpu.CompilerParams(vmem_limit_bytes=VMEM_LIMIT),
    )(shard)


def _pair_exchange_sum(partial, partial_b, name):
    _, R2, C = partial.shape
    R = R2 // 2

    def body(p_ref, pb_ref, send_ref, own_ref, mine_ref, sib_ref, loc_sems, send_sems, recv_sems):
        x, y, c, chips = _mesh_pos()
        me = 2 * x + y
        mine, theirs = _half_rows(c, R)
        order = [2 * cx + cy for cx, cy in chips] + [me]
        locs, pairs = [], []
        for i, k in enumerate(order):
            loc = pltpu.make_async_copy(p_ref.at[k, mine, :], mine_ref.at[i], loc_sems.at[i])
            pair = _remote(pb_ref.at[k, theirs, :], sib_ref.at[i], (send_sems.at[i], recv_sems.at[i]), (x, y, 1 - c))
            loc.start()
            pair.start()
            locs.append(loc)
            pairs.append(pair)
        for i in range(N_CHIPS):
            locs[i].wait()
            pairs[i].wait_recv()
            total = mine_ref[i] + sib_ref[i].astype(F32)
            if i < 3:
                send_ref[i] = total.astype(BF16)
            else:
                own_ref[...] = total
        for pair in pairs:
            pair.wait_send()

    vm = pl.BlockSpec(memory_space=pltpu.VMEM)
    return pl.pallas_call(
        body, name=name, in_specs=[pl.BlockSpec(memory_space=pl.ANY)] * 2, out_specs=[vm, vm],
        out_shape=[jax.ShapeDtypeStruct((3, R, C), BF16), jax.ShapeDtypeStruct((R, C), F32)],
        scratch_shapes=[pltpu.VMEM((N_CHIPS, R, C), F32), pltpu.VMEM((N_CHIPS, R, C), BF16),
                        pltpu.SemaphoreType.DMA((N_CHIPS,)), pltpu.SemaphoreType.DMA((N_CHIPS,)),
                        pltpu.SemaphoreType.DMA((N_CHIPS,))],
        compiler_params=pltpu.CompilerParams(vmem_limit_bytes=VMEM_LIMIT),
    )(partial, partial_b)


def _allreduce_small(buf, name):
    rows, width = buf.shape
    h = rows // 2

    def body(b_ref, o_ref, sib_ref, pair_ref, in_ref, pair_sems, send_sems, recv_sems, fin_sems):
        x, y, c, chips = _mesh_pos()
        me = 2 * x + y
        mine, theirs = _half_rows(c, h)
        sibling = (x, y, 1 - c)
        pair = _remote(b_ref.at[theirs], sib_ref, (pair_sems.at[0], pair_sems.at[1]), sibling)
        pair.start()
        pair.wait()
        pair_ref[...] = b_ref[mine, :] + sib_ref[...]
        sends = []
        for j, (cx, cy) in enumerate(chips):
            cp = _remote(pair_ref, in_ref.at[j], (send_sems.at[j], recv_sems.at[j]), (cx, cy, c))
            cp.start()
            sends.append(cp)
        for cp in sends:
            cp.wait_recv()
        acc = None
        for k in range(N_CHIPS):
            term = jnp.where(me == k, pair_ref[...], in_ref[_peer_slot(k, x, y)])
            acc = term if acc is None else acc + term
        o_ref[mine, :] = acc
        fin = _remote(o_ref.at[mine], o_ref.at[mine], (fin_sems.at[0], fin_sems.at[1]), sibling)
        fin.start()
        fin.wait_send()
        _remote(o_ref.at[theirs], o_ref.at[theirs], (fin_sems.at[0], fin_sems.at[1]), sibling).wait_recv()
        for cp in sends:
            cp.wait_send()

    return pl.pallas_call(
        body, name=name, out_shape=jax.ShapeDtypeStruct((rows, width), F32),
        in_specs=[pl.BlockSpec(memory_space=pltpu.VMEM)], out_specs=pl.BlockSpec(memory_space=pltpu.VMEM),
        scratch_shapes=[pltpu.VMEM((h, width), F32), pltpu.VMEM((h, width), F32), pltpu.VMEM((3, h, width), F32),
                        pltpu.SemaphoreType.DMA((2,)), pltpu.SemaphoreType.DMA((3,)), pltpu.SemaphoreType.DMA((3,)),
                        pltpu.SemaphoreType.DMA((2,))],
        compiler_params=pltpu.CompilerParams(vmem_limit_bytes=VMEM_LIMIT),
    )(buf)


def _adamw_small(ws, gs, ms, vs):
    n = len(ws)
    c1 = 1.0 / (1.0 - ADAM_B1 ** ADAM_STEP)
    c2 = 1.0 / (1.0 - ADAM_B2 ** ADAM_STEP)

    def body(*refs):
        w_r, g_r, m_r, v_r = refs[0:n], refs[n:2 * n], refs[2 * n:3 * n], refs[3 * n:4 * n]
        d_r, nm_r, nv_r = refs[4 * n:5 * n], refs[5 * n:6 * n], refs[6 * n:7 * n]
        for i in range(n):
            gv = g_r[i][...]
            nm = ADAM_B1 * m_r[i][...] + (1.0 - ADAM_B1) * gv
            nv = ADAM_B2 * v_r[i][...] + (1.0 - ADAM_B2) * (gv * gv)
            nm_r[i][...] = nm
            nv_r[i][...] = nv
            d_r[i][...] = (-ADAM_LR) * ((nm * c1) / (jnp.sqrt(nv * c2) + ADAM_EPS) + ADAM_WD * w_r[i][...])

    vm = pl.BlockSpec(memory_space=pltpu.VMEM)
    sds = [jax.ShapeDtypeStruct(w.shape, F32) for w in ws]
    out = pl.pallas_call(body, name="adamw_small", in_specs=[vm] * (4 * n), out_specs=[vm] * (3 * n),
                         out_shape=sds * 3)(*ws, *gs, *ms, *vs)
    return out[0:n], out[n:2 * n], out[2 * n:3 * n]


_BIG = ("w_in", "w_rnn_proj", "w_attn_proj", "w_out", "w_up", "w_down", "w_ple_gate", "w_ple_proj")
_SMALL = ("g_mix", "conv_w", "conv_b", "w_rg", "b_rg", "w_ig", "b_ig", "lru_lambda", "q_gain", "k_gain", "sinks",
          "g_mlp", "g_ple")
_WEIGHTS = ("g_mix", "w_in", "conv_w", "conv_b", "w_rg", "b_rg", "w_ig", "b_ig", "lru_lambda", "w_rnn_proj",
            "q_gain", "k_gain", "sinks", "w_attn_proj", "w_out", "g_mlp", "w_up", "w_down", "g_ple", "w_ple_gate",
            "w_ple_proj")


def _pad_row(v):
    v = v.reshape(1, -1)
    return jnp.pad(v, ((0, 0), (0, D_MODEL - v.shape[1])))


def kernel(x, p, g_mix, w_in, conv_w, conv_b, w_rg, b_rg, w_ig, b_ig, lru_lambda, w_rnn_proj, q_gain, k_gain, sinks, w_attn_proj, w_out, g_mlp, w_up, w_down, g_ple, w_ple_gate, w_ple_proj, loss_target, m_g_mix, m_w_in, m_conv_w, m_conv_b, m_w_rg, m_b_rg, m_w_ig, m_b_ig, m_lru_lambda, m_w_rnn_proj, m_q_gain, m_k_gain, m_sinks, m_w_attn_proj, m_w_out, m_g_mlp, m_w_up, m_w_down, m_g_ple, m_w_ple_gate, m_w_ple_proj, v_g_mix, v_w_in, v_conv_w, v_conv_b, v_w_rg, v_b_rg, v_w_ig, v_b_ig, v_lru_lambda, v_w_rnn_proj, v_q_gain, v_k_gain, v_sinks, v_w_attn_proj, v_w_out, v_g_mlp, v_w_up, v_w_down, v_g_ple, v_w_ple_gate, v_w_ple_proj):
    w = dict(g_mix=g_mix, w_in=w_in, conv_w=conv_w, conv_b=conv_b, w_rg=w_rg, b_rg=b_rg, w_ig=w_ig, b_ig=b_ig,
             lru_lambda=lru_lambda, w_rnn_proj=w_rnn_proj, q_gain=q_gain, k_gain=k_gain, sinks=sinks,
             w_attn_proj=w_attn_proj, w_out=w_out, g_mlp=g_mlp, w_up=w_up, w_down=w_down, g_ple=g_ple,
             w_ple_gate=w_ple_gate, w_ple_proj=w_ple_proj)
    m = dict(g_mix=m_g_mix, w_in=m_w_in, conv_w=m_conv_w, conv_b=m_conv_b, w_rg=m_w_rg, b_rg=m_b_rg, w_ig=m_w_ig,
             b_ig=m_b_ig, lru_lambda=m_lru_lambda, w_rnn_proj=m_w_rnn_proj, q_gain=m_q_gain, k_gain=m_k_gain,
             sinks=m_sinks, w_attn_proj=m_w_attn_proj, w_out=m_w_out, g_mlp=m_g_mlp, w_up=m_w_up, w_down=m_w_down,
             g_ple=m_g_ple, w_ple_gate=m_w_ple_gate, w_ple_proj=m_w_ple_proj)
    v = dict(g_mix=v_g_mix, w_in=v_w_in, conv_w=v_conv_w, conv_b=v_conv_b, w_rg=v_w_rg, b_rg=v_b_rg, w_ig=v_w_ig,
             b_ig=v_b_ig, lru_lambda=v_lru_lambda, w_rnn_proj=v_w_rnn_proj, q_gain=v_q_gain, k_gain=v_k_gain,
             sinks=v_sinks, w_attn_proj=v_w_attn_proj, w_out=v_w_out, g_mlp=v_g_mlp, w_up=v_w_up, w_down=v_w_down,
             g_ple=v_g_ple, w_ple_gate=v_w_ple_gate, w_ple_proj=v_w_ple_proj)
    n_seq, S, _ = x.shape
    T = n_seq * S
    chip = 2 * lax.axis_index("x") + lax.axis_index("y")

    tm, tm_rnn = TM, TM_RNN
    xf, pf, tf = x.reshape(T, D_MODEL), p.reshape(T, PLE_DIM), loss_target.reshape(T, D_MODEL)
    first = lambda outs: [o[0] for o in outs]

    w_in_g = _gather_bf16(w["w_in"][0], "gather_w_in")
    wb = {name: w[name][0].astype(BF16) for name in _BIG if name != "w_in"}
    grp_mix, grp_mlp, grp_ple = ("w_rnn_proj", "w_attn_proj", "w_out"), ("w_up", "w_down"), ("w_ple_gate", "w_ple_proj")

    wb["conv_w"] = jnp.pad(conv_w[0], ((0, 16 - CONV_W), (0, 0)))

    cosf, sins = _rope_tables(S)
    ind_q, ind_qt = _indicator(D_MODEL)
    ind_k, ind_kt = _indicator(KV_W)
    wrg2 = _pair_blockdiag(w_rg[0]).astype(BF16)
    wig2 = _pair_blockdiag(w_ig[0]).astype(BF16)
    qg = jnp.tile(q_gain, (1, N_HEADS))
    kg = jnp.tile(k_gain, (1, N_KV))
    sk = sinks.reshape(N_HEADS)
    attn_c = (qg, kg, sk, cosf, sins, ind_q, ind_qt, ind_k, ind_kt, n_seq, S)

    (h0, xr, gr, zq, zk, zv, ga, gb), ph = _inproj_fwd(xf, g_mix, w_in_g, tm,
                                                     phases=[_ph_gather_send(wb[n]) for n in grp_mix + ("conv_w",)])
    g_small = first(ph)
    o, ph = _attn_fwd(zq, zk, zv, *attn_c,
                      phases=[_ph_gather_pass(g) for g in g_small]
                      + [_ph_gather_send(wb[n]) for n in ("w_up",) + grp_ple])
    g_small, (wu, wpg, wpp) = first(ph[:4]), first(ph[4:])
    cw_full = g_small[3][:, :CONV_W, :].transpose(1, 0, 2).reshape(CONV_W, D_MODEL)
    rnn_w = (cw_full, conv_b, wrg2, b_rg, wig2, b_ig, lru_lambda)
    (xc, h, *gates, ya), ph = _rnn_fwd(xr, gr, *rnn_w, n_seq, S, tm_rnn,
                               phases=[_ph_gather_pass(g) for g in (wu, wpg, wpp)]
                               + [_ph_gather_send(wb["w_down"])])
    (wu, wpg, wpp), wd = first(ph[:3]), ph[3][0]
    wr, wa, wo = (g.reshape(D_MODEL, D_MODEL) for g in g_small[:3])
    wpg = wpg.reshape(D_MODEL, D_MODEL)
    (x1, merged), ph = _merge_fwd(xf, ya, o, ga, gb, wr, wa, wo, tm, phases=[_ph_gather_pass(wd)])
    wd = ph[0][0].reshape(D_FF, D_MODEL)
    (x2, hm, u, act), _ = _mlp_fwd(x1, g_mlp, wu, wd, tm // 2)
    (loss_t, dx2, pb, de, hp, dtg, dg_ple), _ = _ple_loss(x2, pf, tf, g_ple, wpg, wpp, tm)

    chipmajor = lambda g: g.reshape(N_CHIPS, g.shape[-2] // N_CHIPS, g.shape[-1]) if g.ndim == 2 else g
    tmw = min(2 * tm, T)
    dw_pp = _wgrad(pb, de, "wgrad_ple_proj", False, D_MODEL, tmw)[0]
    part_ple = [chipmajor(_wgrad(hp, dtg, "wgrad_ple_gate", False, D_MODEL, tmw)[0]),
                dw_pp.reshape(PLE_DIM, N_CHIPS, D_MODEL // N_CHIPS).transpose(1, 0, 2)]
    (dx1, du, dg_mlp), ph = _mlp_bwd(dx2, u, x1, g_mlp, wu, wd, tm // 2, phases=[_ph_pair_send(g) for g in part_ple])
    send_ple, own_ple = _pair_sum(part_ple, first(ph), "pair_sum_ple")
    dw_down, dw_down_b, ph = _wgrad(act, dx2, "wgrad_down", False, D_MODEL // 2, tmw, narrow=True,
                                    phases=[_ph_chip_send(s) for s in send_ple])
    red_ple = _chip_sum(own_ple, first(ph), "chip_sum_ple")
    dw_up, dw_up_b, _ = _wgrad(hm, du, "wgrad_up", True, D_MODEL, tmw, narrow=True)
    part_mlp, part_mlp_b = [dw_up, chipmajor(dw_down)], [dw_up_b, chipmajor(dw_down_b)]
    (dga, dgb, dya, dyb, dyain, do), _ = _merge_bwd(dx1, ga, gb, ya, o, wr, wa, wo, tm)
    dw_rnn, ph_up = _wgrad(ya, dya, "wgrad_rnn_proj", False, D_MODEL, tmw, phases=[_ph_pair_send(part_mlp_b[0])])
    dw_attn, ph_down = _wgrad(o, dyb, "wgrad_attn_proj", False, D_MODEL, tmw, phases=[_ph_pair_send(part_mlp_b[1])])
    dw_out, ph = _wgrad(merged, dx1, "wgrad_out", False, D_MODEL, tmw, phases=[_ph_half_swap(r) for r in red_ple])
    red_ple = first(ph)
    send_mlp, own_mlp = _pair_sum(part_mlp, [ph_up[0][0], ph_down[0][0]], "pair_sum_mlp")
    part_mix = [chipmajor(dw_rnn), chipmajor(dw_attn), chipmajor(dw_out)]
    (dxr, dgr, vec, dwrg2, dwig2), ph = _rnn_bwd(
        dyain, xr, gr, xc, h, gates, cw_full, wrg2, wig2, lru_lambda, n_seq, S, tm_rnn,
        phases=[_ph_chip_send(s) for s in send_mlp] + [_ph_pair_send(g) for g in part_mix])
    red_mlp = _chip_sum(own_mlp, first(ph[:2]), "chip_sum_mlp")
    send_mix, own_mix = _pair_sum(part_mix, first(ph[2:]), "pair_sum_mix")
    (dq, dkc, dkp, dvc, dvp, dqg, dsk), ph = _attn_bwd(
        do, zq, zk, zv, *attn_c, phases=[_ph_half_swap(r) for r in red_mlp] + [_ph_chip_send(s) for s in send_mix])
    red_mlp = first(ph[:2])
    red_mix = _chip_sum(own_mix, first(ph[2:]), "chip_sum_mix")
    (dk, dv, dkg), _ = _kv_bwd(dkc, dkp, dvc, dvp, zk, kg, cosf, sins, ind_k, ind_kt, n_seq, S)
    dz_parts = [dxr, dgr, dq, dk, dv, dga, dgb]
    send_in, own_in = _pair_exchange_sum(*_wgrad_in(h0, dz_parts, tm), "pair_sum_in")
    (grad_x, dg_mix), ph = _inproj_bwd(dz_parts, w_in_g, xf, g_mix, dx1, tm,
                                       phases=[_ph_half_swap(r) for r in red_mix] + [_ph_chip_send(send_in)])
    red_mix = first(ph[:3])
    red_in = _chip_sum([own_in], first(ph[3:]), "chip_sum_in")
    reduced = dict(zip(grp_ple + grp_mlp + grp_mix, red_ple + red_mlp + red_mix))
    grads = {
        "g_mix": dg_mix[0], "g_mlp": dg_mlp[0], "g_ple": dg_ple[0],
        "conv_w": vec[0:CONV_W], "conv_b": vec[4], "b_rg": vec[5], "b_ig": vec[6], "lru_lambda": vec[7],
        "w_rg": _pair_blockdiag_extract(dwrg2), "w_ig": _pair_blockdiag_extract(dwig2),
        "q_gain": dqg.reshape(N_HEADS, HEAD_DIM).sum(0), "k_gain": dkg.reshape(N_KV, HEAD_DIM).sum(0),
        "sinks": dsk.sum(1),
    }

    rows = [grads["conv_w"], _pad_row(grads["conv_b"]), _pad_row(grads["b_rg"]), _pad_row(grads["b_ig"]),
            _pad_row(grads["lru_lambda"]), _pad_row(grads["g_mix"]), _pad_row(grads["g_mlp"]),
            _pad_row(grads["g_ple"]), _pad_row(grads["q_gain"]), _pad_row(grads["k_gain"]), _pad_row(grads["sinks"]),
            _pad_row(loss_t[0:1, 0:1]), jnp.zeros((1, D_MODEL), F32)]
    vecs = jnp.concatenate(rows, axis=0)
    packed = jnp.concatenate([vecs.reshape(-1, LANES), grads["w_rg"].reshape(-1, LANES),
                              grads["w_ig"].reshape(-1, LANES)], axis=0)
    red = _allreduce_small(packed, "allreduce_small")
    nv = vecs.size // LANES
    rvec = red[0:nv].reshape(16, D_MODEL)
    loss = rvec[14, 0]
    nw = grads["w_rg"].size // LANES
    sg = {
        "conv_w": lax.dynamic_slice(rvec[0:CONV_W], (0, chip * (D_MODEL // N_CHIPS)), (CONV_W, D_MODEL // N_CHIPS)),
        "conv_b": rvec[4], "b_rg": rvec[5], "b_ig": rvec[6], "lru_lambda": rvec[7], "g_mix": rvec[8],
        "g_mlp": rvec[9], "g_ple": rvec[10], "q_gain": rvec[11, :HEAD_DIM], "k_gain": rvec[12, :HEAD_DIM],
        "sinks": rvec[13, :N_HEADS], "w_rg": red[nv:nv + nw], "w_ig": red[nv + nw:nv + 2 * nw],
    }
    sg = {k: sg[k].reshape(w[k].shape) for k in _SMALL}
    d_s, m_s, v_s = _adamw_small([w[k] for k in _SMALL], [sg[k] for k in _SMALL], [m[k] for k in _SMALL],
                                 [v[k] for k in _SMALL])
    grad, delta, new_m, new_v = dict(sg), dict(zip(_SMALL, d_s)), dict(zip(_SMALL, m_s)), dict(zip(_SMALL, v_s))

    for name in ("w_ple_proj", "w_up", "w_down", "w_rnn_proj", "w_attn_proj", "w_out", "w_ple_gate", "w_in"):
        shape = w[name].shape
        outs, ph = _adamw(w[name][0], reduced[name], m[name][0], v[name][0], "adamw_" + name, min(ADAMW_ROWS, shape[1] // 2),
                          phases=[_ph_half_swap(r) for r in red_in] if name == "w_ple_proj" else ())
        if name == "w_ple_proj":
            reduced["w_in"] = ph[0][0]
        grad[name], delta[name], new_m[name], new_v[name] = (a.reshape(shape) for a in outs)

    return (loss, grad_x.reshape(x.shape), *[grad[k] for k in _WEIGHTS], *[delta[k] for k in _WEIGHTS],
            *[new_m[k] for k in _WEIGHTS], *[new_v[k] for k in _WEIGHTS])
```
